```python
import math
import jax, jax.numpy as jnp
from jax import lax
import numpy as np

D_MODEL = 1024
BATCH = 16
SEQ = 2048
DEPTH = 4

SSD_HEADS = 8
SSD_HEAD_DIM = 64
SSD_GROUPS = 2
SSD_STATE = 128
SSD_CONV = 5
SSD_CHUNK = 128
SSD_INNER = SSD_HEADS * SSD_HEAD_DIM
SSD_XBC = SSD_INNER + 2 * SSD_GROUPS * SSD_STATE
GMLP_GROUPS = 4
GMLP_GROUP_DIM = 64
GMLP_WIDTH = GMLP_GROUPS * GMLP_GROUP_DIM
GMLP_CHUNK = 128
ATTN_HEADS = 4
ATTN_HEAD_DIM = 64
ATTN_WIDTH = ATTN_HEADS * ATTN_HEAD_DIM
ATTN_WINDOWS = (128, 512, 2048)
ATTN_DILATIONS = (1, 4, 16)
REL_BUCKETS = 32
REL_MAX_DISTANCE = 1024
MIX_WIDTH = SSD_INNER + GMLP_WIDTH + ATTN_WIDTH
IN_WIDTHS = (SSD_XBC, SSD_INNER, 2 * SSD_HEADS, GMLP_WIDTH, GMLP_WIDTH, ATTN_WIDTH, ATTN_WIDTH, ATTN_WIDTH)
IN_WIDTH = SSD_XBC + SSD_INNER + 2 * SSD_HEADS + 2 * GMLP_WIDTH + 3 * ATTN_WIDTH
FFN_DIM = 2816
FFN_CONV = 3
NORM_EPS = 1e-6
NEG_INF = -1e30

kernel_name = 'hybrid_ssd_gmlp_dilated_attn_encoder'


def rmsnorm(x, g):
    xf = x.astype(jnp.float32)
    y = xf * lax.rsqrt(jnp.mean(xf * xf, axis=-1, keepdims=True) + NORM_EPS)
    return (y * g.astype(jnp.float32)).astype(x.dtype)


def dwconv_centred(x, w, b):
    k, c = w.shape
    y = lax.conv_general_dilated(x, w[:, None, :].astype(x.dtype), window_strides=(1,),
                                 padding=[(k // 2, k // 2)],
                                 dimension_numbers=('NWC', 'WIO', 'NWC'),
                                 feature_group_count=c)
    return y + b


def ssd_chunked(x, dt, a_neg, bm, cm):
    b, s, g, hg, p = x.shape
    n = bm.shape[-1]
    q = SSD_CHUNK
    nc = s // q
    xdt = (x.astype(jnp.float32) * dt[..., None]).reshape(b, nc, q, g, hg, p)
    acs = jnp.cumsum((dt * a_neg).reshape(b, nc, q, g, hg), axis=2)
    bc = bm.astype(jnp.float32).reshape(b, nc, q, g, n)
    cc = cm.astype(jnp.float32).reshape(b, nc, q, g, n)
    tri = jnp.tril(jnp.ones((q, q), dtype=bool))
    seg = acs[:, :, :, None] - acs[:, :, None, :]
    decay_in = jnp.exp(jnp.where(tri[:, :, None, None], seg, NEG_INF))
    cb = jnp.einsum('bctgn,bcsgn->bctsg', cc, bc)
    y_diag = jnp.einsum('bctsg,bctsgh,bcsghp->bctghp', cb, decay_in, xdt)
    decay_out = jnp.exp(acs[:, :, -1:] - acs)
    states = jnp.einsum('bcsgn,bcsgh,bcsghp->bcghpn', bc, decay_out, xdt)
    chunk_decay = jnp.exp(acs[:, :, -1])

    def step(h, inp):
        st, dec = inp
        return h * dec[..., None, None] + st, h

    h0 = jnp.zeros((b, g, hg, p, n), jnp.float32)
    _, prev = lax.scan(step, h0, (jnp.moveaxis(states, 1, 0), jnp.moveaxis(chunk_decay, 1, 0)))
    prev = jnp.moveaxis(prev, 0, 1)
    y_off = jnp.einsum('bctgn,bcghpn,bctgh->bctghp', cc, prev, jnp.exp(acs))
    return (y_diag + y_off).reshape(b, s, g, hg, p)


def ssd_mixer(xbc, z, dt_raw, conv_w, conv_b, dt_bias, a_log, d_skip, out_gain):
    b, s, _ = xbc.shape
    hg = SSD_HEADS // SSD_GROUPS
    xbc_c = jax.nn.silu(dwconv_centred(xbc, conv_w, conv_b))
    xs, bm, cm = jnp.split(xbc_c, [SSD_INNER, SSD_INNER + SSD_GROUPS * SSD_STATE], axis=-1)
    xs = xs.reshape(b, s, SSD_GROUPS, hg, SSD_HEAD_DIM)
    bm = bm.reshape(b, s, SSD_GROUPS, SSD_STATE)
    cm = cm.reshape(b, s, SSD_GROUPS, SSD_STATE)
    dt = jax.nn.softplus(dt_raw.astype(jnp.float32).reshape(b, s, 2, SSD_HEADS)
                         + dt_bias.astype(jnp.float32)).reshape(b, s, 2, SSD_GROUPS, hg)
    a_neg = -jnp.exp(a_log.astype(jnp.float32)).reshape(2, SSD_GROUPS, hg)
    flip = lambda t: jnp.flip(t, axis=1)
    y_fwd = ssd_chunked(xs, dt[:, :, 0], a_neg[0], bm, cm)
    y_bwd = flip(ssd_chunked(flip(xs), flip(dt[:, :, 1]), a_neg[1], flip(bm), flip(cm)))
    y = y_fwd + y_bwd + d_skip.astype(jnp.float32).reshape(SSD_GROUPS, hg)[..., None] * xs
    y = y.reshape(b, s, SSD_INNER) * jax.nn.silu(z.astype(jnp.float32))
    y = rmsnorm(y.reshape(b, s, SSD_GROUPS, SSD_INNER // SSD_GROUPS),
                out_gain.reshape(SSD_GROUPS, SSD_INNER // SSD_GROUPS))
    return y.reshape(b, s, SSD_INNER).astype(xbc.dtype)


def gmlp_mixer(u, v, v_gain, w_s, b_s):
    b, s, _ = u.shape
    nc = s // GMLP_CHUNK
    u = jax.nn.gelu(u)
    v = rmsnorm(jax.nn.gelu(v), v_gain)
    vc = v.reshape(b, nc, GMLP_CHUNK, GMLP_GROUPS, GMLP_GROUP_DIM)
    mixed = jnp.einsum('gts,bcsge->bctge', w_s, vc) + jnp.swapaxes(b_s, 0, 1)[:, :, None]
    return u * mixed.reshape(b, s, GMLP_WIDTH)


def rel_bucket(rel):
    nb = REL_BUCKETS // 2
    max_exact = nb // 2
    n = jnp.abs(rel)
    large = max_exact + (jnp.log(jnp.maximum(n, 1).astype(jnp.float32) / max_exact)
                         / math.log(REL_MAX_DISTANCE / max_exact) * (nb - max_exact)).astype(jnp.int32)
    large = jnp.minimum(large, nb - 1)
    return jnp.where(rel > 0, nb, 0) + jnp.where(n < max_exact, n, large)


def dilated_branch(q, k, v, window, dil, rel_table):
    b, s, h, e = q.shape
    half = window // (2 * dil)
    length = s // dil
    nb = -(-length // half)
    lp = nb * half

    def residues(t):
        return t.reshape(b, length, dil, h, e).transpose(0, 2, 1, 3, 4)

    def key_blocks(t):
        tp = jnp.pad(residues(t), ((0, 0), (0, 0), (half, lp - length + half), (0, 0), (0, 0)))
        tp = tp.reshape(b, dil, nb + 2, half, h, e)
        return jnp.concatenate([tp[:, :, i:i + nb] for i in range(3)], axis=3)

    qb = jnp.pad(residues(q), ((0, 0), (0, 0), (0, lp - length), (0, 0), (0, 0))).reshape(b, dil, nb, half, h, e)
    kb = key_blocks(k)
    vb = key_blocks(v)
    rel = jnp.arange(3 * half)[None, :] - half - jnp.arange(half)[:, None]
    bias = jnp.transpose(rel_table[rel_bucket(rel * dil)], (2, 0, 1)).astype(jnp.float32)
    key_pos = jnp.arange(nb)[:, None] * half - half + jnp.arange(3 * half)[None, :]
    mask = (jnp.abs(rel) <= half)[None] & ((key_pos >= 0) & (key_pos < length))[:, None, :]
    logits = jnp.einsum('brcthe,brcshe->brchts', qb, kb).astype(jnp.float32) * (e ** -0.5) + bias
    logits = jnp.where(mask[:, None], logits, NEG_INF)
    lse = jax.nn.logsumexp(logits, axis=-1)
    probs = jnp.exp(logits - lse[..., None]).astype(v.dtype)
    out = jnp.einsum('brchts,brcshe->brcthe', probs, vb).reshape(b, dil, lp, h, e)[:, :, :length]
    out = out.transpose(0, 2, 1, 3, 4).reshape(b, s, h, e)
    lse = lse.transpose(0, 1, 2, 4, 3).reshape(b, dil, lp, h)[:, :, :length]
    lse = lse.transpose(0, 2, 1, 3).reshape(b, s, h)
    return out, lse


def attn_mixer(q, k, v, q_gain, k_gain, rel_table):
    b, s, _ = q.shape
    q = rmsnorm(q.reshape(b, s, ATTN_HEADS, ATTN_HEAD_DIM), q_gain)
    k = rmsnorm(k.reshape(b, s, ATTN_HEADS, ATTN_HEAD_DIM), k_gain)
    v = v.reshape(b, s, ATTN_HEADS, ATTN_HEAD_DIM)
    outs, lses = [], []
    for window, dil in zip(ATTN_WINDOWS, ATTN_DILATIONS):
        o, l = dilated_branch(q, k, v, window, dil, rel_table)
        outs.append(o)
        lses.append(l)
    weights = jax.nn.softmax(jnp.stack(lses), axis=0)
    out = jnp.sum(weights[..., None] * jnp.stack(outs).astype(jnp.float32), axis=0)
    return out.reshape(b, s, ATTN_WIDTH).astype(v.dtype)


def conv_ffn(x, w_up, conv_w, conv_b, w_down):
    hid = dwconv_centred(x @ w_up, conv_w, conv_b)
    gate, val = jnp.split(hid, 2, axis=-1)
    return (jax.nn.silu(gate) * val) @ w_down


def _fwd_setup_inputs(seed: int = 0) -> dict:
    key = jax.random.key(seed)
    ks = jax.random.split(key, 24)
    nrm = lambda k, shape, scale: jax.random.normal(k, shape, jnp.float32) * scale
    dt0 = jnp.exp(jax.random.uniform(ks[5], (DEPTH, 2, SSD_HEADS), jnp.float32,
                                     math.log(1e-3), math.log(1e-1)))
    return {
        'x': nrm(ks[0], (BATCH, SEQ, D_MODEL), 1.0),
        'mix_norm_gain': 1.0 + nrm(ks[1], (DEPTH, D_MODEL), 0.02),
        'w_in': nrm(ks[2], (DEPTH, D_MODEL, IN_WIDTH), D_MODEL ** -0.5),
        'ssd_conv_w': nrm(ks[3], (DEPTH, SSD_CONV, SSD_XBC), SSD_CONV ** -0.5),
        'ssd_conv_b': nrm(ks[4], (DEPTH, SSD_XBC), 0.02),
        'ssd_dt_bias': dt0 + jnp.log(-jnp.expm1(-dt0)),
        'ssd_a_log': jnp.log(jax.random.uniform(ks[6], (DEPTH, 2, SSD_HEADS), jnp.float32, 1.0, 16.0)),
        'ssd_d': 1.0 + nrm(ks[7], (DEPTH, SSD_HEADS), 0.02),
        'ssd_out_gain': 1.0 + nrm(ks[8], (DEPTH, SSD_INNER), 0.02),
        'gmlp_v_gain': 1.0 + nrm(ks[9], (DEPTH, GMLP_WIDTH), 0.02),
        'gmlp_w_s': nrm(ks[10], (DEPTH, GMLP_GROUPS, GMLP_CHUNK, GMLP_CHUNK), GMLP_CHUNK ** -0.5),
        'gmlp_b_s': 1.0 + nrm(ks[11], (DEPTH, GMLP_GROUPS, GMLP_CHUNK), 0.02),
        'attn_q_gain': 1.0 + nrm(ks[12], (DEPTH, ATTN_HEAD_DIM), 0.02),
        'attn_k_gain': 1.0 + nrm(ks[13], (DEPTH, ATTN_HEAD_DIM), 0.02),
        'rel_bias_table': nrm(ks[14], (REL_BUCKETS, ATTN_HEADS), 0.2),
        'w_out': nrm(ks[15], (DEPTH, MIX_WIDTH, D_MODEL), MIX_WIDTH ** -0.5),
        'ffn_norm_gain': 1.0 + nrm(ks[16], (DEPTH, D_MODEL), 0.02),
        'ffn_w_up': nrm(ks[17], (DEPTH, D_MODEL, 2 * FFN_DIM), D_MODEL ** -0.5),
        'ffn_conv_w': nrm(ks[18], (DEPTH, FFN_CONV, 2 * FFN_DIM), FFN_CONV ** -0.5),
        'ffn_conv_b': nrm(ks[19], (DEPTH, 2 * FFN_DIM), 0.02),
        'ffn_w_down': nrm(ks[20], (DEPTH, FFN_DIM, D_MODEL), FFN_DIM ** -0.5),
    }


def _fwd_reference(x, mix_norm_gain, w_in, ssd_conv_w, ssd_conv_b, ssd_dt_bias, ssd_a_log, ssd_d,
              ssd_out_gain, gmlp_v_gain, gmlp_w_s, gmlp_b_s, attn_q_gain, attn_k_gain,
              rel_bias_table, w_out, ffn_norm_gain, ffn_w_up, ffn_conv_w, ffn_conv_b, ffn_w_down):
    split_points = np.cumsum(IN_WIDTHS)[:-1].tolist()
    for l in range(DEPTH):
        h = rmsnorm(x, mix_norm_gain[l])
        proj = h @ w_in[l]
        xbc, z, dt_raw, gu, gv, q, k, v = jnp.split(proj, split_points, axis=-1)
        y_ssd = ssd_mixer(xbc, z, dt_raw, ssd_conv_w[l], ssd_conv_b[l], ssd_dt_bias[l],
                          ssd_a_log[l], ssd_d[l], ssd_out_gain[l])
        y_gmlp = gmlp_mixer(gu, gv, gmlp_v_gain[l], gmlp_w_s[l], gmlp_b_s[l])
        y_attn = attn_mixer(q, k, v, attn_q_gain[l], attn_k_gain[l], rel_bias_table)
        y = jnp.concatenate([y_ssd, y_gmlp.astype(x.dtype), y_attn], axis=-1)
        x = x + y @ w_out[l]
        x = x + conv_ffn(rmsnorm(x, ffn_norm_gain[l]), ffn_w_up[l], ffn_conv_w[l],
                         ffn_conv_b[l], ffn_w_down[l])
    return x


import jax as _jax
import jax.numpy as _jnp

TWIN_FORMAT = 'train_step'
FWD_PARAMS = ['x', 'mix_norm_gain', 'w_in', 'ssd_conv_w', 'ssd_conv_b', 'ssd_dt_bias', 'ssd_a_log', 'ssd_d', 'ssd_out_gain', 'gmlp_v_gain', 'gmlp_w_s', 'gmlp_b_s', 'attn_q_gain', 'attn_k_gain', 'rel_bias_table', 'w_out', 'ffn_norm_gain', 'ffn_w_up', 'ffn_conv_w', 'ffn_conv_b', 'ffn_w_down']
TWIN_WEIGHTS = ['mix_norm_gain', 'w_in', 'ssd_conv_w', 'ssd_conv_b', 'ssd_dt_bias', 'ssd_a_log', 'ssd_d', 'ssd_out_gain', 'gmlp_v_gain', 'gmlp_w_s', 'gmlp_b_s', 'attn_q_gain', 'attn_k_gain', 'rel_bias_table', 'w_out', 'ffn_norm_gain', 'ffn_w_up', 'ffn_conv_w', 'ffn_conv_b', 'ffn_w_down']
TWIN_DIFF_INPUT = 'x'
TWIN_INPUTS = ['x', 'mix_norm_gain', 'w_in', 'ssd_conv_w', 'ssd_conv_b', 'ssd_dt_bias', 'ssd_a_log', 'ssd_d', 'ssd_out_gain', 'gmlp_v_gain', 'gmlp_w_s', 'gmlp_b_s', 'attn_q_gain', 'attn_k_gain', 'rel_bias_table', 'w_out', 'ffn_norm_gain', 'ffn_w_up', 'ffn_conv_w', 'ffn_conv_b', 'ffn_w_down', 'loss_target', 'm_mix_norm_gain', 'm_w_in', 'm_ssd_conv_w', 'm_ssd_conv_b', 'm_ssd_dt_bias', 'm_ssd_a_log', 'm_ssd_d', 'm_ssd_out_gain', 'm_gmlp_v_gain', 'm_gmlp_w_s', 'm_gmlp_b_s', 'm_attn_q_gain', 'm_attn_k_gain', 'm_rel_bias_table', 'm_w_out', 'm_ffn_norm_gain', 'm_ffn_w_up', 'm_ffn_conv_w', 'm_ffn_conv_b', 'm_ffn_w_down', 'v_mix_norm_gain', 'v_w_in', 'v_ssd_conv_w', 'v_ssd_conv_b', 'v_ssd_dt_bias', 'v_ssd_a_log', 'v_ssd_d', 'v_ssd_out_gain', 'v_gmlp_v_gain', 'v_gmlp_w_s', 'v_gmlp_b_s', 'v_attn_q_gain', 'v_attn_k_gain', 'v_rel_bias_table', 'v_w_out', 'v_ffn_norm_gain', 'v_ffn_w_up', 'v_ffn_conv_w', 'v_ffn_conv_b', 'v_ffn_w_down']
TWIN_OUTPUTS = ['loss', 'grad_x', 'grad_mix_norm_gain', 'grad_w_in', 'grad_ssd_conv_w', 'grad_ssd_conv_b', 'grad_ssd_dt_bias', 'grad_ssd_a_log', 'grad_ssd_d', 'grad_ssd_out_gain', 'grad_gmlp_v_gain', 'grad_gmlp_w_s', 'grad_gmlp_b_s', 'grad_attn_q_gain', 'grad_attn_k_gain', 'grad_rel_bias_table', 'grad_w_out', 'grad_ffn_norm_gain', 'grad_ffn_w_up', 'grad_ffn_conv_w', 'grad_ffn_conv_b', 'grad_ffn_w_down', 'delta_mix_norm_gain', 'delta_w_in', 'delta_ssd_conv_w', 'delta_ssd_conv_b', 'delta_ssd_dt_bias', 'delta_ssd_a_log', 'delta_ssd_d', 'delta_ssd_out_gain', 'delta_gmlp_v_gain', 'delta_gmlp_w_s', 'delta_gmlp_b_s', 'delta_attn_q_gain', 'delta_attn_k_gain', 'delta_rel_bias_table', 'delta_w_out', 'delta_ffn_norm_gain', 'delta_ffn_w_up', 'delta_ffn_conv_w', 'delta_ffn_conv_b', 'delta_ffn_w_down', 'new_m_mix_norm_gain', 'new_m_w_in', 'new_m_ssd_conv_w', 'new_m_ssd_conv_b', 'new_m_ssd_dt_bias', 'new_m_ssd_a_log', 'new_m_ssd_d', 'new_m_ssd_out_gain', 'new_m_gmlp_v_gain', 'new_m_gmlp_w_s', 'new_m_gmlp_b_s', 'new_m_attn_q_gain', 'new_m_attn_k_gain', 'new_m_rel_bias_table', 'new_m_w_out', 'new_m_ffn_norm_gain', 'new_m_ffn_w_up', 'new_m_ffn_conv_w', 'new_m_ffn_conv_b', 'new_m_ffn_w_down', 'new_v_mix_norm_gain', 'new_v_w_in', 'new_v_ssd_conv_w', 'new_v_ssd_conv_b', 'new_v_ssd_dt_bias', 'new_v_ssd_a_log', 'new_v_ssd_d', 'new_v_ssd_out_gain', 'new_v_gmlp_v_gain', 'new_v_gmlp_w_s', 'new_v_gmlp_b_s', 'new_v_attn_q_gain', 'new_v_attn_k_gain', 'new_v_rel_bias_table', 'new_v_w_out', 'new_v_ffn_norm_gain', 'new_v_ffn_w_up', 'new_v_ffn_conv_w', 'new_v_ffn_conv_b', 'new_v_ffn_w_down']
TWIN_LEAF_KINDS = {'loss': 'loss', 'grad_x': 'grad_x', 'grad_mix_norm_gain': 'grad_w', 'grad_w_in': 'grad_w', 'grad_ssd_conv_w': 'grad_w', 'grad_ssd_conv_b': 'grad_w', 'grad_ssd_dt_bias': 'grad_w', 'grad_ssd_a_log': 'grad_w', 'grad_ssd_d': 'grad_w', 'grad_ssd_out_gain': 'grad_w', 'grad_gmlp_v_gain': 'grad_w', 'grad_gmlp_w_s': 'grad_w', 'grad_gmlp_b_s': 'grad_w', 'grad_attn_q_gain': 'grad_w', 'grad_attn_k_gain': 'grad_w', 'grad_rel_bias_table': 'grad_w', 'grad_w_out': 'grad_w', 'grad_ffn_norm_gain': 'grad_w', 'grad_ffn_w_up': 'grad_w', 'grad_ffn_conv_w': 'grad_w', 'grad_ffn_conv_b': 'grad_w', 'grad_ffn_w_down': 'grad_w', 'delta_mix_norm_gain': 'delta_w', 'delta_w_in': 'delta_w', 'delta_ssd_conv_w': 'delta_w', 'delta_ssd_conv_b': 'delta_w', 'delta_ssd_dt_bias': 'delta_w', 'delta_ssd_a_log': 'delta_w', 'delta_ssd_d': 'delta_w', 'delta_ssd_out_gain': 'delta_w', 'delta_gmlp_v_gain': 'delta_w', 'delta_gmlp_w_s': 'delta_w', 'delta_gmlp_b_s': 'delta_w', 'delta_attn_q_gain': 'delta_w', 'delta_attn_k_gain': 'delta_w', 'delta_rel_bias_table': 'delta_w', 'delta_w_out': 'delta_w', 'delta_ffn_norm_gain': 'delta_w', 'delta_ffn_w_up': 'delta_w', 'delta_ffn_conv_w': 'delta_w', 'delta_ffn_conv_b': 'delta_w', 'delta_ffn_w_down': 'delta_w', 'new_m_mix_norm_gain': 'new_m', 'new_m_w_in': 'new_m', 'new_m_ssd_conv_w': 'new_m', 'new_m_ssd_conv_b': 'new_m', 'new_m_ssd_dt_bias': 'new_m', 'new_m_ssd_a_log': 'new_m', 'new_m_ssd_d': 'new_m', 'new_m_ssd_out_gain': 'new_m', 'new_m_gmlp_v_gain': 'new_m', 'new_m_gmlp_w_s': 'new_m', 'new_m_gmlp_b_s': 'new_m', 'new_m_attn_q_gain': 'new_m', 'new_m_attn_k_gain': 'new_m', 'new_m_rel_bias_table': 'new_m', 'new_m_w_out': 'new_m', 'new_m_ffn_norm_gain': 'new_m', 'new_m_ffn_w_up': 'new_m', 'new_m_ffn_conv_w': 'new_m', 'new_m_ffn_conv_b': 'new_m', 'new_m_ffn_w_down': 'new_m', 'new_v_mix_norm_gain': 'new_v', 'new_v_w_in': 'new_v', 'new_v_ssd_conv_w': 'new_v', 'new_v_ssd_conv_b': 'new_v', 'new_v_ssd_dt_bias': 'new_v', 'new_v_ssd_a_log': 'new_v', 'new_v_ssd_d': 'new_v', 'new_v_ssd_out_gain': 'new_v', 'new_v_gmlp_v_gain': 'new_v', 'new_v_gmlp_w_s': 'new_v', 'new_v_gmlp_b_s': 'new_v', 'new_v_attn_q_gain': 'new_v', 'new_v_attn_k_gain': 'new_v', 'new_v_rel_bias_table': 'new_v', 'new_v_w_out': 'new_v', 'new_v_ffn_norm_gain': 'new_v', 'new_v_ffn_w_up': 'new_v', 'new_v_ffn_conv_w': 'new_v', 'new_v_ffn_conv_b': 'new_v', 'new_v_ffn_w_down': 'new_v'}


def _forward(args):
    return _fwd_reference(*[args[k] for k in FWD_PARAMS])


def _output_shape():
    out = _jax.eval_shape(lambda: _forward(_fwd_setup_inputs(0)))
    return out.shape, out.dtype

N_MICROBATCH = 1
ADAM_LR = 0.001
ADAM_B1 = 0.9
ADAM_B2 = 0.999
ADAM_EPS = 1e-08
ADAM_WD = 0.01
ADAM_STEP = 10
PER_EXAMPLE_BATCH_AXIS = {'x': 0, 'loss_target': 0}
SHARED_INPUTS = []
_WEIGHT_DTYPES = {'mix_norm_gain': _jnp.float32, 'w_in': _jnp.float32, 'ssd_conv_w': _jnp.float32, 'ssd_conv_b': _jnp.float32, 'ssd_dt_bias': _jnp.float32, 'ssd_a_log': _jnp.float32, 'ssd_d': _jnp.float32, 'ssd_out_gain': _jnp.float32, 'gmlp_v_gain': _jnp.float32, 'gmlp_w_s': _jnp.float32, 'gmlp_b_s': _jnp.float32, 'attn_q_gain': _jnp.float32, 'attn_k_gain': _jnp.float32, 'rel_bias_table': _jnp.float32, 'w_out': _jnp.float32, 'ffn_norm_gain': _jnp.float32, 'ffn_w_up': _jnp.float32, 'ffn_conv_w': _jnp.float32, 'ffn_conv_b': _jnp.float32, 'ffn_w_down': _jnp.float32}
MOMENT_SCALE = {'mix_norm_gain': 7.516036e+00, 'w_in': 1.396063e+00, 'ssd_conv_w': 2.047474e+00, 'ssd_conv_b': 8.198040e+00, 'ssd_dt_bias': 2.090471e+00, 'ssd_a_log': 1.045219e+01, 'ssd_d': 9.043400e+00, 'ssd_out_gain': 4.781228e+01, 'gmlp_v_gain': 1.437024e+01, 'gmlp_w_s': 2.792952e+00, 'gmlp_b_s': 7.230670e+00, 'attn_q_gain': 8.306421e-01, 'attn_k_gain': 8.308109e-01, 'rel_bias_table': 5.078932e-01, 'w_out': 4.840771e+00, 'ffn_norm_gain': 2.552334e+01, 'ffn_w_up': 8.106826e-01, 'ffn_conv_w': 3.570046e+00, 'ffn_conv_b': 4.047117e+00, 'ffn_w_down': 8.862563e-01}


def _to_microbatches(a, axis):
    t = _jnp.moveaxis(a, axis, 0)
    t = t.reshape((N_MICROBATCH, t.shape[0] // N_MICROBATCH) + t.shape[1:])
    return _jnp.moveaxis(t, 1, axis + 1)


def setup_inputs(seed: int = 0) -> dict:
    inp = _fwd_setup_inputs(seed)
    key = _jax.random.fold_in(_jax.random.key(seed), 7919)
    shape, _ = _output_shape()
    out = dict(inp)
    out["loss_target"] = _jax.random.normal(_jax.random.fold_in(key, 0), shape, _jnp.float32)
    for i, name in enumerate(TWIN_WEIGHTS):
        w = inp[name].astype(_jnp.float32)
        if MOMENT_SCALE is None:
            s = _jnp.sqrt(_jnp.mean(_jnp.square(w)) + 1e-30)
        else:
            s = MOMENT_SCALE[name]
        km, kv = _jax.random.split(_jax.random.fold_in(key, i + 1))
        out[name] = w
        out["m_" + name] = s * _jax.random.normal(km, w.shape, _jnp.float32)
        out["v_" + name] = (s * s) * _jax.random.uniform(kv, w.shape, _jnp.float32, 0.5, 1.5)
    if N_MICROBATCH > 1:
        for name, axis in PER_EXAMPLE_BATCH_AXIS.items():
            out[name] = _to_microbatches(out[name], axis)
    return {'x': out['x'], 'mix_norm_gain': out['mix_norm_gain'], 'w_in': out['w_in'], 'ssd_conv_w': out['ssd_conv_w'], 'ssd_conv_b': out['ssd_conv_b'], 'ssd_dt_bias': out['ssd_dt_bias'], 'ssd_a_log': out['ssd_a_log'], 'ssd_d': out['ssd_d'], 'ssd_out_gain': out['ssd_out_gain'], 'gmlp_v_gain': out['gmlp_v_gain'], 'gmlp_w_s': out['gmlp_w_s'], 'gmlp_b_s': out['gmlp_b_s'], 'attn_q_gain': out['attn_q_gain'], 'attn_k_gain': out['attn_k_gain'], 'rel_bias_table': out['rel_bias_table'], 'w_out': out['w_out'], 'ffn_norm_gain': out['ffn_norm_gain'], 'ffn_w_up': out['ffn_w_up'], 'ffn_conv_w': out['ffn_conv_w'], 'ffn_conv_b': out['ffn_conv_b'], 'ffn_w_down': out['ffn_w_down'], 'loss_target': out['loss_target'], 'm_mix_norm_gain': out['m_mix_norm_gain'], 'm_w_in': out['m_w_in'], 'm_ssd_conv_w': out['m_ssd_conv_w'], 'm_ssd_conv_b': out['m_ssd_conv_b'], 'm_ssd_dt_bias': out['m_ssd_dt_bias'], 'm_ssd_a_log': out['m_ssd_a_log'], 'm_ssd_d': out['m_ssd_d'], 'm_ssd_out_gain': out['m_ssd_out_gain'], 'm_gmlp_v_gain': out['m_gmlp_v_gain'], 'm_gmlp_w_s': out['m_gmlp_w_s'], 'm_gmlp_b_s': out['m_gmlp_b_s'], 'm_attn_q_gain': out['m_attn_q_gain'], 'm_attn_k_gain': out['m_attn_k_gain'], 'm_rel_bias_table': out['m_rel_bias_table'], 'm_w_out': out['m_w_out'], 'm_ffn_norm_gain': out['m_ffn_norm_gain'], 'm_ffn_w_up': out['m_ffn_w_up'], 'm_ffn_conv_w': out['m_ffn_conv_w'], 'm_ffn_conv_b': out['m_ffn_conv_b'], 'm_ffn_w_down': out['m_ffn_w_down'], 'v_mix_norm_gain': out['v_mix_norm_gain'], 'v_w_in': out['v_w_in'], 'v_ssd_conv_w': out['v_ssd_conv_w'], 'v_ssd_conv_b': out['v_ssd_conv_b'], 'v_ssd_dt_bias': out['v_ssd_dt_bias'], 'v_ssd_a_log': out['v_ssd_a_log'], 'v_ssd_d': out['v_ssd_d'], 'v_ssd_out_gain': out['v_ssd_out_gain'], 'v_gmlp_v_gain': out['v_gmlp_v_gain'], 'v_gmlp_w_s': out['v_gmlp_w_s'], 'v_gmlp_b_s': out['v_gmlp_b_s'], 'v_attn_q_gain': out['v_attn_q_gain'], 'v_attn_k_gain': out['v_attn_k_gain'], 'v_rel_bias_table': out['v_rel_bias_table'], 'v_w_out': out['v_w_out'], 'v_ffn_norm_gain': out['v_ffn_norm_gain'], 'v_ffn_w_up': out['v_ffn_w_up'], 'v_ffn_conv_w': out['v_ffn_conv_w'], 'v_ffn_conv_b': out['v_ffn_conv_b'], 'v_ffn_w_down': out['v_ffn_w_down']}


def _loss(weights, diff, rest, loss_target):
    with _jax.named_scope("forward"):
        args = {**rest, TWIN_DIFF_INPUT: diff, **{k: w.astype(_WEIGHT_DTYPES[k]) for k, w in weights.items()}}
        y = _forward(args)
    with _jax.named_scope("loss_head"):
        err = _jnp.square(y.astype(_jnp.float32) - loss_target)
        return 0.5 * _jnp.sum(_jnp.mean(err, axis=-1)) if err.ndim else 0.5 * err


def _adamw(w, g, m, v):
    m = ADAM_B1 * m + (1.0 - ADAM_B1) * g
    v = ADAM_B2 * v + (1.0 - ADAM_B2) * _jnp.square(g)
    m_hat = m / (1.0 - ADAM_B1 ** ADAM_STEP)
    v_hat = v / (1.0 - ADAM_B2 ** ADAM_STEP)
    delta = -ADAM_LR * (m_hat / (_jnp.sqrt(v_hat) + ADAM_EPS) + ADAM_WD * w)
    return delta, m, v


def reference(x, mix_norm_gain, w_in, ssd_conv_w, ssd_conv_b, ssd_dt_bias, ssd_a_log, ssd_d, ssd_out_gain, gmlp_v_gain, gmlp_w_s, gmlp_b_s, attn_q_gain, attn_k_gain, rel_bias_table, w_out, ffn_norm_gain, ffn_w_up, ffn_conv_w, ffn_conv_b, ffn_w_down, loss_target, m_mix_norm_gain, m_w_in, m_ssd_conv_w, m_ssd_conv_b, m_ssd_dt_bias, m_ssd_a_log, m_ssd_d, m_ssd_out_gain, m_gmlp_v_gain, m_gmlp_w_s, m_gmlp_b_s, m_attn_q_gain, m_attn_k_gain, m_rel_bias_table, m_w_out, m_ffn_norm_gain, m_ffn_w_up, m_ffn_conv_w, m_ffn_conv_b, m_ffn_w_down, v_mix_norm_gain, v_w_in, v_ssd_conv_w, v_ssd_conv_b, v_ssd_dt_bias, v_ssd_a_log, v_ssd_d, v_ssd_out_gain, v_gmlp_v_gain, v_gmlp_w_s, v_gmlp_b_s, v_attn_q_gain, v_attn_k_gain, v_rel_bias_table, v_w_out, v_ffn_norm_gain, v_ffn_w_up, v_ffn_conv_w, v_ffn_conv_b, v_ffn_w_down):
    given = dict(x=x, mix_norm_gain=mix_norm_gain, w_in=w_in, ssd_conv_w=ssd_conv_w, ssd_conv_b=ssd_conv_b, ssd_dt_bias=ssd_dt_bias, ssd_a_log=ssd_a_log, ssd_d=ssd_d, ssd_out_gain=ssd_out_gain, gmlp_v_gain=gmlp_v_gain, gmlp_w_s=gmlp_w_s, gmlp_b_s=gmlp_b_s, attn_q_gain=attn_q_gain, attn_k_gain=attn_k_gain, rel_bias_table=rel_bias_table, w_out=w_out, ffn_norm_gain=ffn_norm_gain, ffn_w_up=ffn_w_up, ffn_conv_w=ffn_conv_w, ffn_conv_b=ffn_conv_b, ffn_w_down=ffn_w_down, loss_target=loss_target, m_mix_norm_gain=m_mix_norm_gain, m_w_in=m_w_in, m_ssd_conv_w=m_ssd_conv_w, m_ssd_conv_b=m_ssd_conv_b, m_ssd_dt_bias=m_ssd_dt_bias, m_ssd_a_log=m_ssd_a_log, m_ssd_d=m_ssd_d, m_ssd_out_gain=m_ssd_out_gain, m_gmlp_v_gain=m_gmlp_v_gain, m_gmlp_w_s=m_gmlp_w_s, m_gmlp_b_s=m_gmlp_b_s, m_attn_q_gain=m_attn_q_gain, m_attn_k_gain=m_attn_k_gain, m_rel_bias_table=m_rel_bias_table, m_w_out=m_w_out, m_ffn_norm_gain=m_ffn_norm_gain, m_ffn_w_up=m_ffn_w_up, m_ffn_conv_w=m_ffn_conv_w, m_ffn_conv_b=m_ffn_conv_b, m_ffn_w_down=m_ffn_w_down, v_mix_norm_gain=v_mix_norm_gain, v_w_in=v_w_in, v_ssd_conv_w=v_ssd_conv_w, v_ssd_conv_b=v_ssd_conv_b, v_ssd_dt_bias=v_ssd_dt_bias, v_ssd_a_log=v_ssd_a_log, v_ssd_d=v_ssd_d, v_ssd_out_gain=v_ssd_out_gain, v_gmlp_v_gain=v_gmlp_v_gain, v_gmlp_w_s=v_gmlp_w_s, v_gmlp_b_s=v_gmlp_b_s, v_attn_q_gain=v_attn_q_gain, v_attn_k_gain=v_attn_k_gain, v_rel_bias_table=v_rel_bias_table, v_w_out=v_w_out, v_ffn_norm_gain=v_ffn_norm_gain, v_ffn_w_up=v_ffn_w_up, v_ffn_conv_w=v_ffn_conv_w, v_ffn_conv_b=v_ffn_conv_b, v_ffn_w_down=v_ffn_w_down)
    weights = {n: given[n] for n in TWIN_WEIGHTS}
    shared = {n: given[n] for n in SHARED_INPUTS}
    per_example = {n: given[n] for n in ['x']}
    grad_fn = _jax.value_and_grad(_loss, argnums=(0, 1))

    def one_microbatch(ex, loss_target):
        ex = dict(ex)
        diff = ex.pop(TWIN_DIFF_INPUT)
        return grad_fn(weights, diff, {**shared, **ex}, loss_target)

    if N_MICROBATCH == 1:
        loss, (grad_w, grad_x) = one_microbatch(per_example, given["loss_target"])
    else:
        def body(carry, xs):
            loss_sum, grad_sum = carry
            l_k, (gw_k, gx_k) = one_microbatch(xs[0], xs[1])
            with _jax.named_scope("update"):
                return (loss_sum + l_k, _jax.tree.map(_jnp.add, grad_sum, gw_k)), gx_k

        init = (_jnp.zeros((), _jnp.float32), _jax.tree.map(_jnp.zeros_like, weights))
        (loss, grad_w), grad_x = _jax.lax.scan(body, init, (per_example, given["loss_target"]))
    with _jax.named_scope("update"):
        delta_w, new_m, new_v = {}, {}, {}
        for n in TWIN_WEIGHTS:
            delta_w[n], new_m[n], new_v[n] = _adamw(weights[n], grad_w[n], given["m_" + n], given["v_" + n])
    return (loss, grad_x, *[grad_w[n] for n in TWIN_WEIGHTS], *[delta_w[n] for n in TWIN_WEIGHTS],
            *[new_m[n] for n in TWIN_WEIGHTS], *[new_v[n] for n in TWIN_WEIGHTS])
```

```python
import functools
import math

import jax
import jax.numpy as jnp
import numpy as np
from jax import lax
from jax.experimental import pallas as pl
from jax.experimental.pallas import tpu as pltpu

F32 = jnp.float32
BF16 = jnp.bfloat16
HI = lax.Precision.HIGHEST
MESH = pl.DeviceIdType.MESH
ANY = pl.BlockSpec(memory_space=pl.ANY)

D_MODEL = 1024
SEQ = 2048
B_LOC = 2
T = B_LOC * SEQ
DEPTH = 4
N_CHIPS = 4
N_DEV = 8
HEAD = 64
CHUNK = 128
N_CHUNK = SEQ // CHUNK
SSD_INNER = 512
SSD_XBC = 1024
FFN_DIM = 2816
IN_WIDTH = 2832
NP = 3072
C_XS, C_B, C_C, C_Z, C_GU, C_GV, C_Q, C_K, C_V, C_DT = 0, 512, 768, 1024, 1536, 1792, 2048, 2304, 2560, 2816
NORM_EPS = 1e-6
NEG_INF = -1e30
ATTN_DILS = (1, 4, 16)
ATTN_HALF = 64
ADAM_LR, ADAM_B1, ADAM_B2, ADAM_EPS, ADAM_WD, ADAM_STEP = 0.001, 0.9, 0.999, 1e-08, 0.01, 10
VMEM_LIMIT = 56 * 1024 * 1024

S_ = jax.ShapeDtypeStruct


def _cp():
    return pltpu.CompilerParams(vmem_limit_bytes=VMEM_LIMIT)


def _shift_rows(x, k):
    n = x.shape[0]
    if k == 0:
        return x
    r = pltpu.roll(x, (-k) % n, 0)
    t = lax.broadcasted_iota(jnp.int32, (n, 1), 0)
    return jnp.where((t + k >= 0) & (t + k < n), r, 0.0)


@functools.partial(jax.custom_vjp, nondiff_argnums=(1,))
def _shift(x, k):
    return _shift_rows(x, k)


def _shift_fwd(x, k):
    return _shift_rows(x, k), None


def _shift_bwd(k, _, g):
    return (_shift_rows(g, -k),)


_shift.defvjp(_shift_fwd, _shift_bwd)


def _dwconv(x, taps, bias):
    half = len(taps) // 2
    y = bias
    for k, w in enumerate(taps):
        y = y + w * _shift(x, k - half)
    return y


def _softplus(x):
    return jnp.maximum(x, 0.0) + jnp.log1p(jnp.exp(-jnp.abs(x)))


def _dot(a, b):
    return jnp.dot(a.astype(BF16), b.astype(BF16), preferred_element_type=F32)


def _dot_nt(a, b):
    return lax.dot_general(a.astype(BF16), b.astype(BF16), (((1,), (1,)), ((), ())), preferred_element_type=F32)


def _dot_tn(a, b):
    return lax.dot_general(a.astype(BF16), b.astype(BF16), (((0,), (0,)), ((), ())), preferred_element_type=F32)


def _head_sum_matrix(width):
    i = lax.broadcasted_iota(jnp.int32, (width, width), 0) // HEAD
    j = lax.broadcasted_iota(jnp.int32, (width, width), 1) // HEAD
    return (i == j).astype(F32)


def _matmul(a, b, *, dims, grid, a_spec, b_spec, o_spec, out_shape, acc_shape, res=None, res_spec=None, name):
    nk = grid[2]

    def body(*refs):
        if res is not None:
            a_ref, b_ref, r_ref, o_ref = refs[:4]
        else:
            a_ref, b_ref, o_ref = refs[:3]
            r_ref = None
        part = lax.dot_general(a_ref[...].astype(BF16), b_ref[...].astype(BF16), dims, preferred_element_type=F32)
        if nk == 1:
            if r_ref is not None:
                part = part + r_ref[...]
            o_ref[...] = part.astype(o_ref.dtype)
            return
        acc_ref = refs[-1]
        k = pl.program_id(2)

        @pl.when(k == 0)
        def _():
            acc_ref[...] = part

        @pl.when(k > 0)
        def _():
            acc_ref[...] += part

        @pl.when(k == nk - 1)
        def _():
            tot = acc_ref[...]
            if r_ref is not None:
                tot = tot + r_ref[...]
            o_ref[...] = tot.astype(o_ref.dtype)

    in_specs = [a_spec, b_spec] + ([res_spec] if res is not None else [])
    args = (a, b) + ((res,) if res is not None else ())
    scratch = [] if nk == 1 else [pltpu.VMEM(acc_shape, F32)]
    return pl.pallas_call(body, grid=grid, in_specs=in_specs, out_specs=o_spec, out_shape=out_shape,
                          scratch_shapes=scratch, compiler_params=_cp(), name=name)(*args)


NN = (((1,), (0,)), ((), ()))
NT = (((1,), (1,)), ((), ()))
TN = (((0,), (0,)), ((), ()))


def _mm_nn(a, b, *, tm, tn, tk, out_dtype, res=None, name):
    m, k = a.shape
    n = b.shape[1]
    return _matmul(a, b, dims=NN, grid=(m // tm, n // tn, k // tk),
                   a_spec=pl.BlockSpec((tm, tk), lambda i, j, q: (i, q)),
                   b_spec=pl.BlockSpec((tk, tn), lambda i, j, q: (q, j)),
                   o_spec=pl.BlockSpec((tm, tn), lambda i, j, q: (i, j)),
                   out_shape=S_((m, n), out_dtype), acc_shape=(tm, tn), res=res,
                   res_spec=pl.BlockSpec((tm, tn), lambda i, j, q: (i, j)), name=name)


def _mm_nt(a, b, *, tm, tn, tk, out_dtype, name):
    m, k = a.shape
    n = b.shape[0]
    return _matmul(a, b, dims=NT, grid=(m // tm, n // tn, k // tk),
                   a_spec=pl.BlockSpec((tm, tk), lambda i, j, q: (i, q)),
                   b_spec=pl.BlockSpec((tn, tk), lambda i, j, q: (j, q)),
                   o_spec=pl.BlockSpec((tm, tn), lambda i, j, q: (i, j)),
                   out_shape=S_((m, n), out_dtype), acc_shape=(tm, tn), name=name)


def _mm_tn(a, b, *, tm, tn, tk, out_dtype, name):
    k, m = a.shape
    n = b.shape[1]
    return _matmul(a, b, dims=TN, grid=(m // tm, n // tn, k // tk),
                   a_spec=pl.BlockSpec((tk, tm), lambda i, j, q: (q, i)),
                   b_spec=pl.BlockSpec((tk, tn), lambda i, j, q: (q, j)),
                   o_spec=pl.BlockSpec((tm, tn), lambda i, j, q: (i, j)),
                   out_shape=S_((m, n), out_dtype), acc_shape=(tm, tn), name=name)


HALF_TILE = FFN_DIM // 2


def _mm_up(hn, w_up):
    return _matmul(hn, w_up, dims=NN, grid=(T // 1024, 4, 1),
                   a_spec=pl.BlockSpec((1024, D_MODEL), lambda i, j, q: (i, 0)),
                   b_spec=pl.BlockSpec((D_MODEL, HALF_TILE), lambda i, j, q: (0, j)),
                   o_spec=pl.BlockSpec((None, 1024, HALF_TILE), lambda i, j, q: (j // 2, i, j % 2)),
                   out_shape=S_((2, T, FFN_DIM), F32), acc_shape=(1024, HALF_TILE), name="mm_up")


def _mm_dhn(dup3, w_up):
    return _matmul(dup3, w_up, dims=NT, grid=(T // 1024, 1, 4),
                   a_spec=pl.BlockSpec((None, 1024, HALF_TILE), lambda i, j, q: (q // 2, i, q % 2)),
                   b_spec=pl.BlockSpec((D_MODEL, HALF_TILE), lambda i, j, q: (0, q)),
                   o_spec=pl.BlockSpec((1024, D_MODEL), lambda i, j, q: (i, 0)),
                   out_shape=S_((T, D_MODEL), F32), acc_shape=(1024, D_MODEL), name="mm_dhn")


def _mm_dwup(hn, dup3):
    return _matmul(hn, dup3, dims=TN, grid=(1, 4, T // 1024),
                   a_spec=pl.BlockSpec((1024, D_MODEL), lambda i, j, q: (q, 0)),
                   b_spec=pl.BlockSpec((None, 1024, HALF_TILE), lambda i, j, q: (j // 2, q, j % 2)),
                   o_spec=pl.BlockSpec((D_MODEL, HALF_TILE), lambda i, j, q: (0, j)),
                   out_shape=S_((D_MODEL, 2 * FFN_DIM), BF16), acc_shape=(D_MODEL, HALF_TILE), name="mm_dwup")


ROWS = 512


def _rmsnorm_fwd(x, gain, name):
    def body(x_ref, g_ref, o_ref):
        xv = x_ref[...]
        r = lax.rsqrt(jnp.mean(xv * xv, axis=-1, keepdims=True) + NORM_EPS)
        o_ref[...] = (xv * r * g_ref[...]).astype(BF16)

    return pl.pallas_call(body, grid=(T // ROWS,),
                          in_specs=[pl.BlockSpec((ROWS, D_MODEL), lambda i: (i, 0)), pl.BlockSpec((1, D_MODEL), lambda i: (0, 0))],
                          out_specs=pl.BlockSpec((ROWS, D_MODEL), lambda i: (i, 0)),
                          out_shape=S_((T, D_MODEL), BF16), name=name)(x, gain)


def _rmsnorm_bwd(x, gain, dh, dres, name):
    def body(x_ref, g_ref, dh_ref, dres_ref, dx_ref, dg_ref):
        xv = x_ref[...]
        r = lax.rsqrt(jnp.mean(xv * xv, axis=-1, keepdims=True) + NORM_EPS)
        gd = dh_ref[...] * g_ref[...]
        dot = jnp.mean(gd * xv, axis=-1, keepdims=True)
        dx_ref[...] = dres_ref[...] + r * gd - xv * (r * r * r * dot)
        part = jnp.sum(dh_ref[...] * xv * r, axis=0, keepdims=True)

        @pl.when(pl.program_id(0) == 0)
        def _():
            dg_ref[...] = part

        @pl.when(pl.program_id(0) > 0)
        def _():
            dg_ref[...] += part

    row = pl.BlockSpec((ROWS, D_MODEL), lambda i: (i, 0))
    vec = pl.BlockSpec((1, D_MODEL), lambda i: (0, 0))
    return pl.pallas_call(body, grid=(T // ROWS,), in_specs=[row, vec, row, row], out_specs=[row, vec],
                          out_shape=[S_((T, D_MODEL), F32), S_((1, D_MODEL), F32)], name=name)(x, gain, dh, dres)


def _loss_head(y, target):
    def body(y_ref, t_ref, dy_ref, p_ref):
        e = y_ref[...] - t_ref[...]
        dy_ref[...] = e * (1.0 / D_MODEL)
        p_ref[...] = jnp.full((8, 128), 0.5 / D_MODEL, F32) * jnp.sum(e * e)

    row = pl.BlockSpec((ROWS, D_MODEL), lambda i: (i, 0))
    return pl.pallas_call(body, grid=(T // ROWS,), in_specs=[row, row],
                          out_specs=[row, pl.BlockSpec((8, 128), lambda i: (i, 0))],
                          out_shape=[S_((T, D_MODEL), F32), S_((T // ROWS * 8, 128), F32)], name="loss_head")(y, target)


def _adamw(w, g, m, v, rows, name):
    r_tot, c = w.shape

    def body(w_ref, g_ref, m_ref, v_ref, d_ref, nm_ref, nv_ref):
        gv = g_ref[...]
        nm = ADAM_B1 * m_ref[...] + (1.0 - ADAM_B1) * gv
        nv = ADAM_B2 * v_ref[...] + (1.0 - ADAM_B2) * (gv * gv)
        m_hat = nm / (1.0 - ADAM_B1 ** ADAM_STEP)
        v_hat = nv / (1.0 - ADAM_B2 ** ADAM_STEP)
        d_ref[...] = -ADAM_LR * (m_hat / (jnp.sqrt(v_hat) + ADAM_EPS) + ADAM_WD * w_ref[...])
        nm_ref[...] = nm
        nv_ref[...] = nv

    blk = pl.BlockSpec((rows, c), lambda i: (i, 0))
    out = S_((r_tot, c), F32)
    return pl.pallas_call(body, grid=(r_tot // rows,), in_specs=[blk] * 4, out_specs=[blk] * 3,
                          out_shape=[out, out, out], name=name)(w, g, m, v)


FFN_CT = 256


def _gate_fn(up_g, up_v, wg0, wg1, wg2, bg, wv0, wv1, wv2, bv):
    gate = _dwconv(up_g, [wg0, wg1, wg2], bg)
    val = _dwconv(up_v, [wv0, wv1, wv2], bv)
    return jax.nn.silu(gate) * val


def _taps(ref, part, n):
    return [ref[part, k:k + 1, :] for k in range(n)]


def _convgate_fwd(up3, cw, cb):
    def body(up_ref, cw_ref, cb_ref, o_ref):
        o_ref[...] = _gate_fn(up_ref[0], up_ref[1], *_taps(cw_ref, 0, 3), cb_ref[0], *_taps(cw_ref, 1, 3), cb_ref[1]).astype(BF16)

    return pl.pallas_call(
        body, grid=(FFN_DIM // FFN_CT, B_LOC),
        in_specs=[pl.BlockSpec((2, SEQ, FFN_CT), lambda j, b: (0, b, j)),
                  pl.BlockSpec((2, 8, FFN_CT), lambda j, b: (0, 0, j)),
                  pl.BlockSpec((2, 1, FFN_CT), lambda j, b: (0, 0, j))],
        out_specs=pl.BlockSpec((SEQ, FFN_CT), lambda j, b: (b, j)),
        out_shape=S_((T, FFN_DIM), BF16), compiler_params=_cp(), name="convgate_fwd")(up3, cw, cb)


def _convgate_bwd(up3, cw, cb, dact):
    def body(up_ref, cw_ref, cb_ref, da_ref, dup_ref, dcw_ref):
        args = (up_ref[0], up_ref[1], *_taps(cw_ref, 0, 3), cb_ref[0], *_taps(cw_ref, 1, 3), cb_ref[1])
        _, vjp = jax.vjp(_gate_fn, *args)
        dg, dv, g0, g1, g2, gb, v0, v1, v2, vb = vjp(da_ref[...])
        dup_ref[0] = dg.astype(BF16)
        dup_ref[1] = dv.astype(BF16)
        zero = jnp.zeros((4, FFN_CT), F32)
        new = jnp.stack([jnp.concatenate([g0, g1, g2, gb, zero], axis=0), jnp.concatenate([v0, v1, v2, vb, zero], axis=0)])

        @pl.when(pl.program_id(1) == 0)
        def _():
            dcw_ref[...] = new

        @pl.when(pl.program_id(1) > 0)
        def _():
            dcw_ref[...] += new

    return pl.pallas_call(
        body, grid=(FFN_DIM // FFN_CT, B_LOC),
        in_specs=[pl.BlockSpec((2, SEQ, FFN_CT), lambda j, b: (0, b, j)),
                  pl.BlockSpec((2, 8, FFN_CT), lambda j, b: (0, 0, j)),
                  pl.BlockSpec((2, 1, FFN_CT), lambda j, b: (0, 0, j)),
                  pl.BlockSpec((SEQ, FFN_CT), lambda j, b: (b, j))],
        out_specs=[pl.BlockSpec((2, SEQ, FFN_CT), lambda j, b: (0, b, j)),
                   pl.BlockSpec((2, 8, FFN_CT), lambda j, b: (0, 0, j))],
        out_shape=[S_((2, T, FFN_DIM), BF16), S_((2, 8, FFN_DIM), F32)],
        compiler_params=_cp(), name="convgate_bwd")(up3, cw, cb, dact)


SSD_CT = 256


def _conv5_fn(x, w0, w1, w2, w3, w4, b):
    return jax.nn.silu(_dwconv(x, [w0, w1, w2, w3, w4], b))


def _ssd_pre_fwd(proj, cw, cb):
    def body(x_ref, cw_ref, cb_ref, o_ref):
        o_ref[...] = _conv5_fn(x_ref[...], *[cw_ref[k:k + 1, :] for k in range(5)], cb_ref[...])

    return pl.pallas_call(
        body, grid=(SSD_XBC // SSD_CT, B_LOC),
        in_specs=[pl.BlockSpec((SEQ, SSD_CT), lambda j, b: (b, j)),
                  pl.BlockSpec((8, SSD_CT), lambda j, b: (0, j)),
                  pl.BlockSpec((1, SSD_CT), lambda j, b: (0, j))],
        out_specs=pl.BlockSpec((SEQ, SSD_CT), lambda j, b: (b, j)),
        out_shape=S_((T, SSD_XBC), F32), compiler_params=_cp(), name="ssd_pre_fwd")(proj, cw, cb)


def _ssd_pre_bwd(proj, cw, cb, dxc):
    def body(x_ref, cw_ref, cb_ref, d_ref, dx_ref, dcw_ref):
        _, vjp = jax.vjp(_conv5_fn, x_ref[...], *[cw_ref[k:k + 1, :] for k in range(5)], cb_ref[...])
        dx, g0, g1, g2, g3, g4, gb = vjp(d_ref[...])
        dx_ref[...] = dx.astype(BF16)
        new = jnp.concatenate([g0, g1, g2, g3, g4, gb, jnp.zeros((2, SSD_CT), F32)], axis=0)

        @pl.when(pl.program_id(1) == 0)
        def _():
            dcw_ref[...] = new

        @pl.when(pl.program_id(1) > 0)
        def _():
            dcw_ref[...] += new

    return pl.pallas_call(
        body, grid=(SSD_XBC // SSD_CT, B_LOC),
        in_specs=[pl.BlockSpec((SEQ, SSD_CT), lambda j, b: (b, j)),
                  pl.BlockSpec((8, SSD_CT), lambda j, b: (0, j)),
                  pl.BlockSpec((1, SSD_CT), lambda j, b: (0, j)),
                  pl.BlockSpec((SEQ, SSD_CT), lambda j, b: (b, j))],
        out_specs=[pl.BlockSpec((SEQ, SSD_CT), lambda j, b: (b, j)),
                   pl.BlockSpec((8, SSD_CT), lambda j, b: (0, j))],
        out_shape=[S_((T, SSD_XBC), BF16), S_((8, SSD_XBC), F32)],
        compiler_params=_cp(), name="ssd_pre_bwd")(proj, cw, cb, dxc)


GROUP_W = 256
HEADS_PER_GROUP = 4


def _ssd_dt_fn(dt_raw, bias, alog):
    dt = _softplus(dt_raw + bias)
    return dt, dt * (-jnp.exp(alog))


def _ssd_chunk_fn(direction, group, xc, bc, cc, dt, da, prev):
    q = CHUNK
    ti = lax.broadcasted_iota(jnp.int32, (q, q), 0)
    si = lax.broadcasted_iota(jnp.int32, (q, q), 1)
    keep = (ti >= si) if direction == 0 else (ti <= si)
    mat = keep.astype(F32)
    acs = jnp.dot(mat, da, precision=HI, preferred_element_type=F32)
    acs_t = lax.dot_general(da, mat, (((0,), (1,)), ((), ())), precision=HI, preferred_element_type=F32)
    tot = jnp.sum(da, axis=0, keepdims=True)
    lane = lax.broadcasted_iota(jnp.int32, (1, 128), 1)
    sub = lax.broadcasted_iota(jnp.int32, (128, 1), 0)
    col = lax.broadcasted_iota(jnp.int32, (1, GROUP_W), 1) // HEAD
    cb = _dot_nt(cc, bc)
    lows, douts, dt_e, ein_e, cd_e = [], [], 0.0, 0.0, 0.0
    for h in range(HEADS_PER_GROUP):
        ln = 8 * direction + 4 * group + h
        oh_l = (lane == ln).astype(F32)
        oh_s = (sub == ln).astype(F32)
        cm = (col == h).astype(F32)
        a_col = jnp.sum(acs * oh_l, axis=1, keepdims=True)
        a_row = jnp.sum(acs_t * oh_s, axis=0, keepdims=True)
        tot_h = jnp.sum(tot * oh_l, axis=1, keepdims=True)
        dt_col = jnp.sum(dt * oh_l, axis=1, keepdims=True)
        lows.append(jnp.exp(jnp.where(keep, a_col - a_row, NEG_INF)))
        douts.append(jnp.exp(tot_h - a_col))
        dt_e = dt_e + cm * dt_col
        ein_e = ein_e + cm * jnp.exp(a_col)
        cd_e = cd_e + cm * jnp.exp(tot_h)
    xdt = xc * dt_e
    y = ein_e * _dot(cc, prev)
    st = 0.0
    for h in range(HEADS_PER_GROUP):
        cm = (col == h).astype(F32)
        y = y + cm * _dot(cb * lows[h], xdt)
        st = st + cm * _dot_tn(bc * douts[h], xdt)
    return y, prev * cd_e + st


def _ssd_post_fn(y, xc, z, d_exp, gain):
    y = (y + d_exp * xc) * jax.nn.silu(z)
    return y * lax.rsqrt(jnp.mean(y * y, axis=-1, keepdims=True) + NORM_EPS) * gain


def _chunk_rows(c):
    return pl.ds(pl.multiple_of(c * CHUNK, CHUNK), CHUNK)


def _ssd_scan_specs():
    return [pl.BlockSpec((SEQ, GROUP_W), lambda g, b: (b, g)),
            pl.BlockSpec((SEQ, 128), lambda g, b: (b, C_B // 128 + g)),
            pl.BlockSpec((SEQ, 128), lambda g, b: (b, C_C // 128 + g)),
            pl.BlockSpec((SEQ, GROUP_W), lambda g, b: (b, C_Z // GROUP_W + g)),
            pl.BlockSpec((SEQ, 128), lambda g, b: (b, C_DT // 128)),
            pl.BlockSpec((1, 128), lambda g, b: (0, 0)),
            pl.BlockSpec((1, 128), lambda g, b: (0, 0)),
            pl.BlockSpec((1, GROUP_W), lambda g, b: (0, g)),
            pl.BlockSpec((1, GROUP_W), lambda g, b: (0, g))]


def _ssd_scan_fwd(xc, proj, dtb, alog, d_exp, gain):
    def body(x_ref, b_ref, c_ref, z_ref, dt_ref, dtb_ref, al_ref, de_ref, g_ref, o_ref, dt_s, da_s, y_s):
        group = pl.program_id(0)
        dt, da = _ssd_dt_fn(dt_ref[...], dtb_ref[...], al_ref[...])
        dt_s[...] = dt
        da_s[...] = da
        for direction in (0, 1):
            def step(i, prev, direction=direction):
                rows = _chunk_rows(i if direction == 0 else N_CHUNK - 1 - i)
                y, nxt = _ssd_chunk_fn(direction, group, x_ref[rows, :], b_ref[rows, :], c_ref[rows, :], dt_s[rows, :], da_s[rows, :], prev)
                if direction == 0:
                    y_s[rows, :] = y
                else:
                    y_s[rows, :] += y
                return nxt

            lax.fori_loop(0, N_CHUNK, step, jnp.zeros((128, GROUP_W), F32))

        def post(c, carry):
            rows = _chunk_rows(c)
            o_ref[rows, :] = _ssd_post_fn(y_s[rows, :], x_ref[rows, :], z_ref[rows, :], de_ref[...], g_ref[...]).astype(BF16)
            return carry

        lax.fori_loop(0, N_CHUNK, post, 0)

    return pl.pallas_call(
        body, grid=(2, B_LOC), in_specs=_ssd_scan_specs(),
        out_specs=pl.BlockSpec((SEQ, GROUP_W), lambda g, b: (b, g)),
        out_shape=S_((T, SSD_INNER), BF16),
        scratch_shapes=[pltpu.VMEM((SEQ, 128), F32), pltpu.VMEM((SEQ, 128), F32), pltpu.VMEM((SEQ, GROUP_W), F32)],
        compiler_params=_cp(), name="ssd_scan_fwd")(xc, xc, xc, proj, proj, dtb, alog, d_exp, gain)


def _ssd_scan_bwd(xc, proj, dtb, alog, d_exp, gain, dy):
    def body(x_ref, b_ref, c_ref, z_ref, dt_ref, dtb_ref, al_ref, de_ref, g_ref, dy_ref,
             dx_ref, db_ref, dc_ref, dz_ref, ddt_ref, ddtb_ref, dal_ref, dde_ref, dg_ref,
             dt_s, da_s, y_s, st_s, ddt_s, dda_s):
        group = pl.program_id(0)
        first = pl.program_id(1) == 0
        (dt, da), dt_vjp = jax.vjp(_ssd_dt_fn, dt_ref[...], dtb_ref[...], al_ref[...])
        dt_s[...] = dt
        da_s[...] = da
        for direction in (0, 1):
            def step(i, prev, direction=direction):
                c = i if direction == 0 else N_CHUNK - 1 - i
                rows = _chunk_rows(c)
                st_s[direction * N_CHUNK + c] = prev
                y, nxt = _ssd_chunk_fn(direction, group, x_ref[rows, :], b_ref[rows, :], c_ref[rows, :], dt_s[rows, :], da_s[rows, :], prev)
                if direction == 0:
                    y_s[rows, :] = y
                else:
                    y_s[rows, :] += y
                return nxt

            lax.fori_loop(0, N_CHUNK, step, jnp.zeros((128, GROUP_W), F32))

        def post(c, carry):
            rows = _chunk_rows(c)
            _, post_vjp = jax.vjp(_ssd_post_fn, y_s[rows, :], x_ref[rows, :], z_ref[rows, :], de_ref[...], g_ref[...])
            d_y, d_x_skip, d_z, g_de, g_g = post_vjp(dy_ref[rows, :])
            dz_ref[rows, :] = d_z.astype(BF16)
            dx_ref[rows, :] = d_x_skip
            y_s[rows, :] = d_y
            return carry[0] + g_de, carry[1] + g_g

        d_de, d_g = lax.fori_loop(0, N_CHUNK, post, (jnp.zeros((1, GROUP_W), F32), jnp.zeros((1, GROUP_W), F32)))
        db_ref[...] = jnp.zeros((SEQ, 128), F32)
        dc_ref[...] = jnp.zeros((SEQ, 128), F32)
        ddt_s[...] = jnp.zeros((SEQ, 128), F32)
        dda_s[...] = jnp.zeros((SEQ, 128), F32)
        for direction in (0, 1):
            def bstep(i, dnxt, direction=direction):
                c = N_CHUNK - 1 - i if direction == 0 else i
                rows = _chunk_rows(c)
                fn = functools.partial(_ssd_chunk_fn, direction, group)
                _, vjp = jax.vjp(fn, x_ref[rows, :], b_ref[rows, :], c_ref[rows, :], dt_s[rows, :], da_s[rows, :],
                                 st_s[direction * N_CHUNK + c])
                g_x, g_b, g_c, g_dt, g_da, g_prev = vjp((y_s[rows, :], dnxt))
                dx_ref[rows, :] += g_x
                db_ref[rows, :] += g_b
                dc_ref[rows, :] += g_c
                ddt_s[rows, :] += g_dt
                dda_s[rows, :] += g_da
                return g_prev

            lax.fori_loop(0, N_CHUNK, bstep, jnp.zeros((128, GROUP_W), F32))
        g_raw, g_bias, g_alog = dt_vjp((ddt_s[...], dda_s[...]))
        ddt_ref[...] = g_raw
        pad7 = jnp.zeros((7, 128), F32)
        new_b = jnp.concatenate([g_bias, pad7], axis=0)
        new_a = jnp.concatenate([g_alog, pad7], axis=0)

        @pl.when(first)
        def _():
            ddtb_ref[...] = new_b
            dal_ref[...] = new_a
            dde_ref[...] = d_de
            dg_ref[...] = d_g

        @pl.when(jnp.logical_not(first))
        def _():
            ddtb_ref[...] += new_b
            dal_ref[...] += new_a
            dde_ref[...] += d_de
            dg_ref[...] += d_g

    return pl.pallas_call(
        body, grid=(2, B_LOC),
        in_specs=_ssd_scan_specs() + [pl.BlockSpec((SEQ, GROUP_W), lambda g, b: (b, g))],
        out_specs=[pl.BlockSpec((SEQ, GROUP_W), lambda g, b: (b, g)),
                   pl.BlockSpec((SEQ, 128), lambda g, b: (b, g)),
                   pl.BlockSpec((SEQ, 128), lambda g, b: (b, g)),
                   pl.BlockSpec((SEQ, GROUP_W), lambda g, b: (b, g)),
                   pl.BlockSpec((None, SEQ, 128), lambda g, b: (g, b, 0)),
                   pl.BlockSpec((None, 8, 128), lambda g, b: (g, 0, 0)),
                   pl.BlockSpec((None, 8, 128), lambda g, b: (g, 0, 0)),
                   pl.BlockSpec((1, GROUP_W), lambda g, b: (0, g)),
                   pl.BlockSpec((1, GROUP_W), lambda g, b: (0, g))],
        out_shape=[S_((T, SSD_INNER), F32), S_((T, 256), F32), S_((T, 256), F32), S_((T, SSD_INNER), BF16),
                   S_((2, T, 128), F32), S_((2, 8, 128), F32), S_((2, 8, 128), F32),
                   S_((1, SSD_INNER), F32), S_((1, SSD_INNER), F32)],
        scratch_shapes=[pltpu.VMEM((SEQ, 128), F32), pltpu.VMEM((SEQ, 128), F32), pltpu.VMEM((SEQ, GROUP_W), F32),
                        pltpu.VMEM((2 * N_CHUNK, 128, GROUP_W), F32), pltpu.VMEM((SEQ, 128), F32), pltpu.VMEM((SEQ, 128), F32)],
        compiler_params=_cp(), name="ssd_scan_bwd")(xc, xc, xc, proj, proj, dtb, alog, d_exp, gain, dy)


GMLP_W = 256


def _gmlp_chunk_fn(gu, gv, v_gain, w0, w1, w2, w3, b_exp):
    u = jax.nn.gelu(gu)
    v = jax.nn.gelu(gv)
    v = v * lax.rsqrt(jnp.mean(v * v, axis=-1, keepdims=True) + NORM_EPS) * v_gain
    col = lax.broadcasted_iota(jnp.int32, (1, GMLP_W), 1) // HEAD
    mixed = b_exp
    for g, w in enumerate((w0, w1, w2, w3)):
        mixed = mixed + (col == g).astype(F32) * _dot(w, v)
    return u * mixed


def _gmlp_specs():
    return [pl.BlockSpec((SEQ, GMLP_W), lambda b: (b, C_GU // GMLP_W)),
            pl.BlockSpec((SEQ, GMLP_W), lambda b: (b, C_GV // GMLP_W)),
            pl.BlockSpec((1, GMLP_W), lambda b: (0, 0)),
            pl.BlockSpec((4, CHUNK, CHUNK), lambda b: (0, 0, 0)),
            pl.BlockSpec((CHUNK, GMLP_W), lambda b: (0, 0))]


def _gmlp_fwd(proj, v_gain, w_s, b_exp):
    def body(u_ref, v_ref, g_ref, w_ref, b_ref, o_ref):
        def step(c, carry):
            rows = _chunk_rows(c)
            o_ref[rows, :] = _gmlp_chunk_fn(u_ref[rows, :], v_ref[rows, :], g_ref[...], w_ref[0], w_ref[1], w_ref[2], w_ref[3],
                                            b_ref[...]).astype(BF16)
            return carry

        lax.fori_loop(0, N_CHUNK, step, 0)

    return pl.pallas_call(body, grid=(B_LOC,), in_specs=_gmlp_specs(),
                          out_specs=pl.BlockSpec((SEQ, GMLP_W), lambda b: (b, 0)),
                          out_shape=S_((T, GMLP_W), BF16), name="gmlp_fwd")(proj, proj, v_gain, w_s, b_exp)


def _gmlp_bwd(proj, v_gain, w_s, b_exp, dy):
    def body(u_ref, v_ref, g_ref, w_ref, b_ref, dy_ref, du_ref, dv_ref, dg_ref, dw_ref, db_ref):
        @pl.when(pl.program_id(0) == 0)
        def _():
            dg_ref[...] = jnp.zeros_like(dg_ref)
            dw_ref[...] = jnp.zeros_like(dw_ref)
            db_ref[...] = jnp.zeros_like(db_ref)

        def step(c, carry):
            rows = _chunk_rows(c)
            _, vjp = jax.vjp(_gmlp_chunk_fn, u_ref[rows, :], v_ref[rows, :], g_ref[...], w_ref[0], w_ref[1], w_ref[2], w_ref[3], b_ref[...])
            g_u, g_v, g_g, g_w0, g_w1, g_w2, g_w3, g_b = vjp(dy_ref[rows, :])
            du_ref[rows, :] = g_u.astype(BF16)
            dv_ref[rows, :] = g_v.astype(BF16)
            dg_ref[...] += g_g
            db_ref[...] += g_b
            for g, gw in enumerate((g_w0, g_w1, g_w2, g_w3)):
                dw_ref[g] += gw
            return carry

        lax.fori_loop(0, N_CHUNK, step, 0)

    blk = pl.BlockSpec((SEQ, GMLP_W), lambda b: (b, 0))
    return pl.pallas_call(
        body, grid=(B_LOC,),
        in_specs=_gmlp_specs() + [pl.BlockSpec((SEQ, GMLP_W), lambda b: (b, SSD_INNER // GMLP_W))],
        out_specs=[blk, blk, pl.BlockSpec((1, GMLP_W), lambda b: (0, 0)),
                   pl.BlockSpec((4, CHUNK, CHUNK), lambda b: (0, 0, 0)), pl.BlockSpec((CHUNK, GMLP_W), lambda b: (0, 0))],
        out_shape=[S_((T, GMLP_W), BF16), S_((T, GMLP_W), BF16), S_((1, GMLP_W), F32),
                   S_((4, CHUNK, CHUNK), F32), S_((CHUNK, GMLP_W), F32)],
        name="gmlp_bwd")(proj, proj, v_gain, w_s, b_exp, dy)


PAIR_W = 128
QB = 128
KW = QB + 2 * ATTN_HALF
N_QB = SEQ // QB
PAD_ROWS = SEQ + 2 * ATTN_HALF


def _qk_norm_fn(x, gain):
    ms = jnp.dot(x * x, _head_sum_matrix(PAIR_W), precision=HI, preferred_element_type=F32) * (1.0 / HEAD)
    return x * lax.rsqrt(ms + NORM_EPS) * gain


def _deinterleave(dst_ref, src_ref, dil, offset):
    length = SEQ // dil
    if dil == 1:
        dst_ref[pl.ds(offset, SEQ), :] = src_ref[...]
        return
    for r in range(dil):
        dst_ref[pl.ds(offset + r * length, length), :] = src_ref[pl.ds(r, length, stride=dil), :]


def _interleave(dst_ref, src_ref, dil, offset):
    length = SEQ // dil
    if dil == 1:
        dst_ref[...] = src_ref[pl.ds(offset, SEQ), :]
        return
    for r in range(dil):
        dst_ref[pl.ds(r, length, stride=dil), :] = src_ref[pl.ds(offset + r * length, length), :]


def _edge_mask(blk, dil):
    length = SEQ // dil
    qi = blk * QB + lax.broadcasted_iota(jnp.int32, (QB, KW), 0)
    kj = blk * QB - ATTN_HALF + lax.broadcasted_iota(jnp.int32, (QB, KW), 1)
    return (kj >= 0) & (kj < SEQ) & ((qi // length) == (kj // length))


def _lane_is_head(hh):
    return (lax.broadcasted_iota(jnp.int32, (1, PAIR_W), 1) // HEAD) == hh


def _attn_branch_fwd(br, dil, qn_s, kn_s, v_ref, bias_ref, qd_s, kd_s, vd_s, od_s, ld_s):
    _deinterleave(qd_s, qn_s, dil, 0)
    _deinterleave(kd_s, kn_s, dil, ATTN_HALF)
    _deinterleave(vd_s, v_ref, dil, ATTN_HALF)

    def step(blk, carry):
        rows = pl.ds(pl.multiple_of(blk * QB, QB), QB)
        win = pl.ds(pl.multiple_of(blk * QB, QB), KW)
        qb, kw, vw = qd_s[rows, :], kd_s[win, :], vd_s[win, :]
        edge = _edge_mask(blk, dil)
        out, lse = 0.0, 0.0
        for hh in range(2):
            is_h = _lane_is_head(hh)
            s = _dot_nt(jnp.where(is_h, qb, 0.0), kw) * (HEAD ** -0.5) + bias_ref[br, hh]
            s = jnp.where(edge, s, NEG_INF)
            m = jnp.max(s, axis=-1, keepdims=True)
            l_h = m + jnp.log(jnp.sum(jnp.exp(s - m), axis=-1, keepdims=True))
            out = out + jnp.where(is_h, _dot(jnp.exp(s - l_h), vw), 0.0)
            lse = lse + jnp.where(is_h, l_h, 0.0)
        od_s[rows, :] = out
        ld_s[rows, :] = lse
        return carry

    lax.fori_loop(0, N_QB, step, 0)


def _attn_specs():
    col = lambda c0: (lambda p, b: (b, c0 // PAIR_W + p))
    return [pl.BlockSpec((SEQ, PAIR_W), col(C_Q)), pl.BlockSpec((SEQ, PAIR_W), col(C_K)), pl.BlockSpec((SEQ, PAIR_W), col(C_V)),
            pl.BlockSpec((1, PAIR_W), lambda p, b: (0, 0)), pl.BlockSpec((1, PAIR_W), lambda p, b: (0, 0)),
            pl.BlockSpec((3, 2, QB, KW), lambda p, b: (0, p, 0, 0))]


def _attn_scratch():
    seq = pltpu.VMEM((SEQ, PAIR_W), F32)
    pad = pltpu.VMEM((PAD_ROWS, PAIR_W), F32)
    return [seq, seq, seq, pad, pad, seq, seq]


def _zero_pads(*refs):
    for ref in refs:
        ref[pl.ds(0, ATTN_HALF), :] = jnp.zeros((ATTN_HALF, PAIR_W), F32)
        ref[pl.ds(ATTN_HALF + SEQ, ATTN_HALF), :] = jnp.zeros((ATTN_HALF, PAIR_W), F32)


ROW_STEP = 256


def _row_steps(fn, init=0):
    return lax.fori_loop(0, SEQ // ROW_STEP, lambda i, c: fn(pl.ds(pl.multiple_of(i * ROW_STEP, ROW_STEP), ROW_STEP), c), init)


def _interleave_add(acc_ref, src_ref, dil, offset):
    length = SEQ // dil
    if dil == 1:
        acc_ref[...] += src_ref[pl.ds(offset, SEQ), :]
        return
    for r in range(dil):
        acc_ref[pl.ds(r, length, stride=dil), :] += src_ref[pl.ds(offset + r * length, length), :]


def _attn_forward_all(q_ref, k_ref, v_ref, qg_ref, kg_ref, bias_ref, qn_s, kn_s, qd_s, kd_s, vd_s, od_s, ld_s, on_s, ln_s):
    def norm(rows, carry):
        qn_s[rows, :] = _qk_norm_fn(q_ref[rows, :], qg_ref[...])
        kn_s[rows, :] = _qk_norm_fn(k_ref[rows, :], kg_ref[...])
        return carry

    _row_steps(norm)
    _zero_pads(kd_s, vd_s)
    for br, dil in enumerate(ATTN_DILS):
        _attn_branch_fwd(br, dil, qn_s, kn_s, v_ref, bias_ref, qd_s, kd_s, vd_s, od_s, ld_s)
        _interleave(on_s.at[br], od_s, dil, 0)
        _interleave(ln_s.at[br], ld_s, dil, 0)


def _merge_weights(ln_s, rows):
    l0, l1, l2 = ln_s[0, rows, :], ln_s[1, rows, :], ln_s[2, rows, :]
    m = jnp.maximum(jnp.maximum(l0, l1), l2)
    e = [jnp.exp(l0 - m), jnp.exp(l1 - m), jnp.exp(l2 - m)]
    den = e[0] + e[1] + e[2]
    return [e[0] / den, e[1] / den, e[2] / den]


def _attn_fwd(proj, q_gain, k_gain, bias):
    def body(q_ref, k_ref, v_ref, qg_ref, kg_ref, bias_ref, o_ref, qn_s, kn_s, qd_s, kd_s, vd_s, od_s, ld_s, on_s, ln_s):
        _attn_forward_all(q_ref, k_ref, v_ref, qg_ref, kg_ref, bias_ref, qn_s, kn_s, qd_s, kd_s, vd_s, od_s, ld_s, on_s, ln_s)

        def merge(rows, carry):
            w = _merge_weights(ln_s, rows)
            o_ref[rows, :] = (w[0] * on_s[0, rows, :] + w[1] * on_s[1, rows, :] + w[2] * on_s[2, rows, :]).astype(BF16)
            return carry

        _row_steps(merge)

    seq3 = pltpu.VMEM((3, SEQ, PAIR_W), F32)
    return pl.pallas_call(body, grid=(2, B_LOC), in_specs=_attn_specs(),
                          out_specs=pl.BlockSpec((SEQ, PAIR_W), lambda p, b: (b, p)),
                          out_shape=S_((T, 2 * PAIR_W), BF16), scratch_shapes=_attn_scratch() + [seq3, seq3],
                          compiler_params=_cp(), name="attn_fwd")(proj, proj, proj, q_gain, k_gain, bias)


def _attn_bwd(proj, q_gain, k_gain, bias, dy):
    def body(q_ref, k_ref, v_ref, qg_ref, kg_ref, bias_ref, dy_ref,
             dq_ref, dk_ref, dv_ref, dqg_ref, dkg_ref, dbias_ref,
             qn_s, kn_s, qd_s, kd_s, vd_s, od_s, ld_s, on_s, ln_s, dod_s, dld_s, dqd_s, dkd_s, dvd_s, dqn_s, dkn_s, dvn_s):
        first = pl.program_id(1) == 0
        _attn_forward_all(q_ref, k_ref, v_ref, qg_ref, kg_ref, bias_ref, qn_s, kn_s, qd_s, kd_s, vd_s, od_s, ld_s, on_s, ln_s)

        def merge_bwd(rows, carry):
            w = _merge_weights(ln_s, rows)
            dy = dy_ref[rows, :]
            same_head = _head_sum_matrix(PAIR_W)
            dws = [jnp.dot(dy * on_s[br, rows, :], same_head, precision=HI, preferred_element_type=F32) for br in range(3)]
            dbar = w[0] * dws[0] + w[1] * dws[1] + w[2] * dws[2]
            for br in range(3):
                ln_s[br, rows, :] = w[br] * (dws[br] - dbar)
                on_s[br, rows, :] = w[br] * dy
            dqn_s[rows, :] = jnp.zeros((ROW_STEP, PAIR_W), F32)
            dkn_s[rows, :] = jnp.zeros((ROW_STEP, PAIR_W), F32)
            dvn_s[rows, :] = jnp.zeros((ROW_STEP, PAIR_W), F32)
            return carry

        _row_steps(merge_bwd)

        @pl.when(first)
        def _():
            dbias_ref[...] = jnp.zeros_like(dbias_ref)

        for br, dil in enumerate(ATTN_DILS):
            _attn_branch_fwd(br, dil, qn_s, kn_s, v_ref, bias_ref, qd_s, kd_s, vd_s, od_s, ld_s)
            _deinterleave(dod_s, on_s.at[br], dil, 0)
            _deinterleave(dld_s, ln_s.at[br], dil, 0)

            def clear(rows, carry):
                dkd_s[rows, :] = jnp.zeros((ROW_STEP, PAIR_W), F32)
                dvd_s[rows, :] = jnp.zeros((ROW_STEP, PAIR_W), F32)
                return carry

            _row_steps(clear)
            tail = pl.ds(SEQ, 2 * ATTN_HALF)
            dkd_s[tail, :] = jnp.zeros((2 * ATTN_HALF, PAIR_W), F32)
            dvd_s[tail, :] = jnp.zeros((2 * ATTN_HALF, PAIR_W), F32)

            def step(blk, carry, br=br, dil=dil):
                rows = pl.ds(pl.multiple_of(blk * QB, QB), QB)
                win = pl.ds(pl.multiple_of(blk * QB, QB), KW)
                qb, kw, vw = qd_s[rows, :], kd_s[win, :], vd_s[win, :]
                do_b, dl_b, o_b, l_b = dod_s[rows, :], dld_s[rows, :], od_s[rows, :], ld_s[rows, :]
                edge = _edge_mask(blk, dil)
                dq, dk, dv = 0.0, 0.0, 0.0
                for hh in range(2):
                    is_h = _lane_is_head(hh)
                    pick = (lax.broadcasted_iota(jnp.int32, (1, PAIR_W), 1) == hh * HEAD).astype(F32)
                    q_h = jnp.where(is_h, qb, 0.0)
                    do_h = jnp.where(is_h, do_b, 0.0)
                    s = _dot_nt(q_h, kw) * (HEAD ** -0.5) + bias_ref[br, hh]
                    s = jnp.where(edge, s, NEG_INF)
                    p = jnp.exp(s - jnp.sum(l_b * pick, axis=-1, keepdims=True))
                    dp = _dot_nt(do_h, vw)
                    delta = jnp.sum(do_h * o_b, axis=-1, keepdims=True)
                    ds = p * (dp - delta + jnp.sum(dl_b * pick, axis=-1, keepdims=True))
                    dbias_ref[br, hh] += ds
                    dq = dq + jnp.where(is_h, _dot(ds, kw), 0.0) * (HEAD ** -0.5)
                    dk = dk + _dot_tn(ds, q_h) * (HEAD ** -0.5)
                    dv = dv + _dot_tn(p, do_h)
                dqd_s[rows, :] = dq
                dkd_s[win, :] += dk
                dvd_s[win, :] += dv
                return carry

            lax.fori_loop(0, N_QB, step, 0)
            _interleave_add(dqn_s, dqd_s, dil, 0)
            _interleave_add(dkn_s, dkd_s, dil, ATTN_HALF)
            _interleave_add(dvn_s, dvd_s, dil, ATTN_HALF)

        def norm_bwd(rows, carry):
            _, q_vjp = jax.vjp(_qk_norm_fn, q_ref[rows, :], qg_ref[...])
            _, k_vjp = jax.vjp(_qk_norm_fn, k_ref[rows, :], kg_ref[...])
            g_q, g_qg = q_vjp(dqn_s[rows, :])
            g_k, g_kg = k_vjp(dkn_s[rows, :])
            dq_ref[rows, :] = g_q.astype(BF16)
            dk_ref[rows, :] = g_k.astype(BF16)
            dv_ref[rows, :] = dvn_s[rows, :].astype(BF16)
            return carry[0] + g_qg, carry[1] + g_kg

        g_qg, g_kg = _row_steps(norm_bwd, (jnp.zeros((1, PAIR_W), F32), jnp.zeros((1, PAIR_W), F32)))
        pad7 = jnp.zeros((7, PAIR_W), F32)
        new_q = jnp.concatenate([g_qg, pad7], axis=0)
        new_k = jnp.concatenate([g_kg, pad7], axis=0)

        @pl.when(first)
        def _():
            dqg_ref[...] = new_q
            dkg_ref[...] = new_k

        @pl.when(jnp.logical_not(first))
        def _():
            dqg_ref[...] += new_q
            dkg_ref[...] += new_k

    seq = pltpu.VMEM((SEQ, PAIR_W), F32)
    seq3 = pltpu.VMEM((3, SEQ, PAIR_W), F32)
    pad = pltpu.VMEM((PAD_ROWS, PAIR_W), F32)
    out_blk = pl.BlockSpec((SEQ, PAIR_W), lambda p, b: (b, p))
    gain_blk = pl.BlockSpec((None, 8, PAIR_W), lambda p, b: (p, 0, 0))
    return pl.pallas_call(
        body, grid=(2, B_LOC),
        in_specs=_attn_specs() + [pl.BlockSpec((SEQ, PAIR_W), lambda p, b: (b, (SSD_INNER + GMLP_W) // PAIR_W + p))],
        out_specs=[out_blk, out_blk, out_blk, gain_blk, gain_blk, pl.BlockSpec((3, 2, QB, KW), lambda p, b: (0, p, 0, 0))],
        out_shape=[S_((T, 2 * PAIR_W), BF16)] * 3 + [S_((2, 8, PAIR_W), F32)] * 2 + [S_((3, 4, QB, KW), F32)],
        scratch_shapes=_attn_scratch() + [seq3, seq3, seq, seq, seq, pad, pad, seq, seq, seq],
        compiler_params=_cp(), name="attn_bwd")(proj, proj, proj, q_gain, k_gain, bias, dy)


def _rel_bucket(rel):
    nb = 16
    max_exact = nb // 2
    n = jnp.abs(rel)
    large = max_exact + (jnp.log(jnp.maximum(n, 1).astype(F32) / max_exact) / math.log(1024 / max_exact) * (nb - max_exact)).astype(jnp.int32)
    large = jnp.minimum(large, nb - 1)
    return jnp.where(rel > 0, nb, 0) + jnp.where(n < max_exact, n, large)


def _attn_bias(rel_table):
    rel = jnp.arange(KW)[None, :] - ATTN_HALF - jnp.arange(QB)[:, None]
    inside = (jnp.abs(rel) <= ATTN_HALF)
    out = []
    for dil in ATTN_DILS:
        b = jnp.transpose(rel_table[_rel_bucket(rel * dil)], (2, 0, 1))
        out.append(jnp.where(inside[None], b, NEG_INF))
    return jnp.stack(out).astype(F32)


def _place():
    return lax.axis_index("x"), lax.axis_index("y"), lax.axis_index("c")


def _allgather8(buf, name):
    rows = buf.shape[0]
    flips = [(fx, fy, fc) for fx in (0, 1) for fy in (0, 1) for fc in (0, 1)][1:]

    def body(in_ref, out_ref, send_sems, recv_sems, local_sem):
        x, y, c = _place()
        me = 4 * x + 2 * y + c
        mine = pltpu.make_async_copy(in_ref, out_ref.at[me], local_sem)
        mine.start()
        peers = [(1 - x if fx else x, 1 - y if fy else y, 1 - c if fc else c) for fx, fy, fc in flips]

        def copy(k, slot, peer):
            return pltpu.make_async_remote_copy(src_ref=in_ref, dst_ref=out_ref.at[slot], send_sem=send_sems.at[k],
                                                recv_sem=recv_sems.at[k], device_id=peer, device_id_type=MESH)

        sends = [copy(k, me, peer) for k, peer in enumerate(peers)]
        for cp in sends:
            cp.start()
        for k, (px, py, pc) in enumerate(peers):
            copy(k, 4 * px + 2 * py + pc, (px, py, pc)).wait_recv()
        for cp in sends:
            cp.wait_send()
        mine.wait()

    return pl.pallas_call(body, in_specs=[ANY], out_specs=ANY, out_shape=S_((N_DEV, rows, 128), F32),
                          scratch_shapes=[pltpu.SemaphoreType.DMA((7,)), pltpu.SemaphoreType.DMA((7,)), pltpu.SemaphoreType.DMA(())],
                          name=name)(buf)


N_BIG = 4
HALF_L = DEPTH // 2


def _other_chips(x, y):
    return [(1 - x, y), (x, 1 - y), (1 - x, 1 - y)]


def _gather_weights(shards):
    def body(*refs):
        ins, outs = refs[:N_BIG], refs[N_BIG:2 * N_BIG]
        ici_send, ici_recv, d2d_send, d2d_recv, local_sems = refs[2 * N_BIG:]
        x, y, c = _place()
        chip = 2 * x + y
        mine_l = pl.ds(HALF_L * c, HALF_L)
        other_l = pl.ds(HALF_L * (1 - c), HALF_L)
        chips = _other_chips(x, y)
        local = [pltpu.make_async_copy(ins[t], outs[t].at[chip], local_sems.at[t]) for t in range(N_BIG)]
        for cp in local:
            cp.start()

        def ici(t, f, src, slot, peer):
            return pltpu.make_async_remote_copy(src_ref=src, dst_ref=outs[t].at[slot, mine_l], send_sem=ici_send.at[3 * t + f],
                                                recv_sem=ici_recv.at[3 * t + f], device_id=peer, device_id_type=MESH)

        def d2d(t, f, slot, layers):
            return pltpu.make_async_remote_copy(src_ref=outs[t].at[slot, layers], dst_ref=outs[t].at[slot, layers],
                                                send_sem=d2d_send.at[3 * t + f], recv_sem=d2d_recv.at[3 * t + f],
                                                device_id=(x, y, 1 - c), device_id_type=MESH)

        sends = []
        for t in range(N_BIG):
            for f, (px, py) in enumerate(chips):
                sends.append(ici(t, f, ins[t].at[mine_l], chip, (px, py, c)))
                sends[-1].start()
        for t in range(N_BIG):
            for f, (px, py) in enumerate(chips):
                ici(t, f, ins[t].at[mine_l], 2 * px + py, (px, py, c)).wait_recv()
                sends.append(d2d(t, f, 2 * px + py, mine_l))
                sends[-1].start()
        for t in range(N_BIG):
            for f, (px, py) in enumerate(chips):
                d2d(t, f, 2 * px + py, other_l).wait_recv()
        for cp in sends:
            cp.wait_send()
        for cp in local:
            cp.wait()

    n_sem = 3 * N_BIG
    return pl.pallas_call(
        body, in_specs=[ANY] * N_BIG, out_specs=[ANY] * N_BIG,
        out_shape=[S_((N_CHIPS,) + s.shape, s.dtype) for s in shards],
        scratch_shapes=[pltpu.SemaphoreType.DMA((n_sem,)) for _ in range(4)] + [pltpu.SemaphoreType.DMA((N_BIG,))],
        name="gather_weights")(*shards)


def _pair_exchange(parts, name):
    def body(*refs):
        ins, outs = refs[:N_BIG], refs[N_BIG:2 * N_BIG]
        send_sems, recv_sems = refs[2 * N_BIG:]
        x, y, c = _place()
        cps = [pltpu.make_async_remote_copy(src_ref=ins[t], dst_ref=outs[t], send_sem=send_sems.at[t], recv_sem=recv_sems.at[t],
                                            device_id=(x, y, 1 - c), device_id_type=MESH) for t in range(N_BIG)]
        for cp in cps:
            cp.start()
        for cp in cps:
            cp.wait_recv()
        for cp in cps:
            cp.wait_send()

    return pl.pallas_call(body, in_specs=[ANY] * N_BIG, out_specs=[ANY] * N_BIG,
                          out_shape=[S_(p.shape, p.dtype) for p in parts],
                          scratch_shapes=[pltpu.SemaphoreType.DMA((N_BIG,)), pltpu.SemaphoreType.DMA((N_BIG,))],
                          name=name)(*parts)


def _scatter_to_chips(parts):
    def body(*refs):
        ins, outs = refs[:N_BIG], refs[N_BIG:2 * N_BIG]
        send_sems, recv_sems = refs[2 * N_BIG:]
        x, y, c = _place()
        cps = []
        for t in range(N_BIG):
            for f, (px, py) in enumerate(_other_chips(x, y)):
                cps.append(pltpu.make_async_remote_copy(src_ref=ins[t].at[2 * px + py], dst_ref=outs[t].at[f],
                                                        send_sem=send_sems.at[3 * t + f], recv_sem=recv_sems.at[3 * t + f],
                                                        device_id=(px, py, c), device_id_type=MESH))
        for cp in cps:
            cp.start()
        for cp in cps:
            cp.wait_recv()
        for cp in cps:
            cp.wait_send()

    return pl.pallas_call(body, in_specs=[ANY] * N_BIG, out_specs=[ANY] * N_BIG,
                          out_shape=[S_((3,) + p.shape[1:], p.dtype) for p in parts],
                          scratch_shapes=[pltpu.SemaphoreType.DMA((3 * N_BIG,)), pltpu.SemaphoreType.DMA((3 * N_BIG,))],
                          name="scatter_to_chips")(*parts)


def _join_halves(halves):
    def body(*refs):
        ins, outs = refs[:N_BIG], refs[N_BIG:2 * N_BIG]
        send_sems, recv_sems, local_sems = refs[2 * N_BIG:]
        x, y, c = _place()
        mine_l = pl.ds(HALF_L * c, HALF_L)
        other_l = pl.ds(HALF_L * (1 - c), HALF_L)
        local = [pltpu.make_async_copy(ins[t], outs[t].at[mine_l], local_sems.at[t]) for t in range(N_BIG)]
        for cp in local:
            cp.start()
        cps = [pltpu.make_async_remote_copy(src_ref=ins[t], dst_ref=outs[t].at[mine_l], send_sem=send_sems.at[t], recv_sem=recv_sems.at[t],
                                            device_id=(x, y, 1 - c), device_id_type=MESH) for t in range(N_BIG)]
        for cp in cps:
            cp.start()
        for t in range(N_BIG):
            pltpu.make_async_remote_copy(src_ref=ins[t], dst_ref=outs[t].at[other_l], send_sem=send_sems.at[t], recv_sem=recv_sems.at[t],
                                         device_id=(x, y, 1 - c), device_id_type=MESH).wait_recv()
        for cp in cps:
            cp.wait_send()
        for cp in local:
            cp.wait()

    return pl.pallas_call(body, in_specs=[ANY] * N_BIG, out_specs=[ANY] * N_BIG,
                          out_shape=[S_((DEPTH,) + h.shape[1:], h.dtype) for h in halves],
                          scratch_shapes=[pltpu.SemaphoreType.DMA((N_BIG,)) for _ in range(3)],
                          name="join_halves")(*halves)


def _add_pair(a, b, rows, name):
    n, r, c = a.shape

    def body(a_ref, b_ref, o_ref):
        o_ref[...] = (a_ref[...].astype(F32) + b_ref[...].astype(F32)).astype(BF16)

    blk = pl.BlockSpec((None, rows, c), lambda i, j: (i, j, 0))
    return pl.pallas_call(body, grid=(n, r // rows), in_specs=[blk, blk], out_specs=blk, out_shape=S_(a.shape, BF16), name=name)(a, b)


def _add_four(own, got, rows, name):
    n, r, c = own.shape

    def body(a_ref, g_ref, o_ref):
        o_ref[...] = ((a_ref[...].astype(F32) + g_ref[0].astype(F32)) + g_ref[1].astype(F32)) + g_ref[2].astype(F32)

    blk = pl.BlockSpec((None, rows, c), lambda i, j: (i, j, 0))
    return pl.pallas_call(body, grid=(n, r // rows), in_specs=[blk, pl.BlockSpec((3, None, rows, c), lambda i, j: (0, i, j, 0))],
                          out_specs=blk, out_shape=S_(own.shape, F32), name=name)(own, got)


def _sum_slots(slots):
    rows = slots.shape[1]

    def body(s_ref, o_ref):
        tot = s_ref[0]
        for k in range(1, N_DEV):
            tot = tot + s_ref[k]
        o_ref[...] = tot

    return pl.pallas_call(body, out_shape=S_((rows, 128), F32), name="sum_slots")(slots)


SMALL = ("mix_norm_gain", "ssd_conv_w", "ssd_conv_b", "ssd_dt_bias", "ssd_a_log", "ssd_d", "ssd_out_gain", "gmlp_v_gain",
         "gmlp_w_s", "gmlp_b_s", "attn_q_gain", "attn_k_gain", "rel_bias_table", "ffn_norm_gain", "ffn_conv_w", "ffn_conv_b")
BIG = ("w_in", "w_out", "ffn_w_up", "ffn_w_down")
WEIGHTS = ("mix_norm_gain", "w_in", "ssd_conv_w", "ssd_conv_b", "ssd_dt_bias", "ssd_a_log", "ssd_d", "ssd_out_gain", "gmlp_v_gain",
           "gmlp_w_s", "gmlp_b_s", "attn_q_gain", "attn_k_gain", "rel_bias_table", "w_out", "ffn_norm_gain", "ffn_w_up",
           "ffn_conv_w", "ffn_conv_b", "ffn_w_down")
ADAM_ROWS = {"w_in": 512, "w_out": 512, "ffn_w_up": 256, "ffn_w_down": 352}
SUM_ROWS = {"w_in": 512, "w_out": 256, "ffn_w_up": 256, "ffn_w_down": 352}


PACK_ROWS = 64


def _pack(arrays):
    flat = jnp.concatenate([a.reshape(-1).astype(F32) for a in arrays])
    rows = -(-flat.shape[0] // (PACK_ROWS * 128)) * PACK_ROWS
    return jnp.pad(flat, (0, rows * 128 - flat.shape[0])).reshape(rows, 128)


def _unpack(buf, shapes):
    flat = buf.reshape(-1)
    out, pos = [], 0
    for s in shapes:
        n = int(np.prod(s))
        out.append(flat[pos:pos + n].reshape(s))
        pos += n
    return out


def _perm_cols(w):
    pad = jnp.zeros(w.shape[:-1] + (NP - IN_WIDTH,), w.dtype)
    return jnp.concatenate([w[..., :1536], w[..., 1552:], w[..., 1536:1552], pad], axis=-1)


def _unperm_cols(w):
    return jnp.concatenate([w[..., :1536], w[..., C_DT:C_DT + 16], w[..., 1536:C_DT]], axis=-1)


def _layer_params(l, p, conv5_w, conv3_w, bias):
    def make(mix_g, conv5, conv5_b, dt_bias, a_log, d_skip, out_gain, v_gain, w_s, b_s, q_gain, k_gain, ffn_g, conv3, conv3_b):
        lanes = lambda a: jnp.pad(a.reshape(1, 16), ((0, 0), (0, 112)))
        cw3 = jnp.pad(jnp.transpose(conv3.reshape(3, 2, FFN_DIM), (1, 0, 2)), ((0, 0), (0, 5), (0, 0)))
        return dict(mix_g=mix_g.reshape(1, D_MODEL), cw5=jnp.pad(conv5, ((0, 3), (0, 0))), cb5=conv5_b.reshape(1, SSD_XBC),
                    dtb=lanes(dt_bias), alog=lanes(a_log), d_exp=jnp.repeat(d_skip, HEAD).reshape(1, SSD_INNER),
                    out_gain=out_gain.reshape(1, SSD_INNER), v_gain=v_gain.reshape(1, GMLP_W), w_s=w_s,
                    b_exp=jnp.repeat(b_s.T, HEAD, axis=1), q_gain=jnp.tile(q_gain, 2).reshape(1, PAIR_W),
                    k_gain=jnp.tile(k_gain, 2).reshape(1, PAIR_W), ffn_g=ffn_g.reshape(1, D_MODEL), cw3=cw3,
                    cb3=conv3_b.reshape(2, 1, FFN_DIM))

    args = (p["mix_norm_gain"][l], conv5_w[l], p["ssd_conv_b"][l], p["ssd_dt_bias"][l], p["ssd_a_log"][l], p["ssd_d"][l],
            p["ssd_out_gain"][l], p["gmlp_v_gain"][l], p["gmlp_w_s"][l], p["gmlp_b_s"][l], p["attn_q_gain"][l], p["attn_k_gain"][l],
            p["ffn_norm_gain"][l], conv3_w[l], p["ffn_conv_b"][l])
    return jax.vjp(make, *args)


def _forward_layer(x, lp, w_in, w_out, w_up, w_down, bias):
    h = _rmsnorm_fwd(x, lp["mix_g"], "rmsnorm_fwd")
    proj = _mm_nn(h, w_in, tm=1024, tn=1024, tk=1024, out_dtype=F32, name="mm_proj")
    xc = _ssd_pre_fwd(proj, lp["cw5"], lp["cb5"])
    y_ssd = _ssd_scan_fwd(xc, proj, lp["dtb"], lp["alog"], lp["d_exp"], lp["out_gain"])
    y_gmlp = _gmlp_fwd(proj, lp["v_gain"], lp["w_s"], lp["b_exp"])
    y_attn = _attn_fwd(proj, lp["q_gain"], lp["k_gain"], bias)
    y = jnp.concatenate([y_ssd, y_gmlp, y_attn], axis=1)
    x2 = _mm_nn(y, w_out, tm=1024, tn=1024, tk=1024, out_dtype=F32, res=x, name="mm_out")
    hn = _rmsnorm_fwd(x2, lp["ffn_g"], "rmsnorm_fwd")
    up3 = _mm_up(hn, w_up)
    act = _convgate_fwd(up3, lp["cw3"], lp["cb3"])
    x3 = _mm_nn(act, w_down, tm=1024, tn=1024, tk=HALF_TILE, out_dtype=F32, res=x2, name="mm_down")
    return x3, dict(x=x, h=h, proj=proj, xc=xc, y=y, x2=x2, hn=hn, up3=up3, act=act)


def _backward_layer(dx3, sv, lp, w_in, w_out, w_up, w_down, bias):
    d_act = _mm_nt(dx3, w_down, tm=1024, tn=HALF_TILE, tk=1024, out_dtype=F32, name="mm_dact")
    dw_down = _mm_tn(sv["act"], dx3, tm=HALF_TILE, tn=1024, tk=1024, out_dtype=BF16, name="mm_dwdown")
    dup3, dcw3 = _convgate_bwd(sv["up3"], lp["cw3"], lp["cb3"], d_act)
    d_hn = _mm_dhn(dup3, w_up)
    dw_up = _mm_dwup(sv["hn"], dup3)
    dx2, d_ffn_g = _rmsnorm_bwd(sv["x2"], lp["ffn_g"], d_hn, dx3, "rmsnorm_bwd")
    d_y = _mm_nt(dx2, w_out, tm=1024, tn=1024, tk=1024, out_dtype=F32, name="mm_dy")
    dw_out = _mm_tn(sv["y"], dx2, tm=1024, tn=1024, tk=1024, out_dtype=BF16, name="mm_dwout")
    proj, xc = sv["proj"], sv["xc"]
    dxs, dbc, dcc, dz, ddt2, ddtb2, dal2, d_dexp, d_outg = _ssd_scan_bwd(xc, proj, lp["dtb"], lp["alog"], lp["d_exp"], lp["out_gain"], d_y)
    d_xbc, dcw5 = _ssd_pre_bwd(proj, lp["cw5"], lp["cb5"], jnp.concatenate([dxs, dbc, dcc], axis=1))
    d_gu, d_gv, d_vg, d_ws, d_bexp = _gmlp_bwd(proj, lp["v_gain"], lp["w_s"], lp["b_exp"], d_y)
    d_q, d_k, d_v, d_qg2, d_kg2, d_bias = _attn_bwd(proj, lp["q_gain"], lp["k_gain"], bias, d_y)
    d_dt = (ddt2[0] + ddt2[1]).astype(BF16)
    d_proj = jnp.concatenate([d_xbc, dz, d_gu, d_gv, d_q, d_k, d_v, d_dt, jnp.zeros((T, NP - C_DT - 128), BF16)], axis=1)
    d_h = _mm_nt(d_proj, w_in, tm=1024, tn=1024, tk=1024, out_dtype=F32, name="mm_dh")
    dw_in = _mm_tn(sv["h"], d_proj, tm=1024, tn=1024, tk=1024, out_dtype=BF16, name="mm_dwin")
    dx, d_mix_g = _rmsnorm_bwd(sv["x"], lp["mix_g"], d_h, dx2, "rmsnorm_bwd")
    d_lp = dict(mix_g=d_mix_g, cw5=dcw5[:8] * (jnp.arange(8) < 5)[:, None].astype(F32), cb5=dcw5[5:6],
                dtb=(ddtb2[0, :1] + ddtb2[1, :1]), alog=(dal2[0, :1] + dal2[1, :1]), d_exp=d_dexp, out_gain=d_outg,
                v_gain=d_vg, w_s=d_ws, b_exp=d_bexp, q_gain=d_qg2[0, :1] + d_qg2[1, :1], k_gain=d_kg2[0, :1] + d_kg2[1, :1],
                ffn_g=d_ffn_g, cw3=dcw3 * (jnp.arange(8) < 3)[None, :, None].astype(F32), cb3=dcw3[:, 3:4])
    return dx, (dw_in, dw_out, dw_up, dw_down), d_lp, d_bias


def _to_shard_major(name, dw):
    if name == "w_in":
        dw = _unperm_cols(dw)
    if name in ("w_in", "ffn_w_up"):
        l, r, c = dw.shape
        return jnp.transpose(dw.reshape(l, r, N_CHIPS, c // N_CHIPS), (2, 0, 1, 3))
    l, r, c = dw.shape
    return jnp.transpose(dw.reshape(l, N_CHIPS, r // N_CHIPS, c), (1, 0, 2, 3))


def _from_gathered(name, g):
    if name in ("w_in", "ffn_w_up"):
        n, l, r, c = g.shape
        w = jnp.transpose(g, (1, 2, 0, 3)).reshape(l, r, n * c)
        return _perm_cols(w) if name == "w_in" else w
    n, l, r, c = g.shape
    return jnp.transpose(g, (1, 0, 2, 3)).reshape(l, n * r, c)


def _reduce_to_chips(parts, chip, core, sum_rows=None):
    sum_rows = sum_rows or [SUM_ROWS[n] for n in BIG]
    mine = [lax.dynamic_slice_in_dim(g, HALF_L * core, HALF_L, axis=1) for g in parts]
    theirs = [lax.dynamic_slice_in_dim(g, HALF_L * (1 - core), HALF_L, axis=1) for g in parts]
    got = _pair_exchange(theirs, "pair_exchange")
    chip_sums = []
    for n, rows, a, b in zip(BIG, sum_rows, mine, got):
        shp = a.shape
        s2 = _add_pair(a.reshape(N_CHIPS * HALF_L, shp[2], shp[3]), b.reshape(N_CHIPS * HALF_L, shp[2], shp[3]), rows, "add_pair_" + n)
        chip_sums.append(s2.reshape(shp))
    landed = _scatter_to_chips(chip_sums)
    halves = []
    for n, rows, s2, g3 in zip(BIG, sum_rows, chip_sums, landed):
        own = lax.dynamic_index_in_dim(s2, chip, axis=0, keepdims=False)
        halves.append(_add_four(own, g3, rows, "add_four_" + n))
    return _join_halves(halves)


LAYER_SMALL = ("mix_norm_gain", "ssd_conv_w", "ssd_conv_b", "ssd_dt_bias", "ssd_a_log", "ssd_d", "ssd_out_gain", "gmlp_v_gain",
               "gmlp_w_s", "gmlp_b_s", "attn_q_gain", "attn_k_gain", "ffn_norm_gain", "ffn_conv_w", "ffn_conv_b")


def _local_grads(x, loss_target, p, conv5_w, conv3_w, full):
    bias, bias_vjp = jax.vjp(_attn_bias, p["rel_bias_table"])
    xt = x.reshape(T, D_MODEL)
    saved, lps, lp_vjps = [], [], []
    for l in range(DEPTH):
        lp, lp_vjp = _layer_params(l, p, conv5_w, conv3_w, bias)
        xt, sv = _forward_layer(xt, lp, full["w_in"][l], full["w_out"][l], full["ffn_w_up"][l], full["ffn_w_down"][l], bias)
        saved.append(sv)
        lps.append(lp)
        lp_vjps.append(lp_vjp)
    dxt, loss_parts = _loss_head(xt, loss_target.reshape(T, D_MODEL))
    loss_local = jnp.sum(loss_parts[::8, 0])

    big_grads = [None] * DEPTH
    small_layers = [None] * DEPTH
    d_bias_tot = jnp.zeros_like(bias)
    for l in reversed(range(DEPTH)):
        dxt, big_grads[l], d_lp, d_bias = _backward_layer(dxt, saved[l], lps[l], full["w_in"][l], full["w_out"][l],
                                                          full["ffn_w_up"][l], full["ffn_w_down"][l], bias)
        small_layers[l] = lp_vjps[l](d_lp)
        d_bias_tot = d_bias_tot + d_bias
    (d_rel_table,) = bias_vjp(d_bias_tot)
    local_small = {n: jnp.stack([small_layers[l][i] for l in range(DEPTH)]) for i, n in enumerate(LAYER_SMALL)}
    local_small["rel_bias_table"] = d_rel_table
    return dxt, loss_local, big_grads, local_small


def kernel(x, mix_norm_gain, w_in, ssd_conv_w, ssd_conv_b, ssd_dt_bias, ssd_a_log, ssd_d, ssd_out_gain, gmlp_v_gain, gmlp_w_s, gmlp_b_s, attn_q_gain, attn_k_gain, rel_bias_table, w_out, ffn_norm_gain, ffn_w_up, ffn_conv_w, ffn_conv_b, ffn_w_down, loss_target, m_mix_norm_gain, m_w_in, m_ssd_conv_w, m_ssd_conv_b, m_ssd_dt_bias, m_ssd_a_log, m_ssd_d, m_ssd_out_gain, m_gmlp_v_gain, m_gmlp_w_s, m_gmlp_b_s, m_attn_q_gain, m_attn_k_gain, m_rel_bias_table, m_w_out, m_ffn_norm_gain, m_ffn_w_up, m_ffn_conv_w, m_ffn_conv_b, m_ffn_w_down, v_mix_norm_gain, v_w_in, v_ssd_conv_w, v_ssd_conv_b, v_ssd_dt_bias, v_ssd_a_log, v_ssd_d, v_ssd_out_gain, v_gmlp_v_gain, v_gmlp_w_s, v_gmlp_b_s, v_attn_q_gain, v_attn_k_gain, v_rel_bias_table, v_w_out, v_ffn_norm_gain, v_ffn_w_up, v_ffn_conv_w, v_ffn_conv_b, v_ffn_w_down):
    env = dict(locals())
    p = {n: env[n] for n in WEIGHTS}
    chip = 2 * lax.axis_index("x") + lax.axis_index("y")
    core = lax.axis_index("c")

    conv_slots = _allgather8(_pack([ssd_conv_w, ffn_conv_w]), "allgather_conv")
    conv_parts = [_unpack(conv_slots[2 * k], [ssd_conv_w.shape, ffn_conv_w.shape]) for k in range(N_CHIPS)]
    conv5_w = jnp.concatenate([cp[0] for cp in conv_parts], axis=-1)
    conv3_w = jnp.concatenate([cp[1] for cp in conv_parts], axis=-1)
    gathered = _gather_weights([p[n].astype(BF16) for n in BIG])
    full = {n: _from_gathered(n, g) for n, g in zip(BIG, gathered)}

    dxt, loss_local, big_grads, local_small = _local_grads(x, loss_target, p, conv5_w, conv3_w, full)

    small_shapes = [local_small[n].shape for n in SMALL] + [(1,)]
    slots = _allgather8(_pack([local_small[n] for n in SMALL] + [loss_local.reshape(1)]), "allgather_small")
    summed = _unpack(_sum_slots(slots), small_shapes)
    grads = dict(zip(SMALL, summed[:-1]))
    loss = summed[-1][0]
    grads["ssd_conv_w"] = lax.dynamic_slice_in_dim(grads["ssd_conv_w"], chip * 256, 256, axis=2)
    grads["ffn_conv_w"] = lax.dynamic_slice_in_dim(grads["ffn_conv_w"], chip * (2 * FFN_DIM // N_CHIPS), 2 * FFN_DIM // N_CHIPS, axis=2)

    parts = [_to_shard_major(n, jnp.stack([big_grads[l][i] for l in range(DEPTH)])) for i, n in enumerate(BIG)]
    for n, g in zip(BIG, _reduce_to_chips(parts, chip, core)):
        grads[n] = g

    delta, new_m, new_v = {}, {}, {}
    for n in BIG:
        shp = p[n].shape
        two = lambda a: a.reshape(shp[0] * shp[1], shp[2])
        d, nm, nv = _adamw(two(p[n]), two(grads[n]), two(env["m_" + n]), two(env["v_" + n]), ADAM_ROWS[n], "adamw_" + n)
        delta[n], new_m[n], new_v[n] = d.reshape(shp), nm.reshape(shp), nv.reshape(shp)
    shapes = [p[n].shape for n in SMALL]
    d, nm, nv = _adamw(_pack([p[n] for n in SMALL]), _pack([grads[n] for n in SMALL]), _pack([env["m_" + n] for n in SMALL]),
                       _pack([env["v_" + n] for n in SMALL]), PACK_ROWS, "adamw_small")
    for n, a, b, c in zip(SMALL, _unpack(d, shapes), _unpack(nm, shapes), _unpack(nv, shapes)):
        delta[n], new_m[n], new_v[n] = a, b, c

    return (loss, dxt.reshape(B_LOC, SEQ, D_MODEL), *[grads[n] for n in WEIGHTS], *[delta[n] for n in WEIGHTS],
            *[new_m[n] for n in WEIGHTS], *[new_v[n] for n in WEIGHTS])
```

```python
import functools
import math

import jax
import jax.numpy as jnp
import numpy as np
from jax import lax
from jax.experimental import pallas as pl
from jax.experimental.pallas import tpu as pltpu

F32 = jnp.float32
BF16 = jnp.bfloat16
HI = lax.Precision.HIGHEST
MESH = pl.DeviceIdType.MESH
ANY = pl.BlockSpec(memory_space=pl.ANY)

D_MODEL = 1024
SEQ = 2048
B_LOC = 2
T = B_LOC * SEQ
DEPTH = 4
N_CHIPS = 4
N_DEV = 8
HEAD = 64
CHUNK = 128
N_CHUNK = SEQ // CHUNK
SSD_INNER = 512
SSD_XBC = 1024
FFN_DIM = 2816
IN_WIDTH = 2832
NP = 3072
C_XS, C_B, C_C, C_Z, C_GU, C_GV, C_Q, C_K, C_V, C_DT = 0, 512, 768, 1024, 1536, 1792, 2048, 2304, 2560, 2816
NORM_EPS = 1e-6
NEG_INF = -1e30
ATTN_DILS = (1, 4, 16)
ATTN_HALF = 64
ADAM_LR, ADAM_B1, ADAM_B2, ADAM_EPS, ADAM_WD, ADAM_STEP = 0.001, 0.9, 0.999, 1e-08, 0.01, 10
VMEM_LIMIT = 56 * 1024 * 1024

S_ = jax.ShapeDtypeStruct


def _cp():
    return pltpu.CompilerParams(vmem_limit_bytes=VMEM_LIMIT)


def _shift_rows(x, k):
    n = x.shape[0]
    if k == 0:
        return x
    r = pltpu.roll(x, (-k) % n, 0)
    t = lax.broadcasted_iota(jnp.int32, (n, 1), 0)
    return jnp.where((t + k >= 0) & (t + k < n), r, 0.0)


@functools.partial(jax.custom_vjp, nondiff_argnums=(1,))
def _shift(x, k):
    return _shift_rows(x, k)


def _shift_fwd(x, k):
    return _shift_rows(x, k), None


def _shift_bwd(k, _, g):
    return (_shift_rows(g, -k),)


_shift.defvjp(_shift_fwd, _shift_bwd)


def _dwconv(x, taps, bias):
    half = len(taps) // 2
    y = bias
    for k, w in enumerate(taps):
        y = y + w * _shift(x, k - half)
    return y


def _softplus(x):
    return jnp.maximum(x, 0.0) + jnp.log1p(jnp.exp(-jnp.abs(x)))


def _dot(a, b):
    return jnp.dot(a.astype(BF16), b.astype(BF16), preferred_element_type=F32)


def _dot_nt(a, b):
    return lax.dot_general(a.astype(BF16), b.astype(BF16), (((1,), (1,)), ((), ())), preferred_element_type=F32)


def _dot_tn(a, b):
    return lax.dot_general(a.astype(BF16), b.astype(BF16), (((0,), (0,)), ((), ())), preferred_element_type=F32)


def _head_sum_matrix(width):
    i = lax.broadcasted_iota(jnp.int32, (width, width), 0) // HEAD
    j = lax.broadcasted_iota(jnp.int32, (width, width), 1) // HEAD
    return (i == j).astype(F32)


def _matmul(a, b, *, dims, grid, a_spec, b_spec, o_spec, out_shape, acc_shape, res=None, res_spec=None, name):
    nk = grid[2]

    def body(*refs):
        if res is not None:
            a_ref, b_ref, r_ref, o_ref = refs[:4]
        else:
            a_ref, b_ref, o_ref = refs[:3]
            r_ref = None
        part = lax.dot_general(a_ref[...].astype(BF16), b_ref[...].astype(BF16), dims, preferred_element_type=F32)
        if nk == 1:
            if r_ref is not None:
                part = part + r_ref[...]
            o_ref[...] = part.astype(o_ref.dtype)
            return
        acc_ref = refs[-1]
        k = pl.program_id(2)

        @pl.when(k == 0)
        def _():
            acc_ref[...] = part

        @pl.when(k > 0)
        def _():
            acc_ref[...] += part

        @pl.when(k == nk - 1)
        def _():
            tot = acc_ref[...]
            if r_ref is not None:
                tot = tot + r_ref[...]
            o_ref[...] = tot.astype(o_ref.dtype)

    in_specs = [a_spec, b_spec] + ([res_spec] if res is not None else [])
    args = (a, b) + ((res,) if res is not None else ())
    scratch = [] if nk == 1 else [pltpu.VMEM(acc_shape, F32)]
    return pl.pallas_call(body, grid=grid, in_specs=in_specs, out_specs=o_spec, out_shape=out_shape,
                          scratch_shapes=scratch, compiler_params=_cp(), name=name)(*args)


NN = (((1,), (0,)), ((), ()))
NT = (((1,), (1,)), ((), ()))
TN = (((0,), (0,)), ((), ()))


def _mm_nn(a, b, *, tm, tn, tk, out_dtype, res=None, name):
    m, k = a.shape
    n = b.shape[1]
    return _matmul(a, b, dims=NN, grid=(m // tm, n // tn, k // tk),
                   a_spec=pl.BlockSpec((tm, tk), lambda i, j, q: (i, q)),
                   b_spec=pl.BlockSpec((tk, tn), lambda i, j, q: (q, j)),
                   o_spec=pl.BlockSpec((tm, tn), lambda i, j, q: (i, j)),
                   out_shape=S_((m, n), out_dtype), acc_shape=(tm, tn), res=res,
                   res_spec=pl.BlockSpec((tm, tn), lambda i, j, q: (i, j)), name=name)


def _mm_nt(a, b, *, tm, tn, tk, out_dtype, name):
    m, k = a.shape
    n = b.shape[0]
    return _matmul(a, b, dims=NT, grid=(m // tm, n // tn, k // tk),
                   a_spec=pl.BlockSpec((tm, tk), lambda i, j, q: (i, q)),
                   b_spec=pl.BlockSpec((tn, tk), lambda i, j, q: (j, q)),
                   o_spec=pl.BlockSpec((tm, tn), lambda i, j, q: (i, j)),
                   out_shape=S_((m, n), out_dtype), acc_shape=(tm, tn), name=name)


def _mm_tn(a, b, *, tm, tn, tk, out_dtype, name):
    k, m = a.shape
    n = b.shape[1]
    return _matmul(a, b, dims=TN, grid=(m // tm, n // tn, k // tk),
                   a_spec=pl.BlockSpec((tk, tm), lambda i, j, q: (q, i)),
                   b_spec=pl.BlockSpec((tk, tn), lambda i, j, q: (q, j)),
                   o_spec=pl.BlockSpec((tm, tn), lambda i, j, q: (i, j)),
                   out_shape=S_((m, n), out_dtype), acc_shape=(tm, tn), name=name)


HALF_TILE = FFN_DIM // 2


def _mm_up(hn, w_up):
    return _matmul(hn, w_up, dims=NN, grid=(T // 1024, 4, 1),
                   a_spec=pl.BlockSpec((1024, D_MODEL), lambda i, j, q: (i, 0)),
                   b_spec=pl.BlockSpec((D_MODEL, HALF_TILE), lambda i, j, q: (0, j)),
                   o_spec=pl.BlockSpec((None, 1024, HALF_TILE), lambda i, j, q: (j // 2, i, j % 2)),
                   out_shape=S_((2, T, FFN_DIM), F32), acc_shape=(1024, HALF_TILE), name="mm_up")


def _mm_dhn(dup3, w_up):
    return _matmul(dup3, w_up, dims=NT, grid=(T // 1024, 1, 4),
                   a_spec=pl.BlockSpec((None, 1024, HALF_TILE), lambda i, j, q: (q // 2, i, q % 2)),
                   b_spec=pl.BlockSpec((D_MODEL, HALF_TILE), lambda i, j, q: (0, q)),
                   o_spec=pl.BlockSpec((1024, D_MODEL), lambda i, j, q: (i, 0)),
                   out_shape=S_((T, D_MODEL), F32), acc_shape=(1024, D_MODEL), name="mm_dhn")


def _mm_dwup(hn, dup3):
    return _matmul(hn, dup3, dims=TN, grid=(1, 4, T // 1024),
                   a_spec=pl.BlockSpec((1024, D_MODEL), lambda i, j, q: (q, 0)),
                   b_spec=pl.BlockSpec((None, 1024, HALF_TILE), lambda i, j, q: (j // 2, q, j % 2)),
                   o_spec=pl.BlockSpec((D_MODEL, HALF_TILE), lambda i, j, q: (0, j)),
                   out_shape=S_((D_MODEL, 2 * FFN_DIM), BF16), acc_shape=(D_MODEL, HALF_TILE), name="mm_dwup")


ROWS = 512


def _rmsnorm_fwd(x, gain, name):
    def body(x_ref, g_ref, o_ref):
        xv = x_ref[...]
        r = lax.rsqrt(jnp.mean(xv * xv, axis=-1, keepdims=True) + NORM_EPS)
        o_ref[...] = (xv * r * g_ref[...]).astype(BF16)

    return pl.pallas_call(body, grid=(T // ROWS,),
                          in_specs=[pl.BlockSpec((ROWS, D_MODEL), lambda i: (i, 0)), pl.BlockSpec((1, D_MODEL), lambda i: (0, 0))],
                          out_specs=pl.BlockSpec((ROWS, D_MODEL), lambda i: (i, 0)),
                          out_shape=S_((T, D_MODEL), BF16), name=name)(x, gain)


def _rmsnorm_bwd(x, gain, dh, dres, name):
    def body(x_ref, g_ref, dh_ref, dres_ref, dx_ref, dg_ref):
        xv = x_ref[...]
        r = lax.rsqrt(jnp.mean(xv * xv, axis=-1, keepdims=True) + NORM_EPS)
        gd = dh_ref[...] * g_ref[...]
        dot = jnp.mean(gd * xv, axis=-1, keepdims=True)
        dx_ref[...] = dres_ref[...] + r * gd - xv * (r * r * r * dot)
        part = jnp.sum(dh_ref[...] * xv * r, axis=0, keepdims=True)

        @pl.when(pl.program_id(0) == 0)
        def _():
            dg_ref[...] = part

        @pl.when(pl.program_id(0) > 0)
        def _():
            dg_ref[...] += part

    row = pl.BlockSpec((ROWS, D_MODEL), lambda i: (i, 0))
    vec = pl.BlockSpec((1, D_MODEL), lambda i: (0, 0))
    return pl.pallas_call(body, grid=(T // ROWS,), in_specs=[row, vec, row, row], out_specs=[row, vec],
                          out_shape=[S_((T, D_MODEL), F32), S_((1, D_MODEL), F32)], name=name)(x, gain, dh, dres)


def _loss_head(y, target):
    def body(y_ref, t_ref, dy_ref, p_ref):
        e = y_ref[...] - t_ref[...]
        dy_ref[...] = e * (1.0 / D_MODEL)
        p_ref[...] = jnp.full((8, 128), 0.5 / D_MODEL, F32) * jnp.sum(e * e)

    row = pl.BlockSpec((ROWS, D_MODEL), lambda i: (i, 0))
    return pl.pallas_call(body, grid=(T // ROWS,), in_specs=[row, row],
                          out_specs=[row, pl.BlockSpec((8, 128), lambda i: (i, 0))],
                          out_shape=[S_((T, D_MODEL), F32), S_((T // ROWS * 8, 128), F32)], name="loss_head")(y, target)


def _adamw(w, g, m, v, rows, name):
    r_tot, c = w.shape

    def body(w_ref, g_ref, m_ref, v_ref, d_ref, nm_ref, nv_ref):
        gv = g_ref[...]
        nm = ADAM_B1 * m_ref[...] + (1.0 - ADAM_B1) * gv
        nv = ADAM_B2 * v_ref[...] + (1.0 - ADAM_B2) * (gv * gv)
        m_hat = nm / (1.0 - ADAM_B1 ** ADAM_STEP)
        v_hat = nv / (1.0 - ADAM_B2 ** ADAM_STEP)
        d_ref[...] = -ADAM_LR * (m_hat / (jnp.sqrt(v_hat) + ADAM_EPS) + ADAM_WD * w_ref[...])
        nm_ref[...] = nm
        nv_ref[...] = nv

    blk = pl.BlockSpec((rows, c), lambda i: (i, 0))
    out = S_((r_tot, c), F32)
    return pl.pallas_call(body, grid=(r_tot // rows,), in_specs=[blk] * 4, out_specs=[blk] * 3,
                          out_shape=[out, out, out], name=name)(w, g, m, v)


FFN_CT = 256


def _gate_fn(up_g, up_v, wg0, wg1, wg2, bg, wv0, wv1, wv2, bv):
    gate = _dwconv(up_g, [wg0, wg1, wg2], bg)
    val = _dwconv(up_v, [wv0, wv1, wv2], bv)
    return jax.nn.silu(gate) * val


def _taps(ref, part, n):
    return [ref[part, k:k + 1, :] for k in range(n)]


def _convgate_fwd(up3, cw, cb):
    def body(up_ref, cw_ref, cb_ref, o_ref):
        o_ref[...] = _gate_fn(up_ref[0], up_ref[1], *_taps(cw_ref, 0, 3), cb_ref[0], *_taps(cw_ref, 1, 3), cb_ref[1]).astype(BF16)

    return pl.pallas_call(
        body, grid=(FFN_DIM // FFN_CT, B_LOC),
        in_specs=[pl.BlockSpec((2, SEQ, FFN_CT), lambda j, b: (0, b, j)),
                  pl.BlockSpec((2, 8, FFN_CT), lambda j, b: (0, 0, j)),
                  pl.BlockSpec((2, 1, FFN_CT), lambda j, b: (0, 0, j))],
        out_specs=pl.BlockSpec((SEQ, FFN_CT), lambda j, b: (b, j)),
        out_shape=S_((T, FFN_DIM), BF16), compiler_params=_cp(), name="convgate_fwd")(up3, cw, cb)


def _convgate_bwd(up3, cw, cb, dact):
    def body(up_ref, cw_ref, cb_ref, da_ref, dup_ref, dcw_ref):
        args = (up_ref[0], up_ref[1], *_taps(cw_ref, 0, 3), cb_ref[0], *_taps(cw_ref, 1, 3), cb_ref[1])
        _, vjp = jax.vjp(_gate_fn, *args)
        dg, dv, g0, g1, g2, gb, v0, v1, v2, vb = vjp(da_ref[...])
        dup_ref[0] = dg.astype(BF16)
        dup_ref[1] = dv.astype(BF16)
        zero = jnp.zeros((4, FFN_CT), F32)
        new = jnp.stack([jnp.concatenate([g0, g1, g2, gb, zero], axis=0), jnp.concatenate([v0, v1, v2, vb, zero], axis=0)])

        @pl.when(pl.program_id(1) == 0)
        def _():
            dcw_ref[...] = new

        @pl.when(pl.program_id(1) > 0)
        def _():
            dcw_ref[...] += new

    return pl.pallas_call(
        body, grid=(FFN_DIM // FFN_CT, B_LOC),
        in_specs=[pl.BlockSpec((2, SEQ, FFN_CT), lambda j, b: (0, b, j)),
                  pl.BlockSpec((2, 8, FFN_CT), lambda j, b: (0, 0, j)),
                  pl.BlockSpec((2, 1, FFN_CT), lambda j, b: (0, 0, j)),
                  pl.BlockSpec((SEQ, FFN_CT), lambda j, b: (b, j))],
        out_specs=[pl.BlockSpec((2, SEQ, FFN_CT), lambda j, b: (0, b, j)),
                   pl.BlockSpec((2, 8, FFN_CT), lambda j, b: (0, 0, j))],
        out_shape=[S_((2, T, FFN_DIM), BF16), S_((2, 8, FFN_DIM), F32)],
        compiler_params=_cp(), name="convgate_bwd")(up3, cw, cb, dact)


SSD_CT = 256


def _conv5_fn(x, w0, w1, w2, w3, w4, b):
    return jax.nn.silu(_dwconv(x, [w0, w1, w2, w3, w4], b))


def _ssd_pre_fwd(proj, cw, cb):
    def body(x_ref, cw_ref, cb_ref, o_ref):
        o_ref[...] = _conv5_fn(x_ref[...], *[cw_ref[k:k + 1, :] for k in range(5)], cb_ref[...])

    return pl.pallas_call(
        body, grid=(SSD_XBC // SSD_CT, B_LOC),
        in_specs=[pl.BlockSpec((SEQ, SSD_CT), lambda j, b: (b, j)),
                  pl.BlockSpec((8, SSD_CT), lambda j, b: (0, j)),
                  pl.BlockSpec((1, SSD_CT), lambda j, b: (0, j))],
        out_specs=pl.BlockSpec((SEQ, SSD_CT), lambda j, b: (b, j)),
        out_shape=S_((T, SSD_XBC), F32), compiler_params=_cp(), name="ssd_pre_fwd")(proj, cw, cb)


def _ssd_pre_bwd(proj, cw, cb, dxc):
    def body(x_ref, cw_ref, cb_ref, d_ref, dx_ref, dcw_ref):
        _, vjp = jax.vjp(_conv5_fn, x_ref[...], *[cw_ref[k:k + 1, :] for k in range(5)], cb_ref[...])
        dx, g0, g1, g2, g3, g4, gb = vjp(d_ref[...])
        dx_ref[...] = dx.astype(BF16)
        new = jnp.concatenate([g0, g1, g2, g3, g4, gb, jnp.zeros((2, SSD_CT), F32)], axis=0)

        @pl.when(pl.program_id(1) == 0)
        def _():
            dcw_ref[...] = new

        @pl.when(pl.program_id(1) > 0)
        def _():
            dcw_ref[...] += new

    return pl.pallas_call(
        body, grid=(SSD_XBC // SSD_CT, B_LOC),
        in_specs=[pl.BlockSpec((SEQ, SSD_CT), lambda j, b: (b, j)),
                  pl.BlockSpec((8, SSD_CT), lambda j, b: (0, j)),
                  pl.BlockSpec((1, SSD_CT), lambda j, b: (0, j)),
                  pl.BlockSpec((SEQ, SSD_CT), lambda j, b: (b, j))],
        out_specs=[pl.BlockSpec((SEQ, SSD_CT), lambda j, b: (b, j)),
                   pl.BlockSpec((8, SSD_CT), lambda j, b: (0, j))],
        out_shape=[S_((T, SSD_XBC), BF16), S_((8, SSD_XBC), F32)],
        compiler_params=_cp(), name="ssd_pre_bwd")(proj, cw, cb, dxc)


GROUP_W = 256
ONE_BUFFER = dict(pipeline_mode=pl.Buffered(1))
HEADS_PER_GROUP = 4


def _ssd_dt_fn(dt_raw, bias, alog):
    dt = _softplus(dt_raw + bias)
    return dt, dt * (-jnp.exp(alog))


def _ssd_chunk_fn(direction, group, xc, bc, cc, dt, da, prev):
    q = CHUNK
    ti = lax.broadcasted_iota(jnp.int32, (q, q), 0)
    si = lax.broadcasted_iota(jnp.int32, (q, q), 1)
    keep = (ti >= si) if direction == 0 else (ti <= si)
    mat = keep.astype(F32)
    acs = jnp.dot(mat, da, precision=HI, preferred_element_type=F32)
    acs_t = lax.dot_general(da, mat, (((0,), (1,)), ((), ())), precision=HI, preferred_element_type=F32)
    tot = jnp.sum(da, axis=0, keepdims=True)
    lane = lax.broadcasted_iota(jnp.int32, (1, 128), 1)
    sub = lax.broadcasted_iota(jnp.int32, (128, 1), 0)
    col = lax.broadcasted_iota(jnp.int32, (1, GROUP_W), 1) // HEAD
    cb = _dot_nt(cc, bc)
    lows, douts, dt_e, ein_e, cd_e = [], [], 0.0, 0.0, 0.0
    for h in range(HEADS_PER_GROUP):
        ln = 8 * direction + 4 * group + h
        oh_l = (lane == ln).astype(F32)
        oh_s = (sub == ln).astype(F32)
        cm = (col == h).astype(F32)
        a_col = jnp.sum(acs * oh_l, axis=1, keepdims=True)
        a_row = jnp.sum(acs_t * oh_s, axis=0, keepdims=True)
        tot_h = jnp.sum(tot * oh_l, axis=1, keepdims=True)
        dt_col = jnp.sum(dt * oh_l, axis=1, keepdims=True)
        lows.append(jnp.exp(jnp.where(keep, a_col - a_row, NEG_INF)))
        douts.append(jnp.exp(tot_h - a_col))
        dt_e = dt_e + cm * dt_col
        ein_e = ein_e + cm * jnp.exp(a_col)
        cd_e = cd_e + cm * jnp.exp(tot_h)
    xdt = xc * dt_e
    y = ein_e * _dot(cc, prev)
    st = 0.0
    for h in range(HEADS_PER_GROUP):
        cm = (col == h).astype(F32)
        y = y + cm * _dot(cb * lows[h], xdt)
        st = st + cm * _dot_tn(bc * douts[h], xdt)
    return y, prev * cd_e + st


def _ssd_post_fn(y, xc, z, d_exp, gain):
    y = (y + d_exp * xc) * jax.nn.silu(z)
    return y * lax.rsqrt(jnp.mean(y * y, axis=-1, keepdims=True) + NORM_EPS) * gain


def _chunk_rows(c):
    return pl.ds(pl.multiple_of(c * CHUNK, CHUNK), CHUNK)


def _ssd_scan_specs(**mode):
    return [pl.BlockSpec((SEQ, GROUP_W), lambda g, b: (b, g), **mode),
            pl.BlockSpec((SEQ, 128), lambda g, b: (b, C_B // 128 + g), **mode),
            pl.BlockSpec((SEQ, 128), lambda g, b: (b, C_C // 128 + g), **mode),
            pl.BlockSpec((SEQ, GROUP_W), lambda g, b: (b, C_Z // GROUP_W + g), **mode),
            pl.BlockSpec((SEQ, 128), lambda g, b: (b, C_DT // 128), **mode),
            pl.BlockSpec((1, 128), lambda g, b: (0, 0)),
            pl.BlockSpec((1, 128), lambda g, b: (0, 0)),
            pl.BlockSpec((1, GROUP_W), lambda g, b: (0, g)),
            pl.BlockSpec((1, GROUP_W), lambda g, b: (0, g))]


def _ssd_state_spec(**mode):
    return pl.BlockSpec((None, None, 2 * N_CHUNK, 128, GROUP_W), lambda g, b: (g, b, 0, 0, 0), **mode)


def _ssd_scan_fwd(xc, proj, dtb, alog, d_exp, gain):
    def body(x_ref, b_ref, c_ref, z_ref, dt_ref, dtb_ref, al_ref, de_ref, g_ref, o_ref, y_s, st_ref, dt_s, da_s):
        group = pl.program_id(0)
        dt, da = _ssd_dt_fn(dt_ref[...], dtb_ref[...], al_ref[...])
        dt_s[...] = dt
        da_s[...] = da
        for direction in (0, 1):
            def step(i, prev, direction=direction):
                c = i if direction == 0 else N_CHUNK - 1 - i
                rows = _chunk_rows(c)
                st_ref[direction * N_CHUNK + c] = prev
                y, nxt = _ssd_chunk_fn(direction, group, x_ref[rows, :], b_ref[rows, :], c_ref[rows, :], dt_s[rows, :], da_s[rows, :], prev)
                if direction == 0:
                    y_s[rows, :] = y
                else:
                    y_s[rows, :] += y
                return nxt

            lax.fori_loop(0, N_CHUNK, step, jnp.zeros((128, GROUP_W), F32))

        def post(c, carry):
            rows = _chunk_rows(c)
            o_ref[rows, :] = _ssd_post_fn(y_s[rows, :], x_ref[rows, :], z_ref[rows, :], de_ref[...], g_ref[...]).astype(BF16)
            return carry

        lax.fori_loop(0, N_CHUNK, post, 0)

    return pl.pallas_call(
        body, grid=(2, B_LOC), in_specs=_ssd_scan_specs(),
        out_specs=[pl.BlockSpec((SEQ, GROUP_W), lambda g, b: (b, g)), pl.BlockSpec((SEQ, GROUP_W), lambda g, b: (b, g)), _ssd_state_spec()],
        out_shape=[S_((T, SSD_INNER), BF16), S_((T, SSD_INNER), F32), S_((2, B_LOC, 2 * N_CHUNK, 128, GROUP_W), F32)],
        scratch_shapes=[pltpu.VMEM((SEQ, 128), F32), pltpu.VMEM((SEQ, 128), F32)],
        compiler_params=_cp(), name="ssd_scan_fwd")(xc, xc, xc, proj, proj, dtb, alog, d_exp, gain)


def _ssd_scan_bwd(xc, proj, dtb, alog, d_exp, gain, dy, ysum, states):
    def body(x_ref, b_ref, c_ref, z_ref, dt_ref, dtb_ref, al_ref, de_ref, g_ref, dy_ref, ys_ref, st_s,
             dx_ref, db_ref, dc_ref, dz_ref, ddt_ref, ddtb_ref, dal_ref, dde_ref, dg_ref,
             dt_s, da_s, y_s, ddt_s, dda_s):
        group = pl.program_id(0)
        first = pl.program_id(1) == 0
        (dt, da), dt_vjp = jax.vjp(_ssd_dt_fn, dt_ref[...], dtb_ref[...], al_ref[...])
        dt_s[...] = dt
        da_s[...] = da

        def post(c, carry):
            rows = _chunk_rows(c)
            _, post_vjp = jax.vjp(_ssd_post_fn, ys_ref[rows, :], x_ref[rows, :], z_ref[rows, :], de_ref[...], g_ref[...])
            d_y, d_x_skip, d_z, g_de, g_g = post_vjp(dy_ref[rows, :])
            dz_ref[rows, :] = d_z.astype(BF16)
            dx_ref[rows, :] = d_x_skip
            y_s[rows, :] = d_y
            return carry[0] + g_de, carry[1] + g_g

        d_de, d_g = lax.fori_loop(0, N_CHUNK, post, (jnp.zeros((1, GROUP_W), F32), jnp.zeros((1, GROUP_W), F32)))
        db_ref[...] = jnp.zeros((SEQ, 128), F32)
        dc_ref[...] = jnp.zeros((SEQ, 128), F32)
        ddt_s[...] = jnp.zeros((SEQ, 128), F32)
        dda_s[...] = jnp.zeros((SEQ, 128), F32)
        for direction in (0, 1):
            def bstep(i, dnxt, direction=direction):
                c = N_CHUNK - 1 - i if direction == 0 else i
                rows = _chunk_rows(c)
                fn = functools.partial(_ssd_chunk_fn, direction, group)
                _, vjp = jax.vjp(fn, x_ref[rows, :], b_ref[rows, :], c_ref[rows, :], dt_s[rows, :], da_s[rows, :],
                                 st_s[direction * N_CHUNK + c])
                g_x, g_b, g_c, g_dt, g_da, g_prev = vjp((y_s[rows, :], dnxt))
                dx_ref[rows, :] += g_x
                db_ref[rows, :] += g_b
                dc_ref[rows, :] += g_c
                ddt_s[rows, :] += g_dt
                dda_s[rows, :] += g_da
                return g_prev

            lax.fori_loop(0, N_CHUNK, bstep, jnp.zeros((128, GROUP_W), F32))
        g_raw, g_bias, g_alog = dt_vjp((ddt_s[...], dda_s[...]))
        ddt_ref[...] = g_raw
        pad7 = jnp.zeros((7, 128), F32)
        new_b = jnp.concatenate([g_bias, pad7], axis=0)
        new_a = jnp.concatenate([g_alog, pad7], axis=0)

        @pl.when(first)
        def _():
            ddtb_ref[...] = new_b
            dal_ref[...] = new_a
            dde_ref[...] = d_de
            dg_ref[...] = d_g

        @pl.when(jnp.logical_not(first))
        def _():
            ddtb_ref[...] += new_b
            dal_ref[...] += new_a
            dde_ref[...] += d_de
            dg_ref[...] += d_g

    return pl.pallas_call(
        body, grid=(2, B_LOC),
        in_specs=_ssd_scan_specs(**ONE_BUFFER) + [pl.BlockSpec((SEQ, GROUP_W), lambda g, b: (b, g), **ONE_BUFFER),
                                                  pl.BlockSpec((SEQ, GROUP_W), lambda g, b: (b, g), **ONE_BUFFER),
                                                  _ssd_state_spec(**ONE_BUFFER)],
        out_specs=[pl.BlockSpec((SEQ, GROUP_W), lambda g, b: (b, g)),
                   pl.BlockSpec((SEQ, 128), lambda g, b: (b, g)),
                   pl.BlockSpec((SEQ, 128), lambda g, b: (b, g)),
                   pl.BlockSpec((SEQ, GROUP_W), lambda g, b: (b, g)),
                   pl.BlockSpec((None, SEQ, 128), lambda g, b: (g, b, 0)),
                   pl.BlockSpec((None, 8, 128), lambda g, b: (g, 0, 0)),
                   pl.BlockSpec((None, 8, 128), lambda g, b: (g, 0, 0)),
                   pl.BlockSpec((1, GROUP_W), lambda g, b: (0, g)),
                   pl.BlockSpec((1, GROUP_W), lambda g, b: (0, g))],
        out_shape=[S_((T, SSD_INNER), F32), S_((T, 256), F32), S_((T, 256), F32), S_((T, SSD_INNER), BF16),
                   S_((2, T, 128), F32), S_((2, 8, 128), F32), S_((2, 8, 128), F32),
                   S_((1, SSD_INNER), F32), S_((1, SSD_INNER), F32)],
        scratch_shapes=[pltpu.VMEM((SEQ, 128), F32), pltpu.VMEM((SEQ, 128), F32), pltpu.VMEM((SEQ, GROUP_W), F32),
                        pltpu.VMEM((SEQ, 128), F32), pltpu.VMEM((SEQ, 128), F32)],
        compiler_params=_cp(), name="ssd_scan_bwd")(xc, xc, xc, proj, proj, dtb, alog, d_exp, gain, dy, ysum, states)


GMLP_W = 256


def _gmlp_chunk_fn(gu, gv, v_gain, w0, w1, w2, w3, b_exp):
    u = jax.nn.gelu(gu)
    v = jax.nn.gelu(gv)
    v = v * lax.rsqrt(jnp.mean(v * v, axis=-1, keepdims=True) + NORM_EPS) * v_gain
    col = lax.broadcasted_iota(jnp.int32, (1, GMLP_W), 1) // HEAD
    mixed = b_exp
    for g, w in enumerate((w0, w1, w2, w3)):
        mixed = mixed + (col == g).astype(F32) * _dot(w, v)
    return u * mixed


def _gmlp_specs():
    return [pl.BlockSpec((SEQ, GMLP_W), lambda b: (b, C_GU // GMLP_W)),
            pl.BlockSpec((SEQ, GMLP_W), lambda b: (b, C_GV // GMLP_W)),
            pl.BlockSpec((1, GMLP_W), lambda b: (0, 0)),
            pl.BlockSpec((4, CHUNK, CHUNK), lambda b: (0, 0, 0)),
            pl.BlockSpec((CHUNK, GMLP_W), lambda b: (0, 0))]


def _gmlp_fwd(proj, v_gain, w_s, b_exp):
    def body(u_ref, v_ref, g_ref, w_ref, b_ref, o_ref):
        def step(c, carry):
            rows = _chunk_rows(c)
            o_ref[rows, :] = _gmlp_chunk_fn(u_ref[rows, :], v_ref[rows, :], g_ref[...], w_ref[0], w_ref[1], w_ref[2], w_ref[3],
                                            b_ref[...]).astype(BF16)
            return carry

        lax.fori_loop(0, N_CHUNK, step, 0)

    return pl.pallas_call(body, grid=(B_LOC,), in_specs=_gmlp_specs(),
                          out_specs=pl.BlockSpec((SEQ, GMLP_W), lambda b: (b, 0)),
                          out_shape=S_((T, GMLP_W), BF16), name="gmlp_fwd")(proj, proj, v_gain, w_s, b_exp)


def _gmlp_bwd(proj, v_gain, w_s, b_exp, dy):
    def body(u_ref, v_ref, g_ref, w_ref, b_ref, dy_ref, du_ref, dv_ref, dg_ref, dw_ref, db_ref):
        @pl.when(pl.program_id(0) == 0)
        def _():
            dg_ref[...] = jnp.zeros_like(dg_ref)
            dw_ref[...] = jnp.zeros_like(dw_ref)
            db_ref[...] = jnp.zeros_like(db_ref)

        def step(c, carry):
            rows = _chunk_rows(c)
            _, vjp = jax.vjp(_gmlp_chunk_fn, u_ref[rows, :], v_ref[rows, :], g_ref[...], w_ref[0], w_ref[1], w_ref[2], w_ref[3], b_ref[...])
            g_u, g_v, g_g, g_w0, g_w1, g_w2, g_w3, g_b = vjp(dy_ref[rows, :])
            du_ref[rows, :] = g_u.astype(BF16)
            dv_ref[rows, :] = g_v.astype(BF16)
            dg_ref[...] += g_g
            db_ref[...] += g_b
            for g, gw in enumerate((g_w0, g_w1, g_w2, g_w3)):
                dw_ref[g] += gw
            return carry

        lax.fori_loop(0, N_CHUNK, step, 0)

    blk = pl.BlockSpec((SEQ, GMLP_W), lambda b: (b, 0))
    return pl.pallas_call(
        body, grid=(B_LOC,),
        in_specs=_gmlp_specs() + [pl.BlockSpec((SEQ, GMLP_W), lambda b: (b, SSD_INNER // GMLP_W))],
        out_specs=[blk, blk, pl.BlockSpec((1, GMLP_W), lambda b: (0, 0)),
                   pl.BlockSpec((4, CHUNK, CHUNK), lambda b: (0, 0, 0)), pl.BlockSpec((CHUNK, GMLP_W), lambda b: (0, 0))],
        out_shape=[S_((T, GMLP_W), BF16), S_((T, GMLP_W), BF16), S_((1, GMLP_W), F32),
                   S_((4, CHUNK, CHUNK), F32), S_((CHUNK, GMLP_W), F32)],
        name="gmlp_bwd")(proj, proj, v_gain, w_s, b_exp, dy)


PAIR_W = 128
QB = 128
KW = QB + 2 * ATTN_HALF
N_QB = SEQ // QB
PAD_ROWS = SEQ + 2 * ATTN_HALF


def _qk_norm_fn(x, gain):
    ms = jnp.dot(x * x, _head_sum_matrix(PAIR_W), precision=HI, preferred_element_type=F32) * (1.0 / HEAD)
    return x * lax.rsqrt(ms + NORM_EPS) * gain


def _deinterleave(dst_ref, src_ref, dil, offset):
    length = SEQ // dil
    if dil == 1:
        dst_ref[pl.ds(offset, SEQ), :] = src_ref[...]
        return
    for r in range(dil):
        dst_ref[pl.ds(offset + r * length, length), :] = src_ref[pl.ds(r, length, stride=dil), :]


def _interleave(dst_ref, src_ref, dil, offset):
    length = SEQ // dil
    if dil == 1:
        dst_ref[...] = src_ref[pl.ds(offset, SEQ), :]
        return
    for r in range(dil):
        dst_ref[pl.ds(r, length, stride=dil), :] = src_ref[pl.ds(offset + r * length, length), :]


def _edge_mask(blk, dil):
    length = SEQ // dil
    qi = blk * QB + lax.broadcasted_iota(jnp.int32, (QB, KW), 0)
    kj = blk * QB - ATTN_HALF + lax.broadcasted_iota(jnp.int32, (QB, KW), 1)
    return (kj >= 0) & (kj < SEQ) & ((qi // length) == (kj // length))


def _lane_is_head(hh):
    return (lax.broadcasted_iota(jnp.int32, (1, PAIR_W), 1) // HEAD) == hh


def _dilate_qkv(dil, qn_s, kn_s, v_ref, qd_s, kd_s, vd_s):
    _deinterleave(qd_s, qn_s, dil, 0)
    _deinterleave(kd_s, kn_s, dil, ATTN_HALF)
    _deinterleave(vd_s, v_ref, dil, ATTN_HALF)


def _attn_branch_fwd(br, dil, qn_s, kn_s, v_ref, bias_ref, qd_s, kd_s, vd_s, od_s, ld_s):
    _dilate_qkv(dil, qn_s, kn_s, v_ref, qd_s, kd_s, vd_s)

    def step(blk, carry):
        rows = pl.ds(pl.multiple_of(blk * QB, QB), QB)
        win = pl.ds(pl.multiple_of(blk * QB, QB), KW)
        qb, kw, vw = qd_s[rows, :], kd_s[win, :], vd_s[win, :]
        edge = _edge_mask(blk, dil)
        out, lse = 0.0, 0.0
        for hh in range(2):
            is_h = _lane_is_head(hh)
            s = _dot_nt(jnp.where(is_h, qb, 0.0), kw) * (HEAD ** -0.5) + bias_ref[br, hh]
            s = jnp.where(edge, s, NEG_INF)
            m = jnp.max(s, axis=-1, keepdims=True)
            l_h = m + jnp.log(jnp.sum(jnp.exp(s - m), axis=-1, keepdims=True))
            out = out + jnp.where(is_h, _dot(jnp.exp(s - l_h), vw), 0.0)
            lse = lse + jnp.where(is_h, l_h, 0.0)
        od_s[rows, :] = out
        ld_s[rows, :] = lse
        return carry

    lax.fori_loop(0, N_QB, step, 0)


def _attn_specs():
    col = lambda c0: (lambda p, b: (b, c0 // PAIR_W + p))
    return [pl.BlockSpec((SEQ, PAIR_W), col(C_Q)), pl.BlockSpec((SEQ, PAIR_W), col(C_K)), pl.BlockSpec((SEQ, PAIR_W), col(C_V)),
            pl.BlockSpec((1, PAIR_W), lambda p, b: (0, 0)), pl.BlockSpec((1, PAIR_W), lambda p, b: (0, 0)),
            pl.BlockSpec((3, 2, QB, KW), lambda p, b: (0, p, 0, 0))]


def _attn_scratch():
    seq = pltpu.VMEM((SEQ, PAIR_W), F32)
    pad = pltpu.VMEM((PAD_ROWS, PAIR_W), F32)
    return [seq, seq, seq, pad, pad, seq, seq]


def _zero_pads(*refs):
    for ref in refs:
        ref[pl.ds(0, ATTN_HALF), :] = jnp.zeros((ATTN_HALF, PAIR_W), F32)
        ref[pl.ds(ATTN_HALF + SEQ, ATTN_HALF), :] = jnp.zeros((ATTN_HALF, PAIR_W), F32)


ROW_STEP = 256


def _row_steps(fn, init=0):
    return lax.fori_loop(0, SEQ // ROW_STEP, lambda i, c: fn(pl.ds(pl.multiple_of(i * ROW_STEP, ROW_STEP), ROW_STEP), c), init)


def _interleave_add(acc_ref, src_ref, dil, offset):
    length = SEQ // dil
    if dil == 1:
        acc_ref[...] += src_ref[pl.ds(offset, SEQ), :]
        return
    for r in range(dil):
        acc_ref[pl.ds(r, length, stride=dil), :] += src_ref[pl.ds(offset + r * length, length), :]


def _attn_norm_qk(q_ref, k_ref, qg_ref, kg_ref, qn_s, kn_s):
    def norm(rows, carry):
        qn_s[rows, :] = _qk_norm_fn(q_ref[rows, :], qg_ref[...])
        kn_s[rows, :] = _qk_norm_fn(k_ref[rows, :], kg_ref[...])
        return carry

    _row_steps(norm)


def _attn_forward_all(q_ref, k_ref, v_ref, qg_ref, kg_ref, bias_ref, qn_s, kn_s, qd_s, kd_s, vd_s, od_s, ld_s, on_s, ln_s):
    _attn_norm_qk(q_ref, k_ref, qg_ref, kg_ref, qn_s, kn_s)
    _zero_pads(kd_s, vd_s)
    for br, dil in enumerate(ATTN_DILS):
        _attn_branch_fwd(br, dil, qn_s, kn_s, v_ref, bias_ref, qd_s, kd_s, vd_s, od_s, ld_s)
        _interleave(on_s.at[br], od_s, dil, 0)
        _interleave(ln_s.at[br], ld_s, dil, 0)


def _merge_weights(ln_s, rows):
    l0, l1, l2 = ln_s[0, rows, :], ln_s[1, rows, :], ln_s[2, rows, :]
    m = jnp.maximum(jnp.maximum(l0, l1), l2)
    e = [jnp.exp(l0 - m), jnp.exp(l1 - m), jnp.exp(l2 - m)]
    den = e[0] + e[1] + e[2]
    return [e[0] / den, e[1] / den, e[2] / den]


def _attn_fwd(proj, q_gain, k_gain, bias):
    def body(q_ref, k_ref, v_ref, qg_ref, kg_ref, bias_ref, o_ref, on_s, ln_s, qn_s, kn_s, qd_s, kd_s, vd_s, od_s, ld_s):
        _attn_forward_all(q_ref, k_ref, v_ref, qg_ref, kg_ref, bias_ref, qn_s, kn_s, qd_s, kd_s, vd_s, od_s, ld_s, on_s, ln_s)

        def merge(rows, carry):
            w = _merge_weights(ln_s, rows)
            o_ref[rows, :] = (w[0] * on_s[0, rows, :] + w[1] * on_s[1, rows, :] + w[2] * on_s[2, rows, :]).astype(BF16)
            return carry

        _row_steps(merge)

    kept = pl.BlockSpec((3, SEQ, PAIR_W), lambda p, b: (0, b, p))
    return pl.pallas_call(body, grid=(2, B_LOC), in_specs=_attn_specs(),
                          out_specs=[pl.BlockSpec((SEQ, PAIR_W), lambda p, b: (b, p)), kept, kept],
                          out_shape=[S_((T, 2 * PAIR_W), BF16), S_((3, T, 2 * PAIR_W), F32), S_((3, T, 2 * PAIR_W), F32)],
                          scratch_shapes=_attn_scratch(), compiler_params=_cp(), name="attn_fwd")(proj, proj, proj, q_gain, k_gain, bias)


def _attn_bwd(proj, q_gain, k_gain, bias, dy, kept_o, kept_l):
    def body(q_ref, k_ref, v_ref, qg_ref, kg_ref, bias_ref, dy_ref, on_ref, ln_ref,
             dq_ref, dk_ref, dv_ref, dqg_ref, dkg_ref, dbias_ref,
             qn_s, kn_s, qd_s, kd_s, vd_s, od_s, ld_s, don_s, dln_s, dod_s, dld_s, dqd_s, dkd_s, dvd_s, dqn_s, dkn_s, dvn_s):
        first = pl.program_id(1) == 0
        _attn_norm_qk(q_ref, k_ref, qg_ref, kg_ref, qn_s, kn_s)
        _zero_pads(kd_s, vd_s)

        def clear_acc(rows, carry):
            dqn_s[rows, :] = jnp.zeros((ROW_STEP, PAIR_W), F32)
            dkn_s[rows, :] = jnp.zeros((ROW_STEP, PAIR_W), F32)
            dvn_s[rows, :] = jnp.zeros((ROW_STEP, PAIR_W), F32)
            return carry

        _row_steps(clear_acc)

        @pl.when(first)
        def _():
            dbias_ref[...] = jnp.zeros_like(dbias_ref)

        for br, dil in enumerate(ATTN_DILS):
            def merge_bwd(rows, carry, br=br):
                w = _merge_weights(ln_ref, rows)
                dy = dy_ref[rows, :]
                same_head = _head_sum_matrix(PAIR_W)
                dws = [jnp.dot(dy * on_ref[j, rows, :], same_head, precision=HI, preferred_element_type=F32) for j in range(3)]
                dbar = w[0] * dws[0] + w[1] * dws[1] + w[2] * dws[2]
                don_s[rows, :] = w[br] * dy
                dln_s[rows, :] = w[br] * (dws[br] - dbar)
                return carry

            _row_steps(merge_bwd)
            _dilate_qkv(dil, qn_s, kn_s, v_ref, qd_s, kd_s, vd_s)
            _deinterleave(od_s, on_ref.at[br], dil, 0)
            _deinterleave(ld_s, ln_ref.at[br], dil, 0)
            _deinterleave(dod_s, don_s, dil, 0)
            _deinterleave(dld_s, dln_s, dil, 0)

            def clear(rows, carry):
                dkd_s[rows, :] = jnp.zeros((ROW_STEP, PAIR_W), F32)
                dvd_s[rows, :] = jnp.zeros((ROW_STEP, PAIR_W), F32)
                return carry

            _row_steps(clear)
            tail = pl.ds(SEQ, 2 * ATTN_HALF)
            dkd_s[tail, :] = jnp.zeros((2 * ATTN_HALF, PAIR_W), F32)
            dvd_s[tail, :] = jnp.zeros((2 * ATTN_HALF, PAIR_W), F32)

            def step(blk, carry, br=br, dil=dil):
                rows = pl.ds(pl.multiple_of(blk * QB, QB), QB)
                win = pl.ds(pl.multiple_of(blk * QB, QB), KW)
                qb, kw, vw = qd_s[rows, :], kd_s[win, :], vd_s[win, :]
                do_b, dl_b, o_b, l_b = dod_s[rows, :], dld_s[rows, :], od_s[rows, :], ld_s[rows, :]
                edge = _edge_mask(blk, dil)
                dq, dk, dv = 0.0, 0.0, 0.0
                for hh in range(2):
                    is_h = _lane_is_head(hh)
                    pick = (lax.broadcasted_iota(jnp.int32, (1, PAIR_W), 1) == hh * HEAD).astype(F32)
                    q_h = jnp.where(is_h, qb, 0.0)
                    do_h = jnp.where(is_h, do_b, 0.0)
                    s = _dot_nt(q_h, kw) * (HEAD ** -0.5) + bias_ref[br, hh]
                    s = jnp.where(edge, s, NEG_INF)
                    p = jnp.exp(s - jnp.sum(l_b * pick, axis=-1, keepdims=True))
                    dp = _dot_nt(do_h, vw)
                    delta = jnp.sum(do_h * o_b, axis=-1, keepdims=True)
                    ds = p * (dp - delta + jnp.sum(dl_b * pick, axis=-1, keepdims=True))
                    dbias_ref[br, hh] += ds
                    dq = dq + jnp.where(is_h, _dot(ds, kw), 0.0) * (HEAD ** -0.5)
                    dk = dk + _dot_tn(ds, q_h) * (HEAD ** -0.5)
                    dv = dv + _dot_tn(p, do_h)
                dqd_s[rows, :] = dq
                dkd_s[win, :] += dk
                dvd_s[win, :] += dv
                return carry

            lax.fori_loop(0, N_QB, step, 0)
            _interleave_add(dqn_s, dqd_s, dil, 0)
            _interleave_add(dkn_s, dkd_s, dil, ATTN_HALF)
            _interleave_add(dvn_s, dvd_s, dil, ATTN_HALF)

        def norm_bwd(rows, carry):
            _, q_vjp = jax.vjp(_qk_norm_fn, q_ref[rows, :], qg_ref[...])
            _, k_vjp = jax.vjp(_qk_norm_fn, k_ref[rows, :], kg_ref[...])
            g_q, g_qg = q_vjp(dqn_s[rows, :])
            g_k, g_kg = k_vjp(dkn_s[rows, :])
            dq_ref[rows, :] = g_q.astype(BF16)
            dk_ref[rows, :] = g_k.astype(BF16)
            dv_ref[rows, :] = dvn_s[rows, :].astype(BF16)
            return carry[0] + g_qg, carry[1] + g_kg

        g_qg, g_kg = _row_steps(norm_bwd, (jnp.zeros((1, PAIR_W), F32), jnp.zeros((1, PAIR_W), F32)))
        pad7 = jnp.zeros((7, PAIR_W), F32)
        new_q = jnp.concatenate([g_qg, pad7], axis=0)
        new_k = jnp.concatenate([g_kg, pad7], axis=0)

        @pl.when(first)
        def _():
            dqg_ref[...] = new_q
            dkg_ref[...] = new_k

        @pl.when(jnp.logical_not(first))
        def _():
            dqg_ref[...] += new_q
            dkg_ref[...] += new_k

    seq = pltpu.VMEM((SEQ, PAIR_W), F32)
    pad = pltpu.VMEM((PAD_ROWS, PAIR_W), F32)
    kept = pl.BlockSpec((3, SEQ, PAIR_W), lambda p, b: (0, b, p))
    out_blk = pl.BlockSpec((SEQ, PAIR_W), lambda p, b: (b, p))
    gain_blk = pl.BlockSpec((None, 8, PAIR_W), lambda p, b: (p, 0, 0))
    return pl.pallas_call(
        body, grid=(2, B_LOC),
        in_specs=_attn_specs() + [pl.BlockSpec((SEQ, PAIR_W), lambda p, b: (b, (SSD_INNER + GMLP_W) // PAIR_W + p)), kept, kept],
        out_specs=[out_blk, out_blk, out_blk, gain_blk, gain_blk, pl.BlockSpec((3, 2, QB, KW), lambda p, b: (0, p, 0, 0))],
        out_shape=[S_((T, 2 * PAIR_W), BF16)] * 3 + [S_((2, 8, PAIR_W), F32)] * 2 + [S_((3, 4, QB, KW), F32)],
        scratch_shapes=_attn_scratch() + [seq, seq, seq, seq, seq, pad, pad, seq, seq, seq],
        compiler_params=_cp(), name="attn_bwd")(proj, proj, proj, q_gain, k_gain, bias, dy, kept_o, kept_l)


def _rel_bucket(rel):
    nb = 16
    max_exact = nb // 2
    n = jnp.abs(rel)
    large = max_exact + (jnp.log(jnp.maximum(n, 1).astype(F32) / max_exact) / math.log(1024 / max_exact) * (nb - max_exact)).astype(jnp.int32)
    large = jnp.minimum(large, nb - 1)
    return jnp.where(rel > 0, nb, 0) + jnp.where(n < max_exact, n, large)


def _attn_bias(rel_table):
    rel = jnp.arange(KW)[None, :] - ATTN_HALF - jnp.arange(QB)[:, None]
    inside = (jnp.abs(rel) <= ATTN_HALF)
    out = []
    for dil in ATTN_DILS:
        one_hot = (_rel_bucket(rel * dil)[None] == jnp.arange(32)[:, None, None]).astype(F32)
        b = jnp.einsum("kh,kts->hts", rel_table, one_hot, precision=HI)
        out.append(jnp.where(inside[None], b, NEG_INF))
    return jnp.stack(out).astype(F32)


def _place():
    return lax.axis_index("x"), lax.axis_index("y"), lax.axis_index("c")


def _allgather8(buf, name):
    rows = buf.shape[0]
    flips = [(fx, fy, fc) for fx in (0, 1) for fy in (0, 1) for fc in (0, 1)][1:]

    def body(in_ref, out_ref, send_sems, recv_sems, local_sem):
        x, y, c = _place()
        me = 4 * x + 2 * y + c
        mine = pltpu.make_async_copy(in_ref, out_ref.at[me], local_sem)
        mine.start()
        peers = [(1 - x if fx else x, 1 - y if fy else y, 1 - c if fc else c) for fx, fy, fc in flips]

        def copy(k, slot, peer):
            return pltpu.make_async_remote_copy(src_ref=in_ref, dst_ref=out_ref.at[slot], send_sem=send_sems.at[k],
                                                recv_sem=recv_sems.at[k], device_id=peer, device_id_type=MESH)

        sends = [copy(k, me, peer) for k, peer in enumerate(peers)]
        for cp in sends:
            cp.start()
        for k, (px, py, pc) in enumerate(peers):
            copy(k, 4 * px + 2 * py + pc, (px, py, pc)).wait_recv()
        for cp in sends:
            cp.wait_send()
        mine.wait()

    return pl.pallas_call(body, in_specs=[ANY], out_specs=ANY, out_shape=S_((N_DEV, rows, 128), F32),
                          scratch_shapes=[pltpu.SemaphoreType.DMA((7,)), pltpu.SemaphoreType.DMA((7,)), pltpu.SemaphoreType.DMA(())],
                          name=name)(buf)


N_BIG = 4
HALF_L = DEPTH // 2


def _other_chips(x, y):
    return [(1 - x, y), (x, 1 - y), (1 - x, 1 - y)]


def _gather_weights(shards):
    def body(*refs):
        ins, outs = refs[:N_BIG], refs[N_BIG:2 * N_BIG]
        ici_send, ici_recv, d2d_send, d2d_recv = refs[2 * N_BIG:]
        x, y, c = _place()
        chip = 2 * x + y
        mine_l = pl.ds(HALF_L * c, HALF_L)
        other_l = pl.ds(HALF_L * (1 - c), HALF_L)
        chips = _other_chips(x, y)

        def ici(t, f, src, slot, peer):
            return pltpu.make_async_remote_copy(src_ref=src, dst_ref=outs[t].at[slot, mine_l], send_sem=ici_send.at[3 * t + f],
                                                recv_sem=ici_recv.at[3 * t + f], device_id=peer, device_id_type=MESH)

        def d2d(t, f, slot, layers):
            return pltpu.make_async_remote_copy(src_ref=outs[t].at[slot, layers], dst_ref=outs[t].at[slot, layers],
                                                send_sem=d2d_send.at[3 * t + f], recv_sem=d2d_recv.at[3 * t + f],
                                                device_id=(x, y, 1 - c), device_id_type=MESH)

        sends = []
        for t in range(N_BIG):
            for f, (px, py) in enumerate(chips):
                sends.append(ici(t, f, ins[t].at[mine_l], chip, (px, py, c)))
                sends[-1].start()
        for t in range(N_BIG):
            for f, (px, py) in enumerate(chips):
                ici(t, f, ins[t].at[mine_l], 2 * px + py, (px, py, c)).wait_recv()
                sends.append(d2d(t, f, 2 * px + py, mine_l))
                sends[-1].start()
        for t in range(N_BIG):
            for f, (px, py) in enumerate(chips):
                d2d(t, f, 2 * px + py, other_l).wait_recv()
        for cp in sends:
            cp.wait_send()

    n_sem = 3 * N_BIG
    return pl.pallas_call(
        body, in_specs=[ANY] * N_BIG, out_specs=[ANY] * N_BIG,
        out_shape=[S_((N_CHIPS,) + s.shape, s.dtype) for s in shards],
        scratch_shapes=[pltpu.SemaphoreType.DMA((n_sem,)) for _ in range(4)],
        name="gather_weights")(*shards)


def _pair_exchange(parts, name):
    def body(*refs):
        ins, outs = refs[:N_BIG], refs[N_BIG:2 * N_BIG]
        send_sems, recv_sems = refs[2 * N_BIG:]
        x, y, c = _place()
        cps = [pltpu.make_async_remote_copy(src_ref=ins[t], dst_ref=outs[t], send_sem=send_sems.at[t], recv_sem=recv_sems.at[t],
                                            device_id=(x, y, 1 - c), device_id_type=MESH) for t in range(N_BIG)]
        for cp in cps:
            cp.start()
        for cp in cps:
            cp.wait_recv()
        for cp in cps:
            cp.wait_send()

    return pl.pallas_call(body, in_specs=[ANY] * N_BIG, out_specs=[ANY] * N_BIG,
                          out_shape=[S_(p.shape, p.dtype) for p in parts],
                          scratch_shapes=[pltpu.SemaphoreType.DMA((N_BIG,)), pltpu.SemaphoreType.DMA((N_BIG,))],
                          name=name)(*parts)


def _scatter_to_chips(parts):
    def body(*refs):
        ins, outs = refs[:N_BIG], refs[N_BIG:2 * N_BIG]
        send_sems, recv_sems = refs[2 * N_BIG:]
        x, y, c = _place()
        cps = []
        for t in range(N_BIG):
            for f, (px, py) in enumerate(_other_chips(x, y)):
                cps.append(pltpu.make_async_remote_copy(src_ref=ins[t].at[2 * px + py], dst_ref=outs[t].at[f],
                                                        send_sem=send_sems.at[3 * t + f], recv_sem=recv_sems.at[3 * t + f],
                                                        device_id=(px, py, c), device_id_type=MESH))
        for cp in cps:
            cp.start()
        for cp in cps:
            cp.wait_recv()
        for cp in cps:
            cp.wait_send()

    return pl.pallas_call(body, in_specs=[ANY] * N_BIG, out_specs=[ANY] * N_BIG,
                          out_shape=[S_((3,) + p.shape[1:], p.dtype) for p in parts],
                          scratch_shapes=[pltpu.SemaphoreType.DMA((3 * N_BIG,)), pltpu.SemaphoreType.DMA((3 * N_BIG,))],
                          name="scatter_to_chips")(*parts)


def _add_pair(a, b, rows, name):
    n, r, c = a.shape

    def body(a_ref, b_ref, o_ref):
        o_ref[...] = (a_ref[...].astype(F32) + b_ref[...].astype(F32)).astype(BF16)

    blk = pl.BlockSpec((None, rows, c), lambda i, j: (i, j, 0))
    return pl.pallas_call(body, grid=(n, r // rows), in_specs=[blk, blk], out_specs=blk, out_shape=S_(a.shape, BF16), name=name)(a, b)


def _add_four(own, got, rows, name):
    n, r, c = own.shape

    def body(a_ref, g_ref, o_ref):
        o_ref[...] = ((a_ref[...].astype(F32) + g_ref[0].astype(F32)) + g_ref[1].astype(F32)) + g_ref[2].astype(F32)

    blk = pl.BlockSpec((None, rows, c), lambda i, j: (i, j, 0))
    return pl.pallas_call(body, grid=(n, r // rows), in_specs=[blk, pl.BlockSpec((3, None, rows, c), lambda i, j: (0, i, j, 0))],
                          out_specs=blk, out_shape=S_(own.shape, F32), name=name)(own, got)


def _sum_slots(slots):
    rows = slots.shape[1]

    def body(s_ref, o_ref):
        tot = s_ref[0]
        for k in range(1, N_DEV):
            tot = tot + s_ref[k]
        o_ref[...] = tot

    return pl.pallas_call(body, out_shape=S_((rows, 128), F32), name="sum_slots")(slots)


SMALL = ("mix_norm_gain", "ssd_conv_w", "ssd_conv_b", "ssd_dt_bias", "ssd_a_log", "ssd_d", "ssd_out_gain", "gmlp_v_gain",
         "gmlp_w_s", "gmlp_b_s", "attn_q_gain", "attn_k_gain", "rel_bias_table", "ffn_norm_gain", "ffn_conv_w", "ffn_conv_b")
BIG = ("w_in", "w_out", "ffn_w_up", "ffn_w_down")
WEIGHTS = ("mix_norm_gain", "w_in", "ssd_conv_w", "ssd_conv_b", "ssd_dt_bias", "ssd_a_log", "ssd_d", "ssd_out_gain", "gmlp_v_gain",
           "gmlp_w_s", "gmlp_b_s", "attn_q_gain", "attn_k_gain", "rel_bias_table", "w_out", "ffn_norm_gain", "ffn_w_up",
           "ffn_conv_w", "ffn_conv_b", "ffn_w_down")
ADAM_ROWS = {"w_in": 512, "w_out": 512, "ffn_w_up": 256, "ffn_w_down": 352}
SUM_ROWS = {"w_in": 512, "w_out": 256, "ffn_w_up": 256, "ffn_w_down": 352}


PACK_ROWS = 64


def _pack(arrays):
    flat = jnp.concatenate([a.reshape(-1).astype(F32) for a in arrays])
    rows = -(-flat.shape[0] // (PACK_ROWS * 128)) * PACK_ROWS
    return jnp.pad(flat, (0, rows * 128 - flat.shape[0])).reshape(rows, 128)


def _unpack(buf, shapes):
    flat = buf.reshape(-1)
    out, pos = [], 0
    for s in shapes:
        n = int(np.prod(s))
        out.append(flat[pos:pos + n].reshape(s))
        pos += n
    return out


def _perm_cols(w):
    pad = jnp.zeros(w.shape[:-1] + (NP - IN_WIDTH,), w.dtype)
    return jnp.concatenate([w[..., :1536], w[..., 1552:], w[..., 1536:1552], pad], axis=-1)


def _unperm_cols(w):
    return jnp.concatenate([w[..., :1536], w[..., C_DT:C_DT + 16], w[..., 1536:C_DT]], axis=-1)


def _layer_params(l, p, conv5_w, conv3_w, bias):
    def make(mix_g, conv5, conv5_b, dt_bias, a_log, d_skip, out_gain, v_gain, w_s, b_s, q_gain, k_gain, ffn_g, conv3, conv3_b):
        lanes = lambda a: jnp.pad(a.reshape(1, 16), ((0, 0), (0, 112)))
        cw3 = jnp.pad(jnp.transpose(conv3.reshape(3, 2, FFN_DIM), (1, 0, 2)), ((0, 0), (0, 5), (0, 0)))
        return dict(mix_g=mix_g.reshape(1, D_MODEL), cw5=jnp.pad(conv5, ((0, 3), (0, 0))), cb5=conv5_b.reshape(1, SSD_XBC),
                    dtb=lanes(dt_bias), alog=lanes(a_log), d_exp=jnp.repeat(d_skip, HEAD).reshape(1, SSD_INNER),
                    out_gain=out_gain.reshape(1, SSD_INNER), v_gain=v_gain.reshape(1, GMLP_W), w_s=w_s,
                    b_exp=jnp.repeat(b_s.T, HEAD, axis=1), q_gain=jnp.tile(q_gain, 2).reshape(1, PAIR_W),
                    k_gain=jnp.tile(k_gain, 2).reshape(1, PAIR_W), ffn_g=ffn_g.reshape(1, D_MODEL), cw3=cw3,
                    cb3=conv3_b.reshape(2, 1, FFN_DIM))

    args = (p["mix_norm_gain"][l], conv5_w[l], p["ssd_conv_b"][l], p["ssd_dt_bias"][l], p["ssd_a_log"][l], p["ssd_d"][l],
            p["ssd_out_gain"][l], p["gmlp_v_gain"][l], p["gmlp_w_s"][l], p["gmlp_b_s"][l], p["attn_q_gain"][l], p["attn_k_gain"][l],
            p["ffn_norm_gain"][l], conv3_w[l], p["ffn_conv_b"][l])
    return jax.vjp(make, *args)


def _forward_layer(x, lp, w_in, w_out, w_up, w_down, bias):
    h = _rmsnorm_fwd(x, lp["mix_g"], "rmsnorm_fwd")
    proj = _mm_nn(h, w_in, tm=1024, tn=1024, tk=1024, out_dtype=F32, name="mm_proj")
    xc = _ssd_pre_fwd(proj, lp["cw5"], lp["cb5"])
    y_ssd, ssd_sum, ssd_states = _ssd_scan_fwd(xc, proj, lp["dtb"], lp["alog"], lp["d_exp"], lp["out_gain"])
    y_gmlp = _gmlp_fwd(proj, lp["v_gain"], lp["w_s"], lp["b_exp"])
    y_attn, attn_o, attn_l = _attn_fwd(proj, lp["q_gain"], lp["k_gain"], bias)
    y = jnp.concatenate([y_ssd, y_gmlp, y_attn], axis=1)
    x2 = _mm_nn(y, w_out, tm=1024, tn=1024, tk=1024, out_dtype=F32, res=x, name="mm_out")
    hn = _rmsnorm_fwd(x2, lp["ffn_g"], "rmsnorm_fwd")
    up3 = _mm_up(hn, w_up)
    act = _convgate_fwd(up3, lp["cw3"], lp["cb3"])
    x3 = _mm_nn(act, w_down, tm=1024, tn=1024, tk=HALF_TILE, out_dtype=F32, res=x2, name="mm_down")
    return x3, dict(x=x, h=h, proj=proj, xc=xc, y=y, x2=x2, hn=hn, up3=up3, act=act, attn_o=attn_o, attn_l=attn_l,
                    ssd_sum=ssd_sum, ssd_states=ssd_states)


def _backward_layer(dx3, sv, lp, w_in, w_out, w_up, w_down, bias):
    d_act = _mm_nt(dx3, w_down, tm=1024, tn=HALF_TILE, tk=1024, out_dtype=F32, name="mm_dact")
    dw_down = _mm_tn(sv["act"], dx3, tm=HALF_TILE, tn=1024, tk=1024, out_dtype=BF16, name="mm_dwdown")
    dup3, dcw3 = _convgate_bwd(sv["up3"], lp["cw3"], lp["cb3"], d_act)
    d_hn = _mm_dhn(dup3, w_up)
    dw_up = _mm_dwup(sv["hn"], dup3)
    dx2, d_ffn_g = _rmsnorm_bwd(sv["x2"], lp["ffn_g"], d_hn, dx3, "rmsnorm_bwd")
    d_y = _mm_nt(dx2, w_out, tm=1024, tn=1024, tk=1024, out_dtype=F32, name="mm_dy")
    dw_out = _mm_tn(sv["y"], dx2, tm=1024, tn=1024, tk=1024, out_dtype=BF16, name="mm_dwout")
    proj, xc = sv["proj"], sv["xc"]
    dxs, dbc, dcc, dz, ddt2, ddtb2, dal2, d_dexp, d_outg = _ssd_scan_bwd(xc, proj, lp["dtb"], lp["alog"], lp["d_exp"], lp["out_gain"], d_y, sv["ssd_sum"],
                                                                         sv["ssd_states"])
    d_xbc, dcw5 = _ssd_pre_bwd(proj, lp["cw5"], lp["cb5"], jnp.concatenate([dxs, dbc, dcc], axis=1))
    d_gu, d_gv, d_vg, d_ws, d_bexp = _gmlp_bwd(proj, lp["v_gain"], lp["w_s"], lp["b_exp"], d_y)
    d_q, d_k, d_v, d_qg2, d_kg2, d_bias = _attn_bwd(proj, lp["q_gain"], lp["k_gain"], bias, d_y, sv["attn_o"], sv["attn_l"])
    d_dt = (ddt2[0] + ddt2[1]).astype(BF16)
    d_proj = jnp.concatenate([d_xbc, dz, d_gu, d_gv, d_q, d_k, d_v, d_dt, jnp.zeros((T, NP - C_DT - 128), BF16)], axis=1)
    d_h = _mm_nt(d_proj, w_in, tm=1024, tn=1024, tk=1024, out_dtype=F32, name="mm_dh")
    dw_in = _mm_tn(sv["h"], d_proj, tm=1024, tn=1024, tk=1024, out_dtype=BF16, name="mm_dwin")
    dx, d_mix_g = _rmsnorm_bwd(sv["x"], lp["mix_g"], d_h, dx2, "rmsnorm_bwd")
    d_lp = dict(mix_g=d_mix_g, cw5=dcw5[:8] * (jnp.arange(8) < 5)[:, None].astype(F32), cb5=dcw5[5:6],
                dtb=(ddtb2[0, :1] + ddtb2[1, :1]), alog=(dal2[0, :1] + dal2[1, :1]), d_exp=d_dexp, out_gain=d_outg,
                v_gain=d_vg, w_s=d_ws, b_exp=d_bexp, q_gain=d_qg2[0, :1] + d_qg2[1, :1], k_gain=d_kg2[0, :1] + d_kg2[1, :1],
                ffn_g=d_ffn_g, cw3=dcw3 * (jnp.arange(8) < 3)[None, :, None].astype(F32), cb3=dcw3[:, 3:4])
    return dx, (dw_in, dw_out, dw_up, dw_down), d_lp, d_bias


def _to_shard_major(name, dw):
    if name == "w_in":
        dw = _unperm_cols(dw)
    if name in ("w_in", "ffn_w_up"):
        l, r, c = dw.shape
        return jnp.transpose(dw.reshape(l, r, N_CHIPS, c // N_CHIPS), (2, 0, 1, 3))
    l, r, c = dw.shape
    return jnp.transpose(dw.reshape(l, N_CHIPS, r // N_CHIPS, c), (1, 0, 2, 3))


def _whole_weight(name, gathered, own, chip, l):
    parts = [jnp.where(chip == k, own[l], gathered[k, l]) for k in range(N_CHIPS)]
    if name in ("w_in", "ffn_w_up"):
        w = jnp.concatenate(parts, axis=1)
        return _perm_cols(w) if name == "w_in" else w
    return jnp.concatenate(parts, axis=0)


def _reduce_to_chips(parts, chip, core, sum_rows=None):
    sum_rows = sum_rows or [SUM_ROWS[n] for n in BIG]
    mine = [lax.dynamic_slice_in_dim(g, HALF_L * core, HALF_L, axis=1) for g in parts]
    theirs = [lax.dynamic_slice_in_dim(g, HALF_L * (1 - core), HALF_L, axis=1) for g in parts]
    got = _pair_exchange(theirs, "pair_exchange")
    chip_sums = []
    for n, rows, a, b in zip(BIG, sum_rows, mine, got):
        shp = a.shape
        s2 = _add_pair(a.reshape(N_CHIPS * HALF_L, shp[2], shp[3]), b.reshape(N_CHIPS * HALF_L, shp[2], shp[3]), rows, "add_pair_" + n)
        chip_sums.append(s2.reshape(shp))
    landed = _scatter_to_chips(chip_sums)
    halves = []
    for n, rows, s2, g3 in zip(BIG, sum_rows, chip_sums, landed):
        own = lax.dynamic_index_in_dim(s2, chip, axis=0, keepdims=False)
        halves.append(_add_four(own, g3, rows, "add_four_" + n))
    other = _pair_exchange(halves, "swap_halves")
    out = []
    for mine_h, other_h in zip(halves, other):
        both = jnp.stack([mine_h, other_h])
        first = lax.dynamic_index_in_dim(both, core, axis=0, keepdims=False)
        second = lax.dynamic_index_in_dim(both, 1 - core, axis=0, keepdims=False)
        out.append(jnp.concatenate([first, second], axis=0))
    return out


LAYER_SMALL = ("mix_norm_gain", "ssd_conv_w", "ssd_conv_b", "ssd_dt_bias", "ssd_a_log", "ssd_d", "ssd_out_gain", "gmlp_v_gain",
               "gmlp_w_s", "gmlp_b_s", "attn_q_gain", "attn_k_gain", "ffn_norm_gain", "ffn_conv_w", "ffn_conv_b")


def _local_grads(x, loss_target, p, conv5_w, conv3_w, full):
    bias, bias_vjp = jax.vjp(_attn_bias, p["rel_bias_table"])
    xt = x.reshape(T, D_MODEL)
    saved, lps, lp_vjps = [], [], []
    for l in range(DEPTH):
        lp, lp_vjp = _layer_params(l, p, conv5_w, conv3_w, bias)
        xt, sv = _forward_layer(xt, lp, full["w_in"][l], full["w_out"][l], full["ffn_w_up"][l], full["ffn_w_down"][l], bias)
        saved.append(sv)
        lps.append(lp)
        lp_vjps.append(lp_vjp)
    dxt, loss_parts = _loss_head(xt, loss_target.reshape(T, D_MODEL))
    loss_local = jnp.sum(loss_parts[::8, 0])

    big_grads = [None] * DEPTH
    small_layers = [None] * DEPTH
    d_bias_tot = jnp.zeros_like(bias)
    for l in reversed(range(DEPTH)):
        dxt, big_grads[l], d_lp, d_bias = _backward_layer(dxt, saved[l], lps[l], full["w_in"][l], full["w_out"][l],
                                                          full["ffn_w_up"][l], full["ffn_w_down"][l], bias)
        small_layers[l] = lp_vjps[l](d_lp)
        d_bias_tot = d_bias_tot + d_bias
    (d_rel_table,) = bias_vjp(d_bias_tot)
    local_small = {n: jnp.stack([small_layers[l][i] for l in range(DEPTH)]) for i, n in enumerate(LAYER_SMALL)}
    local_small["rel_bias_table"] = d_rel_table
    return dxt, loss_local, big_grads, local_small


def kernel(x, mix_norm_gain, w_in, ssd_conv_w, ssd_conv_b, ssd_dt_bias, ssd_a_log, ssd_d, ssd_out_gain, gmlp_v_gain, gmlp_w_s, gmlp_b_s, attn_q_gain, attn_k_gain, rel_bias_table, w_out, ffn_norm_gain, ffn_w_up, ffn_conv_w, ffn_conv_b, ffn_w_down, loss_target, m_mix_norm_gain, m_w_in, m_ssd_conv_w, m_ssd_conv_b, m_ssd_dt_bias, m_ssd_a_log, m_ssd_d, m_ssd_out_gain, m_gmlp_v_gain, m_gmlp_w_s, m_gmlp_b_s, m_attn_q_gain, m_attn_k_gain, m_rel_bias_table, m_w_out, m_ffn_norm_gain, m_ffn_w_up, m_ffn_conv_w, m_ffn_conv_b, m_ffn_w_down, v_mix_norm_gain, v_w_in, v_ssd_conv_w, v_ssd_conv_b, v_ssd_dt_bias, v_ssd_a_log, v_ssd_d, v_ssd_out_gain, v_gmlp_v_gain, v_gmlp_w_s, v_gmlp_b_s, v_attn_q_gain, v_attn_k_gain, v_rel_bias_table, v_w_out, v_ffn_norm_gain, v_ffn_w_up, v_ffn_conv_w, v_ffn_conv_b, v_ffn_w_down):
    env = dict(locals())
    p = {n: env[n] for n in WEIGHTS}
    chip = 2 * lax.axis_index("x") + lax.axis_index("y")
    core = lax.axis_index("c")

    conv_slots = _allgather8(_pack([ssd_conv_w, ffn_conv_w]), "allgather_conv")
    conv_parts = [_unpack(conv_slots[2 * k], [ssd_conv_w.shape, ffn_conv_w.shape]) for k in range(N_CHIPS)]
    conv5_w = jnp.concatenate([cp[0] for cp in conv_parts], axis=-1)
    conv3_w = jnp.concatenate([cp[1] for cp in conv_parts], axis=-1)
    own = [p[n].astype(BF16) for n in BIG]
    gathered = _gather_weights(own)
    full = {n: [_whole_weight(n, g, o, chip, l) for l in range(DEPTH)] for n, g, o in zip(BIG, gathered, own)}

    dxt, loss_local, big_grads, local_small = _local_grads(x, loss_target, p, conv5_w, conv3_w, full)

    small_shapes = [local_small[n].shape for n in SMALL] + [(1,)]
    slots = _allgather8(_pack([local_small[n] for n in SMALL] + [loss_local.reshape(1)]), "allgather_small")
    summed = _unpack(_sum_slots(slots), small_shapes)
    grads = dict(zip(SMALL, summed[:-1]))
    loss = summed[-1][0]
    grads["ssd_conv_w"] = lax.dynamic_slice_in_dim(grads["ssd_conv_w"], chip * 256, 256, axis=2)
    grads["ffn_conv_w"] = lax.dynamic_slice_in_dim(grads["ffn_conv_w"], chip * (2 * FFN_DIM // N_CHIPS), 2 * FFN_DIM // N_CHIPS, axis=2)

    parts = [_to_shard_major(n, jnp.stack([big_grads[l][i] for l in range(DEPTH)])) for i, n in enumerate(BIG)]
    for n, g in zip(BIG, _reduce_to_chips(parts, chip, core)):
        grads[n] = g

    delta, new_m, new_v = {}, {}, {}
    for n in BIG:
        shp = p[n].shape
        two = lambda a: a.reshape(shp[0] * shp[1], shp[2])
        d, nm, nv = _adamw(two(p[n]), two(grads[n]), two(env["m_" + n]), two(env["v_" + n]), ADAM_ROWS[n], "adamw_" + n)
        delta[n], new_m[n], new_v[n] = d.reshape(shp), nm.reshape(shp), nv.reshape(shp)
    shapes = [p[n].shape for n in SMALL]
    d, nm, nv = _adamw(_pack([p[n] for n in SMALL]), _pack([grads[n] for n in SMALL]), _pack([env["m_" + n] for n in SMALL]),
                       _pack([env["v_" + n] for n in SMALL]), PACK_ROWS, "adamw_small")
    for n, a, b, c in zip(SMALL, _unpack(d, shapes), _unpack(nm, shapes), _unpack(nv, shapes)):
        delta[n], new_m[n], new_v[n] = a, b, c

    return (loss, dxt.reshape(B_LOC, SEQ, D_MODEL), *[grads[n] for n in WEIGHTS], *[delta[n] for n in WEIGHTS],
            *[new_m[n] for n in WEIGHTS], *[new_v[n] for n in WEIGHTS])
```

```python
import functools
import math

import jax
import jax.numpy as jnp
import numpy as np
from jax import lax
from jax.experimental import pallas as pl
from jax.experimental.pallas import tpu as pltpu

F32 = jnp.float32
BF16 = jnp.bfloat16
HI = lax.Precision.HIGHEST
MESH = pl.DeviceIdType.MESH
ANY = pl.BlockSpec(memory_space=pl.ANY)

D_MODEL = 1024
SEQ = 2048
B_LOC = 2
T = B_LOC * SEQ
DEPTH = 4
N_CHIPS = 4
N_DEV = 8
HEAD = 64
CHUNK = 128
N_CHUNK = SEQ // CHUNK
SSD_INNER = 512
SSD_XBC = 1024
FFN_DIM = 2816
IN_WIDTH = 2832
NP = 3072
C_XS, C_B, C_C, C_Z, C_GU, C_GV, C_Q, C_K, C_V, C_DT = 0, 512, 768, 1024, 1536, 1792, 2048, 2304, 2560, 2816
NORM_EPS = 1e-6
NEG_INF = -1e30
ATTN_DILS = (1, 4, 16)
ATTN_HALF = 64
ADAM_LR, ADAM_B1, ADAM_B2, ADAM_EPS, ADAM_WD, ADAM_STEP = 0.001, 0.9, 0.999, 1e-08, 0.01, 10
VMEM_LIMIT = 56 * 1024 * 1024

S_ = jax.ShapeDtypeStruct


def _cp():
    return pltpu.CompilerParams(vmem_limit_bytes=VMEM_LIMIT)


def _shift_rows(x, k):
    n = x.shape[0]
    if k == 0:
        return x
    r = pltpu.roll(x, (-k) % n, 0)
    t = lax.broadcasted_iota(jnp.int32, (n, 1), 0)
    return jnp.where((t + k >= 0) & (t + k < n), r, 0.0)


@functools.partial(jax.custom_vjp, nondiff_argnums=(1,))
def _shift(x, k):
    return _shift_rows(x, k)


def _shift_fwd(x, k):
    return _shift_rows(x, k), None


def _shift_bwd(k, _, g):
    return (_shift_rows(g, -k),)


_shift.defvjp(_shift_fwd, _shift_bwd)


def _dwconv(x, taps, bias):
    half = len(taps) // 2
    y = bias
    for k, w in enumerate(taps):
        y = y + w * _shift(x, k - half)
    return y


def _softplus(x):
    return jnp.maximum(x, 0.0) + jnp.log1p(jnp.exp(-jnp.abs(x)))


def _dot(a, b):
    return jnp.dot(a.astype(BF16), b.astype(BF16), preferred_element_type=F32)


def _dot_nt(a, b):
    return lax.dot_general(a.astype(BF16), b.astype(BF16), (((1,), (1,)), ((), ())), preferred_element_type=F32)


def _dot_tn(a, b):
    return lax.dot_general(a.astype(BF16), b.astype(BF16), (((0,), (0,)), ((), ())), preferred_element_type=F32)


def _head_sum_matrix(width):
    i = lax.broadcasted_iota(jnp.int32, (width, width), 0) // HEAD
    j = lax.broadcasted_iota(jnp.int32, (width, width), 1) // HEAD
    return (i == j).astype(F32)


def _matmul(a, b, *, dims, grid, a_spec, b_spec, o_spec, out_shape, acc_shape, res=None, res_spec=None, name):
    nk = grid[2]

    def body(*refs):
        if res is not None:
            a_ref, b_ref, r_ref, o_ref = refs[:4]
        else:
            a_ref, b_ref, o_ref = refs[:3]
            r_ref = None
        part = lax.dot_general(a_ref[...].astype(BF16), b_ref[...].astype(BF16), dims, preferred_element_type=F32)
        if nk == 1:
            if r_ref is not None:
                part = part + r_ref[...]
            o_ref[...] = part.astype(o_ref.dtype)
            return
        acc_ref = refs[-1]
        k = pl.program_id(2)

        @pl.when(k == 0)
        def _():
            acc_ref[...] = part

        @pl.when(k > 0)
        def _():
            acc_ref[...] += part

        @pl.when(k == nk - 1)
        def _():
            tot = acc_ref[...]
            if r_ref is not None:
                tot = tot + r_ref[...]
            o_ref[...] = tot.astype(o_ref.dtype)

    in_specs = [a_spec, b_spec] + ([res_spec] if res is not None else [])
    args = (a, b) + ((res,) if res is not None else ())
    scratch = [] if nk == 1 else [pltpu.VMEM(acc_shape, F32)]
    return pl.pallas_call(body, grid=grid, in_specs=in_specs, out_specs=o_spec, out_shape=out_shape,
                          scratch_shapes=scratch, compiler_params=_cp(), name=name)(*args)


NN = (((1,), (0,)), ((), ()))
NT = (((1,), (1,)), ((), ()))
TN = (((0,), (0,)), ((), ()))


def _mm_nn(a, b, *, tm, tn, tk, out_dtype, res=None, name):
    m, k = a.shape
    n = b.shape[1]
    return _matmul(a, b, dims=NN, grid=(m // tm, n // tn, k // tk),
                   a_spec=pl.BlockSpec((tm, tk), lambda i, j, q: (i, q)),
                   b_spec=pl.BlockSpec((tk, tn), lambda i, j, q: (q, j)),
                   o_spec=pl.BlockSpec((tm, tn), lambda i, j, q: (i, j)),
                   out_shape=S_((m, n), out_dtype), acc_shape=(tm, tn), res=res,
                   res_spec=pl.BlockSpec((tm, tn), lambda i, j, q: (i, j)), name=name)


def _mm_nt(a, b, *, tm, tn, tk, out_dtype, name):
    m, k = a.shape
    n = b.shape[0]
    return _matmul(a, b, dims=NT, grid=(m // tm, n // tn, k // tk),
                   a_spec=pl.BlockSpec((tm, tk), lambda i, j, q: (i, q)),
                   b_spec=pl.BlockSpec((tn, tk), lambda i, j, q: (j, q)),
                   o_spec=pl.BlockSpec((tm, tn), lambda i, j, q: (i, j)),
                   out_shape=S_((m, n), out_dtype), acc_shape=(tm, tn), name=name)


def _mm_tn(a, b, *, tm, tn, tk, out_dtype, name):
    k, m = a.shape
    n = b.shape[1]
    return _matmul(a, b, dims=TN, grid=(m // tm, n // tn, k // tk),
                   a_spec=pl.BlockSpec((tk, tm), lambda i, j, q: (q, i)),
                   b_spec=pl.BlockSpec((tk, tn), lambda i, j, q: (q, j)),
                   o_spec=pl.BlockSpec((tm, tn), lambda i, j, q: (i, j)),
                   out_shape=S_((m, n), out_dtype), acc_shape=(tm, tn), name=name)


HALF_TILE = FFN_DIM // 2


def _mm_up(hn, w_up):
    return _matmul(hn, w_up, dims=NN, grid=(T // 1024, 4, 1),
                   a_spec=pl.BlockSpec((1024, D_MODEL), lambda i, j, q: (i, 0)),
                   b_spec=pl.BlockSpec((D_MODEL, HALF_TILE), lambda i, j, q: (0, j)),
                   o_spec=pl.BlockSpec((None, 1024, HALF_TILE), lambda i, j, q: (j // 2, i, j % 2)),
                   out_shape=S_((2, T, FFN_DIM), F32), acc_shape=(1024, HALF_TILE), name="mm_up")


def _mm_dhn(dup3, w_up):
    return _matmul(dup3, w_up, dims=NT, grid=(T // 1024, 1, 4),
                   a_spec=pl.BlockSpec((None, 1024, HALF_TILE), lambda i, j, q: (q // 2, i, q % 2)),
                   b_spec=pl.BlockSpec((D_MODEL, HALF_TILE), lambda i, j, q: (0, q)),
                   o_spec=pl.BlockSpec((1024, D_MODEL), lambda i, j, q: (i, 0)),
                   out_shape=S_((T, D_MODEL), F32), acc_shape=(1024, D_MODEL), name="mm_dhn")


def _mm_dwup(hn, dup3):
    return _matmul(hn, dup3, dims=TN, grid=(1, 4, T // 1024),
                   a_spec=pl.BlockSpec((1024, D_MODEL), lambda i, j, q: (q, 0)),
                   b_spec=pl.BlockSpec((None, 1024, HALF_TILE), lambda i, j, q: (j // 2, q, j % 2)),
                   o_spec=pl.BlockSpec((D_MODEL, HALF_TILE), lambda i, j, q: (0, j)),
                   out_shape=S_((D_MODEL, 2 * FFN_DIM), BF16), acc_shape=(D_MODEL, HALF_TILE), name="mm_dwup")


ROWS = 512


def _rmsnorm_fwd(x, gain, name):
    def body(x_ref, g_ref, o_ref):
        xv = x_ref[...]
        r = lax.rsqrt(jnp.mean(xv * xv, axis=-1, keepdims=True) + NORM_EPS)
        o_ref[...] = (xv * r * g_ref[...]).astype(BF16)

    return pl.pallas_call(body, grid=(T // ROWS,),
                          in_specs=[pl.BlockSpec((ROWS, D_MODEL), lambda i: (i, 0)), pl.BlockSpec((1, D_MODEL), lambda i: (0, 0))],
                          out_specs=pl.BlockSpec((ROWS, D_MODEL), lambda i: (i, 0)),
                          out_shape=S_((T, D_MODEL), BF16), name=name)(x, gain)


def _rmsnorm_bwd(x, gain, dh, dres, name):
    def body(x_ref, g_ref, dh_ref, dres_ref, dx_ref, dg_ref):
        xv = x_ref[...]
        r = lax.rsqrt(jnp.mean(xv * xv, axis=-1, keepdims=True) + NORM_EPS)
        gd = dh_ref[...] * g_ref[...]
        dot = jnp.mean(gd * xv, axis=-1, keepdims=True)
        dx_ref[...] = dres_ref[...] + r * gd - xv * (r * r * r * dot)
        part = jnp.sum(dh_ref[...] * xv * r, axis=0, keepdims=True)

        @pl.when(pl.program_id(0) == 0)
        def _():
            dg_ref[...] = part

        @pl.when(pl.program_id(0) > 0)
        def _():
            dg_ref[...] += part

    row = pl.BlockSpec((ROWS, D_MODEL), lambda i: (i, 0))
    vec = pl.BlockSpec((1, D_MODEL), lambda i: (0, 0))
    return pl.pallas_call(body, grid=(T // ROWS,), in_specs=[row, vec, row, row], out_specs=[row, vec],
                          out_shape=[S_((T, D_MODEL), F32), S_((1, D_MODEL), F32)], name=name)(x, gain, dh, dres)


def _loss_head(y, target):
    def body(y_ref, t_ref, dy_ref, p_ref):
        e = y_ref[...] - t_ref[...]
        dy_ref[...] = e * (1.0 / D_MODEL)
        p_ref[...] = jnp.full((8, 128), 0.5 / D_MODEL, F32) * jnp.sum(e * e)

    row = pl.BlockSpec((ROWS, D_MODEL), lambda i: (i, 0))
    return pl.pallas_call(body, grid=(T // ROWS,), in_specs=[row, row],
                          out_specs=[row, pl.BlockSpec((8, 128), lambda i: (i, 0))],
                          out_shape=[S_((T, D_MODEL), F32), S_((T // ROWS * 8, 128), F32)], name="loss_head")(y, target)


def _adamw(w, g, m, v, rows, name):
    r_tot, c = w.shape

    def body(w_ref, g_ref, m_ref, v_ref, d_ref, nm_ref, nv_ref):
        gv = g_ref[...]
        nm = ADAM_B1 * m_ref[...] + (1.0 - ADAM_B1) * gv
        nv = ADAM_B2 * v_ref[...] + (1.0 - ADAM_B2) * (gv * gv)
        m_hat = nm / (1.0 - ADAM_B1 ** ADAM_STEP)
        v_hat = nv / (1.0 - ADAM_B2 ** ADAM_STEP)
        d_ref[...] = -ADAM_LR * (m_hat / (jnp.sqrt(v_hat) + ADAM_EPS) + ADAM_WD * w_ref[...])
        nm_ref[...] = nm
        nv_ref[...] = nv

    blk = pl.BlockSpec((rows, c), lambda i: (i, 0))
    out = S_((r_tot, c), F32)
    return pl.pallas_call(body, grid=(r_tot // rows,), in_specs=[blk] * 4, out_specs=[blk] * 3,
                          out_shape=[out, out, out], name=name)(w, g, m, v)


FFN_CT = 256


def _gate_fn(up_g, up_v, wg0, wg1, wg2, bg, wv0, wv1, wv2, bv):
    gate = _dwconv(up_g, [wg0, wg1, wg2], bg)
    val = _dwconv(up_v, [wv0, wv1, wv2], bv)
    return jax.nn.silu(gate) * val


def _taps(ref, part, n):
    return [ref[part, k:k + 1, :] for k in range(n)]


def _convgate_fwd(up3, cw, cb):
    def body(up_ref, cw_ref, cb_ref, o_ref):
        o_ref[...] = _gate_fn(up_ref[0], up_ref[1], *_taps(cw_ref, 0, 3), cb_ref[0], *_taps(cw_ref, 1, 3), cb_ref[1]).astype(BF16)

    return pl.pallas_call(
        body, grid=(FFN_DIM // FFN_CT, B_LOC),
        in_specs=[pl.BlockSpec((2, SEQ, FFN_CT), lambda j, b: (0, b, j)),
                  pl.BlockSpec((2, 8, FFN_CT), lambda j, b: (0, 0, j)),
                  pl.BlockSpec((2, 1, FFN_CT), lambda j, b: (0, 0, j))],
        out_specs=pl.BlockSpec((SEQ, FFN_CT), lambda j, b: (b, j)),
        out_shape=S_((T, FFN_DIM), BF16), compiler_params=_cp(), name="convgate_fwd")(up3, cw, cb)


def _convgate_bwd(up3, cw, cb, dact, hook=None):
    def body(up_ref, cw_ref, cb_ref, da_ref, dup_ref, dcw_ref):
        args = (up_ref[0], up_ref[1], *_taps(cw_ref, 0, 3), cb_ref[0], *_taps(cw_ref, 1, 3), cb_ref[1])
        _, vjp = jax.vjp(_gate_fn, *args)
        dg, dv, g0, g1, g2, gb, v0, v1, v2, vb = vjp(da_ref[...])
        dup_ref[0] = dg.astype(BF16)
        dup_ref[1] = dv.astype(BF16)
        zero = jnp.zeros((4, FFN_CT), F32)
        new = jnp.stack([jnp.concatenate([g0, g1, g2, gb, zero], axis=0), jnp.concatenate([v0, v1, v2, vb, zero], axis=0)])

        @pl.when(pl.program_id(1) == 0)
        def _():
            dcw_ref[...] = new

        @pl.when(pl.program_id(1) > 0)
        def _():
            dcw_ref[...] += new

    return _hooked_call(
        body, grid=(FFN_DIM // FFN_CT, B_LOC),
        in_specs=[pl.BlockSpec((2, SEQ, FFN_CT), lambda j, b: (0, b, j)),
                  pl.BlockSpec((2, 8, FFN_CT), lambda j, b: (0, 0, j)),
                  pl.BlockSpec((2, 1, FFN_CT), lambda j, b: (0, 0, j)),
                  pl.BlockSpec((SEQ, FFN_CT), lambda j, b: (b, j))],
        out_specs=[pl.BlockSpec((2, SEQ, FFN_CT), lambda j, b: (0, b, j)),
                   pl.BlockSpec((2, 8, FFN_CT), lambda j, b: (0, 0, j))],
        out_shape=[S_((2, T, FFN_DIM), BF16), S_((2, 8, FFN_DIM), F32)],
        scratch_shapes=[], args=(up3, cw, cb, dact), hook=hook, name="convgate_bwd")


SSD_CT = 256


def _conv5_fn(x, w0, w1, w2, w3, w4, b):
    return jax.nn.silu(_dwconv(x, [w0, w1, w2, w3, w4], b))


def _ssd_pre_fwd(proj, cw, cb):
    def body(x_ref, cw_ref, cb_ref, o_ref):
        o_ref[...] = _conv5_fn(x_ref[...], *[cw_ref[k:k + 1, :] for k in range(5)], cb_ref[...])

    return pl.pallas_call(
        body, grid=(SSD_XBC // SSD_CT, B_LOC),
        in_specs=[pl.BlockSpec((SEQ, SSD_CT), lambda j, b: (b, j)),
                  pl.BlockSpec((8, SSD_CT), lambda j, b: (0, j)),
                  pl.BlockSpec((1, SSD_CT), lambda j, b: (0, j))],
        out_specs=pl.BlockSpec((SEQ, SSD_CT), lambda j, b: (b, j)),
        out_shape=S_((T, SSD_XBC), F32), compiler_params=_cp(), name="ssd_pre_fwd")(proj, cw, cb)


def _ssd_pre_bwd(proj, cw, cb, dxc):
    def body(x_ref, cw_ref, cb_ref, d_ref, dx_ref, dcw_ref):
        _, vjp = jax.vjp(_conv5_fn, x_ref[...], *[cw_ref[k:k + 1, :] for k in range(5)], cb_ref[...])
        dx, g0, g1, g2, g3, g4, gb = vjp(d_ref[...])
        dx_ref[...] = dx.astype(BF16)
        new = jnp.concatenate([g0, g1, g2, g3, g4, gb, jnp.zeros((2, SSD_CT), F32)], axis=0)

        @pl.when(pl.program_id(1) == 0)
        def _():
            dcw_ref[...] = new

        @pl.when(pl.program_id(1) > 0)
        def _():
            dcw_ref[...] += new

    return pl.pallas_call(
        body, grid=(SSD_XBC // SSD_CT, B_LOC),
        in_specs=[pl.BlockSpec((SEQ, SSD_CT), lambda j, b: (b, j)),
                  pl.BlockSpec((8, SSD_CT), lambda j, b: (0, j)),
                  pl.BlockSpec((1, SSD_CT), lambda j, b: (0, j)),
                  pl.BlockSpec((SEQ, SSD_CT), lambda j, b: (b, j))],
        out_specs=[pl.BlockSpec((SEQ, SSD_CT), lambda j, b: (b, j)),
                   pl.BlockSpec((8, SSD_CT), lambda j, b: (0, j))],
        out_shape=[S_((T, SSD_XBC), BF16), S_((8, SSD_XBC), F32)],
        compiler_params=_cp(), name="ssd_pre_bwd")(proj, cw, cb, dxc)


GROUP_W = 256
ONE_BUFFER = dict(pipeline_mode=pl.Buffered(1))
HEADS_PER_GROUP = 4


def _ssd_dt_fn(dt_raw, bias, alog):
    dt = _softplus(dt_raw + bias)
    return dt, dt * (-jnp.exp(alog))


def _ssd_chunk_fn(direction, group, xc, bc, cc, dt, da, prev):
    q = CHUNK
    ti = lax.broadcasted_iota(jnp.int32, (q, q), 0)
    si = lax.broadcasted_iota(jnp.int32, (q, q), 1)
    keep = (ti >= si) if direction == 0 else (ti <= si)
    mat = keep.astype(F32)
    acs = jnp.dot(mat, da, precision=HI, preferred_element_type=F32)
    acs_t = lax.dot_general(da, mat, (((0,), (1,)), ((), ())), precision=HI, preferred_element_type=F32)
    tot = jnp.sum(da, axis=0, keepdims=True)
    lane = lax.broadcasted_iota(jnp.int32, (1, 128), 1)
    sub = lax.broadcasted_iota(jnp.int32, (128, 1), 0)
    col = lax.broadcasted_iota(jnp.int32, (1, GROUP_W), 1) // HEAD
    cb = _dot_nt(cc, bc)
    lows, douts, dt_e, ein_e, cd_e = [], [], 0.0, 0.0, 0.0
    for h in range(HEADS_PER_GROUP):
        ln = 8 * direction + 4 * group + h
        oh_l = (lane == ln).astype(F32)
        oh_s = (sub == ln).astype(F32)
        cm = (col == h).astype(F32)
        a_col = jnp.sum(acs * oh_l, axis=1, keepdims=True)
        a_row = jnp.sum(acs_t * oh_s, axis=0, keepdims=True)
        tot_h = jnp.sum(tot * oh_l, axis=1, keepdims=True)
        dt_col = jnp.sum(dt * oh_l, axis=1, keepdims=True)
        lows.append(jnp.exp(jnp.where(keep, a_col - a_row, NEG_INF)))
        douts.append(jnp.exp(tot_h - a_col))
        dt_e = dt_e + cm * dt_col
        ein_e = ein_e + cm * jnp.exp(a_col)
        cd_e = cd_e + cm * jnp.exp(tot_h)
    xdt = xc * dt_e
    y = ein_e * _dot(cc, prev)
    st = 0.0
    for h in range(HEADS_PER_GROUP):
        cm = (col == h).astype(F32)
        y = y + cm * _dot(cb * lows[h], xdt)
        st = st + cm * _dot_tn(bc * douts[h], xdt)
    return y, prev * cd_e + st


def _ssd_post_fn(y, xc, z, d_exp, gain):
    y = (y + d_exp * xc) * jax.nn.silu(z)
    return y * lax.rsqrt(jnp.mean(y * y, axis=-1, keepdims=True) + NORM_EPS) * gain


def _chunk_rows(c):
    return pl.ds(pl.multiple_of(c * CHUNK, CHUNK), CHUNK)


def _ssd_scan_specs(**mode):
    return [pl.BlockSpec((SEQ, GROUP_W), lambda g, b: (b, g), **mode),
            pl.BlockSpec((SEQ, 128), lambda g, b: (b, C_B // 128 + g), **mode),
            pl.BlockSpec((SEQ, 128), lambda g, b: (b, C_C // 128 + g), **mode),
            pl.BlockSpec((SEQ, GROUP_W), lambda g, b: (b, C_Z // GROUP_W + g), **mode),
            pl.BlockSpec((SEQ, 128), lambda g, b: (b, C_DT // 128), **mode),
            pl.BlockSpec((1, 128), lambda g, b: (0, 0)),
            pl.BlockSpec((1, 128), lambda g, b: (0, 0)),
            pl.BlockSpec((1, GROUP_W), lambda g, b: (0, g)),
            pl.BlockSpec((1, GROUP_W), lambda g, b: (0, g))]


def _ssd_state_spec(**mode):
    return pl.BlockSpec((None, None, 2 * N_CHUNK, 128, GROUP_W), lambda g, b: (g, b, 0, 0, 0), **mode)


def _ssd_scan_fwd(xc, proj, dtb, alog, d_exp, gain):
    def body(x_ref, b_ref, c_ref, z_ref, dt_ref, dtb_ref, al_ref, de_ref, g_ref, o_ref, y_s, st_ref, dt_s, da_s):
        group = pl.program_id(0)
        dt, da = _ssd_dt_fn(dt_ref[...], dtb_ref[...], al_ref[...])
        dt_s[...] = dt
        da_s[...] = da
        for direction in (0, 1):
            def step(i, prev, direction=direction):
                c = i if direction == 0 else N_CHUNK - 1 - i
                rows = _chunk_rows(c)
                st_ref[direction * N_CHUNK + c] = prev
                y, nxt = _ssd_chunk_fn(direction, group, x_ref[rows, :], b_ref[rows, :], c_ref[rows, :], dt_s[rows, :], da_s[rows, :], prev)
                if direction == 0:
                    y_s[rows, :] = y
                else:
                    y_s[rows, :] += y
                return nxt

            lax.fori_loop(0, N_CHUNK, step, jnp.zeros((128, GROUP_W), F32))

        def post(c, carry):
            rows = _chunk_rows(c)
            o_ref[rows, :] = _ssd_post_fn(y_s[rows, :], x_ref[rows, :], z_ref[rows, :], de_ref[...], g_ref[...]).astype(BF16)
            return carry

        lax.fori_loop(0, N_CHUNK, post, 0)

    return pl.pallas_call(
        body, grid=(2, B_LOC), in_specs=_ssd_scan_specs(),
        out_specs=[pl.BlockSpec((SEQ, GROUP_W), lambda g, b: (b, g)), pl.BlockSpec((SEQ, GROUP_W), lambda g, b: (b, g)), _ssd_state_spec()],
        out_shape=[S_((T, SSD_INNER), BF16), S_((T, SSD_INNER), F32), S_((2, B_LOC, 2 * N_CHUNK, 128, GROUP_W), F32)],
        scratch_shapes=[pltpu.VMEM((SEQ, 128), F32), pltpu.VMEM((SEQ, 128), F32)],
        compiler_params=_cp(), name="ssd_scan_fwd")(xc, xc, xc, proj, proj, dtb, alog, d_exp, gain)


def _ssd_scan_bwd(xc, proj, dtb, alog, d_exp, gain, dy, ysum, states, hook=None):
    def body(x_ref, b_ref, c_ref, z_ref, dt_ref, dtb_ref, al_ref, de_ref, g_ref, dy_ref, ys_ref, st_s,
             dx_ref, db_ref, dc_ref, dz_ref, ddt_ref, ddtb_ref, dal_ref, dde_ref, dg_ref,
             dt_s, da_s, y_s, ddt_s, dda_s):
        group = pl.program_id(0)
        first = pl.program_id(1) == 0
        (dt, da), dt_vjp = jax.vjp(_ssd_dt_fn, dt_ref[...], dtb_ref[...], al_ref[...])
        dt_s[...] = dt
        da_s[...] = da

        def post(c, carry):
            rows = _chunk_rows(c)
            _, post_vjp = jax.vjp(_ssd_post_fn, ys_ref[rows, :], x_ref[rows, :], z_ref[rows, :], de_ref[...], g_ref[...])
            d_y, d_x_skip, d_z, g_de, g_g = post_vjp(dy_ref[rows, :])
            dz_ref[rows, :] = d_z.astype(BF16)
            dx_ref[rows, :] = d_x_skip
            y_s[rows, :] = d_y
            return carry[0] + g_de, carry[1] + g_g

        d_de, d_g = lax.fori_loop(0, N_CHUNK, post, (jnp.zeros((1, GROUP_W), F32), jnp.zeros((1, GROUP_W), F32)))
        db_ref[...] = jnp.zeros((SEQ, 128), F32)
        dc_ref[...] = jnp.zeros((SEQ, 128), F32)
        ddt_s[...] = jnp.zeros((SEQ, 128), F32)
        dda_s[...] = jnp.zeros((SEQ, 128), F32)
        for direction in (0, 1):
            def bstep(i, dnxt, direction=direction):
                c = N_CHUNK - 1 - i if direction == 0 else i
                rows = _chunk_rows(c)
                fn = functools.partial(_ssd_chunk_fn, direction, group)
                _, vjp = jax.vjp(fn, x_ref[rows, :], b_ref[rows, :], c_ref[rows, :], dt_s[rows, :], da_s[rows, :],
                                 st_s[direction * N_CHUNK + c])
                g_x, g_b, g_c, g_dt, g_da, g_prev = vjp((y_s[rows, :], dnxt))
                dx_ref[rows, :] += g_x
                db_ref[rows, :] += g_b
                dc_ref[rows, :] += g_c
                ddt_s[rows, :] += g_dt
                dda_s[rows, :] += g_da
                return g_prev

            lax.fori_loop(0, N_CHUNK, bstep, jnp.zeros((128, GROUP_W), F32))
        g_raw, g_bias, g_alog = dt_vjp((ddt_s[...], dda_s[...]))
        ddt_ref[...] = g_raw
        pad7 = jnp.zeros((7, 128), F32)
        new_b = jnp.concatenate([g_bias, pad7], axis=0)
        new_a = jnp.concatenate([g_alog, pad7], axis=0)

        @pl.when(first)
        def _():
            ddtb_ref[...] = new_b
            dal_ref[...] = new_a
            dde_ref[...] = d_de
            dg_ref[...] = d_g

        @pl.when(jnp.logical_not(first))
        def _():
            ddtb_ref[...] += new_b
            dal_ref[...] += new_a
            dde_ref[...] += d_de
            dg_ref[...] += d_g

    return _hooked_call(
        body, grid=(2, B_LOC),
        in_specs=_ssd_scan_specs(**ONE_BUFFER) + [pl.BlockSpec((SEQ, GROUP_W), lambda g, b: (b, g), **ONE_BUFFER),
                                                  pl.BlockSpec((SEQ, GROUP_W), lambda g, b: (b, g), **ONE_BUFFER),
                                                  _ssd_state_spec(**ONE_BUFFER)],
        out_specs=[pl.BlockSpec((SEQ, GROUP_W), lambda g, b: (b, g)),
                   pl.BlockSpec((SEQ, 128), lambda g, b: (b, g)),
                   pl.BlockSpec((SEQ, 128), lambda g, b: (b, g)),
                   pl.BlockSpec((SEQ, GROUP_W), lambda g, b: (b, g)),
                   pl.BlockSpec((None, SEQ, 128), lambda g, b: (g, b, 0)),
                   pl.BlockSpec((None, 8, 128), lambda g, b: (g, 0, 0)),
                   pl.BlockSpec((None, 8, 128), lambda g, b: (g, 0, 0)),
                   pl.BlockSpec((1, GROUP_W), lambda g, b: (0, g)),
                   pl.BlockSpec((1, GROUP_W), lambda g, b: (0, g))],
        out_shape=[S_((T, SSD_INNER), F32), S_((T, 256), F32), S_((T, 256), F32), S_((T, SSD_INNER), BF16),
                   S_((2, T, 128), F32), S_((2, 8, 128), F32), S_((2, 8, 128), F32),
                   S_((1, SSD_INNER), F32), S_((1, SSD_INNER), F32)],
        scratch_shapes=[pltpu.VMEM((SEQ, 128), F32), pltpu.VMEM((SEQ, 128), F32), pltpu.VMEM((SEQ, GROUP_W), F32),
                        pltpu.VMEM((SEQ, 128), F32), pltpu.VMEM((SEQ, 128), F32)],
        args=(xc, xc, xc, proj, proj, dtb, alog, d_exp, gain, dy, ysum, states), hook=hook, name="ssd_scan_bwd")


GMLP_W = 256


def _gmlp_chunk_fn(gu, gv, v_gain, w0, w1, w2, w3, b_exp):
    u = jax.nn.gelu(gu)
    v = jax.nn.gelu(gv)
    v = v * lax.rsqrt(jnp.mean(v * v, axis=-1, keepdims=True) + NORM_EPS) * v_gain
    col = lax.broadcasted_iota(jnp.int32, (1, GMLP_W), 1) // HEAD
    mixed = b_exp
    for g, w in enumerate((w0, w1, w2, w3)):
        mixed = mixed + (col == g).astype(F32) * _dot(w, v)
    return u * mixed


def _gmlp_specs():
    return [pl.BlockSpec((SEQ, GMLP_W), lambda b: (b, C_GU // GMLP_W)),
            pl.BlockSpec((SEQ, GMLP_W), lambda b: (b, C_GV // GMLP_W)),
            pl.BlockSpec((1, GMLP_W), lambda b: (0, 0)),
            pl.BlockSpec((4, CHUNK, CHUNK), lambda b: (0, 0, 0)),
            pl.BlockSpec((CHUNK, GMLP_W), lambda b: (0, 0))]


def _gmlp_fwd(proj, v_gain, w_s, b_exp):
    def body(u_ref, v_ref, g_ref, w_ref, b_ref, o_ref):
        def step(c, carry):
            rows = _chunk_rows(c)
            o_ref[rows, :] = _gmlp_chunk_fn(u_ref[rows, :], v_ref[rows, :], g_ref[...], w_ref[0], w_ref[1], w_ref[2], w_ref[3],
                                            b_ref[...]).astype(BF16)
            return carry

        lax.fori_loop(0, N_CHUNK, step, 0)

    return pl.pallas_call(body, grid=(B_LOC,), in_specs=_gmlp_specs(),
                          out_specs=pl.BlockSpec((SEQ, GMLP_W), lambda b: (b, 0)),
                          out_shape=S_((T, GMLP_W), BF16), name="gmlp_fwd")(proj, proj, v_gain, w_s, b_exp)


def _gmlp_bwd(proj, v_gain, w_s, b_exp, dy):
    def body(u_ref, v_ref, g_ref, w_ref, b_ref, dy_ref, du_ref, dv_ref, dg_ref, dw_ref, db_ref):
        @pl.when(pl.program_id(0) == 0)
        def _():
            dg_ref[...] = jnp.zeros_like(dg_ref)
            dw_ref[...] = jnp.zeros_like(dw_ref)
            db_ref[...] = jnp.zeros_like(db_ref)

        def step(c, carry):
            rows = _chunk_rows(c)
            _, vjp = jax.vjp(_gmlp_chunk_fn, u_ref[rows, :], v_ref[rows, :], g_ref[...], w_ref[0], w_ref[1], w_ref[2], w_ref[3], b_ref[...])
            g_u, g_v, g_g, g_w0, g_w1, g_w2, g_w3, g_b = vjp(dy_ref[rows, :])
            du_ref[rows, :] = g_u.astype(BF16)
            dv_ref[rows, :] = g_v.astype(BF16)
            dg_ref[...] += g_g
            db_ref[...] += g_b
            for g, gw in enumerate((g_w0, g_w1, g_w2, g_w3)):
                dw_ref[g] += gw
            return carry

        lax.fori_loop(0, N_CHUNK, step, 0)

    blk = pl.BlockSpec((SEQ, GMLP_W), lambda b: (b, 0))
    return pl.pallas_call(
        body, grid=(B_LOC,),
        in_specs=_gmlp_specs() + [pl.BlockSpec((SEQ, GMLP_W), lambda b: (b, SSD_INNER // GMLP_W))],
        out_specs=[blk, blk, pl.BlockSpec((1, GMLP_W), lambda b: (0, 0)),
                   pl.BlockSpec((4, CHUNK, CHUNK), lambda b: (0, 0, 0)), pl.BlockSpec((CHUNK, GMLP_W), lambda b: (0, 0))],
        out_shape=[S_((T, GMLP_W), BF16), S_((T, GMLP_W), BF16), S_((1, GMLP_W), F32),
                   S_((4, CHUNK, CHUNK), F32), S_((CHUNK, GMLP_W), F32)],
        name="gmlp_bwd")(proj, proj, v_gain, w_s, b_exp, dy)


PAIR_W = 128
QB = 128
KW = QB + 2 * ATTN_HALF
N_QB = SEQ // QB
PAD_ROWS = SEQ + 2 * ATTN_HALF


def _qk_norm_fn(x, gain):
    ms = jnp.dot(x * x, _head_sum_matrix(PAIR_W), precision=HI, preferred_element_type=F32) * (1.0 / HEAD)
    return x * lax.rsqrt(ms + NORM_EPS) * gain


def _deinterleave(dst_ref, src_ref, dil, offset):
    length = SEQ // dil
    if dil == 1:
        dst_ref[pl.ds(offset, SEQ), :] = src_ref[...]
        return
    for r in range(dil):
        dst_ref[pl.ds(offset + r * length, length), :] = src_ref[pl.ds(r, length, stride=dil), :]


def _interleave(dst_ref, src_ref, dil, offset):
    length = SEQ // dil
    if dil == 1:
        dst_ref[...] = src_ref[pl.ds(offset, SEQ), :]
        return
    for r in range(dil):
        dst_ref[pl.ds(r, length, stride=dil), :] = src_ref[pl.ds(offset + r * length, length), :]


def _edge_mask(blk, dil):
    length = SEQ // dil
    qi = blk * QB + lax.broadcasted_iota(jnp.int32, (QB, KW), 0)
    kj = blk * QB - ATTN_HALF + lax.broadcasted_iota(jnp.int32, (QB, KW), 1)
    return (kj >= 0) & (kj < SEQ) & ((qi // length) == (kj // length))


def _lane_is_head(hh):
    return (lax.broadcasted_iota(jnp.int32, (1, PAIR_W), 1) // HEAD) == hh


def _dilate_qkv(dil, qn_s, kn_s, v_ref, qd_s, kd_s, vd_s):
    _deinterleave(qd_s, qn_s, dil, 0)
    _deinterleave(kd_s, kn_s, dil, ATTN_HALF)
    _deinterleave(vd_s, v_ref, dil, ATTN_HALF)


def _attn_branch_fwd(br, dil, qn_s, kn_s, v_ref, bias_ref, qd_s, kd_s, vd_s, od_s, ld_s):
    _dilate_qkv(dil, qn_s, kn_s, v_ref, qd_s, kd_s, vd_s)

    def step(blk, carry):
        rows = pl.ds(pl.multiple_of(blk * QB, QB), QB)
        win = pl.ds(pl.multiple_of(blk * QB, QB), KW)
        qb, kw, vw = qd_s[rows, :], kd_s[win, :], vd_s[win, :]
        edge = _edge_mask(blk, dil)
        out, lse = 0.0, 0.0
        for hh in range(2):
            is_h = _lane_is_head(hh)
            s = _dot_nt(jnp.where(is_h, qb, 0.0), kw) * (HEAD ** -0.5) + bias_ref[br, hh]
            s = jnp.where(edge, s, NEG_INF)
            m = jnp.max(s, axis=-1, keepdims=True)
            l_h = m + jnp.log(jnp.sum(jnp.exp(s - m), axis=-1, keepdims=True))
            out = out + jnp.where(is_h, _dot(jnp.exp(s - l_h), vw), 0.0)
            lse = lse + jnp.where(is_h, l_h, 0.0)
        od_s[rows, :] = out
        ld_s[rows, :] = lse
        return carry

    lax.fori_loop(0, N_QB, step, 0)


def _attn_specs():
    col = lambda c0: (lambda p, b: (b, c0 // PAIR_W + p))
    return [pl.BlockSpec((SEQ, PAIR_W), col(C_Q)), pl.BlockSpec((SEQ, PAIR_W), col(C_K)), pl.BlockSpec((SEQ, PAIR_W), col(C_V)),
            pl.BlockSpec((1, PAIR_W), lambda p, b: (0, 0)), pl.BlockSpec((1, PAIR_W), lambda p, b: (0, 0)),
            pl.BlockSpec((3, 2, QB, KW), lambda p, b: (0, p, 0, 0))]


def _attn_scratch():
    seq = pltpu.VMEM((SEQ, PAIR_W), F32)
    pad = pltpu.VMEM((PAD_ROWS, PAIR_W), F32)
    return [seq, seq, seq, pad, pad, seq, seq]


def _zero_pads(*refs):
    for ref in refs:
        ref[pl.ds(0, ATTN_HALF), :] = jnp.zeros((ATTN_HALF, PAIR_W), F32)
        ref[pl.ds(ATTN_HALF + SEQ, ATTN_HALF), :] = jnp.zeros((ATTN_HALF, PAIR_W), F32)


ROW_STEP = 256


def _row_steps(fn, init=0):
    return lax.fori_loop(0, SEQ // ROW_STEP, lambda i, c: fn(pl.ds(pl.multiple_of(i * ROW_STEP, ROW_STEP), ROW_STEP), c), init)


def _interleave_add(acc_ref, src_ref, dil, offset):
    length = SEQ // dil
    if dil == 1:
        acc_ref[...] += src_ref[pl.ds(offset, SEQ), :]
        return
    for r in range(dil):
        acc_ref[pl.ds(r, length, stride=dil), :] += src_ref[pl.ds(offset + r * length, length), :]


def _attn_norm_qk(q_ref, k_ref, qg_ref, kg_ref, qn_s, kn_s):
    def norm(rows, carry):
        qn_s[rows, :] = _qk_norm_fn(q_ref[rows, :], qg_ref[...])
        kn_s[rows, :] = _qk_norm_fn(k_ref[rows, :], kg_ref[...])
        return carry

    _row_steps(norm)


def _attn_forward_all(q_ref, k_ref, v_ref, qg_ref, kg_ref, bias_ref, qn_s, kn_s, qd_s, kd_s, vd_s, od_s, ld_s, on_s, ln_s):
    _attn_norm_qk(q_ref, k_ref, qg_ref, kg_ref, qn_s, kn_s)
    _zero_pads(kd_s, vd_s)
    for br, dil in enumerate(ATTN_DILS):
        _attn_branch_fwd(br, dil, qn_s, kn_s, v_ref, bias_ref, qd_s, kd_s, vd_s, od_s, ld_s)
        _interleave(on_s.at[br], od_s, dil, 0)
        _interleave(ln_s.at[br], ld_s, dil, 0)


def _merge_weights(ln_s, rows):
    l0, l1, l2 = ln_s[0, rows, :], ln_s[1, rows, :], ln_s[2, rows, :]
    m = jnp.maximum(jnp.maximum(l0, l1), l2)
    e = [jnp.exp(l0 - m), jnp.exp(l1 - m), jnp.exp(l2 - m)]
    den = e[0] + e[1] + e[2]
    return [e[0] / den, e[1] / den, e[2] / den]


def _attn_fwd(proj, q_gain, k_gain, bias, hook=None):
    def body(q_ref, k_ref, v_ref, qg_ref, kg_ref, bias_ref, o_ref, on_s, ln_s, qn_s, kn_s, qd_s, kd_s, vd_s, od_s, ld_s):
        _attn_forward_all(q_ref, k_ref, v_ref, qg_ref, kg_ref, bias_ref, qn_s, kn_s, qd_s, kd_s, vd_s, od_s, ld_s, on_s, ln_s)

        def merge(rows, carry):
            w = _merge_weights(ln_s, rows)
            o_ref[rows, :] = (w[0] * on_s[0, rows, :] + w[1] * on_s[1, rows, :] + w[2] * on_s[2, rows, :]).astype(BF16)
            return carry

        _row_steps(merge)

    kept = pl.BlockSpec((3, SEQ, PAIR_W), lambda p, b: (0, b, p))
    return _hooked_call(body, grid=(2, B_LOC), in_specs=_attn_specs(),
                        out_specs=[pl.BlockSpec((SEQ, PAIR_W), lambda p, b: (b, p)), kept, kept],
                        out_shape=[S_((T, 2 * PAIR_W), BF16), S_((3, T, 2 * PAIR_W), F32), S_((3, T, 2 * PAIR_W), F32)],
                        scratch_shapes=_attn_scratch(), args=(proj, proj, proj, q_gain, k_gain, bias), hook=hook, name="attn_fwd")


def _attn_bwd(proj, q_gain, k_gain, bias, dy, kept_o, kept_l):
    def body(q_ref, k_ref, v_ref, qg_ref, kg_ref, bias_ref, dy_ref, on_ref, ln_ref,
             dq_ref, dk_ref, dv_ref, dqg_ref, dkg_ref, dbias_ref,
             qn_s, kn_s, qd_s, kd_s, vd_s, od_s, ld_s, don_s, dln_s, dod_s, dld_s, dqd_s, dkd_s, dvd_s, dqn_s, dkn_s, dvn_s):
        first = pl.program_id(1) == 0
        _attn_norm_qk(q_ref, k_ref, qg_ref, kg_ref, qn_s, kn_s)
        _zero_pads(kd_s, vd_s)

        def clear_acc(rows, carry):
            dqn_s[rows, :] = jnp.zeros((ROW_STEP, PAIR_W), F32)
            dkn_s[rows, :] = jnp.zeros((ROW_STEP, PAIR_W), F32)
            dvn_s[rows, :] = jnp.zeros((ROW_STEP, PAIR_W), F32)
            return carry

        _row_steps(clear_acc)

        @pl.when(first)
        def _():
            dbias_ref[...] = jnp.zeros_like(dbias_ref)

        for br, dil in enumerate(ATTN_DILS):
            def merge_bwd(rows, carry, br=br):
                w = _merge_weights(ln_ref, rows)
                dy = dy_ref[rows, :]
                same_head = _head_sum_matrix(PAIR_W)
                dws = [jnp.dot(dy * on_ref[j, rows, :], same_head, precision=HI, preferred_element_type=F32) for j in range(3)]
                dbar = w[0] * dws[0] + w[1] * dws[1] + w[2] * dws[2]
                don_s[rows, :] = w[br] * dy
                dln_s[rows, :] = w[br] * (dws[br] - dbar)
                return carry

            _row_steps(merge_bwd)
            _dilate_qkv(dil, qn_s, kn_s, v_ref, qd_s, kd_s, vd_s)
            _deinterleave(od_s, on_ref.at[br], dil, 0)
            _deinterleave(ld_s, ln_ref.at[br], dil, 0)
            _deinterleave(dod_s, don_s, dil, 0)
            _deinterleave(dld_s, dln_s, dil, 0)

            def clear(rows, carry):
                dkd_s[rows, :] = jnp.zeros((ROW_STEP, PAIR_W), F32)
                dvd_s[rows, :] = jnp.zeros((ROW_STEP, PAIR_W), F32)
                return carry

            _row_steps(clear)
            tail = pl.ds(SEQ, 2 * ATTN_HALF)
            dkd_s[tail, :] = jnp.zeros((2 * ATTN_HALF, PAIR_W), F32)
            dvd_s[tail, :] = jnp.zeros((2 * ATTN_HALF, PAIR_W), F32)

            def step(blk, carry, br=br, dil=dil):
                rows = pl.ds(pl.multiple_of(blk * QB, QB), QB)
                win = pl.ds(pl.multiple_of(blk * QB, QB), KW)
                qb, kw, vw = qd_s[rows, :], kd_s[win, :], vd_s[win, :]
                do_b, dl_b, o_b, l_b = dod_s[rows, :], dld_s[rows, :], od_s[rows, :], ld_s[rows, :]
                edge = _edge_mask(blk, dil)
                dq, dk, dv = 0.0, 0.0, 0.0
                for hh in range(2):
                    is_h = _lane_is_head(hh)
                    pick = (lax.broadcasted_iota(jnp.int32, (1, PAIR_W), 1) == hh * HEAD).astype(F32)
                    q_h = jnp.where(is_h, qb, 0.0)
                    do_h = jnp.where(is_h, do_b, 0.0)
                    s = _dot_nt(q_h, kw) * (HEAD ** -0.5) + bias_ref[br, hh]
                    s = jnp.where(edge, s, NEG_INF)
                    p = jnp.exp(s - jnp.sum(l_b * pick, axis=-1, keepdims=True))
                    dp = _dot_nt(do_h, vw)
                    delta = jnp.sum(do_h * o_b, axis=-1, keepdims=True)
                    ds = p * (dp - delta + jnp.sum(dl_b * pick, axis=-1, keepdims=True))
                    dbias_ref[br, hh] += ds
                    dq = dq + jnp.where(is_h, _dot(ds, kw), 0.0) * (HEAD ** -0.5)
                    dk = dk + _dot_tn(ds, q_h) * (HEAD ** -0.5)
                    dv = dv + _dot_tn(p, do_h)
                dqd_s[rows, :] = dq
                dkd_s[win, :] += dk
                dvd_s[win, :] += dv
                return carry

            lax.fori_loop(0, N_QB, step, 0)
            _interleave_add(dqn_s, dqd_s, dil, 0)
            _interleave_add(dkn_s, dkd_s, dil, ATTN_HALF)
            _interleave_add(dvn_s, dvd_s, dil, ATTN_HALF)

        def norm_bwd(rows, carry):
            _, q_vjp = jax.vjp(_qk_norm_fn, q_ref[rows, :], qg_ref[...])
            _, k_vjp = jax.vjp(_qk_norm_fn, k_ref[rows, :], kg_ref[...])
            g_q, g_qg = q_vjp(dqn_s[rows, :])
            g_k, g_kg = k_vjp(dkn_s[rows, :])
            dq_ref[rows, :] = g_q.astype(BF16)
            dk_ref[rows, :] = g_k.astype(BF16)
            dv_ref[rows, :] = dvn_s[rows, :].astype(BF16)
            return carry[0] + g_qg, carry[1] + g_kg

        g_qg, g_kg = _row_steps(norm_bwd, (jnp.zeros((1, PAIR_W), F32), jnp.zeros((1, PAIR_W), F32)))
        pad7 = jnp.zeros((7, PAIR_W), F32)
        new_q = jnp.concatenate([g_qg, pad7], axis=0)
        new_k = jnp.concatenate([g_kg, pad7], axis=0)

        @pl.when(first)
        def _():
            dqg_ref[...] = new_q
            dkg_ref[...] = new_k

        @pl.when(jnp.logical_not(first))
        def _():
            dqg_ref[...] += new_q
            dkg_ref[...] += new_k

    seq = pltpu.VMEM((SEQ, PAIR_W), F32)
    pad = pltpu.VMEM((PAD_ROWS, PAIR_W), F32)
    kept = pl.BlockSpec((3, SEQ, PAIR_W), lambda p, b: (0, b, p))
    out_blk = pl.BlockSpec((SEQ, PAIR_W), lambda p, b: (b, p))
    gain_blk = pl.BlockSpec((None, 8, PAIR_W), lambda p, b: (p, 0, 0))
    return pl.pallas_call(
        body, grid=(2, B_LOC),
        in_specs=_attn_specs() + [pl.BlockSpec((SEQ, PAIR_W), lambda p, b: (b, (SSD_INNER + GMLP_W) // PAIR_W + p)), kept, kept],
        out_specs=[out_blk, out_blk, out_blk, gain_blk, gain_blk, pl.BlockSpec((3, 2, QB, KW), lambda p, b: (0, p, 0, 0))],
        out_shape=[S_((T, 2 * PAIR_W), BF16)] * 3 + [S_((2, 8, PAIR_W), F32)] * 2 + [S_((3, 4, QB, KW), F32)],
        scratch_shapes=_attn_scratch() + [seq, seq, seq, seq, seq, pad, pad, seq, seq, seq],
        compiler_params=_cp(), name="attn_bwd")(proj, proj, proj, q_gain, k_gain, bias, dy, kept_o, kept_l)


def _rel_bucket(rel):
    nb = 16
    max_exact = nb // 2
    n = jnp.abs(rel)
    large = max_exact + (jnp.log(jnp.maximum(n, 1).astype(F32) / max_exact) / math.log(1024 / max_exact) * (nb - max_exact)).astype(jnp.int32)
    large = jnp.minimum(large, nb - 1)
    return jnp.where(rel > 0, nb, 0) + jnp.where(n < max_exact, n, large)


def _attn_bias(rel_table):
    rel = jnp.arange(KW)[None, :] - ATTN_HALF - jnp.arange(QB)[:, None]
    inside = (jnp.abs(rel) <= ATTN_HALF)
    out = []
    for dil in ATTN_DILS:
        one_hot = (_rel_bucket(rel * dil)[None] == jnp.arange(32)[:, None, None]).astype(F32)
        b = jnp.einsum("kh,kts->hts", rel_table, one_hot, precision=HI)
        out.append(jnp.where(inside[None], b, NEG_INF))
    return jnp.stack(out).astype(F32)


def _place():
    return lax.axis_index("x"), lax.axis_index("y"), lax.axis_index("c")


def _allgather8(buf, name):
    rows = buf.shape[0]
    flips = [(fx, fy, fc) for fx in (0, 1) for fy in (0, 1) for fc in (0, 1)][1:]

    def body(in_ref, out_ref, send_sems, recv_sems, local_sem):
        x, y, c = _place()
        me = 4 * x + 2 * y + c
        mine = pltpu.make_async_copy(in_ref, out_ref.at[me], local_sem)
        mine.start()
        peers = [(1 - x if fx else x, 1 - y if fy else y, 1 - c if fc else c) for fx, fy, fc in flips]

        def copy(k, slot, peer):
            return pltpu.make_async_remote_copy(src_ref=in_ref, dst_ref=out_ref.at[slot], send_sem=send_sems.at[k],
                                                recv_sem=recv_sems.at[k], device_id=peer, device_id_type=MESH)

        sends = [copy(k, me, peer) for k, peer in enumerate(peers)]
        for cp in sends:
            cp.start()
        for k, (px, py, pc) in enumerate(peers):
            copy(k, 4 * px + 2 * py + pc, (px, py, pc)).wait_recv()
        for cp in sends:
            cp.wait_send()
        mine.wait()

    return pl.pallas_call(body, in_specs=[ANY], out_specs=ANY, out_shape=S_((N_DEV, rows, 128), F32),
                          scratch_shapes=[pltpu.SemaphoreType.DMA((7,)), pltpu.SemaphoreType.DMA((7,)), pltpu.SemaphoreType.DMA(())],
                          name=name)(buf)


N_BIG = 4


def _other_chips(x, y):
    return [(1 - x, y), (x, 1 - y), (1 - x, 1 - y)]


def _hooked_call(body, *, grid, in_specs, out_specs, out_shape, scratch_shapes, args, hook, name):
    if hook is None:
        res = pl.pallas_call(body, grid=grid, in_specs=in_specs, out_specs=out_specs, out_shape=out_shape,
                             scratch_shapes=scratch_shapes, compiler_params=_cp(), name=name)(*args)
        return res, None
    counts = (len(in_specs), len(hook["arrays"]), len(out_specs), len(hook["out_shape"]), len(scratch_shapes), len(hook["sems"]))

    def wrapped(*refs):
        groups, pos = [], 0
        for n in counts:
            groups.append(refs[pos:pos + n])
            pos += n
        ins, h_ins, outs, h_outs, scr, sems = groups
        idx = [pl.program_id(a) for a in range(len(grid))]
        first = functools.reduce(jnp.logical_and, [i == 0 for i in idx])
        last = functools.reduce(jnp.logical_and, [i == g - 1 for i, g in zip(idx, grid)])

        @pl.when(first)
        def _():
            hook["start"](h_ins, h_outs, sems)

        body(*ins, *outs, *scr)

        @pl.when(last)
        def _():
            hook["finish"](h_ins, h_outs, sems)

    res = pl.pallas_call(wrapped, grid=grid, in_specs=list(in_specs) + [ANY] * counts[1], out_specs=list(out_specs) + [ANY] * counts[3],
                         out_shape=list(out_shape) + list(hook["out_shape"]), scratch_shapes=list(scratch_shapes) + list(hook["sems"]),
                         compiler_params=_cp(), name=name + "_" + hook["name"])(*args, *hook["arrays"])
    return res[:counts[2]], res[counts[2]:]


def _run_hook(hook):
    n_in, n_out = len(hook["arrays"]), len(hook["out_shape"])

    def body(*refs):
        h_ins, h_outs, sems = refs[:n_in], refs[n_in:n_in + n_out], refs[n_in + n_out:]
        hook["start"](h_ins, h_outs, sems)
        hook["finish"](h_ins, h_outs, sems)

    return pl.pallas_call(body, in_specs=[ANY] * n_in, out_specs=[ANY] * n_out, out_shape=list(hook["out_shape"]),
                          scratch_shapes=list(hook["sems"]), name=hook["name"])(*hook["arrays"])


def _remote(src, dst, send_sem, recv_sem, peer):
    return pltpu.make_async_remote_copy(src_ref=src, dst_ref=dst, send_sem=send_sem, recv_sem=recv_sem, device_id=peer, device_id_type=MESH)


def _gather_hook(shards):
    def copies(h_ins, h_outs, sems, kind):
        ici_send, ici_recv, d2d_send, d2d_recv = sems
        x, y, c = _place()
        chip = 2 * x + y
        out = []
        for t in range(N_BIG):
            half = shards[t].shape[0] // 2
            mine_r, other_r = pl.ds(c * half, half), pl.ds((1 - c) * half, half)
            for f, (px, py) in enumerate(_other_chips(x, y)):
                k, peer_chip = 3 * t + f, 2 * px + py
                if kind in ("send", "land"):
                    slot = chip if kind == "send" else peer_chip
                    out.append(_remote(h_ins[t].at[mine_r], h_outs[t].at[slot, mine_r], ici_send.at[k], ici_recv.at[k], (px, py, c)))
                else:
                    rows = mine_r if kind == "pass" else other_r
                    out.append(_remote(h_outs[t].at[peer_chip, rows], h_outs[t].at[peer_chip, rows], d2d_send.at[k], d2d_recv.at[k],
                                       (x, y, 1 - c)))
        return out

    def start(h_ins, h_outs, sems):
        for cp in copies(h_ins, h_outs, sems, "send"):
            cp.start()

    def finish(h_ins, h_outs, sems):
        passed = copies(h_ins, h_outs, sems, "pass")
        for landed, forward in zip(copies(h_ins, h_outs, sems, "land"), passed):
            landed.wait_recv()
            forward.start()
        for cp in copies(h_ins, h_outs, sems, "get"):
            cp.wait_recv()
        for cp in copies(h_ins, h_outs, sems, "send") + passed:
            cp.wait_send()

    return dict(name="gather", arrays=list(shards), out_shape=[S_((N_CHIPS,) + s.shape, s.dtype) for s in shards],
                sems=[pltpu.SemaphoreType.DMA((3 * N_BIG,)) for _ in range(4)], start=start, finish=finish)


def _to_sibling_hook(parts):
    def copies(h_ins, h_outs, sems):
        x, y, c = _place()
        return [_remote(h_ins[t], h_outs[t], sems[0].at[t], sems[1].at[t], (x, y, 1 - c)) for t in range(N_BIG)]

    def start(h_ins, h_outs, sems):
        for cp in copies(h_ins, h_outs, sems):
            cp.start()

    def finish(h_ins, h_outs, sems):
        cps = copies(h_ins, h_outs, sems)
        for cp in cps:
            cp.wait_recv()
        for cp in cps:
            cp.wait_send()

    return dict(name="to_sibling", arrays=list(parts), out_shape=[S_(p.shape, p.dtype) for p in parts],
                sems=[pltpu.SemaphoreType.DMA((N_BIG,)), pltpu.SemaphoreType.DMA((N_BIG,))], start=start, finish=finish)


def _to_chips_hook(parts):
    def copies(h_ins, h_outs, sems):
        x, y, c = _place()
        return [_remote(h_ins[t].at[2 * px + py], h_outs[t].at[f], sems[0].at[3 * t + f], sems[1].at[3 * t + f], (px, py, c))
                for t in range(N_BIG) for f, (px, py) in enumerate(_other_chips(x, y))]

    def start(h_ins, h_outs, sems):
        for cp in copies(h_ins, h_outs, sems):
            cp.start()

    def finish(h_ins, h_outs, sems):
        cps = copies(h_ins, h_outs, sems)
        for cp in cps:
            cp.wait_recv()
        for cp in cps:
            cp.wait_send()

    return dict(name="to_chips", arrays=list(parts), out_shape=[S_((3,) + p.shape[1:], p.dtype) for p in parts],
                sems=[pltpu.SemaphoreType.DMA((3 * N_BIG,)), pltpu.SemaphoreType.DMA((3 * N_BIG,))], start=start, finish=finish)


def _add_pair(a, b, rows, name):
    n, r, c = a.shape

    def body(a_ref, b_ref, o_ref):
        o_ref[...] = (a_ref[...].astype(F32) + b_ref[...].astype(F32)).astype(BF16)

    blk = pl.BlockSpec((None, rows, c), lambda i, j: (i, j, 0))
    return pl.pallas_call(body, grid=(n, r // rows), in_specs=[blk, blk], out_specs=blk, out_shape=S_(a.shape, BF16), name=name)(a, b)


def _add_four(own, got, rows, name):
    n, r, c = own.shape

    def body(a_ref, g_ref, o_ref):
        o_ref[...] = ((a_ref[...].astype(F32) + g_ref[0].astype(F32)) + g_ref[1].astype(F32)) + g_ref[2].astype(F32)

    blk = pl.BlockSpec((None, rows, c), lambda i, j: (i, j, 0))
    return pl.pallas_call(body, grid=(n, r // rows), in_specs=[blk, pl.BlockSpec((3, None, rows, c), lambda i, j: (0, i, j, 0))],
                          out_specs=blk, out_shape=S_(own.shape, F32), name=name)(own, got)


def _sum_slots(slots):
    rows = slots.shape[1]

    def body(s_ref, o_ref):
        tot = s_ref[0]
        for k in range(1, N_DEV):
            tot = tot + s_ref[k]
        o_ref[...] = tot

    return pl.pallas_call(body, out_shape=S_((rows, 128), F32), name="sum_slots")(slots)


SMALL = ("mix_norm_gain", "ssd_conv_w", "ssd_conv_b", "ssd_dt_bias", "ssd_a_log", "ssd_d", "ssd_out_gain", "gmlp_v_gain",
         "gmlp_w_s", "gmlp_b_s", "attn_q_gain", "attn_k_gain", "rel_bias_table", "ffn_norm_gain", "ffn_conv_w", "ffn_conv_b")
BIG = ("w_in", "w_out", "ffn_w_up", "ffn_w_down")
WEIGHTS = ("mix_norm_gain", "w_in", "ssd_conv_w", "ssd_conv_b", "ssd_dt_bias", "ssd_a_log", "ssd_d", "ssd_out_gain", "gmlp_v_gain",
           "gmlp_w_s", "gmlp_b_s", "attn_q_gain", "attn_k_gain", "rel_bias_table", "w_out", "ffn_norm_gain", "ffn_w_up",
           "ffn_conv_w", "ffn_conv_b", "ffn_w_down")
ADAM_ROWS = {"w_in": 512, "w_out": 512, "ffn_w_up": 256, "ffn_w_down": 352}


PACK_ROWS = 64


def _packed_rows(shape):
    return -(-int(np.prod(shape)) // 1024) * 8


def _pack(arrays):
    parts = []
    for a in arrays:
        rows = _packed_rows(a.shape)
        flat = a.reshape(-1).astype(F32)
        parts.append(jnp.pad(flat, (0, rows * 128 - flat.shape[0])).reshape(rows, 128))
    total = sum(p.shape[0] for p in parts)
    tail = -total % PACK_ROWS
    if tail:
        parts.append(jnp.zeros((tail, 128), F32))
    return jnp.concatenate(parts, axis=0)


def _unpack(buf, shapes):
    out, row = [], 0
    for s in shapes:
        rows, n = _packed_rows(s), int(np.prod(s))
        out.append(buf[row:row + rows].reshape(-1)[:n].reshape(s))
        row += rows
    return out


def _perm_cols(w):
    pad = jnp.zeros(w.shape[:-1] + (NP - IN_WIDTH,), w.dtype)
    return jnp.concatenate([w[..., :1536], w[..., 1552:], w[..., 1536:1552], pad], axis=-1)


def _unperm_cols(w):
    return jnp.concatenate([w[..., :1536], w[..., C_DT:C_DT + 16], w[..., 1536:C_DT]], axis=-1)


def _layer_params(l, p, conv5_w, conv3_w, bias):
    def make(mix_g, conv5, conv5_b, dt_bias, a_log, d_skip, out_gain, v_gain, w_s, b_s, q_gain, k_gain, ffn_g, conv3, conv3_b):
        lanes = lambda a: jnp.pad(a.reshape(1, 16), ((0, 0), (0, 112)))
        cw3 = jnp.pad(jnp.transpose(conv3.reshape(3, 2, FFN_DIM), (1, 0, 2)), ((0, 0), (0, 5), (0, 0)))
        return dict(mix_g=mix_g.reshape(1, D_MODEL), cw5=jnp.pad(conv5, ((0, 3), (0, 0))), cb5=conv5_b.reshape(1, SSD_XBC),
                    dtb=lanes(dt_bias), alog=lanes(a_log), d_exp=jnp.repeat(d_skip, HEAD).reshape(1, SSD_INNER),
                    out_gain=out_gain.reshape(1, SSD_INNER), v_gain=v_gain.reshape(1, GMLP_W), w_s=w_s,
                    b_exp=jnp.repeat(b_s.T, HEAD, axis=1), q_gain=jnp.tile(q_gain, 2).reshape(1, PAIR_W),
                    k_gain=jnp.tile(k_gain, 2).reshape(1, PAIR_W), ffn_g=ffn_g.reshape(1, D_MODEL), cw3=cw3,
                    cb3=conv3_b.reshape(2, 1, FFN_DIM))

    args = (p["mix_norm_gain"][l], conv5_w[l], p["ssd_conv_b"][l], p["ssd_dt_bias"][l], p["ssd_a_log"][l], p["ssd_d"][l],
            p["ssd_out_gain"][l], p["gmlp_v_gain"][l], p["gmlp_w_s"][l], p["gmlp_b_s"][l], p["attn_q_gain"][l], p["attn_k_gain"][l],
            p["ffn_norm_gain"][l], conv3_w[l], p["ffn_conv_b"][l])
    return jax.vjp(make, *args)


def _forward_layer(x, lp, w_in, w_out, w_up, w_down, bias, gather=None):
    h = _rmsnorm_fwd(x, lp["mix_g"], "rmsnorm_fwd")
    proj = _mm_nn(h, w_in, tm=1024, tn=1024, tk=1024, out_dtype=F32, name="mm_proj")
    xc = _ssd_pre_fwd(proj, lp["cw5"], lp["cb5"])
    y_ssd, ssd_sum, ssd_states = _ssd_scan_fwd(xc, proj, lp["dtb"], lp["alog"], lp["d_exp"], lp["out_gain"])
    y_gmlp = _gmlp_fwd(proj, lp["v_gain"], lp["w_s"], lp["b_exp"])
    (y_attn, attn_o, attn_l), gathered = _attn_fwd(proj, lp["q_gain"], lp["k_gain"], bias, hook=gather)
    y = jnp.concatenate([y_ssd, y_gmlp, y_attn], axis=1)
    x2 = _mm_nn(y, w_out, tm=1024, tn=1024, tk=1024, out_dtype=F32, res=x, name="mm_out")
    hn = _rmsnorm_fwd(x2, lp["ffn_g"], "rmsnorm_fwd")
    up3 = _mm_up(hn, w_up)
    act = _convgate_fwd(up3, lp["cw3"], lp["cb3"])
    x3 = _mm_nn(act, w_down, tm=1024, tn=1024, tk=HALF_TILE, out_dtype=F32, res=x2, name="mm_down")
    return x3, dict(x=x, h=h, proj=proj, xc=xc, y=y, x2=x2, hn=hn, up3=up3, act=act, attn_o=attn_o, attn_l=attn_l,
                    ssd_sum=ssd_sum, ssd_states=ssd_states), gathered


def _backward_layer(dx3, sv, lp, w_in, w_out, w_up, w_down, bias, pending=None):
    d_act = _mm_nt(dx3, w_down, tm=1024, tn=HALF_TILE, tk=1024, out_dtype=F32, name="mm_dact")
    dw_down = _mm_tn(sv["act"], dx3, tm=HALF_TILE, tn=1024, tk=1024, out_dtype=BF16, name="mm_dwdown")
    (dup3, dcw3), from_sibling = _convgate_bwd(sv["up3"], lp["cw3"], lp["cb3"], d_act, hook=pending.sibling_hook() if pending else None)
    if pending:
        pending.add_sibling(from_sibling)
    d_hn = _mm_dhn(dup3, w_up)
    dw_up = _mm_dwup(sv["hn"], dup3)
    dx2, d_ffn_g = _rmsnorm_bwd(sv["x2"], lp["ffn_g"], d_hn, dx3, "rmsnorm_bwd")
    d_y = _mm_nt(dx2, w_out, tm=1024, tn=1024, tk=1024, out_dtype=F32, name="mm_dy")
    dw_out = _mm_tn(sv["y"], dx2, tm=1024, tn=1024, tk=1024, out_dtype=BF16, name="mm_dwout")
    proj, xc = sv["proj"], sv["xc"]
    (dxs, dbc, dcc, dz, ddt2, ddtb2, dal2, d_dexp, d_outg), from_chips = _ssd_scan_bwd(
        xc, proj, lp["dtb"], lp["alog"], lp["d_exp"], lp["out_gain"], d_y, sv["ssd_sum"], sv["ssd_states"],
        hook=pending.chips_hook() if pending else None)
    if pending:
        pending.add_chips(from_chips)
    d_xbc, dcw5 = _ssd_pre_bwd(proj, lp["cw5"], lp["cb5"], jnp.concatenate([dxs, dbc, dcc], axis=1))
    d_gu, d_gv, d_vg, d_ws, d_bexp = _gmlp_bwd(proj, lp["v_gain"], lp["w_s"], lp["b_exp"], d_y)
    d_q, d_k, d_v, d_qg2, d_kg2, d_bias = _attn_bwd(proj, lp["q_gain"], lp["k_gain"], bias, d_y, sv["attn_o"], sv["attn_l"])
    d_dt = (ddt2[0] + ddt2[1]).astype(BF16)
    d_proj = jnp.concatenate([d_xbc, dz, d_gu, d_gv, d_q, d_k, d_v, d_dt, jnp.zeros((T, NP - C_DT - 128), BF16)], axis=1)
    d_h = _mm_nt(d_proj, w_in, tm=1024, tn=1024, tk=1024, out_dtype=F32, name="mm_dh")
    dw_in = _mm_tn(sv["h"], d_proj, tm=1024, tn=1024, tk=1024, out_dtype=BF16, name="mm_dwin")
    dx, d_mix_g = _rmsnorm_bwd(sv["x"], lp["mix_g"], d_h, dx2, "rmsnorm_bwd")
    d_lp = dict(mix_g=d_mix_g, cw5=dcw5[:8] * (jnp.arange(8) < 5)[:, None].astype(F32), cb5=dcw5[5:6],
                dtb=(ddtb2[0, :1] + ddtb2[1, :1]), alog=(dal2[0, :1] + dal2[1, :1]), d_exp=d_dexp, out_gain=d_outg,
                v_gain=d_vg, w_s=d_ws, b_exp=d_bexp, q_gain=d_qg2[0, :1] + d_qg2[1, :1], k_gain=d_kg2[0, :1] + d_kg2[1, :1],
                ffn_g=d_ffn_g, cw3=dcw3 * (jnp.arange(8) < 3)[None, :, None].astype(F32), cb3=dcw3[:, 3:4])
    return dx, (dw_in, dw_out, dw_up, dw_down), d_lp, d_bias


def _to_shard_major(name, dw):
    if name == "w_in":
        dw = _unperm_cols(dw)
    r, c = dw.shape
    if name in ("w_in", "ffn_w_up"):
        return jnp.transpose(dw.reshape(r, N_CHIPS, c // N_CHIPS), (1, 0, 2))
    return dw.reshape(N_CHIPS, r // N_CHIPS, c)


def _whole_weight(name, gathered, own, chip):
    parts = [jnp.where(chip == k, own, gathered[k]) for k in range(N_CHIPS)]
    if name in ("w_in", "ffn_w_up"):
        w = jnp.concatenate(parts, axis=1)
        return _perm_cols(w) if name == "w_in" else w
    return jnp.concatenate(parts, axis=0)


class _LayerReduce:
    def __init__(self, dws, chip, core):
        self.chip = chip
        parts = [_to_shard_major(n, dw) for n, dw in zip(BIG, dws)]
        halves = [g.shape[1] // 2 for g in parts]
        self.mine = [lax.dynamic_slice_in_dim(g, h * core, h, axis=1) for g, h in zip(parts, halves)]
        self.theirs = [lax.dynamic_slice_in_dim(g, h * (1 - core), h, axis=1) for g, h in zip(parts, halves)]

    def sibling_hook(self):
        return _to_sibling_hook(self.theirs)

    def add_sibling(self, got):
        self.sums = [_add_pair(a, b, a.shape[1], "add_pair_" + n) for n, a, b in zip(BIG, self.mine, got)]

    def chips_hook(self):
        return _to_chips_hook(self.sums)

    def add_chips(self, got):
        self.half = []
        for n, s2, g3 in zip(BIG, self.sums, got):
            own = lax.dynamic_index_in_dim(s2, self.chip, axis=0, keepdims=True)
            self.half.append(_add_four(own, g3[:, None], own.shape[1], "add_four_" + n)[0])

    def run_alone(self):
        self.add_sibling(_run_hook(self.sibling_hook()))
        self.add_chips(_run_hook(self.chips_hook()))


def _join_halves(halves, core):
    other = _run_hook(dict(_to_sibling_hook(halves), name="swap_halves"))
    out = []
    for mine_h, other_h in zip(halves, other):
        both = jnp.stack([mine_h, other_h])
        first = lax.dynamic_index_in_dim(both, core, axis=0, keepdims=False)
        second = lax.dynamic_index_in_dim(both, 1 - core, axis=0, keepdims=False)
        out.append(jnp.concatenate([first, second], axis=1))
    return out


LAYER_SMALL = ("mix_norm_gain", "ssd_conv_w", "ssd_conv_b", "ssd_dt_bias", "ssd_a_log", "ssd_d", "ssd_out_gain", "gmlp_v_gain",
               "gmlp_w_s", "gmlp_b_s", "attn_q_gain", "attn_k_gain", "ffn_norm_gain", "ffn_conv_w", "ffn_conv_b")


def _local_grads(x, loss_target, p, conv5_w, conv3_w, full, exchange=None):
    bias, bias_vjp = jax.vjp(_attn_bias, p["rel_bias_table"])
    xt = x.reshape(T, D_MODEL)
    saved, lps, lp_vjps = [], [], []
    for l in range(DEPTH):
        lp, lp_vjp = _layer_params(l, p, conv5_w, conv3_w, bias)
        ahead = exchange is not None and l + 1 < DEPTH
        xt, sv, gathered = _forward_layer(xt, lp, full["w_in"][l], full["w_out"][l], full["ffn_w_up"][l], full["ffn_w_down"][l], bias,
                                          gather=_gather_hook(exchange[2][l + 1]) if ahead else None)
        if ahead:
            for n, g, o in zip(BIG, gathered, exchange[2][l + 1]):
                full[n].append(_whole_weight(n, g, o, exchange[0]))
        saved.append(sv)
        lps.append(lp)
        lp_vjps.append(lp_vjp)
    dxt, loss_parts = _loss_head(xt, loss_target.reshape(T, D_MODEL))
    loss_local = jnp.sum(loss_parts[::8, 0])

    big_grads = [None] * DEPTH
    small_layers = [None] * DEPTH
    d_bias_tot = jnp.zeros_like(bias)
    pending = None
    for l in reversed(range(DEPTH)):
        dxt, big_grads[l], d_lp, d_bias = _backward_layer(dxt, saved[l], lps[l], full["w_in"][l], full["w_out"][l],
                                                          full["ffn_w_up"][l], full["ffn_w_down"][l], bias, pending=pending)
        if exchange is not None:
            pending = big_grads[l] = _LayerReduce(big_grads[l], exchange[0], exchange[1])
        small_layers[l] = lp_vjps[l](d_lp)
        d_bias_tot = d_bias_tot + d_bias
    if pending is not None:
        pending.run_alone()
    (d_rel_table,) = bias_vjp(d_bias_tot)
    local_small = {n: jnp.stack([small_layers[l][i] for l in range(DEPTH)]) for i, n in enumerate(LAYER_SMALL)}
    local_small["rel_bias_table"] = d_rel_table
    return dxt, loss_local, big_grads, local_small


def kernel(x, mix_norm_gain, w_in, ssd_conv_w, ssd_conv_b, ssd_dt_bias, ssd_a_log, ssd_d, ssd_out_gain, gmlp_v_gain, gmlp_w_s, gmlp_b_s, attn_q_gain, attn_k_gain, rel_bias_table, w_out, ffn_norm_gain, ffn_w_up, ffn_conv_w, ffn_conv_b, ffn_w_down, loss_target, m_mix_norm_gain, m_w_in, m_ssd_conv_w, m_ssd_conv_b, m_ssd_dt_bias, m_ssd_a_log, m_ssd_d, m_ssd_out_gain, m_gmlp_v_gain, m_gmlp_w_s, m_gmlp_b_s, m_attn_q_gain, m_attn_k_gain, m_rel_bias_table, m_w_out, m_ffn_norm_gain, m_ffn_w_up, m_ffn_conv_w, m_ffn_conv_b, m_ffn_w_down, v_mix_norm_gain, v_w_in, v_ssd_conv_w, v_ssd_conv_b, v_ssd_dt_bias, v_ssd_a_log, v_ssd_d, v_ssd_out_gain, v_gmlp_v_gain, v_gmlp_w_s, v_gmlp_b_s, v_attn_q_gain, v_attn_k_gain, v_rel_bias_table, v_w_out, v_ffn_norm_gain, v_ffn_w_up, v_ffn_conv_w, v_ffn_conv_b, v_ffn_w_down):
    env = dict(locals())
    p = {n: env[n] for n in WEIGHTS}
    chip = 2 * lax.axis_index("x") + lax.axis_index("y")
    core = lax.axis_index("c")

    conv_slots = _allgather8(_pack([ssd_conv_w, ffn_conv_w]), "allgather_conv")
    conv_parts = [_unpack(conv_slots[2 * k], [ssd_conv_w.shape, ffn_conv_w.shape]) for k in range(N_CHIPS)]
    conv5_w = jnp.concatenate([cp[0] for cp in conv_parts], axis=-1)
    conv3_w = jnp.concatenate([cp[1] for cp in conv_parts], axis=-1)
    own = [[p[n][l].astype(BF16) for n in BIG] for l in range(DEPTH)]
    first = _run_hook(_gather_hook(own[0]))
    full = {n: [_whole_weight(n, g, o, chip)] for n, g, o in zip(BIG, first, own[0])}

    dxt, loss_local, reduced, local_small = _local_grads(x, loss_target, p, conv5_w, conv3_w, full, exchange=(chip, core, own))

    small_shapes = [local_small[n].shape for n in SMALL] + [(1,)]
    slots = _allgather8(_pack([local_small[n] for n in SMALL] + [loss_local.reshape(1)]), "allgather_small")
    summed = _unpack(_sum_slots(slots), small_shapes)
    grads = dict(zip(SMALL, summed[:-1]))
    loss = summed[-1][0]
    grads["ssd_conv_w"] = lax.dynamic_slice_in_dim(grads["ssd_conv_w"], chip * 256, 256, axis=2)
    grads["ffn_conv_w"] = lax.dynamic_slice_in_dim(grads["ffn_conv_w"], chip * (2 * FFN_DIM // N_CHIPS), 2 * FFN_DIM // N_CHIPS, axis=2)

    halves = [jnp.stack([reduced[l].half[i] for l in range(DEPTH)]) for i in range(N_BIG)]
    for n, g in zip(BIG, _join_halves(halves, core)):
        grads[n] = g

    delta, new_m, new_v = {}, {}, {}
    for n in BIG:
        shp = p[n].shape
        two = lambda a: a.reshape(shp[0] * shp[1], shp[2])
        d, nm, nv = _adamw(two(p[n]), two(grads[n]), two(env["m_" + n]), two(env["v_" + n]), ADAM_ROWS[n], "adamw_" + n)
        delta[n], new_m[n], new_v[n] = d.reshape(shp), nm.reshape(shp), nv.reshape(shp)
    shapes = [p[n].shape for n in SMALL]
    d, nm, nv = _adamw(_pack([p[n] for n in SMALL]), _pack([grads[n] for n in SMALL]), _pack([env["m_" + n] for n in SMALL]),
                       _pack([env["v_" + n] for n in SMALL]), PACK_ROWS, "adamw_small")
    for n, a, b, c in zip(SMALL, _unpack(d, shapes), _unpack(nm, shapes), _unpack(nv, shapes)):
        delta[n], new_m[n], new_v[n] = a, b, c

    return (loss, dxt.reshape(B_LOC, SEQ, D_MODEL), *[grads[n] for n in WEIGHTS], *[delta[n] for n in WEIGHTS],
            *[new_m[n] for n in WEIGHTS], *[new_v[n] for n in WEIGHTS])
```

```python
import functools
import math

import jax
import jax.numpy as jnp
import numpy as np
from jax import lax
from jax.experimental import pallas as pl
from jax.experimental.pallas import tpu as pltpu

F32 = jnp.float32
BF16 = jnp.bfloat16
HI = lax.Precision.HIGHEST
MESH = pl.DeviceIdType.MESH
ANY = pl.BlockSpec(memory_space=pl.ANY)

D_MODEL = 1024
SEQ = 2048
B_LOC = 2
T = B_LOC * SEQ
DEPTH = 4
N_CHIPS = 4
N_DEV = 8
HEAD = 64
CHUNK = 128
N_CHUNK = SEQ // CHUNK
SSD_INNER = 512
SSD_XBC = 1024
FFN_DIM = 2816
IN_WIDTH = 2832
NP = 3072
C_XS, C_B, C_C, C_Z, C_GU, C_GV, C_Q, C_K, C_V, C_DT = 0, 512, 768, 1024, 1536, 1792, 2048, 2304, 2560, 2816
NORM_EPS = 1e-6
NEG_INF = -1e30
ATTN_DILS = (1, 4, 16)
ATTN_HALF = 64
ADAM_LR, ADAM_B1, ADAM_B2, ADAM_EPS, ADAM_WD, ADAM_STEP = 0.001, 0.9, 0.999, 1e-08, 0.01, 10
VMEM_LIMIT = 56 * 1024 * 1024

S_ = jax.ShapeDtypeStruct


def _cp():
    return pltpu.CompilerParams(vmem_limit_bytes=VMEM_LIMIT)


def _shift_rows(x, k):
    n = x.shape[0]
    if k == 0:
        return x
    r = pltpu.roll(x, (-k) % n, 0)
    t = lax.broadcasted_iota(jnp.int32, (n, 1), 0)
    return jnp.where((t + k >= 0) & (t + k < n), r, 0.0)


@functools.partial(jax.custom_vjp, nondiff_argnums=(1,))
def _shift(x, k):
    return _shift_rows(x, k)


def _shift_fwd(x, k):
    return _shift_rows(x, k), None


def _shift_bwd(k, _, g):
    return (_shift_rows(g, -k),)


_shift.defvjp(_shift_fwd, _shift_bwd)


def _dwconv(x, taps, bias):
    half = len(taps) // 2
    y = bias
    for k, w in enumerate(taps):
        y = y + w * _shift(x, k - half)
    return y


def _softplus(x):
    return jnp.maximum(x, 0.0) + jnp.log1p(jnp.exp(-jnp.abs(x)))


def _dot(a, b):
    return jnp.dot(a.astype(BF16), b.astype(BF16), preferred_element_type=F32)


def _dot_nt(a, b):
    return lax.dot_general(a.astype(BF16), b.astype(BF16), (((1,), (1,)), ((), ())), preferred_element_type=F32)


def _dot_tn(a, b):
    return lax.dot_general(a.astype(BF16), b.astype(BF16), (((0,), (0,)), ((), ())), preferred_element_type=F32)


def _head_sum_matrix(width):
    i = lax.broadcasted_iota(jnp.int32, (width, width), 0) // HEAD
    j = lax.broadcasted_iota(jnp.int32, (width, width), 1) // HEAD
    return (i == j).astype(F32)


def _matmul(a, b, *, dims, grid, a_spec, b_spec, o_spec, out_shape, acc_shape, res=None, res_spec=None, name):
    nk = grid[2]

    def body(*refs):
        if res is not None:
            a_ref, b_ref, r_ref, o_ref = refs[:4]
        else:
            a_ref, b_ref, o_ref = refs[:3]
            r_ref = None
        part = lax.dot_general(a_ref[...].astype(BF16), b_ref[...].astype(BF16), dims, preferred_element_type=F32)
        if nk == 1:
            if r_ref is not None:
                part = part + r_ref[...]
            o_ref[...] = part.astype(o_ref.dtype)
            return
        acc_ref = refs[-1]
        k = pl.program_id(2)

        @pl.when(k == 0)
        def _():
            acc_ref[...] = part

        @pl.when(k > 0)
        def _():
            acc_ref[...] += part

        @pl.when(k == nk - 1)
        def _():
            tot = acc_ref[...]
            if r_ref is not None:
                tot = tot + r_ref[...]
            o_ref[...] = tot.astype(o_ref.dtype)

    in_specs = [a_spec, b_spec] + ([res_spec] if res is not None else [])
    args = (a, b) + ((res,) if res is not None else ())
    scratch = [] if nk == 1 else [pltpu.VMEM(acc_shape, F32)]
    return pl.pallas_call(body, grid=grid, in_specs=in_specs, out_specs=o_spec, out_shape=out_shape,
                          scratch_shapes=scratch, compiler_params=_cp(), name=name)(*args)


NN = (((1,), (0,)), ((), ()))
NT = (((1,), (1,)), ((), ()))
TN = (((0,), (0,)), ((), ()))


def _mm_nn(a, b, *, tm, tn, tk, out_dtype, res=None, name):
    m, k = a.shape
    n = b.shape[1]
    return _matmul(a, b, dims=NN, grid=(m // tm, n // tn, k // tk),
                   a_spec=pl.BlockSpec((tm, tk), lambda i, j, q: (i, q)),
                   b_spec=pl.BlockSpec((tk, tn), lambda i, j, q: (q, j)),
                   o_spec=pl.BlockSpec((tm, tn), lambda i, j, q: (i, j)),
                   out_shape=S_((m, n), out_dtype), acc_shape=(tm, tn), res=res,
                   res_spec=pl.BlockSpec((tm, tn), lambda i, j, q: (i, j)), name=name)


def _mm_nt(a, b, *, tm, tn, tk, out_dtype, name):
    m, k = a.shape
    n = b.shape[0]
    return _matmul(a, b, dims=NT, grid=(m // tm, n // tn, k // tk),
                   a_spec=pl.BlockSpec((tm, tk), lambda i, j, q: (i, q)),
                   b_spec=pl.BlockSpec((tn, tk), lambda i, j, q: (j, q)),
                   o_spec=pl.BlockSpec((tm, tn), lambda i, j, q: (i, j)),
                   out_shape=S_((m, n), out_dtype), acc_shape=(tm, tn), name=name)


def _mm_tn(a, b, *, tm, tn, tk, out_dtype, name):
    k, m = a.shape
    n = b.shape[1]
    return _matmul(a, b, dims=TN, grid=(m // tm, n // tn, k // tk),
                   a_spec=pl.BlockSpec((tk, tm), lambda i, j, q: (q, i)),
                   b_spec=pl.BlockSpec((tk, tn), lambda i, j, q: (q, j)),
                   o_spec=pl.BlockSpec((tm, tn), lambda i, j, q: (i, j)),
                   out_shape=S_((m, n), out_dtype), acc_shape=(tm, tn), name=name)


HALF_TILE = FFN_DIM // 2


def _mm_up(hn, w_up):
    return _matmul(hn, w_up, dims=NN, grid=(T // 1024, 4, 1),
                   a_spec=pl.BlockSpec((1024, D_MODEL), lambda i, j, q: (i, 0)),
                   b_spec=pl.BlockSpec((D_MODEL, HALF_TILE), lambda i, j, q: (0, j)),
                   o_spec=pl.BlockSpec((None, 1024, HALF_TILE), lambda i, j, q: (j // 2, i, j % 2)),
                   out_shape=S_((2, T, FFN_DIM), F32), acc_shape=(1024, HALF_TILE), name="mm_up")


def _mm_dhn(dup3, w_up):
    return _matmul(dup3, w_up, dims=NT, grid=(T // 1024, 1, 4),
                   a_spec=pl.BlockSpec((None, 1024, HALF_TILE), lambda i, j, q: (q // 2, i, q % 2)),
                   b_spec=pl.BlockSpec((D_MODEL, HALF_TILE), lambda i, j, q: (0, q)),
                   o_spec=pl.BlockSpec((1024, D_MODEL), lambda i, j, q: (i, 0)),
                   out_shape=S_((T, D_MODEL), F32), acc_shape=(1024, D_MODEL), name="mm_dhn")


def _mm_dwup(hn, dup3):
    return _matmul(hn, dup3, dims=TN, grid=(1, 4, T // 1024),
                   a_spec=pl.BlockSpec((1024, D_MODEL), lambda i, j, q: (q, 0)),
                   b_spec=pl.BlockSpec((None, 1024, HALF_TILE), lambda i, j, q: (j // 2, q, j % 2)),
                   o_spec=pl.BlockSpec((D_MODEL, HALF_TILE), lambda i, j, q: (0, j)),
                   out_shape=S_((D_MODEL, 2 * FFN_DIM), BF16), acc_shape=(D_MODEL, HALF_TILE), name="mm_dwup")


ROWS = 512


def _rmsnorm_fwd(x, gain, name):
    def body(x_ref, g_ref, o_ref):
        xv = x_ref[...]
        r = lax.rsqrt(jnp.mean(xv * xv, axis=-1, keepdims=True) + NORM_EPS)
        o_ref[...] = (xv * r * g_ref[...]).astype(BF16)

    return pl.pallas_call(body, grid=(T // ROWS,),
                          in_specs=[pl.BlockSpec((ROWS, D_MODEL), lambda i: (i, 0)), pl.BlockSpec((1, D_MODEL), lambda i: (0, 0))],
                          out_specs=pl.BlockSpec((ROWS, D_MODEL), lambda i: (i, 0)),
                          out_shape=S_((T, D_MODEL), BF16), name=name)(x, gain)


def _rmsnorm_bwd(x, gain, dh, dres, name):
    def body(x_ref, g_ref, dh_ref, dres_ref, dx_ref, dg_ref):
        xv = x_ref[...]
        r = lax.rsqrt(jnp.mean(xv * xv, axis=-1, keepdims=True) + NORM_EPS)
        gd = dh_ref[...] * g_ref[...]
        dot = jnp.mean(gd * xv, axis=-1, keepdims=True)
        dx_ref[...] = dres_ref[...] + r * gd - xv * (r * r * r * dot)
        part = jnp.sum(dh_ref[...] * xv * r, axis=0, keepdims=True)

        @pl.when(pl.program_id(0) == 0)
        def _():
            dg_ref[...] = part

        @pl.when(pl.program_id(0) > 0)
        def _():
            dg_ref[...] += part

    row = pl.BlockSpec((ROWS, D_MODEL), lambda i: (i, 0))
    vec = pl.BlockSpec((1, D_MODEL), lambda i: (0, 0))
    return pl.pallas_call(body, grid=(T // ROWS,), in_specs=[row, vec, row, row], out_specs=[row, vec],
                          out_shape=[S_((T, D_MODEL), F32), S_((1, D_MODEL), F32)], name=name)(x, gain, dh, dres)


def _loss_head(y, target):
    def body(y_ref, t_ref, dy_ref, p_ref):
        e = y_ref[...] - t_ref[...]
        dy_ref[...] = e * (1.0 / D_MODEL)
        p_ref[...] = jnp.full((8, 128), 0.5 / D_MODEL, F32) * jnp.sum(e * e)

    row = pl.BlockSpec((ROWS, D_MODEL), lambda i: (i, 0))
    return pl.pallas_call(body, grid=(T // ROWS,), in_specs=[row, row],
                          out_specs=[row, pl.BlockSpec((8, 128), lambda i: (i, 0))],
                          out_shape=[S_((T, D_MODEL), F32), S_((T // ROWS * 8, 128), F32)], name="loss_head")(y, target)


def _adamw(w, g, m, v, rows, name):
    r_tot, c = w.shape

    def body(w_ref, g_ref, m_ref, v_ref, d_ref, nm_ref, nv_ref):
        gv = g_ref[...]
        nm = ADAM_B1 * m_ref[...] + (1.0 - ADAM_B1) * gv
        nv = ADAM_B2 * v_ref[...] + (1.0 - ADAM_B2) * (gv * gv)
        m_hat = nm / (1.0 - ADAM_B1 ** ADAM_STEP)
        v_hat = nv / (1.0 - ADAM_B2 ** ADAM_STEP)
        d_ref[...] = -ADAM_LR * (m_hat / (jnp.sqrt(v_hat) + ADAM_EPS) + ADAM_WD * w_ref[...])
        nm_ref[...] = nm
        nv_ref[...] = nv

    blk = pl.BlockSpec((rows, c), lambda i: (i, 0))
    out = S_((r_tot, c), F32)
    return pl.pallas_call(body, grid=(r_tot // rows,), in_specs=[blk] * 4, out_specs=[blk] * 3,
                          out_shape=[out, out, out], name=name)(w, g, m, v)


FFN_CT = 256


def _gate_fn(up_g, up_v, wg0, wg1, wg2, bg, wv0, wv1, wv2, bv):
    gate = _dwconv(up_g, [wg0, wg1, wg2], bg)
    val = _dwconv(up_v, [wv0, wv1, wv2], bv)
    return jax.nn.silu(gate) * val


def _taps(ref, part, n):
    return [ref[part, k:k + 1, :] for k in range(n)]


def _convgate_fwd(up3, cw, cb):
    def body(up_ref, cw_ref, cb_ref, o_ref):
        o_ref[...] = _gate_fn(up_ref[0], up_ref[1], *_taps(cw_ref, 0, 3), cb_ref[0], *_taps(cw_ref, 1, 3), cb_ref[1]).astype(BF16)

    return pl.pallas_call(
        body, grid=(FFN_DIM // FFN_CT, B_LOC),
        in_specs=[pl.BlockSpec((2, SEQ, FFN_CT), lambda j, b: (0, b, j)),
                  pl.BlockSpec((2, 8, FFN_CT), lambda j, b: (0, 0, j)),
                  pl.BlockSpec((2, 1, FFN_CT), lambda j, b: (0, 0, j))],
        out_specs=pl.BlockSpec((SEQ, FFN_CT), lambda j, b: (b, j)),
        out_shape=S_((T, FFN_DIM), BF16), compiler_params=_cp(), name="convgate_fwd")(up3, cw, cb)


def _convgate_bwd(up3, cw, cb, dact, hook=None):
    def body(up_ref, cw_ref, cb_ref, da_ref, dup_ref, dcw_ref):
        args = (up_ref[0], up_ref[1], *_taps(cw_ref, 0, 3), cb_ref[0], *_taps(cw_ref, 1, 3), cb_ref[1])
        _, vjp = jax.vjp(_gate_fn, *args)
        dg, dv, g0, g1, g2, gb, v0, v1, v2, vb = vjp(da_ref[...])
        dup_ref[0] = dg.astype(BF16)
        dup_ref[1] = dv.astype(BF16)
        zero = jnp.zeros((4, FFN_CT), F32)
        new = jnp.stack([jnp.concatenate([g0, g1, g2, gb, zero], axis=0), jnp.concatenate([v0, v1, v2, vb, zero], axis=0)])

        @pl.when(pl.program_id(1) == 0)
        def _():
            dcw_ref[...] = new

        @pl.when(pl.program_id(1) > 0)
        def _():
            dcw_ref[...] += new

    return _hooked_call(
        body, grid=(FFN_DIM // FFN_CT, B_LOC),
        in_specs=[pl.BlockSpec((2, SEQ, FFN_CT), lambda j, b: (0, b, j)),
                  pl.BlockSpec((2, 8, FFN_CT), lambda j, b: (0, 0, j)),
                  pl.BlockSpec((2, 1, FFN_CT), lambda j, b: (0, 0, j)),
                  pl.BlockSpec((SEQ, FFN_CT), lambda j, b: (b, j))],
        out_specs=[pl.BlockSpec((2, SEQ, FFN_CT), lambda j, b: (0, b, j)),
                   pl.BlockSpec((2, 8, FFN_CT), lambda j, b: (0, 0, j))],
        out_shape=[S_((2, T, FFN_DIM), BF16), S_((2, 8, FFN_DIM), F32)],
        scratch_shapes=[], args=(up3, cw, cb, dact), hook=hook, name="convgate_bwd")


SSD_CT = 256


def _conv5_fn(x, w0, w1, w2, w3, w4, b):
    return jax.nn.silu(_dwconv(x, [w0, w1, w2, w3, w4], b))


def _ssd_pre_fwd(proj, cw, cb):
    def body(x_ref, cw_ref, cb_ref, o_ref):
        o_ref[...] = _conv5_fn(x_ref[...], *[cw_ref[k:k + 1, :] for k in range(5)], cb_ref[...])

    return pl.pallas_call(
        body, grid=(SSD_XBC // SSD_CT, B_LOC),
        in_specs=[pl.BlockSpec((SEQ, SSD_CT), lambda j, b: (b, j)),
                  pl.BlockSpec((8, SSD_CT), lambda j, b: (0, j)),
                  pl.BlockSpec((1, SSD_CT), lambda j, b: (0, j))],
        out_specs=pl.BlockSpec((SEQ, SSD_CT), lambda j, b: (b, j)),
        out_shape=S_((T, SSD_XBC), F32), compiler_params=_cp(), name="ssd_pre_fwd")(proj, cw, cb)


def _ssd_pre_bwd(proj, cw, cb, dxc):
    def body(x_ref, cw_ref, cb_ref, d_ref, dx_ref, dcw_ref):
        _, vjp = jax.vjp(_conv5_fn, x_ref[...], *[cw_ref[k:k + 1, :] for k in range(5)], cb_ref[...])
        dx, g0, g1, g2, g3, g4, gb = vjp(d_ref[...])
        dx_ref[...] = dx.astype(BF16)
        new = jnp.concatenate([g0, g1, g2, g3, g4, gb, jnp.zeros((2, SSD_CT), F32)], axis=0)

        @pl.when(pl.program_id(1) == 0)
        def _():
            dcw_ref[...] = new

        @pl.when(pl.program_id(1) > 0)
        def _():
            dcw_ref[...] += new

    return pl.pallas_call(
        body, grid=(SSD_XBC // SSD_CT, B_LOC),
        in_specs=[pl.BlockSpec((SEQ, SSD_CT), lambda j, b: (b, j)),
                  pl.BlockSpec((8, SSD_CT), lambda j, b: (0, j)),
                  pl.BlockSpec((1, SSD_CT), lambda j, b: (0, j)),
                  pl.BlockSpec((SEQ, SSD_CT), lambda j, b: (b, j))],
        out_specs=[pl.BlockSpec((SEQ, SSD_CT), lambda j, b: (b, j)),
                   pl.BlockSpec((8, SSD_CT), lambda j, b: (0, j))],
        out_shape=[S_((T, SSD_XBC), BF16), S_((8, SSD_XBC), F32)],
        compiler_params=_cp(), name="ssd_pre_bwd")(proj, cw, cb, dxc)


GROUP_W = 256
ONE_BUFFER = dict(pipeline_mode=pl.Buffered(1))
HEADS_PER_GROUP = 4
SCAN_UNROLL = 2


def _ssd_dt_fn(dt_raw, bias, alog):
    dt = _softplus(dt_raw + bias)
    return dt, dt * (-jnp.exp(alog))


def _ssd_chunk_fn(direction, group, xc0, xc1, bc, cc, dt, da, prev0, prev1):
    q = CHUNK
    ti = lax.broadcasted_iota(jnp.int32, (q, q), 0)
    si = lax.broadcasted_iota(jnp.int32, (q, q), 1)
    keep = (ti >= si) if direction == 0 else (ti <= si)
    mat = keep.astype(F32)
    acs = jnp.dot(mat, da, precision=HI, preferred_element_type=F32)
    acs_t = lax.dot_general(da, mat, (((0,), (1,)), ((), ())), precision=HI, preferred_element_type=F32)
    tot = jnp.sum(da, axis=0, keepdims=True)
    lane = lax.broadcasted_iota(jnp.int32, (1, 128), 1)
    sub = lax.broadcasted_iota(jnp.int32, (128, 1), 0)
    first_head = lane < HEAD
    cb = _dot_nt(cc, bc)
    a_cols, tots, dt_cols, lows, douts = [], [], [], [], []
    for h in range(HEADS_PER_GROUP):
        ln = 8 * direction + 4 * group + h
        oh_l = (lane == ln).astype(F32)
        oh_s = (sub == ln).astype(F32)
        a_col = jnp.sum(acs * oh_l, axis=1, keepdims=True)
        a_row = jnp.sum(acs_t * oh_s, axis=0, keepdims=True)
        tot_h = jnp.sum(tot * oh_l, axis=1, keepdims=True)
        a_cols.append(a_col)
        tots.append(tot_h)
        dt_cols.append(jnp.sum(dt * oh_l, axis=1, keepdims=True))
        lows.append(cb * jnp.exp(jnp.where(keep, a_col - a_row, NEG_INF)))
        douts.append(bc * jnp.exp(tot_h - a_col))
    out = []
    for pair, (xc, prev) in enumerate(((xc0, prev0), (xc1, prev1))):
        h0, h1 = 2 * pair, 2 * pair + 1
        xdt = xc * jnp.where(first_head, dt_cols[h0], dt_cols[h1])
        y = jnp.where(first_head, jnp.exp(a_cols[h0]), jnp.exp(a_cols[h1])) * _dot(cc, prev)
        y = y + jnp.where(first_head, _dot(lows[h0], xdt), _dot(lows[h1], xdt))
        st = jnp.where(first_head, _dot_tn(douts[h0], xdt), _dot_tn(douts[h1], xdt))
        out.append((y, prev * jnp.where(first_head, jnp.exp(tots[h0]), jnp.exp(tots[h1])) + st))
    return out[0][0], out[1][0], out[0][1], out[1][1]


def _ssd_post_fn(y, xc, z, d_exp, gain):
    y = (y + d_exp * xc) * jax.nn.silu(z)
    return y * lax.rsqrt(jnp.mean(y * y, axis=-1, keepdims=True) + NORM_EPS) * gain


def _chunk_rows(c):
    return pl.ds(pl.multiple_of(c * CHUNK, CHUNK), CHUNK)


def _scan_loop(step, init):
    def body(i, carry):
        for k in range(SCAN_UNROLL):
            carry = step(i * SCAN_UNROLL + k, carry)
        return carry

    return lax.fori_loop(0, N_CHUNK // SCAN_UNROLL, body, init)


def _ssd_scan_specs(**mode):
    return [pl.BlockSpec((SEQ, GROUP_W), lambda g, b: (b, g), **mode),
            pl.BlockSpec((SEQ, 128), lambda g, b: (b, C_B // 128 + g), **mode),
            pl.BlockSpec((SEQ, 128), lambda g, b: (b, C_C // 128 + g), **mode),
            pl.BlockSpec((SEQ, GROUP_W), lambda g, b: (b, C_Z // GROUP_W + g), **mode),
            pl.BlockSpec((SEQ, 128), lambda g, b: (b, C_DT // 128), **mode),
            pl.BlockSpec((1, 128), lambda g, b: (0, 0)),
            pl.BlockSpec((1, 128), lambda g, b: (0, 0)),
            pl.BlockSpec((1, GROUP_W), lambda g, b: (0, g)),
            pl.BlockSpec((1, GROUP_W), lambda g, b: (0, g))]


def _ssd_state_spec(**mode):
    return pl.BlockSpec((None, None, 2 * N_CHUNK, 128, GROUP_W), lambda g, b: (g, b, 0, 0, 0), **mode)


def _ssd_scan_fwd(xc, proj, dtb, alog, d_exp, gain, hook=None):
    def body(x_ref, b_ref, c_ref, z_ref, dt_ref, dtb_ref, al_ref, de_ref, g_ref, o_ref, y_s, st_ref, dt_s, da_s):
        group = pl.program_id(0)
        dt, da = _ssd_dt_fn(dt_ref[...], dtb_ref[...], al_ref[...])
        dt_s[...] = dt
        da_s[...] = da
        for direction in (0, 1):
            def step(i, prev, direction=direction):
                c = i if direction == 0 else N_CHUNK - 1 - i
                rows = _chunk_rows(c)
                st_ref[direction * N_CHUNK + c, :, 0:128] = prev[0]
                st_ref[direction * N_CHUNK + c, :, 128:256] = prev[1]
                y0, y1, nxt0, nxt1 = _ssd_chunk_fn(direction, group, x_ref[rows, 0:128], x_ref[rows, 128:256], b_ref[rows, :], c_ref[rows, :],
                                                   dt_s[rows, :], da_s[rows, :], prev[0], prev[1])
                if direction == 0:
                    y_s[rows, 0:128] = y0
                    y_s[rows, 128:256] = y1
                else:
                    y_s[rows, 0:128] += y0
                    y_s[rows, 128:256] += y1
                return nxt0, nxt1

            _scan_loop(step, (jnp.zeros((128, 128), F32), jnp.zeros((128, 128), F32)))

        def post(c, carry):
            rows = _chunk_rows(c)
            o_ref[rows, :] = _ssd_post_fn(y_s[rows, :], x_ref[rows, :], z_ref[rows, :], de_ref[...], g_ref[...]).astype(BF16)
            return carry

        lax.fori_loop(0, N_CHUNK, post, 0)

    return _hooked_call(
        body, grid=(2, B_LOC), in_specs=_ssd_scan_specs(),
        out_specs=[pl.BlockSpec((SEQ, GROUP_W), lambda g, b: (b, g)), pl.BlockSpec((SEQ, GROUP_W), lambda g, b: (b, g)), _ssd_state_spec()],
        out_shape=[S_((T, SSD_INNER), BF16), S_((T, SSD_INNER), F32), S_((2, B_LOC, 2 * N_CHUNK, 128, GROUP_W), F32)],
        scratch_shapes=[pltpu.VMEM((SEQ, 128), F32), pltpu.VMEM((SEQ, 128), F32)],
        args=(xc, xc, xc, proj, proj, dtb, alog, d_exp, gain), hook=hook, name="ssd_scan_fwd")


def _ssd_scan_bwd(xc, proj, dtb, alog, d_exp, gain, dy, ysum, states, hook=None):
    def body(x_ref, b_ref, c_ref, z_ref, dt_ref, dtb_ref, al_ref, de_ref, g_ref, dy_ref, ys_ref, st_s,
             dx_ref, db_ref, dc_ref, dz_ref, ddt_ref, ddtb_ref, dal_ref, dde_ref, dg_ref,
             dt_s, da_s, y_s, ddt_s, dda_s):
        group = pl.program_id(0)
        first = pl.program_id(1) == 0
        (dt, da), dt_vjp = jax.vjp(_ssd_dt_fn, dt_ref[...], dtb_ref[...], al_ref[...])
        dt_s[...] = dt
        da_s[...] = da

        def post(c, carry):
            rows = _chunk_rows(c)
            _, post_vjp = jax.vjp(_ssd_post_fn, ys_ref[rows, :], x_ref[rows, :], z_ref[rows, :], de_ref[...], g_ref[...])
            d_y, d_x_skip, d_z, g_de, g_g = post_vjp(dy_ref[rows, :])
            dz_ref[rows, :] = d_z.astype(BF16)
            dx_ref[rows, :] = d_x_skip
            y_s[rows, :] = d_y
            return carry[0] + g_de, carry[1] + g_g

        d_de, d_g = lax.fori_loop(0, N_CHUNK, post, (jnp.zeros((1, GROUP_W), F32), jnp.zeros((1, GROUP_W), F32)))
        db_ref[...] = jnp.zeros((SEQ, 128), F32)
        dc_ref[...] = jnp.zeros((SEQ, 128), F32)
        ddt_s[...] = jnp.zeros((SEQ, 128), F32)
        dda_s[...] = jnp.zeros((SEQ, 128), F32)
        for direction in (0, 1):
            def bstep(i, dnxt, direction=direction):
                c = N_CHUNK - 1 - i if direction == 0 else i
                rows = _chunk_rows(c)
                fn = functools.partial(_ssd_chunk_fn, direction, group)
                _, vjp = jax.vjp(fn, x_ref[rows, 0:128], x_ref[rows, 128:256], b_ref[rows, :], c_ref[rows, :], dt_s[rows, :], da_s[rows, :],
                                 st_s[direction * N_CHUNK + c, :, 0:128], st_s[direction * N_CHUNK + c, :, 128:256])
                g_x0, g_x1, g_b, g_c, g_dt, g_da, g_prev0, g_prev1 = vjp((y_s[rows, 0:128], y_s[rows, 128:256], dnxt[0], dnxt[1]))
                dx_ref[rows, 0:128] += g_x0
                dx_ref[rows, 128:256] += g_x1
                db_ref[rows, :] += g_b
                dc_ref[rows, :] += g_c
                ddt_s[rows, :] += g_dt
                dda_s[rows, :] += g_da
                return g_prev0, g_prev1

            _scan_loop(bstep, (jnp.zeros((128, 128), F32), jnp.zeros((128, 128), F32)))
        g_raw, g_bias, g_alog = dt_vjp((ddt_s[...], dda_s[...]))
        ddt_ref[...] = g_raw
        pad7 = jnp.zeros((7, 128), F32)
        new_b = jnp.concatenate([g_bias, pad7], axis=0)
        new_a = jnp.concatenate([g_alog, pad7], axis=0)

        @pl.when(first)
        def _():
            ddtb_ref[...] = new_b
            dal_ref[...] = new_a
            dde_ref[...] = d_de
            dg_ref[...] = d_g

        @pl.when(jnp.logical_not(first))
        def _():
            ddtb_ref[...] += new_b
            dal_ref[...] += new_a
            dde_ref[...] += d_de
            dg_ref[...] += d_g

    return _hooked_call(
        body, grid=(2, B_LOC),
        in_specs=_ssd_scan_specs(**ONE_BUFFER) + [pl.BlockSpec((SEQ, GROUP_W), lambda g, b: (b, g), **ONE_BUFFER),
                                                  pl.BlockSpec((SEQ, GROUP_W), lambda g, b: (b, g), **ONE_BUFFER),
                                                  _ssd_state_spec(**ONE_BUFFER)],
        out_specs=[pl.BlockSpec((SEQ, GROUP_W), lambda g, b: (b, g)),
                   pl.BlockSpec((SEQ, 128), lambda g, b: (b, g)),
                   pl.BlockSpec((SEQ, 128), lambda g, b: (b, g)),
                   pl.BlockSpec((SEQ, GROUP_W), lambda g, b: (b, g)),
                   pl.BlockSpec((None, SEQ, 128), lambda g, b: (g, b, 0)),
                   pl.BlockSpec((None, 8, 128), lambda g, b: (g, 0, 0)),
                   pl.BlockSpec((None, 8, 128), lambda g, b: (g, 0, 0)),
                   pl.BlockSpec((1, GROUP_W), lambda g, b: (0, g)),
                   pl.BlockSpec((1, GROUP_W), lambda g, b: (0, g))],
        out_shape=[S_((T, SSD_INNER), F32), S_((T, 256), F32), S_((T, 256), F32), S_((T, SSD_INNER), BF16),
                   S_((2, T, 128), F32), S_((2, 8, 128), F32), S_((2, 8, 128), F32),
                   S_((1, SSD_INNER), F32), S_((1, SSD_INNER), F32)],
        scratch_shapes=[pltpu.VMEM((SEQ, 128), F32), pltpu.VMEM((SEQ, 128), F32), pltpu.VMEM((SEQ, GROUP_W), F32),
                        pltpu.VMEM((SEQ, 128), F32), pltpu.VMEM((SEQ, 128), F32)],
        args=(xc, xc, xc, proj, proj, dtb, alog, d_exp, gain, dy, ysum, states), hook=hook, name="ssd_scan_bwd")


GMLP_W = 256


def _gmlp_chunk_fn(gu, gv, v_gain, w0, w1, w2, w3, b_exp):
    u = jax.nn.gelu(gu)
    v = jax.nn.gelu(gv)
    v = v * lax.rsqrt(jnp.mean(v * v, axis=-1, keepdims=True) + NORM_EPS) * v_gain
    col = lax.broadcasted_iota(jnp.int32, (1, GMLP_W), 1) // HEAD
    mixed = b_exp
    for g, w in enumerate((w0, w1, w2, w3)):
        mixed = mixed + (col == g).astype(F32) * _dot(w, v)
    return u * mixed


def _gmlp_specs():
    return [pl.BlockSpec((SEQ, GMLP_W), lambda b: (b, C_GU // GMLP_W)),
            pl.BlockSpec((SEQ, GMLP_W), lambda b: (b, C_GV // GMLP_W)),
            pl.BlockSpec((1, GMLP_W), lambda b: (0, 0)),
            pl.BlockSpec((4, CHUNK, CHUNK), lambda b: (0, 0, 0)),
            pl.BlockSpec((CHUNK, GMLP_W), lambda b: (0, 0))]


def _gmlp_fwd(proj, v_gain, w_s, b_exp):
    def body(u_ref, v_ref, g_ref, w_ref, b_ref, o_ref):
        def step(c, carry):
            rows = _chunk_rows(c)
            o_ref[rows, :] = _gmlp_chunk_fn(u_ref[rows, :], v_ref[rows, :], g_ref[...], w_ref[0], w_ref[1], w_ref[2], w_ref[3],
                                            b_ref[...]).astype(BF16)
            return carry

        lax.fori_loop(0, N_CHUNK, step, 0)

    return pl.pallas_call(body, grid=(B_LOC,), in_specs=_gmlp_specs(),
                          out_specs=pl.BlockSpec((SEQ, GMLP_W), lambda b: (b, 0)),
                          out_shape=S_((T, GMLP_W), BF16), name="gmlp_fwd")(proj, proj, v_gain, w_s, b_exp)


def _gmlp_bwd(proj, v_gain, w_s, b_exp, dy):
    def body(u_ref, v_ref, g_ref, w_ref, b_ref, dy_ref, du_ref, dv_ref, dg_ref, dw_ref, db_ref):
        @pl.when(pl.program_id(0) == 0)
        def _():
            dg_ref[...] = jnp.zeros_like(dg_ref)
            dw_ref[...] = jnp.zeros_like(dw_ref)
            db_ref[...] = jnp.zeros_like(db_ref)

        def step(c, carry):
            rows = _chunk_rows(c)
            _, vjp = jax.vjp(_gmlp_chunk_fn, u_ref[rows, :], v_ref[rows, :], g_ref[...], w_ref[0], w_ref[1], w_ref[2], w_ref[3], b_ref[...])
            g_u, g_v, g_g, g_w0, g_w1, g_w2, g_w3, g_b = vjp(dy_ref[rows, :])
            du_ref[rows, :] = g_u.astype(BF16)
            dv_ref[rows, :] = g_v.astype(BF16)
            dg_ref[...] += g_g
            db_ref[...] += g_b
            for g, gw in enumerate((g_w0, g_w1, g_w2, g_w3)):
                dw_ref[g] += gw
            return carry

        lax.fori_loop(0, N_CHUNK, step, 0)

    blk = pl.BlockSpec((SEQ, GMLP_W), lambda b: (b, 0))
    return pl.pallas_call(
        body, grid=(B_LOC,),
        in_specs=_gmlp_specs() + [pl.BlockSpec((SEQ, GMLP_W), lambda b: (b, SSD_INNER // GMLP_W))],
        out_specs=[blk, blk, pl.BlockSpec((1, GMLP_W), lambda b: (0, 0)),
                   pl.BlockSpec((4, CHUNK, CHUNK), lambda b: (0, 0, 0)), pl.BlockSpec((CHUNK, GMLP_W), lambda b: (0, 0))],
        out_shape=[S_((T, GMLP_W), BF16), S_((T, GMLP_W), BF16), S_((1, GMLP_W), F32),
                   S_((4, CHUNK, CHUNK), F32), S_((CHUNK, GMLP_W), F32)],
        name="gmlp_bwd")(proj, proj, v_gain, w_s, b_exp, dy)


PAIR_W = 128
QB = 128
KW = QB + 2 * ATTN_HALF
N_QB = SEQ // QB
FWD_BLOCK_UNROLL = 4
BWD_BLOCK_UNROLL = 4
PAD_ROWS = SEQ + 2 * ATTN_HALF


def _qk_norm_fn(x, gain):
    ms = jnp.dot(x * x, _head_sum_matrix(PAIR_W), precision=HI, preferred_element_type=F32) * (1.0 / HEAD)
    return x * lax.rsqrt(ms + NORM_EPS) * gain


def _deinterleave(dst_ref, src_ref, dil, offset):
    length = SEQ // dil
    if dil == 1:
        dst_ref[pl.ds(offset, SEQ), :] = src_ref[...]
        return
    for r in range(dil):
        dst_ref[pl.ds(offset + r * length, length), :] = src_ref[pl.ds(r, length, stride=dil), :]


def _interleave(dst_ref, src_ref, dil, offset):
    length = SEQ // dil
    if dil == 1:
        dst_ref[...] = src_ref[pl.ds(offset, SEQ), :]
        return
    for r in range(dil):
        dst_ref[pl.ds(r, length, stride=dil), :] = src_ref[pl.ds(offset + r * length, length), :]


def _edge_mask(blk, dil):
    length = SEQ // dil
    qi = blk * QB + lax.broadcasted_iota(jnp.int32, (QB, KW), 0)
    kj = blk * QB - ATTN_HALF + lax.broadcasted_iota(jnp.int32, (QB, KW), 1)
    return (kj >= 0) & (kj < SEQ) & ((qi // length) == (kj // length))


def _lane_is_head(hh):
    return (lax.broadcasted_iota(jnp.int32, (1, PAIR_W), 1) // HEAD) == hh


def _dilate_qkv(dil, qn_s, kn_s, v_ref, qd_s, kd_s, vd_s):
    _deinterleave(qd_s, qn_s, dil, 0)
    _deinterleave(kd_s, kn_s, dil, ATTN_HALF)
    _deinterleave(vd_s, v_ref, dil, ATTN_HALF)


def _attn_branch_fwd(br, dil, qn_s, kn_s, v_ref, bias_ref, qd_s, kd_s, vd_s, od_s, ld_s):
    _dilate_qkv(dil, qn_s, kn_s, v_ref, qd_s, kd_s, vd_s)

    def step(blk, carry):
        rows = pl.ds(pl.multiple_of(blk * QB, QB), QB)
        win = pl.ds(pl.multiple_of(blk * QB, QB), KW)
        qb, kw, vw = qd_s[rows, :], kd_s[win, :], vd_s[win, :]
        edge = _edge_mask(blk, dil)
        out, lse = 0.0, 0.0
        for hh in range(2):
            is_h = _lane_is_head(hh)
            s = _dot_nt(jnp.where(is_h, qb, 0.0), kw) * (HEAD ** -0.5) + bias_ref[br, hh]
            s = jnp.where(edge, s, NEG_INF)
            m = jnp.max(s, axis=-1, keepdims=True)
            l_h = m + jnp.log(jnp.sum(jnp.exp(s - m), axis=-1, keepdims=True))
            out = out + jnp.where(is_h, _dot(jnp.exp(s - l_h), vw), 0.0)
            lse = lse + jnp.where(is_h, l_h, 0.0)
        od_s[rows, :] = out
        ld_s[rows, :] = lse
        return carry

    _block_loop(step, FWD_BLOCK_UNROLL)


def _block_loop(step, unroll):
    def body(i, carry):
        for k in range(unroll):
            carry = step(i * unroll + k, carry)
        return carry

    lax.fori_loop(0, N_QB // unroll, body, 0)


def _attn_specs():
    col = lambda c0: (lambda p, b: (b, c0 // PAIR_W + p))
    return [pl.BlockSpec((SEQ, PAIR_W), col(C_Q)), pl.BlockSpec((SEQ, PAIR_W), col(C_K)), pl.BlockSpec((SEQ, PAIR_W), col(C_V)),
            pl.BlockSpec((1, PAIR_W), lambda p, b: (0, 0)), pl.BlockSpec((1, PAIR_W), lambda p, b: (0, 0)),
            pl.BlockSpec((3, 2, QB, KW), lambda p, b: (0, p, 0, 0))]


def _attn_scratch():
    seq = pltpu.VMEM((SEQ, PAIR_W), F32)
    pad = pltpu.VMEM((PAD_ROWS, PAIR_W), F32)
    return [seq, seq, seq, pad, pad, seq, seq]


def _zero_pads(*refs):
    for ref in refs:
        ref[pl.ds(0, ATTN_HALF), :] = jnp.zeros((ATTN_HALF, PAIR_W), F32)
        ref[pl.ds(ATTN_HALF + SEQ, ATTN_HALF), :] = jnp.zeros((ATTN_HALF, PAIR_W), F32)


ROW_STEP = 256


def _row_steps(fn, init=0):
    return lax.fori_loop(0, SEQ // ROW_STEP, lambda i, c: fn(pl.ds(pl.multiple_of(i * ROW_STEP, ROW_STEP), ROW_STEP), c), init)


def _interleave_add(acc_ref, src_ref, dil, offset):
    length = SEQ // dil
    if dil == 1:
        acc_ref[...] += src_ref[pl.ds(offset, SEQ), :]
        return
    for r in range(dil):
        acc_ref[pl.ds(r, length, stride=dil), :] += src_ref[pl.ds(offset + r * length, length), :]


def _attn_norm_qk(q_ref, k_ref, qg_ref, kg_ref, qn_s, kn_s):
    def norm(rows, carry):
        qn_s[rows, :] = _qk_norm_fn(q_ref[rows, :], qg_ref[...])
        kn_s[rows, :] = _qk_norm_fn(k_ref[rows, :], kg_ref[...])
        return carry

    _row_steps(norm)


def _attn_forward_all(q_ref, k_ref, v_ref, qg_ref, kg_ref, bias_ref, qn_s, kn_s, qd_s, kd_s, vd_s, od_s, ld_s, on_s, ln_s):
    _attn_norm_qk(q_ref, k_ref, qg_ref, kg_ref, qn_s, kn_s)
    _zero_pads(kd_s, vd_s)
    for br, dil in enumerate(ATTN_DILS):
        _attn_branch_fwd(br, dil, qn_s, kn_s, v_ref, bias_ref, qd_s, kd_s, vd_s, od_s, ld_s)
        _interleave(on_s.at[br], od_s, dil, 0)
        _interleave(ln_s.at[br], ld_s, dil, 0)


def _merge_weights(ln_s, rows):
    l0, l1, l2 = ln_s[0, rows, :], ln_s[1, rows, :], ln_s[2, rows, :]
    m = jnp.maximum(jnp.maximum(l0, l1), l2)
    e = [jnp.exp(l0 - m), jnp.exp(l1 - m), jnp.exp(l2 - m)]
    den = e[0] + e[1] + e[2]
    return [e[0] / den, e[1] / den, e[2] / den]


def _attn_fwd(proj, q_gain, k_gain, bias, hook=None):
    def body(q_ref, k_ref, v_ref, qg_ref, kg_ref, bias_ref, o_ref, on_s, ln_s, qn_s, kn_s, qd_s, kd_s, vd_s, od_s, ld_s):
        _attn_forward_all(q_ref, k_ref, v_ref, qg_ref, kg_ref, bias_ref, qn_s, kn_s, qd_s, kd_s, vd_s, od_s, ld_s, on_s, ln_s)

        def merge(rows, carry):
            w = _merge_weights(ln_s, rows)
            o_ref[rows, :] = (w[0] * on_s[0, rows, :] + w[1] * on_s[1, rows, :] + w[2] * on_s[2, rows, :]).astype(BF16)
            return carry

        _row_steps(merge)

    kept = pl.BlockSpec((3, SEQ, PAIR_W), lambda p, b: (0, b, p))
    return _hooked_call(body, grid=(2, B_LOC), in_specs=_attn_specs(),
                        out_specs=[pl.BlockSpec((SEQ, PAIR_W), lambda p, b: (b, p)), kept, kept],
                        out_shape=[S_((T, 2 * PAIR_W), BF16), S_((3, T, 2 * PAIR_W), F32), S_((3, T, 2 * PAIR_W), F32)],
                        scratch_shapes=_attn_scratch(), args=(proj, proj, proj, q_gain, k_gain, bias), hook=hook, name="attn_fwd")


def _attn_bwd(proj, q_gain, k_gain, bias, dy, kept_o, kept_l):
    def body(q_ref, k_ref, v_ref, qg_ref, kg_ref, bias_ref, dy_ref, on_ref, ln_ref,
             dq_ref, dk_ref, dv_ref, dqg_ref, dkg_ref, dbias_ref,
             qn_s, kn_s, qd_s, kd_s, vd_s, od_s, ld_s, don_s, dln_s, dod_s, dld_s, dqd_s, dkd_s, dvd_s, dqn_s, dkn_s, dvn_s):
        first = pl.program_id(1) == 0
        _attn_norm_qk(q_ref, k_ref, qg_ref, kg_ref, qn_s, kn_s)
        _zero_pads(kd_s, vd_s)

        def clear_acc(rows, carry):
            dqn_s[rows, :] = jnp.zeros((ROW_STEP, PAIR_W), F32)
            dkn_s[rows, :] = jnp.zeros((ROW_STEP, PAIR_W), F32)
            dvn_s[rows, :] = jnp.zeros((ROW_STEP, PAIR_W), F32)
            return carry

        _row_steps(clear_acc)

        @pl.when(first)
        def _():
            dbias_ref[...] = jnp.zeros_like(dbias_ref)

        def merge_bwd(rows, carry):
            w = _merge_weights(ln_ref, rows)
            dy = dy_ref[rows, :]
            same_head = _head_sum_matrix(PAIR_W)
            dws = [jnp.dot(dy * on_ref[j, rows, :], same_head, precision=HI, preferred_element_type=F32) for j in range(3)]
            dbar = w[0] * dws[0] + w[1] * dws[1] + w[2] * dws[2]
            for j in range(3):
                don_s[j, rows, :] = w[j] * dy
                dln_s[j, rows, :] = w[j] * (dws[j] - dbar)
            return carry

        _row_steps(merge_bwd)
        for br, dil in enumerate(ATTN_DILS):
            _dilate_qkv(dil, qn_s, kn_s, v_ref, qd_s, kd_s, vd_s)
            _deinterleave(od_s, on_ref.at[br], dil, 0)
            _deinterleave(ld_s, ln_ref.at[br], dil, 0)
            _deinterleave(dod_s, don_s.at[br], dil, 0)
            _deinterleave(dld_s, dln_s.at[br], dil, 0)

            def clear(rows, carry):
                dkd_s[rows, :] = jnp.zeros((ROW_STEP, PAIR_W), F32)
                dvd_s[rows, :] = jnp.zeros((ROW_STEP, PAIR_W), F32)
                return carry

            _row_steps(clear)
            tail = pl.ds(SEQ, 2 * ATTN_HALF)
            dkd_s[tail, :] = jnp.zeros((2 * ATTN_HALF, PAIR_W), F32)
            dvd_s[tail, :] = jnp.zeros((2 * ATTN_HALF, PAIR_W), F32)

            def step(blk, carry, br=br, dil=dil):
                rows = pl.ds(pl.multiple_of(blk * QB, QB), QB)
                win = pl.ds(pl.multiple_of(blk * QB, QB), KW)
                qb, kw, vw = qd_s[rows, :], kd_s[win, :], vd_s[win, :]
                do_b, dl_b, o_b, l_b = dod_s[rows, :], dld_s[rows, :], od_s[rows, :], ld_s[rows, :]
                edge = _edge_mask(blk, dil)
                dq, dk, dv = 0.0, 0.0, 0.0
                for hh in range(2):
                    is_h = _lane_is_head(hh)
                    pick = (lax.broadcasted_iota(jnp.int32, (1, PAIR_W), 1) == hh * HEAD).astype(F32)
                    q_h = jnp.where(is_h, qb, 0.0)
                    do_h = jnp.where(is_h, do_b, 0.0)
                    s = _dot_nt(q_h, kw) * (HEAD ** -0.5) + bias_ref[br, hh]
                    s = jnp.where(edge, s, NEG_INF)
                    p = jnp.exp(s - jnp.sum(l_b * pick, axis=-1, keepdims=True))
                    dp = _dot_nt(do_h, vw)
                    delta = jnp.sum(do_h * o_b, axis=-1, keepdims=True)
                    ds = p * (dp - delta + jnp.sum(dl_b * pick, axis=-1, keepdims=True))
                    dbias_ref[br, hh] += ds
                    dq = dq + jnp.where(is_h, _dot(ds, kw), 0.0) * (HEAD ** -0.5)
                    dk = dk + _dot_tn(ds, q_h) * (HEAD ** -0.5)
                    dv = dv + _dot_tn(p, do_h)
                dqd_s[rows, :] = dq
                dkd_s[win, :] += dk
                dvd_s[win, :] += dv
                return carry

            _block_loop(step, BWD_BLOCK_UNROLL)
            _interleave_add(dqn_s, dqd_s, dil, 0)
            _interleave_add(dkn_s, dkd_s, dil, ATTN_HALF)
            _interleave_add(dvn_s, dvd_s, dil, ATTN_HALF)

        def norm_bwd(rows, carry):
            _, q_vjp = jax.vjp(_qk_norm_fn, q_ref[rows, :], qg_ref[...])
            _, k_vjp = jax.vjp(_qk_norm_fn, k_ref[rows, :], kg_ref[...])
            g_q, g_qg = q_vjp(dqn_s[rows, :])
            g_k, g_kg = k_vjp(dkn_s[rows, :])
            dq_ref[rows, :] = g_q.astype(BF16)
            dk_ref[rows, :] = g_k.astype(BF16)
            dv_ref[rows, :] = dvn_s[rows, :].astype(BF16)
            return carry[0] + g_qg, carry[1] + g_kg

        g_qg, g_kg = _row_steps(norm_bwd, (jnp.zeros((1, PAIR_W), F32), jnp.zeros((1, PAIR_W), F32)))
        pad7 = jnp.zeros((7, PAIR_W), F32)
        new_q = jnp.concatenate([g_qg, pad7], axis=0)
        new_k = jnp.concatenate([g_kg, pad7], axis=0)

        @pl.when(first)
        def _():
            dqg_ref[...] = new_q
            dkg_ref[...] = new_k

        @pl.when(jnp.logical_not(first))
        def _():
            dqg_ref[...] += new_q
            dkg_ref[...] += new_k

    seq = pltpu.VMEM((SEQ, PAIR_W), F32)
    seq3 = pltpu.VMEM((3, SEQ, PAIR_W), F32)
    pad = pltpu.VMEM((PAD_ROWS, PAIR_W), F32)
    kept = pl.BlockSpec((3, SEQ, PAIR_W), lambda p, b: (0, b, p))
    out_blk = pl.BlockSpec((SEQ, PAIR_W), lambda p, b: (b, p))
    gain_blk = pl.BlockSpec((None, 8, PAIR_W), lambda p, b: (p, 0, 0))
    return pl.pallas_call(
        body, grid=(2, B_LOC),
        in_specs=_attn_specs() + [pl.BlockSpec((SEQ, PAIR_W), lambda p, b: (b, (SSD_INNER + GMLP_W) // PAIR_W + p)), kept, kept],
        out_specs=[out_blk, out_blk, out_blk, gain_blk, gain_blk, pl.BlockSpec((3, 2, QB, KW), lambda p, b: (0, p, 0, 0))],
        out_shape=[S_((T, 2 * PAIR_W), BF16)] * 3 + [S_((2, 8, PAIR_W), F32)] * 2 + [S_((3, 4, QB, KW), F32)],
        scratch_shapes=_attn_scratch() + [seq3, seq3, seq, seq, seq, pad, pad, seq, seq, seq],
        compiler_params=_cp(), name="attn_bwd")(proj, proj, proj, q_gain, k_gain, bias, dy, kept_o, kept_l)


def _rel_bucket(rel):
    nb = 16
    max_exact = nb // 2
    n = jnp.abs(rel)
    large = max_exact + (jnp.log(jnp.maximum(n, 1).astype(F32) / max_exact) / math.log(1024 / max_exact) * (nb - max_exact)).astype(jnp.int32)
    large = jnp.minimum(large, nb - 1)
    return jnp.where(rel > 0, nb, 0) + jnp.where(n < max_exact, n, large)


def _attn_bias(rel_table):
    rel = jnp.arange(KW)[None, :] - ATTN_HALF - jnp.arange(QB)[:, None]
    inside = (jnp.abs(rel) <= ATTN_HALF)
    out = []
    for dil in ATTN_DILS:
        one_hot = (_rel_bucket(rel * dil)[None] == jnp.arange(32)[:, None, None]).astype(F32)
        b = jnp.einsum("kh,kts->hts", rel_table, one_hot, precision=HI)
        out.append(jnp.where(inside[None], b, NEG_INF))
    return jnp.stack(out).astype(F32)


def _place():
    return lax.axis_index("x"), lax.axis_index("y"), lax.axis_index("c")


def _allgather8(buf, name):
    rows = buf.shape[0]
    flips = [(fx, fy, fc) for fx in (0, 1) for fy in (0, 1) for fc in (0, 1)][1:]

    def body(in_ref, out_ref, send_sems, recv_sems, local_sem):
        x, y, c = _place()
        me = 4 * x + 2 * y + c
        mine = pltpu.make_async_copy(in_ref, out_ref.at[me], local_sem)
        mine.start()
        peers = [(1 - x if fx else x, 1 - y if fy else y, 1 - c if fc else c) for fx, fy, fc in flips]

        def copy(k, slot, peer):
            return pltpu.make_async_remote_copy(src_ref=in_ref, dst_ref=out_ref.at[slot], send_sem=send_sems.at[k],
                                                recv_sem=recv_sems.at[k], device_id=peer, device_id_type=MESH)

        sends = [copy(k, me, peer) for k, peer in enumerate(peers)]
        for cp in sends:
            cp.start()
        for k, (px, py, pc) in enumerate(peers):
            copy(k, 4 * px + 2 * py + pc, (px, py, pc)).wait_recv()
        for cp in sends:
            cp.wait_send()
        mine.wait()

    return pl.pallas_call(body, in_specs=[ANY], out_specs=ANY, out_shape=S_((N_DEV, rows, 128), F32),
                          scratch_shapes=[pltpu.SemaphoreType.DMA((7,)), pltpu.SemaphoreType.DMA((7,)), pltpu.SemaphoreType.DMA(())],
                          name=name)(buf)


N_BIG = 4


def _other_chips(x, y):
    return [(1 - x, y), (x, 1 - y), (1 - x, 1 - y)]


def _hooked_call(body, *, grid, in_specs, out_specs, out_shape, scratch_shapes, args, hook, name):
    if hook is None:
        res = pl.pallas_call(body, grid=grid, in_specs=in_specs, out_specs=out_specs, out_shape=out_shape,
                             scratch_shapes=scratch_shapes, compiler_params=_cp(), name=name)(*args)
        return res, None
    counts = (len(in_specs), len(hook["arrays"]), len(out_specs), len(hook["out_shape"]), len(scratch_shapes), len(hook["sems"]))

    def wrapped(*refs):
        groups, pos = [], 0
        for n in counts:
            groups.append(refs[pos:pos + n])
            pos += n
        ins, h_ins, outs, h_outs, scr, sems = groups
        idx = [pl.program_id(a) for a in range(len(grid))]
        first = functools.reduce(jnp.logical_and, [i == 0 for i in idx])
        last = functools.reduce(jnp.logical_and, [i == g - 1 for i, g in zip(idx, grid)])

        @pl.when(first)
        def _():
            hook["start"](h_ins, h_outs, sems)

        body(*ins, *outs, *scr)

        @pl.when(last)
        def _():
            hook["finish"](h_ins, h_outs, sems)

    res = pl.pallas_call(wrapped, grid=grid, in_specs=list(in_specs) + [ANY] * counts[1], out_specs=list(out_specs) + [ANY] * counts[3],
                         out_shape=list(out_shape) + list(hook["out_shape"]), scratch_shapes=list(scratch_shapes) + list(hook["sems"]),
                         compiler_params=_cp(), name=name + "_" + hook["name"])(*args, *hook["arrays"])
    return res[:counts[2]], res[counts[2]:]


def _run_hook(hook):
    n_in, n_out = len(hook["arrays"]), len(hook["out_shape"])

    def body(*refs):
        h_ins, h_outs, sems = refs[:n_in], refs[n_in:n_in + n_out], refs[n_in + n_out:]
        hook["start"](h_ins, h_outs, sems)
        hook["finish"](h_ins, h_outs, sems)

    return pl.pallas_call(body, in_specs=[ANY] * n_in, out_specs=[ANY] * n_out, out_shape=list(hook["out_shape"]),
                          scratch_shapes=list(hook["sems"]), name=hook["name"])(*hook["arrays"])


def _remote(src, dst, send_sem, recv_sem, peer):
    return pltpu.make_async_remote_copy(src_ref=src, dst_ref=dst, send_sem=send_sem, recv_sem=recv_sem, device_id=peer, device_id_type=MESH)


def _gather_hook(shards):
    def copies(h_ins, h_outs, sems, kind):
        ici_send, ici_recv, d2d_send, d2d_recv = sems
        x, y, c = _place()
        chip = 2 * x + y
        out = []
        for t in range(len(shards)):
            half = shards[t].shape[0] // 2
            mine_r, other_r = pl.ds(c * half, half), pl.ds((1 - c) * half, half)
            for f, (px, py) in enumerate(_other_chips(x, y)):
                k, peer_chip = 3 * t + f, 2 * px + py
                if kind in ("send", "land"):
                    slot = chip if kind == "send" else peer_chip
                    out.append(_remote(h_ins[t].at[mine_r], h_outs[t].at[slot, mine_r], ici_send.at[k], ici_recv.at[k], (px, py, c)))
                else:
                    rows = mine_r if kind == "pass" else other_r
                    out.append(_remote(h_outs[t].at[peer_chip, rows], h_outs[t].at[peer_chip, rows], d2d_send.at[k], d2d_recv.at[k],
                                       (x, y, 1 - c)))
        return out

    def start(h_ins, h_outs, sems):
        for cp in copies(h_ins, h_outs, sems, "send"):
            cp.start()

    def finish(h_ins, h_outs, sems):
        passed = copies(h_ins, h_outs, sems, "pass")
        for landed, forward in zip(copies(h_ins, h_outs, sems, "land"), passed):
            landed.wait_recv()
            forward.start()
        for cp in copies(h_ins, h_outs, sems, "get"):
            cp.wait_recv()
        for cp in copies(h_ins, h_outs, sems, "send") + passed:
            cp.wait_send()

    return dict(name="gather", arrays=list(shards), out_shape=[S_((N_CHIPS,) + s.shape, s.dtype) for s in shards],
                sems=[pltpu.SemaphoreType.DMA((3 * len(shards),)) for _ in range(4)], start=start, finish=finish)


def _to_sibling_hook(parts):
    def copies(h_ins, h_outs, sems):
        x, y, c = _place()
        return [_remote(h_ins[t], h_outs[t], sems[0].at[t], sems[1].at[t], (x, y, 1 - c)) for t in range(N_BIG)]

    def start(h_ins, h_outs, sems):
        for cp in copies(h_ins, h_outs, sems):
            cp.start()

    def finish(h_ins, h_outs, sems):
        cps = copies(h_ins, h_outs, sems)
        for cp in cps:
            cp.wait_recv()
        for cp in cps:
            cp.wait_send()

    return dict(name="to_sibling", arrays=list(parts), out_shape=[S_(p.shape, p.dtype) for p in parts],
                sems=[pltpu.SemaphoreType.DMA((N_BIG,)), pltpu.SemaphoreType.DMA((N_BIG,))], start=start, finish=finish)


def _to_chips_hook(parts):
    def copies(h_ins, h_outs, sems):
        x, y, c = _place()
        return [_remote(h_ins[t].at[2 * px + py], h_outs[t].at[f], sems[0].at[3 * t + f], sems[1].at[3 * t + f], (px, py, c))
                for t in range(N_BIG) for f, (px, py) in enumerate(_other_chips(x, y))]

    def start(h_ins, h_outs, sems):
        for cp in copies(h_ins, h_outs, sems):
            cp.start()

    def finish(h_ins, h_outs, sems):
        cps = copies(h_ins, h_outs, sems)
        for cp in cps:
            cp.wait_recv()
        for cp in cps:
            cp.wait_send()

    return dict(name="to_chips", arrays=list(parts), out_shape=[S_((3,) + p.shape[1:], p.dtype) for p in parts],
                sems=[pltpu.SemaphoreType.DMA((3 * N_BIG,)), pltpu.SemaphoreType.DMA((3 * N_BIG,))], start=start, finish=finish)


def _add_pair(a, b, rows, name):
    n, r, c = a.shape

    def body(a_ref, b_ref, o_ref):
        o_ref[...] = (a_ref[...].astype(F32) + b_ref[...].astype(F32)).astype(BF16)

    blk = pl.BlockSpec((None, rows, c), lambda i, j: (i, j, 0))
    return pl.pallas_call(body, grid=(n, r // rows), in_specs=[blk, blk], out_specs=blk, out_shape=S_(a.shape, BF16), name=name)(a, b)


def _add_four(own, got, rows, name):
    n, r, c = own.shape

    def body(a_ref, g_ref, o_ref):
        o_ref[...] = ((a_ref[...].astype(F32) + g_ref[0].astype(F32)) + g_ref[1].astype(F32)) + g_ref[2].astype(F32)

    blk = pl.BlockSpec((None, rows, c), lambda i, j: (i, j, 0))
    return pl.pallas_call(body, grid=(n, r // rows), in_specs=[blk, pl.BlockSpec((3, None, rows, c), lambda i, j: (0, i, j, 0))],
                          out_specs=blk, out_shape=S_(own.shape, F32), name=name)(own, got)


def _sum_slots(slots):
    rows = slots.shape[1]

    def body(s_ref, o_ref):
        tot = s_ref[0]
        for k in range(1, N_DEV):
            tot = tot + s_ref[k]
        o_ref[...] = tot

    return pl.pallas_call(body, out_shape=S_((rows, 128), F32), name="sum_slots")(slots)


SMALL = ("mix_norm_gain", "ssd_conv_w", "ssd_conv_b", "ssd_dt_bias", "ssd_a_log", "ssd_d", "ssd_out_gain", "gmlp_v_gain",
         "gmlp_w_s", "gmlp_b_s", "attn_q_gain", "attn_k_gain", "rel_bias_table", "ffn_norm_gain", "ffn_conv_w", "ffn_conv_b")
BIG = ("w_in", "w_out", "ffn_w_up", "ffn_w_down")
WEIGHTS = ("mix_norm_gain", "w_in", "ssd_conv_w", "ssd_conv_b", "ssd_dt_bias", "ssd_a_log", "ssd_d", "ssd_out_gain", "gmlp_v_gain",
           "gmlp_w_s", "gmlp_b_s", "attn_q_gain", "attn_k_gain", "rel_bias_table", "w_out", "ffn_norm_gain", "ffn_w_up",
           "ffn_conv_w", "ffn_conv_b", "ffn_w_down")
ADAM_ROWS = {"w_in": 512, "w_out": 512, "ffn_w_up": 256, "ffn_w_down": 352}


PACK_ROWS = 64


def _packed_rows(shape):
    return -(-int(np.prod(shape)) // 1024) * 8


def _pack(arrays):
    parts = []
    for a in arrays:
        rows = _packed_rows(a.shape)
        flat = a.reshape(-1).astype(F32)
        parts.append(jnp.pad(flat, (0, rows * 128 - flat.shape[0])).reshape(rows, 128))
    total = sum(p.shape[0] for p in parts)
    tail = -total % PACK_ROWS
    if tail:
        parts.append(jnp.zeros((tail, 128), F32))
    return jnp.concatenate(parts, axis=0)


def _unpack(buf, shapes):
    out, row = [], 0
    for s in shapes:
        rows, n = _packed_rows(s), int(np.prod(s))
        out.append(buf[row:row + rows].reshape(-1)[:n].reshape(s))
        row += rows
    return out


def _perm_cols(w):
    pad = jnp.zeros(w.shape[:-1] + (NP - IN_WIDTH,), w.dtype)
    return jnp.concatenate([w[..., :1536], w[..., 1552:], w[..., 1536:1552], pad], axis=-1)


def _unperm_cols(w):
    return jnp.concatenate([w[..., :1536], w[..., C_DT:C_DT + 16], w[..., 1536:C_DT]], axis=-1)


def _layer_params(l, p, conv5_w, conv3_w, bias):
    def make(mix_g, conv5, conv5_b, dt_bias, a_log, d_skip, out_gain, v_gain, w_s, b_s, q_gain, k_gain, ffn_g, conv3, conv3_b):
        lanes = lambda a: jnp.pad(a.reshape(1, 16), ((0, 0), (0, 112)))
        cw3 = jnp.pad(jnp.transpose(conv3.reshape(3, 2, FFN_DIM), (1, 0, 2)), ((0, 0), (0, 5), (0, 0)))
        return dict(mix_g=mix_g.reshape(1, D_MODEL), cw5=jnp.pad(conv5, ((0, 3), (0, 0))), cb5=conv5_b.reshape(1, SSD_XBC),
                    dtb=lanes(dt_bias), alog=lanes(a_log), d_exp=jnp.repeat(d_skip, HEAD).reshape(1, SSD_INNER),
                    out_gain=out_gain.reshape(1, SSD_INNER), v_gain=v_gain.reshape(1, GMLP_W), w_s=w_s,
                    b_exp=jnp.repeat(b_s.T, HEAD, axis=1), q_gain=jnp.tile(q_gain, 2).reshape(1, PAIR_W),
                    k_gain=jnp.tile(k_gain, 2).reshape(1, PAIR_W), ffn_g=ffn_g.reshape(1, D_MODEL), cw3=cw3,
                    cb3=conv3_b.reshape(2, 1, FFN_DIM))

    args = (p["mix_norm_gain"][l], conv5_w[l], p["ssd_conv_b"][l], p["ssd_dt_bias"][l], p["ssd_a_log"][l], p["ssd_d"][l],
            p["ssd_out_gain"][l], p["gmlp_v_gain"][l], p["gmlp_w_s"][l], p["gmlp_b_s"][l], p["attn_q_gain"][l], p["attn_k_gain"][l],
            p["ffn_norm_gain"][l], conv3_w[l], p["ffn_conv_b"][l])
    return jax.vjp(make, *args)


def _forward_layer(x, lp, w_in, w_out, w_up, w_down, bias, gather=None):
    h = _rmsnorm_fwd(x, lp["mix_g"], "rmsnorm_fwd")
    proj = _mm_nn(h, w_in, tm=1024, tn=1024, tk=1024, out_dtype=F32, name="mm_proj")
    xc = _ssd_pre_fwd(proj, lp["cw5"], lp["cb5"])
    (y_ssd, ssd_sum, ssd_states), got_a = _ssd_scan_fwd(xc, proj, lp["dtb"], lp["alog"], lp["d_exp"], lp["out_gain"],
                                                        hook=gather[0] if gather else None)
    y_gmlp = _gmlp_fwd(proj, lp["v_gain"], lp["w_s"], lp["b_exp"])
    (y_attn, attn_o, attn_l), got_b = _attn_fwd(proj, lp["q_gain"], lp["k_gain"], bias, hook=gather[1] if gather else None)
    gathered = list(got_a) + list(got_b) if gather else None
    y = jnp.concatenate([y_ssd, y_gmlp, y_attn], axis=1)
    x2 = _mm_nn(y, w_out, tm=1024, tn=1024, tk=1024, out_dtype=F32, res=x, name="mm_out")
    hn = _rmsnorm_fwd(x2, lp["ffn_g"], "rmsnorm_fwd")
    up3 = _mm_up(hn, w_up)
    act = _convgate_fwd(up3, lp["cw3"], lp["cb3"])
    x3 = _mm_nn(act, w_down, tm=1024, tn=1024, tk=HALF_TILE, out_dtype=F32, res=x2, name="mm_down")
    return x3, dict(x=x, h=h, proj=proj, xc=xc, y=y, x2=x2, hn=hn, up3=up3, act=act, attn_o=attn_o, attn_l=attn_l,
                    ssd_sum=ssd_sum, ssd_states=ssd_states), gathered


def _backward_layer(dx3, sv, lp, w_in, w_out, w_up, w_down, bias, pending=None):
    d_act = _mm_nt(dx3, w_down, tm=1024, tn=HALF_TILE, tk=1024, out_dtype=F32, name="mm_dact")
    dw_down = _mm_tn(sv["act"], dx3, tm=HALF_TILE, tn=1024, tk=1024, out_dtype=BF16, name="mm_dwdown")
    (dup3, dcw3), from_sibling = _convgate_bwd(sv["up3"], lp["cw3"], lp["cb3"], d_act, hook=pending.sibling_hook() if pending else None)
    if pending:
        pending.add_sibling(from_sibling)
    d_hn = _mm_dhn(dup3, w_up)
    dw_up = _mm_dwup(sv["hn"], dup3)
    dx2, d_ffn_g = _rmsnorm_bwd(sv["x2"], lp["ffn_g"], d_hn, dx3, "rmsnorm_bwd")
    d_y = _mm_nt(dx2, w_out, tm=1024, tn=1024, tk=1024, out_dtype=F32, name="mm_dy")
    dw_out = _mm_tn(sv["y"], dx2, tm=1024, tn=1024, tk=1024, out_dtype=BF16, name="mm_dwout")
    proj, xc = sv["proj"], sv["xc"]
    (dxs, dbc, dcc, dz, ddt2, ddtb2, dal2, d_dexp, d_outg), from_chips = _ssd_scan_bwd(
        xc, proj, lp["dtb"], lp["alog"], lp["d_exp"], lp["out_gain"], d_y, sv["ssd_sum"], sv["ssd_states"],
        hook=pending.chips_hook() if pending else None)
    if pending:
        pending.add_chips(from_chips)
    d_xbc, dcw5 = _ssd_pre_bwd(proj, lp["cw5"], lp["cb5"], jnp.concatenate([dxs, dbc, dcc], axis=1))
    d_gu, d_gv, d_vg, d_ws, d_bexp = _gmlp_bwd(proj, lp["v_gain"], lp["w_s"], lp["b_exp"], d_y)
    d_q, d_k, d_v, d_qg2, d_kg2, d_bias = _attn_bwd(proj, lp["q_gain"], lp["k_gain"], bias, d_y, sv["attn_o"], sv["attn_l"])
    d_dt = (ddt2[0] + ddt2[1]).astype(BF16)
    d_proj = jnp.concatenate([d_xbc, dz, d_gu, d_gv, d_q, d_k, d_v, d_dt, jnp.zeros((T, NP - C_DT - 128), BF16)], axis=1)
    d_h = _mm_nt(d_proj, w_in, tm=1024, tn=1024, tk=1024, out_dtype=F32, name="mm_dh")
    dw_in = _mm_tn(sv["h"], d_proj, tm=1024, tn=1024, tk=1024, out_dtype=BF16, name="mm_dwin")
    dx, d_mix_g = _rmsnorm_bwd(sv["x"], lp["mix_g"], d_h, dx2, "rmsnorm_bwd")
    d_lp = dict(mix_g=d_mix_g, cw5=dcw5[:8] * (jnp.arange(8) < 5)[:, None].astype(F32), cb5=dcw5[5:6],
                dtb=(ddtb2[0, :1] + ddtb2[1, :1]), alog=(dal2[0, :1] + dal2[1, :1]), d_exp=d_dexp, out_gain=d_outg,
                v_gain=d_vg, w_s=d_ws, b_exp=d_bexp, q_gain=d_qg2[0, :1] + d_qg2[1, :1], k_gain=d_kg2[0, :1] + d_kg2[1, :1],
                ffn_g=d_ffn_g, cw3=dcw3 * (jnp.arange(8) < 3)[None, :, None].astype(F32), cb3=dcw3[:, 3:4])
    return dx, (dw_in, dw_out, dw_up, dw_down), d_lp, d_bias


def _to_shard_major(name, dw):
    if name == "w_in":
        dw = _unperm_cols(dw)
    r, c = dw.shape
    if name in ("w_in", "ffn_w_up"):
        return jnp.transpose(dw.reshape(r, N_CHIPS, c // N_CHIPS), (1, 0, 2))
    return dw.reshape(N_CHIPS, r // N_CHIPS, c)


def _whole_weight(name, gathered, own, chip):
    parts = [jnp.where(chip == k, own, gathered[k]) for k in range(N_CHIPS)]
    if name in ("w_in", "ffn_w_up"):
        w = jnp.concatenate(parts, axis=1)
        return _perm_cols(w) if name == "w_in" else w
    return jnp.concatenate(parts, axis=0)


class _LayerReduce:
    def __init__(self, dws, chip, core):
        self.chip = chip
        parts = [_to_shard_major(n, dw) for n, dw in zip(BIG, dws)]
        halves = [g.shape[1] // 2 for g in parts]
        self.mine = [lax.dynamic_slice_in_dim(g, h * core, h, axis=1) for g, h in zip(parts, halves)]
        self.theirs = [lax.dynamic_slice_in_dim(g, h * (1 - core), h, axis=1) for g, h in zip(parts, halves)]

    def sibling_hook(self):
        return _to_sibling_hook(self.theirs)

    def add_sibling(self, got):
        self.sums = [_add_pair(a, b, a.shape[1], "add_pair_" + n) for n, a, b in zip(BIG, self.mine, got)]

    def chips_hook(self):
        return _to_chips_hook(self.sums)

    def add_chips(self, got):
        self.half = []
        for n, s2, g3 in zip(BIG, self.sums, got):
            own = lax.dynamic_index_in_dim(s2, self.chip, axis=0, keepdims=True)
            self.half.append(_add_four(own, g3[:, None], own.shape[1], "add_four_" + n)[0])

    def run_alone(self):
        self.add_sibling(_run_hook(self.sibling_hook()))
        self.add_chips(_run_hook(self.chips_hook()))


def _join_halves(halves, core):
    other = _run_hook(dict(_to_sibling_hook(halves), name="swap_halves"))
    out = []
    for mine_h, other_h in zip(halves, other):
        both = jnp.stack([mine_h, other_h])
        first = lax.dynamic_index_in_dim(both, core, axis=0, keepdims=False)
        second = lax.dynamic_index_in_dim(both, 1 - core, axis=0, keepdims=False)
        out.append(jnp.concatenate([first, second], axis=1))
    return out


LAYER_SMALL = ("mix_norm_gain", "ssd_conv_w", "ssd_conv_b", "ssd_dt_bias", "ssd_a_log", "ssd_d", "ssd_out_gain", "gmlp_v_gain",
               "gmlp_w_s", "gmlp_b_s", "attn_q_gain", "attn_k_gain", "ffn_norm_gain", "ffn_conv_w", "ffn_conv_b")


def _local_grads(x, loss_target, p, conv5_w, conv3_w, full, exchange=None):
    bias, bias_vjp = jax.vjp(_attn_bias, p["rel_bias_table"])
    xt = x.reshape(T, D_MODEL)
    saved, lps, lp_vjps = [], [], []
    for l in range(DEPTH):
        lp, lp_vjp = _layer_params(l, p, conv5_w, conv3_w, bias)
        ahead = exchange is not None and l + 1 < DEPTH
        xt, sv, gathered = _forward_layer(xt, lp, full["w_in"][l], full["w_out"][l], full["ffn_w_up"][l], full["ffn_w_down"][l], bias,
                                          gather=(_gather_hook(exchange[2][l + 1][:2]), _gather_hook(exchange[2][l + 1][2:])) if ahead else None)
        if ahead:
            for n, g, o in zip(BIG, gathered, exchange[2][l + 1]):
                full[n].append(_whole_weight(n, g, o, exchange[0]))
        saved.append(sv)
        lps.append(lp)
        lp_vjps.append(lp_vjp)
    dxt, loss_parts = _loss_head(xt, loss_target.reshape(T, D_MODEL))
    loss_local = jnp.sum(loss_parts[::8, 0])

    big_grads = [None] * DEPTH
    small_layers = [None] * DEPTH
    d_bias_tot = jnp.zeros_like(bias)
    pending = None
    for l in reversed(range(DEPTH)):
        dxt, big_grads[l], d_lp, d_bias = _backward_layer(dxt, saved[l], lps[l], full["w_in"][l], full["w_out"][l],
                                                          full["ffn_w_up"][l], full["ffn_w_down"][l], bias, pending=pending)
        if exchange is not None:
            pending = big_grads[l] = _LayerReduce(big_grads[l], exchange[0], exchange[1])
        small_layers[l] = lp_vjps[l](d_lp)
        d_bias_tot = d_bias_tot + d_bias
    if pending is not None:
        pending.run_alone()
    (d_rel_table,) = bias_vjp(d_bias_tot)
    local_small = {n: jnp.stack([small_layers[l][i] for l in range(DEPTH)]) for i, n in enumerate(LAYER_SMALL)}
    local_small["rel_bias_table"] = d_rel_table
    return dxt, loss_local, big_grads, local_small


def kernel(x, mix_norm_gain, w_in, ssd_conv_w, ssd_conv_b, ssd_dt_bias, ssd_a_log, ssd_d, ssd_out_gain, gmlp_v_gain, gmlp_w_s, gmlp_b_s, attn_q_gain, attn_k_gain, rel_bias_table, w_out, ffn_norm_gain, ffn_w_up, ffn_conv_w, ffn_conv_b, ffn_w_down, loss_target, m_mix_norm_gain, m_w_in, m_ssd_conv_w, m_ssd_conv_b, m_ssd_dt_bias, m_ssd_a_log, m_ssd_d, m_ssd_out_gain, m_gmlp_v_gain, m_gmlp_w_s, m_gmlp_b_s, m_attn_q_gain, m_attn_k_gain, m_rel_bias_table, m_w_out, m_ffn_norm_gain, m_ffn_w_up, m_ffn_conv_w, m_ffn_conv_b, m_ffn_w_down, v_mix_norm_gain, v_w_in, v_ssd_conv_w, v_ssd_conv_b, v_ssd_dt_bias, v_ssd_a_log, v_ssd_d, v_ssd_out_gain, v_gmlp_v_gain, v_gmlp_w_s, v_gmlp_b_s, v_attn_q_gain, v_attn_k_gain, v_rel_bias_table, v_w_out, v_ffn_norm_gain, v_ffn_w_up, v_ffn_conv_w, v_ffn_conv_b, v_ffn_w_down):
    env = dict(locals())
    p = {n: env[n] for n in WEIGHTS}
    chip = 2 * lax.axis_index("x") + lax.axis_index("y")
    core = lax.axis_index("c")

    conv_slots = _allgather8(_pack([ssd_conv_w, ffn_conv_w]), "allgather_conv")
    conv_parts = [_unpack(conv_slots[2 * k], [ssd_conv_w.shape, ffn_conv_w.shape]) for k in range(N_CHIPS)]
    conv5_w = jnp.concatenate([cp[0] for cp in conv_parts], axis=-1)
    conv3_w = jnp.concatenate([cp[1] for cp in conv_parts], axis=-1)
    own = [[p[n][l].astype(BF16) for n in BIG] for l in range(DEPTH)]
    first = _run_hook(_gather_hook(own[0]))
    full = {n: [_whole_weight(n, g, o, chip)] for n, g, o in zip(BIG, first, own[0])}

    dxt, loss_local, reduced, local_small = _local_grads(x, loss_target, p, conv5_w, conv3_w, full, exchange=(chip, core, own))

    small_shapes = [local_small[n].shape for n in SMALL] + [(1,)]
    slots = _allgather8(_pack([local_small[n] for n in SMALL] + [loss_local.reshape(1)]), "allgather_small")
    summed = _unpack(_sum_slots(slots), small_shapes)
    grads = dict(zip(SMALL, summed[:-1]))
    loss = summed[-1][0]
    grads["ssd_conv_w"] = lax.dynamic_slice_in_dim(grads["ssd_conv_w"], chip * 256, 256, axis=2)
    grads["ffn_conv_w"] = lax.dynamic_slice_in_dim(grads["ffn_conv_w"], chip * (2 * FFN_DIM // N_CHIPS), 2 * FFN_DIM // N_CHIPS, axis=2)

    halves = [jnp.stack([reduced[l].half[i] for l in range(DEPTH)]) for i in range(N_BIG)]
    for n, g in zip(BIG, _join_halves(halves, core)):
        grads[n] = g

    delta, new_m, new_v = {}, {}, {}
    for n in BIG:
        shp = p[n].shape
        two = lambda a: a.reshape(shp[0] * shp[1], shp[2])
        d, nm, nv = _adamw(two(p[n]), two(grads[n]), two(env["m_" + n]), two(env["v_" + n]), ADAM_ROWS[n], "adamw_" + n)
        delta[n], new_m[n], new_v[n] = d.reshape(shp), nm.reshape(shp), nv.reshape(shp)
    shapes = [p[n].shape for n in SMALL]
    d, nm, nv = _adamw(_pack([p[n] for n in SMALL]), _pack([grads[n] for n in SMALL]), _pack([env["m_" + n] for n in SMALL]),
                       _pack([env["v_" + n] for n in SMALL]), PACK_ROWS, "adamw_small")
    for n, a, b, c in zip(SMALL, _unpack(d, shapes), _unpack(nm, shapes), _unpack(nv, shapes)):
        delta[n], new_m[n], new_v[n] = a, b, c

    return (loss, dxt.reshape(B_LOC, SEQ, D_MODEL), *[grads[n] for n in WEIGHTS], *[delta[n] for n in WEIGHTS],
            *[new_m[n] for n in WEIGHTS], *[new_v[n] for n in WEIGHTS])
```

```python
import functools
import math

import jax
import jax.numpy as jnp
import numpy as np
from jax import lax
from jax.experimental import pallas as pl
from jax.experimental.pallas import tpu as pltpu

F32 = jnp.float32
BF16 = jnp.bfloat16
HI = lax.Precision.HIGHEST
MESH = pl.DeviceIdType.MESH
ANY = pl.BlockSpec(memory_space=pl.ANY)

D_MODEL = 1024
SEQ = 2048
B_LOC = 2
T = B_LOC * SEQ
DEPTH = 4
N_CHIPS = 4
N_DEV = 8
HEAD = 64
CHUNK = 128
N_CHUNK = SEQ // CHUNK
SSD_INNER = 512
SSD_XBC = 1024
FFN_DIM = 2816
IN_WIDTH = 2832
NP = 3072
C_XS, C_B, C_C, C_Z, C_GU, C_GV, C_Q, C_K, C_V, C_DT = 0, 512, 768, 1024, 1536, 1792, 2048, 2304, 2560, 2816
NORM_EPS = 1e-6
NEG_INF = -1e30
ATTN_DILS = (1, 4, 16)
ATTN_HALF = 64
ADAM_LR, ADAM_B1, ADAM_B2, ADAM_EPS, ADAM_WD, ADAM_STEP = 0.001, 0.9, 0.999, 1e-08, 0.01, 10
VMEM_LIMIT = 56 * 1024 * 1024

S_ = jax.ShapeDtypeStruct


def _cp():
    return pltpu.CompilerParams(vmem_limit_bytes=VMEM_LIMIT)


def _shift_rows(x, k):
    n = x.shape[0]
    if k == 0:
        return x
    r = pltpu.roll(x, (-k) % n, 0)
    t = lax.broadcasted_iota(jnp.int32, (n, 1), 0)
    return jnp.where((t + k >= 0) & (t + k < n), r, 0.0)


@functools.partial(jax.custom_vjp, nondiff_argnums=(1,))
def _shift(x, k):
    return _shift_rows(x, k)


def _shift_fwd(x, k):
    return _shift_rows(x, k), None


def _shift_bwd(k, _, g):
    return (_shift_rows(g, -k),)


_shift.defvjp(_shift_fwd, _shift_bwd)


def _dwconv(x, taps, bias):
    half = len(taps) // 2
    y = bias
    for k, w in enumerate(taps):
        y = y + w * _shift(x, k - half)
    return y


def _softplus(x):
    return jnp.maximum(x, 0.0) + jnp.log1p(jnp.exp(-jnp.abs(x)))


def _dot(a, b):
    return jnp.dot(a.astype(BF16), b.astype(BF16), preferred_element_type=F32)


def _dot_nt(a, b):
    return lax.dot_general(a.astype(BF16), b.astype(BF16), (((1,), (1,)), ((), ())), preferred_element_type=F32)


def _dot_tn(a, b):
    return lax.dot_general(a.astype(BF16), b.astype(BF16), (((0,), (0,)), ((), ())), preferred_element_type=F32)


def _head_sum_matrix(width):
    i = lax.broadcasted_iota(jnp.int32, (width, width), 0) // HEAD
    j = lax.broadcasted_iota(jnp.int32, (width, width), 1) // HEAD
    return (i == j).astype(F32)


def _matmul(a, b, *, dims, grid, a_spec, b_spec, o_spec, out_shape, acc_shape, res=None, res_spec=None, name):
    nk = grid[2]

    def body(*refs):
        if res is not None:
            a_ref, b_ref, r_ref, o_ref = refs[:4]
        else:
            a_ref, b_ref, o_ref = refs[:3]
            r_ref = None
        part = lax.dot_general(a_ref[...].astype(BF16), b_ref[...].astype(BF16), dims, preferred_element_type=F32)
        if nk == 1:
            if r_ref is not None:
                part = part + r_ref[...]
            o_ref[...] = part.astype(o_ref.dtype)
            return
        acc_ref = refs[-1]
        k = pl.program_id(2)

        @pl.when(k == 0)
        def _():
            acc_ref[...] = part

        @pl.when(k > 0)
        def _():
            acc_ref[...] += part

        @pl.when(k == nk - 1)
        def _():
            tot = acc_ref[...]
            if r_ref is not None:
                tot = tot + r_ref[...]
            o_ref[...] = tot.astype(o_ref.dtype)

    in_specs = [a_spec, b_spec] + ([res_spec] if res is not None else [])
    args = (a, b) + ((res,) if res is not None else ())
    scratch = [] if nk == 1 else [pltpu.VMEM(acc_shape, F32)]
    return pl.pallas_call(body, grid=grid, in_specs=in_specs, out_specs=o_spec, out_shape=out_shape,
                          scratch_shapes=scratch, compiler_params=_cp(), name=name)(*args)


NN = (((1,), (0,)), ((), ()))
NT = (((1,), (1,)), ((), ()))
TN = (((0,), (0,)), ((), ()))


def _mm_nn(a, b, *, tm, tn, tk, out_dtype, res=None, name):
    m, k = a.shape
    n = b.shape[1]
    return _matmul(a, b, dims=NN, grid=(m // tm, n // tn, k // tk),
                   a_spec=pl.BlockSpec((tm, tk), lambda i, j, q: (i, q)),
                   b_spec=pl.BlockSpec((tk, tn), lambda i, j, q: (q, j)),
                   o_spec=pl.BlockSpec((tm, tn), lambda i, j, q: (i, j)),
                   out_shape=S_((m, n), out_dtype), acc_shape=(tm, tn), res=res,
                   res_spec=pl.BlockSpec((tm, tn), lambda i, j, q: (i, j)), name=name)


def _mm_nt(a, b, *, tm, tn, tk, out_dtype, name):
    m, k = a.shape
    n = b.shape[0]
    return _matmul(a, b, dims=NT, grid=(m // tm, n // tn, k // tk),
                   a_spec=pl.BlockSpec((tm, tk), lambda i, j, q: (i, q)),
                   b_spec=pl.BlockSpec((tn, tk), lambda i, j, q: (j, q)),
                   o_spec=pl.BlockSpec((tm, tn), lambda i, j, q: (i, j)),
                   out_shape=S_((m, n), out_dtype), acc_shape=(tm, tn), name=name)


def _mm_tn(a, b, *, tm, tn, tk, out_dtype, name):
    k, m = a.shape
    n = b.shape[1]
    return _matmul(a, b, dims=TN, grid=(m // tm, n // tn, k // tk),
                   a_spec=pl.BlockSpec((tk, tm), lambda i, j, q: (q, i)),
                   b_spec=pl.BlockSpec((tk, tn), lambda i, j, q: (q, j)),
                   o_spec=pl.BlockSpec((tm, tn), lambda i, j, q: (i, j)),
                   out_shape=S_((m, n), out_dtype), acc_shape=(tm, tn), name=name)


HALF_TILE = FFN_DIM // 2


def _mm_up(hn, w_up):
    return _matmul(hn, w_up, dims=NN, grid=(T // 1024, 4, 1),
                   a_spec=pl.BlockSpec((1024, D_MODEL), lambda i, j, q: (i, 0)),
                   b_spec=pl.BlockSpec((None, D_MODEL, HALF_TILE), lambda i, j, q: (j, 0, 0)),
                   o_spec=pl.BlockSpec((None, 1024, HALF_TILE), lambda i, j, q: (j // 2, i, j % 2)),
                   out_shape=S_((2, T, FFN_DIM), F32), acc_shape=(1024, HALF_TILE), name="mm_up")


def _mm_dhn(dup3, w_up):
    return _matmul(dup3, w_up, dims=NT, grid=(T // 1024, 1, 4),
                   a_spec=pl.BlockSpec((None, 1024, HALF_TILE), lambda i, j, q: (q // 2, i, q % 2)),
                   b_spec=pl.BlockSpec((None, D_MODEL, HALF_TILE), lambda i, j, q: (q, 0, 0)),
                   o_spec=pl.BlockSpec((1024, D_MODEL), lambda i, j, q: (i, 0)),
                   out_shape=S_((T, D_MODEL), F32), acc_shape=(1024, D_MODEL), name="mm_dhn")


def _mm_dwup(hn, dup3):
    return _matmul(hn, dup3, dims=TN, grid=(1, 4, T // 1024),
                   a_spec=pl.BlockSpec((1024, D_MODEL), lambda i, j, q: (q, 0)),
                   b_spec=pl.BlockSpec((None, 1024, HALF_TILE), lambda i, j, q: (j // 2, q, j % 2)),
                   o_spec=pl.BlockSpec((None, D_MODEL, HALF_TILE), lambda i, j, q: (j, 0, 0)),
                   out_shape=S_((N_CHIPS, D_MODEL, HALF_TILE), BF16), acc_shape=(D_MODEL, HALF_TILE), name="mm_dwup")


ROWS = 512


def _rmsnorm_fwd(x, gain, name):
    def body(x_ref, g_ref, o_ref):
        xv = x_ref[...]
        r = lax.rsqrt(jnp.mean(xv * xv, axis=-1, keepdims=True) + NORM_EPS)
        o_ref[...] = (xv * r * g_ref[...]).astype(BF16)

    return pl.pallas_call(body, grid=(T // ROWS,),
                          in_specs=[pl.BlockSpec((ROWS, D_MODEL), lambda i: (i, 0)), pl.BlockSpec((1, D_MODEL), lambda i: (0, 0))],
                          out_specs=pl.BlockSpec((ROWS, D_MODEL), lambda i: (i, 0)),
                          out_shape=S_((T, D_MODEL), BF16), name=name)(x, gain)


def _rmsnorm_bwd(x, gain, dh, dres, name):
    def body(x_ref, g_ref, dh_ref, dres_ref, dx_ref, dg_ref):
        xv = x_ref[...]
        r = lax.rsqrt(jnp.mean(xv * xv, axis=-1, keepdims=True) + NORM_EPS)
        gd = dh_ref[...] * g_ref[...]
        dot = jnp.mean(gd * xv, axis=-1, keepdims=True)
        dx_ref[...] = dres_ref[...] + r * gd - xv * (r * r * r * dot)
        part = jnp.sum(dh_ref[...] * xv * r, axis=0, keepdims=True)

        @pl.when(pl.program_id(0) == 0)
        def _():
            dg_ref[...] = part

        @pl.when(pl.program_id(0) > 0)
        def _():
            dg_ref[...] += part

    row = pl.BlockSpec((ROWS, D_MODEL), lambda i: (i, 0))
    vec = pl.BlockSpec((1, D_MODEL), lambda i: (0, 0))
    return pl.pallas_call(body, grid=(T // ROWS,), in_specs=[row, vec, row, row], out_specs=[row, vec],
                          out_shape=[S_((T, D_MODEL), F32), S_((1, D_MODEL), F32)], name=name)(x, gain, dh, dres)


def _loss_head(y, target):
    def body(y_ref, t_ref, dy_ref, p_ref):
        e = y_ref[...] - t_ref[...]
        dy_ref[...] = e * (1.0 / D_MODEL)
        p_ref[...] = jnp.full((8, 128), 0.5 / D_MODEL, F32) * jnp.sum(e * e)

    row = pl.BlockSpec((ROWS, D_MODEL), lambda i: (i, 0))
    return pl.pallas_call(body, grid=(T // ROWS,), in_specs=[row, row],
                          out_specs=[row, pl.BlockSpec((8, 128), lambda i: (i, 0))],
                          out_shape=[S_((T, D_MODEL), F32), S_((T // ROWS * 8, 128), F32)], name="loss_head")(y, target)


def _adamw(w, g, m, v, rows, name):
    def body(w_ref, g_ref, m_ref, v_ref, d_ref, nm_ref, nv_ref):
        gv = g_ref[...]
        nm = ADAM_B1 * m_ref[...] + (1.0 - ADAM_B1) * gv
        nv = ADAM_B2 * v_ref[...] + (1.0 - ADAM_B2) * (gv * gv)
        m_hat = nm / (1.0 - ADAM_B1 ** ADAM_STEP)
        v_hat = nv / (1.0 - ADAM_B2 ** ADAM_STEP)
        d_ref[...] = -ADAM_LR * (m_hat / (jnp.sqrt(v_hat) + ADAM_EPS) + ADAM_WD * w_ref[...])
        nm_ref[...] = nm
        nv_ref[...] = nv

    if w.ndim == 2:
        grid, blk = (w.shape[0] // rows,), pl.BlockSpec((rows, w.shape[1]), lambda i: (i, 0))
    else:
        grid, blk = (w.shape[0], w.shape[1] // rows), pl.BlockSpec((None, rows, w.shape[2]), lambda l, i: (l, i, 0))
    out = S_(w.shape, F32)
    return pl.pallas_call(body, grid=grid, in_specs=[blk] * 4, out_specs=[blk] * 3, out_shape=[out, out, out], name=name)(w, g, m, v)


FFN_CT = 256


def _gate_fn(up_g, up_v, wg0, wg1, wg2, bg, wv0, wv1, wv2, bv):
    gate = _dwconv(up_g, [wg0, wg1, wg2], bg)
    val = _dwconv(up_v, [wv0, wv1, wv2], bv)
    return jax.nn.silu(gate) * val


def _taps(ref, part, n):
    return [ref[part, k:k + 1, :] for k in range(n)]


def _convgate_fwd(up3, cw, cb, hook=None):
    def body(up_ref, cw_ref, cb_ref, o_ref):
        o_ref[...] = _gate_fn(up_ref[0], up_ref[1], *_taps(cw_ref, 0, 3), cb_ref[0], *_taps(cw_ref, 1, 3), cb_ref[1]).astype(BF16)

    (act,), got = _hooked_call(
        body, grid=(FFN_DIM // FFN_CT, B_LOC),
        in_specs=[pl.BlockSpec((2, SEQ, FFN_CT), lambda j, b: (0, b, j)),
                  pl.BlockSpec((2, 8, FFN_CT), lambda j, b: (0, 0, j)),
                  pl.BlockSpec((2, 1, FFN_CT), lambda j, b: (0, 0, j))],
        out_specs=[pl.BlockSpec((SEQ, FFN_CT), lambda j, b: (b, j))],
        out_shape=[S_((T, FFN_DIM), BF16)], scratch_shapes=[], args=(up3, cw, cb), hook=hook, name="convgate_fwd")
    return act, got


def _convgate_bwd(up3, cw, cb, dact, hook=None):
    def body(up_ref, cw_ref, cb_ref, da_ref, dup_ref, dcw_ref):
        args = (up_ref[0], up_ref[1], *_taps(cw_ref, 0, 3), cb_ref[0], *_taps(cw_ref, 1, 3), cb_ref[1])
        _, vjp = jax.vjp(_gate_fn, *args)
        dg, dv, g0, g1, g2, gb, v0, v1, v2, vb = vjp(da_ref[...])
        dup_ref[0] = dg.astype(BF16)
        dup_ref[1] = dv.astype(BF16)
        zero = jnp.zeros((4, FFN_CT), F32)
        new = jnp.stack([jnp.concatenate([g0, g1, g2, gb, zero], axis=0), jnp.concatenate([v0, v1, v2, vb, zero], axis=0)])

        @pl.when(pl.program_id(1) == 0)
        def _():
            dcw_ref[...] = new

        @pl.when(pl.program_id(1) > 0)
        def _():
            dcw_ref[...] += new

    return _hooked_call(
        body, grid=(FFN_DIM // FFN_CT, B_LOC),
        in_specs=[pl.BlockSpec((2, SEQ, FFN_CT), lambda j, b: (0, b, j)),
                  pl.BlockSpec((2, 8, FFN_CT), lambda j, b: (0, 0, j)),
                  pl.BlockSpec((2, 1, FFN_CT), lambda j, b: (0, 0, j)),
                  pl.BlockSpec((SEQ, FFN_CT), lambda j, b: (b, j))],
        out_specs=[pl.BlockSpec((2, SEQ, FFN_CT), lambda j, b: (0, b, j)),
                   pl.BlockSpec((2, 8, FFN_CT), lambda j, b: (0, 0, j))],
        out_shape=[S_((2, T, FFN_DIM), BF16), S_((2, 8, FFN_DIM), F32)],
        scratch_shapes=[], args=(up3, cw, cb, dact), hook=hook, name="convgate_bwd")


SSD_CT = 256


def _conv5_fn(x, w0, w1, w2, w3, w4, b):
    return jax.nn.silu(_dwconv(x, [w0, w1, w2, w3, w4], b))


def _ssd_pre_fwd(proj, cw, cb):
    def body(x_ref, cw_ref, cb_ref, o_ref):
        o_ref[...] = _conv5_fn(x_ref[...], *[cw_ref[k:k + 1, :] for k in range(5)], cb_ref[...])

    return pl.pallas_call(
        body, grid=(SSD_XBC // SSD_CT, B_LOC),
        in_specs=[pl.BlockSpec((SEQ, SSD_CT), lambda j, b: (b, j)),
                  pl.BlockSpec((8, SSD_CT), lambda j, b: (0, j)),
                  pl.BlockSpec((1, SSD_CT), lambda j, b: (0, j))],
        out_specs=pl.BlockSpec((SEQ, SSD_CT), lambda j, b: (b, j)),
        out_shape=S_((T, SSD_XBC), F32), compiler_params=_cp(), name="ssd_pre_fwd")(proj, cw, cb)


def _ssd_pre_bwd(proj, cw, cb, dxc, hook=None):
    def body(x_ref, cw_ref, cb_ref, d_ref, dx_ref, dcw_ref):
        _, vjp = jax.vjp(_conv5_fn, x_ref[...], *[cw_ref[k:k + 1, :] for k in range(5)], cb_ref[...])
        dx, g0, g1, g2, g3, g4, gb = vjp(d_ref[...])
        dx_ref[...] = dx.astype(BF16)
        new = jnp.concatenate([g0, g1, g2, g3, g4, gb, jnp.zeros((2, SSD_CT), F32)], axis=0)

        @pl.when(pl.program_id(1) == 0)
        def _():
            dcw_ref[...] = new

        @pl.when(pl.program_id(1) > 0)
        def _():
            dcw_ref[...] += new

    return _hooked_call(
        body, grid=(SSD_XBC // SSD_CT, B_LOC),
        in_specs=[pl.BlockSpec((SEQ, SSD_CT), lambda j, b: (b, j)),
                  pl.BlockSpec((8, SSD_CT), lambda j, b: (0, j)),
                  pl.BlockSpec((1, SSD_CT), lambda j, b: (0, j)),
                  pl.BlockSpec((SEQ, SSD_CT), lambda j, b: (b, j))],
        out_specs=[pl.BlockSpec((SEQ, SSD_CT), lambda j, b: (b, j)),
                   pl.BlockSpec((8, SSD_CT), lambda j, b: (0, j))],
        out_shape=[S_((T, SSD_XBC), BF16), S_((8, SSD_XBC), F32)],
        scratch_shapes=[], args=(proj, cw, cb, dxc), hook=hook, name="ssd_pre_bwd")


GROUP_W = 256
ONE_BUFFER = dict(pipeline_mode=pl.Buffered(1))
HEADS_PER_GROUP = 4
SCAN_UNROLL = 2


def _ssd_dt_fn(dt_raw, bias, alog):
    dt = _softplus(dt_raw + bias)
    return dt, dt * (-jnp.exp(alog))


def _ssd_chunk_fn(direction, group, xc0, xc1, bc, cc, dt, da, prev0, prev1):
    q = CHUNK
    ti = lax.broadcasted_iota(jnp.int32, (q, q), 0)
    si = lax.broadcasted_iota(jnp.int32, (q, q), 1)
    keep = (ti >= si) if direction == 0 else (ti <= si)
    mat = keep.astype(F32)
    acs = jnp.dot(mat, da, precision=HI, preferred_element_type=F32)
    acs_t = lax.dot_general(da, mat, (((0,), (1,)), ((), ())), precision=HI, preferred_element_type=F32)
    tot = jnp.sum(da, axis=0, keepdims=True)
    lane = lax.broadcasted_iota(jnp.int32, (1, 128), 1)
    sub = lax.broadcasted_iota(jnp.int32, (128, 1), 0)
    first_head = lane < HEAD
    cb = _dot_nt(cc, bc)
    a_cols, tots, dt_cols, lows, douts = [], [], [], [], []
    for h in range(HEADS_PER_GROUP):
        ln = 8 * direction + 4 * group + h
        oh_l = (lane == ln).astype(F32)
        oh_s = (sub == ln).astype(F32)
        a_col = jnp.sum(acs * oh_l, axis=1, keepdims=True)
        a_row = jnp.sum(acs_t * oh_s, axis=0, keepdims=True)
        tot_h = jnp.sum(tot * oh_l, axis=1, keepdims=True)
        a_cols.append(a_col)
        tots.append(tot_h)
        dt_cols.append(jnp.sum(dt * oh_l, axis=1, keepdims=True))
        lows.append(cb * jnp.exp(jnp.where(keep, a_col - a_row, NEG_INF)))
        douts.append(bc * jnp.exp(tot_h - a_col))
    out = []
    for pair, (xc, prev) in enumerate(((xc0, prev0), (xc1, prev1))):
        h0, h1 = 2 * pair, 2 * pair + 1
        xdt = xc * jnp.where(first_head, dt_cols[h0], dt_cols[h1])
        y = jnp.where(first_head, jnp.exp(a_cols[h0]), jnp.exp(a_cols[h1])) * _dot(cc, prev)
        y = y + jnp.where(first_head, _dot(lows[h0], xdt), _dot(lows[h1], xdt))
        st = jnp.where(first_head, _dot_tn(douts[h0], xdt), _dot_tn(douts[h1], xdt))
        out.append((y, prev * jnp.where(first_head, jnp.exp(tots[h0]), jnp.exp(tots[h1])) + st))
    return out[0][0], out[1][0], out[0][1], out[1][1]


def _ssd_post_fn(y, xc, z, d_exp, gain):
    y = (y + d_exp * xc) * jax.nn.silu(z)
    return y * lax.rsqrt(jnp.mean(y * y, axis=-1, keepdims=True) + NORM_EPS) * gain


def _chunk_rows(c):
    return pl.ds(pl.multiple_of(c * CHUNK, CHUNK), CHUNK)


def _scan_loop(step, init):
    def body(i, carry):
        for k in range(SCAN_UNROLL):
            carry = step(i * SCAN_UNROLL + k, carry)
        return carry

    return lax.fori_loop(0, N_CHUNK // SCAN_UNROLL, body, init)


def _ssd_scan_specs(**mode):
    return [pl.BlockSpec((SEQ, GROUP_W), lambda g, b: (b, g), **mode),
            pl.BlockSpec((SEQ, 128), lambda g, b: (b, C_B // 128 + g), **mode),
            pl.BlockSpec((SEQ, 128), lambda g, b: (b, C_C // 128 + g), **mode),
            pl.BlockSpec((SEQ, GROUP_W), lambda g, b: (b, C_Z // GROUP_W + g), **mode),
            pl.BlockSpec((SEQ, 128), lambda g, b: (b, C_DT // 128), **mode),
            pl.BlockSpec((1, 128), lambda g, b: (0, 0)),
            pl.BlockSpec((1, 128), lambda g, b: (0, 0)),
            pl.BlockSpec((1, GROUP_W), lambda g, b: (0, g)),
            pl.BlockSpec((1, GROUP_W), lambda g, b: (0, g))]


def _ssd_state_spec(**mode):
    return pl.BlockSpec((None, None, 2 * N_CHUNK, 128, GROUP_W), lambda g, b: (g, b, 0, 0, 0), **mode)


def _ssd_scan_fwd(xc, proj, dtb, alog, d_exp, gain, hook=None):
    def body(x_ref, b_ref, c_ref, z_ref, dt_ref, dtb_ref, al_ref, de_ref, g_ref, o_ref, y_s, st_ref, dt_s, da_s):
        group = pl.program_id(0)
        dt, da = _ssd_dt_fn(dt_ref[...], dtb_ref[...], al_ref[...])
        dt_s[...] = dt
        da_s[...] = da
        for direction in (0, 1):
            def step(i, prev, direction=direction):
                c = i if direction == 0 else N_CHUNK - 1 - i
                rows = _chunk_rows(c)
                st_ref[direction * N_CHUNK + c, :, 0:128] = prev[0]
                st_ref[direction * N_CHUNK + c, :, 128:256] = prev[1]
                y0, y1, nxt0, nxt1 = _ssd_chunk_fn(direction, group, x_ref[rows, 0:128], x_ref[rows, 128:256], b_ref[rows, :], c_ref[rows, :],
                                                   dt_s[rows, :], da_s[rows, :], prev[0], prev[1])
                if direction == 0:
                    y_s[rows, 0:128] = y0
                    y_s[rows, 128:256] = y1
                else:
                    y_s[rows, 0:128] += y0
                    y_s[rows, 128:256] += y1
                return nxt0, nxt1

            _scan_loop(step, (jnp.zeros((128, 128), F32), jnp.zeros((128, 128), F32)))

        def post(c, carry):
            rows = _chunk_rows(c)
            o_ref[rows, :] = _ssd_post_fn(y_s[rows, :], x_ref[rows, :], z_ref[rows, :], de_ref[...], g_ref[...]).astype(BF16)
            return carry

        lax.fori_loop(0, N_CHUNK, post, 0)

    return _hooked_call(
        body, grid=(2, B_LOC), in_specs=_ssd_scan_specs(),
        out_specs=[pl.BlockSpec((SEQ, GROUP_W), lambda g, b: (b, g)), pl.BlockSpec((SEQ, GROUP_W), lambda g, b: (b, g)), _ssd_state_spec()],
        out_shape=[S_((T, SSD_INNER), BF16), S_((T, SSD_INNER), F32), S_((2, B_LOC, 2 * N_CHUNK, 128, GROUP_W), F32)],
        scratch_shapes=[pltpu.VMEM((SEQ, 128), F32), pltpu.VMEM((SEQ, 128), F32)],
        args=(xc, xc, xc, proj, proj, dtb, alog, d_exp, gain), hook=hook, name="ssd_scan_fwd")


def _ssd_scan_bwd(xc, proj, dtb, alog, d_exp, gain, dy, ysum, states, hook=None):
    def body(x_ref, b_ref, c_ref, z_ref, dt_ref, dtb_ref, al_ref, de_ref, g_ref, dy_ref, ys_ref, st_s,
             dx_ref, db_ref, dc_ref, dz_ref, ddt_ref, ddtb_ref, dal_ref, dde_ref, dg_ref,
             dt_s, da_s, y_s, ddt_s, dda_s):
        group = pl.program_id(0)
        first = pl.program_id(1) == 0
        (dt, da), dt_vjp = jax.vjp(_ssd_dt_fn, dt_ref[...], dtb_ref[...], al_ref[...])
        dt_s[...] = dt
        da_s[...] = da

        def post(c, carry):
            rows = _chunk_rows(c)
            _, post_vjp = jax.vjp(_ssd_post_fn, ys_ref[rows, :], x_ref[rows, :], z_ref[rows, :], de_ref[...], g_ref[...])
            d_y, d_x_skip, d_z, g_de, g_g = post_vjp(dy_ref[rows, :])
            dz_ref[rows, :] = d_z.astype(BF16)
            dx_ref[rows, :] = d_x_skip
            y_s[rows, :] = d_y
            return carry[0] + g_de, carry[1] + g_g

        d_de, d_g = lax.fori_loop(0, N_CHUNK, post, (jnp.zeros((1, GROUP_W), F32), jnp.zeros((1, GROUP_W), F32)))
        db_ref[...] = jnp.zeros((SEQ, 128), F32)
        dc_ref[...] = jnp.zeros((SEQ, 128), F32)
        ddt_s[...] = jnp.zeros((SEQ, 128), F32)
        dda_s[...] = jnp.zeros((SEQ, 128), F32)
        for direction in (0, 1):
            def bstep(i, dnxt, direction=direction):
                c = N_CHUNK - 1 - i if direction == 0 else i
                rows = _chunk_rows(c)
                fn = functools.partial(_ssd_chunk_fn, direction, group)
                _, vjp = jax.vjp(fn, x_ref[rows, 0:128], x_ref[rows, 128:256], b_ref[rows, :], c_ref[rows, :], dt_s[rows, :], da_s[rows, :],
                                 st_s[direction * N_CHUNK + c, :, 0:128], st_s[direction * N_CHUNK + c, :, 128:256])
                g_x0, g_x1, g_b, g_c, g_dt, g_da, g_prev0, g_prev1 = vjp((y_s[rows, 0:128], y_s[rows, 128:256], dnxt[0], dnxt[1]))
                dx_ref[rows, 0:128] += g_x0
                dx_ref[rows, 128:256] += g_x1
                db_ref[rows, :] += g_b
                dc_ref[rows, :] += g_c
                ddt_s[rows, :] += g_dt
                dda_s[rows, :] += g_da
                return g_prev0, g_prev1

            _scan_loop(bstep, (jnp.zeros((128, 128), F32), jnp.zeros((128, 128), F32)))
        g_raw, g_bias, g_alog = dt_vjp((ddt_s[...], dda_s[...]))
        ddt_ref[...] = g_raw
        pad7 = jnp.zeros((7, 128), F32)
        new_b = jnp.concatenate([g_bias, pad7], axis=0)
        new_a = jnp.concatenate([g_alog, pad7], axis=0)

        @pl.when(first)
        def _():
            ddtb_ref[...] = new_b
            dal_ref[...] = new_a
            dde_ref[...] = d_de
            dg_ref[...] = d_g

        @pl.when(jnp.logical_not(first))
        def _():
            ddtb_ref[...] += new_b
            dal_ref[...] += new_a
            dde_ref[...] += d_de
            dg_ref[...] += d_g

    return _hooked_call(
        body, grid=(2, B_LOC),
        in_specs=_ssd_scan_specs(**ONE_BUFFER) + [pl.BlockSpec((SEQ, GROUP_W), lambda g, b: (b, g), **ONE_BUFFER),
                                                  pl.BlockSpec((SEQ, GROUP_W), lambda g, b: (b, g), **ONE_BUFFER),
                                                  _ssd_state_spec(**ONE_BUFFER)],
        out_specs=[pl.BlockSpec((SEQ, GROUP_W), lambda g, b: (b, g)),
                   pl.BlockSpec((SEQ, 128), lambda g, b: (b, g)),
                   pl.BlockSpec((SEQ, 128), lambda g, b: (b, g)),
                   pl.BlockSpec((SEQ, GROUP_W), lambda g, b: (b, g)),
                   pl.BlockSpec((None, SEQ, 128), lambda g, b: (g, b, 0)),
                   pl.BlockSpec((None, 8, 128), lambda g, b: (g, 0, 0)),
                   pl.BlockSpec((None, 8, 128), lambda g, b: (g, 0, 0)),
                   pl.BlockSpec((1, GROUP_W), lambda g, b: (0, g)),
                   pl.BlockSpec((1, GROUP_W), lambda g, b: (0, g))],
        out_shape=[S_((T, SSD_INNER), F32), S_((T, 256), F32), S_((T, 256), F32), S_((T, SSD_INNER), BF16),
                   S_((2, T, 128), F32), S_((2, 8, 128), F32), S_((2, 8, 128), F32),
                   S_((1, SSD_INNER), F32), S_((1, SSD_INNER), F32)],
        scratch_shapes=[pltpu.VMEM((SEQ, 128), F32), pltpu.VMEM((SEQ, 128), F32), pltpu.VMEM((SEQ, GROUP_W), F32),
                        pltpu.VMEM((SEQ, 128), F32), pltpu.VMEM((SEQ, 128), F32)],
        args=(xc, xc, xc, proj, proj, dtb, alog, d_exp, gain, dy, ysum, states), hook=hook, name="ssd_scan_bwd")


GMLP_W = 256


def _gmlp_chunk_fn(gu, gv, v_gain, w0, w1, w2, w3, b_exp):
    u = jax.nn.gelu(gu)
    v = jax.nn.gelu(gv)
    v = v * lax.rsqrt(jnp.mean(v * v, axis=-1, keepdims=True) + NORM_EPS) * v_gain
    col = lax.broadcasted_iota(jnp.int32, (1, GMLP_W), 1) // HEAD
    mixed = b_exp
    for g, w in enumerate((w0, w1, w2, w3)):
        mixed = mixed + (col == g).astype(F32) * _dot(w, v)
    return u * mixed


def _gmlp_specs():
    return [pl.BlockSpec((SEQ, GMLP_W), lambda b: (b, C_GU // GMLP_W)),
            pl.BlockSpec((SEQ, GMLP_W), lambda b: (b, C_GV // GMLP_W)),
            pl.BlockSpec((1, GMLP_W), lambda b: (0, 0)),
            pl.BlockSpec((4, CHUNK, CHUNK), lambda b: (0, 0, 0)),
            pl.BlockSpec((CHUNK, GMLP_W), lambda b: (0, 0))]


def _gmlp_fwd(proj, v_gain, w_s, b_exp):
    def body(u_ref, v_ref, g_ref, w_ref, b_ref, o_ref):
        def step(c, carry):
            rows = _chunk_rows(c)
            o_ref[rows, :] = _gmlp_chunk_fn(u_ref[rows, :], v_ref[rows, :], g_ref[...], w_ref[0], w_ref[1], w_ref[2], w_ref[3],
                                            b_ref[...]).astype(BF16)
            return carry

        lax.fori_loop(0, N_CHUNK, step, 0)

    return pl.pallas_call(body, grid=(B_LOC,), in_specs=_gmlp_specs(),
                          out_specs=pl.BlockSpec((SEQ, GMLP_W), lambda b: (b, 0)),
                          out_shape=S_((T, GMLP_W), BF16), name="gmlp_fwd")(proj, proj, v_gain, w_s, b_exp)


def _gmlp_bwd(proj, v_gain, w_s, b_exp, dy):
    def body(u_ref, v_ref, g_ref, w_ref, b_ref, dy_ref, du_ref, dv_ref, dg_ref, dw_ref, db_ref):
        @pl.when(pl.program_id(0) == 0)
        def _():
            dg_ref[...] = jnp.zeros_like(dg_ref)
            dw_ref[...] = jnp.zeros_like(dw_ref)
            db_ref[...] = jnp.zeros_like(db_ref)

        def step(c, carry):
            rows = _chunk_rows(c)
            _, vjp = jax.vjp(_gmlp_chunk_fn, u_ref[rows, :], v_ref[rows, :], g_ref[...], w_ref[0], w_ref[1], w_ref[2], w_ref[3], b_ref[...])
            g_u, g_v, g_g, g_w0, g_w1, g_w2, g_w3, g_b = vjp(dy_ref[rows, :])
            du_ref[rows, :] = g_u.astype(BF16)
            dv_ref[rows, :] = g_v.astype(BF16)
            dg_ref[...] += g_g
            db_ref[...] += g_b
            for g, gw in enumerate((g_w0, g_w1, g_w2, g_w3)):
                dw_ref[g] += gw
            return carry

        lax.fori_loop(0, N_CHUNK, step, 0)

    blk = pl.BlockSpec((SEQ, GMLP_W), lambda b: (b, 0))
    return pl.pallas_call(
        body, grid=(B_LOC,),
        in_specs=_gmlp_specs() + [pl.BlockSpec((SEQ, GMLP_W), lambda b: (b, SSD_INNER // GMLP_W))],
        out_specs=[blk, blk, pl.BlockSpec((1, GMLP_W), lambda b: (0, 0)),
                   pl.BlockSpec((4, CHUNK, CHUNK), lambda b: (0, 0, 0)), pl.BlockSpec((CHUNK, GMLP_W), lambda b: (0, 0))],
        out_shape=[S_((T, GMLP_W), BF16), S_((T, GMLP_W), BF16), S_((1, GMLP_W), F32),
                   S_((4, CHUNK, CHUNK), F32), S_((CHUNK, GMLP_W), F32)],
        name="gmlp_bwd")(proj, proj, v_gain, w_s, b_exp, dy)


PAIR_W = 128
QB = 128
KW = QB + 2 * ATTN_HALF
N_QB = SEQ // QB
FWD_BLOCK_UNROLL = 4
BWD_BLOCK_UNROLL = 4
PAD_ROWS = SEQ + 2 * ATTN_HALF


def _qk_norm_fn(x, gain):
    ms = jnp.dot(x * x, _head_sum_matrix(PAIR_W), precision=HI, preferred_element_type=F32) * (1.0 / HEAD)
    return x * lax.rsqrt(ms + NORM_EPS) * gain


def _deinterleave(dst_ref, src_ref, dil, offset):
    length = SEQ // dil
    if dil == 1:
        dst_ref[pl.ds(offset, SEQ), :] = src_ref[...]
        return
    for r in range(dil):
        dst_ref[pl.ds(offset + r * length, length), :] = src_ref[pl.ds(r, length, stride=dil), :]


def _interleave(dst_ref, src_ref, dil, offset):
    length = SEQ // dil
    if dil == 1:
        dst_ref[...] = src_ref[pl.ds(offset, SEQ), :]
        return
    for r in range(dil):
        dst_ref[pl.ds(r, length, stride=dil), :] = src_ref[pl.ds(offset + r * length, length), :]


def _edge_mask(blk, dil):
    length = SEQ // dil
    qi = blk * QB + lax.broadcasted_iota(jnp.int32, (QB, KW), 0)
    kj = blk * QB - ATTN_HALF + lax.broadcasted_iota(jnp.int32, (QB, KW), 1)
    return (kj >= 0) & (kj < SEQ) & ((qi // length) == (kj // length))


def _lane_is_head(hh):
    return (lax.broadcasted_iota(jnp.int32, (1, PAIR_W), 1) // HEAD) == hh


def _dilate_qkv(dil, qn_s, kn_s, v_ref, qd_s, kd_s, vd_s):
    _deinterleave(qd_s, qn_s, dil, 0)
    _deinterleave(kd_s, kn_s, dil, ATTN_HALF)
    _deinterleave(vd_s, v_ref, dil, ATTN_HALF)


def _attn_branch_fwd(br, dil, qn_s, kn_s, v_ref, bias_ref, qd_s, kd_s, vd_s, od_s, ld_s):
    _dilate_qkv(dil, qn_s, kn_s, v_ref, qd_s, kd_s, vd_s)

    def step(blk, carry):
        rows = pl.ds(pl.multiple_of(blk * QB, QB), QB)
        win = pl.ds(pl.multiple_of(blk * QB, QB), KW)
        qb, kw, vw = qd_s[rows, :], kd_s[win, :], vd_s[win, :]
        edge = _edge_mask(blk, dil)
        out, lse = 0.0, 0.0
        for hh in range(2):
            is_h = _lane_is_head(hh)
            s = _dot_nt(jnp.where(is_h, qb, 0.0), kw) * (HEAD ** -0.5) + bias_ref[br, hh]
            s = jnp.where(edge, s, NEG_INF)
            m = jnp.max(s, axis=-1, keepdims=True)
            l_h = m + jnp.log(jnp.sum(jnp.exp(s - m), axis=-1, keepdims=True))
            out = out + jnp.where(is_h, _dot(jnp.exp(s - l_h), vw), 0.0)
            lse = lse + jnp.where(is_h, l_h, 0.0)
        od_s[rows, :] = out
        ld_s[rows, :] = lse
        return carry

    _block_loop(step, FWD_BLOCK_UNROLL)


def _block_loop(step, unroll):
    def body(i, carry):
        for k in range(unroll):
            carry = step(i * unroll + k, carry)
        return carry

    lax.fori_loop(0, N_QB // unroll, body, 0)


def _attn_specs():
    col = lambda c0: (lambda p, b: (b, c0 // PAIR_W + p))
    return [pl.BlockSpec((SEQ, PAIR_W), col(C_Q)), pl.BlockSpec((SEQ, PAIR_W), col(C_K)), pl.BlockSpec((SEQ, PAIR_W), col(C_V)),
            pl.BlockSpec((1, PAIR_W), lambda p, b: (0, 0)), pl.BlockSpec((1, PAIR_W), lambda p, b: (0, 0)),
            pl.BlockSpec((3, 2, QB, KW), lambda p, b: (0, p, 0, 0))]


def _attn_scratch():
    seq = pltpu.VMEM((SEQ, PAIR_W), F32)
    pad = pltpu.VMEM((PAD_ROWS, PAIR_W), F32)
    return [seq, seq, seq, pad, pad, seq, seq]


def _zero_pads(*refs):
    for ref in refs:
        ref[pl.ds(0, ATTN_HALF), :] = jnp.zeros((ATTN_HALF, PAIR_W), F32)
        ref[pl.ds(ATTN_HALF + SEQ, ATTN_HALF), :] = jnp.zeros((ATTN_HALF, PAIR_W), F32)


ROW_STEP = 256


def _row_steps(fn, init=0):
    return lax.fori_loop(0, SEQ // ROW_STEP, lambda i, c: fn(pl.ds(pl.multiple_of(i * ROW_STEP, ROW_STEP), ROW_STEP), c), init)


def _interleave_add(acc_ref, src_ref, dil, offset):
    length = SEQ // dil
    if dil == 1:
        acc_ref[...] += src_ref[pl.ds(offset, SEQ), :]
        return
    for r in range(dil):
        acc_ref[pl.ds(r, length, stride=dil), :] += src_ref[pl.ds(offset + r * length, length), :]


def _attn_norm_qk(q_ref, k_ref, qg_ref, kg_ref, qn_s, kn_s):
    def norm(rows, carry):
        qn_s[rows, :] = _qk_norm_fn(q_ref[rows, :], qg_ref[...])
        kn_s[rows, :] = _qk_norm_fn(k_ref[rows, :], kg_ref[...])
        return carry

    _row_steps(norm)


def _attn_forward_all(q_ref, k_ref, v_ref, qg_ref, kg_ref, bias_ref, qn_s, kn_s, qd_s, kd_s, vd_s, od_s, ld_s, on_s, ln_s):
    _attn_norm_qk(q_ref, k_ref, qg_ref, kg_ref, qn_s, kn_s)
    _zero_pads(kd_s, vd_s)
    for br, dil in enumerate(ATTN_DILS):
        _attn_branch_fwd(br, dil, qn_s, kn_s, v_ref, bias_ref, qd_s, kd_s, vd_s, od_s, ld_s)
        _interleave(on_s.at[br], od_s, dil, 0)
        _interleave(ln_s.at[br], ld_s, dil, 0)


def _merge_weights(ln_s, rows):
    l0, l1, l2 = ln_s[0, rows, :], ln_s[1, rows, :], ln_s[2, rows, :]
    m = jnp.maximum(jnp.maximum(l0, l1), l2)
    e = [jnp.exp(l0 - m), jnp.exp(l1 - m), jnp.exp(l2 - m)]
    den = e[0] + e[1] + e[2]
    return [e[0] / den, e[1] / den, e[2] / den]


def _attn_fwd(proj, q_gain, k_gain, bias, hook=None):
    def body(q_ref, k_ref, v_ref, qg_ref, kg_ref, bias_ref, o_ref, on_s, ln_s, qn_s, kn_s, qd_s, kd_s, vd_s, od_s, ld_s):
        _attn_forward_all(q_ref, k_ref, v_ref, qg_ref, kg_ref, bias_ref, qn_s, kn_s, qd_s, kd_s, vd_s, od_s, ld_s, on_s, ln_s)

        def merge(rows, carry):
            w = _merge_weights(ln_s, rows)
            o_ref[rows, :] = (w[0] * on_s[0, rows, :] + w[1] * on_s[1, rows, :] + w[2] * on_s[2, rows, :]).astype(BF16)
            return carry

        _row_steps(merge)

    kept = pl.BlockSpec((3, SEQ, PAIR_W), lambda p, b: (0, b, p))
    return _hooked_call(body, grid=(2, B_LOC), in_specs=_attn_specs(),
                        out_specs=[pl.BlockSpec((SEQ, PAIR_W), lambda p, b: (b, p)), kept, kept],
                        out_shape=[S_((T, 2 * PAIR_W), BF16), S_((3, T, 2 * PAIR_W), F32), S_((3, T, 2 * PAIR_W), F32)],
                        scratch_shapes=_attn_scratch(), args=(proj, proj, proj, q_gain, k_gain, bias), hook=hook, name="attn_fwd")


def _attn_bwd(proj, q_gain, k_gain, bias, dy, kept_o, kept_l, hook=None):
    def body(q_ref, k_ref, v_ref, qg_ref, kg_ref, bias_ref, dy_ref, on_ref, ln_ref,
             dq_ref, dk_ref, dv_ref, dqg_ref, dkg_ref, dbias_ref,
             qn_s, kn_s, qd_s, kd_s, vd_s, od_s, ld_s, don_s, dln_s, dod_s, dld_s, dqd_s, dkd_s, dvd_s, dqn_s, dkn_s, dvn_s):
        first = pl.program_id(1) == 0
        _attn_norm_qk(q_ref, k_ref, qg_ref, kg_ref, qn_s, kn_s)
        _zero_pads(kd_s, vd_s)

        def clear_acc(rows, carry):
            dqn_s[rows, :] = jnp.zeros((ROW_STEP, PAIR_W), F32)
            dkn_s[rows, :] = jnp.zeros((ROW_STEP, PAIR_W), F32)
            dvn_s[rows, :] = jnp.zeros((ROW_STEP, PAIR_W), F32)
            return carry

        _row_steps(clear_acc)

        @pl.when(first)
        def _():
            dbias_ref[...] = jnp.zeros_like(dbias_ref)

        def merge_bwd(rows, carry):
            w = _merge_weights(ln_ref, rows)
            dy = dy_ref[rows, :]
            same_head = _head_sum_matrix(PAIR_W)
            dws = [jnp.dot(dy * on_ref[j, rows, :], same_head, precision=HI, preferred_element_type=F32) for j in range(3)]
            dbar = w[0] * dws[0] + w[1] * dws[1] + w[2] * dws[2]
            for j in range(3):
                don_s[j, rows, :] = w[j] * dy
                dln_s[j, rows, :] = w[j] * (dws[j] - dbar)
            return carry

        _row_steps(merge_bwd)
        for br, dil in enumerate(ATTN_DILS):
            _dilate_qkv(dil, qn_s, kn_s, v_ref, qd_s, kd_s, vd_s)
            _deinterleave(od_s, on_ref.at[br], dil, 0)
            _deinterleave(ld_s, ln_ref.at[br], dil, 0)
            _deinterleave(dod_s, don_s.at[br], dil, 0)
            _deinterleave(dld_s, dln_s.at[br], dil, 0)

            def clear(rows, carry):
                dkd_s[rows, :] = jnp.zeros((ROW_STEP, PAIR_W), F32)
                dvd_s[rows, :] = jnp.zeros((ROW_STEP, PAIR_W), F32)
                return carry

            _row_steps(clear)
            tail = pl.ds(SEQ, 2 * ATTN_HALF)
            dkd_s[tail, :] = jnp.zeros((2 * ATTN_HALF, PAIR_W), F32)
            dvd_s[tail, :] = jnp.zeros((2 * ATTN_HALF, PAIR_W), F32)

            def step(blk, carry, br=br, dil=dil):
                rows = pl.ds(pl.multiple_of(blk * QB, QB), QB)
                win = pl.ds(pl.multiple_of(blk * QB, QB), KW)
                qb, kw, vw = qd_s[rows, :], kd_s[win, :], vd_s[win, :]
                do_b, dl_b, o_b, l_b = dod_s[rows, :], dld_s[rows, :], od_s[rows, :], ld_s[rows, :]
                edge = _edge_mask(blk, dil)
                dq, dk, dv = 0.0, 0.0, 0.0
                for hh in range(2):
                    is_h = _lane_is_head(hh)
                    pick = (lax.broadcasted_iota(jnp.int32, (1, PAIR_W), 1) == hh * HEAD).astype(F32)
                    q_h = jnp.where(is_h, qb, 0.0)
                    do_h = jnp.where(is_h, do_b, 0.0)
                    s = _dot_nt(q_h, kw) * (HEAD ** -0.5) + bias_ref[br, hh]
                    s = jnp.where(edge, s, NEG_INF)
                    p = jnp.exp(s - jnp.sum(l_b * pick, axis=-1, keepdims=True))
                    dp = _dot_nt(do_h, vw)
                    delta = jnp.sum(do_h * o_b, axis=-1, keepdims=True)
                    ds = p * (dp - delta + jnp.sum(dl_b * pick, axis=-1, keepdims=True))
                    dbias_ref[br, hh] += ds
                    dq = dq + jnp.where(is_h, _dot(ds, kw), 0.0) * (HEAD ** -0.5)
                    dk = dk + _dot_tn(ds, q_h) * (HEAD ** -0.5)
                    dv = dv + _dot_tn(p, do_h)
                dqd_s[rows, :] = dq
                dkd_s[win, :] += dk
                dvd_s[win, :] += dv
                return carry

            _block_loop(step, BWD_BLOCK_UNROLL)
            _interleave_add(dqn_s, dqd_s, dil, 0)
            _interleave_add(dkn_s, dkd_s, dil, ATTN_HALF)
            _interleave_add(dvn_s, dvd_s, dil, ATTN_HALF)

        def norm_bwd(rows, carry):
            _, q_vjp = jax.vjp(_qk_norm_fn, q_ref[rows, :], qg_ref[...])
            _, k_vjp = jax.vjp(_qk_norm_fn, k_ref[rows, :], kg_ref[...])
            g_q, g_qg = q_vjp(dqn_s[rows, :])
            g_k, g_kg = k_vjp(dkn_s[rows, :])
            dq_ref[rows, :] = g_q.astype(BF16)
            dk_ref[rows, :] = g_k.astype(BF16)
            dv_ref[rows, :] = dvn_s[rows, :].astype(BF16)
            return carry[0] + g_qg, carry[1] + g_kg

        g_qg, g_kg = _row_steps(norm_bwd, (jnp.zeros((1, PAIR_W), F32), jnp.zeros((1, PAIR_W), F32)))
        pad7 = jnp.zeros((7, PAIR_W), F32)
        new_q = jnp.concatenate([g_qg, pad7], axis=0)
        new_k = jnp.concatenate([g_kg, pad7], axis=0)

        @pl.when(first)
        def _():
            dqg_ref[...] = new_q
            dkg_ref[...] = new_k

        @pl.when(jnp.logical_not(first))
        def _():
            dqg_ref[...] += new_q
            dkg_ref[...] += new_k

    seq = pltpu.VMEM((SEQ, PAIR_W), F32)
    seq3 = pltpu.VMEM((3, SEQ, PAIR_W), F32)
    pad = pltpu.VMEM((PAD_ROWS, PAIR_W), F32)
    kept = pl.BlockSpec((3, SEQ, PAIR_W), lambda p, b: (0, b, p))
    out_blk = pl.BlockSpec((SEQ, PAIR_W), lambda p, b: (b, p))
    gain_blk = pl.BlockSpec((None, 8, PAIR_W), lambda p, b: (p, 0, 0))
    return _hooked_call(
        body, grid=(2, B_LOC),
        in_specs=_attn_specs() + [pl.BlockSpec((SEQ, PAIR_W), lambda p, b: (b, (SSD_INNER + GMLP_W) // PAIR_W + p)), kept, kept],
        out_specs=[out_blk, out_blk, out_blk, gain_blk, gain_blk, pl.BlockSpec((3, 2, QB, KW), lambda p, b: (0, p, 0, 0))],
        out_shape=[S_((T, 2 * PAIR_W), BF16)] * 3 + [S_((2, 8, PAIR_W), F32)] * 2 + [S_((3, 4, QB, KW), F32)],
        scratch_shapes=_attn_scratch() + [seq3, seq3, seq, seq, seq, pad, pad, seq, seq, seq],
        args=(proj, proj, proj, q_gain, k_gain, bias, dy, kept_o, kept_l), hook=hook, name="attn_bwd")


def _rel_bucket(rel):
    nb = 16
    max_exact = nb // 2
    n = jnp.abs(rel)
    large = max_exact + (jnp.log(jnp.maximum(n, 1).astype(F32) / max_exact) / math.log(1024 / max_exact) * (nb - max_exact)).astype(jnp.int32)
    large = jnp.minimum(large, nb - 1)
    return jnp.where(rel > 0, nb, 0) + jnp.where(n < max_exact, n, large)


def _attn_bias(rel_table):
    rel = jnp.arange(KW)[None, :] - ATTN_HALF - jnp.arange(QB)[:, None]
    inside = (jnp.abs(rel) <= ATTN_HALF)
    out = []
    for dil in ATTN_DILS:
        one_hot = (_rel_bucket(rel * dil)[None] == jnp.arange(32)[:, None, None]).astype(F32)
        b = jnp.einsum("kh,kts->hts", rel_table, one_hot, precision=HI)
        out.append(jnp.where(inside[None], b, NEG_INF))
    return jnp.stack(out).astype(F32)


def _place():
    return lax.axis_index("x"), lax.axis_index("y"), lax.axis_index("c")


def _allgather8(buf, name):
    rows = buf.shape[0]
    flips = [(fx, fy, fc) for fx in (0, 1) for fy in (0, 1) for fc in (0, 1)][1:]

    def body(in_ref, out_ref, send_sems, recv_sems, local_sem):
        x, y, c = _place()
        me = 4 * x + 2 * y + c
        mine = pltpu.make_async_copy(in_ref, out_ref.at[me], local_sem)
        mine.start()
        peers = [(1 - x if fx else x, 1 - y if fy else y, 1 - c if fc else c) for fx, fy, fc in flips]

        def copy(k, slot, peer):
            return pltpu.make_async_remote_copy(src_ref=in_ref, dst_ref=out_ref.at[slot], send_sem=send_sems.at[k],
                                                recv_sem=recv_sems.at[k], device_id=peer, device_id_type=MESH)

        sends = [copy(k, me, peer) for k, peer in enumerate(peers)]
        for cp in sends:
            cp.start()
        for k, (px, py, pc) in enumerate(peers):
            copy(k, 4 * px + 2 * py + pc, (px, py, pc)).wait_recv()
        for cp in sends:
            cp.wait_send()
        mine.wait()

    return pl.pallas_call(body, in_specs=[ANY], out_specs=ANY, out_shape=S_((N_DEV, rows, 128), F32),
                          scratch_shapes=[pltpu.SemaphoreType.DMA((7,)), pltpu.SemaphoreType.DMA((7,)), pltpu.SemaphoreType.DMA(())],
                          name=name)(buf)


N_BIG = 4


def _other_chips(x, y):
    return [(1 - x, y), (x, 1 - y), (1 - x, 1 - y)]


def _hooked_call(body, *, grid, in_specs, out_specs, out_shape, scratch_shapes, args, hook, name):
    if hook is None:
        res = pl.pallas_call(body, grid=grid, in_specs=in_specs, out_specs=out_specs, out_shape=out_shape,
                             scratch_shapes=scratch_shapes, compiler_params=_cp(), name=name)(*args)
        return res, None
    counts = (len(in_specs), len(hook["arrays"]), len(out_specs), len(hook["out_shape"]), len(scratch_shapes), len(hook["sems"]))

    def wrapped(*refs):
        groups, pos = [], 0
        for n in counts:
            groups.append(refs[pos:pos + n])
            pos += n
        ins, h_ins, outs, h_outs, scr, sems = groups
        idx = [pl.program_id(a) for a in range(len(grid))]
        first = functools.reduce(jnp.logical_and, [i == 0 for i in idx])
        last = functools.reduce(jnp.logical_and, [i == g - 1 for i, g in zip(idx, grid)])

        @pl.when(first)
        def _():
            hook["start"](h_ins, h_outs, sems)

        body(*ins, *outs, *scr)

        @pl.when(last)
        def _():
            hook["finish"](h_ins, h_outs, sems)

    res = pl.pallas_call(wrapped, grid=grid, in_specs=list(in_specs) + [ANY] * counts[1], out_specs=list(out_specs) + [ANY] * counts[3],
                         out_shape=list(out_shape) + list(hook["out_shape"]), scratch_shapes=list(scratch_shapes) + list(hook["sems"]),
                         compiler_params=_cp(), name=name + "_" + hook["name"])(*args, *hook["arrays"])
    return res[:counts[2]], res[counts[2]:]


def _run_hook(hook):
    n_in, n_out = len(hook["arrays"]), len(hook["out_shape"])

    def body(*refs):
        h_ins, h_outs, sems = refs[:n_in], refs[n_in:n_in + n_out], refs[n_in + n_out:]
        hook["start"](h_ins, h_outs, sems)
        hook["finish"](h_ins, h_outs, sems)

    return pl.pallas_call(body, in_specs=[ANY] * n_in, out_specs=[ANY] * n_out, out_shape=list(hook["out_shape"]),
                          scratch_shapes=list(hook["sems"]), name=hook["name"])(*hook["arrays"])


def _remote(src, dst, send_sem, recv_sem, peer):
    return pltpu.make_async_remote_copy(src_ref=src, dst_ref=dst, send_sem=send_sem, recv_sem=recv_sem, device_id=peer, device_id_type=MESH)


def _gather_hook(shards):
    def copies(h_ins, h_outs, sems, kind):
        ici_send, ici_recv, d2d_send, d2d_recv = sems
        x, y, c = _place()
        chip = 2 * x + y
        out = []
        for t in range(len(shards)):
            half = shards[t].shape[0] // 2
            mine_r, other_r = pl.ds(c * half, half), pl.ds((1 - c) * half, half)
            for f, (px, py) in enumerate(_other_chips(x, y)):
                k, peer_chip = 3 * t + f, 2 * px + py
                if kind in ("send", "land"):
                    slot = chip if kind == "send" else peer_chip
                    out.append(_remote(h_ins[t].at[mine_r], h_outs[t].at[slot, mine_r], ici_send.at[k], ici_recv.at[k], (px, py, c)))
                else:
                    rows = mine_r if kind == "pass" else other_r
                    out.append(_remote(h_outs[t].at[peer_chip, rows], h_outs[t].at[peer_chip, rows], d2d_send.at[k], d2d_recv.at[k],
                                       (x, y, 1 - c)))
        return out

    def start(h_ins, h_outs, sems):
        for cp in copies(h_ins, h_outs, sems, "send"):
            cp.start()

    def finish(h_ins, h_outs, sems):
        passed = copies(h_ins, h_outs, sems, "pass")
        for landed, forward in zip(copies(h_ins, h_outs, sems, "land"), passed):
            landed.wait_recv()
            forward.start()
        for cp in copies(h_ins, h_outs, sems, "get"):
            cp.wait_recv()
        for cp in copies(h_ins, h_outs, sems, "send") + passed:
            cp.wait_send()

    return dict(name="gather", arrays=list(shards), out_shape=[S_((N_CHIPS,) + s.shape, s.dtype) for s in shards],
                sems=[pltpu.SemaphoreType.DMA((3 * len(shards),)) for _ in range(4)], start=start, finish=finish)


def _to_sibling_hook(parts, half_rows=False):
    def copies(h_ins, h_outs, sems):
        x, y, c = _place()
        out = []
        for t, p in enumerate(parts):
            src = h_ins[t].at[:, pl.ds((1 - c) * (p.shape[1] // 2), p.shape[1] // 2)] if half_rows else h_ins[t]
            out.append(_remote(src, h_outs[t], sems[0].at[t], sems[1].at[t], (x, y, 1 - c)))
        return out

    def start(h_ins, h_outs, sems):
        for cp in copies(h_ins, h_outs, sems):
            cp.start()

    def finish(h_ins, h_outs, sems):
        cps = copies(h_ins, h_outs, sems)
        for cp in cps:
            cp.wait_recv()
        for cp in cps:
            cp.wait_send()

    shapes = [(p.shape[0], p.shape[1] // 2, p.shape[2]) if half_rows else p.shape for p in parts]
    return dict(name="to_sibling", arrays=list(parts), out_shape=[S_(s, p.dtype) for s, p in zip(shapes, parts)],
                sems=[pltpu.SemaphoreType.DMA((len(parts),)), pltpu.SemaphoreType.DMA((len(parts),))], start=start, finish=finish)


def _to_chips_hook(parts):
    def copies(h_ins, h_outs, sems):
        x, y, c = _place()
        return [_remote(h_ins[t].at[2 * px + py], h_outs[t].at[f], sems[0].at[3 * t + f], sems[1].at[3 * t + f], (px, py, c))
                for t in range(len(parts)) for f, (px, py) in enumerate(_other_chips(x, y))]

    def start(h_ins, h_outs, sems):
        for cp in copies(h_ins, h_outs, sems):
            cp.start()

    def finish(h_ins, h_outs, sems):
        cps = copies(h_ins, h_outs, sems)
        for cp in cps:
            cp.wait_recv()
        for cp in cps:
            cp.wait_send()

    return dict(name="to_chips", arrays=list(parts), out_shape=[S_((3,) + p.shape[1:], p.dtype) for p in parts],
                sems=[pltpu.SemaphoreType.DMA((3 * len(parts),)), pltpu.SemaphoreType.DMA((3 * len(parts),))], start=start, finish=finish)


def _add_pair(a, b, core, name):
    n, half, c = b.shape

    def body(core_ref, a_ref, b_ref, o_ref):
        o_ref[...] = (a_ref[...].astype(F32) + b_ref[...].astype(F32)).astype(BF16)

    spec = pltpu.PrefetchScalarGridSpec(
        num_scalar_prefetch=1, grid=(n,),
        in_specs=[pl.BlockSpec((None, half, c), lambda i, core_ref: (i, core_ref[0], 0)),
                  pl.BlockSpec((None, half, c), lambda i, core_ref: (i, 0, 0))],
        out_specs=pl.BlockSpec((None, half, c), lambda i, core_ref: (i, 0, 0)))
    return pl.pallas_call(body, grid_spec=spec, out_shape=S_(b.shape, BF16), name=name)(core.reshape(1).astype(jnp.int32), a, b)


def _add_four(own, got, rows, name):
    n, r, c = own.shape

    def body(a_ref, g_ref, o_ref):
        o_ref[...] = ((a_ref[...].astype(F32) + g_ref[0].astype(F32)) + g_ref[1].astype(F32)) + g_ref[2].astype(F32)

    blk = pl.BlockSpec((None, rows, c), lambda i, j: (i, j, 0))
    return pl.pallas_call(body, grid=(n, r // rows), in_specs=[blk, pl.BlockSpec((3, None, rows, c), lambda i, j: (0, i, j, 0))],
                          out_specs=blk, out_shape=S_(own.shape, F32), name=name)(own, got)


def _sum_slots(slots):
    rows = slots.shape[1]

    def body(s_ref, o_ref):
        tot = s_ref[0]
        for k in range(1, N_DEV):
            tot = tot + s_ref[k]
        o_ref[...] = tot

    return pl.pallas_call(body, out_shape=S_((rows, 128), F32), name="sum_slots")(slots)


SMALL = ("mix_norm_gain", "ssd_conv_w", "ssd_conv_b", "ssd_dt_bias", "ssd_a_log", "ssd_d", "ssd_out_gain", "gmlp_v_gain",
         "gmlp_w_s", "gmlp_b_s", "attn_q_gain", "attn_k_gain", "rel_bias_table", "ffn_norm_gain", "ffn_conv_w", "ffn_conv_b")
BIG = ("w_in", "w_out", "ffn_w_up", "ffn_w_down")
WEIGHTS = ("mix_norm_gain", "w_in", "ssd_conv_w", "ssd_conv_b", "ssd_dt_bias", "ssd_a_log", "ssd_d", "ssd_out_gain", "gmlp_v_gain",
           "gmlp_w_s", "gmlp_b_s", "attn_q_gain", "attn_k_gain", "rel_bias_table", "w_out", "ffn_norm_gain", "ffn_w_up",
           "ffn_conv_w", "ffn_conv_b", "ffn_w_down")
ADAM_ROWS = {"w_in": 512, "w_out": 256, "ffn_w_up": 256, "ffn_w_down": 352}


PACK_ROWS = 64


def _packed_rows(shape):
    return -(-int(np.prod(shape)) // 1024) * 8


def _pack(arrays):
    parts = []
    for a in arrays:
        rows = _packed_rows(a.shape)
        flat = a.reshape(-1).astype(F32)
        parts.append(jnp.pad(flat, (0, rows * 128 - flat.shape[0])).reshape(rows, 128))
    total = sum(p.shape[0] for p in parts)
    tail = -total % PACK_ROWS
    if tail:
        parts.append(jnp.zeros((tail, 128), F32))
    return jnp.concatenate(parts, axis=0)


def _unpack(buf, shapes):
    out, row = [], 0
    for s in shapes:
        rows, n = _packed_rows(s), int(np.prod(s))
        out.append(buf[row:row + rows].reshape(-1)[:n].reshape(s))
        row += rows
    return out


def _perm_cols(w):
    pad = jnp.zeros(w.shape[:-1] + (NP - IN_WIDTH,), w.dtype)
    return jnp.concatenate([w[..., :1536], w[..., 1552:], w[..., 1536:1552], pad], axis=-1)


def _unperm_cols(w):
    return jnp.concatenate([w[..., :1536], w[..., C_DT:C_DT + 16], w[..., 1536:C_DT]], axis=-1)


def _layer_params(l, p, conv5_w, conv3_w, bias):
    def make(mix_g, conv5, conv5_b, dt_bias, a_log, d_skip, out_gain, v_gain, w_s, b_s, q_gain, k_gain, ffn_g, conv3, conv3_b):
        lanes = lambda a: jnp.pad(a.reshape(1, 16), ((0, 0), (0, 112)))
        cw3 = jnp.pad(jnp.transpose(conv3.reshape(3, 2, FFN_DIM), (1, 0, 2)), ((0, 0), (0, 5), (0, 0)))
        return dict(mix_g=mix_g.reshape(1, D_MODEL), cw5=jnp.pad(conv5, ((0, 3), (0, 0))), cb5=conv5_b.reshape(1, SSD_XBC),
                    dtb=lanes(dt_bias), alog=lanes(a_log), d_exp=jnp.repeat(d_skip, HEAD).reshape(1, SSD_INNER),
                    out_gain=out_gain.reshape(1, SSD_INNER), v_gain=v_gain.reshape(1, GMLP_W), w_s=w_s,
                    b_exp=jnp.repeat(b_s.T, HEAD, axis=1), q_gain=jnp.tile(q_gain, 2).reshape(1, PAIR_W),
                    k_gain=jnp.tile(k_gain, 2).reshape(1, PAIR_W), ffn_g=ffn_g.reshape(1, D_MODEL), cw3=cw3,
                    cb3=conv3_b.reshape(2, 1, FFN_DIM))

    args = (p["mix_norm_gain"][l], conv5_w[l], p["ssd_conv_b"][l], p["ssd_dt_bias"][l], p["ssd_a_log"][l], p["ssd_d"][l],
            p["ssd_out_gain"][l], p["gmlp_v_gain"][l], p["gmlp_w_s"][l], p["gmlp_b_s"][l], p["attn_q_gain"][l], p["attn_k_gain"][l],
            p["ffn_norm_gain"][l], conv3_w[l], p["ffn_conv_b"][l])
    return jax.vjp(make, *args)


def _forward_layer(x, lp, w, bias, hooks=None, resolve=None):
    hooks = hooks or {}
    h = _rmsnorm_fwd(x, lp["mix_g"], "rmsnorm_fwd")
    proj = _mm_nn(h, w["w_in"], tm=1024, tn=1024, tk=1024, out_dtype=F32, name="mm_proj")
    xc = _ssd_pre_fwd(proj, lp["cw5"], lp["cb5"])
    (y_ssd, ssd_sum, ssd_states), got_self = _ssd_scan_fwd(xc, proj, lp["dtb"], lp["alog"], lp["d_exp"], lp["out_gain"],
                                                           hook=hooks.get("self"))
    if got_self is not None:
        w = dict(w, **resolve(got_self))
    y_gmlp = _gmlp_fwd(proj, lp["v_gain"], lp["w_s"], lp["b_exp"])
    (y_attn, attn_o, attn_l), got_attn = _attn_fwd(proj, lp["q_gain"], lp["k_gain"], bias, hook=hooks.get("attn"))
    y = jnp.concatenate([y_ssd, y_gmlp, y_attn], axis=1)
    x2 = _mm_nn(y, w["w_out"], tm=1024, tn=1024, tk=1024, out_dtype=F32, res=x, name="mm_out")
    hn = _rmsnorm_fwd(x2, lp["ffn_g"], "rmsnorm_fwd")
    up3 = _mm_up(hn, w["ffn_w_up"])
    act, got_gate = _convgate_fwd(up3, lp["cw3"], lp["cb3"], hook=hooks.get("gate"))
    x3 = _mm_nn(act, w["ffn_w_down"], tm=1024, tn=1024, tk=HALF_TILE, out_dtype=F32, res=x2, name="mm_down")
    saved = dict(x=x, h=h, proj=proj, xc=xc, y=y, x2=x2, hn=hn, up3=up3, act=act, attn_o=attn_o, attn_l=attn_l,
                 ssd_sum=ssd_sum, ssd_states=ssd_states)
    return x3, saved, w, dict(attn=got_attn, gate=got_gate)


def _backward_layer(dx3, sv, lp, w, bias, pending=None, reducer=None):
    d_act = _mm_nt(dx3, w["ffn_w_down"], tm=1024, tn=HALF_TILE, tk=1024, out_dtype=F32, name="mm_dact")
    dw_down = _mm_tn(sv["act"], dx3, tm=HALF_TILE, tn=1024, tk=1024, out_dtype=BF16, name="mm_dwdown")
    (dup3, dcw3), from_sibling = _convgate_bwd(sv["up3"], lp["cw3"], lp["cb3"], d_act, hook=pending.sibling_hook() if pending else None)
    if pending:
        pending.add_sibling(from_sibling)
    d_hn = _mm_dhn(dup3, w["ffn_w_up"])
    dw_up = _mm_dwup(sv["hn"], dup3)
    dx2, d_ffn_g = _rmsnorm_bwd(sv["x2"], lp["ffn_g"], d_hn, dx3, "rmsnorm_bwd")
    d_y = _mm_nt(dx2, w["w_out"], tm=1024, tn=1024, tk=1024, out_dtype=F32, name="mm_dy")
    dw_out = _mm_tn(sv["y"], dx2, tm=1024, tn=1024, tk=1024, out_dtype=BF16, name="mm_dwout")
    early = reducer(("w_out", "ffn_w_up", "ffn_w_down"), (dw_out, dw_up, dw_down)) if reducer else None
    proj, xc = sv["proj"], sv["xc"]
    (dxs, dbc, dcc, dz, ddt2, ddtb2, dal2, d_dexp, d_outg), from_chips = _ssd_scan_bwd(
        xc, proj, lp["dtb"], lp["alog"], lp["d_exp"], lp["out_gain"], d_y, sv["ssd_sum"], sv["ssd_states"],
        hook=pending.chips_hook() if pending else None)
    if pending:
        pending.add_chips(from_chips)
    (d_xbc, dcw5), from_sibling = _ssd_pre_bwd(proj, lp["cw5"], lp["cb5"], jnp.concatenate([dxs, dbc, dcc], axis=1),
                                               hook=early.sibling_hook() if early else None)
    if early:
        early.add_sibling(from_sibling)
    d_gu, d_gv, d_vg, d_ws, d_bexp = _gmlp_bwd(proj, lp["v_gain"], lp["w_s"], lp["b_exp"], d_y)
    (d_q, d_k, d_v, d_qg2, d_kg2, d_bias), from_chips = _attn_bwd(proj, lp["q_gain"], lp["k_gain"], bias, d_y, sv["attn_o"], sv["attn_l"],
                                                                  hook=early.chips_hook() if early else None)
    if early:
        early.add_chips(from_chips)
    d_dt = (ddt2[0] + ddt2[1]).astype(BF16)
    d_proj = jnp.concatenate([d_xbc, dz, d_gu, d_gv, d_q, d_k, d_v, d_dt, jnp.zeros((T, NP - C_DT - 128), BF16)], axis=1)
    d_h = _mm_nt(d_proj, w["w_in"], tm=1024, tn=1024, tk=1024, out_dtype=F32, name="mm_dh")
    dw_in = _mm_tn(sv["h"], d_proj, tm=1024, tn=1024, tk=1024, out_dtype=BF16, name="mm_dwin")
    late = None
    if reducer:
        late = reducer(("w_in",), (dw_in,))
        late.run_alone()
    dx, d_mix_g = _rmsnorm_bwd(sv["x"], lp["mix_g"], d_h, dx2, "rmsnorm_bwd")
    d_lp = dict(mix_g=d_mix_g, cw5=dcw5[:8] * (jnp.arange(8) < 5)[:, None].astype(F32), cb5=dcw5[5:6],
                dtb=(ddtb2[0, :1] + ddtb2[1, :1]), alog=(dal2[0, :1] + dal2[1, :1]), d_exp=d_dexp, out_gain=d_outg,
                v_gain=d_vg, w_s=d_ws, b_exp=d_bexp, q_gain=d_qg2[0, :1] + d_qg2[1, :1], k_gain=d_kg2[0, :1] + d_kg2[1, :1],
                ffn_g=d_ffn_g, cw3=dcw3 * (jnp.arange(8) < 3)[None, :, None].astype(F32), cb3=dcw3[:, 3:4])
    return dx, dict(w_in=dw_in, w_out=dw_out, ffn_w_up=dw_up, ffn_w_down=dw_down), d_lp, d_bias, (early, late)


def _to_shard_major(name, dw):
    if name == "ffn_w_up":
        return dw
    if name == "w_in":
        r, c = dw.shape[0], IN_WIDTH
        return jnp.transpose(_unperm_cols(dw).reshape(r, N_CHIPS, c // N_CHIPS), (1, 0, 2))
    r, c = dw.shape
    return dw.reshape(N_CHIPS, r // N_CHIPS, c)


def _whole_weight(name, gathered, own, chip):
    if name == "w_in":
        return _perm_cols(jnp.concatenate([jnp.where(chip == k, own, gathered[k]) for k in range(N_CHIPS)], axis=1))
    w = lax.dynamic_update_index_in_dim(gathered, own, chip, axis=0)
    return w if name == "ffn_w_up" else w.reshape(N_CHIPS * own.shape[0], own.shape[1])


class _LayerReduce:
    def __init__(self, names, dws, chip, core):
        self.names, self.chip, self.core = names, chip, core
        self.parts = [_to_shard_major(n, dw) for n, dw in zip(names, dws)]

    def sibling_hook(self):
        return _to_sibling_hook(self.parts, half_rows=True)

    def add_sibling(self, got):
        self.sums = [_add_pair(a, b, self.core, "add_pair_" + n) for n, a, b in zip(self.names, self.parts, got)]

    def chips_hook(self):
        return _to_chips_hook(self.sums)

    def add_chips(self, got):
        self.half = {}
        for n, s2, g3 in zip(self.names, self.sums, got):
            own = lax.dynamic_index_in_dim(s2, self.chip, axis=0, keepdims=True)
            self.half[n] = _add_four(own, g3[:, None], own.shape[1], "add_four_" + n)[0]

    def run_alone(self):
        self.add_sibling(_run_hook(self.sibling_hook()))
        self.add_chips(_run_hook(self.chips_hook()))


def _join_halves(halves, core):
    other = _run_hook(dict(_to_sibling_hook(halves), name="swap_halves"))
    out = []
    for mine_h, other_h in zip(halves, other):
        both = jnp.stack([mine_h, other_h])
        first = lax.dynamic_index_in_dim(both, core, axis=0, keepdims=False)
        second = lax.dynamic_index_in_dim(both, 1 - core, axis=0, keepdims=False)
        out.append(jnp.concatenate([first, second], axis=1))
    return out


LAYER_SMALL = ("mix_norm_gain", "ssd_conv_w", "ssd_conv_b", "ssd_dt_bias", "ssd_a_log", "ssd_d", "ssd_out_gain", "gmlp_v_gain",
               "gmlp_w_s", "gmlp_b_s", "attn_q_gain", "attn_k_gain", "ffn_norm_gain", "ffn_conv_w", "ffn_conv_b")


def _local_grads(x, loss_target, p, conv5_w, conv3_w, layer_w, exchange=None):
    bias, bias_vjp = jax.vjp(_attn_bias, p["rel_bias_table"])
    xt = x.reshape(T, D_MODEL)
    layer_w = list(layer_w)
    saved, lps, lp_vjps = [], [], []
    if exchange is not None:
        chip, core, own = exchange
        whole = lambda names, layer, gathered: {n: _whole_weight(n, g, own[layer][BIG.index(n)], chip) for n, g in zip(names, gathered)}
    for l in range(DEPTH):
        lp, lp_vjp = _layer_params(l, p, conv5_w, conv3_w, bias)
        hooks = {}
        if exchange is not None and l == 0:
            hooks["self"] = _gather_hook(own[0][1:])
        if exchange is not None and l + 1 < DEPTH:
            hooks["attn"] = _gather_hook(own[l + 1][2:])
            hooks["gate"] = _gather_hook(own[l + 1][:2])
        xt, sv, layer_w[l], got = _forward_layer(xt, lp, layer_w[l], bias, hooks, resolve=lambda g: whole(BIG[1:], 0, g))
        if "attn" in hooks:
            layer_w.append(dict(whole(BIG[:2], l + 1, got["gate"]), **whole(BIG[2:], l + 1, got["attn"])))
        saved.append(sv)
        lps.append(lp)
        lp_vjps.append(lp_vjp)
    dxt, loss_parts = _loss_head(xt, loss_target.reshape(T, D_MODEL))
    loss_local = jnp.sum(loss_parts[::8, 0])

    big_grads = [None] * DEPTH
    small_layers = [None] * DEPTH
    d_bias_tot = jnp.zeros_like(bias)
    pending = None
    for l in reversed(range(DEPTH)):
        last = exchange is not None and l == 0
        dxt, big_grads[l], d_lp, d_bias, own_reduce = _backward_layer(
            dxt, saved[l], lps[l], layer_w[l], bias, pending=pending,
            reducer=(lambda names, dws: _LayerReduce(names, dws, chip, core)) if last else None)
        if pending is not None:
            big_grads[l + 1] = pending.half
        if last:
            big_grads[l] = dict(own_reduce[0].half, **own_reduce[1].half)
        elif exchange is not None:
            pending = _LayerReduce(BIG, [big_grads[l][n] for n in BIG], chip, core)
        small_layers[l] = lp_vjps[l](d_lp)
        d_bias_tot = d_bias_tot + d_bias
    (d_rel_table,) = bias_vjp(d_bias_tot)
    local_small = {n: jnp.stack([small_layers[l][i] for l in range(DEPTH)]) for i, n in enumerate(LAYER_SMALL)}
    local_small["rel_bias_table"] = d_rel_table
    return dxt, loss_local, big_grads, local_small


def kernel(x, mix_norm_gain, w_in, ssd_conv_w, ssd_conv_b, ssd_dt_bias, ssd_a_log, ssd_d, ssd_out_gain, gmlp_v_gain, gmlp_w_s, gmlp_b_s, attn_q_gain, attn_k_gain, rel_bias_table, w_out, ffn_norm_gain, ffn_w_up, ffn_conv_w, ffn_conv_b, ffn_w_down, loss_target, m_mix_norm_gain, m_w_in, m_ssd_conv_w, m_ssd_conv_b, m_ssd_dt_bias, m_ssd_a_log, m_ssd_d, m_ssd_out_gain, m_gmlp_v_gain, m_gmlp_w_s, m_gmlp_b_s, m_attn_q_gain, m_attn_k_gain, m_rel_bias_table, m_w_out, m_ffn_norm_gain, m_ffn_w_up, m_ffn_conv_w, m_ffn_conv_b, m_ffn_w_down, v_mix_norm_gain, v_w_in, v_ssd_conv_w, v_ssd_conv_b, v_ssd_dt_bias, v_ssd_a_log, v_ssd_d, v_ssd_out_gain, v_gmlp_v_gain, v_gmlp_w_s, v_gmlp_b_s, v_attn_q_gain, v_attn_k_gain, v_rel_bias_table, v_w_out, v_ffn_norm_gain, v_ffn_w_up, v_ffn_conv_w, v_ffn_conv_b, v_ffn_w_down):
    env = dict(locals())
    p = {n: env[n] for n in WEIGHTS}
    chip = 2 * lax.axis_index("x") + lax.axis_index("y")
    core = lax.axis_index("c")

    conv_slots = _allgather8(_pack([ssd_conv_w, ffn_conv_w]), "allgather_conv")
    conv_parts = [_unpack(conv_slots[2 * k], [ssd_conv_w.shape, ffn_conv_w.shape]) for k in range(N_CHIPS)]
    conv5_w = jnp.concatenate([cp[0] for cp in conv_parts], axis=-1)
    conv3_w = jnp.concatenate([cp[1] for cp in conv_parts], axis=-1)
    own = [[p[n][l].astype(BF16) for n in BIG] for l in range(DEPTH)]
    (first,) = _run_hook(_gather_hook(own[0][:1]))
    layer_w = [{"w_in": _whole_weight("w_in", first, own[0][0], chip)}]

    dxt, loss_local, reduced, local_small = _local_grads(x, loss_target, p, conv5_w, conv3_w, layer_w, exchange=(chip, core, own))

    small_shapes = [local_small[n].shape for n in SMALL] + [(1,)]
    slots = _allgather8(_pack([local_small[n] for n in SMALL] + [loss_local.reshape(1)]), "allgather_small")
    summed = _unpack(_sum_slots(slots), small_shapes)
    grads = dict(zip(SMALL, summed[:-1]))
    loss = summed[-1][0]
    grads["ssd_conv_w"] = lax.dynamic_slice_in_dim(grads["ssd_conv_w"], chip * 256, 256, axis=2)
    grads["ffn_conv_w"] = lax.dynamic_slice_in_dim(grads["ffn_conv_w"], chip * (2 * FFN_DIM // N_CHIPS), 2 * FFN_DIM // N_CHIPS, axis=2)

    halves = [jnp.stack([reduced[l][n] for l in range(DEPTH)]) for n in BIG]
    for n, g in zip(BIG, _join_halves(halves, core)):
        grads[n] = g

    delta, new_m, new_v = {}, {}, {}
    for n in BIG:
        delta[n], new_m[n], new_v[n] = _adamw(p[n], grads[n], env["m_" + n], env["v_" + n], ADAM_ROWS[n], "adamw_" + n)
    shapes = [p[n].shape for n in SMALL]
    d, nm, nv = _adamw(_pack([p[n] for n in SMALL]), _pack([grads[n] for n in SMALL]), _pack([env["m_" + n] for n in SMALL]),
                       _pack([env["v_" + n] for n in SMALL]), PACK_ROWS, "adamw_small")
    for n, a, b, c in zip(SMALL, _unpack(d, shapes), _unpack(nm, shapes), _unpack(nv, shapes)):
        delta[n], new_m[n], new_v[n] = a, b, c

    return (loss, dxt.reshape(B_LOC, SEQ, D_MODEL), *[grads[n] for n in WEIGHTS], *[delta[n] for n in WEIGHTS],
            *[new_m[n] for n in WEIGHTS], *[new_v[n] for n in WEIGHTS])
```

```python
import functools
import math

import jax
import jax.numpy as jnp
import numpy as np
from jax import lax
from jax.experimental import pallas as pl
from jax.experimental.pallas import tpu as pltpu

F32 = jnp.float32
BF16 = jnp.bfloat16
HI = lax.Precision.HIGHEST
SUM_PRECISION = lax.Precision.HIGH
MESH = pl.DeviceIdType.MESH
ANY = pl.BlockSpec(memory_space=pl.ANY)

D_MODEL = 1024
SEQ = 2048
B_LOC = 2
T = B_LOC * SEQ
DEPTH = 4
N_CHIPS = 4
N_DEV = 8
HEAD = 64
CHUNK = 128
N_CHUNK = SEQ // CHUNK
SSD_INNER = 512
SSD_XBC = 1024
FFN_DIM = 2816
IN_WIDTH = 2832
NP = 3072
C_XS, C_B, C_C, C_Z, C_GU, C_GV, C_Q, C_K, C_V, C_DT = 0, 512, 768, 1024, 1536, 1792, 2048, 2304, 2560, 2816
NORM_EPS = 1e-6
NEG_INF = -1e30
ATTN_DILS = (1, 4, 16)
ATTN_HALF = 64
ADAM_LR, ADAM_B1, ADAM_B2, ADAM_EPS, ADAM_WD, ADAM_STEP = 0.001, 0.9, 0.999, 1e-08, 0.01, 10
VMEM_LIMIT = 56 * 1024 * 1024

S_ = jax.ShapeDtypeStruct


def _cp():
    return pltpu.CompilerParams(vmem_limit_bytes=VMEM_LIMIT)


def _shift_rows(x, k):
    n = x.shape[0]
    if k == 0:
        return x
    r = pltpu.roll(x, (-k) % n, 0)
    t = lax.broadcasted_iota(jnp.int32, (n, 1), 0)
    return jnp.where((t + k >= 0) & (t + k < n), r, 0.0)


@functools.partial(jax.custom_vjp, nondiff_argnums=(1,))
def _shift(x, k):
    return _shift_rows(x, k)


def _shift_fwd(x, k):
    return _shift_rows(x, k), None


def _shift_bwd(k, _, g):
    return (_shift_rows(g, -k),)


_shift.defvjp(_shift_fwd, _shift_bwd)


def _dwconv(x, taps, bias):
    half = len(taps) // 2
    y = bias
    for k, w in enumerate(taps):
        y = y + w * _shift(x, k - half)
    return y


def _softplus(x):
    return jnp.maximum(x, 0.0) + jnp.log1p(jnp.exp(-jnp.abs(x)))


def _dot(a, b):
    return jnp.dot(a.astype(BF16), b.astype(BF16), preferred_element_type=F32)


def _dot_nt(a, b):
    return lax.dot_general(a.astype(BF16), b.astype(BF16), (((1,), (1,)), ((), ())), preferred_element_type=F32)


def _dot_tn(a, b):
    return lax.dot_general(a.astype(BF16), b.astype(BF16), (((0,), (0,)), ((), ())), preferred_element_type=F32)


def _head_sum_matrix(width):
    i = lax.broadcasted_iota(jnp.int32, (width, width), 0) // HEAD
    j = lax.broadcasted_iota(jnp.int32, (width, width), 1) // HEAD
    return (i == j).astype(F32)


def _matmul(a, b, *, dims, grid, a_spec, b_spec, o_spec, out_shape, acc_shape, res=None, res_spec=None, name):
    nk = grid[2]

    def body(*refs):
        if res is not None:
            a_ref, b_ref, r_ref, o_ref = refs[:4]
        else:
            a_ref, b_ref, o_ref = refs[:3]
            r_ref = None
        part = lax.dot_general(a_ref[...].astype(BF16), b_ref[...].astype(BF16), dims, preferred_element_type=F32)
        if nk == 1:
            if r_ref is not None:
                part = part + r_ref[...]
            o_ref[...] = part.astype(o_ref.dtype)
            return
        acc_ref = refs[-1]
        k = pl.program_id(2)

        @pl.when(k == 0)
        def _():
            acc_ref[...] = part

        @pl.when(k > 0)
        def _():
            acc_ref[...] += part

        @pl.when(k == nk - 1)
        def _():
            tot = acc_ref[...]
            if r_ref is not None:
                tot = tot + r_ref[...]
            o_ref[...] = tot.astype(o_ref.dtype)

    in_specs = [a_spec, b_spec] + ([res_spec] if res is not None else [])
    args = (a, b) + ((res,) if res is not None else ())
    scratch = [] if nk == 1 else [pltpu.VMEM(acc_shape, F32)]
    return pl.pallas_call(body, grid=grid, in_specs=in_specs, out_specs=o_spec, out_shape=out_shape,
                          scratch_shapes=scratch, compiler_params=_cp(), name=name)(*args)


NN = (((1,), (0,)), ((), ()))
NT = (((1,), (1,)), ((), ()))
TN = (((0,), (0,)), ((), ()))


def _mm_nn(a, b, *, tm, tn, tk, out_dtype, res=None, name):
    m, k = a.shape
    n = b.shape[1]
    return _matmul(a, b, dims=NN, grid=(m // tm, n // tn, k // tk),
                   a_spec=pl.BlockSpec((tm, tk), lambda i, j, q: (i, q)),
                   b_spec=pl.BlockSpec((tk, tn), lambda i, j, q: (q, j)),
                   o_spec=pl.BlockSpec((tm, tn), lambda i, j, q: (i, j)),
                   out_shape=S_((m, n), out_dtype), acc_shape=(tm, tn), res=res,
                   res_spec=pl.BlockSpec((tm, tn), lambda i, j, q: (i, j)), name=name)


def _mm_nt(a, b, *, tm, tn, tk, out_dtype, name):
    m, k = a.shape
    n = b.shape[0]
    return _matmul(a, b, dims=NT, grid=(m // tm, n // tn, k // tk),
                   a_spec=pl.BlockSpec((tm, tk), lambda i, j, q: (i, q)),
                   b_spec=pl.BlockSpec((tn, tk), lambda i, j, q: (j, q)),
                   o_spec=pl.BlockSpec((tm, tn), lambda i, j, q: (i, j)),
                   out_shape=S_((m, n), out_dtype), acc_shape=(tm, tn), name=name)


def _mm_tn(a, b, *, tm, tn, tk, out_dtype, name):
    k, m = a.shape
    n = b.shape[1]
    return _matmul(a, b, dims=TN, grid=(m // tm, n // tn, k // tk),
                   a_spec=pl.BlockSpec((tk, tm), lambda i, j, q: (q, i)),
                   b_spec=pl.BlockSpec((tk, tn), lambda i, j, q: (q, j)),
                   o_spec=pl.BlockSpec((tm, tn), lambda i, j, q: (i, j)),
                   out_shape=S_((m, n), out_dtype), acc_shape=(tm, tn), name=name)


HALF_TILE = FFN_DIM // 2


def _mm_up(hn, w_up):
    return _matmul(hn, w_up, dims=NN, grid=(T // 1024, 4, 1),
                   a_spec=pl.BlockSpec((1024, D_MODEL), lambda i, j, q: (i, 0)),
                   b_spec=pl.BlockSpec((None, D_MODEL, HALF_TILE), lambda i, j, q: (j, 0, 0)),
                   o_spec=pl.BlockSpec((None, 1024, HALF_TILE), lambda i, j, q: (j // 2, i, j % 2)),
                   out_shape=S_((2, T, FFN_DIM), F32), acc_shape=(1024, HALF_TILE), name="mm_up")


def _mm_dhn(dup3, w_up):
    return _matmul(dup3, w_up, dims=NT, grid=(T // 1024, 1, 4),
                   a_spec=pl.BlockSpec((None, 1024, HALF_TILE), lambda i, j, q: (q // 2, i, q % 2)),
                   b_spec=pl.BlockSpec((None, D_MODEL, HALF_TILE), lambda i, j, q: (q, 0, 0)),
                   o_spec=pl.BlockSpec((1024, D_MODEL), lambda i, j, q: (i, 0)),
                   out_shape=S_((T, D_MODEL), F32), acc_shape=(1024, D_MODEL), name="mm_dhn")


def _mm_dwup(hn, dup3):
    return _matmul(hn, dup3, dims=TN, grid=(1, 4, T // 1024),
                   a_spec=pl.BlockSpec((1024, D_MODEL), lambda i, j, q: (q, 0)),
                   b_spec=pl.BlockSpec((None, 1024, HALF_TILE), lambda i, j, q: (j // 2, q, j % 2)),
                   o_spec=pl.BlockSpec((None, D_MODEL, HALF_TILE), lambda i, j, q: (j, 0, 0)),
                   out_shape=S_((N_CHIPS, D_MODEL, HALF_TILE), BF16), acc_shape=(D_MODEL, HALF_TILE), name="mm_dwup")


ROWS = 512


def _rmsnorm_fwd(x, gain, name):
    def body(x_ref, g_ref, o_ref):
        xv = x_ref[...]
        r = lax.rsqrt(jnp.mean(xv * xv, axis=-1, keepdims=True) + NORM_EPS)
        o_ref[...] = (xv * r * g_ref[...]).astype(BF16)

    return pl.pallas_call(body, grid=(T // ROWS,),
                          in_specs=[pl.BlockSpec((ROWS, D_MODEL), lambda i: (i, 0)), pl.BlockSpec((1, D_MODEL), lambda i: (0, 0))],
                          out_specs=pl.BlockSpec((ROWS, D_MODEL), lambda i: (i, 0)),
                          out_shape=S_((T, D_MODEL), BF16), name=name)(x, gain)


def _rmsnorm_bwd(x, gain, dh, dres, name):
    def body(x_ref, g_ref, dh_ref, dres_ref, dx_ref, dg_ref):
        xv = x_ref[...]
        r = lax.rsqrt(jnp.mean(xv * xv, axis=-1, keepdims=True) + NORM_EPS)
        gd = dh_ref[...] * g_ref[...]
        dot = jnp.mean(gd * xv, axis=-1, keepdims=True)
        dx_ref[...] = dres_ref[...] + r * gd - xv * (r * r * r * dot)
        part = jnp.sum(dh_ref[...] * xv * r, axis=0, keepdims=True)

        @pl.when(pl.program_id(0) == 0)
        def _():
            dg_ref[...] = part

        @pl.when(pl.program_id(0) > 0)
        def _():
            dg_ref[...] += part

    row = pl.BlockSpec((ROWS, D_MODEL), lambda i: (i, 0))
    vec = pl.BlockSpec((1, D_MODEL), lambda i: (0, 0))
    return pl.pallas_call(body, grid=(T // ROWS,), in_specs=[row, vec, row, row], out_specs=[row, vec],
                          out_shape=[S_((T, D_MODEL), F32), S_((1, D_MODEL), F32)], name=name)(x, gain, dh, dres)


def _loss_head(y, target):
    def body(y_ref, t_ref, dy_ref, p_ref):
        e = y_ref[...] - t_ref[...]
        dy_ref[...] = e * (1.0 / D_MODEL)
        p_ref[...] = jnp.full((8, 128), 0.5 / D_MODEL, F32) * jnp.sum(e * e)

    row = pl.BlockSpec((ROWS, D_MODEL), lambda i: (i, 0))
    return pl.pallas_call(body, grid=(T // ROWS,), in_specs=[row, row],
                          out_specs=[row, pl.BlockSpec((8, 128), lambda i: (i, 0))],
                          out_shape=[S_((T, D_MODEL), F32), S_((T // ROWS * 8, 128), F32)], name="loss_head")(y, target)


def _adamw_update(w_ref, g_ref, m_ref, v_ref, d_ref, nm_ref, nv_ref):
    gv = g_ref[...]
    nm = ADAM_B1 * m_ref[...] + (1.0 - ADAM_B1) * gv
    nv = ADAM_B2 * v_ref[...] + (1.0 - ADAM_B2) * (gv * gv)
    m_hat = nm / (1.0 - ADAM_B1 ** ADAM_STEP)
    v_hat = nv / (1.0 - ADAM_B2 ** ADAM_STEP)
    d_ref[...] = -ADAM_LR * (m_hat / (jnp.sqrt(v_hat) + ADAM_EPS) + ADAM_WD * w_ref[...])
    nm_ref[...] = nm
    nv_ref[...] = nv


def _adamw(w, g, m, v, rows, name):
    def body(*refs):
        _adamw_update(*refs)

    blk = pl.BlockSpec((None, rows, w.shape[2]), lambda l, i: (l, i, 0))
    out = S_(w.shape, F32)
    return pl.pallas_call(body, grid=(w.shape[0], w.shape[1] // rows), in_specs=[blk] * 4, out_specs=[blk] * 3,
                          out_shape=[out, out, out], name=name)(w, g, m, v)


def _adamw_many(ws, gs, ms, vs):
    n = len(ws)

    def body(*refs):
        for i in range(n):
            _adamw_update(*[refs[k * n + i] for k in range(7)])

    out = [S_(w.shape, F32) for w in ws]
    res = pl.pallas_call(body, out_shape=out * 3, name="adamw_small")(*ws, *gs, *ms, *vs)
    return res[:n], res[n:2 * n], res[2 * n:]


FFN_CT = 256


def _gate_fn(up_g, up_v, wg0, wg1, wg2, bg, wv0, wv1, wv2, bv):
    gate = _dwconv(up_g, [wg0, wg1, wg2], bg)
    val = _dwconv(up_v, [wv0, wv1, wv2], bv)
    return jax.nn.silu(gate) * val


def _taps(ref, part, n):
    return [ref[part, k:k + 1, :] for k in range(n)]


def _convgate_fwd(up3, cw, cb, hook=None):
    def body(up_ref, cw_ref, cb_ref, o_ref):
        o_ref[...] = _gate_fn(up_ref[0], up_ref[1], *_taps(cw_ref, 0, 3), cb_ref[0], *_taps(cw_ref, 1, 3), cb_ref[1]).astype(BF16)

    (act,), got = _hooked_call(
        body, grid=(FFN_DIM // FFN_CT, B_LOC),
        in_specs=[pl.BlockSpec((2, SEQ, FFN_CT), lambda j, b: (0, b, j)),
                  pl.BlockSpec((2, 8, FFN_CT), lambda j, b: (0, 0, j)),
                  pl.BlockSpec((2, 1, FFN_CT), lambda j, b: (0, 0, j))],
        out_specs=[pl.BlockSpec((SEQ, FFN_CT), lambda j, b: (b, j))],
        out_shape=[S_((T, FFN_DIM), BF16)], scratch_shapes=[], args=(up3, cw, cb), hook=hook, name="convgate_fwd")
    return act, got


def _convgate_bwd(up3, cw, cb, dact, hook=None):
    def body(up_ref, cw_ref, cb_ref, da_ref, dup_ref, dcw_ref):
        args = (up_ref[0], up_ref[1], *_taps(cw_ref, 0, 3), cb_ref[0], *_taps(cw_ref, 1, 3), cb_ref[1])
        _, vjp = jax.vjp(_gate_fn, *args)
        dg, dv, g0, g1, g2, gb, v0, v1, v2, vb = vjp(da_ref[...])
        dup_ref[0] = dg.astype(BF16)
        dup_ref[1] = dv.astype(BF16)
        zero = jnp.zeros((4, FFN_CT), F32)
        new = jnp.stack([jnp.concatenate([g0, g1, g2, gb, zero], axis=0), jnp.concatenate([v0, v1, v2, vb, zero], axis=0)])

        @pl.when(pl.program_id(1) == 0)
        def _():
            dcw_ref[...] = new

        @pl.when(pl.program_id(1) > 0)
        def _():
            dcw_ref[...] += new

    return _hooked_call(
        body, grid=(FFN_DIM // FFN_CT, B_LOC),
        in_specs=[pl.BlockSpec((2, SEQ, FFN_CT), lambda j, b: (0, b, j)),
                  pl.BlockSpec((2, 8, FFN_CT), lambda j, b: (0, 0, j)),
                  pl.BlockSpec((2, 1, FFN_CT), lambda j, b: (0, 0, j)),
                  pl.BlockSpec((SEQ, FFN_CT), lambda j, b: (b, j))],
        out_specs=[pl.BlockSpec((2, SEQ, FFN_CT), lambda j, b: (0, b, j)),
                   pl.BlockSpec((2, 8, FFN_CT), lambda j, b: (0, 0, j))],
        out_shape=[S_((2, T, FFN_DIM), BF16), S_((2, 8, FFN_DIM), F32)],
        scratch_shapes=[], args=(up3, cw, cb, dact), hook=hook, name="convgate_bwd")


SSD_CT = 256


def _conv5_fn(x, w0, w1, w2, w3, w4, b):
    return jax.nn.silu(_dwconv(x, [w0, w1, w2, w3, w4], b))


def _ssd_pre_fwd(proj, cw, cb):
    def body(x_ref, cw_ref, cb_ref, o_ref):
        o_ref[...] = _conv5_fn(x_ref[...], *[cw_ref[k:k + 1, :] for k in range(5)], cb_ref[...])

    return pl.pallas_call(
        body, grid=(SSD_XBC // SSD_CT, B_LOC),
        in_specs=[pl.BlockSpec((SEQ, SSD_CT), lambda j, b: (b, j)),
                  pl.BlockSpec((8, SSD_CT), lambda j, b: (0, j)),
                  pl.BlockSpec((1, SSD_CT), lambda j, b: (0, j))],
        out_specs=pl.BlockSpec((SEQ, SSD_CT), lambda j, b: (b, j)),
        out_shape=S_((T, SSD_XBC), F32), compiler_params=_cp(), name="ssd_pre_fwd")(proj, cw, cb)


def _ssd_pre_bwd(proj, cw, cb, dxc, hook=None):
    def body(x_ref, cw_ref, cb_ref, d_ref, dx_ref, dcw_ref):
        _, vjp = jax.vjp(_conv5_fn, x_ref[...], *[cw_ref[k:k + 1, :] for k in range(5)], cb_ref[...])
        dx, g0, g1, g2, g3, g4, gb = vjp(d_ref[...])
        dx_ref[...] = dx.astype(BF16)
        new = jnp.concatenate([g0, g1, g2, g3, g4, gb, jnp.zeros((2, SSD_CT), F32)], axis=0)

        @pl.when(pl.program_id(1) == 0)
        def _():
            dcw_ref[...] = new

        @pl.when(pl.program_id(1) > 0)
        def _():
            dcw_ref[...] += new

    return _hooked_call(
        body, grid=(SSD_XBC // SSD_CT, B_LOC),
        in_specs=[pl.BlockSpec((SEQ, SSD_CT), lambda j, b: (b, j)),
                  pl.BlockSpec((8, SSD_CT), lambda j, b: (0, j)),
                  pl.BlockSpec((1, SSD_CT), lambda j, b: (0, j)),
                  pl.BlockSpec((SEQ, SSD_CT), lambda j, b: (b, j))],
        out_specs=[pl.BlockSpec((SEQ, SSD_CT), lambda j, b: (b, j)),
                   pl.BlockSpec((8, SSD_CT), lambda j, b: (0, j))],
        out_shape=[S_((T, SSD_XBC), BF16), S_((8, SSD_XBC), F32)],
        scratch_shapes=[], args=(proj, cw, cb, dxc), hook=hook, name="ssd_pre_bwd")


GROUP_W = 256
ONE_BUFFER = dict(pipeline_mode=pl.Buffered(1))
HEADS_PER_GROUP = 4
FWD_SCAN_UNROLL = 4
BWD_SCAN_UNROLL = 2


def _ssd_dt_fn(dt_raw, bias, alog):
    dt = _softplus(dt_raw + bias)
    return dt, dt * (-jnp.exp(alog))


def _ssd_chunk_fn(direction, group, xc0, xc1, bc, cc, dt, da, prev0, prev1):
    q = CHUNK
    ti = lax.broadcasted_iota(jnp.int32, (q, q), 0)
    si = lax.broadcasted_iota(jnp.int32, (q, q), 1)
    keep = (ti >= si) if direction == 0 else (ti <= si)
    mat = keep.astype(F32)
    acs = jnp.dot(mat, da, precision=HI, preferred_element_type=F32)
    acs_t = lax.dot_general(da, mat, (((0,), (1,)), ((), ())), precision=HI, preferred_element_type=F32)
    tot = jnp.sum(da, axis=0, keepdims=True)
    lane = lax.broadcasted_iota(jnp.int32, (1, 128), 1)
    sub = lax.broadcasted_iota(jnp.int32, (128, 1), 0)
    first_head = lane < HEAD
    cb = _dot_nt(cc, bc)
    a_cols, tots, dt_cols, lows, douts = [], [], [], [], []
    for h in range(HEADS_PER_GROUP):
        ln = 8 * direction + 4 * group + h
        oh_l = (lane == ln).astype(F32)
        oh_s = (sub == ln).astype(F32)
        a_col = jnp.sum(acs * oh_l, axis=1, keepdims=True)
        a_row = jnp.sum(acs_t * oh_s, axis=0, keepdims=True)
        tot_h = jnp.sum(tot * oh_l, axis=1, keepdims=True)
        a_cols.append(a_col)
        tots.append(tot_h)
        dt_cols.append(jnp.sum(dt * oh_l, axis=1, keepdims=True))
        lows.append(cb * jnp.exp(jnp.where(keep, a_col - a_row, NEG_INF)))
        douts.append(bc * jnp.exp(tot_h - a_col))
    out = []
    for pair, (xc, prev) in enumerate(((xc0, prev0), (xc1, prev1))):
        h0, h1 = 2 * pair, 2 * pair + 1
        xdt = xc * jnp.where(first_head, dt_cols[h0], dt_cols[h1])
        y = jnp.where(first_head, jnp.exp(a_cols[h0]), jnp.exp(a_cols[h1])) * _dot(cc, prev)
        y = y + jnp.where(first_head, _dot(lows[h0], xdt), _dot(lows[h1], xdt))
        st = jnp.where(first_head, _dot_tn(douts[h0], xdt), _dot_tn(douts[h1], xdt))
        out.append((y, prev * jnp.where(first_head, jnp.exp(tots[h0]), jnp.exp(tots[h1])) + st))
    return out[0][0], out[1][0], out[0][1], out[1][1]


def _ssd_post_fn(y, xc, z, d_exp, gain):
    y = (y + d_exp * xc) * jax.nn.silu(z)
    return y * lax.rsqrt(jnp.mean(y * y, axis=-1, keepdims=True) + NORM_EPS) * gain


def _chunk_rows(c):
    return pl.ds(pl.multiple_of(c * CHUNK, CHUNK), CHUNK)


def _scan_loop(step, init, unroll):
    def body(i, carry):
        for k in range(unroll):
            carry = step(i * unroll + k, carry)
        return carry

    return lax.fori_loop(0, N_CHUNK // unroll, body, init)


def _ssd_scan_specs(**mode):
    return [pl.BlockSpec((SEQ, GROUP_W), lambda g, b: (b, g), **mode),
            pl.BlockSpec((SEQ, 128), lambda g, b: (b, C_B // 128 + g), **mode),
            pl.BlockSpec((SEQ, 128), lambda g, b: (b, C_C // 128 + g), **mode),
            pl.BlockSpec((SEQ, GROUP_W), lambda g, b: (b, C_Z // GROUP_W + g), **mode),
            pl.BlockSpec((SEQ, 128), lambda g, b: (b, C_DT // 128), **mode),
            pl.BlockSpec((1, 128), lambda g, b: (0, 0)),
            pl.BlockSpec((1, 128), lambda g, b: (0, 0)),
            pl.BlockSpec((1, GROUP_W), lambda g, b: (0, g)),
            pl.BlockSpec((1, GROUP_W), lambda g, b: (0, g))]


def _ssd_state_spec(**mode):
    return pl.BlockSpec((None, None, 2 * N_CHUNK, 128, GROUP_W), lambda g, b: (g, b, 0, 0, 0), **mode)


def _ssd_scan_fwd(xc, proj, dtb, alog, d_exp, gain, hook=None):
    def body(x_ref, b_ref, c_ref, z_ref, dt_ref, dtb_ref, al_ref, de_ref, g_ref, o_ref, y_s, st_ref, dt_s, da_s):
        group = pl.program_id(0)
        dt, da = _ssd_dt_fn(dt_ref[...], dtb_ref[...], al_ref[...])
        dt_s[...] = dt
        da_s[...] = da
        for direction in (0, 1):
            def step(i, prev, direction=direction):
                c = i if direction == 0 else N_CHUNK - 1 - i
                rows = _chunk_rows(c)
                st_ref[direction * N_CHUNK + c, :, 0:128] = prev[0]
                st_ref[direction * N_CHUNK + c, :, 128:256] = prev[1]
                y0, y1, nxt0, nxt1 = _ssd_chunk_fn(direction, group, x_ref[rows, 0:128], x_ref[rows, 128:256], b_ref[rows, :], c_ref[rows, :],
                                                   dt_s[rows, :], da_s[rows, :], prev[0], prev[1])
                if direction == 0:
                    y_s[rows, 0:128] = y0
                    y_s[rows, 128:256] = y1
                else:
                    y_s[rows, 0:128] += y0
                    y_s[rows, 128:256] += y1
                return nxt0, nxt1

            _scan_loop(step, (jnp.zeros((128, 128), F32), jnp.zeros((128, 128), F32)), FWD_SCAN_UNROLL)

        def post(c, carry):
            rows = _chunk_rows(c)
            o_ref[rows, :] = _ssd_post_fn(y_s[rows, :], x_ref[rows, :], z_ref[rows, :], de_ref[...], g_ref[...]).astype(BF16)
            return carry

        lax.fori_loop(0, N_CHUNK, post, 0)

    return _hooked_call(
        body, grid=(2, B_LOC), in_specs=_ssd_scan_specs(),
        out_specs=[pl.BlockSpec((SEQ, GROUP_W), lambda g, b: (b, g)), pl.BlockSpec((SEQ, GROUP_W), lambda g, b: (b, g)), _ssd_state_spec()],
        out_shape=[S_((T, SSD_INNER), BF16), S_((T, SSD_INNER), F32), S_((2, B_LOC, 2 * N_CHUNK, 128, GROUP_W), F32)],
        scratch_shapes=[pltpu.VMEM((SEQ, 128), F32), pltpu.VMEM((SEQ, 128), F32)],
        args=(xc, xc, xc, proj, proj, dtb, alog, d_exp, gain), hook=hook, name="ssd_scan_fwd")


def _ssd_scan_bwd(xc, proj, dtb, alog, d_exp, gain, dy, ysum, states, hook=None):
    def body(x_ref, b_ref, c_ref, z_ref, dt_ref, dtb_ref, al_ref, de_ref, g_ref, dy_ref, ys_ref, st_s,
             dx_ref, db_ref, dc_ref, dz_ref, ddt_ref, ddtb_ref, dal_ref, dde_ref, dg_ref,
             dt_s, da_s, y_s, ddt_s, dda_s):
        group = pl.program_id(0)
        first = pl.program_id(1) == 0
        (dt, da), dt_vjp = jax.vjp(_ssd_dt_fn, dt_ref[...], dtb_ref[...], al_ref[...])
        dt_s[...] = dt
        da_s[...] = da

        def post(c, carry):
            rows = _chunk_rows(c)
            _, post_vjp = jax.vjp(_ssd_post_fn, ys_ref[rows, :], x_ref[rows, :], z_ref[rows, :], de_ref[...], g_ref[...])
            d_y, d_x_skip, d_z, g_de, g_g = post_vjp(dy_ref[rows, :])
            dz_ref[rows, :] = d_z.astype(BF16)
            dx_ref[rows, :] = d_x_skip
            y_s[rows, :] = d_y
            return carry[0] + g_de, carry[1] + g_g

        d_de, d_g = lax.fori_loop(0, N_CHUNK, post, (jnp.zeros((1, GROUP_W), F32), jnp.zeros((1, GROUP_W), F32)))
        db_ref[...] = jnp.zeros((SEQ, 128), F32)
        dc_ref[...] = jnp.zeros((SEQ, 128), F32)
        ddt_s[...] = jnp.zeros((SEQ, 128), F32)
        dda_s[...] = jnp.zeros((SEQ, 128), F32)
        for direction in (0, 1):
            def bstep(i, dnxt, direction=direction):
                c = N_CHUNK - 1 - i if direction == 0 else i
                rows = _chunk_rows(c)
                fn = functools.partial(_ssd_chunk_fn, direction, group)
                _, vjp = jax.vjp(fn, x_ref[rows, 0:128], x_ref[rows, 128:256], b_ref[rows, :], c_ref[rows, :], dt_s[rows, :], da_s[rows, :],
                                 st_s[direction * N_CHUNK + c, :, 0:128], st_s[direction * N_CHUNK + c, :, 128:256])
                g_x0, g_x1, g_b, g_c, g_dt, g_da, g_prev0, g_prev1 = vjp((y_s[rows, 0:128], y_s[rows, 128:256], dnxt[0], dnxt[1]))
                dx_ref[rows, 0:128] += g_x0
                dx_ref[rows, 128:256] += g_x1
                db_ref[rows, :] += g_b
                dc_ref[rows, :] += g_c
                ddt_s[rows, :] += g_dt
                dda_s[rows, :] += g_da
                return g_prev0, g_prev1

            _scan_loop(bstep, (jnp.zeros((128, 128), F32), jnp.zeros((128, 128), F32)), BWD_SCAN_UNROLL)
        g_raw, g_bias, g_alog = dt_vjp((ddt_s[...], dda_s[...]))
        ddt_ref[...] = g_raw
        pad7 = jnp.zeros((7, 128), F32)
        new_b = jnp.concatenate([g_bias, pad7], axis=0)
        new_a = jnp.concatenate([g_alog, pad7], axis=0)

        @pl.when(first)
        def _():
            ddtb_ref[...] = new_b
            dal_ref[...] = new_a
            dde_ref[...] = d_de
            dg_ref[...] = d_g

        @pl.when(jnp.logical_not(first))
        def _():
            ddtb_ref[...] += new_b
            dal_ref[...] += new_a
            dde_ref[...] += d_de
            dg_ref[...] += d_g

    return _hooked_call(
        body, grid=(2, B_LOC),
        in_specs=_ssd_scan_specs(**ONE_BUFFER) + [pl.BlockSpec((SEQ, GROUP_W), lambda g, b: (b, g), **ONE_BUFFER),
                                                  pl.BlockSpec((SEQ, GROUP_W), lambda g, b: (b, g), **ONE_BUFFER),
                                                  _ssd_state_spec(**ONE_BUFFER)],
        out_specs=[pl.BlockSpec((SEQ, GROUP_W), lambda g, b: (b, g)),
                   pl.BlockSpec((SEQ, 128), lambda g, b: (b, g)),
                   pl.BlockSpec((SEQ, 128), lambda g, b: (b, g)),
                   pl.BlockSpec((SEQ, GROUP_W), lambda g, b: (b, g)),
                   pl.BlockSpec((None, SEQ, 128), lambda g, b: (g, b, 0)),
                   pl.BlockSpec((None, 8, 128), lambda g, b: (g, 0, 0)),
                   pl.BlockSpec((None, 8, 128), lambda g, b: (g, 0, 0)),
                   pl.BlockSpec((1, GROUP_W), lambda g, b: (0, g)),
                   pl.BlockSpec((1, GROUP_W), lambda g, b: (0, g))],
        out_shape=[S_((T, SSD_INNER), F32), S_((T, 256), F32), S_((T, 256), F32), S_((T, SSD_INNER), BF16),
                   S_((2, T, 128), F32), S_((2, 8, 128), F32), S_((2, 8, 128), F32),
                   S_((1, SSD_INNER), F32), S_((1, SSD_INNER), F32)],
        scratch_shapes=[pltpu.VMEM((SEQ, 128), F32), pltpu.VMEM((SEQ, 128), F32), pltpu.VMEM((SEQ, GROUP_W), F32),
                        pltpu.VMEM((SEQ, 128), F32), pltpu.VMEM((SEQ, 128), F32)],
        args=(xc, xc, xc, proj, proj, dtb, alog, d_exp, gain, dy, ysum, states), hook=hook, name="ssd_scan_bwd")


GMLP_W = 256


def _gmlp_chunk_fn(gu, gv, v_gain, w0, w1, w2, w3, b_exp):
    u = jax.nn.gelu(gu)
    v = jax.nn.gelu(gv)
    v = v * lax.rsqrt(jnp.mean(v * v, axis=-1, keepdims=True) + NORM_EPS) * v_gain
    col = lax.broadcasted_iota(jnp.int32, (1, GMLP_W), 1) // HEAD
    mixed = b_exp
    for g, w in enumerate((w0, w1, w2, w3)):
        mixed = mixed + (col == g).astype(F32) * _dot(w, v)
    return u * mixed


def _gmlp_specs():
    return [pl.BlockSpec((SEQ, GMLP_W), lambda b: (b, C_GU // GMLP_W)),
            pl.BlockSpec((SEQ, GMLP_W), lambda b: (b, C_GV // GMLP_W)),
            pl.BlockSpec((1, GMLP_W), lambda b: (0, 0)),
            pl.BlockSpec((4, CHUNK, CHUNK), lambda b: (0, 0, 0)),
            pl.BlockSpec((CHUNK, GMLP_W), lambda b: (0, 0))]


def _gmlp_fwd(proj, v_gain, w_s, b_exp):
    def body(u_ref, v_ref, g_ref, w_ref, b_ref, o_ref):
        def step(c, carry):
            rows = _chunk_rows(c)
            o_ref[rows, :] = _gmlp_chunk_fn(u_ref[rows, :], v_ref[rows, :], g_ref[...], w_ref[0], w_ref[1], w_ref[2], w_ref[3],
                                            b_ref[...]).astype(BF16)
            return carry

        lax.fori_loop(0, N_CHUNK, step, 0)

    return pl.pallas_call(body, grid=(B_LOC,), in_specs=_gmlp_specs(),
                          out_specs=pl.BlockSpec((SEQ, GMLP_W), lambda b: (b, 0)),
                          out_shape=S_((T, GMLP_W), BF16), name="gmlp_fwd")(proj, proj, v_gain, w_s, b_exp)


def _gmlp_bwd(proj, v_gain, w_s, b_exp, dy):
    def body(u_ref, v_ref, g_ref, w_ref, b_ref, dy_ref, du_ref, dv_ref, dg_ref, dw_ref, db_ref):
        @pl.when(pl.program_id(0) == 0)
        def _():
            dg_ref[...] = jnp.zeros_like(dg_ref)
            dw_ref[...] = jnp.zeros_like(dw_ref)
            db_ref[...] = jnp.zeros_like(db_ref)

        def step(c, carry):
            rows = _chunk_rows(c)
            _, vjp = jax.vjp(_gmlp_chunk_fn, u_ref[rows, :], v_ref[rows, :], g_ref[...], w_ref[0], w_ref[1], w_ref[2], w_ref[3], b_ref[...])
            g_u, g_v, g_g, g_w0, g_w1, g_w2, g_w3, g_b = vjp(dy_ref[rows, :])
            du_ref[rows, :] = g_u.astype(BF16)
            dv_ref[rows, :] = g_v.astype(BF16)
            dg_ref[...] += g_g
            db_ref[...] += g_b
            for g, gw in enumerate((g_w0, g_w1, g_w2, g_w3)):
                dw_ref[g] += gw
            return carry

        lax.fori_loop(0, N_CHUNK, step, 0)

    blk = pl.BlockSpec((SEQ, GMLP_W), lambda b: (b, 0))
    return pl.pallas_call(
        body, grid=(B_LOC,),
        in_specs=_gmlp_specs() + [pl.BlockSpec((SEQ, GMLP_W), lambda b: (b, SSD_INNER // GMLP_W))],
        out_specs=[blk, blk, pl.BlockSpec((1, GMLP_W), lambda b: (0, 0)),
                   pl.BlockSpec((4, CHUNK, CHUNK), lambda b: (0, 0, 0)), pl.BlockSpec((CHUNK, GMLP_W), lambda b: (0, 0))],
        out_shape=[S_((T, GMLP_W), BF16), S_((T, GMLP_W), BF16), S_((1, GMLP_W), F32),
                   S_((4, CHUNK, CHUNK), F32), S_((CHUNK, GMLP_W), F32)],
        name="gmlp_bwd")(proj, proj, v_gain, w_s, b_exp, dy)


PAIR_W = 128
QB = 128
KW = QB + 2 * ATTN_HALF
N_QB = SEQ // QB
FWD_BLOCK_UNROLL = 4
BWD_BLOCK_UNROLL = 4
PAD_ROWS = SEQ + 2 * ATTN_HALF


def _qk_norm_fn(x, gain):
    ms = jnp.dot(x * x, _head_sum_matrix(PAIR_W), precision=SUM_PRECISION, preferred_element_type=F32) * (1.0 / HEAD)
    return x * lax.rsqrt(ms + NORM_EPS) * gain


def _deinterleave(dst_ref, src_ref, dil, offset):
    length = SEQ // dil
    if dil == 1:
        dst_ref[pl.ds(offset, SEQ), :] = src_ref[...]
        return
    for r in range(dil):
        dst_ref[pl.ds(offset + r * length, length), :] = src_ref[pl.ds(r, length, stride=dil), :]


def _interleave(dst_ref, src_ref, dil, offset):
    length = SEQ // dil
    if dil == 1:
        dst_ref[...] = src_ref[pl.ds(offset, SEQ), :]
        return
    for r in range(dil):
        dst_ref[pl.ds(r, length, stride=dil), :] = src_ref[pl.ds(offset + r * length, length), :]


def _edge_mask(blk, dil):
    length = SEQ // dil
    qi = blk * QB + lax.broadcasted_iota(jnp.int32, (QB, KW), 0)
    kj = blk * QB - ATTN_HALF + lax.broadcasted_iota(jnp.int32, (QB, KW), 1)
    return (kj >= 0) & (kj < SEQ) & ((qi // length) == (kj // length))


def _lane_is_head(hh):
    return (lax.broadcasted_iota(jnp.int32, (1, PAIR_W), 1) // HEAD) == hh


def _dilate_qkv(dil, qn_s, kn_s, v_ref, qd_s, kd_s, vd_s):
    _deinterleave(qd_s, qn_s, dil, 0)
    _deinterleave(kd_s, kn_s, dil, ATTN_HALF)
    _deinterleave(vd_s, v_ref, dil, ATTN_HALF)


def _attn_branch_fwd(br, dil, qn_s, kn_s, v_ref, bias_ref, qd_s, kd_s, vd_s, od_s, ld_s):
    _dilate_qkv(dil, qn_s, kn_s, v_ref, qd_s, kd_s, vd_s)

    def step(blk, carry):
        rows = pl.ds(pl.multiple_of(blk * QB, QB), QB)
        win = pl.ds(pl.multiple_of(blk * QB, QB), KW)
        qb, kw, vw = qd_s[rows, :], kd_s[win, :], vd_s[win, :]
        edge = _edge_mask(blk, dil)
        out, lse = 0.0, 0.0
        for hh in range(2):
            is_h = _lane_is_head(hh)
            s = _dot_nt(jnp.where(is_h, qb, 0.0), kw) * (HEAD ** -0.5) + bias_ref[br, hh]
            s = jnp.where(edge, s, NEG_INF)
            m = jnp.max(s, axis=-1, keepdims=True)
            l_h = m + jnp.log(jnp.sum(jnp.exp(s - m), axis=-1, keepdims=True))
            out = out + jnp.where(is_h, _dot(jnp.exp(s - l_h), vw), 0.0)
            lse = lse + jnp.where(is_h, l_h, 0.0)
        od_s[rows, :] = out
        ld_s[rows, :] = lse
        return carry

    _block_loop(step, FWD_BLOCK_UNROLL)


def _block_loop(step, unroll):
    def body(i, carry):
        for k in range(unroll):
            carry = step(i * unroll + k, carry)
        return carry

    lax.fori_loop(0, N_QB // unroll, body, 0)


def _attn_specs():
    col = lambda c0: (lambda p, b: (b, c0 // PAIR_W + p))
    return [pl.BlockSpec((SEQ, PAIR_W), col(C_Q)), pl.BlockSpec((SEQ, PAIR_W), col(C_K)), pl.BlockSpec((SEQ, PAIR_W), col(C_V)),
            pl.BlockSpec((1, PAIR_W), lambda p, b: (0, 0)), pl.BlockSpec((1, PAIR_W), lambda p, b: (0, 0)),
            pl.BlockSpec((3, 2, QB, KW), lambda p, b: (0, p, 0, 0))]


def _attn_scratch():
    seq = pltpu.VMEM((SEQ, PAIR_W), F32)
    pad = pltpu.VMEM((PAD_ROWS, PAIR_W), F32)
    return [seq, seq, seq, pad, pad, seq, seq]


def _zero_pads(*refs):
    for ref in refs:
        ref[pl.ds(0, ATTN_HALF), :] = jnp.zeros((ATTN_HALF, PAIR_W), F32)
        ref[pl.ds(ATTN_HALF + SEQ, ATTN_HALF), :] = jnp.zeros((ATTN_HALF, PAIR_W), F32)


ROW_STEP = 256


def _row_steps(fn, init=0):
    return lax.fori_loop(0, SEQ // ROW_STEP, lambda i, c: fn(pl.ds(pl.multiple_of(i * ROW_STEP, ROW_STEP), ROW_STEP), c), init)


def _interleave_add(acc_ref, src_ref, dil, offset):
    length = SEQ // dil
    if dil == 1:
        acc_ref[...] += src_ref[pl.ds(offset, SEQ), :]
        return
    for r in range(dil):
        acc_ref[pl.ds(r, length, stride=dil), :] += src_ref[pl.ds(offset + r * length, length), :]


def _attn_norm_qk(q_ref, k_ref, qg_ref, kg_ref, qn_s, kn_s):
    def norm(rows, carry):
        qn_s[rows, :] = _qk_norm_fn(q_ref[rows, :], qg_ref[...])
        kn_s[rows, :] = _qk_norm_fn(k_ref[rows, :], kg_ref[...])
        return carry

    _row_steps(norm)


def _attn_forward_all(q_ref, k_ref, v_ref, qg_ref, kg_ref, bias_ref, qn_s, kn_s, qd_s, kd_s, vd_s, od_s, ld_s, on_s, ln_s):
    _attn_norm_qk(q_ref, k_ref, qg_ref, kg_ref, qn_s, kn_s)
    _zero_pads(kd_s, vd_s)
    for br, dil in enumerate(ATTN_DILS):
        _attn_branch_fwd(br, dil, qn_s, kn_s, v_ref, bias_ref, qd_s, kd_s, vd_s, od_s, ld_s)
        _interleave(on_s.at[br], od_s, dil, 0)
        _interleave(ln_s.at[br], ld_s, dil, 0)


def _merge_weights(ln_s, rows):
    l0, l1, l2 = ln_s[0, rows, :], ln_s[1, rows, :], ln_s[2, rows, :]
    m = jnp.maximum(jnp.maximum(l0, l1), l2)
    e = [jnp.exp(l0 - m), jnp.exp(l1 - m), jnp.exp(l2 - m)]
    den = e[0] + e[1] + e[2]
    return [e[0] / den, e[1] / den, e[2] / den]


def _attn_fwd(proj, q_gain, k_gain, bias, hook=None):
    def body(q_ref, k_ref, v_ref, qg_ref, kg_ref, bias_ref, o_ref, on_s, ln_s, qn_s, kn_s, qd_s, kd_s, vd_s, od_s, ld_s):
        _attn_forward_all(q_ref, k_ref, v_ref, qg_ref, kg_ref, bias_ref, qn_s, kn_s, qd_s, kd_s, vd_s, od_s, ld_s, on_s, ln_s)

        def merge(rows, carry):
            w = _merge_weights(ln_s, rows)
            o_ref[rows, :] = (w[0] * on_s[0, rows, :] + w[1] * on_s[1, rows, :] + w[2] * on_s[2, rows, :]).astype(BF16)
            return carry

        _row_steps(merge)

    kept = pl.BlockSpec((3, SEQ, PAIR_W), lambda p, b: (0, b, p))
    return _hooked_call(body, grid=(2, B_LOC), in_specs=_attn_specs(),
                        out_specs=[pl.BlockSpec((SEQ, PAIR_W), lambda p, b: (b, p)), kept, kept],
                        out_shape=[S_((T, 2 * PAIR_W), BF16), S_((3, T, 2 * PAIR_W), F32), S_((3, T, 2 * PAIR_W), F32)],
                        scratch_shapes=_attn_scratch(), args=(proj, proj, proj, q_gain, k_gain, bias), hook=hook, name="attn_fwd")


def _attn_bwd(proj, q_gain, k_gain, bias, dy, kept_o, kept_l, hook=None):
    def body(q_ref, k_ref, v_ref, qg_ref, kg_ref, bias_ref, dy_ref, on_ref, ln_ref,
             dq_ref, dk_ref, dv_ref, dqg_ref, dkg_ref, dbias_ref,
             qn_s, kn_s, qd_s, kd_s, vd_s, od_s, ld_s, don_s, dln_s, dod_s, dld_s, dqd_s, dkd_s, dvd_s, dqn_s, dkn_s, dvn_s):
        first = pl.program_id(1) == 0
        _attn_norm_qk(q_ref, k_ref, qg_ref, kg_ref, qn_s, kn_s)
        _zero_pads(kd_s, vd_s)

        def clear_acc(rows, carry):
            dqn_s[rows, :] = jnp.zeros((ROW_STEP, PAIR_W), F32)
            dkn_s[rows, :] = jnp.zeros((ROW_STEP, PAIR_W), F32)
            dvn_s[rows, :] = jnp.zeros((ROW_STEP, PAIR_W), F32)
            return carry

        _row_steps(clear_acc)

        @pl.when(first)
        def _():
            dbias_ref[...] = jnp.zeros_like(dbias_ref)

        def merge_bwd(rows, carry):
            w = _merge_weights(ln_ref, rows)
            dy = dy_ref[rows, :]
            same_head = _head_sum_matrix(PAIR_W)
            dws = [jnp.dot(dy * on_ref[j, rows, :], same_head, precision=SUM_PRECISION, preferred_element_type=F32) for j in range(3)]
            dbar = w[0] * dws[0] + w[1] * dws[1] + w[2] * dws[2]
            for j in range(3):
                don_s[j, rows, :] = w[j] * dy
                dln_s[j, rows, :] = w[j] * (dws[j] - dbar)
            return carry

        _row_steps(merge_bwd)
        for br, dil in enumerate(ATTN_DILS):
            _dilate_qkv(dil, qn_s, kn_s, v_ref, qd_s, kd_s, vd_s)
            _deinterleave(od_s, on_ref.at[br], dil, 0)
            _deinterleave(ld_s, ln_ref.at[br], dil, 0)
            _deinterleave(dod_s, don_s.at[br], dil, 0)
            _deinterleave(dld_s, dln_s.at[br], dil, 0)

            def clear(rows, carry):
                dkd_s[rows, :] = jnp.zeros((ROW_STEP, PAIR_W), F32)
                dvd_s[rows, :] = jnp.zeros((ROW_STEP, PAIR_W), F32)
                return carry

            _row_steps(clear)
            tail = pl.ds(SEQ, 2 * ATTN_HALF)
            dkd_s[tail, :] = jnp.zeros((2 * ATTN_HALF, PAIR_W), F32)
            dvd_s[tail, :] = jnp.zeros((2 * ATTN_HALF, PAIR_W), F32)

            def step(blk, carry, br=br, dil=dil):
                rows = pl.ds(pl.multiple_of(blk * QB, QB), QB)
                win = pl.ds(pl.multiple_of(blk * QB, QB), KW)
                qb, kw, vw = qd_s[rows, :], kd_s[win, :], vd_s[win, :]
                do_b, dl_b, o_b, l_b = dod_s[rows, :], dld_s[rows, :], od_s[rows, :], ld_s[rows, :]
                edge = _edge_mask(blk, dil)
                dq, dk, dv = 0.0, 0.0, 0.0
                for hh in range(2):
                    is_h = _lane_is_head(hh)
                    pick = (lax.broadcasted_iota(jnp.int32, (1, PAIR_W), 1) == hh * HEAD).astype(F32)
                    q_h = jnp.where(is_h, qb, 0.0)
                    do_h = jnp.where(is_h, do_b, 0.0)
                    s = _dot_nt(q_h, kw) * (HEAD ** -0.5) + bias_ref[br, hh]
                    s = jnp.where(edge, s, NEG_INF)
                    p = jnp.exp(s - jnp.sum(l_b * pick, axis=-1, keepdims=True))
                    dp = _dot_nt(do_h, vw)
                    delta = jnp.sum(do_h * o_b, axis=-1, keepdims=True)
                    ds = p * (dp - delta + jnp.sum(dl_b * pick, axis=-1, keepdims=True))
                    dbias_ref[br, hh] += ds
                    dq = dq + jnp.where(is_h, _dot(ds, kw), 0.0) * (HEAD ** -0.5)
                    dk = dk + _dot_tn(ds, q_h) * (HEAD ** -0.5)
                    dv = dv + _dot_tn(p, do_h)
                dqd_s[rows, :] = dq
                dkd_s[win, :] += dk
                dvd_s[win, :] += dv
                return carry

            _block_loop(step, BWD_BLOCK_UNROLL)
            _interleave_add(dqn_s, dqd_s, dil, 0)
            _interleave_add(dkn_s, dkd_s, dil, ATTN_HALF)
            _interleave_add(dvn_s, dvd_s, dil, ATTN_HALF)

        def norm_bwd(rows, carry):
            _, q_vjp = jax.vjp(_qk_norm_fn, q_ref[rows, :], qg_ref[...])
            _, k_vjp = jax.vjp(_qk_norm_fn, k_ref[rows, :], kg_ref[...])
            g_q, g_qg = q_vjp(dqn_s[rows, :])
            g_k, g_kg = k_vjp(dkn_s[rows, :])
            dq_ref[rows, :] = g_q.astype(BF16)
            dk_ref[rows, :] = g_k.astype(BF16)
            dv_ref[rows, :] = dvn_s[rows, :].astype(BF16)
            return carry[0] + g_qg, carry[1] + g_kg

        g_qg, g_kg = _row_steps(norm_bwd, (jnp.zeros((1, PAIR_W), F32), jnp.zeros((1, PAIR_W), F32)))
        pad7 = jnp.zeros((7, PAIR_W), F32)
        new_q = jnp.concatenate([g_qg, pad7], axis=0)
        new_k = jnp.concatenate([g_kg, pad7], axis=0)

        @pl.when(first)
        def _():
            dqg_ref[...] = new_q
            dkg_ref[...] = new_k

        @pl.when(jnp.logical_not(first))
        def _():
            dqg_ref[...] += new_q
            dkg_ref[...] += new_k

    seq = pltpu.VMEM((SEQ, PAIR_W), F32)
    seq3 = pltpu.VMEM((3, SEQ, PAIR_W), F32)
    pad = pltpu.VMEM((PAD_ROWS, PAIR_W), F32)
    kept = pl.BlockSpec((3, SEQ, PAIR_W), lambda p, b: (0, b, p))
    out_blk = pl.BlockSpec((SEQ, PAIR_W), lambda p, b: (b, p))
    gain_blk = pl.BlockSpec((None, 8, PAIR_W), lambda p, b: (p, 0, 0))
    return _hooked_call(
        body, grid=(2, B_LOC),
        in_specs=_attn_specs() + [pl.BlockSpec((SEQ, PAIR_W), lambda p, b: (b, (SSD_INNER + GMLP_W) // PAIR_W + p)), kept, kept],
        out_specs=[out_blk, out_blk, out_blk, gain_blk, gain_blk, pl.BlockSpec((3, 2, QB, KW), lambda p, b: (0, p, 0, 0))],
        out_shape=[S_((T, 2 * PAIR_W), BF16)] * 3 + [S_((2, 8, PAIR_W), F32)] * 2 + [S_((3, 4, QB, KW), F32)],
        scratch_shapes=_attn_scratch() + [seq3, seq3, seq, seq, seq, pad, pad, seq, seq, seq],
        args=(proj, proj, proj, q_gain, k_gain, bias, dy, kept_o, kept_l), hook=hook, name="attn_bwd")


def _rel_bucket(rel):
    nb = 16
    max_exact = nb // 2
    n = jnp.abs(rel)
    large = max_exact + (jnp.log(jnp.maximum(n, 1).astype(F32) / max_exact) / math.log(1024 / max_exact) * (nb - max_exact)).astype(jnp.int32)
    large = jnp.minimum(large, nb - 1)
    return jnp.where(rel > 0, nb, 0) + jnp.where(n < max_exact, n, large)


def _attn_bias(rel_table):
    rel = jnp.arange(KW)[None, :] - ATTN_HALF - jnp.arange(QB)[:, None]
    inside = (jnp.abs(rel) <= ATTN_HALF)
    out = []
    for dil in ATTN_DILS:
        one_hot = (_rel_bucket(rel * dil)[None] == jnp.arange(32)[:, None, None]).astype(F32)
        b = jnp.einsum("kh,kts->hts", rel_table, one_hot, precision=HI)
        out.append(jnp.where(inside[None], b, NEG_INF))
    return jnp.stack(out).astype(F32)


def _place():
    return lax.axis_index("x"), lax.axis_index("y"), lax.axis_index("c")


def _allgather8(buf, name):
    rows = buf.shape[0]
    flips = [(fx, fy, fc) for fx in (0, 1) for fy in (0, 1) for fc in (0, 1)][1:]

    def body(in_ref, out_ref, send_sems, recv_sems):
        x, y, c = _place()
        me = 4 * x + 2 * y + c
        peers = [(1 - x if fx else x, 1 - y if fy else y, 1 - c if fc else c) for fx, fy, fc in flips]

        def copy(k, slot, peer):
            return pltpu.make_async_remote_copy(src_ref=in_ref, dst_ref=out_ref.at[slot], send_sem=send_sems.at[k],
                                                recv_sem=recv_sems.at[k], device_id=peer, device_id_type=MESH)

        sends = [copy(k, me, peer) for k, peer in enumerate(peers)]
        for cp in sends:
            cp.start()
        for k, (px, py, pc) in enumerate(peers):
            copy(k, 4 * px + 2 * py + pc, (px, py, pc)).wait_recv()
        for cp in sends:
            cp.wait_send()

    slots = pl.pallas_call(body, in_specs=[ANY], out_specs=ANY, out_shape=S_((N_DEV, rows, 128), F32),
                           scratch_shapes=[pltpu.SemaphoreType.DMA((7,)), pltpu.SemaphoreType.DMA((7,))], name=name)(buf)
    x, y, c = _place()
    return lax.dynamic_update_index_in_dim(slots, buf, 4 * x + 2 * y + c, axis=0)


N_BIG = 4


def _other_chips(x, y):
    return [(1 - x, y), (x, 1 - y), (1 - x, 1 - y)]


def _hooked_call(body, *, grid, in_specs, out_specs, out_shape, scratch_shapes, args, hook, name):
    if hook is None:
        res = pl.pallas_call(body, grid=grid, in_specs=in_specs, out_specs=out_specs, out_shape=out_shape,
                             scratch_shapes=scratch_shapes, compiler_params=_cp(), name=name)(*args)
        return res, None
    counts = (len(in_specs), len(hook["arrays"]), len(out_specs), len(hook["out_shape"]), len(scratch_shapes), len(hook["sems"]))

    def wrapped(*refs):
        groups, pos = [], 0
        for n in counts:
            groups.append(refs[pos:pos + n])
            pos += n
        ins, h_ins, outs, h_outs, scr, sems = groups
        idx = [pl.program_id(a) for a in range(len(grid))]
        first = functools.reduce(jnp.logical_and, [i == 0 for i in idx])
        last = functools.reduce(jnp.logical_and, [i == g - 1 for i, g in zip(idx, grid)])

        @pl.when(first)
        def _():
            hook["start"](h_ins, h_outs, sems)

        body(*ins, *outs, *scr)

        @pl.when(last)
        def _():
            hook["finish"](h_ins, h_outs, sems)

    res = pl.pallas_call(wrapped, grid=grid, in_specs=list(in_specs) + [ANY] * counts[1], out_specs=list(out_specs) + [ANY] * counts[3],
                         out_shape=list(out_shape) + list(hook["out_shape"]), scratch_shapes=list(scratch_shapes) + list(hook["sems"]),
                         compiler_params=_cp(), name=name + "_" + hook["name"])(*args, *hook["arrays"])
    return res[:counts[2]], res[counts[2]:]


def _run_hook(hook):
    n_in, n_out = len(hook["arrays"]), len(hook["out_shape"])

    def body(*refs):
        h_ins, h_outs, sems = refs[:n_in], refs[n_in:n_in + n_out], refs[n_in + n_out:]
        hook["start"](h_ins, h_outs, sems)
        hook["finish"](h_ins, h_outs, sems)

    return pl.pallas_call(body, in_specs=[ANY] * n_in, out_specs=[ANY] * n_out, out_shape=list(hook["out_shape"]),
                          scratch_shapes=list(hook["sems"]), name=hook["name"])(*hook["arrays"])


def _remote(src, dst, send_sem, recv_sem, peer):
    return pltpu.make_async_remote_copy(src_ref=src, dst_ref=dst, send_sem=send_sem, recv_sem=recv_sem, device_id=peer, device_id_type=MESH)


def _gather_hook(shards):
    def copies(h_ins, h_outs, sems, kind):
        ici_send, ici_recv, d2d_send, d2d_recv = sems
        x, y, c = _place()
        chip = 2 * x + y
        out = []
        for t in range(len(shards)):
            half = shards[t].shape[0] // 2
            mine_r, other_r = pl.ds(c * half, half), pl.ds((1 - c) * half, half)
            for f, (px, py) in enumerate(_other_chips(x, y)):
                k, peer_chip = 3 * t + f, 2 * px + py
                if kind in ("send", "land"):
                    slot = chip if kind == "send" else peer_chip
                    out.append(_remote(h_ins[t].at[mine_r], h_outs[t].at[slot, mine_r], ici_send.at[k], ici_recv.at[k], (px, py, c)))
                else:
                    rows = mine_r if kind == "pass" else other_r
                    out.append(_remote(h_outs[t].at[peer_chip, rows], h_outs[t].at[peer_chip, rows], d2d_send.at[k], d2d_recv.at[k],
                                       (x, y, 1 - c)))
        return out

    def start(h_ins, h_outs, sems):
        for cp in copies(h_ins, h_outs, sems, "send"):
            cp.start()

    def finish(h_ins, h_outs, sems):
        passed = copies(h_ins, h_outs, sems, "pass")
        for landed, forward in zip(copies(h_ins, h_outs, sems, "land"), passed):
            landed.wait_recv()
            forward.start()
        for cp in copies(h_ins, h_outs, sems, "get"):
            cp.wait_recv()
        for cp in copies(h_ins, h_outs, sems, "send") + passed:
            cp.wait_send()

    return dict(name="gather", arrays=list(shards), out_shape=[S_((N_CHIPS,) + s.shape, s.dtype) for s in shards],
                sems=[pltpu.SemaphoreType.DMA((3 * len(shards),)) for _ in range(4)], start=start, finish=finish)


def _to_sibling_hook(parts, half_rows=False):
    def copies(h_ins, h_outs, sems):
        x, y, c = _place()
        out = []
        for t, p in enumerate(parts):
            src = h_ins[t].at[:, pl.ds((1 - c) * (p.shape[1] // 2), p.shape[1] // 2)] if half_rows else h_ins[t]
            out.append(_remote(src, h_outs[t], sems[0].at[t], sems[1].at[t], (x, y, 1 - c)))
        return out

    def start(h_ins, h_outs, sems):
        for cp in copies(h_ins, h_outs, sems):
            cp.start()

    def finish(h_ins, h_outs, sems):
        cps = copies(h_ins, h_outs, sems)
        for cp in cps:
            cp.wait_recv()
        for cp in cps:
            cp.wait_send()

    shapes = [(p.shape[0], p.shape[1] // 2, p.shape[2]) if half_rows else p.shape for p in parts]
    return dict(name="to_sibling", arrays=list(parts), out_shape=[S_(s, p.dtype) for s, p in zip(shapes, parts)],
                sems=[pltpu.SemaphoreType.DMA((len(parts),)), pltpu.SemaphoreType.DMA((len(parts),))], start=start, finish=finish)


def _to_chips_hook(parts):
    def copies(h_ins, h_outs, sems):
        x, y, c = _place()
        return [_remote(h_ins[t].at[2 * px + py], h_outs[t].at[f], sems[0].at[3 * t + f], sems[1].at[3 * t + f], (px, py, c))
                for t in range(len(parts)) for f, (px, py) in enumerate(_other_chips(x, y))]

    def start(h_ins, h_outs, sems):
        for cp in copies(h_ins, h_outs, sems):
            cp.start()

    def finish(h_ins, h_outs, sems):
        cps = copies(h_ins, h_outs, sems)
        for cp in cps:
            cp.wait_recv()
        for cp in cps:
            cp.wait_send()

    return dict(name="to_chips", arrays=list(parts), out_shape=[S_((3,) + p.shape[1:], p.dtype) for p in parts],
                sems=[pltpu.SemaphoreType.DMA((3 * len(parts),)), pltpu.SemaphoreType.DMA((3 * len(parts),))], start=start, finish=finish)


def _add_pair(a, b, core, name):
    n, half, c = b.shape

    def body(core_ref, a_ref, b_ref, o_ref):
        o_ref[...] = (a_ref[...].astype(F32) + b_ref[...].astype(F32)).astype(BF16)

    spec = pltpu.PrefetchScalarGridSpec(
        num_scalar_prefetch=1, grid=(n,),
        in_specs=[pl.BlockSpec((None, half, c), lambda i, core_ref: (i, core_ref[0], 0)),
                  pl.BlockSpec((None, half, c), lambda i, core_ref: (i, 0, 0))],
        out_specs=pl.BlockSpec((None, half, c), lambda i, core_ref: (i, 0, 0)))
    return pl.pallas_call(body, grid_spec=spec, out_shape=S_(b.shape, BF16), name=name)(core.reshape(1).astype(jnp.int32), a, b)


def _add_four(own, got, rows, name):
    n, r, c = own.shape

    def body(a_ref, g_ref, o_ref):
        o_ref[...] = ((a_ref[...].astype(F32) + g_ref[0].astype(F32)) + g_ref[1].astype(F32)) + g_ref[2].astype(F32)

    blk = pl.BlockSpec((None, rows, c), lambda i, j: (i, j, 0))
    return pl.pallas_call(body, grid=(n, r // rows), in_specs=[blk, pl.BlockSpec((3, None, rows, c), lambda i, j: (0, i, j, 0))],
                          out_specs=blk, out_shape=S_(own.shape, F32), name=name)(own, got)


def _sum_slots(slots):
    rows = slots.shape[1]

    def body(s_ref, o_ref):
        tot = s_ref[0]
        for k in range(1, N_DEV):
            tot = tot + s_ref[k]
        o_ref[...] = tot

    return pl.pallas_call(body, out_shape=S_((rows, 128), F32), name="sum_slots")(slots)


SMALL = ("mix_norm_gain", "ssd_conv_w", "ssd_conv_b", "ssd_dt_bias", "ssd_a_log", "ssd_d", "ssd_out_gain", "gmlp_v_gain",
         "gmlp_w_s", "gmlp_b_s", "attn_q_gain", "attn_k_gain", "rel_bias_table", "ffn_norm_gain", "ffn_conv_w", "ffn_conv_b")
BIG = ("w_in", "w_out", "ffn_w_up", "ffn_w_down")
WEIGHTS = ("mix_norm_gain", "w_in", "ssd_conv_w", "ssd_conv_b", "ssd_dt_bias", "ssd_a_log", "ssd_d", "ssd_out_gain", "gmlp_v_gain",
           "gmlp_w_s", "gmlp_b_s", "attn_q_gain", "attn_k_gain", "rel_bias_table", "w_out", "ffn_norm_gain", "ffn_w_up",
           "ffn_conv_w", "ffn_conv_b", "ffn_w_down")
ADAM_ROWS = {"w_in": 512, "w_out": 256, "ffn_w_up": 256, "ffn_w_down": 352}


PACK_ROWS = 64


def _packed_rows(shape):
    return -(-int(np.prod(shape)) // 1024) * 8


def _pack(arrays):
    parts = []
    for a in arrays:
        rows = _packed_rows(a.shape)
        flat = a.reshape(-1).astype(F32)
        parts.append(jnp.pad(flat, (0, rows * 128 - flat.shape[0])).reshape(rows, 128))
    total = sum(p.shape[0] for p in parts)
    tail = -total % PACK_ROWS
    if tail:
        parts.append(jnp.zeros((tail, 128), F32))
    return jnp.concatenate(parts, axis=0)


def _unpack(buf, shapes):
    out, row = [], 0
    for s in shapes:
        rows, n = _packed_rows(s), int(np.prod(s))
        out.append(buf[row:row + rows].reshape(-1)[:n].reshape(s))
        row += rows
    return out


def _perm_cols(w):
    pad = jnp.zeros(w.shape[:-1] + (NP - IN_WIDTH,), w.dtype)
    return jnp.concatenate([w[..., :1536], w[..., 1552:], w[..., 1536:1552], pad], axis=-1)


def _unperm_cols(w):
    return jnp.concatenate([w[..., :1536], w[..., C_DT:C_DT + 16], w[..., 1536:C_DT]], axis=-1)


def _layer_params(l, p, conv5_w, conv3_w, bias):
    def make(mix_g, conv5, conv5_b, dt_bias, a_log, d_skip, out_gain, v_gain, w_s, b_s, q_gain, k_gain, ffn_g, conv3, conv3_b):
        lanes = lambda a: jnp.pad(a.reshape(1, 16), ((0, 0), (0, 112)))
        cw3 = jnp.pad(jnp.transpose(conv3.reshape(3, 2, FFN_DIM), (1, 0, 2)), ((0, 0), (0, 5), (0, 0)))
        return dict(mix_g=mix_g.reshape(1, D_MODEL), cw5=jnp.pad(conv5, ((0, 3), (0, 0))), cb5=conv5_b.reshape(1, SSD_XBC),
                    dtb=lanes(dt_bias), alog=lanes(a_log), d_exp=jnp.repeat(d_skip, HEAD).reshape(1, SSD_INNER),
                    out_gain=out_gain.reshape(1, SSD_INNER), v_gain=v_gain.reshape(1, GMLP_W), w_s=w_s,
                    b_exp=jnp.repeat(b_s.T, HEAD, axis=1), q_gain=jnp.tile(q_gain, 2).reshape(1, PAIR_W),
                    k_gain=jnp.tile(k_gain, 2).reshape(1, PAIR_W), ffn_g=ffn_g.reshape(1, D_MODEL), cw3=cw3,
                    cb3=conv3_b.reshape(2, 1, FFN_DIM))

    args = (p["mix_norm_gain"][l], conv5_w[l], p["ssd_conv_b"][l], p["ssd_dt_bias"][l], p["ssd_a_log"][l], p["ssd_d"][l],
            p["ssd_out_gain"][l], p["gmlp_v_gain"][l], p["gmlp_w_s"][l], p["gmlp_b_s"][l], p["attn_q_gain"][l], p["attn_k_gain"][l],
            p["ffn_norm_gain"][l], conv3_w[l], p["ffn_conv_b"][l])
    return jax.vjp(make, *args)


def _forward_layer(x, lp, w, bias, hooks=None, resolve=None):
    hooks = hooks or {}
    h = _rmsnorm_fwd(x, lp["mix_g"], "rmsnorm_fwd")
    proj = _mm_nn(h, w["w_in"], tm=1024, tn=1024, tk=1024, out_dtype=F32, name="mm_proj")
    xc = _ssd_pre_fwd(proj, lp["cw5"], lp["cb5"])
    (y_ssd, ssd_sum, ssd_states), got_self = _ssd_scan_fwd(xc, proj, lp["dtb"], lp["alog"], lp["d_exp"], lp["out_gain"],
                                                           hook=hooks.get("self"))
    if got_self is not None:
        w = dict(w, **resolve(got_self))
    y_gmlp = _gmlp_fwd(proj, lp["v_gain"], lp["w_s"], lp["b_exp"])
    (y_attn, attn_o, attn_l), got_attn = _attn_fwd(proj, lp["q_gain"], lp["k_gain"], bias, hook=hooks.get("attn"))
    y = jnp.concatenate([y_ssd, y_gmlp, y_attn], axis=1)
    x2 = _mm_nn(y, w["w_out"], tm=1024, tn=1024, tk=1024, out_dtype=F32, res=x, name="mm_out")
    hn = _rmsnorm_fwd(x2, lp["ffn_g"], "rmsnorm_fwd")
    up3 = _mm_up(hn, w["ffn_w_up"])
    act, got_gate = _convgate_fwd(up3, lp["cw3"], lp["cb3"], hook=hooks.get("gate"))
    x3 = _mm_nn(act, w["ffn_w_down"], tm=1024, tn=1024, tk=HALF_TILE, out_dtype=F32, res=x2, name="mm_down")
    saved = dict(x=x, h=h, proj=proj, xc=xc, y=y, x2=x2, hn=hn, up3=up3, act=act, attn_o=attn_o, attn_l=attn_l,
                 ssd_sum=ssd_sum, ssd_states=ssd_states)
    return x3, saved, w, dict(attn=got_attn, gate=got_gate)


def _backward_layer(dx3, sv, lp, w, bias, pending=None, reducer=None):
    d_act = _mm_nt(dx3, w["ffn_w_down"], tm=1024, tn=HALF_TILE, tk=1024, out_dtype=F32, name="mm_dact")
    dw_down = _mm_tn(sv["act"], dx3, tm=HALF_TILE, tn=1024, tk=1024, out_dtype=BF16, name="mm_dwdown")
    (dup3, dcw3), from_sibling = _convgate_bwd(sv["up3"], lp["cw3"], lp["cb3"], d_act, hook=pending.sibling_hook() if pending else None)
    if pending:
        pending.add_sibling(from_sibling)
    d_hn = _mm_dhn(dup3, w["ffn_w_up"])
    dw_up = _mm_dwup(sv["hn"], dup3)
    dx2, d_ffn_g = _rmsnorm_bwd(sv["x2"], lp["ffn_g"], d_hn, dx3, "rmsnorm_bwd")
    d_y = _mm_nt(dx2, w["w_out"], tm=1024, tn=1024, tk=1024, out_dtype=F32, name="mm_dy")
    dw_out = _mm_tn(sv["y"], dx2, tm=1024, tn=1024, tk=1024, out_dtype=BF16, name="mm_dwout")
    early = reducer(("w_out", "ffn_w_up", "ffn_w_down"), (dw_out, dw_up, dw_down)) if reducer else None
    proj, xc = sv["proj"], sv["xc"]
    (dxs, dbc, dcc, dz, ddt2, ddtb2, dal2, d_dexp, d_outg), from_chips = _ssd_scan_bwd(
        xc, proj, lp["dtb"], lp["alog"], lp["d_exp"], lp["out_gain"], d_y, sv["ssd_sum"], sv["ssd_states"],
        hook=pending.chips_hook() if pending else None)
    if pending:
        pending.add_chips(from_chips)
    (d_xbc, dcw5), from_sibling = _ssd_pre_bwd(proj, lp["cw5"], lp["cb5"], jnp.concatenate([dxs, dbc, dcc], axis=1),
                                               hook=early.sibling_hook() if early else None)
    if early:
        early.add_sibling(from_sibling)
    d_gu, d_gv, d_vg, d_ws, d_bexp = _gmlp_bwd(proj, lp["v_gain"], lp["w_s"], lp["b_exp"], d_y)
    (d_q, d_k, d_v, d_qg2, d_kg2, d_bias), from_chips = _attn_bwd(proj, lp["q_gain"], lp["k_gain"], bias, d_y, sv["attn_o"], sv["attn_l"],
                                                                  hook=early.chips_hook() if early else None)
    if early:
        early.add_chips(from_chips)
    d_dt = (ddt2[0] + ddt2[1]).astype(BF16)
    d_proj = jnp.concatenate([d_xbc, dz, d_gu, d_gv, d_q, d_k, d_v, d_dt, jnp.zeros((T, NP - C_DT - 128), BF16)], axis=1)
    d_h = _mm_nt(d_proj, w["w_in"], tm=1024, tn=1024, tk=1024, out_dtype=F32, name="mm_dh")
    dw_in = _mm_tn(sv["h"], d_proj, tm=1024, tn=1024, tk=1024, out_dtype=BF16, name="mm_dwin")
    late = None
    if reducer:
        late = reducer(("w_in",), (dw_in,))
        late.run_alone()
    dx, d_mix_g = _rmsnorm_bwd(sv["x"], lp["mix_g"], d_h, dx2, "rmsnorm_bwd")
    d_lp = dict(mix_g=d_mix_g, cw5=dcw5[:8] * (jnp.arange(8) < 5)[:, None].astype(F32), cb5=dcw5[5:6],
                dtb=(ddtb2[0, :1] + ddtb2[1, :1]), alog=(dal2[0, :1] + dal2[1, :1]), d_exp=d_dexp, out_gain=d_outg,
                v_gain=d_vg, w_s=d_ws, b_exp=d_bexp, q_gain=d_qg2[0, :1] + d_qg2[1, :1], k_gain=d_kg2[0, :1] + d_kg2[1, :1],
                ffn_g=d_ffn_g, cw3=dcw3 * (jnp.arange(8) < 3)[None, :, None].astype(F32), cb3=dcw3[:, 3:4])
    return dx, dict(w_in=dw_in, w_out=dw_out, ffn_w_up=dw_up, ffn_w_down=dw_down), d_lp, d_bias, (early, late)


def _to_shard_major(name, dw):
    if name == "ffn_w_up":
        return dw
    if name == "w_in":
        r, c = dw.shape[0], IN_WIDTH
        return jnp.transpose(_unperm_cols(dw).reshape(r, N_CHIPS, c // N_CHIPS), (1, 0, 2))
    r, c = dw.shape
    return dw.reshape(N_CHIPS, r // N_CHIPS, c)


def _whole_weight(name, gathered, own, chip):
    if name == "w_in":
        return _perm_cols(jnp.concatenate([jnp.where(chip == k, own, gathered[k]) for k in range(N_CHIPS)], axis=1))
    w = lax.dynamic_update_index_in_dim(gathered, own, chip, axis=0)
    return w if name == "ffn_w_up" else w.reshape(N_CHIPS * own.shape[0], own.shape[1])


class _LayerReduce:
    def __init__(self, names, dws, chip, core):
        self.names, self.chip, self.core = names, chip, core
        self.parts = [_to_shard_major(n, dw) for n, dw in zip(names, dws)]

    def sibling_hook(self):
        return _to_sibling_hook(self.parts, half_rows=True)

    def add_sibling(self, got):
        self.sums = [_add_pair(a, b, self.core, "add_pair_" + n) for n, a, b in zip(self.names, self.parts, got)]

    def chips_hook(self):
        return _to_chips_hook(self.sums)

    def add_chips(self, got):
        self.half = {}
        for n, s2, g3 in zip(self.names, self.sums, got):
            own = lax.dynamic_index_in_dim(s2, self.chip, axis=0, keepdims=True)
            self.half[n] = _add_four(own, g3[:, None], own.shape[1], "add_four_" + n)[0]

    def run_alone(self):
        self.add_sibling(_run_hook(self.sibling_hook()))
        self.add_chips(_run_hook(self.chips_hook()))


def _join_halves(halves, core):
    other = _run_hook(dict(_to_sibling_hook(halves), name="swap_halves"))
    out = []
    for mine_h, other_h in zip(halves, other):
        both = jnp.stack([mine_h, other_h])
        first = lax.dynamic_index_in_dim(both, core, axis=0, keepdims=False)
        second = lax.dynamic_index_in_dim(both, 1 - core, axis=0, keepdims=False)
        out.append(jnp.concatenate([first, second], axis=1))
    return out


LAYER_SMALL = ("mix_norm_gain", "ssd_conv_w", "ssd_conv_b", "ssd_dt_bias", "ssd_a_log", "ssd_d", "ssd_out_gain", "gmlp_v_gain",
               "gmlp_w_s", "gmlp_b_s", "attn_q_gain", "attn_k_gain", "ffn_norm_gain", "ffn_conv_w", "ffn_conv_b")


def _local_grads(x, loss_target, p, conv5_w, conv3_w, layer_w, exchange=None):
    bias, bias_vjp = jax.vjp(_attn_bias, p["rel_bias_table"])
    xt = x.reshape(T, D_MODEL)
    layer_w = list(layer_w)
    saved, lps, lp_vjps = [], [], []
    if exchange is not None:
        chip, core, own = exchange
        whole = lambda names, layer, gathered: {n: _whole_weight(n, g, own[layer][BIG.index(n)], chip) for n, g in zip(names, gathered)}
    for l in range(DEPTH):
        lp, lp_vjp = _layer_params(l, p, conv5_w, conv3_w, bias)
        hooks = {}
        if exchange is not None and l == 0:
            hooks["self"] = _gather_hook(own[0][1:])
        if exchange is not None and l + 1 < DEPTH:
            hooks["attn"] = _gather_hook(own[l + 1][2:])
            hooks["gate"] = _gather_hook(own[l + 1][:2])
        xt, sv, layer_w[l], got = _forward_layer(xt, lp, layer_w[l], bias, hooks, resolve=lambda g: whole(BIG[1:], 0, g))
        if "attn" in hooks:
            layer_w.append(dict(whole(BIG[:2], l + 1, got["gate"]), **whole(BIG[2:], l + 1, got["attn"])))
        saved.append(sv)
        lps.append(lp)
        lp_vjps.append(lp_vjp)
    dxt, loss_parts = _loss_head(xt, loss_target.reshape(T, D_MODEL))
    loss_local = jnp.sum(loss_parts[::8, 0])

    big_grads = [None] * DEPTH
    small_layers = [None] * DEPTH
    d_bias_tot = jnp.zeros_like(bias)
    pending = None
    for l in reversed(range(DEPTH)):
        last = exchange is not None and l == 0
        dxt, big_grads[l], d_lp, d_bias, own_reduce = _backward_layer(
            dxt, saved[l], lps[l], layer_w[l], bias, pending=pending,
            reducer=(lambda names, dws: _LayerReduce(names, dws, chip, core)) if last else None)
        if pending is not None:
            big_grads[l + 1] = pending.half
        if last:
            big_grads[l] = dict(own_reduce[0].half, **own_reduce[1].half)
        elif exchange is not None:
            pending = _LayerReduce(BIG, [big_grads[l][n] for n in BIG], chip, core)
        small_layers[l] = lp_vjps[l](d_lp)
        d_bias_tot = d_bias_tot + d_bias
    (d_rel_table,) = bias_vjp(d_bias_tot)
    local_small = {n: jnp.stack([small_layers[l][i] for l in range(DEPTH)]) for i, n in enumerate(LAYER_SMALL)}
    local_small["rel_bias_table"] = d_rel_table
    return dxt, loss_local, big_grads, local_small


def kernel(x, mix_norm_gain, w_in, ssd_conv_w, ssd_conv_b, ssd_dt_bias, ssd_a_log, ssd_d, ssd_out_gain, gmlp_v_gain, gmlp_w_s, gmlp_b_s, attn_q_gain, attn_k_gain, rel_bias_table, w_out, ffn_norm_gain, ffn_w_up, ffn_conv_w, ffn_conv_b, ffn_w_down, loss_target, m_mix_norm_gain, m_w_in, m_ssd_conv_w, m_ssd_conv_b, m_ssd_dt_bias, m_ssd_a_log, m_ssd_d, m_ssd_out_gain, m_gmlp_v_gain, m_gmlp_w_s, m_gmlp_b_s, m_attn_q_gain, m_attn_k_gain, m_rel_bias_table, m_w_out, m_ffn_norm_gain, m_ffn_w_up, m_ffn_conv_w, m_ffn_conv_b, m_ffn_w_down, v_mix_norm_gain, v_w_in, v_ssd_conv_w, v_ssd_conv_b, v_ssd_dt_bias, v_ssd_a_log, v_ssd_d, v_ssd_out_gain, v_gmlp_v_gain, v_gmlp_w_s, v_gmlp_b_s, v_attn_q_gain, v_attn_k_gain, v_rel_bias_table, v_w_out, v_ffn_norm_gain, v_ffn_w_up, v_ffn_conv_w, v_ffn_conv_b, v_ffn_w_down):
    env = dict(locals())
    p = {n: env[n] for n in WEIGHTS}
    chip = 2 * lax.axis_index("x") + lax.axis_index("y")
    core = lax.axis_index("c")

    conv_slots = _allgather8(_pack([ssd_conv_w, ffn_conv_w]), "allgather_conv")
    conv_parts = [_unpack(conv_slots[2 * k], [ssd_conv_w.shape, ffn_conv_w.shape]) for k in range(N_CHIPS)]
    conv5_w = jnp.concatenate([cp[0] for cp in conv_parts], axis=-1)
    conv3_w = jnp.concatenate([cp[1] for cp in conv_parts], axis=-1)
    own = [[p[n][l].astype(BF16) for n in BIG] for l in range(DEPTH)]
    (first,) = _run_hook(_gather_hook(own[0][:1]))
    layer_w = [{"w_in": _whole_weight("w_in", first, own[0][0], chip)}]

    dxt, loss_local, reduced, local_small = _local_grads(x, loss_target, p, conv5_w, conv3_w, layer_w, exchange=(chip, core, own))

    small_shapes = [local_small[n].shape for n in SMALL] + [(1,)]
    slots = _allgather8(_pack([local_small[n] for n in SMALL] + [loss_local.reshape(1)]), "allgather_small")
    summed = _unpack(_sum_slots(slots), small_shapes)
    grads = dict(zip(SMALL, summed[:-1]))
    loss = summed[-1][0]
    grads["ssd_conv_w"] = lax.dynamic_slice_in_dim(grads["ssd_conv_w"], chip * 256, 256, axis=2)
    grads["ffn_conv_w"] = lax.dynamic_slice_in_dim(grads["ffn_conv_w"], chip * (2 * FFN_DIM // N_CHIPS), 2 * FFN_DIM // N_CHIPS, axis=2)

    halves = [jnp.stack([reduced[l][n] for l in range(DEPTH)]) for n in BIG]
    for n, g in zip(BIG, _join_halves(halves, core)):
        grads[n] = g

    delta, new_m, new_v = {}, {}, {}
    for n in BIG:
        delta[n], new_m[n], new_v[n] = _adamw(p[n], grads[n], env["m_" + n], env["v_" + n], ADAM_ROWS[n], "adamw_" + n)
    d, nm, nv = _adamw_many([p[n] for n in SMALL], [grads[n] for n in SMALL], [env["m_" + n] for n in SMALL], [env["v_" + n] for n in SMALL])
    for n, a, b, c in zip(SMALL, d, nm, nv):
        delta[n], new_m[n], new_v[n] = a, b, c

    return (loss, dxt.reshape(B_LOC, SEQ, D_MODEL), *[grads[n] for n in WEIGHTS], *[delta[n] for n in WEIGHTS],
            *[new_m[n] for n in WEIGHTS], *[new_v[n] for n in WEIGHTS])
```

```python
import functools
import math

import jax
import jax.numpy as jnp
import numpy as np
from jax import lax
from jax.experimental import pallas as pl
from jax.experimental.pallas import tpu as pltpu

F32 = jnp.float32
BF16 = jnp.bfloat16
HI = lax.Precision.HIGHEST
SUM_PRECISION = lax.Precision.HIGH
MESH = pl.DeviceIdType.MESH
ANY = pl.BlockSpec(memory_space=pl.ANY)

D_MODEL = 1024
SEQ = 2048
B_LOC = 2
T = B_LOC * SEQ
DEPTH = 4
N_CHIPS = 4
N_DEV = 8
HEAD = 64
CHUNK = 128
N_CHUNK = SEQ // CHUNK
SSD_INNER = 512
SSD_XBC = 1024
FFN_DIM = 2816
IN_WIDTH = 2832
NP = 3072
C_XS, C_B, C_C, C_Z, C_GU, C_GV, C_Q, C_K, C_V, C_DT = 0, 512, 768, 1024, 1536, 1792, 2048, 2304, 2560, 2816
NORM_EPS = 1e-6
NEG_INF = -1e30
ATTN_DILS = (1, 4, 16)
ATTN_HALF = 64
ADAM_LR, ADAM_B1, ADAM_B2, ADAM_EPS, ADAM_WD, ADAM_STEP = 0.001, 0.9, 0.999, 1e-08, 0.01, 10
VMEM_LIMIT = 56 * 1024 * 1024

S_ = jax.ShapeDtypeStruct


def _cp():
    return pltpu.CompilerParams(vmem_limit_bytes=VMEM_LIMIT)


def _shift_rows(x, k):
    n = x.shape[0]
    if k == 0:
        return x
    r = pltpu.roll(x, (-k) % n, 0)
    t = lax.broadcasted_iota(jnp.int32, (n, 1), 0)
    return jnp.where((t + k >= 0) & (t + k < n), r, 0.0)


@functools.partial(jax.custom_vjp, nondiff_argnums=(1,))
def _shift(x, k):
    return _shift_rows(x, k)


def _shift_fwd(x, k):
    return _shift_rows(x, k), None


def _shift_bwd(k, _, g):
    return (_shift_rows(g, -k),)


_shift.defvjp(_shift_fwd, _shift_bwd)


def _dwconv(x, taps, bias):
    half = len(taps) // 2
    y = bias
    for k, w in enumerate(taps):
        y = y + w * _shift(x, k - half)
    return y


def _softplus(x):
    return jnp.maximum(x, 0.0) + jnp.log1p(jnp.exp(-jnp.abs(x)))


def _dot(a, b):
    return jnp.dot(a.astype(BF16), b.astype(BF16), preferred_element_type=F32)


def _dot_nt(a, b):
    return lax.dot_general(a.astype(BF16), b.astype(BF16), (((1,), (1,)), ((), ())), preferred_element_type=F32)


def _dot_tn(a, b):
    return lax.dot_general(a.astype(BF16), b.astype(BF16), (((0,), (0,)), ((), ())), preferred_element_type=F32)


def _head_sum_matrix(width):
    i = lax.broadcasted_iota(jnp.int32, (width, width), 0) // HEAD
    j = lax.broadcasted_iota(jnp.int32, (width, width), 1) // HEAD
    return (i == j).astype(F32)


def _matmul(a, b, *, dims, grid, a_spec, b_spec, o_spec, out_shape, acc_shape, res=None, res_spec=None, norm_gain=None, name):
    nk = grid[2]
    n_in = 2 + (res is not None) + (norm_gain is not None)

    def body(*refs):
        a_ref, b_ref = refs[:2]
        r_ref = refs[2] if res is not None else None
        g_ref = refs[n_in - 1] if norm_gain is not None else None
        o_ref = refs[n_in]
        n_ref = refs[n_in + 1] if norm_gain is not None else None

        def finish(tot):
            if r_ref is not None:
                tot = tot + r_ref[...]
            o_ref[...] = tot.astype(o_ref.dtype)
            if n_ref is not None:
                scale = lax.rsqrt(jnp.mean(tot * tot, axis=-1, keepdims=True) + NORM_EPS)
                n_ref[...] = (tot * scale * g_ref[...]).astype(BF16)

        part = lax.dot_general(a_ref[...].astype(BF16), b_ref[...].astype(BF16), dims, preferred_element_type=F32)
        if nk == 1:
            finish(part)
            return
        acc_ref = refs[-1]
        k = pl.program_id(2)

        @pl.when(k == 0)
        def _():
            acc_ref[...] = part

        @pl.when(k > 0)
        def _():
            acc_ref[...] += part

        @pl.when(k == nk - 1)
        def _():
            finish(acc_ref[...])

    in_specs = [a_spec, b_spec] + ([res_spec] if res is not None else [])
    args = (a, b) + ((res,) if res is not None else ())
    out_specs, out_shapes = o_spec, out_shape
    if norm_gain is not None:
        in_specs.append(pl.BlockSpec((1, acc_shape[1]), lambda i, j, q: (0, 0)))
        args = args + (norm_gain,)
        out_specs, out_shapes = [o_spec, o_spec], [out_shape, S_(out_shape.shape, BF16)]
    scratch = [] if nk == 1 else [pltpu.VMEM(acc_shape, F32)]
    return pl.pallas_call(body, grid=grid, in_specs=in_specs, out_specs=out_specs, out_shape=out_shapes,
                          scratch_shapes=scratch, compiler_params=_cp(), name=name)(*args)


NN = (((1,), (0,)), ((), ()))
NT = (((1,), (1,)), ((), ()))
TN = (((0,), (0,)), ((), ()))


def _mm_nn(a, b, *, tm, tn, tk, out_dtype, res=None, norm_gain=None, name):
    m, k = a.shape
    n = b.shape[1]
    assert norm_gain is None or tn == n
    return _matmul(a, b, dims=NN, grid=(m // tm, n // tn, k // tk),
                   a_spec=pl.BlockSpec((tm, tk), lambda i, j, q: (i, q)),
                   b_spec=pl.BlockSpec((tk, tn), lambda i, j, q: (q, j)),
                   o_spec=pl.BlockSpec((tm, tn), lambda i, j, q: (i, j)),
                   out_shape=S_((m, n), out_dtype), acc_shape=(tm, tn), res=res,
                   res_spec=pl.BlockSpec((tm, tn), lambda i, j, q: (i, j)), norm_gain=norm_gain, name=name)


def _mm_nt(a, b, *, tm, tn, tk, out_dtype, name):
    m, k = a.shape
    n = b.shape[0]
    return _matmul(a, b, dims=NT, grid=(m // tm, n // tn, k // tk),
                   a_spec=pl.BlockSpec((tm, tk), lambda i, j, q: (i, q)),
                   b_spec=pl.BlockSpec((tn, tk), lambda i, j, q: (j, q)),
                   o_spec=pl.BlockSpec((tm, tn), lambda i, j, q: (i, j)),
                   out_shape=S_((m, n), out_dtype), acc_shape=(tm, tn), name=name)


def _mm_tn(a, b, *, tm, tn, tk, out_dtype, name):
    k, m = a.shape
    n = b.shape[1]
    return _matmul(a, b, dims=TN, grid=(m // tm, n // tn, k // tk),
                   a_spec=pl.BlockSpec((tk, tm), lambda i, j, q: (q, i)),
                   b_spec=pl.BlockSpec((tk, tn), lambda i, j, q: (q, j)),
                   o_spec=pl.BlockSpec((tm, tn), lambda i, j, q: (i, j)),
                   out_shape=S_((m, n), out_dtype), acc_shape=(tm, tn), name=name)


HALF_TILE = FFN_DIM // 2


def _mm_up(hn, w_up):
    return _matmul(hn, w_up, dims=NN, grid=(T // 1024, 4, 1),
                   a_spec=pl.BlockSpec((1024, D_MODEL), lambda i, j, q: (i, 0)),
                   b_spec=pl.BlockSpec((None, D_MODEL, HALF_TILE), lambda i, j, q: (j, 0, 0)),
                   o_spec=pl.BlockSpec((None, 1024, HALF_TILE), lambda i, j, q: (j // 2, i, j % 2)),
                   out_shape=S_((2, T, FFN_DIM), F32), acc_shape=(1024, HALF_TILE), name="mm_up")


def _mm_dhn(dup3, w_up):
    return _matmul(dup3, w_up, dims=NT, grid=(T // 1024, 1, 4),
                   a_spec=pl.BlockSpec((None, 1024, HALF_TILE), lambda i, j, q: (q // 2, i, q % 2)),
                   b_spec=pl.BlockSpec((None, D_MODEL, HALF_TILE), lambda i, j, q: (q, 0, 0)),
                   o_spec=pl.BlockSpec((1024, D_MODEL), lambda i, j, q: (i, 0)),
                   out_shape=S_((T, D_MODEL), F32), acc_shape=(1024, D_MODEL), name="mm_dhn")


def _mm_dwup(hn, dup3):
    return _matmul(hn, dup3, dims=TN, grid=(1, 4, T // 1024),
                   a_spec=pl.BlockSpec((1024, D_MODEL), lambda i, j, q: (q, 0)),
                   b_spec=pl.BlockSpec((None, 1024, HALF_TILE), lambda i, j, q: (j // 2, q, j % 2)),
                   o_spec=pl.BlockSpec((None, D_MODEL, HALF_TILE), lambda i, j, q: (j, 0, 0)),
                   out_shape=S_((N_CHIPS, D_MODEL, HALF_TILE), BF16), acc_shape=(D_MODEL, HALF_TILE), name="mm_dwup")


ROWS = 512


def _rmsnorm_fwd(x, gain, name):
    def body(x_ref, g_ref, o_ref):
        xv = x_ref[...]
        r = lax.rsqrt(jnp.mean(xv * xv, axis=-1, keepdims=True) + NORM_EPS)
        o_ref[...] = (xv * r * g_ref[...]).astype(BF16)

    return pl.pallas_call(body, grid=(T // ROWS,),
                          in_specs=[pl.BlockSpec((ROWS, D_MODEL), lambda i: (i, 0)), pl.BlockSpec((1, D_MODEL), lambda i: (0, 0))],
                          out_specs=pl.BlockSpec((ROWS, D_MODEL), lambda i: (i, 0)),
                          out_shape=S_((T, D_MODEL), BF16), name=name)(x, gain)


def _rmsnorm_bwd(x, gain, dh, dres, name):
    def body(x_ref, g_ref, dh_ref, dres_ref, dx_ref, dg_ref):
        xv = x_ref[...]
        r = lax.rsqrt(jnp.mean(xv * xv, axis=-1, keepdims=True) + NORM_EPS)
        gd = dh_ref[...] * g_ref[...]
        dot = jnp.mean(gd * xv, axis=-1, keepdims=True)
        dx_ref[...] = dres_ref[...] + r * gd - xv * (r * r * r * dot)
        part = jnp.sum(dh_ref[...] * xv * r, axis=0, keepdims=True)

        @pl.when(pl.program_id(0) == 0)
        def _():
            dg_ref[...] = part

        @pl.when(pl.program_id(0) > 0)
        def _():
            dg_ref[...] += part

    row = pl.BlockSpec((ROWS, D_MODEL), lambda i: (i, 0))
    vec = pl.BlockSpec((1, D_MODEL), lambda i: (0, 0))
    return pl.pallas_call(body, grid=(T // ROWS,), in_specs=[row, vec, row, row], out_specs=[row, vec],
                          out_shape=[S_((T, D_MODEL), F32), S_((1, D_MODEL), F32)], name=name)(x, gain, dh, dres)


def _loss_head(y, target):
    def body(y_ref, t_ref, dy_ref, p_ref):
        e = y_ref[...] - t_ref[...]
        dy_ref[...] = e * (1.0 / D_MODEL)
        p_ref[...] = jnp.full((8, 128), 0.5 / D_MODEL, F32) * jnp.sum(e * e)

    row = pl.BlockSpec((ROWS, D_MODEL), lambda i: (i, 0))
    return pl.pallas_call(body, grid=(T // ROWS,), in_specs=[row, row],
                          out_specs=[row, pl.BlockSpec((8, 128), lambda i: (i, 0))],
                          out_shape=[S_((T, D_MODEL), F32), S_((T // ROWS * 8, 128), F32)], name="loss_head")(y, target)


def _adamw_update(w_ref, g_ref, m_ref, v_ref, d_ref, nm_ref, nv_ref):
    gv = g_ref[...]
    nm = ADAM_B1 * m_ref[...] + (1.0 - ADAM_B1) * gv
    nv = ADAM_B2 * v_ref[...] + (1.0 - ADAM_B2) * (gv * gv)
    m_hat = nm / (1.0 - ADAM_B1 ** ADAM_STEP)
    v_hat = nv / (1.0 - ADAM_B2 ** ADAM_STEP)
    d_ref[...] = -ADAM_LR * (m_hat / (jnp.sqrt(v_hat) + ADAM_EPS) + ADAM_WD * w_ref[...])
    nm_ref[...] = nm
    nv_ref[...] = nv


def _adamw(w, g_mine, g_other, m, v, core, rows, name):
    per_half = w.shape[1] // 2 // rows

    def body(core_ref, w_ref, gm_ref, go_ref, m_ref, v_ref, g_ref, d_ref, nm_ref, nv_ref):
        mine = (pl.program_id(1) // per_half) == core_ref[0]
        g_ref[...] = jnp.where(mine, gm_ref[...], go_ref[...])
        _adamw_update(w_ref, g_ref, m_ref, v_ref, d_ref, nm_ref, nv_ref)

    blk = pl.BlockSpec((None, rows, w.shape[2]), lambda l, i, core_ref: (l, i, 0))
    half = pl.BlockSpec((None, rows, w.shape[2]), lambda l, i, core_ref: (l, i % per_half, 0))
    out = S_(w.shape, F32)
    spec = pltpu.PrefetchScalarGridSpec(num_scalar_prefetch=1, grid=(w.shape[0], w.shape[1] // rows),
                                        in_specs=[blk, half, half, blk, blk], out_specs=[blk] * 4)
    return pl.pallas_call(body, grid_spec=spec, out_shape=[out] * 4, name=name)(core.reshape(1).astype(jnp.int32), w, g_mine, g_other, m, v)


def _adamw_many(ws, gs, ms, vs):
    n = len(ws)

    def body(*refs):
        for i in range(n):
            _adamw_update(*[refs[k * n + i] for k in range(7)])

    out = [S_(w.shape, F32) for w in ws]
    res = pl.pallas_call(body, out_shape=out * 3, name="adamw_small")(*ws, *gs, *ms, *vs)
    return res[:n], res[n:2 * n], res[2 * n:]


FFN_CT = 256


def _gate_fn(up_g, up_v, wg0, wg1, wg2, bg, wv0, wv1, wv2, bv):
    gate = _dwconv(up_g, [wg0, wg1, wg2], bg)
    val = _dwconv(up_v, [wv0, wv1, wv2], bv)
    return jax.nn.silu(gate) * val


def _taps(ref, part, n):
    return [ref[part, k:k + 1, :] for k in range(n)]


def _convgate_fwd(up3, cw, cb, hook=None):
    def body(up_ref, cw_ref, cb_ref, o_ref):
        o_ref[...] = _gate_fn(up_ref[0], up_ref[1], *_taps(cw_ref, 0, 3), cb_ref[0], *_taps(cw_ref, 1, 3), cb_ref[1]).astype(BF16)

    (act,), got = _hooked_call(
        body, grid=(FFN_DIM // FFN_CT, B_LOC),
        in_specs=[pl.BlockSpec((2, SEQ, FFN_CT), lambda j, b: (0, b, j)),
                  pl.BlockSpec((2, 8, FFN_CT), lambda j, b: (0, 0, j)),
                  pl.BlockSpec((2, 1, FFN_CT), lambda j, b: (0, 0, j))],
        out_specs=[pl.BlockSpec((SEQ, FFN_CT), lambda j, b: (b, j))],
        out_shape=[S_((T, FFN_DIM), BF16)], scratch_shapes=[], args=(up3, cw, cb), hook=hook, name="convgate_fwd")
    return act, got


def _convgate_bwd(up3, cw, cb, dact, hook=None):
    def body(up_ref, cw_ref, cb_ref, da_ref, dup_ref, dcw_ref):
        args = (up_ref[0], up_ref[1], *_taps(cw_ref, 0, 3), cb_ref[0], *_taps(cw_ref, 1, 3), cb_ref[1])
        _, vjp = jax.vjp(_gate_fn, *args)
        dg, dv, g0, g1, g2, gb, v0, v1, v2, vb = vjp(da_ref[...])
        dup_ref[0] = dg.astype(BF16)
        dup_ref[1] = dv.astype(BF16)
        zero = jnp.zeros((4, FFN_CT), F32)
        new = jnp.stack([jnp.concatenate([g0, g1, g2, gb, zero], axis=0), jnp.concatenate([v0, v1, v2, vb, zero], axis=0)])

        @pl.when(pl.program_id(1) == 0)
        def _():
            dcw_ref[...] = new

        @pl.when(pl.program_id(1) > 0)
        def _():
            dcw_ref[...] += new

    return _hooked_call(
        body, grid=(FFN_DIM // FFN_CT, B_LOC),
        in_specs=[pl.BlockSpec((2, SEQ, FFN_CT), lambda j, b: (0, b, j)),
                  pl.BlockSpec((2, 8, FFN_CT), lambda j, b: (0, 0, j)),
                  pl.BlockSpec((2, 1, FFN_CT), lambda j, b: (0, 0, j)),
                  pl.BlockSpec((SEQ, FFN_CT), lambda j, b: (b, j))],
        out_specs=[pl.BlockSpec((2, SEQ, FFN_CT), lambda j, b: (0, b, j)),
                   pl.BlockSpec((2, 8, FFN_CT), lambda j, b: (0, 0, j))],
        out_shape=[S_((2, T, FFN_DIM), BF16), S_((2, 8, FFN_DIM), F32)],
        scratch_shapes=[], args=(up3, cw, cb, dact), hook=hook, name="convgate_bwd")


SSD_CT = 256


def _conv5_fn(x, w0, w1, w2, w3, w4, b):
    return jax.nn.silu(_dwconv(x, [w0, w1, w2, w3, w4], b))


def _ssd_pre_fwd(proj, cw, cb):
    def body(x_ref, cw_ref, cb_ref, o_ref):
        o_ref[...] = _conv5_fn(x_ref[...], *[cw_ref[k:k + 1, :] for k in range(5)], cb_ref[...])

    return pl.pallas_call(
        body, grid=(SSD_XBC // SSD_CT, B_LOC),
        in_specs=[pl.BlockSpec((SEQ, SSD_CT), lambda j, b: (b, j)),
                  pl.BlockSpec((8, SSD_CT), lambda j, b: (0, j)),
                  pl.BlockSpec((1, SSD_CT), lambda j, b: (0, j))],
        out_specs=pl.BlockSpec((SEQ, SSD_CT), lambda j, b: (b, j)),
        out_shape=S_((T, SSD_XBC), F32), compiler_params=_cp(), name="ssd_pre_fwd")(proj, cw, cb)


def _ssd_pre_bwd(proj, cw, cb, dxc, hook=None):
    def body(x_ref, cw_ref, cb_ref, d_ref, dx_ref, dcw_ref):
        _, vjp = jax.vjp(_conv5_fn, x_ref[...], *[cw_ref[k:k + 1, :] for k in range(5)], cb_ref[...])
        dx, g0, g1, g2, g3, g4, gb = vjp(d_ref[...])
        dx_ref[...] = dx.astype(BF16)
        new = jnp.concatenate([g0, g1, g2, g3, g4, gb, jnp.zeros((2, SSD_CT), F32)], axis=0)

        @pl.when(pl.program_id(1) == 0)
        def _():
            dcw_ref[...] = new

        @pl.when(pl.program_id(1) > 0)
        def _():
            dcw_ref[...] += new

    return _hooked_call(
        body, grid=(SSD_XBC // SSD_CT, B_LOC),
        in_specs=[pl.BlockSpec((SEQ, SSD_CT), lambda j, b: (b, j)),
                  pl.BlockSpec((8, SSD_CT), lambda j, b: (0, j)),
                  pl.BlockSpec((1, SSD_CT), lambda j, b: (0, j)),
                  pl.BlockSpec((SEQ, SSD_CT), lambda j, b: (b, j))],
        out_specs=[pl.BlockSpec((SEQ, SSD_CT), lambda j, b: (b, j)),
                   pl.BlockSpec((8, SSD_CT), lambda j, b: (0, j))],
        out_shape=[S_((T, SSD_XBC), BF16), S_((8, SSD_XBC), F32)],
        scratch_shapes=[], args=(proj, cw, cb, dxc), hook=hook, name="ssd_pre_bwd")


GROUP_W = 256
ONE_BUFFER = dict(pipeline_mode=pl.Buffered(1))
HEADS_PER_GROUP = 4
FWD_SCAN_UNROLL = 4
BWD_SCAN_UNROLL = 2


def _ssd_dt_fn(dt_raw, bias, alog):
    dt = _softplus(dt_raw + bias)
    return dt, dt * (-jnp.exp(alog))


def _ssd_chunk_fn(direction, group, xc0, xc1, bc, cc, dt, da, prev0, prev1):
    q = CHUNK
    ti = lax.broadcasted_iota(jnp.int32, (q, q), 0)
    si = lax.broadcasted_iota(jnp.int32, (q, q), 1)
    keep = (ti >= si) if direction == 0 else (ti <= si)
    mat = keep.astype(F32)
    acs = jnp.dot(mat, da, precision=HI, preferred_element_type=F32)
    acs_t = lax.dot_general(da, mat, (((0,), (1,)), ((), ())), precision=HI, preferred_element_type=F32)
    tot = jnp.sum(da, axis=0, keepdims=True)
    lane = lax.broadcasted_iota(jnp.int32, (1, 128), 1)
    sub = lax.broadcasted_iota(jnp.int32, (128, 1), 0)
    first_head = lane < HEAD
    cb = _dot_nt(cc, bc)
    a_cols, tots, dt_cols, lows, douts = [], [], [], [], []
    for h in range(HEADS_PER_GROUP):
        ln = 8 * direction + 4 * group + h
        oh_l = (lane == ln).astype(F32)
        oh_s = (sub == ln).astype(F32)
        a_col = jnp.sum(acs * oh_l, axis=1, keepdims=True)
        a_row = jnp.sum(acs_t * oh_s, axis=0, keepdims=True)
        tot_h = jnp.sum(tot * oh_l, axis=1, keepdims=True)
        a_cols.append(a_col)
        tots.append(tot_h)
        dt_cols.append(jnp.sum(dt * oh_l, axis=1, keepdims=True))
        lows.append(cb * jnp.exp(jnp.where(keep, a_col - a_row, NEG_INF)))
        douts.append(bc * jnp.exp(tot_h - a_col))
    out = []
    for pair, (xc, prev) in enumerate(((xc0, prev0), (xc1, prev1))):
        h0, h1 = 2 * pair, 2 * pair + 1
        xdt = xc * jnp.where(first_head, dt_cols[h0], dt_cols[h1])
        y = jnp.where(first_head, jnp.exp(a_cols[h0]), jnp.exp(a_cols[h1])) * _dot(cc, prev)
        y = y + jnp.where(first_head, _dot(lows[h0], xdt), _dot(lows[h1], xdt))
        st = jnp.where(first_head, _dot_tn(douts[h0], xdt), _dot_tn(douts[h1], xdt))
        out.append((y, prev * jnp.where(first_head, jnp.exp(tots[h0]), jnp.exp(tots[h1])) + st))
    return out[0][0], out[1][0], out[0][1], out[1][1]


def _ssd_post_fn(y, xc, z, d_exp, gain):
    y = (y + d_exp * xc) * jax.nn.silu(z)
    return y * lax.rsqrt(jnp.mean(y * y, axis=-1, keepdims=True) + NORM_EPS) * gain


def _chunk_rows(c):
    return pl.ds(pl.multiple_of(c * CHUNK, CHUNK), CHUNK)


def _scan_loop(step, init, unroll):
    def body(i, carry):
        for k in range(unroll):
            carry = step(i * unroll + k, carry)
        return carry

    return lax.fori_loop(0, N_CHUNK // unroll, body, init)


def _ssd_scan_specs(**mode):
    return [pl.BlockSpec((SEQ, GROUP_W), lambda g, b: (b, g), **mode),
            pl.BlockSpec((SEQ, 128), lambda g, b: (b, C_B // 128 + g), **mode),
            pl.BlockSpec((SEQ, 128), lambda g, b: (b, C_C // 128 + g), **mode),
            pl.BlockSpec((SEQ, GROUP_W), lambda g, b: (b, C_Z // GROUP_W + g), **mode),
            pl.BlockSpec((SEQ, 128), lambda g, b: (b, C_DT // 128), **mode),
            pl.BlockSpec((1, 128), lambda g, b: (0, 0)),
            pl.BlockSpec((1, 128), lambda g, b: (0, 0)),
            pl.BlockSpec((1, GROUP_W), lambda g, b: (0, g)),
            pl.BlockSpec((1, GROUP_W), lambda g, b: (0, g))]


def _ssd_state_spec(**mode):
    return pl.BlockSpec((None, None, 2 * N_CHUNK, 128, GROUP_W), lambda g, b: (g, b, 0, 0, 0), **mode)


def _ssd_scan_fwd(xc, proj, dtb, alog, d_exp, gain, hook=None):
    def body(x_ref, b_ref, c_ref, z_ref, dt_ref, dtb_ref, al_ref, de_ref, g_ref, o_ref, y_s, st_ref, dt_s, da_s):
        group = pl.program_id(0)
        dt, da = _ssd_dt_fn(dt_ref[...], dtb_ref[...], al_ref[...])
        dt_s[...] = dt
        da_s[...] = da
        for direction in (0, 1):
            def step(i, prev, direction=direction):
                c = i if direction == 0 else N_CHUNK - 1 - i
                rows = _chunk_rows(c)
                st_ref[direction * N_CHUNK + c, :, 0:128] = prev[0]
                st_ref[direction * N_CHUNK + c, :, 128:256] = prev[1]
                y0, y1, nxt0, nxt1 = _ssd_chunk_fn(direction, group, x_ref[rows, 0:128], x_ref[rows, 128:256], b_ref[rows, :], c_ref[rows, :],
                                                   dt_s[rows, :], da_s[rows, :], prev[0], prev[1])
                if direction == 0:
                    y_s[rows, 0:128] = y0
                    y_s[rows, 128:256] = y1
                else:
                    y_s[rows, 0:128] += y0
                    y_s[rows, 128:256] += y1
                return nxt0, nxt1

            _scan_loop(step, (jnp.zeros((128, 128), F32), jnp.zeros((128, 128), F32)), FWD_SCAN_UNROLL)

        def post(c, carry):
            rows = _chunk_rows(c)
            o_ref[rows, :] = _ssd_post_fn(y_s[rows, :], x_ref[rows, :], z_ref[rows, :], de_ref[...], g_ref[...]).astype(BF16)
            return carry

        lax.fori_loop(0, N_CHUNK, post, 0)

    return _hooked_call(
        body, grid=(2, B_LOC), in_specs=_ssd_scan_specs(),
        out_specs=[pl.BlockSpec((SEQ, GROUP_W), lambda g, b: (b, g)), pl.BlockSpec((SEQ, GROUP_W), lambda g, b: (b, g)), _ssd_state_spec()],
        out_shape=[S_((T, SSD_INNER), BF16), S_((T, SSD_INNER), F32), S_((2, B_LOC, 2 * N_CHUNK, 128, GROUP_W), F32)],
        scratch_shapes=[pltpu.VMEM((SEQ, 128), F32), pltpu.VMEM((SEQ, 128), F32)],
        args=(xc, xc, xc, proj, proj, dtb, alog, d_exp, gain), hook=hook, name="ssd_scan_fwd")


def _ssd_scan_bwd(xc, proj, dtb, alog, d_exp, gain, dy, ysum, states, hook=None):
    def body(x_ref, b_ref, c_ref, z_ref, dt_ref, dtb_ref, al_ref, de_ref, g_ref, dy_ref, ys_ref, st_s,
             dx_ref, db_ref, dc_ref, dz_ref, ddt_ref, ddtb_ref, dal_ref, dde_ref, dg_ref,
             dt_s, da_s, y_s, ddt_s, dda_s):
        group = pl.program_id(0)
        first = pl.program_id(1) == 0
        (dt, da), dt_vjp = jax.vjp(_ssd_dt_fn, dt_ref[...], dtb_ref[...], al_ref[...])
        dt_s[...] = dt
        da_s[...] = da

        def post(c, carry):
            rows = _chunk_rows(c)
            _, post_vjp = jax.vjp(_ssd_post_fn, ys_ref[rows, :], x_ref[rows, :], z_ref[rows, :], de_ref[...], g_ref[...])
            d_y, d_x_skip, d_z, g_de, g_g = post_vjp(dy_ref[rows, :])
            dz_ref[rows, :] = d_z.astype(BF16)
            dx_ref[rows, :] = d_x_skip
            y_s[rows, :] = d_y
            return carry[0] + g_de, carry[1] + g_g

        d_de, d_g = lax.fori_loop(0, N_CHUNK, post, (jnp.zeros((1, GROUP_W), F32), jnp.zeros((1, GROUP_W), F32)))
        db_ref[...] = jnp.zeros((SEQ, 128), F32)
        dc_ref[...] = jnp.zeros((SEQ, 128), F32)
        ddt_s[...] = jnp.zeros((SEQ, 128), F32)
        dda_s[...] = jnp.zeros((SEQ, 128), F32)
        for direction in (0, 1):
            def bstep(i, dnxt, direction=direction):
                c = N_CHUNK - 1 - i if direction == 0 else i
                rows = _chunk_rows(c)
                fn = functools.partial(_ssd_chunk_fn, direction, group)
                _, vjp = jax.vjp(fn, x_ref[rows, 0:128], x_ref[rows, 128:256], b_ref[rows, :], c_ref[rows, :], dt_s[rows, :], da_s[rows, :],
                                 st_s[direction * N_CHUNK + c, :, 0:128], st_s[direction * N_CHUNK + c, :, 128:256])
                g_x0, g_x1, g_b, g_c, g_dt, g_da, g_prev0, g_prev1 = vjp((y_s[rows, 0:128], y_s[rows, 128:256], dnxt[0], dnxt[1]))
                dx_ref[rows, 0:128] += g_x0
                dx_ref[rows, 128:256] += g_x1
                db_ref[rows, :] += g_b
                dc_ref[rows, :] += g_c
                ddt_s[rows, :] += g_dt
                dda_s[rows, :] += g_da
                return g_prev0, g_prev1

            _scan_loop(bstep, (jnp.zeros((128, 128), F32), jnp.zeros((128, 128), F32)), BWD_SCAN_UNROLL)
        g_raw, g_bias, g_alog = dt_vjp((ddt_s[...], dda_s[...]))
        ddt_ref[...] = g_raw
        pad7 = jnp.zeros((7, 128), F32)
        new_b = jnp.concatenate([g_bias, pad7], axis=0)
        new_a = jnp.concatenate([g_alog, pad7], axis=0)

        @pl.when(first)
        def _():
            ddtb_ref[...] = new_b
            dal_ref[...] = new_a
            dde_ref[...] = d_de
            dg_ref[...] = d_g

        @pl.when(jnp.logical_not(first))
        def _():
            ddtb_ref[...] += new_b
            dal_ref[...] += new_a
            dde_ref[...] += d_de
            dg_ref[...] += d_g

    return _hooked_call(
        body, grid=(2, B_LOC),
        in_specs=_ssd_scan_specs(**ONE_BUFFER) + [pl.BlockSpec((SEQ, GROUP_W), lambda g, b: (b, g), **ONE_BUFFER),
                                                  pl.BlockSpec((SEQ, GROUP_W), lambda g, b: (b, g), **ONE_BUFFER),
                                                  _ssd_state_spec(**ONE_BUFFER)],
        out_specs=[pl.BlockSpec((SEQ, GROUP_W), lambda g, b: (b, g)),
                   pl.BlockSpec((SEQ, 128), lambda g, b: (b, g)),
                   pl.BlockSpec((SEQ, 128), lambda g, b: (b, g)),
                   pl.BlockSpec((SEQ, GROUP_W), lambda g, b: (b, g)),
                   pl.BlockSpec((None, SEQ, 128), lambda g, b: (g, b, 0)),
                   pl.BlockSpec((None, 8, 128), lambda g, b: (g, 0, 0)),
                   pl.BlockSpec((None, 8, 128), lambda g, b: (g, 0, 0)),
                   pl.BlockSpec((1, GROUP_W), lambda g, b: (0, g)),
                   pl.BlockSpec((1, GROUP_W), lambda g, b: (0, g))],
        out_shape=[S_((T, SSD_INNER), F32), S_((T, 256), F32), S_((T, 256), F32), S_((T, SSD_INNER), BF16),
                   S_((2, T, 128), F32), S_((2, 8, 128), F32), S_((2, 8, 128), F32),
                   S_((1, SSD_INNER), F32), S_((1, SSD_INNER), F32)],
        scratch_shapes=[pltpu.VMEM((SEQ, 128), F32), pltpu.VMEM((SEQ, 128), F32), pltpu.VMEM((SEQ, GROUP_W), F32),
                        pltpu.VMEM((SEQ, 128), F32), pltpu.VMEM((SEQ, 128), F32)],
        args=(xc, xc, xc, proj, proj, dtb, alog, d_exp, gain, dy, ysum, states), hook=hook, name="ssd_scan_bwd")


GMLP_W = 256


def _gmlp_chunk_fn(gu, gv, v_gain, w0, w1, w2, w3, b_exp):
    u = jax.nn.gelu(gu)
    v = jax.nn.gelu(gv)
    v = v * lax.rsqrt(jnp.mean(v * v, axis=-1, keepdims=True) + NORM_EPS) * v_gain
    col = lax.broadcasted_iota(jnp.int32, (1, GMLP_W), 1) // HEAD
    mixed = b_exp
    for g, w in enumerate((w0, w1, w2, w3)):
        mixed = mixed + (col == g).astype(F32) * _dot(w, v)
    return u * mixed


def _gmlp_specs():
    return [pl.BlockSpec((SEQ, GMLP_W), lambda b: (b, C_GU // GMLP_W)),
            pl.BlockSpec((SEQ, GMLP_W), lambda b: (b, C_GV // GMLP_W)),
            pl.BlockSpec((1, GMLP_W), lambda b: (0, 0)),
            pl.BlockSpec((4, CHUNK, CHUNK), lambda b: (0, 0, 0)),
            pl.BlockSpec((CHUNK, GMLP_W), lambda b: (0, 0))]


def _gmlp_fwd(proj, v_gain, w_s, b_exp):
    def body(u_ref, v_ref, g_ref, w_ref, b_ref, o_ref):
        def step(c, carry):
            rows = _chunk_rows(c)
            o_ref[rows, :] = _gmlp_chunk_fn(u_ref[rows, :], v_ref[rows, :], g_ref[...], w_ref[0], w_ref[1], w_ref[2], w_ref[3],
                                            b_ref[...]).astype(BF16)
            return carry

        lax.fori_loop(0, N_CHUNK, step, 0)

    return pl.pallas_call(body, grid=(B_LOC,), in_specs=_gmlp_specs(),
                          out_specs=pl.BlockSpec((SEQ, GMLP_W), lambda b: (b, 0)),
                          out_shape=S_((T, GMLP_W), BF16), name="gmlp_fwd")(proj, proj, v_gain, w_s, b_exp)


def _gmlp_bwd(proj, v_gain, w_s, b_exp, dy):
    def body(u_ref, v_ref, g_ref, w_ref, b_ref, dy_ref, du_ref, dv_ref, dg_ref, dw_ref, db_ref):
        @pl.when(pl.program_id(0) == 0)
        def _():
            dg_ref[...] = jnp.zeros_like(dg_ref)
            dw_ref[...] = jnp.zeros_like(dw_ref)
            db_ref[...] = jnp.zeros_like(db_ref)

        def step(c, carry):
            rows = _chunk_rows(c)
            _, vjp = jax.vjp(_gmlp_chunk_fn, u_ref[rows, :], v_ref[rows, :], g_ref[...], w_ref[0], w_ref[1], w_ref[2], w_ref[3], b_ref[...])
            g_u, g_v, g_g, g_w0, g_w1, g_w2, g_w3, g_b = vjp(dy_ref[rows, :])
            du_ref[rows, :] = g_u.astype(BF16)
            dv_ref[rows, :] = g_v.astype(BF16)
            dg_ref[...] += g_g
            db_ref[...] += g_b
            for g, gw in enumerate((g_w0, g_w1, g_w2, g_w3)):
                dw_ref[g] += gw
            return carry

        lax.fori_loop(0, N_CHUNK, step, 0)

    blk = pl.BlockSpec((SEQ, GMLP_W), lambda b: (b, 0))
    return pl.pallas_call(
        body, grid=(B_LOC,),
        in_specs=_gmlp_specs() + [pl.BlockSpec((SEQ, GMLP_W), lambda b: (b, SSD_INNER // GMLP_W))],
        out_specs=[blk, blk, pl.BlockSpec((1, GMLP_W), lambda b: (0, 0)),
                   pl.BlockSpec((4, CHUNK, CHUNK), lambda b: (0, 0, 0)), pl.BlockSpec((CHUNK, GMLP_W), lambda b: (0, 0))],
        out_shape=[S_((T, GMLP_W), BF16), S_((T, GMLP_W), BF16), S_((1, GMLP_W), F32),
                   S_((4, CHUNK, CHUNK), F32), S_((CHUNK, GMLP_W), F32)],
        name="gmlp_bwd")(proj, proj, v_gain, w_s, b_exp, dy)


PAIR_W = 128
QB = 128
KW = QB + 2 * ATTN_HALF
N_QB = SEQ // QB
FWD_BLOCK_UNROLL = 4
BWD_BLOCK_UNROLL = 4
PAD_ROWS = SEQ + 2 * ATTN_HALF


def _qk_norm_fn(x, gain):
    ms = jnp.dot(x * x, _head_sum_matrix(PAIR_W), precision=SUM_PRECISION, preferred_element_type=F32) * (1.0 / HEAD)
    return x * lax.rsqrt(ms + NORM_EPS) * gain


def _deinterleave(dst_ref, src_ref, dil, offset):
    length = SEQ // dil
    if dil == 1:
        dst_ref[pl.ds(offset, SEQ), :] = src_ref[...]
        return
    for r in range(dil):
        dst_ref[pl.ds(offset + r * length, length), :] = src_ref[pl.ds(r, length, stride=dil), :]


def _interleave(dst_ref, src_ref, dil, offset):
    length = SEQ // dil
    if dil == 1:
        dst_ref[...] = src_ref[pl.ds(offset, SEQ), :]
        return
    for r in range(dil):
        dst_ref[pl.ds(r, length, stride=dil), :] = src_ref[pl.ds(offset + r * length, length), :]


def _edge_mask(blk, dil):
    length = SEQ // dil
    qi = blk * QB + lax.broadcasted_iota(jnp.int32, (QB, KW), 0)
    kj = blk * QB - ATTN_HALF + lax.broadcasted_iota(jnp.int32, (QB, KW), 1)
    return (kj >= 0) & (kj < SEQ) & ((qi // length) == (kj // length))


def _lane_is_head(hh):
    return (lax.broadcasted_iota(jnp.int32, (1, PAIR_W), 1) // HEAD) == hh


def _dilate_qkv(dil, qn_s, kn_s, v_ref, qd_s, kd_s, vd_s):
    _deinterleave(qd_s, qn_s, dil, 0)
    _deinterleave(kd_s, kn_s, dil, ATTN_HALF)
    _deinterleave(vd_s, v_ref, dil, ATTN_HALF)


def _attn_branch_fwd(br, dil, qn_s, kn_s, v_ref, bias_ref, qd_s, kd_s, vd_s, od_s, ld_s):
    _dilate_qkv(dil, qn_s, kn_s, v_ref, qd_s, kd_s, vd_s)

    def step(blk, carry):
        rows = pl.ds(pl.multiple_of(blk * QB, QB), QB)
        win = pl.ds(pl.multiple_of(blk * QB, QB), KW)
        qb, kw, vw = qd_s[rows, :], kd_s[win, :], vd_s[win, :]
        edge = _edge_mask(blk, dil)
        out, lse = 0.0, 0.0
        for hh in range(2):
            is_h = _lane_is_head(hh)
            s = _dot_nt(jnp.where(is_h, qb, 0.0), kw) * (HEAD ** -0.5) + bias_ref[br, hh]
            s = jnp.where(edge, s, NEG_INF)
            m = jnp.max(s, axis=-1, keepdims=True)
            l_h = m + jnp.log(jnp.sum(jnp.exp(s - m), axis=-1, keepdims=True))
            out = out + jnp.where(is_h, _dot(jnp.exp(s - l_h), vw), 0.0)
            lse = lse + jnp.where(is_h, l_h, 0.0)
        od_s[rows, :] = out
        ld_s[rows, :] = lse
        return carry

    _block_loop(step, FWD_BLOCK_UNROLL)


def _block_loop(step, unroll):
    def body(i, carry):
        for k in range(unroll):
            carry = step(i * unroll + k, carry)
        return carry

    lax.fori_loop(0, N_QB // unroll, body, 0)


def _attn_specs():
    col = lambda c0: (lambda p, b: (b, c0 // PAIR_W + p))
    return [pl.BlockSpec((SEQ, PAIR_W), col(C_Q)), pl.BlockSpec((SEQ, PAIR_W), col(C_K)), pl.BlockSpec((SEQ, PAIR_W), col(C_V)),
            pl.BlockSpec((1, PAIR_W), lambda p, b: (0, 0)), pl.BlockSpec((1, PAIR_W), lambda p, b: (0, 0)),
            pl.BlockSpec((3, 2, QB, KW), lambda p, b: (0, p, 0, 0))]


def _attn_scratch():
    seq = pltpu.VMEM((SEQ, PAIR_W), F32)
    pad = pltpu.VMEM((PAD_ROWS, PAIR_W), F32)
    return [seq, seq, seq, pad, pad, seq, seq]


def _zero_pads(*refs):
    for ref in refs:
        ref[pl.ds(0, ATTN_HALF), :] = jnp.zeros((ATTN_HALF, PAIR_W), F32)
        ref[pl.ds(ATTN_HALF + SEQ, ATTN_HALF), :] = jnp.zeros((ATTN_HALF, PAIR_W), F32)


ROW_STEP = 256


def _row_steps(fn, init=0):
    return lax.fori_loop(0, SEQ // ROW_STEP, lambda i, c: fn(pl.ds(pl.multiple_of(i * ROW_STEP, ROW_STEP), ROW_STEP), c), init)


def _interleave_add(acc_ref, src_ref, dil, offset):
    length = SEQ // dil
    if dil == 1:
        acc_ref[...] += src_ref[pl.ds(offset, SEQ), :]
        return
    for r in range(dil):
        acc_ref[pl.ds(r, length, stride=dil), :] += src_ref[pl.ds(offset + r * length, length), :]


def _attn_norm_qk(q_ref, k_ref, qg_ref, kg_ref, qn_s, kn_s):
    def norm(rows, carry):
        qn_s[rows, :] = _qk_norm_fn(q_ref[rows, :], qg_ref[...])
        kn_s[rows, :] = _qk_norm_fn(k_ref[rows, :], kg_ref[...])
        return carry

    _row_steps(norm)


def _attn_forward_all(q_ref, k_ref, v_ref, qg_ref, kg_ref, bias_ref, qn_s, kn_s, qd_s, kd_s, vd_s, od_s, ld_s, on_s, ln_s):
    _attn_norm_qk(q_ref, k_ref, qg_ref, kg_ref, qn_s, kn_s)
    _zero_pads(kd_s, vd_s)
    for br, dil in enumerate(ATTN_DILS):
        _attn_branch_fwd(br, dil, qn_s, kn_s, v_ref, bias_ref, qd_s, kd_s, vd_s, od_s, ld_s)
        _interleave(on_s.at[br], od_s, dil, 0)
        _interleave(ln_s.at[br], ld_s, dil, 0)


def _merge_weights(ln_s, rows):
    l0, l1, l2 = ln_s[0, rows, :], ln_s[1, rows, :], ln_s[2, rows, :]
    m = jnp.maximum(jnp.maximum(l0, l1), l2)
    e = [jnp.exp(l0 - m), jnp.exp(l1 - m), jnp.exp(l2 - m)]
    den = e[0] + e[1] + e[2]
    return [e[0] / den, e[1] / den, e[2] / den]


def _attn_fwd(proj, q_gain, k_gain, bias, hook=None):
    def body(q_ref, k_ref, v_ref, qg_ref, kg_ref, bias_ref, o_ref, on_s, ln_s, qn_s, kn_s, qd_s, kd_s, vd_s, od_s, ld_s):
        _attn_forward_all(q_ref, k_ref, v_ref, qg_ref, kg_ref, bias_ref, qn_s, kn_s, qd_s, kd_s, vd_s, od_s, ld_s, on_s, ln_s)

        def merge(rows, carry):
            w = _merge_weights(ln_s, rows)
            o_ref[rows, :] = (w[0] * on_s[0, rows, :] + w[1] * on_s[1, rows, :] + w[2] * on_s[2, rows, :]).astype(BF16)
            return carry

        _row_steps(merge)

    kept = pl.BlockSpec((3, SEQ, PAIR_W), lambda p, b: (0, b, p))
    return _hooked_call(body, grid=(2, B_LOC), in_specs=_attn_specs(),
                        out_specs=[pl.BlockSpec((SEQ, PAIR_W), lambda p, b: (b, p)), kept, kept],
                        out_shape=[S_((T, 2 * PAIR_W), BF16), S_((3, T, 2 * PAIR_W), F32), S_((3, T, 2 * PAIR_W), F32)],
                        scratch_shapes=_attn_scratch(), args=(proj, proj, proj, q_gain, k_gain, bias), hook=hook, name="attn_fwd")


def _attn_bwd(proj, q_gain, k_gain, bias, dy, kept_o, kept_l, hook=None):
    def body(q_ref, k_ref, v_ref, qg_ref, kg_ref, bias_ref, dy_ref, on_ref, ln_ref,
             dq_ref, dk_ref, dv_ref, dqg_ref, dkg_ref, dbias_ref,
             qn_s, kn_s, qd_s, kd_s, vd_s, od_s, ld_s, don_s, dln_s, dod_s, dld_s, dqd_s, dkd_s, dvd_s, dqn_s, dkn_s, dvn_s):
        first = pl.program_id(1) == 0
        _attn_norm_qk(q_ref, k_ref, qg_ref, kg_ref, qn_s, kn_s)
        _zero_pads(kd_s, vd_s)

        def clear_acc(rows, carry):
            dqn_s[rows, :] = jnp.zeros((ROW_STEP, PAIR_W), F32)
            dkn_s[rows, :] = jnp.zeros((ROW_STEP, PAIR_W), F32)
            dvn_s[rows, :] = jnp.zeros((ROW_STEP, PAIR_W), F32)
            return carry

        _row_steps(clear_acc)

        @pl.when(first)
        def _():
            dbias_ref[...] = jnp.zeros_like(dbias_ref)

        def merge_bwd(rows, carry):
            w = _merge_weights(ln_ref, rows)
            dy = dy_ref[rows, :]
            same_head = _head_sum_matrix(PAIR_W)
            dws = [jnp.dot(dy * on_ref[j, rows, :], same_head, precision=SUM_PRECISION, preferred_element_type=F32) for j in range(3)]
            dbar = w[0] * dws[0] + w[1] * dws[1] + w[2] * dws[2]
            for j in range(3):
                don_s[j, rows, :] = w[j] * dy
                dln_s[j, rows, :] = w[j] * (dws[j] - dbar)
            return carry

        _row_steps(merge_bwd)
        for br, dil in enumerate(ATTN_DILS):
            _dilate_qkv(dil, qn_s, kn_s, v_ref, qd_s, kd_s, vd_s)
            _deinterleave(od_s, on_ref.at[br], dil, 0)
            _deinterleave(ld_s, ln_ref.at[br], dil, 0)
            _deinterleave(dod_s, don_s.at[br], dil, 0)
            _deinterleave(dld_s, dln_s.at[br], dil, 0)

            def clear(rows, carry):
                dkd_s[rows, :] = jnp.zeros((ROW_STEP, PAIR_W), F32)
                dvd_s[rows, :] = jnp.zeros((ROW_STEP, PAIR_W), F32)
                return carry

            _row_steps(clear)
            tail = pl.ds(SEQ, 2 * ATTN_HALF)
            dkd_s[tail, :] = jnp.zeros((2 * ATTN_HALF, PAIR_W), F32)
            dvd_s[tail, :] = jnp.zeros((2 * ATTN_HALF, PAIR_W), F32)

            def step(blk, carry, br=br, dil=dil):
                rows = pl.ds(pl.multiple_of(blk * QB, QB), QB)
                win = pl.ds(pl.multiple_of(blk * QB, QB), KW)
                qb, kw, vw = qd_s[rows, :], kd_s[win, :], vd_s[win, :]
                do_b, dl_b, o_b, l_b = dod_s[rows, :], dld_s[rows, :], od_s[rows, :], ld_s[rows, :]
                edge = _edge_mask(blk, dil)
                dq, dk, dv = 0.0, 0.0, 0.0
                for hh in range(2):
                    is_h = _lane_is_head(hh)
                    pick = (lax.broadcasted_iota(jnp.int32, (1, PAIR_W), 1) == hh * HEAD).astype(F32)
                    q_h = jnp.where(is_h, qb, 0.0)
                    do_h = jnp.where(is_h, do_b, 0.0)
                    s = _dot_nt(q_h, kw) * (HEAD ** -0.5) + bias_ref[br, hh]
                    s = jnp.where(edge, s, NEG_INF)
                    p = jnp.exp(s - jnp.sum(l_b * pick, axis=-1, keepdims=True))
                    dp = _dot_nt(do_h, vw)
                    delta = jnp.sum(do_h * o_b, axis=-1, keepdims=True)
                    ds = p * (dp - delta + jnp.sum(dl_b * pick, axis=-1, keepdims=True))
                    dbias_ref[br, hh] += ds
                    dq = dq + jnp.where(is_h, _dot(ds, kw), 0.0) * (HEAD ** -0.5)
                    dk = dk + _dot_tn(ds, q_h) * (HEAD ** -0.5)
                    dv = dv + _dot_tn(p, do_h)
                dqd_s[rows, :] = dq
                dkd_s[win, :] += dk
                dvd_s[win, :] += dv
                return carry

            _block_loop(step, BWD_BLOCK_UNROLL)
            _interleave_add(dqn_s, dqd_s, dil, 0)
            _interleave_add(dkn_s, dkd_s, dil, ATTN_HALF)
            _interleave_add(dvn_s, dvd_s, dil, ATTN_HALF)

        def norm_bwd(rows, carry):
            _, q_vjp = jax.vjp(_qk_norm_fn, q_ref[rows, :], qg_ref[...])
            _, k_vjp = jax.vjp(_qk_norm_fn, k_ref[rows, :], kg_ref[...])
            g_q, g_qg = q_vjp(dqn_s[rows, :])
            g_k, g_kg = k_vjp(dkn_s[rows, :])
            dq_ref[rows, :] = g_q.astype(BF16)
            dk_ref[rows, :] = g_k.astype(BF16)
            dv_ref[rows, :] = dvn_s[rows, :].astype(BF16)
            return carry[0] + g_qg, carry[1] + g_kg

        g_qg, g_kg = _row_steps(norm_bwd, (jnp.zeros((1, PAIR_W), F32), jnp.zeros((1, PAIR_W), F32)))
        pad7 = jnp.zeros((7, PAIR_W), F32)
        new_q = jnp.concatenate([g_qg, pad7], axis=0)
        new_k = jnp.concatenate([g_kg, pad7], axis=0)

        @pl.when(first)
        def _():
            dqg_ref[...] = new_q
            dkg_ref[...] = new_k

        @pl.when(jnp.logical_not(first))
        def _():
            dqg_ref[...] += new_q
            dkg_ref[...] += new_k

    seq = pltpu.VMEM((SEQ, PAIR_W), F32)
    seq3 = pltpu.VMEM((3, SEQ, PAIR_W), F32)
    pad = pltpu.VMEM((PAD_ROWS, PAIR_W), F32)
    kept = pl.BlockSpec((3, SEQ, PAIR_W), lambda p, b: (0, b, p))
    out_blk = pl.BlockSpec((SEQ, PAIR_W), lambda p, b: (b, p))
    gain_blk = pl.BlockSpec((None, 8, PAIR_W), lambda p, b: (p, 0, 0))
    return _hooked_call(
        body, grid=(2, B_LOC),
        in_specs=_attn_specs() + [pl.BlockSpec((SEQ, PAIR_W), lambda p, b: (b, (SSD_INNER + GMLP_W) // PAIR_W + p)), kept, kept],
        out_specs=[out_blk, out_blk, out_blk, gain_blk, gain_blk, pl.BlockSpec((3, 2, QB, KW), lambda p, b: (0, p, 0, 0))],
        out_shape=[S_((T, 2 * PAIR_W), BF16)] * 3 + [S_((2, 8, PAIR_W), F32)] * 2 + [S_((3, 4, QB, KW), F32)],
        scratch_shapes=_attn_scratch() + [seq3, seq3, seq, seq, seq, pad, pad, seq, seq, seq],
        args=(proj, proj, proj, q_gain, k_gain, bias, dy, kept_o, kept_l), hook=hook, name="attn_bwd")


def _rel_bucket(rel):
    nb = 16
    max_exact = nb // 2
    n = jnp.abs(rel)
    large = max_exact + (jnp.log(jnp.maximum(n, 1).astype(F32) / max_exact) / math.log(1024 / max_exact) * (nb - max_exact)).astype(jnp.int32)
    large = jnp.minimum(large, nb - 1)
    return jnp.where(rel > 0, nb, 0) + jnp.where(n < max_exact, n, large)


def _attn_bias(rel_table):
    rel = jnp.arange(KW)[None, :] - ATTN_HALF - jnp.arange(QB)[:, None]
    inside = (jnp.abs(rel) <= ATTN_HALF)
    out = []
    for dil in ATTN_DILS:
        one_hot = (_rel_bucket(rel * dil)[None] == jnp.arange(32)[:, None, None]).astype(F32)
        b = jnp.einsum("kh,kts->hts", rel_table, one_hot, precision=HI)
        out.append(jnp.where(inside[None], b, NEG_INF))
    return jnp.stack(out).astype(F32)


def _place():
    return lax.axis_index("x"), lax.axis_index("y"), lax.axis_index("c")


def _allgather8(buf, name):
    rows = buf.shape[0]
    flips = [(fx, fy, fc) for fx in (0, 1) for fy in (0, 1) for fc in (0, 1)][1:]

    def body(in_ref, out_ref, send_sems, recv_sems):
        x, y, c = _place()
        me = 4 * x + 2 * y + c
        peers = [(1 - x if fx else x, 1 - y if fy else y, 1 - c if fc else c) for fx, fy, fc in flips]

        def copy(k, slot, peer):
            return pltpu.make_async_remote_copy(src_ref=in_ref, dst_ref=out_ref.at[slot], send_sem=send_sems.at[k],
                                                recv_sem=recv_sems.at[k], device_id=peer, device_id_type=MESH)

        sends = [copy(k, me, peer) for k, peer in enumerate(peers)]
        for cp in sends:
            cp.start()
        for k, (px, py, pc) in enumerate(peers):
            copy(k, 4 * px + 2 * py + pc, (px, py, pc)).wait_recv()
        for cp in sends:
            cp.wait_send()

    slots = pl.pallas_call(body, in_specs=[ANY], out_specs=ANY, out_shape=S_((N_DEV, rows, 128), F32),
                           scratch_shapes=[pltpu.SemaphoreType.DMA((7,)), pltpu.SemaphoreType.DMA((7,))], name=name)(buf)
    x, y, c = _place()
    return lax.dynamic_update_index_in_dim(slots, buf, 4 * x + 2 * y + c, axis=0)


N_BIG = 4


def _other_chips(x, y):
    return [(1 - x, y), (x, 1 - y), (1 - x, 1 - y)]


def _hooked_call(body, *, grid, in_specs, out_specs, out_shape, scratch_shapes, args, hook, name):
    if hook is None:
        res = pl.pallas_call(body, grid=grid, in_specs=in_specs, out_specs=out_specs, out_shape=out_shape,
                             scratch_shapes=scratch_shapes, compiler_params=_cp(), name=name)(*args)
        return res, None
    counts = (len(in_specs), len(hook["arrays"]), len(out_specs), len(hook["out_shape"]), len(scratch_shapes), len(hook["sems"]))

    def wrapped(*refs):
        groups, pos = [], 0
        for n in counts:
            groups.append(refs[pos:pos + n])
            pos += n
        ins, h_ins, outs, h_outs, scr, sems = groups
        idx = [pl.program_id(a) for a in range(len(grid))]
        first = functools.reduce(jnp.logical_and, [i == 0 for i in idx])
        last = functools.reduce(jnp.logical_and, [i == g - 1 for i, g in zip(idx, grid)])

        @pl.when(first)
        def _():
            hook["start"](h_ins, h_outs, sems)

        body(*ins, *outs, *scr)

        @pl.when(last)
        def _():
            hook["finish"](h_ins, h_outs, sems)

    res = pl.pallas_call(wrapped, grid=grid, in_specs=list(in_specs) + [ANY] * counts[1], out_specs=list(out_specs) + [ANY] * counts[3],
                         out_shape=list(out_shape) + list(hook["out_shape"]), scratch_shapes=list(scratch_shapes) + list(hook["sems"]),
                         compiler_params=_cp(), name=name + "_" + hook["name"])(*args, *hook["arrays"])
    return res[:counts[2]], res[counts[2]:]


def _run_hook(hook):
    n_in, n_out = len(hook["arrays"]), len(hook["out_shape"])

    def body(*refs):
        h_ins, h_outs, sems = refs[:n_in], refs[n_in:n_in + n_out], refs[n_in + n_out:]
        hook["start"](h_ins, h_outs, sems)
        hook["finish"](h_ins, h_outs, sems)

    return pl.pallas_call(body, in_specs=[ANY] * n_in, out_specs=[ANY] * n_out, out_shape=list(hook["out_shape"]),
                          scratch_shapes=list(hook["sems"]), name=hook["name"])(*hook["arrays"])


def _remote(src, dst, send_sem, recv_sem, peer):
    return pltpu.make_async_remote_copy(src_ref=src, dst_ref=dst, send_sem=send_sem, recv_sem=recv_sem, device_id=peer, device_id_type=MESH)


def _gather_hook(shards):
    def copies(h_ins, h_outs, sems, kind):
        ici_send, ici_recv, d2d_send, d2d_recv = sems
        x, y, c = _place()
        chip = 2 * x + y
        out = []
        for t in range(len(shards)):
            half = shards[t].shape[0] // 2
            mine_r, other_r = pl.ds(c * half, half), pl.ds((1 - c) * half, half)
            for f, (px, py) in enumerate(_other_chips(x, y)):
                k, peer_chip = 3 * t + f, 2 * px + py
                if kind in ("send", "land"):
                    slot = chip if kind == "send" else peer_chip
                    out.append(_remote(h_ins[t].at[mine_r], h_outs[t].at[slot, mine_r], ici_send.at[k], ici_recv.at[k], (px, py, c)))
                else:
                    rows = mine_r if kind == "pass" else other_r
                    out.append(_remote(h_outs[t].at[peer_chip, rows], h_outs[t].at[peer_chip, rows], d2d_send.at[k], d2d_recv.at[k],
                                       (x, y, 1 - c)))
        return out

    def start(h_ins, h_outs, sems):
        for cp in copies(h_ins, h_outs, sems, "send"):
            cp.start()

    def finish(h_ins, h_outs, sems):
        passed = copies(h_ins, h_outs, sems, "pass")
        for landed, forward in zip(copies(h_ins, h_outs, sems, "land"), passed):
            landed.wait_recv()
            forward.start()
        for cp in copies(h_ins, h_outs, sems, "get"):
            cp.wait_recv()
        for cp in copies(h_ins, h_outs, sems, "send") + passed:
            cp.wait_send()

    return dict(name="gather", arrays=list(shards), out_shape=[S_((N_CHIPS,) + s.shape, s.dtype) for s in shards],
                sems=[pltpu.SemaphoreType.DMA((3 * len(shards),)) for _ in range(4)], start=start, finish=finish)


def _to_sibling_hook(parts, half_rows=False):
    def copies(h_ins, h_outs, sems):
        x, y, c = _place()
        out = []
        for t, p in enumerate(parts):
            src = h_ins[t].at[:, pl.ds((1 - c) * (p.shape[1] // 2), p.shape[1] // 2)] if half_rows else h_ins[t]
            out.append(_remote(src, h_outs[t], sems[0].at[t], sems[1].at[t], (x, y, 1 - c)))
        return out

    def start(h_ins, h_outs, sems):
        for cp in copies(h_ins, h_outs, sems):
            cp.start()

    def finish(h_ins, h_outs, sems):
        cps = copies(h_ins, h_outs, sems)
        for cp in cps:
            cp.wait_recv()
        for cp in cps:
            cp.wait_send()

    shapes = [(p.shape[0], p.shape[1] // 2, p.shape[2]) if half_rows else p.shape for p in parts]
    return dict(name="to_sibling", arrays=list(parts), out_shape=[S_(s, p.dtype) for s, p in zip(shapes, parts)],
                sems=[pltpu.SemaphoreType.DMA((len(parts),)), pltpu.SemaphoreType.DMA((len(parts),))], start=start, finish=finish)


def _to_chips_hook(parts):
    def copies(h_ins, h_outs, sems):
        x, y, c = _place()
        return [_remote(h_ins[t].at[2 * px + py], h_outs[t].at[f], sems[0].at[3 * t + f], sems[1].at[3 * t + f], (px, py, c))
                for t in range(len(parts)) for f, (px, py) in enumerate(_other_chips(x, y))]

    def start(h_ins, h_outs, sems):
        for cp in copies(h_ins, h_outs, sems):
            cp.start()

    def finish(h_ins, h_outs, sems):
        cps = copies(h_ins, h_outs, sems)
        for cp in cps:
            cp.wait_recv()
        for cp in cps:
            cp.wait_send()

    return dict(name="to_chips", arrays=list(parts), out_shape=[S_((3,) + p.shape[1:], p.dtype) for p in parts],
                sems=[pltpu.SemaphoreType.DMA((3 * len(parts),)), pltpu.SemaphoreType.DMA((3 * len(parts),))], start=start, finish=finish)


def _add_pair(a, b, core, name):
    n, half, c = b.shape

    def body(core_ref, a_ref, b_ref, o_ref):
        o_ref[...] = (a_ref[...].astype(F32) + b_ref[...].astype(F32)).astype(BF16)

    spec = pltpu.PrefetchScalarGridSpec(
        num_scalar_prefetch=1, grid=(n,),
        in_specs=[pl.BlockSpec((None, half, c), lambda i, core_ref: (i, core_ref[0], 0)),
                  pl.BlockSpec((None, half, c), lambda i, core_ref: (i, 0, 0))],
        out_specs=pl.BlockSpec((None, half, c), lambda i, core_ref: (i, 0, 0)))
    return pl.pallas_call(body, grid_spec=spec, out_shape=S_(b.shape, BF16), name=name)(core.reshape(1).astype(jnp.int32), a, b)


def _add_four(own, got, rows, name):
    n, r, c = own.shape

    def body(a_ref, g_ref, o_ref):
        o_ref[...] = ((a_ref[...].astype(F32) + g_ref[0].astype(F32)) + g_ref[1].astype(F32)) + g_ref[2].astype(F32)

    blk = pl.BlockSpec((None, rows, c), lambda i, j: (i, j, 0))
    return pl.pallas_call(body, grid=(n, r // rows), in_specs=[blk, pl.BlockSpec((3, None, rows, c), lambda i, j: (0, i, j, 0))],
                          out_specs=blk, out_shape=S_(own.shape, F32), name=name)(own, got)


def _sum_slots(slots):
    n, rows = slots.shape[:2]

    def body(s_ref, o_ref):
        tot = s_ref[0]
        for k in range(1, n):
            tot = tot + s_ref[k]
        o_ref[...] = tot

    return pl.pallas_call(body, out_shape=S_((rows, 128), F32), name="sum_slots")(slots)


def _add_two(a, b):
    def body(a_ref, b_ref, o_ref):
        o_ref[...] = a_ref[...] + b_ref[...]

    return pl.pallas_call(body, out_shape=S_(a.shape, a.dtype), name="add_two")(a, b)


def _allreduce_small(buf, chip):
    (theirs,) = _run_hook(dict(_to_sibling_hook([buf]), name="small_to_sibling"))
    pair = _add_two(buf, theirs)
    (slots,) = _run_hook(dict(_gather_hook([pair]), name="small_gather"))
    return _sum_slots(lax.dynamic_update_index_in_dim(slots, pair, chip, axis=0))


SMALL = ("mix_norm_gain", "ssd_conv_w", "ssd_conv_b", "ssd_dt_bias", "ssd_a_log", "ssd_d", "ssd_out_gain", "gmlp_v_gain",
         "gmlp_w_s", "gmlp_b_s", "attn_q_gain", "attn_k_gain", "rel_bias_table", "ffn_norm_gain", "ffn_conv_w", "ffn_conv_b")
BIG = ("w_in", "w_out", "ffn_w_up", "ffn_w_down")
WEIGHTS = ("mix_norm_gain", "w_in", "ssd_conv_w", "ssd_conv_b", "ssd_dt_bias", "ssd_a_log", "ssd_d", "ssd_out_gain", "gmlp_v_gain",
           "gmlp_w_s", "gmlp_b_s", "attn_q_gain", "attn_k_gain", "rel_bias_table", "w_out", "ffn_norm_gain", "ffn_w_up",
           "ffn_conv_w", "ffn_conv_b", "ffn_w_down")
ADAM_ROWS = {"w_in": 512, "w_out": 128, "ffn_w_up": 256, "ffn_w_down": 352}


PACK_ROWS = 64


def _packed_rows(shape):
    return -(-int(np.prod(shape)) // 1024) * 8


def _pack(arrays):
    parts = []
    for a in arrays:
        rows = _packed_rows(a.shape)
        flat = a.reshape(-1).astype(F32)
        parts.append(jnp.pad(flat, (0, rows * 128 - flat.shape[0])).reshape(rows, 128))
    total = sum(p.shape[0] for p in parts)
    tail = -total % PACK_ROWS
    if tail:
        parts.append(jnp.zeros((tail, 128), F32))
    return jnp.concatenate(parts, axis=0)


def _unpack(buf, shapes):
    out, row = [], 0
    for s in shapes:
        rows, n = _packed_rows(s), int(np.prod(s))
        out.append(buf[row:row + rows].reshape(-1)[:n].reshape(s))
        row += rows
    return out


def _perm_cols(w):
    pad = jnp.zeros(w.shape[:-1] + (NP - IN_WIDTH,), w.dtype)
    return jnp.concatenate([w[..., :1536], w[..., 1552:], w[..., 1536:1552], pad], axis=-1)


def _unperm_cols(w):
    return jnp.concatenate([w[..., :1536], w[..., C_DT:C_DT + 16], w[..., 1536:C_DT]], axis=-1)


def _layer_params(l, p, conv5_w, conv3_w, bias):
    def make(mix_g, conv5, conv5_b, dt_bias, a_log, d_skip, out_gain, v_gain, w_s, b_s, q_gain, k_gain, ffn_g, conv3, conv3_b):
        lanes = lambda a: jnp.pad(a.reshape(1, 16), ((0, 0), (0, 112)))
        cw3 = jnp.pad(jnp.transpose(conv3.reshape(3, 2, FFN_DIM), (1, 0, 2)), ((0, 0), (0, 5), (0, 0)))
        return dict(mix_g=mix_g.reshape(1, D_MODEL), cw5=jnp.pad(conv5, ((0, 3), (0, 0))), cb5=conv5_b.reshape(1, SSD_XBC),
                    dtb=lanes(dt_bias), alog=lanes(a_log), d_exp=jnp.repeat(d_skip, HEAD).reshape(1, SSD_INNER),
                    out_gain=out_gain.reshape(1, SSD_INNER), v_gain=v_gain.reshape(1, GMLP_W), w_s=w_s,
                    b_exp=jnp.repeat(b_s.T, HEAD, axis=1), q_gain=jnp.tile(q_gain, 2).reshape(1, PAIR_W),
                    k_gain=jnp.tile(k_gain, 2).reshape(1, PAIR_W), ffn_g=ffn_g.reshape(1, D_MODEL), cw3=cw3,
                    cb3=conv3_b.reshape(2, 1, FFN_DIM))

    args = (p["mix_norm_gain"][l], conv5_w[l], p["ssd_conv_b"][l], p["ssd_dt_bias"][l], p["ssd_a_log"][l], p["ssd_d"][l],
            p["ssd_out_gain"][l], p["gmlp_v_gain"][l], p["gmlp_w_s"][l], p["gmlp_b_s"][l], p["attn_q_gain"][l], p["attn_k_gain"][l],
            p["ffn_norm_gain"][l], conv3_w[l], p["ffn_conv_b"][l])
    return jax.vjp(make, *args)


def _forward_layer(x, h, lp, w, bias, next_gain=None, hooks=None, resolve=None):
    hooks = hooks or {}
    proj = _mm_nn(h, w["w_in"], tm=1024, tn=1024, tk=1024, out_dtype=F32, name="mm_proj")
    xc = _ssd_pre_fwd(proj, lp["cw5"], lp["cb5"])
    (y_ssd, ssd_sum, ssd_states), got_self = _ssd_scan_fwd(xc, proj, lp["dtb"], lp["alog"], lp["d_exp"], lp["out_gain"],
                                                           hook=hooks.get("self"))
    if got_self is not None:
        w = dict(w, **resolve(got_self))
    y_gmlp = _gmlp_fwd(proj, lp["v_gain"], lp["w_s"], lp["b_exp"])
    (y_attn, attn_o, attn_l), got_attn = _attn_fwd(proj, lp["q_gain"], lp["k_gain"], bias, hook=hooks.get("attn"))
    y = jnp.concatenate([y_ssd, y_gmlp, y_attn], axis=1)
    x2, hn = _mm_nn(y, w["w_out"], tm=1024, tn=1024, tk=1024, out_dtype=F32, res=x, norm_gain=lp["ffn_g"], name="mm_out")
    up3 = _mm_up(hn, w["ffn_w_up"])
    act, got_gate = _convgate_fwd(up3, lp["cw3"], lp["cb3"], hook=hooks.get("gate"))
    if next_gain is None:
        x3, h_next = _mm_nn(act, w["ffn_w_down"], tm=1024, tn=1024, tk=HALF_TILE, out_dtype=F32, res=x2, name="mm_down"), None
    else:
        x3, h_next = _mm_nn(act, w["ffn_w_down"], tm=1024, tn=1024, tk=HALF_TILE, out_dtype=F32, res=x2, norm_gain=next_gain,
                            name="mm_down_norm")
    saved = dict(x=x, h=h, proj=proj, xc=xc, y=y, x2=x2, hn=hn, up3=up3, act=act, attn_o=attn_o, attn_l=attn_l,
                 ssd_sum=ssd_sum, ssd_states=ssd_states)
    return x3, h_next, saved, w, dict(attn=got_attn, gate=got_gate)


def _backward_layer(dx3, sv, lp, w, bias, pending=None, reducer=None):
    d_act = _mm_nt(dx3, w["ffn_w_down"], tm=1024, tn=HALF_TILE, tk=1024, out_dtype=F32, name="mm_dact")
    dw_down = _mm_tn(sv["act"], dx3, tm=HALF_TILE, tn=1024, tk=1024, out_dtype=BF16, name="mm_dwdown")
    (dup3, dcw3), from_sibling = _convgate_bwd(sv["up3"], lp["cw3"], lp["cb3"], d_act, hook=pending.sibling_hook() if pending else None)
    if pending:
        pending.add_sibling(from_sibling)
    d_hn = _mm_dhn(dup3, w["ffn_w_up"])
    dw_up = _mm_dwup(sv["hn"], dup3)
    dx2, d_ffn_g = _rmsnorm_bwd(sv["x2"], lp["ffn_g"], d_hn, dx3, "rmsnorm_bwd")
    d_y = _mm_nt(dx2, w["w_out"], tm=1024, tn=1024, tk=1024, out_dtype=F32, name="mm_dy")
    dw_out = _mm_tn(sv["y"], dx2, tm=1024, tn=1024, tk=1024, out_dtype=BF16, name="mm_dwout")
    early = reducer(("w_out", "ffn_w_up", "ffn_w_down"), (dw_out, dw_up, dw_down)) if reducer else None
    proj, xc = sv["proj"], sv["xc"]
    (dxs, dbc, dcc, dz, ddt2, ddtb2, dal2, d_dexp, d_outg), from_chips = _ssd_scan_bwd(
        xc, proj, lp["dtb"], lp["alog"], lp["d_exp"], lp["out_gain"], d_y, sv["ssd_sum"], sv["ssd_states"],
        hook=pending.chips_hook() if pending else None)
    if pending:
        pending.add_chips(from_chips)
    (d_xbc, dcw5), from_sibling = _ssd_pre_bwd(proj, lp["cw5"], lp["cb5"], jnp.concatenate([dxs, dbc, dcc], axis=1),
                                               hook=early.sibling_hook() if early else None)
    if early:
        early.add_sibling(from_sibling)
    d_gu, d_gv, d_vg, d_ws, d_bexp = _gmlp_bwd(proj, lp["v_gain"], lp["w_s"], lp["b_exp"], d_y)
    (d_q, d_k, d_v, d_qg2, d_kg2, d_bias), from_chips = _attn_bwd(proj, lp["q_gain"], lp["k_gain"], bias, d_y, sv["attn_o"], sv["attn_l"],
                                                                  hook=early.chips_hook() if early else None)
    if early:
        early.add_chips(from_chips)
    d_dt = (ddt2[0] + ddt2[1]).astype(BF16)
    d_proj = jnp.concatenate([d_xbc, dz, d_gu, d_gv, d_q, d_k, d_v, d_dt, jnp.zeros((T, NP - C_DT - 128), BF16)], axis=1)
    d_h = _mm_nt(d_proj, w["w_in"], tm=1024, tn=1024, tk=1024, out_dtype=F32, name="mm_dh")
    dw_in = _mm_tn(sv["h"], d_proj, tm=1024, tn=1024, tk=1024, out_dtype=BF16, name="mm_dwin")
    late = None
    if reducer:
        late = reducer(("w_in",), (dw_in,))
        late.run_alone()
    dx, d_mix_g = _rmsnorm_bwd(sv["x"], lp["mix_g"], d_h, dx2, "rmsnorm_bwd")
    d_lp = dict(mix_g=d_mix_g, cw5=dcw5[:8] * (jnp.arange(8) < 5)[:, None].astype(F32), cb5=dcw5[5:6],
                dtb=(ddtb2[0, :1] + ddtb2[1, :1]), alog=(dal2[0, :1] + dal2[1, :1]), d_exp=d_dexp, out_gain=d_outg,
                v_gain=d_vg, w_s=d_ws, b_exp=d_bexp, q_gain=d_qg2[0, :1] + d_qg2[1, :1], k_gain=d_kg2[0, :1] + d_kg2[1, :1],
                ffn_g=d_ffn_g, cw3=dcw3 * (jnp.arange(8) < 3)[None, :, None].astype(F32), cb3=dcw3[:, 3:4])
    return dx, dict(w_in=dw_in, w_out=dw_out, ffn_w_up=dw_up, ffn_w_down=dw_down), d_lp, d_bias, (early, late)


def _to_shard_major(name, dw):
    if name == "ffn_w_up":
        return dw
    if name == "w_in":
        r, c = dw.shape[0], IN_WIDTH
        return jnp.transpose(_unperm_cols(dw).reshape(r, N_CHIPS, c // N_CHIPS), (1, 0, 2))
    r, c = dw.shape
    return dw.reshape(N_CHIPS, r // N_CHIPS, c)


def _whole_weight(name, gathered, own, chip):
    if name == "w_in":
        return _perm_cols(jnp.concatenate([jnp.where(chip == k, own, gathered[k]) for k in range(N_CHIPS)], axis=1))
    w = lax.dynamic_update_index_in_dim(gathered, own, chip, axis=0)
    return w if name == "ffn_w_up" else w.reshape(N_CHIPS * own.shape[0], own.shape[1])


class _LayerReduce:
    def __init__(self, names, dws, chip, core):
        self.names, self.chip, self.core = names, chip, core
        self.parts = [_to_shard_major(n, dw) for n, dw in zip(names, dws)]

    def sibling_hook(self):
        return _to_sibling_hook(self.parts, half_rows=True)

    def add_sibling(self, got):
        self.sums = [_add_pair(a, b, self.core, "add_pair_" + n) for n, a, b in zip(self.names, self.parts, got)]

    def chips_hook(self):
        return _to_chips_hook(self.sums)

    def add_chips(self, got):
        self.half = {}
        for n, s2, g3 in zip(self.names, self.sums, got):
            own = lax.dynamic_index_in_dim(s2, self.chip, axis=0, keepdims=True)
            self.half[n] = _add_four(own, g3[:, None], own.shape[1], "add_four_" + n)[0]

    def run_alone(self):
        self.add_sibling(_run_hook(self.sibling_hook()))
        self.add_chips(_run_hook(self.chips_hook()))


LAYER_SMALL = ("mix_norm_gain", "ssd_conv_w", "ssd_conv_b", "ssd_dt_bias", "ssd_a_log", "ssd_d", "ssd_out_gain", "gmlp_v_gain",
               "gmlp_w_s", "gmlp_b_s", "attn_q_gain", "attn_k_gain", "ffn_norm_gain", "ffn_conv_w", "ffn_conv_b")


def _local_grads(x, loss_target, p, conv5_w, conv3_w, layer_w, exchange=None):
    bias, bias_vjp = jax.vjp(_attn_bias, p["rel_bias_table"])
    xt = x.reshape(T, D_MODEL)
    layer_w = list(layer_w)
    saved, lps, lp_vjps = [], [], []
    if exchange is not None:
        chip, core, own = exchange
        whole = lambda names, layer, gathered: {n: _whole_weight(n, g, own[layer][BIG.index(n)], chip) for n, g in zip(names, gathered)}
    for l in range(DEPTH):
        lp, lp_vjp = _layer_params(l, p, conv5_w, conv3_w, bias)
        lps.append(lp)
        lp_vjps.append(lp_vjp)
    h = _rmsnorm_fwd(xt, lps[0]["mix_g"], "rmsnorm_fwd")
    for l in range(DEPTH):
        lp = lps[l]
        hooks = {}
        if exchange is not None and l == 0:
            hooks["self"] = _gather_hook(own[0][1:])
        if exchange is not None and l + 1 < DEPTH:
            hooks["attn"] = _gather_hook(own[l + 1][2:])
            hooks["gate"] = _gather_hook(own[l + 1][:2])
        xt, h, sv, layer_w[l], got = _forward_layer(xt, h, lp, layer_w[l], bias, lps[l + 1]["mix_g"] if l + 1 < DEPTH else None, hooks,
                                                    resolve=lambda g: whole(BIG[1:], 0, g))
        if "attn" in hooks:
            layer_w.append(dict(whole(BIG[:2], l + 1, got["gate"]), **whole(BIG[2:], l + 1, got["attn"])))
        saved.append(sv)
    dxt, loss_parts = _loss_head(xt, loss_target.reshape(T, D_MODEL))
    loss_local = jnp.sum(loss_parts[::8, 0])

    big_grads = [None] * DEPTH
    small_layers = [None] * DEPTH
    d_bias_tot = jnp.zeros_like(bias)
    pending = None
    for l in reversed(range(DEPTH)):
        last = exchange is not None and l == 0
        dxt, big_grads[l], d_lp, d_bias, own_reduce = _backward_layer(
            dxt, saved[l], lps[l], layer_w[l], bias, pending=pending,
            reducer=(lambda names, dws: _LayerReduce(names, dws, chip, core)) if last else None)
        if pending is not None:
            big_grads[l + 1] = pending.half
        if last:
            big_grads[l] = dict(own_reduce[0].half, **own_reduce[1].half)
        elif exchange is not None:
            pending = _LayerReduce(BIG, [big_grads[l][n] for n in BIG], chip, core)
        small_layers[l] = lp_vjps[l](d_lp)
        d_bias_tot = d_bias_tot + d_bias
    (d_rel_table,) = bias_vjp(d_bias_tot)
    local_small = {n: jnp.stack([small_layers[l][i] for l in range(DEPTH)]) for i, n in enumerate(LAYER_SMALL)}
    local_small["rel_bias_table"] = d_rel_table
    return dxt, loss_local, big_grads, local_small


def kernel(x, mix_norm_gain, w_in, ssd_conv_w, ssd_conv_b, ssd_dt_bias, ssd_a_log, ssd_d, ssd_out_gain, gmlp_v_gain, gmlp_w_s, gmlp_b_s, attn_q_gain, attn_k_gain, rel_bias_table, w_out, ffn_norm_gain, ffn_w_up, ffn_conv_w, ffn_conv_b, ffn_w_down, loss_target, m_mix_norm_gain, m_w_in, m_ssd_conv_w, m_ssd_conv_b, m_ssd_dt_bias, m_ssd_a_log, m_ssd_d, m_ssd_out_gain, m_gmlp_v_gain, m_gmlp_w_s, m_gmlp_b_s, m_attn_q_gain, m_attn_k_gain, m_rel_bias_table, m_w_out, m_ffn_norm_gain, m_ffn_w_up, m_ffn_conv_w, m_ffn_conv_b, m_ffn_w_down, v_mix_norm_gain, v_w_in, v_ssd_conv_w, v_ssd_conv_b, v_ssd_dt_bias, v_ssd_a_log, v_ssd_d, v_ssd_out_gain, v_gmlp_v_gain, v_gmlp_w_s, v_gmlp_b_s, v_attn_q_gain, v_attn_k_gain, v_rel_bias_table, v_w_out, v_ffn_norm_gain, v_ffn_w_up, v_ffn_conv_w, v_ffn_conv_b, v_ffn_w_down):
    env = dict(locals())
    p = {n: env[n] for n in WEIGHTS}
    chip = 2 * lax.axis_index("x") + lax.axis_index("y")
    core = lax.axis_index("c")

    conv_slots = _allgather8(_pack([ssd_conv_w, ffn_conv_w]), "allgather_conv")
    conv_parts = [_unpack(conv_slots[2 * k], [ssd_conv_w.shape, ffn_conv_w.shape]) for k in range(N_CHIPS)]
    conv5_w = jnp.concatenate([cp[0] for cp in conv_parts], axis=-1)
    conv3_w = jnp.concatenate([cp[1] for cp in conv_parts], axis=-1)
    own = [[p[n][l].astype(BF16) for n in BIG] for l in range(DEPTH)]
    (first,) = _run_hook(_gather_hook(own[0][:1]))
    layer_w = [{"w_in": _whole_weight("w_in", first, own[0][0], chip)}]

    dxt, loss_local, reduced, local_small = _local_grads(x, loss_target, p, conv5_w, conv3_w, layer_w, exchange=(chip, core, own))

    small_shapes = [local_small[n].shape for n in SMALL] + [(1,)]
    summed = _unpack(_allreduce_small(_pack([local_small[n] for n in SMALL] + [loss_local.reshape(1)]), chip), small_shapes)
    grads = dict(zip(SMALL, summed[:-1]))
    loss = summed[-1][0]
    grads["ssd_conv_w"] = lax.dynamic_slice_in_dim(grads["ssd_conv_w"], chip * 256, 256, axis=2)
    grads["ffn_conv_w"] = lax.dynamic_slice_in_dim(grads["ffn_conv_w"], chip * (2 * FFN_DIM // N_CHIPS), 2 * FFN_DIM // N_CHIPS, axis=2)

    halves = [jnp.stack([reduced[l][n] for l in range(DEPTH)]) for n in BIG]
    others = _run_hook(dict(_to_sibling_hook(halves), name="swap_halves"))

    delta, new_m, new_v = {}, {}, {}
    for n, mine, other in zip(BIG, halves, others):
        grads[n], delta[n], new_m[n], new_v[n] = _adamw(p[n], mine, other, env["m_" + n], env["v_" + n], core, ADAM_ROWS[n], "adamw_" + n)
    d, nm, nv = _adamw_many([p[n] for n in SMALL], [grads[n] for n in SMALL], [env["m_" + n] for n in SMALL], [env["v_" + n] for n in SMALL])
    for n, a, b, c in zip(SMALL, d, nm, nv):
        delta[n], new_m[n], new_v[n] = a, b, c

    return (loss, dxt.reshape(B_LOC, SEQ, D_MODEL), *[grads[n] for n in WEIGHTS], *[delta[n] for n in WEIGHTS],
            *[new_m[n] for n in WEIGHTS], *[new_v[n] for n in WEIGHTS])
```

```python
import functools
import math

import jax
import jax.numpy as jnp
import numpy as np
from jax import lax
from jax.experimental import pallas as pl
from jax.experimental.pallas import tpu as pltpu

F32 = jnp.float32
BF16 = jnp.bfloat16
HI = lax.Precision.HIGHEST
SUM_PRECISION = lax.Precision.HIGH
MESH = pl.DeviceIdType.MESH
ANY = pl.BlockSpec(memory_space=pl.ANY)

D_MODEL = 1024
SEQ = 2048
B_LOC = 2
T = B_LOC * SEQ
DEPTH = 4
N_CHIPS = 4
N_DEV = 8
HEAD = 64
CHUNK = 128
N_CHUNK = SEQ // CHUNK
SSD_INNER = 512
SSD_XBC = 1024
FFN_DIM = 2816
IN_WIDTH = 2832
NP = 3072
C_XS, C_B, C_C, C_Z, C_GU, C_GV, C_Q, C_K, C_V, C_DT = 0, 512, 768, 1024, 1536, 1792, 2048, 2304, 2560, 2816
NORM_EPS = 1e-6
NEG_INF = -1e30
ATTN_DILS = (1, 4, 16)
ATTN_HALF = 64
ADAM_LR, ADAM_B1, ADAM_B2, ADAM_EPS, ADAM_WD, ADAM_STEP = 0.001, 0.9, 0.999, 1e-08, 0.01, 10
VMEM_LIMIT = 56 * 1024 * 1024

S_ = jax.ShapeDtypeStruct


def _cp():
    return pltpu.CompilerParams(vmem_limit_bytes=VMEM_LIMIT)


def _shift_rows(x, k):
    n = x.shape[0]
    if k == 0:
        return x
    r = pltpu.roll(x, (-k) % n, 0)
    t = lax.broadcasted_iota(jnp.int32, (n, 1), 0)
    return jnp.where((t + k >= 0) & (t + k < n), r, 0.0)


@functools.partial(jax.custom_vjp, nondiff_argnums=(1,))
def _shift(x, k):
    return _shift_rows(x, k)


def _shift_fwd(x, k):
    return _shift_rows(x, k), None


def _shift_bwd(k, _, g):
    return (_shift_rows(g, -k),)


_shift.defvjp(_shift_fwd, _shift_bwd)


def _dwconv(x, taps, bias):
    half = len(taps) // 2
    y = bias
    for k, w in enumerate(taps):
        y = y + w * _shift(x, k - half)
    return y


def _softplus(x):
    return jnp.maximum(x, 0.0) + jnp.log1p(jnp.exp(-jnp.abs(x)))


def _dot(a, b):
    return jnp.dot(a.astype(BF16), b.astype(BF16), preferred_element_type=F32)


def _dot_nt(a, b):
    return lax.dot_general(a.astype(BF16), b.astype(BF16), (((1,), (1,)), ((), ())), preferred_element_type=F32)


def _dot_tn(a, b):
    return lax.dot_general(a.astype(BF16), b.astype(BF16), (((0,), (0,)), ((), ())), preferred_element_type=F32)


def _head_sum_matrix(width):
    i = lax.broadcasted_iota(jnp.int32, (width, width), 0) // HEAD
    j = lax.broadcasted_iota(jnp.int32, (width, width), 1) // HEAD
    return (i == j).astype(F32)


def _matmul(a, b, *, dims, grid, a_spec, b_spec, o_spec, out_shape, acc_shape, res=None, res_spec=None, norm_gain=None,
            norm_bwd=None, name):
    nk = grid[2]
    n_in = 2 + (res is not None) + (norm_gain is not None) + 2 * (norm_bwd is not None)

    def body(*refs):
        a_ref, b_ref = refs[:2]
        r_ref = refs[2] if res is not None else None
        g_ref = refs[n_in - 1] if norm_gain is not None or norm_bwd is not None else None
        x_ref = refs[n_in - 2] if norm_bwd is not None else None
        o_ref = refs[n_in]
        n_ref = refs[n_in + 1] if norm_gain is not None or norm_bwd is not None else None
        row_tile = pl.program_id(0)

        def finish(tot):
            if x_ref is not None:
                xv = x_ref[...]
                scale = lax.rsqrt(jnp.mean(xv * xv, axis=-1, keepdims=True) + NORM_EPS)
                gd = tot * g_ref[...]
                dot = jnp.mean(gd * xv, axis=-1, keepdims=True)
                o_ref[...] = r_ref[...] + scale * gd - xv * (scale * scale * scale * dot)
                part = jnp.sum(tot * xv * scale, axis=0, keepdims=True)

                @pl.when(row_tile == 0)
                def _():
                    n_ref[...] = part

                @pl.when(row_tile > 0)
                def _():
                    n_ref[...] += part

                return
            if r_ref is not None:
                tot = tot + r_ref[...]
            o_ref[...] = tot.astype(o_ref.dtype)
            if n_ref is not None:
                scale = lax.rsqrt(jnp.mean(tot * tot, axis=-1, keepdims=True) + NORM_EPS)
                n_ref[...] = (tot * scale * g_ref[...]).astype(BF16)

        part = lax.dot_general(a_ref[...].astype(BF16), b_ref[...].astype(BF16), dims, preferred_element_type=F32)
        if nk == 1:
            finish(part)
            return
        acc_ref = refs[-1]
        k = pl.program_id(2)

        @pl.when(k == 0)
        def _():
            acc_ref[...] = part

        @pl.when(k > 0)
        def _():
            acc_ref[...] += part

        @pl.when(k == nk - 1)
        def _():
            finish(acc_ref[...])

    in_specs = [a_spec, b_spec] + ([res_spec] if res is not None else [])
    args = (a, b) + ((res,) if res is not None else ())
    out_specs, out_shapes = o_spec, out_shape
    row = pl.BlockSpec((1, acc_shape[1]), lambda i, j, q: (0, 0))
    if norm_gain is not None:
        in_specs.append(row)
        args = args + (norm_gain,)
        out_specs, out_shapes = [o_spec, o_spec], [out_shape, S_(out_shape.shape, BF16)]
    if norm_bwd is not None:
        in_specs += [res_spec, row]
        args = args + tuple(norm_bwd)
        out_specs, out_shapes = [o_spec, row], [out_shape, S_((1, acc_shape[1]), F32)]
    scratch = [] if nk == 1 else [pltpu.VMEM(acc_shape, F32)]
    return pl.pallas_call(body, grid=grid, in_specs=in_specs, out_specs=out_specs, out_shape=out_shapes,
                          scratch_shapes=scratch, compiler_params=_cp(), name=name)(*args)


NN = (((1,), (0,)), ((), ()))
NT = (((1,), (1,)), ((), ()))
TN = (((0,), (0,)), ((), ()))


def _mm_nn(a, b, *, tm, tn, tk, out_dtype, res=None, norm_gain=None, name):
    m, k = a.shape
    n = b.shape[1]
    assert norm_gain is None or tn == n
    return _matmul(a, b, dims=NN, grid=(m // tm, n // tn, k // tk),
                   a_spec=pl.BlockSpec((tm, tk), lambda i, j, q: (i, q)),
                   b_spec=pl.BlockSpec((tk, tn), lambda i, j, q: (q, j)),
                   o_spec=pl.BlockSpec((tm, tn), lambda i, j, q: (i, j)),
                   out_shape=S_((m, n), out_dtype), acc_shape=(tm, tn), res=res,
                   res_spec=pl.BlockSpec((tm, tn), lambda i, j, q: (i, j)), norm_gain=norm_gain, name=name)


def _mm_nt(a, b, *, tm, tn, tk, out_dtype, res=None, norm_bwd=None, name):
    m, k = a.shape
    n = b.shape[0]
    assert norm_bwd is None or tn == n
    return _matmul(a, b, dims=NT, grid=(m // tm, n // tn, k // tk),
                   a_spec=pl.BlockSpec((tm, tk), lambda i, j, q: (i, q)),
                   b_spec=pl.BlockSpec((tn, tk), lambda i, j, q: (j, q)),
                   o_spec=pl.BlockSpec((tm, tn), lambda i, j, q: (i, j)),
                   out_shape=S_((m, n), out_dtype), acc_shape=(tm, tn), res=res,
                   res_spec=pl.BlockSpec((tm, tn), lambda i, j, q: (i, j)), norm_bwd=norm_bwd, name=name)


def _mm_tn(a, b, *, tm, tn, tk, out_dtype, name):
    k, m = a.shape
    n = b.shape[1]
    return _matmul(a, b, dims=TN, grid=(m // tm, n // tn, k // tk),
                   a_spec=pl.BlockSpec((tk, tm), lambda i, j, q: (q, i)),
                   b_spec=pl.BlockSpec((tk, tn), lambda i, j, q: (q, j)),
                   o_spec=pl.BlockSpec((tm, tn), lambda i, j, q: (i, j)),
                   out_shape=S_((m, n), out_dtype), acc_shape=(tm, tn), name=name)


HALF_TILE = FFN_DIM // 2


def _mm_up(hn, w_up):
    return _matmul(hn, w_up, dims=NN, grid=(T // 1024, 4, 1),
                   a_spec=pl.BlockSpec((1024, D_MODEL), lambda i, j, q: (i, 0)),
                   b_spec=pl.BlockSpec((None, D_MODEL, HALF_TILE), lambda i, j, q: (j, 0, 0)),
                   o_spec=pl.BlockSpec((None, 1024, HALF_TILE), lambda i, j, q: (j // 2, i, j % 2)),
                   out_shape=S_((2, T, FFN_DIM), F32), acc_shape=(1024, HALF_TILE), name="mm_up")


def _mm_dhn(dup3, w_up, x2, gain, dres):
    row = pl.BlockSpec((1024, D_MODEL), lambda i, j, q: (i, 0))
    return _matmul(dup3, w_up, dims=NT, grid=(T // 1024, 1, 4),
                   a_spec=pl.BlockSpec((None, 1024, HALF_TILE), lambda i, j, q: (q // 2, i, q % 2)),
                   b_spec=pl.BlockSpec((None, D_MODEL, HALF_TILE), lambda i, j, q: (q, 0, 0)),
                   o_spec=row, out_shape=S_((T, D_MODEL), F32), acc_shape=(1024, D_MODEL), res=dres, res_spec=row,
                   norm_bwd=(x2, gain), name="mm_dhn")


def _mm_dwup(hn, dup3):
    return _matmul(hn, dup3, dims=TN, grid=(1, 4, T // 1024),
                   a_spec=pl.BlockSpec((1024, D_MODEL), lambda i, j, q: (q, 0)),
                   b_spec=pl.BlockSpec((None, 1024, HALF_TILE), lambda i, j, q: (j // 2, q, j % 2)),
                   o_spec=pl.BlockSpec((None, D_MODEL, HALF_TILE), lambda i, j, q: (j, 0, 0)),
                   out_shape=S_((N_CHIPS, D_MODEL, HALF_TILE), BF16), acc_shape=(D_MODEL, HALF_TILE), name="mm_dwup")


ROWS = 512


def _rmsnorm_fwd(x, gain, name):
    def body(x_ref, g_ref, o_ref):
        xv = x_ref[...]
        r = lax.rsqrt(jnp.mean(xv * xv, axis=-1, keepdims=True) + NORM_EPS)
        o_ref[...] = (xv * r * g_ref[...]).astype(BF16)

    return pl.pallas_call(body, grid=(T // ROWS,),
                          in_specs=[pl.BlockSpec((ROWS, D_MODEL), lambda i: (i, 0)), pl.BlockSpec((1, D_MODEL), lambda i: (0, 0))],
                          out_specs=pl.BlockSpec((ROWS, D_MODEL), lambda i: (i, 0)),
                          out_shape=S_((T, D_MODEL), BF16), name=name)(x, gain)


def _loss_head(y, target):
    def body(y_ref, t_ref, dy_ref, p_ref):
        e = y_ref[...] - t_ref[...]
        dy_ref[...] = e * (1.0 / D_MODEL)
        p_ref[...] = jnp.full((8, 128), 0.5 / D_MODEL, F32) * jnp.sum(e * e)

    row = pl.BlockSpec((ROWS, D_MODEL), lambda i: (i, 0))
    return pl.pallas_call(body, grid=(T // ROWS,), in_specs=[row, row],
                          out_specs=[row, pl.BlockSpec((8, 128), lambda i: (i, 0))],
                          out_shape=[S_((T, D_MODEL), F32), S_((T // ROWS * 8, 128), F32)], name="loss_head")(y, target)


def _adamw_update(w_ref, g_ref, m_ref, v_ref, d_ref, nm_ref, nv_ref):
    gv = g_ref[...]
    nm = ADAM_B1 * m_ref[...] + (1.0 - ADAM_B1) * gv
    nv = ADAM_B2 * v_ref[...] + (1.0 - ADAM_B2) * (gv * gv)
    m_hat = nm / (1.0 - ADAM_B1 ** ADAM_STEP)
    v_hat = nv / (1.0 - ADAM_B2 ** ADAM_STEP)
    d_ref[...] = -ADAM_LR * (m_hat / (jnp.sqrt(v_hat) + ADAM_EPS) + ADAM_WD * w_ref[...])
    nm_ref[...] = nm
    nv_ref[...] = nv


def _adamw(w, g_mine, g_other, m, v, core, rows, name):
    per_half = w.shape[1] // 2 // rows

    def body(core_ref, w_ref, gm_ref, go_ref, m_ref, v_ref, g_ref, d_ref, nm_ref, nv_ref):
        mine = (pl.program_id(1) // per_half) == core_ref[0]
        g_ref[...] = jnp.where(mine, gm_ref[...], go_ref[...])
        _adamw_update(w_ref, g_ref, m_ref, v_ref, d_ref, nm_ref, nv_ref)

    blk = pl.BlockSpec((None, rows, w.shape[2]), lambda l, i, core_ref: (l, i, 0))
    half = pl.BlockSpec((None, rows, w.shape[2]), lambda l, i, core_ref: (l, i % per_half, 0))
    out = S_(w.shape, F32)
    spec = pltpu.PrefetchScalarGridSpec(num_scalar_prefetch=1, grid=(w.shape[0], w.shape[1] // rows),
                                        in_specs=[blk, half, half, blk, blk], out_specs=[blk] * 4)
    return pl.pallas_call(body, grid_spec=spec, out_shape=[out] * 4, name=name)(core.reshape(1).astype(jnp.int32), w, g_mine, g_other, m, v)


def _adamw_many(ws, gs, ms, vs):
    n = len(ws)

    def body(*refs):
        for i in range(n):
            _adamw_update(*[refs[k * n + i] for k in range(7)])

    out = [S_(w.shape, F32) for w in ws]
    res = pl.pallas_call(body, out_shape=out * 3, name="adamw_small")(*ws, *gs, *ms, *vs)
    return res[:n], res[n:2 * n], res[2 * n:]


FFN_CT = 256


def _gate_fn(up_g, up_v, wg0, wg1, wg2, bg, wv0, wv1, wv2, bv):
    gate = _dwconv(up_g, [wg0, wg1, wg2], bg)
    val = _dwconv(up_v, [wv0, wv1, wv2], bv)
    return jax.nn.silu(gate) * val


def _taps(ref, part, n):
    return [ref[part, k:k + 1, :] for k in range(n)]


def _convgate_fwd(up3, cw, cb, hook=None):
    def body(up_ref, cw_ref, cb_ref, o_ref):
        o_ref[...] = _gate_fn(up_ref[0], up_ref[1], *_taps(cw_ref, 0, 3), cb_ref[0], *_taps(cw_ref, 1, 3), cb_ref[1]).astype(BF16)

    (act,), got = _hooked_call(
        body, grid=(FFN_DIM // FFN_CT, B_LOC),
        in_specs=[pl.BlockSpec((2, SEQ, FFN_CT), lambda j, b: (0, b, j)),
                  pl.BlockSpec((2, 8, FFN_CT), lambda j, b: (0, 0, j)),
                  pl.BlockSpec((2, 1, FFN_CT), lambda j, b: (0, 0, j))],
        out_specs=[pl.BlockSpec((SEQ, FFN_CT), lambda j, b: (b, j))],
        out_shape=[S_((T, FFN_DIM), BF16)], scratch_shapes=[], args=(up3, cw, cb), hook=hook, name="convgate_fwd")
    return act, got


def _convgate_bwd(up3, cw, cb, dact, hook=None):
    def body(up_ref, cw_ref, cb_ref, da_ref, dup_ref, dcw_ref):
        args = (up_ref[0], up_ref[1], *_taps(cw_ref, 0, 3), cb_ref[0], *_taps(cw_ref, 1, 3), cb_ref[1])
        _, vjp = jax.vjp(_gate_fn, *args)
        dg, dv, g0, g1, g2, gb, v0, v1, v2, vb = vjp(da_ref[...])
        dup_ref[0] = dg.astype(BF16)
        dup_ref[1] = dv.astype(BF16)
        zero = jnp.zeros((4, FFN_CT), F32)
        new = jnp.stack([jnp.concatenate([g0, g1, g2, gb, zero], axis=0), jnp.concatenate([v0, v1, v2, vb, zero], axis=0)])

        @pl.when(pl.program_id(1) == 0)
        def _():
            dcw_ref[...] = new

        @pl.when(pl.program_id(1) > 0)
        def _():
            dcw_ref[...] += new

    return _hooked_call(
        body, grid=(FFN_DIM // FFN_CT, B_LOC),
        in_specs=[pl.BlockSpec((2, SEQ, FFN_CT), lambda j, b: (0, b, j)),
                  pl.BlockSpec((2, 8, FFN_CT), lambda j, b: (0, 0, j)),
                  pl.BlockSpec((2, 1, FFN_CT), lambda j, b: (0, 0, j)),
                  pl.BlockSpec((SEQ, FFN_CT), lambda j, b: (b, j))],
        out_specs=[pl.BlockSpec((2, SEQ, FFN_CT), lambda j, b: (0, b, j)),
                   pl.BlockSpec((2, 8, FFN_CT), lambda j, b: (0, 0, j))],
        out_shape=[S_((2, T, FFN_DIM), BF16), S_((2, 8, FFN_DIM), F32)],
        scratch_shapes=[], args=(up3, cw, cb, dact), hook=hook, name="convgate_bwd")


SSD_CT = 256


def _conv5_fn(x, w0, w1, w2, w3, w4, b):
    return jax.nn.silu(_dwconv(x, [w0, w1, w2, w3, w4], b))


def _ssd_pre_fwd(proj, cw, cb):
    def body(x_ref, cw_ref, cb_ref, o_ref):
        o_ref[...] = _conv5_fn(x_ref[...], *[cw_ref[k:k + 1, :] for k in range(5)], cb_ref[...])

    return pl.pallas_call(
        body, grid=(SSD_XBC // SSD_CT, B_LOC),
        in_specs=[pl.BlockSpec((SEQ, SSD_CT), lambda j, b: (b, j)),
                  pl.BlockSpec((8, SSD_CT), lambda j, b: (0, j)),
                  pl.BlockSpec((1, SSD_CT), lambda j, b: (0, j))],
        out_specs=pl.BlockSpec((SEQ, SSD_CT), lambda j, b: (b, j)),
        out_shape=S_((T, SSD_XBC), F32), compiler_params=_cp(), name="ssd_pre_fwd")(proj, cw, cb)


def _ssd_pre_bwd(proj, cw, cb, dxc, hook=None):
    def body(x_ref, cw_ref, cb_ref, d_ref, dx_ref, dcw_ref):
        _, vjp = jax.vjp(_conv5_fn, x_ref[...], *[cw_ref[k:k + 1, :] for k in range(5)], cb_ref[...])
        dx, g0, g1, g2, g3, g4, gb = vjp(d_ref[...])
        dx_ref[...] = dx.astype(BF16)
        new = jnp.concatenate([g0, g1, g2, g3, g4, gb, jnp.zeros((2, SSD_CT), F32)], axis=0)

        @pl.when(pl.program_id(1) == 0)
        def _():
            dcw_ref[...] = new

        @pl.when(pl.program_id(1) > 0)
        def _():
            dcw_ref[...] += new

    return _hooked_call(
        body, grid=(SSD_XBC // SSD_CT, B_LOC),
        in_specs=[pl.BlockSpec((SEQ, SSD_CT), lambda j, b: (b, j)),
                  pl.BlockSpec((8, SSD_CT), lambda j, b: (0, j)),
                  pl.BlockSpec((1, SSD_CT), lambda j, b: (0, j)),
                  pl.BlockSpec((SEQ, SSD_CT), lambda j, b: (b, j))],
        out_specs=[pl.BlockSpec((SEQ, SSD_CT), lambda j, b: (b, j)),
                   pl.BlockSpec((8, SSD_CT), lambda j, b: (0, j))],
        out_shape=[S_((T, SSD_XBC), BF16), S_((8, SSD_XBC), F32)],
        scratch_shapes=[], args=(proj, cw, cb, dxc), hook=hook, name="ssd_pre_bwd")


GROUP_W = 256
ONE_BUFFER = dict(pipeline_mode=pl.Buffered(1))
HEADS_PER_GROUP = 4
FWD_SCAN_UNROLL = 4
BWD_SCAN_UNROLL = 2


def _ssd_dt_fn(dt_raw, bias, alog):
    dt = _softplus(dt_raw + bias)
    return dt, dt * (-jnp.exp(alog))


def _ssd_chunk_fn(direction, group, xc0, xc1, bc, cc, dt, da, prev0, prev1):
    q = CHUNK
    ti = lax.broadcasted_iota(jnp.int32, (q, q), 0)
    si = lax.broadcasted_iota(jnp.int32, (q, q), 1)
    keep = (ti >= si) if direction == 0 else (ti <= si)
    mat = keep.astype(F32)
    acs = jnp.dot(mat, da, precision=HI, preferred_element_type=F32)
    acs_t = lax.dot_general(da, mat, (((0,), (1,)), ((), ())), precision=HI, preferred_element_type=F32)
    tot = jnp.sum(da, axis=0, keepdims=True)
    lane = lax.broadcasted_iota(jnp.int32, (1, 128), 1)
    sub = lax.broadcasted_iota(jnp.int32, (128, 1), 0)
    first_head = lane < HEAD
    cb = _dot_nt(cc, bc)
    a_cols, tots, dt_cols, lows, douts = [], [], [], [], []
    for h in range(HEADS_PER_GROUP):
        ln = 8 * direction + 4 * group + h
        oh_l = (lane == ln).astype(F32)
        oh_s = (sub == ln).astype(F32)
        a_col = jnp.sum(acs * oh_l, axis=1, keepdims=True)
        a_row = jnp.sum(acs_t * oh_s, axis=0, keepdims=True)
        tot_h = jnp.sum(tot * oh_l, axis=1, keepdims=True)
        a_cols.append(a_col)
        tots.append(tot_h)
        dt_cols.append(jnp.sum(dt * oh_l, axis=1, keepdims=True))
        lows.append(cb * jnp.exp(jnp.where(keep, a_col - a_row, NEG_INF)))
        douts.append(bc * jnp.exp(tot_h - a_col))
    out = []
    for pair, (xc, prev) in enumerate(((xc0, prev0), (xc1, prev1))):
        h0, h1 = 2 * pair, 2 * pair + 1
        xdt = xc * jnp.where(first_head, dt_cols[h0], dt_cols[h1])
        y = jnp.where(first_head, jnp.exp(a_cols[h0]), jnp.exp(a_cols[h1])) * _dot(cc, prev)
        y = y + jnp.where(first_head, _dot(lows[h0], xdt), _dot(lows[h1], xdt))
        st = jnp.where(first_head, _dot_tn(douts[h0], xdt), _dot_tn(douts[h1], xdt))
        out.append((y, prev * jnp.where(first_head, jnp.exp(tots[h0]), jnp.exp(tots[h1])) + st))
    return out[0][0], out[1][0], out[0][1], out[1][1]


def _ssd_post_fn(y, xc, z, d_exp, gain):
    y = (y + d_exp * xc) * jax.nn.silu(z)
    return y * lax.rsqrt(jnp.mean(y * y, axis=-1, keepdims=True) + NORM_EPS) * gain


def _chunk_rows(c):
    return pl.ds(pl.multiple_of(c * CHUNK, CHUNK), CHUNK)


def _scan_loop(step, init, unroll):
    def body(i, carry):
        for k in range(unroll):
            carry = step(i * unroll + k, carry)
        return carry

    return lax.fori_loop(0, N_CHUNK // unroll, body, init)


def _ssd_scan_specs(**mode):
    return [pl.BlockSpec((SEQ, GROUP_W), lambda g, b: (b, g), **mode),
            pl.BlockSpec((SEQ, 128), lambda g, b: (b, C_B // 128 + g), **mode),
            pl.BlockSpec((SEQ, 128), lambda g, b: (b, C_C // 128 + g), **mode),
            pl.BlockSpec((SEQ, GROUP_W), lambda g, b: (b, C_Z // GROUP_W + g), **mode),
            pl.BlockSpec((SEQ, 128), lambda g, b: (b, C_DT // 128), **mode),
            pl.BlockSpec((1, 128), lambda g, b: (0, 0)),
            pl.BlockSpec((1, 128), lambda g, b: (0, 0)),
            pl.BlockSpec((1, GROUP_W), lambda g, b: (0, g)),
            pl.BlockSpec((1, GROUP_W), lambda g, b: (0, g))]


def _ssd_state_spec(**mode):
    return pl.BlockSpec((None, None, 2 * N_CHUNK, 128, GROUP_W), lambda g, b: (g, b, 0, 0, 0), **mode)


def _ssd_scan_fwd(xc, proj, dtb, alog, d_exp, gain, hook=None):
    def body(x_ref, b_ref, c_ref, z_ref, dt_ref, dtb_ref, al_ref, de_ref, g_ref, o_ref, y_s, st_ref, dt_s, da_s):
        group = pl.program_id(0)
        dt, da = _ssd_dt_fn(dt_ref[...], dtb_ref[...], al_ref[...])
        dt_s[...] = dt
        da_s[...] = da
        for direction in (0, 1):
            def step(i, prev, direction=direction):
                c = i if direction == 0 else N_CHUNK - 1 - i
                rows = _chunk_rows(c)
                st_ref[direction * N_CHUNK + c, :, 0:128] = prev[0]
                st_ref[direction * N_CHUNK + c, :, 128:256] = prev[1]
                y0, y1, nxt0, nxt1 = _ssd_chunk_fn(direction, group, x_ref[rows, 0:128], x_ref[rows, 128:256], b_ref[rows, :], c_ref[rows, :],
                                                   dt_s[rows, :], da_s[rows, :], prev[0], prev[1])
                if direction == 0:
                    y_s[rows, 0:128] = y0
                    y_s[rows, 128:256] = y1
                else:
                    y_s[rows, 0:128] += y0
                    y_s[rows, 128:256] += y1
                return nxt0, nxt1

            _scan_loop(step, (jnp.zeros((128, 128), F32), jnp.zeros((128, 128), F32)), FWD_SCAN_UNROLL)

        def post(c, carry):
            rows = _chunk_rows(c)
            o_ref[rows, :] = _ssd_post_fn(y_s[rows, :], x_ref[rows, :], z_ref[rows, :], de_ref[...], g_ref[...]).astype(BF16)
            return carry

        lax.fori_loop(0, N_CHUNK, post, 0)

    return _hooked_call(
        body, grid=(2, B_LOC), in_specs=_ssd_scan_specs(),
        out_specs=[pl.BlockSpec((SEQ, GROUP_W), lambda g, b: (b, g)), pl.BlockSpec((SEQ, GROUP_W), lambda g, b: (b, g)), _ssd_state_spec()],
        out_shape=[S_((T, SSD_INNER), BF16), S_((T, SSD_INNER), F32), S_((2, B_LOC, 2 * N_CHUNK, 128, GROUP_W), F32)],
        scratch_shapes=[pltpu.VMEM((SEQ, 128), F32), pltpu.VMEM((SEQ, 128), F32)],
        args=(xc, xc, xc, proj, proj, dtb, alog, d_exp, gain), hook=hook, name="ssd_scan_fwd")


def _ssd_scan_bwd(xc, proj, dtb, alog, d_exp, gain, dy, ysum, states, hook=None):
    def body(x_ref, b_ref, c_ref, z_ref, dt_ref, dtb_ref, al_ref, de_ref, g_ref, dy_ref, ys_ref, st_s,
             dx_ref, db_ref, dc_ref, dz_ref, ddt_ref, ddtb_ref, dal_ref, dde_ref, dg_ref,
             dt_s, da_s, y_s, ddt_s, dda_s):
        group = pl.program_id(0)
        first = pl.program_id(1) == 0
        (dt, da), dt_vjp = jax.vjp(_ssd_dt_fn, dt_ref[...], dtb_ref[...], al_ref[...])
        dt_s[...] = dt
        da_s[...] = da

        def post(c, carry):
            rows = _chunk_rows(c)
            _, post_vjp = jax.vjp(_ssd_post_fn, ys_ref[rows, :], x_ref[rows, :], z_ref[rows, :], de_ref[...], g_ref[...])
            d_y, d_x_skip, d_z, g_de, g_g = post_vjp(dy_ref[rows, :])
            dz_ref[rows, :] = d_z.astype(BF16)
            dx_ref[rows, :] = d_x_skip
            y_s[rows, :] = d_y
            return carry[0] + g_de, carry[1] + g_g

        d_de, d_g = lax.fori_loop(0, N_CHUNK, post, (jnp.zeros((1, GROUP_W), F32), jnp.zeros((1, GROUP_W), F32)))
        db_ref[...] = jnp.zeros((SEQ, 128), F32)
        dc_ref[...] = jnp.zeros((SEQ, 128), F32)
        ddt_s[...] = jnp.zeros((SEQ, 128), F32)
        dda_s[...] = jnp.zeros((SEQ, 128), F32)
        for direction in (0, 1):
            def bstep(i, dnxt, direction=direction):
                c = N_CHUNK - 1 - i if direction == 0 else i
                rows = _chunk_rows(c)
                fn = functools.partial(_ssd_chunk_fn, direction, group)
                _, vjp = jax.vjp(fn, x_ref[rows, 0:128], x_ref[rows, 128:256], b_ref[rows, :], c_ref[rows, :], dt_s[rows, :], da_s[rows, :],
                                 st_s[direction * N_CHUNK + c, :, 0:128], st_s[direction * N_CHUNK + c, :, 128:256])
                g_x0, g_x1, g_b, g_c, g_dt, g_da, g_prev0, g_prev1 = vjp((y_s[rows, 0:128], y_s[rows, 128:256], dnxt[0], dnxt[1]))
                dx_ref[rows, 0:128] += g_x0
                dx_ref[rows, 128:256] += g_x1
                db_ref[rows, :] += g_b
                dc_ref[rows, :] += g_c
                ddt_s[rows, :] += g_dt
                dda_s[rows, :] += g_da
                return g_prev0, g_prev1

            _scan_loop(bstep, (jnp.zeros((128, 128), F32), jnp.zeros((128, 128), F32)), BWD_SCAN_UNROLL)
        g_raw, g_bias, g_alog = dt_vjp((ddt_s[...], dda_s[...]))
        ddt_ref[...] = g_raw
        pad7 = jnp.zeros((7, 128), F32)
        new_b = jnp.concatenate([g_bias, pad7], axis=0)
        new_a = jnp.concatenate([g_alog, pad7], axis=0)

        @pl.when(first)
        def _():
            ddtb_ref[...] = new_b
            dal_ref[...] = new_a
            dde_ref[...] = d_de
            dg_ref[...] = d_g

        @pl.when(jnp.logical_not(first))
        def _():
            ddtb_ref[...] += new_b
            dal_ref[...] += new_a
            dde_ref[...] += d_de
            dg_ref[...] += d_g

    return _hooked_call(
        body, grid=(2, B_LOC),
        in_specs=_ssd_scan_specs(**ONE_BUFFER) + [pl.BlockSpec((SEQ, GROUP_W), lambda g, b: (b, g), **ONE_BUFFER),
                                                  pl.BlockSpec((SEQ, GROUP_W), lambda g, b: (b, g), **ONE_BUFFER),
                                                  _ssd_state_spec(**ONE_BUFFER)],
        out_specs=[pl.BlockSpec((SEQ, GROUP_W), lambda g, b: (b, g)),
                   pl.BlockSpec((SEQ, 128), lambda g, b: (b, g)),
                   pl.BlockSpec((SEQ, 128), lambda g, b: (b, g)),
                   pl.BlockSpec((SEQ, GROUP_W), lambda g, b: (b, g)),
                   pl.BlockSpec((None, SEQ, 128), lambda g, b: (g, b, 0)),
                   pl.BlockSpec((None, 8, 128), lambda g, b: (g, 0, 0)),
                   pl.BlockSpec((None, 8, 128), lambda g, b: (g, 0, 0)),
                   pl.BlockSpec((1, GROUP_W), lambda g, b: (0, g)),
                   pl.BlockSpec((1, GROUP_W), lambda g, b: (0, g))],
        out_shape=[S_((T, SSD_INNER), F32), S_((T, 256), F32), S_((T, 256), F32), S_((T, SSD_INNER), BF16),
                   S_((2, T, 128), F32), S_((2, 8, 128), F32), S_((2, 8, 128), F32),
                   S_((1, SSD_INNER), F32), S_((1, SSD_INNER), F32)],
        scratch_shapes=[pltpu.VMEM((SEQ, 128), F32), pltpu.VMEM((SEQ, 128), F32), pltpu.VMEM((SEQ, GROUP_W), F32),
                        pltpu.VMEM((SEQ, 128), F32), pltpu.VMEM((SEQ, 128), F32)],
        args=(xc, xc, xc, proj, proj, dtb, alog, d_exp, gain, dy, ysum, states), hook=hook, name="ssd_scan_bwd")


GMLP_W = 256


def _gmlp_chunk_fn(gu, gv, v_gain, w0, w1, w2, w3, b_exp):
    u = jax.nn.gelu(gu)
    v = jax.nn.gelu(gv)
    v = v * lax.rsqrt(jnp.mean(v * v, axis=-1, keepdims=True) + NORM_EPS) * v_gain
    col = lax.broadcasted_iota(jnp.int32, (1, GMLP_W), 1) // HEAD
    mixed = b_exp
    for g, w in enumerate((w0, w1, w2, w3)):
        mixed = mixed + (col == g).astype(F32) * _dot(w, v)
    return u * mixed


def _gmlp_specs():
    return [pl.BlockSpec((SEQ, GMLP_W), lambda b: (b, C_GU // GMLP_W)),
            pl.BlockSpec((SEQ, GMLP_W), lambda b: (b, C_GV // GMLP_W)),
            pl.BlockSpec((1, GMLP_W), lambda b: (0, 0)),
            pl.BlockSpec((4, CHUNK, CHUNK), lambda b: (0, 0, 0)),
            pl.BlockSpec((CHUNK, GMLP_W), lambda b: (0, 0))]


def _gmlp_fwd(proj, v_gain, w_s, b_exp):
    def body(u_ref, v_ref, g_ref, w_ref, b_ref, o_ref):
        def step(c, carry):
            rows = _chunk_rows(c)
            o_ref[rows, :] = _gmlp_chunk_fn(u_ref[rows, :], v_ref[rows, :], g_ref[...], w_ref[0], w_ref[1], w_ref[2], w_ref[3],
                                            b_ref[...]).astype(BF16)
            return carry

        lax.fori_loop(0, N_CHUNK, step, 0)

    return pl.pallas_call(body, grid=(B_LOC,), in_specs=_gmlp_specs(),
                          out_specs=pl.BlockSpec((SEQ, GMLP_W), lambda b: (b, 0)),
                          out_shape=S_((T, GMLP_W), BF16), name="gmlp_fwd")(proj, proj, v_gain, w_s, b_exp)


def _gmlp_bwd(proj, v_gain, w_s, b_exp, dy):
    def body(u_ref, v_ref, g_ref, w_ref, b_ref, dy_ref, du_ref, dv_ref, dg_ref, dw_ref, db_ref):
        @pl.when(pl.program_id(0) == 0)
        def _():
            dg_ref[...] = jnp.zeros_like(dg_ref)
            dw_ref[...] = jnp.zeros_like(dw_ref)
            db_ref[...] = jnp.zeros_like(db_ref)

        def step(c, carry):
            rows = _chunk_rows(c)
            _, vjp = jax.vjp(_gmlp_chunk_fn, u_ref[rows, :], v_ref[rows, :], g_ref[...], w_ref[0], w_ref[1], w_ref[2], w_ref[3], b_ref[...])
            g_u, g_v, g_g, g_w0, g_w1, g_w2, g_w3, g_b = vjp(dy_ref[rows, :])
            du_ref[rows, :] = g_u.astype(BF16)
            dv_ref[rows, :] = g_v.astype(BF16)
            dg_ref[...] += g_g
            db_ref[...] += g_b
            for g, gw in enumerate((g_w0, g_w1, g_w2, g_w3)):
                dw_ref[g] += gw
            return carry

        lax.fori_loop(0, N_CHUNK, step, 0)

    blk = pl.BlockSpec((SEQ, GMLP_W), lambda b: (b, 0))
    return pl.pallas_call(
        body, grid=(B_LOC,),
        in_specs=_gmlp_specs() + [pl.BlockSpec((SEQ, GMLP_W), lambda b: (b, SSD_INNER // GMLP_W))],
        out_specs=[blk, blk, pl.BlockSpec((1, GMLP_W), lambda b: (0, 0)),
                   pl.BlockSpec((4, CHUNK, CHUNK), lambda b: (0, 0, 0)), pl.BlockSpec((CHUNK, GMLP_W), lambda b: (0, 0))],
        out_shape=[S_((T, GMLP_W), BF16), S_((T, GMLP_W), BF16), S_((1, GMLP_W), F32),
                   S_((4, CHUNK, CHUNK), F32), S_((CHUNK, GMLP_W), F32)],
        name="gmlp_bwd")(proj, proj, v_gain, w_s, b_exp, dy)


PAIR_W = 128
QB = 128
KW = QB + 2 * ATTN_HALF
N_QB = SEQ // QB
FWD_BLOCK_UNROLL = 4
BWD_BLOCK_UNROLL = 4
PAD_ROWS = SEQ + 2 * ATTN_HALF


def _qk_norm_fn(x, gain):
    ms = jnp.dot(x * x, _head_sum_matrix(PAIR_W), precision=SUM_PRECISION, preferred_element_type=F32) * (1.0 / HEAD)
    return x * lax.rsqrt(ms + NORM_EPS) * gain


def _deinterleave(dst_ref, src_ref, dil, offset):
    length = SEQ // dil
    if dil == 1:
        dst_ref[pl.ds(offset, SEQ), :] = src_ref[...]
        return
    for r in range(dil):
        dst_ref[pl.ds(offset + r * length, length), :] = src_ref[pl.ds(r, length, stride=dil), :]


def _interleave(dst_ref, src_ref, dil, offset):
    length = SEQ // dil
    if dil == 1:
        dst_ref[...] = src_ref[pl.ds(offset, SEQ), :]
        return
    for r in range(dil):
        dst_ref[pl.ds(r, length, stride=dil), :] = src_ref[pl.ds(offset + r * length, length), :]


def _edge_mask(blk, dil):
    length = SEQ // dil
    qi = blk * QB + lax.broadcasted_iota(jnp.int32, (QB, KW), 0)
    kj = blk * QB - ATTN_HALF + lax.broadcasted_iota(jnp.int32, (QB, KW), 1)
    return (kj >= 0) & (kj < SEQ) & ((qi // length) == (kj // length))


def _lane_is_head(hh):
    return (lax.broadcasted_iota(jnp.int32, (1, PAIR_W), 1) // HEAD) == hh


def _dilate_qkv(dil, qn_s, kn_s, v_ref, qd_s, kd_s, vd_s):
    _deinterleave(qd_s, qn_s, dil, 0)
    _deinterleave(kd_s, kn_s, dil, ATTN_HALF)
    _deinterleave(vd_s, v_ref, dil, ATTN_HALF)


def _attn_branch_fwd(br, dil, qn_s, kn_s, v_ref, bias_ref, qd_s, kd_s, vd_s, od_s, ld_s):
    _dilate_qkv(dil, qn_s, kn_s, v_ref, qd_s, kd_s, vd_s)

    def step(blk, carry):
        rows = pl.ds(pl.multiple_of(blk * QB, QB), QB)
        win = pl.ds(pl.multiple_of(blk * QB, QB), KW)
        qb, kw, vw = qd_s[rows, :], kd_s[win, :], vd_s[win, :]
        edge = _edge_mask(blk, dil)
        out, lse = 0.0, 0.0
        for hh in range(2):
            is_h = _lane_is_head(hh)
            s = _dot_nt(jnp.where(is_h, qb, 0.0), kw) * (HEAD ** -0.5) + bias_ref[br, hh]
            s = jnp.where(edge, s, NEG_INF)
            m = jnp.max(s, axis=-1, keepdims=True)
            l_h = m + jnp.log(jnp.sum(jnp.exp(s - m), axis=-1, keepdims=True))
            out = out + jnp.where(is_h, _dot(jnp.exp(s - l_h), vw), 0.0)
            lse = lse + jnp.where(is_h, l_h, 0.0)
        od_s[rows, :] = out
        ld_s[rows, :] = lse
        return carry

    _block_loop(step, FWD_BLOCK_UNROLL)


def _block_loop(step, unroll):
    def body(i, carry):
        for k in range(unroll):
            carry = step(i * unroll + k, carry)
        return carry

    lax.fori_loop(0, N_QB // unroll, body, 0)


def _attn_specs():
    col = lambda c0: (lambda p, b: (b, c0 // PAIR_W + p))
    return [pl.BlockSpec((SEQ, PAIR_W), col(C_Q)), pl.BlockSpec((SEQ, PAIR_W), col(C_K)), pl.BlockSpec((SEQ, PAIR_W), col(C_V)),
            pl.BlockSpec((1, PAIR_W), lambda p, b: (0, 0)), pl.BlockSpec((1, PAIR_W), lambda p, b: (0, 0)),
            pl.BlockSpec((3, 2, QB, KW), lambda p, b: (0, p, 0, 0))]


def _attn_scratch():
    seq = pltpu.VMEM((SEQ, PAIR_W), F32)
    pad = pltpu.VMEM((PAD_ROWS, PAIR_W), F32)
    return [seq, seq, seq, pad, pad, seq, seq]


def _zero_pads(*refs):
    for ref in refs:
        ref[pl.ds(0, ATTN_HALF), :] = jnp.zeros((ATTN_HALF, PAIR_W), F32)
        ref[pl.ds(ATTN_HALF + SEQ, ATTN_HALF), :] = jnp.zeros((ATTN_HALF, PAIR_W), F32)


ROW_STEP = 256


def _row_steps(fn, init=0):
    return lax.fori_loop(0, SEQ // ROW_STEP, lambda i, c: fn(pl.ds(pl.multiple_of(i * ROW_STEP, ROW_STEP), ROW_STEP), c), init)


def _interleave_add(acc_ref, src_ref, dil, offset):
    length = SEQ // dil
    if dil == 1:
        acc_ref[...] += src_ref[pl.ds(offset, SEQ), :]
        return
    for r in range(dil):
        acc_ref[pl.ds(r, length, stride=dil), :] += src_ref[pl.ds(offset + r * length, length), :]


def _attn_norm_qk(q_ref, k_ref, qg_ref, kg_ref, qn_s, kn_s):
    def norm(rows, carry):
        qn_s[rows, :] = _qk_norm_fn(q_ref[rows, :], qg_ref[...])
        kn_s[rows, :] = _qk_norm_fn(k_ref[rows, :], kg_ref[...])
        return carry

    _row_steps(norm)


def _attn_forward_all(q_ref, k_ref, v_ref, qg_ref, kg_ref, bias_ref, qn_s, kn_s, qd_s, kd_s, vd_s, od_s, ld_s, on_s, ln_s):
    _attn_norm_qk(q_ref, k_ref, qg_ref, kg_ref, qn_s, kn_s)
    _zero_pads(kd_s, vd_s)
    for br, dil in enumerate(ATTN_DILS):
        _attn_branch_fwd(br, dil, qn_s, kn_s, v_ref, bias_ref, qd_s, kd_s, vd_s, od_s, ld_s)
        _interleave(on_s.at[br], od_s, dil, 0)
        _interleave(ln_s.at[br], ld_s, dil, 0)


def _merge_weights(ln_s, rows):
    l0, l1, l2 = ln_s[0, rows, :], ln_s[1, rows, :], ln_s[2, rows, :]
    m = jnp.maximum(jnp.maximum(l0, l1), l2)
    e = [jnp.exp(l0 - m), jnp.exp(l1 - m), jnp.exp(l2 - m)]
    den = e[0] + e[1] + e[2]
    return [e[0] / den, e[1] / den, e[2] / den]


def _attn_fwd(proj, q_gain, k_gain, bias, hook=None):
    def body(q_ref, k_ref, v_ref, qg_ref, kg_ref, bias_ref, o_ref, on_s, ln_s, qn_s, kn_s, qd_s, kd_s, vd_s, od_s, ld_s):
        _attn_forward_all(q_ref, k_ref, v_ref, qg_ref, kg_ref, bias_ref, qn_s, kn_s, qd_s, kd_s, vd_s, od_s, ld_s, on_s, ln_s)

        def merge(rows, carry):
            w = _merge_weights(ln_s, rows)
            o_ref[rows, :] = (w[0] * on_s[0, rows, :] + w[1] * on_s[1, rows, :] + w[2] * on_s[2, rows, :]).astype(BF16)
            return carry

        _row_steps(merge)

    kept = pl.BlockSpec((3, SEQ, PAIR_W), lambda p, b: (0, b, p))
    return _hooked_call(body, grid=(2, B_LOC), in_specs=_attn_specs(),
                        out_specs=[pl.BlockSpec((SEQ, PAIR_W), lambda p, b: (b, p)), kept, kept],
                        out_shape=[S_((T, 2 * PAIR_W), BF16), S_((3, T, 2 * PAIR_W), F32), S_((3, T, 2 * PAIR_W), F32)],
                        scratch_shapes=_attn_scratch(), args=(proj, proj, proj, q_gain, k_gain, bias), hook=hook, name="attn_fwd")


def _attn_bwd(proj, q_gain, k_gain, bias, dy, kept_o, kept_l, hook=None):
    def body(q_ref, k_ref, v_ref, qg_ref, kg_ref, bias_ref, dy_ref, on_ref, ln_ref,
             dq_ref, dk_ref, dv_ref, dqg_ref, dkg_ref, dbias_ref,
             qn_s, kn_s, qd_s, kd_s, vd_s, od_s, ld_s, don_s, dln_s, dod_s, dld_s, dqd_s, dkd_s, dvd_s, dqn_s, dkn_s, dvn_s):
        first = pl.program_id(1) == 0
        _attn_norm_qk(q_ref, k_ref, qg_ref, kg_ref, qn_s, kn_s)
        _zero_pads(kd_s, vd_s)

        def clear_acc(rows, carry):
            dqn_s[rows, :] = jnp.zeros((ROW_STEP, PAIR_W), F32)
            dkn_s[rows, :] = jnp.zeros((ROW_STEP, PAIR_W), F32)
            dvn_s[rows, :] = jnp.zeros((ROW_STEP, PAIR_W), F32)
            return carry

        _row_steps(clear_acc)

        @pl.when(first)
        def _():
            dbias_ref[...] = jnp.zeros_like(dbias_ref)

        def merge_bwd(rows, carry):
            w = _merge_weights(ln_ref, rows)
            dy = dy_ref[rows, :]
            same_head = _head_sum_matrix(PAIR_W)
            dws = [jnp.dot(dy * on_ref[j, rows, :], same_head, precision=SUM_PRECISION, preferred_element_type=F32) for j in range(3)]
            dbar = w[0] * dws[0] + w[1] * dws[1] + w[2] * dws[2]
            for j in range(3):
                don_s[j, rows, :] = w[j] * dy
                dln_s[j, rows, :] = w[j] * (dws[j] - dbar)
            return carry

        _row_steps(merge_bwd)
        for br, dil in enumerate(ATTN_DILS):
            _dilate_qkv(dil, qn_s, kn_s, v_ref, qd_s, kd_s, vd_s)
            _deinterleave(od_s, on_ref.at[br], dil, 0)
            _deinterleave(ld_s, ln_ref.at[br], dil, 0)
            _deinterleave(dod_s, don_s.at[br], dil, 0)
            _deinterleave(dld_s, dln_s.at[br], dil, 0)

            def clear(rows, carry):
                dkd_s[rows, :] = jnp.zeros((ROW_STEP, PAIR_W), F32)
                dvd_s[rows, :] = jnp.zeros((ROW_STEP, PAIR_W), F32)
                return carry

            _row_steps(clear)
            tail = pl.ds(SEQ, 2 * ATTN_HALF)
            dkd_s[tail, :] = jnp.zeros((2 * ATTN_HALF, PAIR_W), F32)
            dvd_s[tail, :] = jnp.zeros((2 * ATTN_HALF, PAIR_W), F32)

            def step(blk, carry, br=br, dil=dil):
                rows = pl.ds(pl.multiple_of(blk * QB, QB), QB)
                win = pl.ds(pl.multiple_of(blk * QB, QB), KW)
                qb, kw, vw = qd_s[rows, :], kd_s[win, :], vd_s[win, :]
                do_b, dl_b, o_b, l_b = dod_s[rows, :], dld_s[rows, :], od_s[rows, :], ld_s[rows, :]
                edge = _edge_mask(blk, dil)
                dq, dk, dv = 0.0, 0.0, 0.0
                for hh in range(2):
                    is_h = _lane_is_head(hh)
                    pick = (lax.broadcasted_iota(jnp.int32, (1, PAIR_W), 1) == hh * HEAD).astype(F32)
                    q_h = jnp.where(is_h, qb, 0.0)
                    do_h = jnp.where(is_h, do_b, 0.0)
                    s = _dot_nt(q_h, kw) * (HEAD ** -0.5) + bias_ref[br, hh]
                    s = jnp.where(edge, s, NEG_INF)
                    p = jnp.exp(s - jnp.sum(l_b * pick, axis=-1, keepdims=True))
                    dp = _dot_nt(do_h, vw)
                    delta = jnp.sum(do_h * o_b, axis=-1, keepdims=True)
                    ds = p * (dp - delta + jnp.sum(dl_b * pick, axis=-1, keepdims=True))
                    dbias_ref[br, hh] += ds
                    dq = dq + jnp.where(is_h, _dot(ds, kw), 0.0) * (HEAD ** -0.5)
                    dk = dk + _dot_tn(ds, q_h) * (HEAD ** -0.5)
                    dv = dv + _dot_tn(p, do_h)
                dqd_s[rows, :] = dq
                dkd_s[win, :] += dk
                dvd_s[win, :] += dv
                return carry

            _block_loop(step, BWD_BLOCK_UNROLL)
            _interleave_add(dqn_s, dqd_s, dil, 0)
            _interleave_add(dkn_s, dkd_s, dil, ATTN_HALF)
            _interleave_add(dvn_s, dvd_s, dil, ATTN_HALF)

        def norm_bwd(rows, carry):
            _, q_vjp = jax.vjp(_qk_norm_fn, q_ref[rows, :], qg_ref[...])
            _, k_vjp = jax.vjp(_qk_norm_fn, k_ref[rows, :], kg_ref[...])
            g_q, g_qg = q_vjp(dqn_s[rows, :])
            g_k, g_kg = k_vjp(dkn_s[rows, :])
            dq_ref[rows, :] = g_q.astype(BF16)
            dk_ref[rows, :] = g_k.astype(BF16)
            dv_ref[rows, :] = dvn_s[rows, :].astype(BF16)
            return carry[0] + g_qg, carry[1] + g_kg

        g_qg, g_kg = _row_steps(norm_bwd, (jnp.zeros((1, PAIR_W), F32), jnp.zeros((1, PAIR_W), F32)))
        pad7 = jnp.zeros((7, PAIR_W), F32)
        new_q = jnp.concatenate([g_qg, pad7], axis=0)
        new_k = jnp.concatenate([g_kg, pad7], axis=0)

        @pl.when(first)
        def _():
            dqg_ref[...] = new_q
            dkg_ref[...] = new_k

        @pl.when(jnp.logical_not(first))
        def _():
            dqg_ref[...] += new_q
            dkg_ref[...] += new_k

    seq = pltpu.VMEM((SEQ, PAIR_W), F32)
    seq3 = pltpu.VMEM((3, SEQ, PAIR_W), F32)
    pad = pltpu.VMEM((PAD_ROWS, PAIR_W), F32)
    kept = pl.BlockSpec((3, SEQ, PAIR_W), lambda p, b: (0, b, p))
    out_blk = pl.BlockSpec((SEQ, PAIR_W), lambda p, b: (b, p))
    gain_blk = pl.BlockSpec((None, 8, PAIR_W), lambda p, b: (p, 0, 0))
    return _hooked_call(
        body, grid=(2, B_LOC),
        in_specs=_attn_specs() + [pl.BlockSpec((SEQ, PAIR_W), lambda p, b: (b, (SSD_INNER + GMLP_W) // PAIR_W + p)), kept, kept],
        out_specs=[out_blk, out_blk, out_blk, gain_blk, gain_blk, pl.BlockSpec((3, 2, QB, KW), lambda p, b: (0, p, 0, 0))],
        out_shape=[S_((T, 2 * PAIR_W), BF16)] * 3 + [S_((2, 8, PAIR_W), F32)] * 2 + [S_((3, 4, QB, KW), F32)],
        scratch_shapes=_attn_scratch() + [seq3, seq3, seq, seq, seq, pad, pad, seq, seq, seq],
        args=(proj, proj, proj, q_gain, k_gain, bias, dy, kept_o, kept_l), hook=hook, name="attn_bwd")


def _rel_bucket(rel):
    nb = 16
    max_exact = nb // 2
    n = jnp.abs(rel)
    large = max_exact + (jnp.log(jnp.maximum(n, 1).astype(F32) / max_exact) / math.log(1024 / max_exact) * (nb - max_exact)).astype(jnp.int32)
    large = jnp.minimum(large, nb - 1)
    return jnp.where(rel > 0, nb, 0) + jnp.where(n < max_exact, n, large)


def _attn_bias(rel_table):
    rel = jnp.arange(KW)[None, :] - ATTN_HALF - jnp.arange(QB)[:, None]
    inside = (jnp.abs(rel) <= ATTN_HALF)
    out = []
    for dil in ATTN_DILS:
        one_hot = (_rel_bucket(rel * dil)[None] == jnp.arange(32)[:, None, None]).astype(F32)
        b = jnp.einsum("kh,kts->hts", rel_table, one_hot, precision=HI)
        out.append(jnp.where(inside[None], b, NEG_INF))
    return jnp.stack(out).astype(F32)


def _place():
    return lax.axis_index("x"), lax.axis_index("y"), lax.axis_index("c")


def _allgather8(buf, name):
    rows = buf.shape[0]
    flips = [(fx, fy, fc) for fx in (0, 1) for fy in (0, 1) for fc in (0, 1)][1:]

    def body(in_ref, out_ref, send_sems, recv_sems):
        x, y, c = _place()
        me = 4 * x + 2 * y + c
        peers = [(1 - x if fx else x, 1 - y if fy else y, 1 - c if fc else c) for fx, fy, fc in flips]

        def copy(k, slot, peer):
            return pltpu.make_async_remote_copy(src_ref=in_ref, dst_ref=out_ref.at[slot], send_sem=send_sems.at[k],
                                                recv_sem=recv_sems.at[k], device_id=peer, device_id_type=MESH)

        sends = [copy(k, me, peer) for k, peer in enumerate(peers)]
        for cp in sends:
            cp.start()
        for k, (px, py, pc) in enumerate(peers):
            copy(k, 4 * px + 2 * py + pc, (px, py, pc)).wait_recv()
        for cp in sends:
            cp.wait_send()

    slots = pl.pallas_call(body, in_specs=[ANY], out_specs=ANY, out_shape=S_((N_DEV, rows, 128), F32),
                           scratch_shapes=[pltpu.SemaphoreType.DMA((7,)), pltpu.SemaphoreType.DMA((7,))], name=name)(buf)
    x, y, c = _place()
    return lax.dynamic_update_index_in_dim(slots, buf, 4 * x + 2 * y + c, axis=0)


N_BIG = 4


def _other_chips(x, y):
    return [(1 - x, y), (x, 1 - y), (1 - x, 1 - y)]


def _hooked_call(body, *, grid, in_specs, out_specs, out_shape, scratch_shapes, args, hook, name):
    if hook is None:
        res = pl.pallas_call(body, grid=grid, in_specs=in_specs, out_specs=out_specs, out_shape=out_shape,
                             scratch_shapes=scratch_shapes, compiler_params=_cp(), name=name)(*args)
        return res, None
    counts = (len(in_specs), len(hook["arrays"]), len(out_specs), len(hook["out_shape"]), len(scratch_shapes), len(hook["sems"]))

    def wrapped(*refs):
        groups, pos = [], 0
        for n in counts:
            groups.append(refs[pos:pos + n])
            pos += n
        ins, h_ins, outs, h_outs, scr, sems = groups
        idx = [pl.program_id(a) for a in range(len(grid))]
        first = functools.reduce(jnp.logical_and, [i == 0 for i in idx])
        last = functools.reduce(jnp.logical_and, [i == g - 1 for i, g in zip(idx, grid)])

        @pl.when(first)
        def _():
            hook["start"](h_ins, h_outs, sems)

        body(*ins, *outs, *scr)

        @pl.when(last)
        def _():
            hook["finish"](h_ins, h_outs, sems)

    res = pl.pallas_call(wrapped, grid=grid, in_specs=list(in_specs) + [ANY] * counts[1], out_specs=list(out_specs) + [ANY] * counts[3],
                         out_shape=list(out_shape) + list(hook["out_shape"]), scratch_shapes=list(scratch_shapes) + list(hook["sems"]),
                         compiler_params=_cp(), name=name + "_" + hook["name"])(*args, *hook["arrays"])
    return res[:counts[2]], res[counts[2]:]


def _run_hook(hook):
    n_in, n_out = len(hook["arrays"]), len(hook["out_shape"])

    def body(*refs):
        h_ins, h_outs, sems = refs[:n_in], refs[n_in:n_in + n_out], refs[n_in + n_out:]
        hook["start"](h_ins, h_outs, sems)
        hook["finish"](h_ins, h_outs, sems)

    return pl.pallas_call(body, in_specs=[ANY] * n_in, out_specs=[ANY] * n_out, out_shape=list(hook["out_shape"]),
                          scratch_shapes=list(hook["sems"]), name=hook["name"])(*hook["arrays"])


def _remote(src, dst, send_sem, recv_sem, peer):
    return pltpu.make_async_remote_copy(src_ref=src, dst_ref=dst, send_sem=send_sem, recv_sem=recv_sem, device_id=peer, device_id_type=MESH)


def _gather_hook(shards):
    def copies(h_ins, h_outs, sems, kind):
        ici_send, ici_recv, d2d_send, d2d_recv = sems
        x, y, c = _place()
        chip = 2 * x + y
        out = []
        for t in range(len(shards)):
            half = shards[t].shape[0] // 2
            mine_r, other_r = pl.ds(c * half, half), pl.ds((1 - c) * half, half)
            for f, (px, py) in enumerate(_other_chips(x, y)):
                k, peer_chip = 3 * t + f, 2 * px + py
                if kind in ("send", "land"):
                    slot = chip if kind == "send" else peer_chip
                    out.append(_remote(h_ins[t].at[mine_r], h_outs[t].at[slot, mine_r], ici_send.at[k], ici_recv.at[k], (px, py, c)))
                else:
                    rows = mine_r if kind == "pass" else other_r
                    out.append(_remote(h_outs[t].at[peer_chip, rows], h_outs[t].at[peer_chip, rows], d2d_send.at[k], d2d_recv.at[k],
                                       (x, y, 1 - c)))
        return out

    def start(h_ins, h_outs, sems):
        for cp in copies(h_ins, h_outs, sems, "send"):
            cp.start()

    def finish(h_ins, h_outs, sems):
        passed = copies(h_ins, h_outs, sems, "pass")
        for landed, forward in zip(copies(h_ins, h_outs, sems, "land"), passed):
            landed.wait_recv()
            forward.start()
        for cp in copies(h_ins, h_outs, sems, "get"):
            cp.wait_recv()
        for cp in copies(h_ins, h_outs, sems, "send") + passed:
            cp.wait_send()

    return dict(name="gather", arrays=list(shards), out_shape=[S_((N_CHIPS,) + s.shape, s.dtype) for s in shards],
                sems=[pltpu.SemaphoreType.DMA((3 * len(shards),)) for _ in range(4)], start=start, finish=finish)


def _to_sibling_hook(parts, half_rows=False):
    def copies(h_ins, h_outs, sems):
        x, y, c = _place()
        out = []
        for t, p in enumerate(parts):
            src = h_ins[t].at[:, pl.ds((1 - c) * (p.shape[1] // 2), p.shape[1] // 2)] if half_rows else h_ins[t]
            out.append(_remote(src, h_outs[t], sems[0].at[t], sems[1].at[t], (x, y, 1 - c)))
        return out

    def start(h_ins, h_outs, sems):
        for cp in copies(h_ins, h_outs, sems):
            cp.start()

    def finish(h_ins, h_outs, sems):
        cps = copies(h_ins, h_outs, sems)
        for cp in cps:
            cp.wait_recv()
        for cp in cps:
            cp.wait_send()

    shapes = [(p.shape[0], p.shape[1] // 2, p.shape[2]) if half_rows else p.shape for p in parts]
    return dict(name="to_sibling", arrays=list(parts), out_shape=[S_(s, p.dtype) for s, p in zip(shapes, parts)],
                sems=[pltpu.SemaphoreType.DMA((len(parts),)), pltpu.SemaphoreType.DMA((len(parts),))], start=start, finish=finish)


def _to_chips_hook(parts):
    def copies(h_ins, h_outs, sems):
        x, y, c = _place()
        return [_remote(h_ins[t].at[2 * px + py], h_outs[t].at[f], sems[0].at[3 * t + f], sems[1].at[3 * t + f], (px, py, c))
                for t in range(len(parts)) for f, (px, py) in enumerate(_other_chips(x, y))]

    def start(h_ins, h_outs, sems):
        for cp in copies(h_ins, h_outs, sems):
            cp.start()

    def finish(h_ins, h_outs, sems):
        cps = copies(h_ins, h_outs, sems)
        for cp in cps:
            cp.wait_recv()
        for cp in cps:
            cp.wait_send()

    return dict(name="to_chips", arrays=list(parts), out_shape=[S_((3,) + p.shape[1:], p.dtype) for p in parts],
                sems=[pltpu.SemaphoreType.DMA((3 * len(parts),)), pltpu.SemaphoreType.DMA((3 * len(parts),))], start=start, finish=finish)


def _add_pair(a, b, core, name):
    n, half, c = b.shape

    def body(core_ref, a_ref, b_ref, o_ref):
        o_ref[...] = (a_ref[...].astype(F32) + b_ref[...].astype(F32)).astype(BF16)

    spec = pltpu.PrefetchScalarGridSpec(
        num_scalar_prefetch=1, grid=(n,),
        in_specs=[pl.BlockSpec((None, half, c), lambda i, core_ref: (i, core_ref[0], 0)),
                  pl.BlockSpec((None, half, c), lambda i, core_ref: (i, 0, 0))],
        out_specs=pl.BlockSpec((None, half, c), lambda i, core_ref: (i, 0, 0)))
    return pl.pallas_call(body, grid_spec=spec, out_shape=S_(b.shape, BF16), name=name)(core.reshape(1).astype(jnp.int32), a, b)


def _add_four(own, got, rows, name):
    n, r, c = own.shape

    def body(a_ref, g_ref, o_ref):
        o_ref[...] = ((a_ref[...].astype(F32) + g_ref[0].astype(F32)) + g_ref[1].astype(F32)) + g_ref[2].astype(F32)

    blk = pl.BlockSpec((None, rows, c), lambda i, j: (i, j, 0))
    return pl.pallas_call(body, grid=(n, r // rows), in_specs=[blk, pl.BlockSpec((3, None, rows, c), lambda i, j: (0, i, j, 0))],
                          out_specs=blk, out_shape=S_(own.shape, F32), name=name)(own, got)


def _sum_slots(slots):
    n, rows = slots.shape[:2]

    def body(s_ref, o_ref):
        tot = s_ref[0]
        for k in range(1, n):
            tot = tot + s_ref[k]
        o_ref[...] = tot

    return pl.pallas_call(body, out_shape=S_((rows, 128), F32), name="sum_slots")(slots)


def _add_two(a, b):
    def body(a_ref, b_ref, o_ref):
        o_ref[...] = a_ref[...] + b_ref[...]

    return pl.pallas_call(body, out_shape=S_(a.shape, a.dtype), name="add_two")(a, b)


def _allreduce_small(buf, chip):
    (theirs,) = _run_hook(dict(_to_sibling_hook([buf]), name="small_to_sibling"))
    pair = _add_two(buf, theirs)
    (slots,) = _run_hook(dict(_gather_hook([pair]), name="small_gather"))
    return _sum_slots(lax.dynamic_update_index_in_dim(slots, pair, chip, axis=0))


SMALL = ("mix_norm_gain", "ssd_conv_w", "ssd_conv_b", "ssd_dt_bias", "ssd_a_log", "ssd_d", "ssd_out_gain", "gmlp_v_gain",
         "gmlp_w_s", "gmlp_b_s", "attn_q_gain", "attn_k_gain", "rel_bias_table", "ffn_norm_gain", "ffn_conv_w", "ffn_conv_b")
BIG = ("w_in", "w_out", "ffn_w_up", "ffn_w_down")
WEIGHTS = ("mix_norm_gain", "w_in", "ssd_conv_w", "ssd_conv_b", "ssd_dt_bias", "ssd_a_log", "ssd_d", "ssd_out_gain", "gmlp_v_gain",
           "gmlp_w_s", "gmlp_b_s", "attn_q_gain", "attn_k_gain", "rel_bias_table", "w_out", "ffn_norm_gain", "ffn_w_up",
           "ffn_conv_w", "ffn_conv_b", "ffn_w_down")
ADAM_ROWS = {"w_in": 512, "w_out": 128, "ffn_w_up": 256, "ffn_w_down": 352}


PACK_ROWS = 64


def _packed_rows(shape):
    return -(-int(np.prod(shape)) // 1024) * 8


def _pack(arrays):
    parts = []
    for a in arrays:
        rows = _packed_rows(a.shape)
        flat = a.reshape(-1).astype(F32)
        parts.append(jnp.pad(flat, (0, rows * 128 - flat.shape[0])).reshape(rows, 128))
    total = sum(p.shape[0] for p in parts)
    tail = -total % PACK_ROWS
    if tail:
        parts.append(jnp.zeros((tail, 128), F32))
    return jnp.concatenate(parts, axis=0)


def _unpack(buf, shapes):
    out, row = [], 0
    for s in shapes:
        rows, n = _packed_rows(s), int(np.prod(s))
        out.append(buf[row:row + rows].reshape(-1)[:n].reshape(s))
        row += rows
    return out


def _perm_cols(w):
    pad = jnp.zeros(w.shape[:-1] + (NP - IN_WIDTH,), w.dtype)
    return jnp.concatenate([w[..., :1536], w[..., 1552:], w[..., 1536:1552], pad], axis=-1)


def _unperm_cols(w):
    return jnp.concatenate([w[..., :1536], w[..., C_DT:C_DT + 16], w[..., 1536:C_DT]], axis=-1)


def _layer_params(l, p, conv5_w, conv3_w, bias):
    def make(mix_g, conv5, conv5_b, dt_bias, a_log, d_skip, out_gain, v_gain, w_s, b_s, q_gain, k_gain, ffn_g, conv3, conv3_b):
        lanes = lambda a: jnp.pad(a.reshape(1, 16), ((0, 0), (0, 112)))
        cw3 = jnp.pad(jnp.transpose(conv3.reshape(3, 2, FFN_DIM), (1, 0, 2)), ((0, 0), (0, 5), (0, 0)))
        return dict(mix_g=mix_g.reshape(1, D_MODEL), cw5=jnp.pad(conv5, ((0, 3), (0, 0))), cb5=conv5_b.reshape(1, SSD_XBC),
                    dtb=lanes(dt_bias), alog=lanes(a_log), d_exp=jnp.repeat(d_skip, HEAD).reshape(1, SSD_INNER),
                    out_gain=out_gain.reshape(1, SSD_INNER), v_gain=v_gain.reshape(1, GMLP_W), w_s=w_s,
                    b_exp=jnp.repeat(b_s.T, HEAD, axis=1), q_gain=jnp.tile(q_gain, 2).reshape(1, PAIR_W),
                    k_gain=jnp.tile(k_gain, 2).reshape(1, PAIR_W), ffn_g=ffn_g.reshape(1, D_MODEL), cw3=cw3,
                    cb3=conv3_b.reshape(2, 1, FFN_DIM))

    args = (p["mix_norm_gain"][l], conv5_w[l], p["ssd_conv_b"][l], p["ssd_dt_bias"][l], p["ssd_a_log"][l], p["ssd_d"][l],
            p["ssd_out_gain"][l], p["gmlp_v_gain"][l], p["gmlp_w_s"][l], p["gmlp_b_s"][l], p["attn_q_gain"][l], p["attn_k_gain"][l],
            p["ffn_norm_gain"][l], conv3_w[l], p["ffn_conv_b"][l])
    return jax.vjp(make, *args)


def _forward_layer(x, h, lp, w, bias, next_gain=None, hooks=None, resolve=None):
    hooks = hooks or {}
    proj = _mm_nn(h, w["w_in"], tm=1024, tn=1024, tk=1024, out_dtype=F32, name="mm_proj")
    xc = _ssd_pre_fwd(proj, lp["cw5"], lp["cb5"])
    (y_ssd, ssd_sum, ssd_states), got_self = _ssd_scan_fwd(xc, proj, lp["dtb"], lp["alog"], lp["d_exp"], lp["out_gain"],
                                                           hook=hooks.get("self"))
    if got_self is not None:
        w = dict(w, **resolve(got_self))
    y_gmlp = _gmlp_fwd(proj, lp["v_gain"], lp["w_s"], lp["b_exp"])
    (y_attn, attn_o, attn_l), got_attn = _attn_fwd(proj, lp["q_gain"], lp["k_gain"], bias, hook=hooks.get("attn"))
    y = jnp.concatenate([y_ssd, y_gmlp, y_attn], axis=1)
    x2, hn = _mm_nn(y, w["w_out"], tm=1024, tn=1024, tk=1024, out_dtype=F32, res=x, norm_gain=lp["ffn_g"], name="mm_out")
    up3 = _mm_up(hn, w["ffn_w_up"])
    act, got_gate = _convgate_fwd(up3, lp["cw3"], lp["cb3"], hook=hooks.get("gate"))
    if next_gain is None:
        x3, h_next = _mm_nn(act, w["ffn_w_down"], tm=1024, tn=1024, tk=HALF_TILE, out_dtype=F32, res=x2, name="mm_down"), None
    else:
        x3, h_next = _mm_nn(act, w["ffn_w_down"], tm=1024, tn=1024, tk=HALF_TILE, out_dtype=F32, res=x2, norm_gain=next_gain,
                            name="mm_down_norm")
    saved = dict(x=x, h=h, proj=proj, xc=xc, y=y, x2=x2, hn=hn, up3=up3, act=act, attn_o=attn_o, attn_l=attn_l,
                 ssd_sum=ssd_sum, ssd_states=ssd_states)
    return x3, h_next, saved, w, dict(attn=got_attn, gate=got_gate)


def _backward_layer(dx3, sv, lp, w, bias, pending=None, reducer=None):
    d_act = _mm_nt(dx3, w["ffn_w_down"], tm=1024, tn=HALF_TILE, tk=1024, out_dtype=F32, name="mm_dact")
    dw_down = _mm_tn(sv["act"], dx3, tm=HALF_TILE, tn=1024, tk=1024, out_dtype=BF16, name="mm_dwdown")
    (dup3, dcw3), from_sibling = _convgate_bwd(sv["up3"], lp["cw3"], lp["cb3"], d_act, hook=pending.sibling_hook() if pending else None)
    if pending:
        pending.add_sibling(from_sibling)
    dx2, d_ffn_g = _mm_dhn(dup3, w["ffn_w_up"], sv["x2"], lp["ffn_g"], dx3)
    dw_up = _mm_dwup(sv["hn"], dup3)
    d_y = _mm_nt(dx2, w["w_out"], tm=1024, tn=1024, tk=1024, out_dtype=F32, name="mm_dy")
    dw_out = _mm_tn(sv["y"], dx2, tm=1024, tn=1024, tk=1024, out_dtype=BF16, name="mm_dwout")
    early = reducer(("w_out", "ffn_w_up", "ffn_w_down"), (dw_out, dw_up, dw_down)) if reducer else None
    proj, xc = sv["proj"], sv["xc"]
    (dxs, dbc, dcc, dz, ddt2, ddtb2, dal2, d_dexp, d_outg), from_chips = _ssd_scan_bwd(
        xc, proj, lp["dtb"], lp["alog"], lp["d_exp"], lp["out_gain"], d_y, sv["ssd_sum"], sv["ssd_states"],
        hook=pending.chips_hook() if pending else None)
    if pending:
        pending.add_chips(from_chips)
    (d_xbc, dcw5), from_sibling = _ssd_pre_bwd(proj, lp["cw5"], lp["cb5"], jnp.concatenate([dxs, dbc, dcc], axis=1),
                                               hook=early.sibling_hook() if early else None)
    if early:
        early.add_sibling(from_sibling)
    d_gu, d_gv, d_vg, d_ws, d_bexp = _gmlp_bwd(proj, lp["v_gain"], lp["w_s"], lp["b_exp"], d_y)
    (d_q, d_k, d_v, d_qg2, d_kg2, d_bias), from_chips = _attn_bwd(proj, lp["q_gain"], lp["k_gain"], bias, d_y, sv["attn_o"], sv["attn_l"],
                                                                  hook=early.chips_hook() if early else None)
    if early:
        early.add_chips(from_chips)
    d_dt = (ddt2[0] + ddt2[1]).astype(BF16)
    d_proj = jnp.concatenate([d_xbc, dz, d_gu, d_gv, d_q, d_k, d_v, d_dt, jnp.zeros((T, NP - C_DT - 128), BF16)], axis=1)
    dx, d_mix_g = _mm_nt(d_proj, w["w_in"], tm=1024, tn=1024, tk=1024, out_dtype=F32, res=dx2, norm_bwd=(sv["x"], lp["mix_g"]), name="mm_dh")
    dw_in = _mm_tn(sv["h"], d_proj, tm=1024, tn=1024, tk=1024, out_dtype=BF16, name="mm_dwin")
    late = None
    if reducer:
        late = reducer(("w_in",), (dw_in,))
        late.run_alone()
    d_lp = dict(mix_g=d_mix_g, cw5=dcw5[:8] * (jnp.arange(8) < 5)[:, None].astype(F32), cb5=dcw5[5:6],
                dtb=(ddtb2[0, :1] + ddtb2[1, :1]), alog=(dal2[0, :1] + dal2[1, :1]), d_exp=d_dexp, out_gain=d_outg,
                v_gain=d_vg, w_s=d_ws, b_exp=d_bexp, q_gain=d_qg2[0, :1] + d_qg2[1, :1], k_gain=d_kg2[0, :1] + d_kg2[1, :1],
                ffn_g=d_ffn_g, cw3=dcw3 * (jnp.arange(8) < 3)[None, :, None].astype(F32), cb3=dcw3[:, 3:4])
    return dx, dict(w_in=dw_in, w_out=dw_out, ffn_w_up=dw_up, ffn_w_down=dw_down), d_lp, d_bias, (early, late)


def _to_shard_major(name, dw):
    if name == "ffn_w_up":
        return dw
    if name == "w_in":
        r, c = dw.shape[0], IN_WIDTH
        return jnp.transpose(_unperm_cols(dw).reshape(r, N_CHIPS, c // N_CHIPS), (1, 0, 2))
    r, c = dw.shape
    return dw.reshape(N_CHIPS, r // N_CHIPS, c)


def _whole_weight(name, gathered, own, chip):
    if name == "w_in":
        return _perm_cols(jnp.concatenate([jnp.where(chip == k, own, gathered[k]) for k in range(N_CHIPS)], axis=1))
    w = lax.dynamic_update_index_in_dim(gathered, own, chip, axis=0)
    return w if name == "ffn_w_up" else w.reshape(N_CHIPS * own.shape[0], own.shape[1])


class _LayerReduce:
    def __init__(self, names, dws, chip, core):
        self.names, self.chip, self.core = names, chip, core
        self.parts = [_to_shard_major(n, dw) for n, dw in zip(names, dws)]

    def sibling_hook(self):
        return _to_sibling_hook(self.parts, half_rows=True)

    def add_sibling(self, got):
        self.sums = [_add_pair(a, b, self.core, "add_pair_" + n) for n, a, b in zip(self.names, self.parts, got)]

    def chips_hook(self):
        return _to_chips_hook(self.sums)

    def add_chips(self, got):
        self.half = {}
        for n, s2, g3 in zip(self.names, self.sums, got):
            own = lax.dynamic_index_in_dim(s2, self.chip, axis=0, keepdims=True)
            self.half[n] = _add_four(own, g3[:, None], own.shape[1], "add_four_" + n)[0]

    def run_alone(self):
        self.add_sibling(_run_hook(self.sibling_hook()))
        self.add_chips(_run_hook(self.chips_hook()))


LAYER_SMALL = ("mix_norm_gain", "ssd_conv_w", "ssd_conv_b", "ssd_dt_bias", "ssd_a_log", "ssd_d", "ssd_out_gain", "gmlp_v_gain",
               "gmlp_w_s", "gmlp_b_s", "attn_q_gain", "attn_k_gain", "ffn_norm_gain", "ffn_conv_w", "ffn_conv_b")


def _local_grads(x, loss_target, p, conv5_w, conv3_w, layer_w, exchange=None):
    bias, bias_vjp = jax.vjp(_attn_bias, p["rel_bias_table"])
    xt = x.reshape(T, D_MODEL)
    layer_w = list(layer_w)
    saved, lps, lp_vjps = [], [], []
    if exchange is not None:
        chip, core, own = exchange
        whole = lambda names, layer, gathered: {n: _whole_weight(n, g, own[layer][BIG.index(n)], chip) for n, g in zip(names, gathered)}
    for l in range(DEPTH):
        lp, lp_vjp = _layer_params(l, p, conv5_w, conv3_w, bias)
        lps.append(lp)
        lp_vjps.append(lp_vjp)
    h = _rmsnorm_fwd(xt, lps[0]["mix_g"], "rmsnorm_fwd")
    for l in range(DEPTH):
        lp = lps[l]
        hooks = {}
        if exchange is not None and l == 0:
            hooks["self"] = _gather_hook(own[0][1:])
        if exchange is not None and l + 1 < DEPTH:
            hooks["attn"] = _gather_hook(own[l + 1][2:])
            hooks["gate"] = _gather_hook(own[l + 1][:2])
        xt, h, sv, layer_w[l], got = _forward_layer(xt, h, lp, layer_w[l], bias, lps[l + 1]["mix_g"] if l + 1 < DEPTH else None, hooks,
                                                    resolve=lambda g: whole(BIG[1:], 0, g))
        if "attn" in hooks:
            layer_w.append(dict(whole(BIG[:2], l + 1, got["gate"]), **whole(BIG[2:], l + 1, got["attn"])))
        saved.append(sv)
    dxt, loss_parts = _loss_head(xt, loss_target.reshape(T, D_MODEL))
    loss_local = jnp.sum(loss_parts[::8, 0])

    big_grads = [None] * DEPTH
    small_layers = [None] * DEPTH
    d_bias_tot = jnp.zeros_like(bias)
    pending = None
    for l in reversed(range(DEPTH)):
        last = exchange is not None and l == 0
        dxt, big_grads[l], d_lp, d_bias, own_reduce = _backward_layer(
            dxt, saved[l], lps[l], layer_w[l], bias, pending=pending,
            reducer=(lambda names, dws: _LayerReduce(names, dws, chip, core)) if last else None)
        if pending is not None:
            big_grads[l + 1] = pending.half
        if last:
            big_grads[l] = dict(own_reduce[0].half, **own_reduce[1].half)
        elif exchange is not None:
            pending = _LayerReduce(BIG, [big_grads[l][n] for n in BIG], chip, core)
        small_layers[l] = lp_vjps[l](d_lp)
        d_bias_tot = d_bias_tot + d_bias
    (d_rel_table,) = bias_vjp(d_bias_tot)
    local_small = {n: jnp.stack([small_layers[l][i] for l in range(DEPTH)]) for i, n in enumerate(LAYER_SMALL)}
    local_small["rel_bias_table"] = d_rel_table
    return dxt, loss_local, big_grads, local_small


def kernel(x, mix_norm_gain, w_in, ssd_conv_w, ssd_conv_b, ssd_dt_bias, ssd_a_log, ssd_d, ssd_out_gain, gmlp_v_gain, gmlp_w_s, gmlp_b_s, attn_q_gain, attn_k_gain, rel_bias_table, w_out, ffn_norm_gain, ffn_w_up, ffn_conv_w, ffn_conv_b, ffn_w_down, loss_target, m_mix_norm_gain, m_w_in, m_ssd_conv_w, m_ssd_conv_b, m_ssd_dt_bias, m_ssd_a_log, m_ssd_d, m_ssd_out_gain, m_gmlp_v_gain, m_gmlp_w_s, m_gmlp_b_s, m_attn_q_gain, m_attn_k_gain, m_rel_bias_table, m_w_out, m_ffn_norm_gain, m_ffn_w_up, m_ffn_conv_w, m_ffn_conv_b, m_ffn_w_down, v_mix_norm_gain, v_w_in, v_ssd_conv_w, v_ssd_conv_b, v_ssd_dt_bias, v_ssd_a_log, v_ssd_d, v_ssd_out_gain, v_gmlp_v_gain, v_gmlp_w_s, v_gmlp_b_s, v_attn_q_gain, v_attn_k_gain, v_rel_bias_table, v_w_out, v_ffn_norm_gain, v_ffn_w_up, v_ffn_conv_w, v_ffn_conv_b, v_ffn_w_down):
    env = dict(locals())
    p = {n: env[n] for n in WEIGHTS}
    chip = 2 * lax.axis_index("x") + lax.axis_index("y")
    core = lax.axis_index("c")

    conv_slots = _allgather8(_pack([ssd_conv_w, ffn_conv_w]), "allgather_conv")
    conv_parts = [_unpack(conv_slots[2 * k], [ssd_conv_w.shape, ffn_conv_w.shape]) for k in range(N_CHIPS)]
    conv5_w = jnp.concatenate([cp[0] for cp in conv_parts], axis=-1)
    conv3_w = jnp.concatenate([cp[1] for cp in conv_parts], axis=-1)
    own = [[p[n][l].astype(BF16) for n in BIG] for l in range(DEPTH)]
    (first,) = _run_hook(_gather_hook(own[0][:1]))
    layer_w = [{"w_in": _whole_weight("w_in", first, own[0][0], chip)}]

    dxt, loss_local, reduced, local_small = _local_grads(x, loss_target, p, conv5_w, conv3_w, layer_w, exchange=(chip, core, own))

    small_shapes = [local_small[n].shape for n in SMALL] + [(1,)]
    summed = _unpack(_allreduce_small(_pack([local_small[n] for n in SMALL] + [loss_local.reshape(1)]), chip), small_shapes)
    grads = dict(zip(SMALL, summed[:-1]))
    loss = summed[-1][0]
    grads["ssd_conv_w"] = lax.dynamic_slice_in_dim(grads["ssd_conv_w"], chip * 256, 256, axis=2)
    grads["ffn_conv_w"] = lax.dynamic_slice_in_dim(grads["ffn_conv_w"], chip * (2 * FFN_DIM // N_CHIPS), 2 * FFN_DIM // N_CHIPS, axis=2)

    halves = [jnp.stack([reduced[l][n] for l in range(DEPTH)]) for n in BIG]
    others = _run_hook(dict(_to_sibling_hook(halves), name="swap_halves"))

    delta, new_m, new_v = {}, {}, {}
    for n, mine, other in zip(BIG, halves, others):
        grads[n], delta[n], new_m[n], new_v[n] = _adamw(p[n], mine, other, env["m_" + n], env["v_" + n], core, ADAM_ROWS[n], "adamw_" + n)
    d, nm, nv = _adamw_many([p[n] for n in SMALL], [grads[n] for n in SMALL], [env["m_" + n] for n in SMALL], [env["v_" + n] for n in SMALL])
    for n, a, b, c in zip(SMALL, d, nm, nv):
        delta[n], new_m[n], new_v[n] = a, b, c

    return (loss, dxt.reshape(B_LOC, SEQ, D_MODEL), *[grads[n] for n in WEIGHTS], *[delta[n] for n in WEIGHTS],
            *[new_m[n] for n in WEIGHTS], *[new_v[n] for n in WEIGHTS])
```

```python
import functools
import math

import jax
import jax.numpy as jnp
import numpy as np
from jax import lax
from jax.experimental import pallas as pl
from jax.experimental.pallas import tpu as pltpu

F32 = jnp.float32
BF16 = jnp.bfloat16
HI = lax.Precision.HIGHEST
SUM_PRECISION = lax.Precision.HIGH
MESH = pl.DeviceIdType.MESH
ANY = pl.BlockSpec(memory_space=pl.ANY)

D_MODEL = 1024
SEQ = 2048
B_LOC = 2
T = B_LOC * SEQ
DEPTH = 4
N_CHIPS = 4
N_DEV = 8
HEAD = 64
CHUNK = 128
N_CHUNK = SEQ // CHUNK
SSD_INNER = 512
SSD_XBC = 1024
FFN_DIM = 2816
IN_WIDTH = 2832
NP = 3072
C_XS, C_B, C_C, C_Z, C_GU, C_GV, C_Q, C_K, C_V, C_DT = 0, 512, 768, 1024, 1536, 1792, 2048, 2304, 2560, 2816
NORM_EPS = 1e-6
NEG_INF = -1e30
ATTN_DILS = (1, 4, 16)
ATTN_HALF = 64
ADAM_LR, ADAM_B1, ADAM_B2, ADAM_EPS, ADAM_WD, ADAM_STEP = 0.001, 0.9, 0.999, 1e-08, 0.01, 10
VMEM_LIMIT = 56 * 1024 * 1024

S_ = jax.ShapeDtypeStruct


def _cp():
    return pltpu.CompilerParams(vmem_limit_bytes=VMEM_LIMIT)


def _shift_rows(x, k):
    n = x.shape[0]
    if k == 0:
        return x
    r = pltpu.roll(x, (-k) % n, 0)
    t = lax.broadcasted_iota(jnp.int32, (n, 1), 0)
    return jnp.where((t + k >= 0) & (t + k < n), r, 0.0)


@functools.partial(jax.custom_vjp, nondiff_argnums=(1,))
def _shift(x, k):
    return _shift_rows(x, k)


def _shift_fwd(x, k):
    return _shift_rows(x, k), None


def _shift_bwd(k, _, g):
    return (_shift_rows(g, -k),)


_shift.defvjp(_shift_fwd, _shift_bwd)


def _dwconv(x, taps, bias):
    half = len(taps) // 2
    y = bias
    for k, w in enumerate(taps):
        y = y + w * _shift(x, k - half)
    return y


def _softplus(x):
    return jnp.maximum(x, 0.0) + jnp.log1p(jnp.exp(-jnp.abs(x)))


def _dot(a, b):
    return jnp.dot(a.astype(BF16), b.astype(BF16), preferred_element_type=F32)


def _dot_nt(a, b):
    return lax.dot_general(a.astype(BF16), b.astype(BF16), (((1,), (1,)), ((), ())), preferred_element_type=F32)


def _dot_tn(a, b):
    return lax.dot_general(a.astype(BF16), b.astype(BF16), (((0,), (0,)), ((), ())), preferred_element_type=F32)


def _head_sum_matrix(width):
    i = lax.broadcasted_iota(jnp.int32, (width, width), 0) // HEAD
    j = lax.broadcasted_iota(jnp.int32, (width, width), 1) // HEAD
    return (i == j).astype(F32)


def _matmul(a, b, *, dims, grid, a_spec, b_spec, o_spec, out_shape, acc_shape, res=None, res_spec=None, norm_gain=None,
            norm_bwd=None, name):
    nk = grid[2]
    n_in = 2 + (res is not None) + (norm_gain is not None) + 2 * (norm_bwd is not None)

    def body(*refs):
        a_ref, b_ref = refs[:2]
        r_ref = refs[2] if res is not None else None
        g_ref = refs[n_in - 1] if norm_gain is not None or norm_bwd is not None else None
        x_ref = refs[n_in - 2] if norm_bwd is not None else None
        o_ref = refs[n_in]
        n_ref = refs[n_in + 1] if norm_gain is not None or norm_bwd is not None else None
        row_tile = pl.program_id(0)

        def finish(tot):
            if x_ref is not None:
                xv = x_ref[...]
                scale = lax.rsqrt(jnp.mean(xv * xv, axis=-1, keepdims=True) + NORM_EPS)
                gd = tot * g_ref[...]
                dot = jnp.mean(gd * xv, axis=-1, keepdims=True)
                o_ref[...] = r_ref[...] + scale * gd - xv * (scale * scale * scale * dot)
                part = jnp.sum(tot * xv * scale, axis=0, keepdims=True)

                @pl.when(row_tile == 0)
                def _():
                    n_ref[...] = part

                @pl.when(row_tile > 0)
                def _():
                    n_ref[...] += part

                return
            if r_ref is not None:
                tot = tot + r_ref[...]
            o_ref[...] = tot.astype(o_ref.dtype)
            if n_ref is not None:
                scale = lax.rsqrt(jnp.mean(tot * tot, axis=-1, keepdims=True) + NORM_EPS)
                n_ref[...] = (tot * scale * g_ref[...]).astype(BF16)

        part = lax.dot_general(a_ref[...].astype(BF16), b_ref[...].astype(BF16), dims, preferred_element_type=F32)
        if nk == 1:
            finish(part)
            return
        acc_ref = refs[-1]
        k = pl.program_id(2)

        @pl.when(k == 0)
        def _():
            acc_ref[...] = part

        @pl.when(k > 0)
        def _():
            acc_ref[...] += part

        @pl.when(k == nk - 1)
        def _():
            finish(acc_ref[...])

    in_specs = [a_spec, b_spec] + ([res_spec] if res is not None else [])
    args = (a, b) + ((res,) if res is not None else ())
    out_specs, out_shapes = o_spec, out_shape
    row = pl.BlockSpec((1, acc_shape[1]), lambda i, j, q: (0, 0))
    if norm_gain is not None:
        in_specs.append(row)
        args = args + (norm_gain,)
        out_specs, out_shapes = [o_spec, o_spec], [out_shape, S_(out_shape.shape, BF16)]
    if norm_bwd is not None:
        in_specs += [res_spec, row]
        args = args + tuple(norm_bwd)
        out_specs, out_shapes = [o_spec, row], [out_shape, S_((1, acc_shape[1]), F32)]
    scratch = [] if nk == 1 else [pltpu.VMEM(acc_shape, F32)]
    return pl.pallas_call(body, grid=grid, in_specs=in_specs, out_specs=out_specs, out_shape=out_shapes,
                          scratch_shapes=scratch, compiler_params=_cp(), name=name)(*args)


NN = (((1,), (0,)), ((), ()))
NT = (((1,), (1,)), ((), ()))
TN = (((0,), (0,)), ((), ()))


def _mm_nn(a, b, *, tm, tn, tk, out_dtype, res=None, norm_gain=None, name):
    m, k = a.shape
    n = b.shape[1]
    assert norm_gain is None or tn == n
    return _matmul(a, b, dims=NN, grid=(m // tm, n // tn, k // tk),
                   a_spec=pl.BlockSpec((tm, tk), lambda i, j, q: (i, q)),
                   b_spec=pl.BlockSpec((tk, tn), lambda i, j, q: (q, j)),
                   o_spec=pl.BlockSpec((tm, tn), lambda i, j, q: (i, j)),
                   out_shape=S_((m, n), out_dtype), acc_shape=(tm, tn), res=res,
                   res_spec=pl.BlockSpec((tm, tn), lambda i, j, q: (i, j)), norm_gain=norm_gain, name=name)


def _mm_nt(a, b, *, tm, tn, tk, out_dtype, res=None, norm_bwd=None, name):
    m, k = a.shape
    n = b.shape[0]
    assert norm_bwd is None or tn == n
    return _matmul(a, b, dims=NT, grid=(m // tm, n // tn, k // tk),
                   a_spec=pl.BlockSpec((tm, tk), lambda i, j, q: (i, q)),
                   b_spec=pl.BlockSpec((tn, tk), lambda i, j, q: (j, q)),
                   o_spec=pl.BlockSpec((tm, tn), lambda i, j, q: (i, j)),
                   out_shape=S_((m, n), out_dtype), acc_shape=(tm, tn), res=res,
                   res_spec=pl.BlockSpec((tm, tn), lambda i, j, q: (i, j)), norm_bwd=norm_bwd, name=name)


def _mm_tn(a, b, *, tm, tn, tk, out_dtype, name):
    k, m = a.shape
    n = b.shape[1]
    return _matmul(a, b, dims=TN, grid=(m // tm, n // tn, k // tk),
                   a_spec=pl.BlockSpec((tk, tm), lambda i, j, q: (q, i)),
                   b_spec=pl.BlockSpec((tk, tn), lambda i, j, q: (q, j)),
                   o_spec=pl.BlockSpec((tm, tn), lambda i, j, q: (i, j)),
                   out_shape=S_((m, n), out_dtype), acc_shape=(tm, tn), name=name)


HALF_TILE = FFN_DIM // 2


def _mm_up(hn, w_up):
    return _matmul(hn, w_up, dims=NN, grid=(T // 1024, 4, 1),
                   a_spec=pl.BlockSpec((1024, D_MODEL), lambda i, j, q: (i, 0)),
                   b_spec=pl.BlockSpec((None, D_MODEL, HALF_TILE), lambda i, j, q: (j, 0, 0)),
                   o_spec=pl.BlockSpec((None, 1024, HALF_TILE), lambda i, j, q: (j // 2, i, j % 2)),
                   out_shape=S_((2, T, FFN_DIM), F32), acc_shape=(1024, HALF_TILE), name="mm_up")


def _mm_dhn(dup3, w_up, x2, gain, dres):
    row = pl.BlockSpec((1024, D_MODEL), lambda i, j, q: (i, 0))
    return _matmul(dup3, w_up, dims=NT, grid=(T // 1024, 1, 4),
                   a_spec=pl.BlockSpec((None, 1024, HALF_TILE), lambda i, j, q: (q // 2, i, q % 2)),
                   b_spec=pl.BlockSpec((None, D_MODEL, HALF_TILE), lambda i, j, q: (q, 0, 0)),
                   o_spec=row, out_shape=S_((T, D_MODEL), F32), acc_shape=(1024, D_MODEL), res=dres, res_spec=row,
                   norm_bwd=(x2, gain), name="mm_dhn")


def _mm_dwup(hn, dup3):
    return _matmul(hn, dup3, dims=TN, grid=(1, 4, T // 1024),
                   a_spec=pl.BlockSpec((1024, D_MODEL), lambda i, j, q: (q, 0)),
                   b_spec=pl.BlockSpec((None, 1024, HALF_TILE), lambda i, j, q: (j // 2, q, j % 2)),
                   o_spec=pl.BlockSpec((None, D_MODEL, HALF_TILE), lambda i, j, q: (j, 0, 0)),
                   out_shape=S_((N_CHIPS, D_MODEL, HALF_TILE), BF16), acc_shape=(D_MODEL, HALF_TILE), name="mm_dwup")


ROWS = 512


def _rmsnorm_fwd(x, gain, name):
    def body(x_ref, g_ref, o_ref):
        xv = x_ref[...]
        r = lax.rsqrt(jnp.mean(xv * xv, axis=-1, keepdims=True) + NORM_EPS)
        o_ref[...] = (xv * r * g_ref[...]).astype(BF16)

    return pl.pallas_call(body, grid=(T // ROWS,),
                          in_specs=[pl.BlockSpec((ROWS, D_MODEL), lambda i: (i, 0)), pl.BlockSpec((1, D_MODEL), lambda i: (0, 0))],
                          out_specs=pl.BlockSpec((ROWS, D_MODEL), lambda i: (i, 0)),
                          out_shape=S_((T, D_MODEL), BF16), name=name)(x, gain)


def _loss_head(y, target):
    def body(y_ref, t_ref, dy_ref, p_ref):
        e = y_ref[...] - t_ref[...]
        dy_ref[...] = e * (1.0 / D_MODEL)
        p_ref[...] = jnp.full((8, 128), 0.5 / D_MODEL, F32) * jnp.sum(e * e)

    row = pl.BlockSpec((ROWS, D_MODEL), lambda i: (i, 0))
    return pl.pallas_call(body, grid=(T // ROWS,), in_specs=[row, row],
                          out_specs=[row, pl.BlockSpec((8, 128), lambda i: (i, 0))],
                          out_shape=[S_((T, D_MODEL), F32), S_((T // ROWS * 8, 128), F32)], name="loss_head")(y, target)


def _adamw_update(w_ref, g_ref, m_ref, v_ref, d_ref, nm_ref, nv_ref):
    gv = g_ref[...]
    nm = ADAM_B1 * m_ref[...] + (1.0 - ADAM_B1) * gv
    nv = ADAM_B2 * v_ref[...] + (1.0 - ADAM_B2) * (gv * gv)
    m_hat = nm / (1.0 - ADAM_B1 ** ADAM_STEP)
    v_hat = nv / (1.0 - ADAM_B2 ** ADAM_STEP)
    d_ref[...] = -ADAM_LR * (m_hat / (jnp.sqrt(v_hat) + ADAM_EPS) + ADAM_WD * w_ref[...])
    nm_ref[...] = nm
    nv_ref[...] = nv


def _adamw(w, g_mine, g_other, m, v, core, rows, name):
    per_half = w.shape[1] // 2 // rows

    def body(core_ref, w_ref, gm_ref, go_ref, m_ref, v_ref, g_ref, d_ref, nm_ref, nv_ref):
        mine = (pl.program_id(1) // per_half) == core_ref[0]
        g_ref[...] = jnp.where(mine, gm_ref[...], go_ref[...])
        _adamw_update(w_ref, g_ref, m_ref, v_ref, d_ref, nm_ref, nv_ref)

    blk = pl.BlockSpec((None, rows, w.shape[2]), lambda l, i, core_ref: (l, i, 0))
    half = pl.BlockSpec((None, rows, w.shape[2]), lambda l, i, core_ref: (l, i % per_half, 0))
    out = S_(w.shape, F32)
    spec = pltpu.PrefetchScalarGridSpec(num_scalar_prefetch=1, grid=(w.shape[0], w.shape[1] // rows),
                                        in_specs=[blk, half, half, blk, blk], out_specs=[blk] * 4)
    return pl.pallas_call(body, grid_spec=spec, out_shape=[out] * 4, name=name)(core.reshape(1).astype(jnp.int32), w, g_mine, g_other, m, v)


def _adamw_many(ws, gs, ms, vs):
    n = len(ws)

    def body(*refs):
        for i in range(n):
            _adamw_update(*[refs[k * n + i] for k in range(7)])

    out = [S_(w.shape, F32) for w in ws]
    res = pl.pallas_call(body, out_shape=out * 3, name="adamw_small")(*ws, *gs, *ms, *vs)
    return res[:n], res[n:2 * n], res[2 * n:]


FFN_CT = 256


def _gate_fn(up_g, up_v, wg0, wg1, wg2, bg, wv0, wv1, wv2, bv):
    gate = _dwconv(up_g, [wg0, wg1, wg2], bg)
    val = _dwconv(up_v, [wv0, wv1, wv2], bv)
    return jax.nn.silu(gate) * val


def _taps(ref, part, n):
    return [ref[part, k:k + 1, :] for k in range(n)]


def _convgate_fwd(up3, cw, cb, hook=None):
    def body(up_ref, cw_ref, cb_ref, o_ref):
        o_ref[...] = _gate_fn(up_ref[0], up_ref[1], *_taps(cw_ref, 0, 3), cb_ref[0], *_taps(cw_ref, 1, 3), cb_ref[1]).astype(BF16)

    (act,), got = _hooked_call(
        body, grid=(FFN_DIM // FFN_CT, B_LOC),
        in_specs=[pl.BlockSpec((2, SEQ, FFN_CT), lambda j, b: (0, b, j)),
                  pl.BlockSpec((2, 8, FFN_CT), lambda j, b: (0, 0, j)),
                  pl.BlockSpec((2, 1, FFN_CT), lambda j, b: (0, 0, j))],
        out_specs=[pl.BlockSpec((SEQ, FFN_CT), lambda j, b: (b, j))],
        out_shape=[S_((T, FFN_DIM), BF16)], scratch_shapes=[], args=(up3, cw, cb), hook=hook, name="convgate_fwd")
    return act, got


def _convgate_bwd(up3, cw, cb, dact, hook=None):
    def body(up_ref, cw_ref, cb_ref, da_ref, dup_ref, dcw_ref):
        args = (up_ref[0], up_ref[1], *_taps(cw_ref, 0, 3), cb_ref[0], *_taps(cw_ref, 1, 3), cb_ref[1])
        _, vjp = jax.vjp(_gate_fn, *args)
        dg, dv, g0, g1, g2, gb, v0, v1, v2, vb = vjp(da_ref[...])
        dup_ref[0] = dg.astype(BF16)
        dup_ref[1] = dv.astype(BF16)
        zero = jnp.zeros((4, FFN_CT), F32)
        new = jnp.stack([jnp.concatenate([g0, g1, g2, gb, zero], axis=0), jnp.concatenate([v0, v1, v2, vb, zero], axis=0)])

        @pl.when(pl.program_id(1) == 0)
        def _():
            dcw_ref[...] = new

        @pl.when(pl.program_id(1) > 0)
        def _():
            dcw_ref[...] += new

    return _hooked_call(
        body, grid=(FFN_DIM // FFN_CT, B_LOC),
        in_specs=[pl.BlockSpec((2, SEQ, FFN_CT), lambda j, b: (0, b, j)),
                  pl.BlockSpec((2, 8, FFN_CT), lambda j, b: (0, 0, j)),
                  pl.BlockSpec((2, 1, FFN_CT), lambda j, b: (0, 0, j)),
                  pl.BlockSpec((SEQ, FFN_CT), lambda j, b: (b, j))],
        out_specs=[pl.BlockSpec((2, SEQ, FFN_CT), lambda j, b: (0, b, j)),
                   pl.BlockSpec((2, 8, FFN_CT), lambda j, b: (0, 0, j))],
        out_shape=[S_((2, T, FFN_DIM), BF16), S_((2, 8, FFN_DIM), F32)],
        scratch_shapes=[], args=(up3, cw, cb, dact), hook=hook, name="convgate_bwd")


SSD_CT = 256


def _conv5_fn(x, w0, w1, w2, w3, w4, b):
    return jax.nn.silu(_dwconv(x, [w0, w1, w2, w3, w4], b))


def _ssd_pre_fwd(proj, cw, cb):
    def body(x_ref, cw_ref, cb_ref, o_ref):
        o_ref[...] = _conv5_fn(x_ref[...], *[cw_ref[k:k + 1, :] for k in range(5)], cb_ref[...])

    return pl.pallas_call(
        body, grid=(SSD_XBC // SSD_CT, B_LOC),
        in_specs=[pl.BlockSpec((SEQ, SSD_CT), lambda j, b: (b, j)),
                  pl.BlockSpec((8, SSD_CT), lambda j, b: (0, j)),
                  pl.BlockSpec((1, SSD_CT), lambda j, b: (0, j))],
        out_specs=pl.BlockSpec((SEQ, SSD_CT), lambda j, b: (b, j)),
        out_shape=S_((T, SSD_XBC), F32), compiler_params=_cp(), name="ssd_pre_fwd")(proj, cw, cb)


def _ssd_pre_bwd(proj, cw, cb, dxc, hook=None):
    def body(x_ref, cw_ref, cb_ref, d_ref, dx_ref, dcw_ref):
        _, vjp = jax.vjp(_conv5_fn, x_ref[...], *[cw_ref[k:k + 1, :] for k in range(5)], cb_ref[...])
        dx, g0, g1, g2, g3, g4, gb = vjp(d_ref[...])
        dx_ref[...] = dx.astype(BF16)
        new = jnp.concatenate([g0, g1, g2, g3, g4, gb, jnp.zeros((2, SSD_CT), F32)], axis=0)

        @pl.when(pl.program_id(1) == 0)
        def _():
            dcw_ref[...] = new

        @pl.when(pl.program_id(1) > 0)
        def _():
            dcw_ref[...] += new

    return _hooked_call(
        body, grid=(SSD_XBC // SSD_CT, B_LOC),
        in_specs=[pl.BlockSpec((SEQ, SSD_CT), lambda j, b: (b, j)),
                  pl.BlockSpec((8, SSD_CT), lambda j, b: (0, j)),
                  pl.BlockSpec((1, SSD_CT), lambda j, b: (0, j)),
                  pl.BlockSpec((SEQ, SSD_CT), lambda j, b: (b, j))],
        out_specs=[pl.BlockSpec((SEQ, SSD_CT), lambda j, b: (b, j)),
                   pl.BlockSpec((8, SSD_CT), lambda j, b: (0, j))],
        out_shape=[S_((T, SSD_XBC), BF16), S_((8, SSD_XBC), F32)],
        scratch_shapes=[], args=(proj, cw, cb, dxc), hook=hook, name="ssd_pre_bwd")


GROUP_W = 256
ONE_BUFFER = dict(pipeline_mode=pl.Buffered(1))
HEADS_PER_GROUP = 4
FWD_SCAN_UNROLL = 4
BWD_SCAN_UNROLL = 2


def _ssd_dt_fn(dt_raw, bias, alog):
    dt = _softplus(dt_raw + bias)
    return dt, dt * (-jnp.exp(alog))


def _ssd_chunk_fn(direction, group, xc0, xc1, bc, cc, dt, da, prev0, prev1):
    q = CHUNK
    ti = lax.broadcasted_iota(jnp.int32, (q, q), 0)
    si = lax.broadcasted_iota(jnp.int32, (q, q), 1)
    keep = (ti >= si) if direction == 0 else (ti <= si)
    mat = keep.astype(F32)
    acs = jnp.dot(mat, da, precision=HI, preferred_element_type=F32)
    acs_t = lax.dot_general(da, mat, (((0,), (1,)), ((), ())), precision=HI, preferred_element_type=F32)
    tot = jnp.sum(da, axis=0, keepdims=True)
    lane = lax.broadcasted_iota(jnp.int32, (1, 128), 1)
    sub = lax.broadcasted_iota(jnp.int32, (128, 1), 0)
    first_head = lane < HEAD
    cb = _dot_nt(cc, bc)
    a_cols, tots, dt_cols, lows, douts = [], [], [], [], []
    for h in range(HEADS_PER_GROUP):
        ln = 8 * direction + 4 * group + h
        oh_l = (lane == ln).astype(F32)
        oh_s = (sub == ln).astype(F32)
        a_col = jnp.sum(acs * oh_l, axis=1, keepdims=True)
        a_row = jnp.sum(acs_t * oh_s, axis=0, keepdims=True)
        tot_h = jnp.sum(tot * oh_l, axis=1, keepdims=True)
        a_cols.append(a_col)
        tots.append(tot_h)
        dt_cols.append(jnp.sum(dt * oh_l, axis=1, keepdims=True))
        lows.append(cb * jnp.exp(jnp.where(keep, a_col - a_row, NEG_INF)))
        douts.append(bc * jnp.exp(tot_h - a_col))
    out = []
    for pair, (xc, prev) in enumerate(((xc0, prev0), (xc1, prev1))):
        h0, h1 = 2 * pair, 2 * pair + 1
        xdt = xc * jnp.where(first_head, dt_cols[h0], dt_cols[h1])
        y = jnp.where(first_head, jnp.exp(a_cols[h0]), jnp.exp(a_cols[h1])) * _dot(cc, prev)
        y = y + jnp.where(first_head, _dot(lows[h0], xdt), _dot(lows[h1], xdt))
        st = jnp.where(first_head, _dot_tn(douts[h0], xdt), _dot_tn(douts[h1], xdt))
        out.append((y, prev * jnp.where(first_head, jnp.exp(tots[h0]), jnp.exp(tots[h1])) + st))
    return out[0][0], out[1][0], out[0][1], out[1][1]


def _ssd_post_fn(y, xc, z, d_exp, gain):
    y = (y + d_exp * xc) * jax.nn.silu(z)
    return y * lax.rsqrt(jnp.mean(y * y, axis=-1, keepdims=True) + NORM_EPS) * gain


def _chunk_rows(c):
    return pl.ds(pl.multiple_of(c * CHUNK, CHUNK), CHUNK)


def _scan_loop(step, init, unroll):
    def body(i, carry):
        for k in range(unroll):
            carry = step(i * unroll + k, carry)
        return carry

    return lax.fori_loop(0, N_CHUNK // unroll, body, init)


def _ssd_scan_specs(**mode):
    return [pl.BlockSpec((SEQ, GROUP_W), lambda g, b: (b, g), **mode),
            pl.BlockSpec((SEQ, 128), lambda g, b: (b, C_B // 128 + g), **mode),
            pl.BlockSpec((SEQ, 128), lambda g, b: (b, C_C // 128 + g), **mode),
            pl.BlockSpec((SEQ, GROUP_W), lambda g, b: (b, C_Z // GROUP_W + g), **mode),
            pl.BlockSpec((SEQ, 128), lambda g, b: (b, C_DT // 128), **mode),
            pl.BlockSpec((1, 128), lambda g, b: (0, 0)),
            pl.BlockSpec((1, 128), lambda g, b: (0, 0)),
            pl.BlockSpec((1, GROUP_W), lambda g, b: (0, g)),
            pl.BlockSpec((1, GROUP_W), lambda g, b: (0, g))]


def _ssd_state_spec(**mode):
    return pl.BlockSpec((None, None, 2 * N_CHUNK, 128, GROUP_W), lambda g, b: (g, b, 0, 0, 0), **mode)


def _ssd_scan_fwd(xc, proj, dtb, alog, d_exp, gain, hook=None):
    def body(x_ref, b_ref, c_ref, z_ref, dt_ref, dtb_ref, al_ref, de_ref, g_ref, o_ref, y_s, st_ref, dt_s, da_s):
        group = pl.program_id(0)
        dt, da = _ssd_dt_fn(dt_ref[...], dtb_ref[...], al_ref[...])
        dt_s[...] = dt
        da_s[...] = da
        for direction in (0, 1):
            def step(i, prev, direction=direction):
                c = i if direction == 0 else N_CHUNK - 1 - i
                rows = _chunk_rows(c)
                st_ref[direction * N_CHUNK + c, :, 0:128] = prev[0]
                st_ref[direction * N_CHUNK + c, :, 128:256] = prev[1]
                y0, y1, nxt0, nxt1 = _ssd_chunk_fn(direction, group, x_ref[rows, 0:128], x_ref[rows, 128:256], b_ref[rows, :], c_ref[rows, :],
                                                   dt_s[rows, :], da_s[rows, :], prev[0], prev[1])
                if direction == 0:
                    y_s[rows, 0:128] = y0
                    y_s[rows, 128:256] = y1
                else:
                    y_s[rows, 0:128] += y0
                    y_s[rows, 128:256] += y1
                return nxt0, nxt1

            _scan_loop(step, (jnp.zeros((128, 128), F32), jnp.zeros((128, 128), F32)), FWD_SCAN_UNROLL)

        def post(c, carry):
            rows = _chunk_rows(c)
            o_ref[rows, :] = _ssd_post_fn(y_s[rows, :], x_ref[rows, :], z_ref[rows, :], de_ref[...], g_ref[...]).astype(BF16)
            return carry

        lax.fori_loop(0, N_CHUNK, post, 0)

    return _hooked_call(
        body, grid=(2, B_LOC), in_specs=_ssd_scan_specs(),
        out_specs=[pl.BlockSpec((SEQ, GROUP_W), lambda g, b: (b, g)), pl.BlockSpec((SEQ, GROUP_W), lambda g, b: (b, g)), _ssd_state_spec()],
        out_shape=[S_((T, SSD_INNER), BF16), S_((T, SSD_INNER), F32), S_((2, B_LOC, 2 * N_CHUNK, 128, GROUP_W), F32)],
        scratch_shapes=[pltpu.VMEM((SEQ, 128), F32), pltpu.VMEM((SEQ, 128), F32)],
        args=(xc, xc, xc, proj, proj, dtb, alog, d_exp, gain), hook=hook, name="ssd_scan_fwd")


def _ssd_scan_bwd(xc, proj, dtb, alog, d_exp, gain, dy, ysum, states, hook=None):
    def body(x_ref, b_ref, c_ref, z_ref, dt_ref, dtb_ref, al_ref, de_ref, g_ref, dy_ref, ys_ref, st_s,
             dx_ref, db_ref, dc_ref, dz_ref, ddt_ref, ddtb_ref, dal_ref, dde_ref, dg_ref,
             dt_s, da_s, y_s, ddt_s, dda_s):
        group = pl.program_id(0)
        first = pl.program_id(1) == 0
        (dt, da), dt_vjp = jax.vjp(_ssd_dt_fn, dt_ref[...], dtb_ref[...], al_ref[...])
        dt_s[...] = dt
        da_s[...] = da

        def post(c, carry):
            rows = _chunk_rows(c)
            _, post_vjp = jax.vjp(_ssd_post_fn, ys_ref[rows, :], x_ref[rows, :], z_ref[rows, :], de_ref[...], g_ref[...])
            d_y, d_x_skip, d_z, g_de, g_g = post_vjp(dy_ref[rows, :])
            dz_ref[rows, :] = d_z.astype(BF16)
            dx_ref[rows, :] = d_x_skip
            y_s[rows, :] = d_y
            return carry[0] + g_de, carry[1] + g_g

        d_de, d_g = lax.fori_loop(0, N_CHUNK, post, (jnp.zeros((1, GROUP_W), F32), jnp.zeros((1, GROUP_W), F32)))
        db_ref[...] = jnp.zeros((SEQ, 128), F32)
        dc_ref[...] = jnp.zeros((SEQ, 128), F32)
        ddt_s[...] = jnp.zeros((SEQ, 128), F32)
        dda_s[...] = jnp.zeros((SEQ, 128), F32)
        for direction in (0, 1):
            def bstep(i, dnxt, direction=direction):
                c = N_CHUNK - 1 - i if direction == 0 else i
                rows = _chunk_rows(c)
                fn = functools.partial(_ssd_chunk_fn, direction, group)
                _, vjp = jax.vjp(fn, x_ref[rows, 0:128], x_ref[rows, 128:256], b_ref[rows, :], c_ref[rows, :], dt_s[rows, :], da_s[rows, :],
                                 st_s[direction * N_CHUNK + c, :, 0:128], st_s[direction * N_CHUNK + c, :, 128:256])
                g_x0, g_x1, g_b, g_c, g_dt, g_da, g_prev0, g_prev1 = vjp((y_s[rows, 0:128], y_s[rows, 128:256], dnxt[0], dnxt[1]))
                dx_ref[rows, 0:128] += g_x0
                dx_ref[rows, 128:256] += g_x1
                db_ref[rows, :] += g_b
                dc_ref[rows, :] += g_c
                ddt_s[rows, :] += g_dt
                dda_s[rows, :] += g_da
                return g_prev0, g_prev1

            _scan_loop(bstep, (jnp.zeros((128, 128), F32), jnp.zeros((128, 128), F32)), BWD_SCAN_UNROLL)
        g_raw, g_bias, g_alog = dt_vjp((ddt_s[...], dda_s[...]))
        ddt_ref[...] = g_raw
        pad7 = jnp.zeros((7, 128), F32)
        new_b = jnp.concatenate([g_bias, pad7], axis=0)
        new_a = jnp.concatenate([g_alog, pad7], axis=0)

        @pl.when(first)
        def _():
            ddtb_ref[...] = new_b
            dal_ref[...] = new_a
            dde_ref[...] = d_de
            dg_ref[...] = d_g

        @pl.when(jnp.logical_not(first))
        def _():
            ddtb_ref[...] += new_b
            dal_ref[...] += new_a
            dde_ref[...] += d_de
            dg_ref[...] += d_g

    return _hooked_call(
        body, grid=(2, B_LOC),
        in_specs=_ssd_scan_specs(**ONE_BUFFER) + [pl.BlockSpec((SEQ, GROUP_W), lambda g, b: (b, g), **ONE_BUFFER),
                                                  pl.BlockSpec((SEQ, GROUP_W), lambda g, b: (b, g), **ONE_BUFFER),
                                                  _ssd_state_spec(**ONE_BUFFER)],
        out_specs=[pl.BlockSpec((SEQ, GROUP_W), lambda g, b: (b, g)),
                   pl.BlockSpec((SEQ, 128), lambda g, b: (b, g)),
                   pl.BlockSpec((SEQ, 128), lambda g, b: (b, g)),
                   pl.BlockSpec((SEQ, GROUP_W), lambda g, b: (b, g)),
                   pl.BlockSpec((None, SEQ, 128), lambda g, b: (g, b, 0)),
                   pl.BlockSpec((None, 8, 128), lambda g, b: (g, 0, 0)),
                   pl.BlockSpec((None, 8, 128), lambda g, b: (g, 0, 0)),
                   pl.BlockSpec((1, GROUP_W), lambda g, b: (0, g)),
                   pl.BlockSpec((1, GROUP_W), lambda g, b: (0, g))],
        out_shape=[S_((T, SSD_INNER), F32), S_((T, 256), F32), S_((T, 256), F32), S_((T, SSD_INNER), BF16),
                   S_((2, T, 128), F32), S_((2, 8, 128), F32), S_((2, 8, 128), F32),
                   S_((1, SSD_INNER), F32), S_((1, SSD_INNER), F32)],
        scratch_shapes=[pltpu.VMEM((SEQ, 128), F32), pltpu.VMEM((SEQ, 128), F32), pltpu.VMEM((SEQ, GROUP_W), F32),
                        pltpu.VMEM((SEQ, 128), F32), pltpu.VMEM((SEQ, 128), F32)],
        args=(xc, xc, xc, proj, proj, dtb, alog, d_exp, gain, dy, ysum, states), hook=hook, name="ssd_scan_bwd")


GMLP_W = 256


def _gmlp_chunk_fn(gu, gv, v_gain, w0, w1, w2, w3, b_exp):
    u = jax.nn.gelu(gu)
    v = jax.nn.gelu(gv)
    v = v * lax.rsqrt(jnp.mean(v * v, axis=-1, keepdims=True) + NORM_EPS) * v_gain
    col = lax.broadcasted_iota(jnp.int32, (1, GMLP_W), 1) // HEAD
    mixed = b_exp
    for g, w in enumerate((w0, w1, w2, w3)):
        mixed = mixed + (col == g).astype(F32) * _dot(w, v)
    return u * mixed


def _gmlp_specs():
    return [pl.BlockSpec((SEQ, GMLP_W), lambda b: (b, C_GU // GMLP_W)),
            pl.BlockSpec((SEQ, GMLP_W), lambda b: (b, C_GV // GMLP_W)),
            pl.BlockSpec((1, GMLP_W), lambda b: (0, 0)),
            pl.BlockSpec((4, CHUNK, CHUNK), lambda b: (0, 0, 0)),
            pl.BlockSpec((CHUNK, GMLP_W), lambda b: (0, 0))]


def _gmlp_fwd(proj, v_gain, w_s, b_exp):
    def body(u_ref, v_ref, g_ref, w_ref, b_ref, o_ref):
        def step(c, carry):
            rows = _chunk_rows(c)
            o_ref[rows, :] = _gmlp_chunk_fn(u_ref[rows, :], v_ref[rows, :], g_ref[...], w_ref[0], w_ref[1], w_ref[2], w_ref[3],
                                            b_ref[...]).astype(BF16)
            return carry

        lax.fori_loop(0, N_CHUNK, step, 0)

    return pl.pallas_call(body, grid=(B_LOC,), in_specs=_gmlp_specs(),
                          out_specs=pl.BlockSpec((SEQ, GMLP_W), lambda b: (b, 0)),
                          out_shape=S_((T, GMLP_W), BF16), name="gmlp_fwd")(proj, proj, v_gain, w_s, b_exp)


def _gmlp_bwd(proj, v_gain, w_s, b_exp, dy):
    def body(u_ref, v_ref, g_ref, w_ref, b_ref, dy_ref, du_ref, dv_ref, dg_ref, dw_ref, db_ref):
        @pl.when(pl.program_id(0) == 0)
        def _():
            dg_ref[...] = jnp.zeros_like(dg_ref)
            dw_ref[...] = jnp.zeros_like(dw_ref)
            db_ref[...] = jnp.zeros_like(db_ref)

        def step(c, carry):
            rows = _chunk_rows(c)
            _, vjp = jax.vjp(_gmlp_chunk_fn, u_ref[rows, :], v_ref[rows, :], g_ref[...], w_ref[0], w_ref[1], w_ref[2], w_ref[3], b_ref[...])
            g_u, g_v, g_g, g_w0, g_w1, g_w2, g_w3, g_b = vjp(dy_ref[rows, :])
            du_ref[rows, :] = g_u.astype(BF16)
            dv_ref[rows, :] = g_v.astype(BF16)
            dg_ref[...] += g_g
            db_ref[...] += g_b
            for g, gw in enumerate((g_w0, g_w1, g_w2, g_w3)):
                dw_ref[g] += gw
            return carry

        lax.fori_loop(0, N_CHUNK, step, 0)

    blk = pl.BlockSpec((SEQ, GMLP_W), lambda b: (b, 0))
    return pl.pallas_call(
        body, grid=(B_LOC,),
        in_specs=_gmlp_specs() + [pl.BlockSpec((SEQ, GMLP_W), lambda b: (b, SSD_INNER // GMLP_W))],
        out_specs=[blk, blk, pl.BlockSpec((1, GMLP_W), lambda b: (0, 0)),
                   pl.BlockSpec((4, CHUNK, CHUNK), lambda b: (0, 0, 0)), pl.BlockSpec((CHUNK, GMLP_W), lambda b: (0, 0))],
        out_shape=[S_((T, GMLP_W), BF16), S_((T, GMLP_W), BF16), S_((1, GMLP_W), F32),
                   S_((4, CHUNK, CHUNK), F32), S_((CHUNK, GMLP_W), F32)],
        name="gmlp_bwd")(proj, proj, v_gain, w_s, b_exp, dy)


PAIR_W = 128
QB = 128
KW = QB + 2 * ATTN_HALF
N_QB = SEQ // QB
FWD_BLOCK_UNROLL = 8
BWD_BLOCK_UNROLL = 4
PAD_ROWS = SEQ + 2 * ATTN_HALF


def _qk_norm_fn(x, gain):
    ms = jnp.dot(x * x, _head_sum_matrix(PAIR_W), precision=SUM_PRECISION, preferred_element_type=F32) * (1.0 / HEAD)
    return x * lax.rsqrt(ms + NORM_EPS) * gain


def _deinterleave(dst_ref, src_ref, dil, offset):
    length = SEQ // dil
    if dil == 1:
        dst_ref[pl.ds(offset, SEQ), :] = src_ref[...]
        return
    for r in range(dil):
        dst_ref[pl.ds(offset + r * length, length), :] = src_ref[pl.ds(r, length, stride=dil), :]


def _interleave(dst_ref, src_ref, dil, offset):
    length = SEQ // dil
    if dil == 1:
        dst_ref[...] = src_ref[pl.ds(offset, SEQ), :]
        return
    for r in range(dil):
        dst_ref[pl.ds(r, length, stride=dil), :] = src_ref[pl.ds(offset + r * length, length), :]


def _edge_mask(blk, dil):
    length = SEQ // dil
    qi = blk * QB + lax.broadcasted_iota(jnp.int32, (QB, KW), 0)
    kj = blk * QB - ATTN_HALF + lax.broadcasted_iota(jnp.int32, (QB, KW), 1)
    return (kj >= 0) & (kj < SEQ) & ((qi // length) == (kj // length))


def _lane_is_head(hh):
    return (lax.broadcasted_iota(jnp.int32, (1, PAIR_W), 1) // HEAD) == hh


def _dilate_qkv(dil, qn_s, kn_s, v_ref, qd_s, kd_s, vd_s):
    _deinterleave(qd_s, qn_s, dil, 0)
    _deinterleave(kd_s, kn_s, dil, ATTN_HALF)
    _deinterleave(vd_s, v_ref, dil, ATTN_HALF)


def _attn_branch_fwd(br, dil, qn_s, kn_s, v_ref, bias_ref, qd_s, kd_s, vd_s, od_s, ld_s):
    _dilate_qkv(dil, qn_s, kn_s, v_ref, qd_s, kd_s, vd_s)

    def step(blk, carry):
        rows = pl.ds(pl.multiple_of(blk * QB, QB), QB)
        win = pl.ds(pl.multiple_of(blk * QB, QB), KW)
        qb, kw, vw = qd_s[rows, :], kd_s[win, :], vd_s[win, :]
        edge = _edge_mask(blk, dil)
        out, lse = 0.0, 0.0
        for hh in range(2):
            is_h = _lane_is_head(hh)
            s = _dot_nt(jnp.where(is_h, qb, 0.0), kw) * (HEAD ** -0.5) + bias_ref[br, hh]
            s = jnp.where(edge, s, NEG_INF)
            m = jnp.max(s, axis=-1, keepdims=True)
            l_h = m + jnp.log(jnp.sum(jnp.exp(s - m), axis=-1, keepdims=True))
            out = out + jnp.where(is_h, _dot(jnp.exp(s - l_h), vw), 0.0)
            lse = lse + jnp.where(is_h, l_h, 0.0)
        od_s[rows, :] = out
        ld_s[rows, :] = lse
        return carry

    _block_loop(step, FWD_BLOCK_UNROLL)


def _block_loop(step, unroll):
    def body(i, carry):
        for k in range(unroll):
            carry = step(i * unroll + k, carry)
        return carry

    lax.fori_loop(0, N_QB // unroll, body, 0)


def _attn_specs():
    col = lambda c0: (lambda p, b: (b, c0 // PAIR_W + p))
    return [pl.BlockSpec((SEQ, PAIR_W), col(C_Q)), pl.BlockSpec((SEQ, PAIR_W), col(C_K)), pl.BlockSpec((SEQ, PAIR_W), col(C_V)),
            pl.BlockSpec((1, PAIR_W), lambda p, b: (0, 0)), pl.BlockSpec((1, PAIR_W), lambda p, b: (0, 0)),
            pl.BlockSpec((3, 2, QB, KW), lambda p, b: (0, p, 0, 0))]


def _attn_scratch():
    seq = pltpu.VMEM((SEQ, PAIR_W), F32)
    pad = pltpu.VMEM((PAD_ROWS, PAIR_W), F32)
    return [seq, seq, seq, pad, pad, seq, seq]


def _zero_pads(*refs):
    for ref in refs:
        ref[pl.ds(0, ATTN_HALF), :] = jnp.zeros((ATTN_HALF, PAIR_W), F32)
        ref[pl.ds(ATTN_HALF + SEQ, ATTN_HALF), :] = jnp.zeros((ATTN_HALF, PAIR_W), F32)


ROW_STEP = 256


def _row_steps(fn, init=0):
    return lax.fori_loop(0, SEQ // ROW_STEP, lambda i, c: fn(pl.ds(pl.multiple_of(i * ROW_STEP, ROW_STEP), ROW_STEP), c), init)


def _interleave_add(acc_ref, src_ref, dil, offset):
    length = SEQ // dil
    if dil == 1:
        acc_ref[...] += src_ref[pl.ds(offset, SEQ), :]
        return
    for r in range(dil):
        acc_ref[pl.ds(r, length, stride=dil), :] += src_ref[pl.ds(offset + r * length, length), :]


def _attn_norm_qk(q_ref, k_ref, qg_ref, kg_ref, qn_s, kn_s):
    def norm(rows, carry):
        qn_s[rows, :] = _qk_norm_fn(q_ref[rows, :], qg_ref[...])
        kn_s[rows, :] = _qk_norm_fn(k_ref[rows, :], kg_ref[...])
        return carry

    _row_steps(norm)


def _attn_forward_all(q_ref, k_ref, v_ref, qg_ref, kg_ref, bias_ref, qn_s, kn_s, qd_s, kd_s, vd_s, od_s, ld_s, on_s, ln_s):
    _attn_norm_qk(q_ref, k_ref, qg_ref, kg_ref, qn_s, kn_s)
    _zero_pads(kd_s, vd_s)
    for br, dil in enumerate(ATTN_DILS):
        _attn_branch_fwd(br, dil, qn_s, kn_s, v_ref, bias_ref, qd_s, kd_s, vd_s, od_s, ld_s)
        _interleave(on_s.at[br], od_s, dil, 0)
        _interleave(ln_s.at[br], ld_s, dil, 0)


def _merge_weights(ln_s, rows):
    l0, l1, l2 = ln_s[0, rows, :], ln_s[1, rows, :], ln_s[2, rows, :]
    m = jnp.maximum(jnp.maximum(l0, l1), l2)
    e = [jnp.exp(l0 - m), jnp.exp(l1 - m), jnp.exp(l2 - m)]
    den = e[0] + e[1] + e[2]
    return [e[0] / den, e[1] / den, e[2] / den]


def _attn_fwd(proj, q_gain, k_gain, bias, hook=None):
    def body(q_ref, k_ref, v_ref, qg_ref, kg_ref, bias_ref, o_ref, on_s, ln_s, qn_s, kn_s, qd_s, kd_s, vd_s, od_s, ld_s):
        _attn_forward_all(q_ref, k_ref, v_ref, qg_ref, kg_ref, bias_ref, qn_s, kn_s, qd_s, kd_s, vd_s, od_s, ld_s, on_s, ln_s)

        def merge(rows, carry):
            w = _merge_weights(ln_s, rows)
            o_ref[rows, :] = (w[0] * on_s[0, rows, :] + w[1] * on_s[1, rows, :] + w[2] * on_s[2, rows, :]).astype(BF16)
            return carry

        _row_steps(merge)

    kept = pl.BlockSpec((3, SEQ, PAIR_W), lambda p, b: (0, b, p))
    return _hooked_call(body, grid=(2, B_LOC), in_specs=_attn_specs(),
                        out_specs=[pl.BlockSpec((SEQ, PAIR_W), lambda p, b: (b, p)), kept, kept],
                        out_shape=[S_((T, 2 * PAIR_W), BF16), S_((3, T, 2 * PAIR_W), F32), S_((3, T, 2 * PAIR_W), F32)],
                        scratch_shapes=_attn_scratch(), args=(proj, proj, proj, q_gain, k_gain, bias), hook=hook, name="attn_fwd")


def _attn_bwd(proj, q_gain, k_gain, bias, dy, kept_o, kept_l, hook=None):
    def body(q_ref, k_ref, v_ref, qg_ref, kg_ref, bias_ref, dy_ref, on_ref, ln_ref,
             dq_ref, dk_ref, dv_ref, dqg_ref, dkg_ref, dbias_ref,
             qn_s, kn_s, qd_s, kd_s, vd_s, od_s, ld_s, don_s, dln_s, dod_s, dld_s, dqd_s, dkd_s, dvd_s, dqn_s, dkn_s, dvn_s):
        first = pl.program_id(1) == 0
        _attn_norm_qk(q_ref, k_ref, qg_ref, kg_ref, qn_s, kn_s)
        _zero_pads(kd_s, vd_s)

        def clear_acc(rows, carry):
            dqn_s[rows, :] = jnp.zeros((ROW_STEP, PAIR_W), F32)
            dkn_s[rows, :] = jnp.zeros((ROW_STEP, PAIR_W), F32)
            dvn_s[rows, :] = jnp.zeros((ROW_STEP, PAIR_W), F32)
            return carry

        _row_steps(clear_acc)

        @pl.when(first)
        def _():
            dbias_ref[...] = jnp.zeros_like(dbias_ref)

        def merge_bwd(rows, carry):
            w = _merge_weights(ln_ref, rows)
            dy = dy_ref[rows, :]
            same_head = _head_sum_matrix(PAIR_W)
            dws = [jnp.dot(dy * on_ref[j, rows, :], same_head, precision=SUM_PRECISION, preferred_element_type=F32) for j in range(3)]
            dbar = w[0] * dws[0] + w[1] * dws[1] + w[2] * dws[2]
            for j in range(3):
                don_s[j, rows, :] = w[j] * dy
                dln_s[j, rows, :] = w[j] * (dws[j] - dbar)
            return carry

        _row_steps(merge_bwd)
        for br, dil in enumerate(ATTN_DILS):
            _dilate_qkv(dil, qn_s, kn_s, v_ref, qd_s, kd_s, vd_s)
            _deinterleave(od_s, on_ref.at[br], dil, 0)
            _deinterleave(ld_s, ln_ref.at[br], dil, 0)
            _deinterleave(dod_s, don_s.at[br], dil, 0)
            _deinterleave(dld_s, dln_s.at[br], dil, 0)

            def clear(rows, carry):
                dkd_s[rows, :] = jnp.zeros((ROW_STEP, PAIR_W), F32)
                dvd_s[rows, :] = jnp.zeros((ROW_STEP, PAIR_W), F32)
                return carry

            _row_steps(clear)
            tail = pl.ds(SEQ, 2 * ATTN_HALF)
            dkd_s[tail, :] = jnp.zeros((2 * ATTN_HALF, PAIR_W), F32)
            dvd_s[tail, :] = jnp.zeros((2 * ATTN_HALF, PAIR_W), F32)

            def step(blk, carry, br=br, dil=dil):
                rows = pl.ds(pl.multiple_of(blk * QB, QB), QB)
                win = pl.ds(pl.multiple_of(blk * QB, QB), KW)
                qb, kw, vw = qd_s[rows, :], kd_s[win, :], vd_s[win, :]
                do_b, dl_b, o_b, l_b = dod_s[rows, :], dld_s[rows, :], od_s[rows, :], ld_s[rows, :]
                edge = _edge_mask(blk, dil)
                dq, dk, dv = 0.0, 0.0, 0.0
                for hh in range(2):
                    is_h = _lane_is_head(hh)
                    pick = (lax.broadcasted_iota(jnp.int32, (1, PAIR_W), 1) == hh * HEAD).astype(F32)
                    q_h = jnp.where(is_h, qb, 0.0)
                    do_h = jnp.where(is_h, do_b, 0.0)
                    s = _dot_nt(q_h, kw) * (HEAD ** -0.5) + bias_ref[br, hh]
                    s = jnp.where(edge, s, NEG_INF)
                    p = jnp.exp(s - jnp.sum(l_b * pick, axis=-1, keepdims=True))
                    dp = _dot_nt(do_h, vw)
                    delta = jnp.sum(do_h * o_b, axis=-1, keepdims=True)
                    ds = p * (dp - delta + jnp.sum(dl_b * pick, axis=-1, keepdims=True))
                    dbias_ref[br, hh] += ds
                    dq = dq + jnp.where(is_h, _dot(ds, kw), 0.0) * (HEAD ** -0.5)
                    dk = dk + _dot_tn(ds, q_h) * (HEAD ** -0.5)
                    dv = dv + _dot_tn(p, do_h)
                dqd_s[rows, :] = dq
                dkd_s[win, :] += dk
                dvd_s[win, :] += dv
                return carry

            _block_loop(step, BWD_BLOCK_UNROLL)
            _interleave_add(dqn_s, dqd_s, dil, 0)
            _interleave_add(dkn_s, dkd_s, dil, ATTN_HALF)
            _interleave_add(dvn_s, dvd_s, dil, ATTN_HALF)

        def norm_bwd(rows, carry):
            _, q_vjp = jax.vjp(_qk_norm_fn, q_ref[rows, :], qg_ref[...])
            _, k_vjp = jax.vjp(_qk_norm_fn, k_ref[rows, :], kg_ref[...])
            g_q, g_qg = q_vjp(dqn_s[rows, :])
            g_k, g_kg = k_vjp(dkn_s[rows, :])
            dq_ref[rows, :] = g_q.astype(BF16)
            dk_ref[rows, :] = g_k.astype(BF16)
            dv_ref[rows, :] = dvn_s[rows, :].astype(BF16)
            return carry[0] + g_qg, carry[1] + g_kg

        g_qg, g_kg = _row_steps(norm_bwd, (jnp.zeros((1, PAIR_W), F32), jnp.zeros((1, PAIR_W), F32)))
        pad7 = jnp.zeros((7, PAIR_W), F32)
        new_q = jnp.concatenate([g_qg, pad7], axis=0)
        new_k = jnp.concatenate([g_kg, pad7], axis=0)

        @pl.when(first)
        def _():
            dqg_ref[...] = new_q
            dkg_ref[...] = new_k

        @pl.when(jnp.logical_not(first))
        def _():
            dqg_ref[...] += new_q
            dkg_ref[...] += new_k

    seq = pltpu.VMEM((SEQ, PAIR_W), F32)
    seq3 = pltpu.VMEM((3, SEQ, PAIR_W), F32)
    pad = pltpu.VMEM((PAD_ROWS, PAIR_W), F32)
    kept = pl.BlockSpec((3, SEQ, PAIR_W), lambda p, b: (0, b, p))
    out_blk = pl.BlockSpec((SEQ, PAIR_W), lambda p, b: (b, p))
    gain_blk = pl.BlockSpec((None, 8, PAIR_W), lambda p, b: (p, 0, 0))
    return _hooked_call(
        body, grid=(2, B_LOC),
        in_specs=_attn_specs() + [pl.BlockSpec((SEQ, PAIR_W), lambda p, b: (b, (SSD_INNER + GMLP_W) // PAIR_W + p)), kept, kept],
        out_specs=[out_blk, out_blk, out_blk, gain_blk, gain_blk, pl.BlockSpec((3, 2, QB, KW), lambda p, b: (0, p, 0, 0))],
        out_shape=[S_((T, 2 * PAIR_W), BF16)] * 3 + [S_((2, 8, PAIR_W), F32)] * 2 + [S_((3, 4, QB, KW), F32)],
        scratch_shapes=_attn_scratch() + [seq3, seq3, seq, seq, seq, pad, pad, seq, seq, seq],
        args=(proj, proj, proj, q_gain, k_gain, bias, dy, kept_o, kept_l), hook=hook, name="attn_bwd")


def _rel_bucket(rel):
    nb = 16
    max_exact = nb // 2
    n = jnp.abs(rel)
    large = max_exact + (jnp.log(jnp.maximum(n, 1).astype(F32) / max_exact) / math.log(1024 / max_exact) * (nb - max_exact)).astype(jnp.int32)
    large = jnp.minimum(large, nb - 1)
    return jnp.where(rel > 0, nb, 0) + jnp.where(n < max_exact, n, large)


def _attn_bias(rel_table):
    rel = jnp.arange(KW)[None, :] - ATTN_HALF - jnp.arange(QB)[:, None]
    inside = (jnp.abs(rel) <= ATTN_HALF)
    out = []
    for dil in ATTN_DILS:
        one_hot = (_rel_bucket(rel * dil)[None] == jnp.arange(32)[:, None, None]).astype(F32)
        b = jnp.einsum("kh,kts->hts", rel_table, one_hot, precision=HI)
        out.append(jnp.where(inside[None], b, NEG_INF))
    return jnp.stack(out).astype(F32)


def _place():
    return lax.axis_index("x"), lax.axis_index("y"), lax.axis_index("c")


def _allgather8(buf, name):
    rows = buf.shape[0]
    flips = [(fx, fy, fc) for fx in (0, 1) for fy in (0, 1) for fc in (0, 1)][1:]

    def body(in_ref, out_ref, send_sems, recv_sems):
        x, y, c = _place()
        me = 4 * x + 2 * y + c
        peers = [(1 - x if fx else x, 1 - y if fy else y, 1 - c if fc else c) for fx, fy, fc in flips]

        def copy(k, slot, peer):
            return pltpu.make_async_remote_copy(src_ref=in_ref, dst_ref=out_ref.at[slot], send_sem=send_sems.at[k],
                                                recv_sem=recv_sems.at[k], device_id=peer, device_id_type=MESH)

        sends = [copy(k, me, peer) for k, peer in enumerate(peers)]
        for cp in sends:
            cp.start()
        for k, (px, py, pc) in enumerate(peers):
            copy(k, 4 * px + 2 * py + pc, (px, py, pc)).wait_recv()
        for cp in sends:
            cp.wait_send()

    slots = pl.pallas_call(body, in_specs=[ANY], out_specs=ANY, out_shape=S_((N_DEV, rows, 128), F32),
                           scratch_shapes=[pltpu.SemaphoreType.DMA((7,)), pltpu.SemaphoreType.DMA((7,))], name=name)(buf)
    x, y, c = _place()
    return lax.dynamic_update_index_in_dim(slots, buf, 4 * x + 2 * y + c, axis=0)


N_BIG = 4


def _other_chips(x, y):
    return [(1 - x, y), (x, 1 - y), (1 - x, 1 - y)]


def _hooked_call(body, *, grid, in_specs, out_specs, out_shape, scratch_shapes, args, hook, name):
    if hook is None:
        res = pl.pallas_call(body, grid=grid, in_specs=in_specs, out_specs=out_specs, out_shape=out_shape,
                             scratch_shapes=scratch_shapes, compiler_params=_cp(), name=name)(*args)
        return res, None
    counts = (len(in_specs), len(hook["arrays"]), len(out_specs), len(hook["out_shape"]), len(scratch_shapes), len(hook["sems"]))

    def wrapped(*refs):
        groups, pos = [], 0
        for n in counts:
            groups.append(refs[pos:pos + n])
            pos += n
        ins, h_ins, outs, h_outs, scr, sems = groups
        idx = [pl.program_id(a) for a in range(len(grid))]
        first = functools.reduce(jnp.logical_and, [i == 0 for i in idx])
        last = functools.reduce(jnp.logical_and, [i == g - 1 for i, g in zip(idx, grid)])

        @pl.when(first)
        def _():
            hook["start"](h_ins, h_outs, sems)

        body(*ins, *outs, *scr)

        @pl.when(last)
        def _():
            hook["finish"](h_ins, h_outs, sems)

    res = pl.pallas_call(wrapped, grid=grid, in_specs=list(in_specs) + [ANY] * counts[1], out_specs=list(out_specs) + [ANY] * counts[3],
                         out_shape=list(out_shape) + list(hook["out_shape"]), scratch_shapes=list(scratch_shapes) + list(hook["sems"]),
                         compiler_params=_cp(), name=name + "_" + hook["name"])(*args, *hook["arrays"])
    return res[:counts[2]], res[counts[2]:]


def _run_hook(hook):
    n_in, n_out = len(hook["arrays"]), len(hook["out_shape"])

    def body(*refs):
        h_ins, h_outs, sems = refs[:n_in], refs[n_in:n_in + n_out], refs[n_in + n_out:]
        hook["start"](h_ins, h_outs, sems)
        hook["finish"](h_ins, h_outs, sems)

    return pl.pallas_call(body, in_specs=[ANY] * n_in, out_specs=[ANY] * n_out, out_shape=list(hook["out_shape"]),
                          scratch_shapes=list(hook["sems"]), name=hook["name"])(*hook["arrays"])


def _remote(src, dst, send_sem, recv_sem, peer):
    return pltpu.make_async_remote_copy(src_ref=src, dst_ref=dst, send_sem=send_sem, recv_sem=recv_sem, device_id=peer, device_id_type=MESH)


def _gather_hook(shards):
    def copies(h_ins, h_outs, sems, kind):
        ici_send, ici_recv, d2d_send, d2d_recv = sems
        x, y, c = _place()
        chip = 2 * x + y
        out = []
        for t in range(len(shards)):
            half = shards[t].shape[0] // 2
            mine_r, other_r = pl.ds(c * half, half), pl.ds((1 - c) * half, half)
            for f, (px, py) in enumerate(_other_chips(x, y)):
                k, peer_chip = 3 * t + f, 2 * px + py
                if kind in ("send", "land"):
                    slot = chip if kind == "send" else peer_chip
                    out.append(_remote(h_ins[t].at[mine_r], h_outs[t].at[slot, mine_r], ici_send.at[k], ici_recv.at[k], (px, py, c)))
                else:
                    rows = mine_r if kind == "pass" else other_r
                    out.append(_remote(h_outs[t].at[peer_chip, rows], h_outs[t].at[peer_chip, rows], d2d_send.at[k], d2d_recv.at[k],
                                       (x, y, 1 - c)))
        return out

    def start(h_ins, h_outs, sems):
        for cp in copies(h_ins, h_outs, sems, "send"):
            cp.start()

    def finish(h_ins, h_outs, sems):
        passed = copies(h_ins, h_outs, sems, "pass")
        for landed, forward in zip(copies(h_ins, h_outs, sems, "land"), passed):
            landed.wait_recv()
            forward.start()
        for cp in copies(h_ins, h_outs, sems, "get"):
            cp.wait_recv()
        for cp in copies(h_ins, h_outs, sems, "send") + passed:
            cp.wait_send()

    return dict(name="gather", arrays=list(shards), out_shape=[S_((N_CHIPS,) + s.shape, s.dtype) for s in shards],
                sems=[pltpu.SemaphoreType.DMA((3 * len(shards),)) for _ in range(4)], start=start, finish=finish)


def _to_sibling_hook(parts, half_rows=False):
    def copies(h_ins, h_outs, sems):
        x, y, c = _place()
        out = []
        for t, p in enumerate(parts):
            src = h_ins[t].at[:, pl.ds((1 - c) * (p.shape[1] // 2), p.shape[1] // 2)] if half_rows else h_ins[t]
            out.append(_remote(src, h_outs[t], sems[0].at[t], sems[1].at[t], (x, y, 1 - c)))
        return out

    def start(h_ins, h_outs, sems):
        for cp in copies(h_ins, h_outs, sems):
            cp.start()

    def finish(h_ins, h_outs, sems):
        cps = copies(h_ins, h_outs, sems)
        for cp in cps:
            cp.wait_recv()
        for cp in cps:
            cp.wait_send()

    shapes = [(p.shape[0], p.shape[1] // 2, p.shape[2]) if half_rows else p.shape for p in parts]
    return dict(name="to_sibling", arrays=list(parts), out_shape=[S_(s, p.dtype) for s, p in zip(shapes, parts)],
                sems=[pltpu.SemaphoreType.DMA((len(parts),)), pltpu.SemaphoreType.DMA((len(parts),))], start=start, finish=finish)


def _to_chips_hook(parts):
    def copies(h_ins, h_outs, sems):
        x, y, c = _place()
        return [_remote(h_ins[t].at[2 * px + py], h_outs[t].at[f], sems[0].at[3 * t + f], sems[1].at[3 * t + f], (px, py, c))
                for t in range(len(parts)) for f, (px, py) in enumerate(_other_chips(x, y))]

    def start(h_ins, h_outs, sems):
        for cp in copies(h_ins, h_outs, sems):
            cp.start()

    def finish(h_ins, h_outs, sems):
        cps = copies(h_ins, h_outs, sems)
        for cp in cps:
            cp.wait_recv()
        for cp in cps:
            cp.wait_send()

    return dict(name="to_chips", arrays=list(parts), out_shape=[S_((3,) + p.shape[1:], p.dtype) for p in parts],
                sems=[pltpu.SemaphoreType.DMA((3 * len(parts),)), pltpu.SemaphoreType.DMA((3 * len(parts),))], start=start, finish=finish)


def _add_pair(a, b, core, name):
    n, half, c = b.shape

    def body(core_ref, a_ref, b_ref, o_ref):
        o_ref[...] = (a_ref[...].astype(F32) + b_ref[...].astype(F32)).astype(BF16)

    spec = pltpu.PrefetchScalarGridSpec(
        num_scalar_prefetch=1, grid=(n,),
        in_specs=[pl.BlockSpec((None, half, c), lambda i, core_ref: (i, core_ref[0], 0)),
                  pl.BlockSpec((None, half, c), lambda i, core_ref: (i, 0, 0))],
        out_specs=pl.BlockSpec((None, half, c), lambda i, core_ref: (i, 0, 0)))
    return pl.pallas_call(body, grid_spec=spec, out_shape=S_(b.shape, BF16), name=name)(core.reshape(1).astype(jnp.int32), a, b)


def _add_four(own, got, rows, name):
    n, r, c = own.shape

    def body(a_ref, g_ref, o_ref):
        o_ref[...] = ((a_ref[...].astype(F32) + g_ref[0].astype(F32)) + g_ref[1].astype(F32)) + g_ref[2].astype(F32)

    blk = pl.BlockSpec((None, rows, c), lambda i, j: (i, j, 0))
    return pl.pallas_call(body, grid=(n, r // rows), in_specs=[blk, pl.BlockSpec((3, None, rows, c), lambda i, j: (0, i, j, 0))],
                          out_specs=blk, out_shape=S_(own.shape, F32), name=name)(own, got)


def _sum_slots(slots):
    n, rows = slots.shape[:2]

    def body(s_ref, o_ref):
        tot = s_ref[0]
        for k in range(1, n):
            tot = tot + s_ref[k]
        o_ref[...] = tot

    return pl.pallas_call(body, out_shape=S_((rows, 128), F32), name="sum_slots")(slots)


def _add_two(a, b):
    def body(a_ref, b_ref, o_ref):
        o_ref[...] = a_ref[...] + b_ref[...]

    return pl.pallas_call(body, out_shape=S_(a.shape, a.dtype), name="add_two")(a, b)


def _allreduce_small(buf, chip):
    (theirs,) = _run_hook(dict(_to_sibling_hook([buf]), name="small_to_sibling"))
    pair = _add_two(buf, theirs)
    (slots,) = _run_hook(dict(_gather_hook([pair]), name="small_gather"))
    return _sum_slots(lax.dynamic_update_index_in_dim(slots, pair, chip, axis=0))


SMALL = ("mix_norm_gain", "ssd_conv_w", "ssd_conv_b", "ssd_dt_bias", "ssd_a_log", "ssd_d", "ssd_out_gain", "gmlp_v_gain",
         "gmlp_w_s", "gmlp_b_s", "attn_q_gain", "attn_k_gain", "rel_bias_table", "ffn_norm_gain", "ffn_conv_w", "ffn_conv_b")
BIG = ("w_in", "w_out", "ffn_w_up", "ffn_w_down")
WEIGHTS = ("mix_norm_gain", "w_in", "ssd_conv_w", "ssd_conv_b", "ssd_dt_bias", "ssd_a_log", "ssd_d", "ssd_out_gain", "gmlp_v_gain",
           "gmlp_w_s", "gmlp_b_s", "attn_q_gain", "attn_k_gain", "rel_bias_table", "w_out", "ffn_norm_gain", "ffn_w_up",
           "ffn_conv_w", "ffn_conv_b", "ffn_w_down")
ADAM_ROWS = {"w_in": 512, "w_out": 128, "ffn_w_up": 256, "ffn_w_down": 352}


PACK_ROWS = 64


def _packed_rows(shape):
    return -(-int(np.prod(shape)) // 1024) * 8


def _pack(arrays):
    parts = []
    for a in arrays:
        rows = _packed_rows(a.shape)
        flat = a.reshape(-1).astype(F32)
        parts.append(jnp.pad(flat, (0, rows * 128 - flat.shape[0])).reshape(rows, 128))
    total = sum(p.shape[0] for p in parts)
    tail = -total % PACK_ROWS
    if tail:
        parts.append(jnp.zeros((tail, 128), F32))
    return jnp.concatenate(parts, axis=0)


def _unpack(buf, shapes):
    out, row = [], 0
    for s in shapes:
        rows, n = _packed_rows(s), int(np.prod(s))
        out.append(buf[row:row + rows].reshape(-1)[:n].reshape(s))
        row += rows
    return out


def _perm_cols(w):
    pad = jnp.zeros(w.shape[:-1] + (NP - IN_WIDTH,), w.dtype)
    return jnp.concatenate([w[..., :1536], w[..., 1552:], w[..., 1536:1552], pad], axis=-1)


def _unperm_cols(w):
    return jnp.concatenate([w[..., :1536], w[..., C_DT:C_DT + 16], w[..., 1536:C_DT]], axis=-1)


def _layer_params(l, p, conv5_w, conv3_w, bias):
    def make(mix_g, conv5, conv5_b, dt_bias, a_log, d_skip, out_gain, v_gain, w_s, b_s, q_gain, k_gain, ffn_g, conv3, conv3_b):
        lanes = lambda a: jnp.pad(a.reshape(1, 16), ((0, 0), (0, 112)))
        cw3 = jnp.pad(jnp.transpose(conv3.reshape(3, 2, FFN_DIM), (1, 0, 2)), ((0, 0), (0, 5), (0, 0)))
        return dict(mix_g=mix_g.reshape(1, D_MODEL), cw5=jnp.pad(conv5, ((0, 3), (0, 0))), cb5=conv5_b.reshape(1, SSD_XBC),
                    dtb=lanes(dt_bias), alog=lanes(a_log), d_exp=jnp.repeat(d_skip, HEAD).reshape(1, SSD_INNER),
                    out_gain=out_gain.reshape(1, SSD_INNER), v_gain=v_gain.reshape(1, GMLP_W), w_s=w_s,
                    b_exp=jnp.repeat(b_s.T, HEAD, axis=1), q_gain=jnp.tile(q_gain, 2).reshape(1, PAIR_W),
                    k_gain=jnp.tile(k_gain, 2).reshape(1, PAIR_W), ffn_g=ffn_g.reshape(1, D_MODEL), cw3=cw3,
                    cb3=conv3_b.reshape(2, 1, FFN_DIM))

    args = (p["mix_norm_gain"][l], conv5_w[l], p["ssd_conv_b"][l], p["ssd_dt_bias"][l], p["ssd_a_log"][l], p["ssd_d"][l],
            p["ssd_out_gain"][l], p["gmlp_v_gain"][l], p["gmlp_w_s"][l], p["gmlp_b_s"][l], p["attn_q_gain"][l], p["attn_k_gain"][l],
            p["ffn_norm_gain"][l], conv3_w[l], p["ffn_conv_b"][l])
    return jax.vjp(make, *args)


def _forward_layer(x, h, lp, w, bias, next_gain=None, hooks=None, resolve=None):
    hooks = hooks or {}
    proj = _mm_nn(h, w["w_in"], tm=1024, tn=1024, tk=1024, out_dtype=F32, name="mm_proj")
    xc = _ssd_pre_fwd(proj, lp["cw5"], lp["cb5"])
    (y_ssd, ssd_sum, ssd_states), got_self = _ssd_scan_fwd(xc, proj, lp["dtb"], lp["alog"], lp["d_exp"], lp["out_gain"],
                                                           hook=hooks.get("ssd"))
    if resolve is not None and got_self is not None:
        w = dict(w, **resolve(got_self))
    y_gmlp = _gmlp_fwd(proj, lp["v_gain"], lp["w_s"], lp["b_exp"])
    (y_attn, attn_o, attn_l), got_attn = _attn_fwd(proj, lp["q_gain"], lp["k_gain"], bias, hook=hooks.get("attn"))
    y = jnp.concatenate([y_ssd, y_gmlp, y_attn], axis=1)
    x2, hn = _mm_nn(y, w["w_out"], tm=1024, tn=1024, tk=1024, out_dtype=F32, res=x, norm_gain=lp["ffn_g"], name="mm_out")
    up3 = _mm_up(hn, w["ffn_w_up"])
    act, got_gate = _convgate_fwd(up3, lp["cw3"], lp["cb3"], hook=hooks.get("gate"))
    if next_gain is None:
        x3, h_next = _mm_nn(act, w["ffn_w_down"], tm=1024, tn=1024, tk=HALF_TILE, out_dtype=F32, res=x2, name="mm_down"), None
    else:
        x3, h_next = _mm_nn(act, w["ffn_w_down"], tm=1024, tn=1024, tk=HALF_TILE, out_dtype=F32, res=x2, norm_gain=next_gain,
                            name="mm_down_norm")
    saved = dict(x=x, h=h, proj=proj, xc=xc, y=y, x2=x2, hn=hn, up3=up3, act=act, attn_o=attn_o, attn_l=attn_l,
                 ssd_sum=ssd_sum, ssd_states=ssd_states)
    return x3, h_next, saved, w, dict(ssd=got_self, attn=got_attn, gate=got_gate)


def _backward_layer(dx3, sv, lp, w, bias, pending=None, reducer=None):
    d_act = _mm_nt(dx3, w["ffn_w_down"], tm=1024, tn=HALF_TILE, tk=1024, out_dtype=F32, name="mm_dact")
    dw_down = _mm_tn(sv["act"], dx3, tm=HALF_TILE, tn=1024, tk=1024, out_dtype=BF16, name="mm_dwdown")
    (dup3, dcw3), from_sibling = _convgate_bwd(sv["up3"], lp["cw3"], lp["cb3"], d_act, hook=pending.sibling_hook() if pending else None)
    if pending:
        pending.add_sibling(from_sibling)
    dx2, d_ffn_g = _mm_dhn(dup3, w["ffn_w_up"], sv["x2"], lp["ffn_g"], dx3)
    dw_up = _mm_dwup(sv["hn"], dup3)
    d_y = _mm_nt(dx2, w["w_out"], tm=1024, tn=1024, tk=1024, out_dtype=F32, name="mm_dy")
    dw_out = _mm_tn(sv["y"], dx2, tm=1024, tn=1024, tk=1024, out_dtype=BF16, name="mm_dwout")
    early = reducer(("w_out", "ffn_w_up", "ffn_w_down"), (dw_out, dw_up, dw_down)) if reducer else None
    proj, xc = sv["proj"], sv["xc"]
    (dxs, dbc, dcc, dz, ddt2, ddtb2, dal2, d_dexp, d_outg), from_chips = _ssd_scan_bwd(
        xc, proj, lp["dtb"], lp["alog"], lp["d_exp"], lp["out_gain"], d_y, sv["ssd_sum"], sv["ssd_states"],
        hook=pending.chips_hook() if pending else None)
    if pending:
        pending.add_chips(from_chips)
    (d_xbc, dcw5), from_sibling = _ssd_pre_bwd(proj, lp["cw5"], lp["cb5"], jnp.concatenate([dxs, dbc, dcc], axis=1),
                                               hook=early.sibling_hook() if early else None)
    if early:
        early.add_sibling(from_sibling)
    d_gu, d_gv, d_vg, d_ws, d_bexp = _gmlp_bwd(proj, lp["v_gain"], lp["w_s"], lp["b_exp"], d_y)
    (d_q, d_k, d_v, d_qg2, d_kg2, d_bias), from_chips = _attn_bwd(proj, lp["q_gain"], lp["k_gain"], bias, d_y, sv["attn_o"], sv["attn_l"],
                                                                  hook=early.chips_hook() if early else None)
    if early:
        early.add_chips(from_chips)
    d_dt = (ddt2[0] + ddt2[1]).astype(BF16)
    d_proj = jnp.concatenate([d_xbc, dz, d_gu, d_gv, d_q, d_k, d_v, d_dt, jnp.zeros((T, NP - C_DT - 128), BF16)], axis=1)
    dx, d_mix_g = _mm_nt(d_proj, w["w_in"], tm=1024, tn=1024, tk=1024, out_dtype=F32, res=dx2, norm_bwd=(sv["x"], lp["mix_g"]), name="mm_dh")
    dw_in = _mm_tn(sv["h"], d_proj, tm=1024, tn=1024, tk=1024, out_dtype=BF16, name="mm_dwin")
    late = None
    if reducer:
        late = reducer(("w_in",), (dw_in,))
        late.run_alone()
    d_lp = dict(mix_g=d_mix_g, cw5=dcw5[:8] * (jnp.arange(8) < 5)[:, None].astype(F32), cb5=dcw5[5:6],
                dtb=(ddtb2[0, :1] + ddtb2[1, :1]), alog=(dal2[0, :1] + dal2[1, :1]), d_exp=d_dexp, out_gain=d_outg,
                v_gain=d_vg, w_s=d_ws, b_exp=d_bexp, q_gain=d_qg2[0, :1] + d_qg2[1, :1], k_gain=d_kg2[0, :1] + d_kg2[1, :1],
                ffn_g=d_ffn_g, cw3=dcw3 * (jnp.arange(8) < 3)[None, :, None].astype(F32), cb3=dcw3[:, 3:4])
    return dx, dict(w_in=dw_in, w_out=dw_out, ffn_w_up=dw_up, ffn_w_down=dw_down), d_lp, d_bias, (early, late)


def _to_shard_major(name, dw):
    if name == "ffn_w_up":
        return dw
    if name == "w_in":
        r, c = dw.shape[0], IN_WIDTH
        return jnp.transpose(_unperm_cols(dw).reshape(r, N_CHIPS, c // N_CHIPS), (1, 0, 2))
    r, c = dw.shape
    return dw.reshape(N_CHIPS, r // N_CHIPS, c)


def _whole_weight(name, gathered, own, chip):
    if name == "w_in":
        return _perm_cols(jnp.concatenate([jnp.where(chip == k, own, gathered[k]) for k in range(N_CHIPS)], axis=1))
    w = lax.dynamic_update_index_in_dim(gathered, own, chip, axis=0)
    return w if name == "ffn_w_up" else w.reshape(N_CHIPS * own.shape[0], own.shape[1])


class _LayerReduce:
    def __init__(self, names, dws, chip, core):
        self.names, self.chip, self.core = names, chip, core
        self.parts = [_to_shard_major(n, dw) for n, dw in zip(names, dws)]

    def sibling_hook(self):
        return _to_sibling_hook(self.parts, half_rows=True)

    def add_sibling(self, got):
        self.sums = [_add_pair(a, b, self.core, "add_pair_" + n) for n, a, b in zip(self.names, self.parts, got)]

    def chips_hook(self):
        return _to_chips_hook(self.sums)

    def add_chips(self, got):
        self.half = {}
        for n, s2, g3 in zip(self.names, self.sums, got):
            own = lax.dynamic_index_in_dim(s2, self.chip, axis=0, keepdims=True)
            self.half[n] = _add_four(own, g3[:, None], own.shape[1], "add_four_" + n)[0]

    def run_alone(self):
        self.add_sibling(_run_hook(self.sibling_hook()))
        self.add_chips(_run_hook(self.chips_hook()))


LAYER_SMALL = ("mix_norm_gain", "ssd_conv_w", "ssd_conv_b", "ssd_dt_bias", "ssd_a_log", "ssd_d", "ssd_out_gain", "gmlp_v_gain",
               "gmlp_w_s", "gmlp_b_s", "attn_q_gain", "attn_k_gain", "ffn_norm_gain", "ffn_conv_w", "ffn_conv_b")


def _local_grads(x, loss_target, p, conv5_w, conv3_w, layer_w, exchange=None):
    bias, bias_vjp = jax.vjp(_attn_bias, p["rel_bias_table"])
    xt = x.reshape(T, D_MODEL)
    layer_w = list(layer_w)
    saved, lps, lp_vjps = [], [], []
    if exchange is not None:
        chip, core, own = exchange
        whole = lambda names, layer, gathered: {n: _whole_weight(n, g, own[layer][BIG.index(n)], chip) for n, g in zip(names, gathered)}
    for l in range(DEPTH):
        lp, lp_vjp = _layer_params(l, p, conv5_w, conv3_w, bias)
        lps.append(lp)
        lp_vjps.append(lp_vjp)
    h = _rmsnorm_fwd(xt, lps[0]["mix_g"], "rmsnorm_fwd")
    for l in range(DEPTH):
        lp = lps[l]
        plan = {}
        if exchange is not None and l == 0:
            plan = {"ssd": (BIG[1:], 0), "attn": (BIG[2:], 1), "gate": (BIG[:2], 1)}
        elif exchange is not None and l + 1 < DEPTH:
            plan = {"ssd": (BIG[:2], l + 1), "attn": (BIG[2:3], l + 1), "gate": (BIG[3:], l + 1)}
        hooks = {tag: _gather_hook([own[layer][BIG.index(n)] for n in names]) for tag, (names, layer) in plan.items()}
        xt, h, sv, layer_w[l], got = _forward_layer(xt, h, lp, layer_w[l], bias, lps[l + 1]["mix_g"] if l + 1 < DEPTH else None, hooks,
                                                    resolve=(lambda g: whole(BIG[1:], 0, g)) if l == 0 else None)
        ahead = {}
        for tag, (names, layer) in plan.items():
            if layer == l + 1:
                ahead.update(whole(names, layer, got[tag]))
        if ahead:
            layer_w.append(ahead)
        saved.append(sv)
    dxt, loss_parts = _loss_head(xt, loss_target.reshape(T, D_MODEL))
    loss_local = jnp.sum(loss_parts[::8, 0])

    big_grads = [None] * DEPTH
    small_layers = [None] * DEPTH
    d_bias_tot = jnp.zeros_like(bias)
    pending = None
    for l in reversed(range(DEPTH)):
        last = exchange is not None and l == 0
        dxt, big_grads[l], d_lp, d_bias, own_reduce = _backward_layer(
            dxt, saved[l], lps[l], layer_w[l], bias, pending=pending,
            reducer=(lambda names, dws: _LayerReduce(names, dws, chip, core)) if last else None)
        if pending is not None:
            big_grads[l + 1] = pending.half
        if last:
            big_grads[l] = dict(own_reduce[0].half, **own_reduce[1].half)
        elif exchange is not None:
            pending = _LayerReduce(BIG, [big_grads[l][n] for n in BIG], chip, core)
        small_layers[l] = lp_vjps[l](d_lp)
        d_bias_tot = d_bias_tot + d_bias
    (d_rel_table,) = bias_vjp(d_bias_tot)
    local_small = {n: jnp.stack([small_layers[l][i] for l in range(DEPTH)]) for i, n in enumerate(LAYER_SMALL)}
    local_small["rel_bias_table"] = d_rel_table
    return dxt, loss_local, big_grads, local_small


def kernel(x, mix_norm_gain, w_in, ssd_conv_w, ssd_conv_b, ssd_dt_bias, ssd_a_log, ssd_d, ssd_out_gain, gmlp_v_gain, gmlp_w_s, gmlp_b_s, attn_q_gain, attn_k_gain, rel_bias_table, w_out, ffn_norm_gain, ffn_w_up, ffn_conv_w, ffn_conv_b, ffn_w_down, loss_target, m_mix_norm_gain, m_w_in, m_ssd_conv_w, m_ssd_conv_b, m_ssd_dt_bias, m_ssd_a_log, m_ssd_d, m_ssd_out_gain, m_gmlp_v_gain, m_gmlp_w_s, m_gmlp_b_s, m_attn_q_gain, m_attn_k_gain, m_rel_bias_table, m_w_out, m_ffn_norm_gain, m_ffn_w_up, m_ffn_conv_w, m_ffn_conv_b, m_ffn_w_down, v_mix_norm_gain, v_w_in, v_ssd_conv_w, v_ssd_conv_b, v_ssd_dt_bias, v_ssd_a_log, v_ssd_d, v_ssd_out_gain, v_gmlp_v_gain, v_gmlp_w_s, v_gmlp_b_s, v_attn_q_gain, v_attn_k_gain, v_rel_bias_table, v_w_out, v_ffn_norm_gain, v_ffn_w_up, v_ffn_conv_w, v_ffn_conv_b, v_ffn_w_down):
    env = dict(locals())
    p = {n: env[n] for n in WEIGHTS}
    chip = 2 * lax.axis_index("x") + lax.axis_index("y")
    core = lax.axis_index("c")

    conv_slots = _allgather8(_pack([ssd_conv_w, ffn_conv_w]), "allgather_conv")
    conv_parts = [_unpack(conv_slots[2 * k], [ssd_conv_w.shape, ffn_conv_w.shape]) for k in range(N_CHIPS)]
    conv5_w = jnp.concatenate([cp[0] for cp in conv_parts], axis=-1)
    conv3_w = jnp.concatenate([cp[1] for cp in conv_parts], axis=-1)
    own = [[p[n][l].astype(BF16) for n in BIG] for l in range(DEPTH)]
    (first,) = _run_hook(_gather_hook(own[0][:1]))
    layer_w = [{"w_in": _whole_weight("w_in", first, own[0][0], chip)}]

    dxt, loss_local, reduced, local_small = _local_grads(x, loss_target, p, conv5_w, conv3_w, layer_w, exchange=(chip, core, own))

    small_shapes = [local_small[n].shape for n in SMALL] + [(1,)]
    summed = _unpack(_allreduce_small(_pack([local_small[n] for n in SMALL] + [loss_local.reshape(1)]), chip), small_shapes)
    grads = dict(zip(SMALL, summed[:-1]))
    loss = summed[-1][0]
    grads["ssd_conv_w"] = lax.dynamic_slice_in_dim(grads["ssd_conv_w"], chip * 256, 256, axis=2)
    grads["ffn_conv_w"] = lax.dynamic_slice_in_dim(grads["ffn_conv_w"], chip * (2 * FFN_DIM // N_CHIPS), 2 * FFN_DIM // N_CHIPS, axis=2)

    halves = [jnp.stack([reduced[l][n] for l in range(DEPTH)]) for n in BIG]
    others = _run_hook(dict(_to_sibling_hook(halves), name="swap_halves"))

    delta, new_m, new_v = {}, {}, {}
    for n, mine, other in zip(BIG, halves, others):
        grads[n], delta[n], new_m[n], new_v[n] = _adamw(p[n], mine, other, env["m_" + n], env["v_" + n], core, ADAM_ROWS[n], "adamw_" + n)
    d, nm, nv = _adamw_many([p[n] for n in SMALL], [grads[n] for n in SMALL], [env["m_" + n] for n in SMALL], [env["v_" + n] for n in SMALL])
    for n, a, b, c in zip(SMALL, d, nm, nv):
        delta[n], new_m[n], new_v[n] = a, b, c

    return (loss, dxt.reshape(B_LOC, SEQ, D_MODEL), *[grads[n] for n in WEIGHTS], *[delta[n] for n in WEIGHTS],
            *[new_m[n] for n in WEIGHTS], *[new_v[n] for n in WEIGHTS])
```

```python
import functools
import math

import jax
import jax.numpy as jnp
import numpy as np
from jax import lax
from jax.experimental import pallas as pl
from jax.experimental.pallas import tpu as pltpu

F32 = jnp.float32
BF16 = jnp.bfloat16
HI = lax.Precision.HIGHEST
SUM_PRECISION = lax.Precision.HIGH
MESH = pl.DeviceIdType.MESH
ANY = pl.BlockSpec(memory_space=pl.ANY)

D_MODEL = 1024
SEQ = 2048
B_LOC = 2
T = B_LOC * SEQ
DEPTH = 4
N_CHIPS = 4
N_DEV = 8
HEAD = 64
CHUNK = 128
N_CHUNK = SEQ // CHUNK
SSD_INNER = 512
SSD_XBC = 1024
FFN_DIM = 2816
IN_WIDTH = 2832
NP = 3072
C_XS, C_B, C_C, C_Z, C_GU, C_GV, C_Q, C_K, C_V, C_DT = 0, 512, 768, 1024, 1536, 1792, 2048, 2304, 2560, 2816
NORM_EPS = 1e-6
NEG_INF = -1e30
ATTN_DILS = (1, 4, 16)
ATTN_HALF = 64
ADAM_LR, ADAM_B1, ADAM_B2, ADAM_EPS, ADAM_WD, ADAM_STEP = 0.001, 0.9, 0.999, 1e-08, 0.01, 10
VMEM_LIMIT = 56 * 1024 * 1024

S_ = jax.ShapeDtypeStruct


def _cp():
    return pltpu.CompilerParams(vmem_limit_bytes=VMEM_LIMIT)


def _shift_rows(x, k):
    n = x.shape[0]
    if k == 0:
        return x
    r = pltpu.roll(x, (-k) % n, 0)
    t = lax.broadcasted_iota(jnp.int32, (n, 1), 0)
    return jnp.where((t + k >= 0) & (t + k < n), r, 0.0)


@functools.partial(jax.custom_vjp, nondiff_argnums=(1,))
def _shift(x, k):
    return _shift_rows(x, k)


def _shift_fwd(x, k):
    return _shift_rows(x, k), None


def _shift_bwd(k, _, g):
    return (_shift_rows(g, -k),)


_shift.defvjp(_shift_fwd, _shift_bwd)


def _dwconv(x, taps, bias):
    half = len(taps) // 2
    y = bias
    for k, w in enumerate(taps):
        y = y + w * _shift(x, k - half)
    return y


def _softplus(x):
    return jnp.maximum(x, 0.0) + jnp.log1p(jnp.exp(-jnp.abs(x)))


def _dot(a, b):
    return jnp.dot(a.astype(BF16), b.astype(BF16), preferred_element_type=F32)


def _dot_nt(a, b):
    return lax.dot_general(a.astype(BF16), b.astype(BF16), (((1,), (1,)), ((), ())), preferred_element_type=F32)


def _dot_tn(a, b):
    return lax.dot_general(a.astype(BF16), b.astype(BF16), (((0,), (0,)), ((), ())), preferred_element_type=F32)


def _head_sum_matrix(width):
    i = lax.broadcasted_iota(jnp.int32, (width, width), 0) // HEAD
    j = lax.broadcasted_iota(jnp.int32, (width, width), 1) // HEAD
    return (i == j).astype(F32)


def _matmul(a, b, *, dims, grid, a_spec, b_spec, o_spec, out_shape, acc_shape, res=None, res_spec=None, norm_gain=None,
            norm_bwd=None, name):
    nk = grid[2]
    n_in = 2 + (res is not None) + (norm_gain is not None) + 2 * (norm_bwd is not None)

    def body(*refs):
        a_ref, b_ref = refs[:2]
        r_ref = refs[2] if res is not None else None
        g_ref = refs[n_in - 1] if norm_gain is not None or norm_bwd is not None else None
        x_ref = refs[n_in - 2] if norm_bwd is not None else None
        o_ref = refs[n_in]
        n_ref = refs[n_in + 1] if norm_gain is not None or norm_bwd is not None else None
        row_tile = pl.program_id(0)

        def finish(tot):
            if x_ref is not None:
                xv = x_ref[...]
                scale = lax.rsqrt(jnp.mean(xv * xv, axis=-1, keepdims=True) + NORM_EPS)
                gd = tot * g_ref[...]
                dot = jnp.mean(gd * xv, axis=-1, keepdims=True)
                o_ref[...] = r_ref[...] + scale * gd - xv * (scale * scale * scale * dot)
                part = jnp.sum(tot * xv * scale, axis=0, keepdims=True)

                @pl.when(row_tile == 0)
                def _():
                    n_ref[...] = part

                @pl.when(row_tile > 0)
                def _():
                    n_ref[...] += part

                return
            if r_ref is not None:
                tot = tot + r_ref[...]
            o_ref[...] = tot.astype(o_ref.dtype)
            if n_ref is not None:
                scale = lax.rsqrt(jnp.mean(tot * tot, axis=-1, keepdims=True) + NORM_EPS)
                n_ref[...] = (tot * scale * g_ref[...]).astype(BF16)

        part = lax.dot_general(a_ref[...].astype(BF16), b_ref[...].astype(BF16), dims, preferred_element_type=F32)
        if nk == 1:
            finish(part)
            return
        acc_ref = refs[-1]
        k = pl.program_id(2)

        @pl.when(k == 0)
        def _():
            acc_ref[...] = part

        @pl.when(k > 0)
        def _():
            acc_ref[...] += part

        @pl.when(k == nk - 1)
        def _():
            finish(acc_ref[...])

    in_specs = [a_spec, b_spec] + ([res_spec] if res is not None else [])
    args = (a, b) + ((res,) if res is not None else ())
    out_specs, out_shapes = o_spec, out_shape
    row = pl.BlockSpec((1, acc_shape[1]), lambda i, j, q: (0, 0))
    if norm_gain is not None:
        in_specs.append(row)
        args = args + (norm_gain,)
        out_specs, out_shapes = [o_spec, o_spec], [out_shape, S_(out_shape.shape, BF16)]
    if norm_bwd is not None:
        in_specs += [res_spec, row]
        args = args + tuple(norm_bwd)
        out_specs, out_shapes = [o_spec, row], [out_shape, S_((1, acc_shape[1]), F32)]
    scratch = [] if nk == 1 else [pltpu.VMEM(acc_shape, F32)]
    return pl.pallas_call(body, grid=grid, in_specs=in_specs, out_specs=out_specs, out_shape=out_shapes,
                          scratch_shapes=scratch, compiler_params=_cp(), name=name)(*args)


NN = (((1,), (0,)), ((), ()))
NT = (((1,), (1,)), ((), ()))
TN = (((0,), (0,)), ((), ()))


def _mm_nn(a, b, *, tm, tn, tk, out_dtype, res=None, norm_gain=None, name):
    m, k = a.shape
    n = b.shape[1]
    assert norm_gain is None or tn == n
    return _matmul(a, b, dims=NN, grid=(m // tm, n // tn, k // tk),
                   a_spec=pl.BlockSpec((tm, tk), lambda i, j, q: (i, q)),
                   b_spec=pl.BlockSpec((tk, tn), lambda i, j, q: (q, j)),
                   o_spec=pl.BlockSpec((tm, tn), lambda i, j, q: (i, j)),
                   out_shape=S_((m, n), out_dtype), acc_shape=(tm, tn), res=res,
                   res_spec=pl.BlockSpec((tm, tn), lambda i, j, q: (i, j)), norm_gain=norm_gain, name=name)


def _mm_nt(a, b, *, tm, tn, tk, out_dtype, res=None, norm_bwd=None, name):
    m, k = a.shape
    n = b.shape[0]
    assert norm_bwd is None or tn == n
    return _matmul(a, b, dims=NT, grid=(m // tm, n // tn, k // tk),
                   a_spec=pl.BlockSpec((tm, tk), lambda i, j, q: (i, q)),
                   b_spec=pl.BlockSpec((tn, tk), lambda i, j, q: (j, q)),
                   o_spec=pl.BlockSpec((tm, tn), lambda i, j, q: (i, j)),
                   out_shape=S_((m, n), out_dtype), acc_shape=(tm, tn), res=res,
                   res_spec=pl.BlockSpec((tm, tn), lambda i, j, q: (i, j)), norm_bwd=norm_bwd, name=name)


def _mm_tn(a, b, *, tm, tn, tk, out_dtype, name):
    k, m = a.shape
    n = b.shape[1]
    return _matmul(a, b, dims=TN, grid=(m // tm, n // tn, k // tk),
                   a_spec=pl.BlockSpec((tk, tm), lambda i, j, q: (q, i)),
                   b_spec=pl.BlockSpec((tk, tn), lambda i, j, q: (q, j)),
                   o_spec=pl.BlockSpec((tm, tn), lambda i, j, q: (i, j)),
                   out_shape=S_((m, n), out_dtype), acc_shape=(tm, tn), name=name)


HALF_TILE = FFN_DIM // 2


def _mm_up(hn, w_up):
    return _matmul(hn, w_up, dims=NN, grid=(T // 1024, 4, 1),
                   a_spec=pl.BlockSpec((1024, D_MODEL), lambda i, j, q: (i, 0)),
                   b_spec=pl.BlockSpec((None, D_MODEL, HALF_TILE), lambda i, j, q: (j, 0, 0)),
                   o_spec=pl.BlockSpec((None, 1024, HALF_TILE), lambda i, j, q: (j // 2, i, j % 2)),
                   out_shape=S_((2, T, FFN_DIM), F32), acc_shape=(1024, HALF_TILE), name="mm_up")


def _mm_dhn(dup3, w_up, x2, gain, dres):
    row = pl.BlockSpec((1024, D_MODEL), lambda i, j, q: (i, 0))
    return _matmul(dup3, w_up, dims=NT, grid=(T // 1024, 1, 4),
                   a_spec=pl.BlockSpec((None, 1024, HALF_TILE), lambda i, j, q: (q // 2, i, q % 2)),
                   b_spec=pl.BlockSpec((None, D_MODEL, HALF_TILE), lambda i, j, q: (q, 0, 0)),
                   o_spec=row, out_shape=S_((T, D_MODEL), F32), acc_shape=(1024, D_MODEL), res=dres, res_spec=row,
                   norm_bwd=(x2, gain), name="mm_dhn")


def _mm_dwup(hn, dup3):
    return _matmul(hn, dup3, dims=TN, grid=(1, 4, T // 1024),
                   a_spec=pl.BlockSpec((1024, D_MODEL), lambda i, j, q: (q, 0)),
                   b_spec=pl.BlockSpec((None, 1024, HALF_TILE), lambda i, j, q: (j // 2, q, j % 2)),
                   o_spec=pl.BlockSpec((None, D_MODEL, HALF_TILE), lambda i, j, q: (j, 0, 0)),
                   out_shape=S_((N_CHIPS, D_MODEL, HALF_TILE), BF16), acc_shape=(D_MODEL, HALF_TILE), name="mm_dwup")


ROWS = 512


def _rmsnorm_fwd(x, gain, name):
    def body(x_ref, g_ref, o_ref):
        xv = x_ref[...]
        r = lax.rsqrt(jnp.mean(xv * xv, axis=-1, keepdims=True) + NORM_EPS)
        o_ref[...] = (xv * r * g_ref[...]).astype(BF16)

    return pl.pallas_call(body, grid=(T // ROWS,),
                          in_specs=[pl.BlockSpec((ROWS, D_MODEL), lambda i: (i, 0)), pl.BlockSpec((1, D_MODEL), lambda i: (0, 0))],
                          out_specs=pl.BlockSpec((ROWS, D_MODEL), lambda i: (i, 0)),
                          out_shape=S_((T, D_MODEL), BF16), name=name)(x, gain)


def _loss_head(y, target):
    def body(y_ref, t_ref, dy_ref, p_ref):
        e = y_ref[...] - t_ref[...]
        dy_ref[...] = e * (1.0 / D_MODEL)
        p_ref[...] = jnp.full((8, 128), 0.5 / D_MODEL, F32) * jnp.sum(e * e)

    row = pl.BlockSpec((ROWS, D_MODEL), lambda i: (i, 0))
    return pl.pallas_call(body, grid=(T // ROWS,), in_specs=[row, row],
                          out_specs=[row, pl.BlockSpec((8, 128), lambda i: (i, 0))],
                          out_shape=[S_((T, D_MODEL), F32), S_((T // ROWS * 8, 128), F32)], name="loss_head")(y, target)


def _adamw_update(w_ref, g_ref, m_ref, v_ref, d_ref, nm_ref, nv_ref):
    gv = g_ref[...]
    nm = ADAM_B1 * m_ref[...] + (1.0 - ADAM_B1) * gv
    nv = ADAM_B2 * v_ref[...] + (1.0 - ADAM_B2) * (gv * gv)
    m_hat = nm / (1.0 - ADAM_B1 ** ADAM_STEP)
    v_hat = nv / (1.0 - ADAM_B2 ** ADAM_STEP)
    d_ref[...] = -ADAM_LR * (m_hat / (jnp.sqrt(v_hat) + ADAM_EPS) + ADAM_WD * w_ref[...])
    nm_ref[...] = nm
    nv_ref[...] = nv


def _adamw(w, g_mine, g_other, m, v, core, rows, name):
    per_half = w.shape[1] // 2 // rows

    def body(core_ref, w_ref, gm_ref, go_ref, m_ref, v_ref, g_ref, d_ref, nm_ref, nv_ref):
        mine = (pl.program_id(1) // per_half) == core_ref[0]
        g_ref[...] = jnp.where(mine, gm_ref[...], go_ref[...])
        _adamw_update(w_ref, g_ref, m_ref, v_ref, d_ref, nm_ref, nv_ref)

    blk = pl.BlockSpec((None, rows, w.shape[2]), lambda l, i, core_ref: (l, i, 0))
    half = pl.BlockSpec((None, rows, w.shape[2]), lambda l, i, core_ref: (l, i % per_half, 0))
    out = S_(w.shape, F32)
    spec = pltpu.PrefetchScalarGridSpec(num_scalar_prefetch=1, grid=(w.shape[0], w.shape[1] // rows),
                                        in_specs=[blk, half, half, blk, blk], out_specs=[blk] * 4)
    return pl.pallas_call(body, grid_spec=spec, out_shape=[out] * 4, name=name)(core.reshape(1).astype(jnp.int32), w, g_mine, g_other, m, v)


def _adamw_many(ws, gs, ms, vs):
    n = len(ws)

    def body(*refs):
        for i in range(n):
            _adamw_update(*[refs[k * n + i] for k in range(7)])

    out = [S_(w.shape, F32) for w in ws]
    res = pl.pallas_call(body, out_shape=out * 3, name="adamw_small")(*ws, *gs, *ms, *vs)
    return res[:n], res[n:2 * n], res[2 * n:]


FFN_CT = 256


def _gate_fn(up_g, up_v, wg0, wg1, wg2, bg, wv0, wv1, wv2, bv):
    gate = _dwconv(up_g, [wg0, wg1, wg2], bg)
    val = _dwconv(up_v, [wv0, wv1, wv2], bv)
    return jax.nn.silu(gate) * val


def _taps(ref, part, n):
    return [ref[part, k:k + 1, :] for k in range(n)]


def _convgate_fwd(up3, cw, cb, hook=None):
    def body(up_ref, cw_ref, cb_ref, o_ref):
        o_ref[...] = _gate_fn(up_ref[0], up_ref[1], *_taps(cw_ref, 0, 3), cb_ref[0], *_taps(cw_ref, 1, 3), cb_ref[1]).astype(BF16)

    (act,), got = _hooked_call(
        body, grid=(FFN_DIM // FFN_CT, B_LOC),
        in_specs=[pl.BlockSpec((2, SEQ, FFN_CT), lambda j, b: (0, b, j)),
                  pl.BlockSpec((2, 8, FFN_CT), lambda j, b: (0, 0, j)),
                  pl.BlockSpec((2, 1, FFN_CT), lambda j, b: (0, 0, j))],
        out_specs=[pl.BlockSpec((SEQ, FFN_CT), lambda j, b: (b, j))],
        out_shape=[S_((T, FFN_DIM), BF16)], scratch_shapes=[], args=(up3, cw, cb), hook=hook, name="convgate_fwd")
    return act, got


def _convgate_bwd(up3, cw, cb, dact, hook=None):
    def body(up_ref, cw_ref, cb_ref, da_ref, dup_ref, dcw_ref):
        args = (up_ref[0], up_ref[1], *_taps(cw_ref, 0, 3), cb_ref[0], *_taps(cw_ref, 1, 3), cb_ref[1])
        _, vjp = jax.vjp(_gate_fn, *args)
        dg, dv, g0, g1, g2, gb, v0, v1, v2, vb = vjp(da_ref[...])
        dup_ref[0] = dg.astype(BF16)
        dup_ref[1] = dv.astype(BF16)
        zero = jnp.zeros((4, FFN_CT), F32)
        new = jnp.stack([jnp.concatenate([g0, g1, g2, gb, zero], axis=0), jnp.concatenate([v0, v1, v2, vb, zero], axis=0)])

        @pl.when(pl.program_id(1) == 0)
        def _():
            dcw_ref[...] = new

        @pl.when(pl.program_id(1) > 0)
        def _():
            dcw_ref[...] += new

    return _hooked_call(
        body, grid=(FFN_DIM // FFN_CT, B_LOC),
        in_specs=[pl.BlockSpec((2, SEQ, FFN_CT), lambda j, b: (0, b, j)),
                  pl.BlockSpec((2, 8, FFN_CT), lambda j, b: (0, 0, j)),
                  pl.BlockSpec((2, 1, FFN_CT), lambda j, b: (0, 0, j)),
                  pl.BlockSpec((SEQ, FFN_CT), lambda j, b: (b, j))],
        out_specs=[pl.BlockSpec((2, SEQ, FFN_CT), lambda j, b: (0, b, j)),
                   pl.BlockSpec((2, 8, FFN_CT), lambda j, b: (0, 0, j))],
        out_shape=[S_((2, T, FFN_DIM), BF16), S_((2, 8, FFN_DIM), F32)],
        scratch_shapes=[], args=(up3, cw, cb, dact), hook=hook, name="convgate_bwd")


SSD_CT = 256


def _conv5_fn(x, w0, w1, w2, w3, w4, b):
    return jax.nn.silu(_dwconv(x, [w0, w1, w2, w3, w4], b))


def _ssd_pre_fwd(proj, cw, cb):
    def body(x_ref, cw_ref, cb_ref, o_ref):
        o_ref[...] = _conv5_fn(x_ref[...], *[cw_ref[k:k + 1, :] for k in range(5)], cb_ref[...])

    return pl.pallas_call(
        body, grid=(SSD_XBC // SSD_CT, B_LOC),
        in_specs=[pl.BlockSpec((SEQ, SSD_CT), lambda j, b: (b, j)),
                  pl.BlockSpec((8, SSD_CT), lambda j, b: (0, j)),
                  pl.BlockSpec((1, SSD_CT), lambda j, b: (0, j))],
        out_specs=pl.BlockSpec((SEQ, SSD_CT), lambda j, b: (b, j)),
        out_shape=S_((T, SSD_XBC), F32), compiler_params=_cp(), name="ssd_pre_fwd")(proj, cw, cb)


def _ssd_pre_bwd(proj, cw, cb, dxc, hook=None):
    def body(x_ref, cw_ref, cb_ref, d_ref, dx_ref, dcw_ref):
        _, vjp = jax.vjp(_conv5_fn, x_ref[...], *[cw_ref[k:k + 1, :] for k in range(5)], cb_ref[...])
        dx, g0, g1, g2, g3, g4, gb = vjp(d_ref[...])
        dx_ref[...] = dx.astype(BF16)
        new = jnp.concatenate([g0, g1, g2, g3, g4, gb, jnp.zeros((2, SSD_CT), F32)], axis=0)

        @pl.when(pl.program_id(1) == 0)
        def _():
            dcw_ref[...] = new

        @pl.when(pl.program_id(1) > 0)
        def _():
            dcw_ref[...] += new

    return _hooked_call(
        body, grid=(SSD_XBC // SSD_CT, B_LOC),
        in_specs=[pl.BlockSpec((SEQ, SSD_CT), lambda j, b: (b, j)),
                  pl.BlockSpec((8, SSD_CT), lambda j, b: (0, j)),
                  pl.BlockSpec((1, SSD_CT), lambda j, b: (0, j)),
                  pl.BlockSpec((SEQ, SSD_CT), lambda j, b: (b, j))],
        out_specs=[pl.BlockSpec((SEQ, SSD_CT), lambda j, b: (b, j)),
                   pl.BlockSpec((8, SSD_CT), lambda j, b: (0, j))],
        out_shape=[S_((T, SSD_XBC), BF16), S_((8, SSD_XBC), F32)],
        scratch_shapes=[], args=(proj, cw, cb, dxc), hook=hook, name="ssd_pre_bwd")


GROUP_W = 256
ONE_BUFFER = dict(pipeline_mode=pl.Buffered(1))
HEADS_PER_GROUP = 4
FWD_SCAN_UNROLL = 4
BWD_SCAN_UNROLL = 2


def _ssd_dt_fn(dt_raw, bias, alog):
    dt = _softplus(dt_raw + bias)
    return dt, dt * (-jnp.exp(alog))


def _ssd_chunk_fn(direction, group, xc0, xc1, bc, cc, dt, da, prev0, prev1):
    q = CHUNK
    ti = lax.broadcasted_iota(jnp.int32, (q, q), 0)
    si = lax.broadcasted_iota(jnp.int32, (q, q), 1)
    keep = (ti >= si) if direction == 0 else (ti <= si)
    mat = keep.astype(F32)
    acs = jnp.dot(mat, da, precision=HI, preferred_element_type=F32)
    acs_t = lax.dot_general(da, mat, (((0,), (1,)), ((), ())), precision=HI, preferred_element_type=F32)
    tot = jnp.sum(da, axis=0, keepdims=True)
    lane = lax.broadcasted_iota(jnp.int32, (1, 128), 1)
    sub = lax.broadcasted_iota(jnp.int32, (128, 1), 0)
    first_head = lane < HEAD
    cb = _dot_nt(cc, bc)
    a_cols, tots, dt_cols, lows, douts = [], [], [], [], []
    for h in range(HEADS_PER_GROUP):
        ln = 8 * direction + 4 * group + h
        oh_l = (lane == ln).astype(F32)
        oh_s = (sub == ln).astype(F32)
        a_col = jnp.sum(acs * oh_l, axis=1, keepdims=True)
        a_row = jnp.sum(acs_t * oh_s, axis=0, keepdims=True)
        tot_h = jnp.sum(tot * oh_l, axis=1, keepdims=True)
        a_cols.append(a_col)
        tots.append(tot_h)
        dt_cols.append(jnp.sum(dt * oh_l, axis=1, keepdims=True))
        lows.append(cb * jnp.exp(jnp.where(keep, a_col - a_row, NEG_INF)))
        douts.append(bc * jnp.exp(tot_h - a_col))
    out = []
    for pair, (xc, prev) in enumerate(((xc0, prev0), (xc1, prev1))):
        h0, h1 = 2 * pair, 2 * pair + 1
        xdt = xc * jnp.where(first_head, dt_cols[h0], dt_cols[h1])
        y = jnp.where(first_head, jnp.exp(a_cols[h0]), jnp.exp(a_cols[h1])) * _dot(cc, prev)
        y = y + jnp.where(first_head, _dot(lows[h0], xdt), _dot(lows[h1], xdt))
        st = jnp.where(first_head, _dot_tn(douts[h0], xdt), _dot_tn(douts[h1], xdt))
        out.append((y, prev * jnp.where(first_head, jnp.exp(tots[h0]), jnp.exp(tots[h1])) + st))
    return out[0][0], out[1][0], out[0][1], out[1][1]


def _ssd_post_fn(y, xc, z, d_exp, gain):
    y = (y + d_exp * xc) * jax.nn.silu(z)
    return y * lax.rsqrt(jnp.mean(y * y, axis=-1, keepdims=True) + NORM_EPS) * gain


def _chunk_rows(c):
    return pl.ds(pl.multiple_of(c * CHUNK, CHUNK), CHUNK)


def _scan_loop(step, init, unroll):
    def body(i, carry):
        for k in range(unroll):
            carry = step(i * unroll + k, carry)
        return carry

    return lax.fori_loop(0, N_CHUNK // unroll, body, init)


def _ssd_scan_specs(**mode):
    return [pl.BlockSpec((SEQ, GROUP_W), lambda g, b: (b, g), **mode),
            pl.BlockSpec((SEQ, 128), lambda g, b: (b, C_B // 128 + g), **mode),
            pl.BlockSpec((SEQ, 128), lambda g, b: (b, C_C // 128 + g), **mode),
            pl.BlockSpec((SEQ, GROUP_W), lambda g, b: (b, C_Z // GROUP_W + g), **mode),
            pl.BlockSpec((SEQ, 128), lambda g, b: (b, C_DT // 128), **mode),
            pl.BlockSpec((1, 128), lambda g, b: (0, 0)),
            pl.BlockSpec((1, 128), lambda g, b: (0, 0)),
            pl.BlockSpec((1, GROUP_W), lambda g, b: (0, g)),
            pl.BlockSpec((1, GROUP_W), lambda g, b: (0, g))]


def _ssd_state_spec(**mode):
    return pl.BlockSpec((None, None, 2 * N_CHUNK, 128, GROUP_W), lambda g, b: (g, b, 0, 0, 0), **mode)


def _ssd_scan_fwd(xc, proj, dtb, alog, d_exp, gain, hook=None):
    def body(x_ref, b_ref, c_ref, z_ref, dt_ref, dtb_ref, al_ref, de_ref, g_ref, o_ref, y_s, st_ref, dt_s, da_s):
        group = pl.program_id(0)
        dt, da = _ssd_dt_fn(dt_ref[...], dtb_ref[...], al_ref[...])
        dt_s[...] = dt
        da_s[...] = da
        for direction in (0, 1):
            def step(i, prev, direction=direction):
                c = i if direction == 0 else N_CHUNK - 1 - i
                rows = _chunk_rows(c)
                st_ref[direction * N_CHUNK + c, :, 0:128] = prev[0]
                st_ref[direction * N_CHUNK + c, :, 128:256] = prev[1]
                y0, y1, nxt0, nxt1 = _ssd_chunk_fn(direction, group, x_ref[rows, 0:128], x_ref[rows, 128:256], b_ref[rows, :], c_ref[rows, :],
                                                   dt_s[rows, :], da_s[rows, :], prev[0], prev[1])
                if direction == 0:
                    y_s[rows, 0:128] = y0
                    y_s[rows, 128:256] = y1
                else:
                    y_s[rows, 0:128] += y0
                    y_s[rows, 128:256] += y1
                return nxt0, nxt1

            _scan_loop(step, (jnp.zeros((128, 128), F32), jnp.zeros((128, 128), F32)), FWD_SCAN_UNROLL)

        def post(c, carry):
            rows = _chunk_rows(c)
            o_ref[rows, :] = _ssd_post_fn(y_s[rows, :], x_ref[rows, :], z_ref[rows, :], de_ref[...], g_ref[...]).astype(BF16)
            return carry

        lax.fori_loop(0, N_CHUNK, post, 0)

    return _hooked_call(
        body, grid=(2, B_LOC), in_specs=_ssd_scan_specs(),
        out_specs=[pl.BlockSpec((SEQ, GROUP_W), lambda g, b: (b, g)), pl.BlockSpec((SEQ, GROUP_W), lambda g, b: (b, g)), _ssd_state_spec()],
        out_shape=[S_((T, SSD_INNER), BF16), S_((T, SSD_INNER), F32), S_((2, B_LOC, 2 * N_CHUNK, 128, GROUP_W), F32)],
        scratch_shapes=[pltpu.VMEM((SEQ, 128), F32), pltpu.VMEM((SEQ, 128), F32)],
        args=(xc, xc, xc, proj, proj, dtb, alog, d_exp, gain), hook=hook, name="ssd_scan_fwd")


def _ssd_scan_bwd(xc, proj, dtb, alog, d_exp, gain, dy, ysum, states, hook=None):
    def body(x_ref, b_ref, c_ref, z_ref, dt_ref, dtb_ref, al_ref, de_ref, g_ref, dy_ref, ys_ref, st_s,
             dx_ref, db_ref, dc_ref, dz_ref, ddt_ref, ddtb_ref, dal_ref, dde_ref, dg_ref,
             dt_s, da_s, y_s, ddt_s, dda_s):
        group = pl.program_id(0)
        first = pl.program_id(1) == 0
        (dt, da), dt_vjp = jax.vjp(_ssd_dt_fn, dt_ref[...], dtb_ref[...], al_ref[...])
        dt_s[...] = dt
        da_s[...] = da

        def post(c, carry):
            rows = _chunk_rows(c)
            _, post_vjp = jax.vjp(_ssd_post_fn, ys_ref[rows, :], x_ref[rows, :], z_ref[rows, :], de_ref[...], g_ref[...])
            d_y, d_x_skip, d_z, g_de, g_g = post_vjp(dy_ref[rows, :])
            dz_ref[rows, :] = d_z.astype(BF16)
            dx_ref[rows, :] = d_x_skip
            y_s[rows, :] = d_y
            return carry[0] + g_de, carry[1] + g_g

        d_de, d_g = lax.fori_loop(0, N_CHUNK, post, (jnp.zeros((1, GROUP_W), F32), jnp.zeros((1, GROUP_W), F32)))
        db_ref[...] = jnp.zeros((SEQ, 128), F32)
        dc_ref[...] = jnp.zeros((SEQ, 128), F32)
        ddt_s[...] = jnp.zeros((SEQ, 128), F32)
        dda_s[...] = jnp.zeros((SEQ, 128), F32)
        for direction in (0, 1):
            def bstep(i, dnxt, direction=direction):
                c = N_CHUNK - 1 - i if direction == 0 else i
                rows = _chunk_rows(c)
                fn = functools.partial(_ssd_chunk_fn, direction, group)
                _, vjp = jax.vjp(fn, x_ref[rows, 0:128], x_ref[rows, 128:256], b_ref[rows, :], c_ref[rows, :], dt_s[rows, :], da_s[rows, :],
                                 st_s[direction * N_CHUNK + c, :, 0:128], st_s[direction * N_CHUNK + c, :, 128:256])
                g_x0, g_x1, g_b, g_c, g_dt, g_da, g_prev0, g_prev1 = vjp((y_s[rows, 0:128], y_s[rows, 128:256], dnxt[0], dnxt[1]))
                dx_ref[rows, 0:128] += g_x0
                dx_ref[rows, 128:256] += g_x1
                db_ref[rows, :] += g_b
                dc_ref[rows, :] += g_c
                ddt_s[rows, :] += g_dt
                dda_s[rows, :] += g_da
                return g_prev0, g_prev1

            _scan_loop(bstep, (jnp.zeros((128, 128), F32), jnp.zeros((128, 128), F32)), BWD_SCAN_UNROLL)
        g_raw, g_bias, g_alog = dt_vjp((ddt_s[...], dda_s[...]))
        ddt_ref[...] = g_raw
        pad7 = jnp.zeros((7, 128), F32)
        new_b = jnp.concatenate([g_bias, pad7], axis=0)
        new_a = jnp.concatenate([g_alog, pad7], axis=0)

        @pl.when(first)
        def _():
            ddtb_ref[...] = new_b
            dal_ref[...] = new_a
            dde_ref[...] = d_de
            dg_ref[...] = d_g

        @pl.when(jnp.logical_not(first))
        def _():
            ddtb_ref[...] += new_b
            dal_ref[...] += new_a
            dde_ref[...] += d_de
            dg_ref[...] += d_g

    return _hooked_call(
        body, grid=(2, B_LOC),
        in_specs=_ssd_scan_specs(**ONE_BUFFER) + [pl.BlockSpec((SEQ, GROUP_W), lambda g, b: (b, g), **ONE_BUFFER),
                                                  pl.BlockSpec((SEQ, GROUP_W), lambda g, b: (b, g), **ONE_BUFFER),
                                                  _ssd_state_spec(**ONE_BUFFER)],
        out_specs=[pl.BlockSpec((SEQ, GROUP_W), lambda g, b: (b, g)),
                   pl.BlockSpec((SEQ, 128), lambda g, b: (b, g)),
                   pl.BlockSpec((SEQ, 128), lambda g, b: (b, g)),
                   pl.BlockSpec((SEQ, GROUP_W), lambda g, b: (b, g)),
                   pl.BlockSpec((None, SEQ, 128), lambda g, b: (g, b, 0)),
                   pl.BlockSpec((None, 8, 128), lambda g, b: (g, 0, 0)),
                   pl.BlockSpec((None, 8, 128), lambda g, b: (g, 0, 0)),
                   pl.BlockSpec((1, GROUP_W), lambda g, b: (0, g)),
                   pl.BlockSpec((1, GROUP_W), lambda g, b: (0, g))],
        out_shape=[S_((T, SSD_INNER), F32), S_((T, 256), F32), S_((T, 256), F32), S_((T, SSD_INNER), BF16),
                   S_((2, T, 128), F32), S_((2, 8, 128), F32), S_((2, 8, 128), F32),
                   S_((1, SSD_INNER), F32), S_((1, SSD_INNER), F32)],
        scratch_shapes=[pltpu.VMEM((SEQ, 128), F32), pltpu.VMEM((SEQ, 128), F32), pltpu.VMEM((SEQ, GROUP_W), F32),
                        pltpu.VMEM((SEQ, 128), F32), pltpu.VMEM((SEQ, 128), F32)],
        args=(xc, xc, xc, proj, proj, dtb, alog, d_exp, gain, dy, ysum, states), hook=hook, name="ssd_scan_bwd")


GMLP_W = 256


def _gmlp_chunk_fn(gu, gv, v_gain, w0, w1, w2, w3, b_exp):
    u = jax.nn.gelu(gu)
    v = jax.nn.gelu(gv)
    v = v * lax.rsqrt(jnp.mean(v * v, axis=-1, keepdims=True) + NORM_EPS) * v_gain
    col = lax.broadcasted_iota(jnp.int32, (1, GMLP_W), 1) // HEAD
    mixed = b_exp
    for g, w in enumerate((w0, w1, w2, w3)):
        mixed = mixed + (col == g).astype(F32) * _dot(w, v)
    return u * mixed


def _gmlp_specs():
    return [pl.BlockSpec((SEQ, GMLP_W), lambda b: (b, C_GU // GMLP_W)),
            pl.BlockSpec((SEQ, GMLP_W), lambda b: (b, C_GV // GMLP_W)),
            pl.BlockSpec((1, GMLP_W), lambda b: (0, 0)),
            pl.BlockSpec((4, CHUNK, CHUNK), lambda b: (0, 0, 0)),
            pl.BlockSpec((CHUNK, GMLP_W), lambda b: (0, 0))]


def _gmlp_fwd(proj, v_gain, w_s, b_exp):
    def body(u_ref, v_ref, g_ref, w_ref, b_ref, o_ref):
        def step(c, carry):
            rows = _chunk_rows(c)
            o_ref[rows, :] = _gmlp_chunk_fn(u_ref[rows, :], v_ref[rows, :], g_ref[...], w_ref[0], w_ref[1], w_ref[2], w_ref[3],
                                            b_ref[...]).astype(BF16)
            return carry

        lax.fori_loop(0, N_CHUNK, step, 0)

    return pl.pallas_call(body, grid=(B_LOC,), in_specs=_gmlp_specs(),
                          out_specs=pl.BlockSpec((SEQ, GMLP_W), lambda b: (b, 0)),
                          out_shape=S_((T, GMLP_W), BF16), name="gmlp_fwd")(proj, proj, v_gain, w_s, b_exp)


def _gmlp_bwd(proj, v_gain, w_s, b_exp, dy):
    def body(u_ref, v_ref, g_ref, w_ref, b_ref, dy_ref, du_ref, dv_ref, dg_ref, dw_ref, db_ref):
        @pl.when(pl.program_id(0) == 0)
        def _():
            dg_ref[...] = jnp.zeros_like(dg_ref)
            dw_ref[...] = jnp.zeros_like(dw_ref)
            db_ref[...] = jnp.zeros_like(db_ref)

        def step(c, carry):
            rows = _chunk_rows(c)
            _, vjp = jax.vjp(_gmlp_chunk_fn, u_ref[rows, :], v_ref[rows, :], g_ref[...], w_ref[0], w_ref[1], w_ref[2], w_ref[3], b_ref[...])
            g_u, g_v, g_g, g_w0, g_w1, g_w2, g_w3, g_b = vjp(dy_ref[rows, :])
            du_ref[rows, :] = g_u.astype(BF16)
            dv_ref[rows, :] = g_v.astype(BF16)
            dg_ref[...] += g_g
            db_ref[...] += g_b
            for g, gw in enumerate((g_w0, g_w1, g_w2, g_w3)):
                dw_ref[g] += gw
            return carry

        lax.fori_loop(0, N_CHUNK, step, 0)

    blk = pl.BlockSpec((SEQ, GMLP_W), lambda b: (b, 0))
    return pl.pallas_call(
        body, grid=(B_LOC,),
        in_specs=_gmlp_specs() + [pl.BlockSpec((SEQ, GMLP_W), lambda b: (b, SSD_INNER // GMLP_W))],
        out_specs=[blk, blk, pl.BlockSpec((1, GMLP_W), lambda b: (0, 0)),
                   pl.BlockSpec((4, CHUNK, CHUNK), lambda b: (0, 0, 0)), pl.BlockSpec((CHUNK, GMLP_W), lambda b: (0, 0))],
        out_shape=[S_((T, GMLP_W), BF16), S_((T, GMLP_W), BF16), S_((1, GMLP_W), F32),
                   S_((4, CHUNK, CHUNK), F32), S_((CHUNK, GMLP_W), F32)],
        name="gmlp_bwd")(proj, proj, v_gain, w_s, b_exp, dy)


PAIR_W = 128
QB = 128
KW = QB + 2 * ATTN_HALF
N_QB = SEQ // QB
FWD_BLOCK_UNROLL = 16
BWD_BLOCK_UNROLL = 16
PAD_ROWS = SEQ + 2 * ATTN_HALF


def _qk_norm_fn(x, gain):
    ms = jnp.dot(x * x, _head_sum_matrix(PAIR_W), precision=SUM_PRECISION, preferred_element_type=F32) * (1.0 / HEAD)
    return x * lax.rsqrt(ms + NORM_EPS) * gain


def _deinterleave(dst_ref, src_ref, dil, offset):
    length = SEQ // dil
    if dil == 1:
        dst_ref[pl.ds(offset, SEQ), :] = src_ref[...]
        return
    for r in range(dil):
        dst_ref[pl.ds(offset + r * length, length), :] = src_ref[pl.ds(r, length, stride=dil), :]


def _interleave(dst_ref, src_ref, dil, offset):
    length = SEQ // dil
    if dil == 1:
        dst_ref[...] = src_ref[pl.ds(offset, SEQ), :]
        return
    for r in range(dil):
        dst_ref[pl.ds(r, length, stride=dil), :] = src_ref[pl.ds(offset + r * length, length), :]


def _edge_mask(blk, dil):
    length = SEQ // dil
    qi = blk * QB + lax.broadcasted_iota(jnp.int32, (QB, KW), 0)
    kj = blk * QB - ATTN_HALF + lax.broadcasted_iota(jnp.int32, (QB, KW), 1)
    return (kj >= 0) & (kj < SEQ) & ((qi // length) == (kj // length))


def _lane_is_head(hh):
    return (lax.broadcasted_iota(jnp.int32, (1, PAIR_W), 1) // HEAD) == hh


def _dilate_qkv(dil, qn_s, kn_s, v_ref, qd_s, kd_s, vd_s):
    _deinterleave(qd_s, qn_s, dil, 0)
    _deinterleave(kd_s, kn_s, dil, ATTN_HALF)
    _deinterleave(vd_s, v_ref, dil, ATTN_HALF)


def _attn_branch_fwd(br, dil, qn_s, kn_s, v_ref, bias_ref, qd_s, kd_s, vd_s, od_s, ld_s):
    _dilate_qkv(dil, qn_s, kn_s, v_ref, qd_s, kd_s, vd_s)

    def step(blk, carry):
        rows = pl.ds(pl.multiple_of(blk * QB, QB), QB)
        win = pl.ds(pl.multiple_of(blk * QB, QB), KW)
        qb, kw, vw = qd_s[rows, :], kd_s[win, :], vd_s[win, :]
        edge = _edge_mask(blk, dil)
        out, lse = 0.0, 0.0
        for hh in range(2):
            is_h = _lane_is_head(hh)
            s = _dot_nt(jnp.where(is_h, qb, 0.0), kw) * (HEAD ** -0.5) + bias_ref[br, hh]
            s = jnp.where(edge, s, NEG_INF)
            m = jnp.max(s, axis=-1, keepdims=True)
            l_h = m + jnp.log(jnp.sum(jnp.exp(s - m), axis=-1, keepdims=True))
            out = out + jnp.where(is_h, _dot(jnp.exp(s - l_h), vw), 0.0)
            lse = lse + jnp.where(is_h, l_h, 0.0)
        od_s[rows, :] = out
        ld_s[rows, :] = lse
        return carry

    _block_loop(step, FWD_BLOCK_UNROLL)


def _block_loop(step, unroll):
    def body(i, carry):
        for k in range(unroll):
            carry = step(i * unroll + k, carry)
        return carry

    lax.fori_loop(0, N_QB // unroll, body, 0)


def _attn_specs():
    col = lambda c0: (lambda p, b: (b, c0 // PAIR_W + p))
    return [pl.BlockSpec((SEQ, PAIR_W), col(C_Q)), pl.BlockSpec((SEQ, PAIR_W), col(C_K)), pl.BlockSpec((SEQ, PAIR_W), col(C_V)),
            pl.BlockSpec((1, PAIR_W), lambda p, b: (0, 0)), pl.BlockSpec((1, PAIR_W), lambda p, b: (0, 0)),
            pl.BlockSpec((3, 2, QB, KW), lambda p, b: (0, p, 0, 0))]


def _attn_scratch():
    seq = pltpu.VMEM((SEQ, PAIR_W), F32)
    pad = pltpu.VMEM((PAD_ROWS, PAIR_W), F32)
    return [seq, seq, seq, pad, pad, seq, seq]


def _zero_pads(*refs):
    for ref in refs:
        ref[pl.ds(0, ATTN_HALF), :] = jnp.zeros((ATTN_HALF, PAIR_W), F32)
        ref[pl.ds(ATTN_HALF + SEQ, ATTN_HALF), :] = jnp.zeros((ATTN_HALF, PAIR_W), F32)


ROW_STEP = 256


def _row_steps(fn, init=0):
    return lax.fori_loop(0, SEQ // ROW_STEP, lambda i, c: fn(pl.ds(pl.multiple_of(i * ROW_STEP, ROW_STEP), ROW_STEP), c), init)


def _interleave_add(acc_ref, src_ref, dil, offset):
    length = SEQ // dil
    if dil == 1:
        acc_ref[...] += src_ref[pl.ds(offset, SEQ), :]
        return
    for r in range(dil):
        acc_ref[pl.ds(r, length, stride=dil), :] += src_ref[pl.ds(offset + r * length, length), :]


def _attn_norm_qk(q_ref, k_ref, qg_ref, kg_ref, qn_s, kn_s):
    def norm(rows, carry):
        qn_s[rows, :] = _qk_norm_fn(q_ref[rows, :], qg_ref[...])
        kn_s[rows, :] = _qk_norm_fn(k_ref[rows, :], kg_ref[...])
        return carry

    _row_steps(norm)


def _attn_forward_all(q_ref, k_ref, v_ref, qg_ref, kg_ref, bias_ref, qn_s, kn_s, qd_s, kd_s, vd_s, od_s, ld_s, on_s, ln_s):
    _attn_norm_qk(q_ref, k_ref, qg_ref, kg_ref, qn_s, kn_s)
    _zero_pads(kd_s, vd_s)
    for br, dil in enumerate(ATTN_DILS):
        _attn_branch_fwd(br, dil, qn_s, kn_s, v_ref, bias_ref, qd_s, kd_s, vd_s, od_s, ld_s)
        _interleave(on_s.at[br], od_s, dil, 0)
        _interleave(ln_s.at[br], ld_s, dil, 0)


def _merge_weights(ln_s, rows):
    l0, l1, l2 = ln_s[0, rows, :], ln_s[1, rows, :], ln_s[2, rows, :]
    m = jnp.maximum(jnp.maximum(l0, l1), l2)
    e = [jnp.exp(l0 - m), jnp.exp(l1 - m), jnp.exp(l2 - m)]
    den = e[0] + e[1] + e[2]
    return [e[0] / den, e[1] / den, e[2] / den]


def _attn_fwd(proj, q_gain, k_gain, bias, hook=None):
    def body(q_ref, k_ref, v_ref, qg_ref, kg_ref, bias_ref, o_ref, on_s, ln_s, qn_s, kn_s, qd_s, kd_s, vd_s, od_s, ld_s):
        _attn_forward_all(q_ref, k_ref, v_ref, qg_ref, kg_ref, bias_ref, qn_s, kn_s, qd_s, kd_s, vd_s, od_s, ld_s, on_s, ln_s)

        def merge(rows, carry):
            w = _merge_weights(ln_s, rows)
            o_ref[rows, :] = (w[0] * on_s[0, rows, :] + w[1] * on_s[1, rows, :] + w[2] * on_s[2, rows, :]).astype(BF16)
            return carry

        _row_steps(merge)

    kept = pl.BlockSpec((3, SEQ, PAIR_W), lambda p, b: (0, b, p))
    return _hooked_call(body, grid=(2, B_LOC), in_specs=_attn_specs(),
                        out_specs=[pl.BlockSpec((SEQ, PAIR_W), lambda p, b: (b, p)), kept, kept],
                        out_shape=[S_((T, 2 * PAIR_W), BF16), S_((3, T, 2 * PAIR_W), F32), S_((3, T, 2 * PAIR_W), F32)],
                        scratch_shapes=_attn_scratch(), args=(proj, proj, proj, q_gain, k_gain, bias), hook=hook, name="attn_fwd")


def _attn_bwd(proj, q_gain, k_gain, bias, dy, kept_o, kept_l, hook=None):
    def body(q_ref, k_ref, v_ref, qg_ref, kg_ref, bias_ref, dy_ref, on_ref, ln_ref,
             dq_ref, dk_ref, dv_ref, dqg_ref, dkg_ref, dbias_ref,
             qn_s, kn_s, qd_s, kd_s, vd_s, od_s, ld_s, don_s, dln_s, dod_s, dld_s, dqd_s, dkd_s, dvd_s, dqn_s, dkn_s, dvn_s):
        first = pl.program_id(1) == 0
        _attn_norm_qk(q_ref, k_ref, qg_ref, kg_ref, qn_s, kn_s)
        _zero_pads(kd_s, vd_s)

        def clear_acc(rows, carry):
            dqn_s[rows, :] = jnp.zeros((ROW_STEP, PAIR_W), F32)
            dkn_s[rows, :] = jnp.zeros((ROW_STEP, PAIR_W), F32)
            dvn_s[rows, :] = jnp.zeros((ROW_STEP, PAIR_W), F32)
            return carry

        _row_steps(clear_acc)

        @pl.when(first)
        def _():
            dbias_ref[...] = jnp.zeros_like(dbias_ref)

        def merge_bwd(rows, carry):
            w = _merge_weights(ln_ref, rows)
            dy = dy_ref[rows, :]
            same_head = _head_sum_matrix(PAIR_W)
            dws = [jnp.dot(dy * on_ref[j, rows, :], same_head, precision=SUM_PRECISION, preferred_element_type=F32) for j in range(3)]
            dbar = w[0] * dws[0] + w[1] * dws[1] + w[2] * dws[2]
            for j in range(3):
                don_s[j, rows, :] = w[j] * dy
                dln_s[j, rows, :] = w[j] * (dws[j] - dbar)
            return carry

        _row_steps(merge_bwd)
        for br, dil in enumerate(ATTN_DILS):
            _dilate_qkv(dil, qn_s, kn_s, v_ref, qd_s, kd_s, vd_s)
            _deinterleave(od_s, on_ref.at[br], dil, 0)
            _deinterleave(ld_s, ln_ref.at[br], dil, 0)
            _deinterleave(dod_s, don_s.at[br], dil, 0)
            _deinterleave(dld_s, dln_s.at[br], dil, 0)

            def clear(rows, carry):
                dkd_s[rows, :] = jnp.zeros((ROW_STEP, PAIR_W), F32)
                dvd_s[rows, :] = jnp.zeros((ROW_STEP, PAIR_W), F32)
                return carry

            _row_steps(clear)
            tail = pl.ds(SEQ, 2 * ATTN_HALF)
            dkd_s[tail, :] = jnp.zeros((2 * ATTN_HALF, PAIR_W), F32)
            dvd_s[tail, :] = jnp.zeros((2 * ATTN_HALF, PAIR_W), F32)

            def step(blk, carry, br=br, dil=dil):
                rows = pl.ds(pl.multiple_of(blk * QB, QB), QB)
                win = pl.ds(pl.multiple_of(blk * QB, QB), KW)
                qb, kw, vw = qd_s[rows, :], kd_s[win, :], vd_s[win, :]
                do_b, dl_b, o_b, l_b = dod_s[rows, :], dld_s[rows, :], od_s[rows, :], ld_s[rows, :]
                edge = _edge_mask(blk, dil)
                dq, dk, dv = 0.0, 0.0, 0.0
                for hh in range(2):
                    is_h = _lane_is_head(hh)
                    pick = (lax.broadcasted_iota(jnp.int32, (1, PAIR_W), 1) == hh * HEAD).astype(F32)
                    q_h = jnp.where(is_h, qb, 0.0)
                    do_h = jnp.where(is_h, do_b, 0.0)
                    s = _dot_nt(q_h, kw) * (HEAD ** -0.5) + bias_ref[br, hh]
                    s = jnp.where(edge, s, NEG_INF)
                    p = jnp.exp(s - jnp.sum(l_b * pick, axis=-1, keepdims=True))
                    dp = _dot_nt(do_h, vw)
                    delta = jnp.sum(do_h * o_b, axis=-1, keepdims=True)
                    ds = p * (dp - delta + jnp.sum(dl_b * pick, axis=-1, keepdims=True))
                    dbias_ref[br, hh] += ds
                    dq = dq + jnp.where(is_h, _dot(ds, kw), 0.0) * (HEAD ** -0.5)
                    dk = dk + _dot_tn(ds, q_h) * (HEAD ** -0.5)
                    dv = dv + _dot_tn(p, do_h)
                dqd_s[rows, :] = dq
                dkd_s[win, :] += dk
                dvd_s[win, :] += dv
                return carry

            _block_loop(step, BWD_BLOCK_UNROLL)
            _interleave_add(dqn_s, dqd_s, dil, 0)
            _interleave_add(dkn_s, dkd_s, dil, ATTN_HALF)
            _interleave_add(dvn_s, dvd_s, dil, ATTN_HALF)

        def norm_bwd(rows, carry):
            _, q_vjp = jax.vjp(_qk_norm_fn, q_ref[rows, :], qg_ref[...])
            _, k_vjp = jax.vjp(_qk_norm_fn, k_ref[rows, :], kg_ref[...])
            g_q, g_qg = q_vjp(dqn_s[rows, :])
            g_k, g_kg = k_vjp(dkn_s[rows, :])
            dq_ref[rows, :] = g_q.astype(BF16)
            dk_ref[rows, :] = g_k.astype(BF16)
            dv_ref[rows, :] = dvn_s[rows, :].astype(BF16)
            return carry[0] + g_qg, carry[1] + g_kg

        g_qg, g_kg = _row_steps(norm_bwd, (jnp.zeros((1, PAIR_W), F32), jnp.zeros((1, PAIR_W), F32)))
        pad7 = jnp.zeros((7, PAIR_W), F32)
        new_q = jnp.concatenate([g_qg, pad7], axis=0)
        new_k = jnp.concatenate([g_kg, pad7], axis=0)

        @pl.when(first)
        def _():
            dqg_ref[...] = new_q
            dkg_ref[...] = new_k

        @pl.when(jnp.logical_not(first))
        def _():
            dqg_ref[...] += new_q
            dkg_ref[...] += new_k

    seq = pltpu.VMEM((SEQ, PAIR_W), F32)
    seq3 = pltpu.VMEM((3, SEQ, PAIR_W), F32)
    pad = pltpu.VMEM((PAD_ROWS, PAIR_W), F32)
    kept = pl.BlockSpec((3, SEQ, PAIR_W), lambda p, b: (0, b, p))
    out_blk = pl.BlockSpec((SEQ, PAIR_W), lambda p, b: (b, p))
    gain_blk = pl.BlockSpec((None, 8, PAIR_W), lambda p, b: (p, 0, 0))
    return _hooked_call(
        body, grid=(2, B_LOC),
        in_specs=_attn_specs() + [pl.BlockSpec((SEQ, PAIR_W), lambda p, b: (b, (SSD_INNER + GMLP_W) // PAIR_W + p)), kept, kept],
        out_specs=[out_blk, out_blk, out_blk, gain_blk, gain_blk, pl.BlockSpec((3, 2, QB, KW), lambda p, b: (0, p, 0, 0))],
        out_shape=[S_((T, 2 * PAIR_W), BF16)] * 3 + [S_((2, 8, PAIR_W), F32)] * 2 + [S_((3, 4, QB, KW), F32)],
        scratch_shapes=_attn_scratch() + [seq3, seq3, seq, seq, seq, pad, pad, seq, seq, seq],
        args=(proj, proj, proj, q_gain, k_gain, bias, dy, kept_o, kept_l), hook=hook, name="attn_bwd")


def _rel_bucket(rel):
    nb = 16
    max_exact = nb // 2
    n = jnp.abs(rel)
    large = max_exact + (jnp.log(jnp.maximum(n, 1).astype(F32) / max_exact) / math.log(1024 / max_exact) * (nb - max_exact)).astype(jnp.int32)
    large = jnp.minimum(large, nb - 1)
    return jnp.where(rel > 0, nb, 0) + jnp.where(n < max_exact, n, large)


def _attn_bias(rel_table):
    rel = jnp.arange(KW)[None, :] - ATTN_HALF - jnp.arange(QB)[:, None]
    inside = (jnp.abs(rel) <= ATTN_HALF)
    out = []
    for dil in ATTN_DILS:
        one_hot = (_rel_bucket(rel * dil)[None] == jnp.arange(32)[:, None, None]).astype(F32)
        b = jnp.einsum("kh,kts->hts", rel_table, one_hot, precision=HI)
        out.append(jnp.where(inside[None], b, NEG_INF))
    return jnp.stack(out).astype(F32)


def _place():
    return lax.axis_index("x"), lax.axis_index("y"), lax.axis_index("c")


def _allgather8(buf, name):
    rows = buf.shape[0]
    flips = [(fx, fy, fc) for fx in (0, 1) for fy in (0, 1) for fc in (0, 1)][1:]

    def body(in_ref, out_ref, send_sems, recv_sems):
        x, y, c = _place()
        me = 4 * x + 2 * y + c
        peers = [(1 - x if fx else x, 1 - y if fy else y, 1 - c if fc else c) for fx, fy, fc in flips]

        def copy(k, slot, peer):
            return pltpu.make_async_remote_copy(src_ref=in_ref, dst_ref=out_ref.at[slot], send_sem=send_sems.at[k],
                                                recv_sem=recv_sems.at[k], device_id=peer, device_id_type=MESH)

        sends = [copy(k, me, peer) for k, peer in enumerate(peers)]
        for cp in sends:
            cp.start()
        for k, (px, py, pc) in enumerate(peers):
            copy(k, 4 * px + 2 * py + pc, (px, py, pc)).wait_recv()
        for cp in sends:
            cp.wait_send()

    slots = pl.pallas_call(body, in_specs=[ANY], out_specs=ANY, out_shape=S_((N_DEV, rows, 128), F32),
                           scratch_shapes=[pltpu.SemaphoreType.DMA((7,)), pltpu.SemaphoreType.DMA((7,))], name=name)(buf)
    x, y, c = _place()
    return lax.dynamic_update_index_in_dim(slots, buf, 4 * x + 2 * y + c, axis=0)


N_BIG = 4


def _other_chips(x, y):
    return [(1 - x, y), (x, 1 - y), (1 - x, 1 - y)]


def _hooked_call(body, *, grid, in_specs, out_specs, out_shape, scratch_shapes, args, hook, name):
    if hook is None:
        res = pl.pallas_call(body, grid=grid, in_specs=in_specs, out_specs=out_specs, out_shape=out_shape,
                             scratch_shapes=scratch_shapes, compiler_params=_cp(), name=name)(*args)
        return res, None
    counts = (len(in_specs), len(hook["arrays"]), len(out_specs), len(hook["out_shape"]), len(scratch_shapes), len(hook["sems"]))

    def wrapped(*refs):
        groups, pos = [], 0
        for n in counts:
            groups.append(refs[pos:pos + n])
            pos += n
        ins, h_ins, outs, h_outs, scr, sems = groups
        idx = [pl.program_id(a) for a in range(len(grid))]
        first = functools.reduce(jnp.logical_and, [i == 0 for i in idx])
        last = functools.reduce(jnp.logical_and, [i == g - 1 for i, g in zip(idx, grid)])

        @pl.when(first)
        def _():
            hook["start"](h_ins, h_outs, sems)

        body(*ins, *outs, *scr)

        @pl.when(last)
        def _():
            hook["finish"](h_ins, h_outs, sems)

    res = pl.pallas_call(wrapped, grid=grid, in_specs=list(in_specs) + [ANY] * counts[1], out_specs=list(out_specs) + [ANY] * counts[3],
                         out_shape=list(out_shape) + list(hook["out_shape"]), scratch_shapes=list(scratch_shapes) + list(hook["sems"]),
                         compiler_params=_cp(), name=name + "_" + hook["name"])(*args, *hook["arrays"])
    return res[:counts[2]], res[counts[2]:]


def _run_hook(hook):
    n_in, n_out = len(hook["arrays"]), len(hook["out_shape"])

    def body(*refs):
        h_ins, h_outs, sems = refs[:n_in], refs[n_in:n_in + n_out], refs[n_in + n_out:]
        hook["start"](h_ins, h_outs, sems)
        hook["finish"](h_ins, h_outs, sems)

    return pl.pallas_call(body, in_specs=[ANY] * n_in, out_specs=[ANY] * n_out, out_shape=list(hook["out_shape"]),
                          scratch_shapes=list(hook["sems"]), name=hook["name"])(*hook["arrays"])


def _remote(src, dst, send_sem, recv_sem, peer):
    return pltpu.make_async_remote_copy(src_ref=src, dst_ref=dst, send_sem=send_sem, recv_sem=recv_sem, device_id=peer, device_id_type=MESH)


def _gather_hook(shards):
    def copies(h_ins, h_outs, sems, kind):
        ici_send, ici_recv, d2d_send, d2d_recv = sems
        x, y, c = _place()
        chip = 2 * x + y
        out = []
        for t in range(len(shards)):
            half = shards[t].shape[0] // 2
            mine_r, other_r = pl.ds(c * half, half), pl.ds((1 - c) * half, half)
            for f, (px, py) in enumerate(_other_chips(x, y)):
                k, peer_chip = 3 * t + f, 2 * px + py
                if kind in ("send", "land"):
                    slot = chip if kind == "send" else peer_chip
                    out.append(_remote(h_ins[t].at[mine_r], h_outs[t].at[slot, mine_r], ici_send.at[k], ici_recv.at[k], (px, py, c)))
                else:
                    rows = mine_r if kind == "pass" else other_r
                    out.append(_remote(h_outs[t].at[peer_chip, rows], h_outs[t].at[peer_chip, rows], d2d_send.at[k], d2d_recv.at[k],
                                       (x, y, 1 - c)))
        return out

    def start(h_ins, h_outs, sems):
        for cp in copies(h_ins, h_outs, sems, "send"):
            cp.start()

    def finish(h_ins, h_outs, sems):
        passed = copies(h_ins, h_outs, sems, "pass")
        for landed, forward in zip(copies(h_ins, h_outs, sems, "land"), passed):
            landed.wait_recv()
            forward.start()
        for cp in copies(h_ins, h_outs, sems, "get"):
            cp.wait_recv()
        for cp in copies(h_ins, h_outs, sems, "send") + passed:
            cp.wait_send()

    return dict(name="gather", arrays=list(shards), out_shape=[S_((N_CHIPS,) + s.shape, s.dtype) for s in shards],
                sems=[pltpu.SemaphoreType.DMA((3 * len(shards),)) for _ in range(4)], start=start, finish=finish)


def _to_sibling_hook(parts, half_rows=False):
    def copies(h_ins, h_outs, sems):
        x, y, c = _place()
        out = []
        for t, p in enumerate(parts):
            src = h_ins[t].at[:, pl.ds((1 - c) * (p.shape[1] // 2), p.shape[1] // 2)] if half_rows else h_ins[t]
            out.append(_remote(src, h_outs[t], sems[0].at[t], sems[1].at[t], (x, y, 1 - c)))
        return out

    def start(h_ins, h_outs, sems):
        for cp in copies(h_ins, h_outs, sems):
            cp.start()

    def finish(h_ins, h_outs, sems):
        cps = copies(h_ins, h_outs, sems)
        for cp in cps:
            cp.wait_recv()
        for cp in cps:
            cp.wait_send()

    shapes = [(p.shape[0], p.shape[1] // 2, p.shape[2]) if half_rows else p.shape for p in parts]
    return dict(name="to_sibling", arrays=list(parts), out_shape=[S_(s, p.dtype) for s, p in zip(shapes, parts)],
                sems=[pltpu.SemaphoreType.DMA((len(parts),)), pltpu.SemaphoreType.DMA((len(parts),))], start=start, finish=finish)


def _to_chips_hook(parts):
    def copies(h_ins, h_outs, sems):
        x, y, c = _place()
        return [_remote(h_ins[t].at[2 * px + py], h_outs[t].at[f], sems[0].at[3 * t + f], sems[1].at[3 * t + f], (px, py, c))
                for t in range(len(parts)) for f, (px, py) in enumerate(_other_chips(x, y))]

    def start(h_ins, h_outs, sems):
        for cp in copies(h_ins, h_outs, sems):
            cp.start()

    def finish(h_ins, h_outs, sems):
        cps = copies(h_ins, h_outs, sems)
        for cp in cps:
            cp.wait_recv()
        for cp in cps:
            cp.wait_send()

    return dict(name="to_chips", arrays=list(parts), out_shape=[S_((3,) + p.shape[1:], p.dtype) for p in parts],
                sems=[pltpu.SemaphoreType.DMA((3 * len(parts),)), pltpu.SemaphoreType.DMA((3 * len(parts),))], start=start, finish=finish)


def _add_pair(a, b, core, name):
    n, half, c = b.shape

    def body(core_ref, a_ref, b_ref, o_ref):
        o_ref[...] = (a_ref[...].astype(F32) + b_ref[...].astype(F32)).astype(BF16)

    spec = pltpu.PrefetchScalarGridSpec(
        num_scalar_prefetch=1, grid=(n,),
        in_specs=[pl.BlockSpec((None, half, c), lambda i, core_ref: (i, core_ref[0], 0)),
                  pl.BlockSpec((None, half, c), lambda i, core_ref: (i, 0, 0))],
        out_specs=pl.BlockSpec((None, half, c), lambda i, core_ref: (i, 0, 0)))
    return pl.pallas_call(body, grid_spec=spec, out_shape=S_(b.shape, BF16), name=name)(core.reshape(1).astype(jnp.int32), a, b)


def _add_four(own, got, rows, name):
    n, r, c = own.shape

    def body(a_ref, g_ref, o_ref):
        o_ref[...] = ((a_ref[...].astype(F32) + g_ref[0].astype(F32)) + g_ref[1].astype(F32)) + g_ref[2].astype(F32)

    blk = pl.BlockSpec((None, rows, c), lambda i, j: (i, j, 0))
    return pl.pallas_call(body, grid=(n, r // rows), in_specs=[blk, pl.BlockSpec((3, None, rows, c), lambda i, j: (0, i, j, 0))],
                          out_specs=blk, out_shape=S_(own.shape, F32), name=name)(own, got)


def _sum_slots(slots):
    n, rows = slots.shape[:2]

    def body(s_ref, o_ref):
        tot = s_ref[0]
        for k in range(1, n):
            tot = tot + s_ref[k]
        o_ref[...] = tot

    return pl.pallas_call(body, out_shape=S_((rows, 128), F32), name="sum_slots")(slots)


def _add_two(a, b):
    def body(a_ref, b_ref, o_ref):
        o_ref[...] = a_ref[...] + b_ref[...]

    return pl.pallas_call(body, out_shape=S_(a.shape, a.dtype), name="add_two")(a, b)


def _allreduce_small(buf, chip):
    (theirs,) = _run_hook(dict(_to_sibling_hook([buf]), name="small_to_sibling"))
    pair = _add_two(buf, theirs)
    (slots,) = _run_hook(dict(_gather_hook([pair]), name="small_gather"))
    return _sum_slots(lax.dynamic_update_index_in_dim(slots, pair, chip, axis=0))


SMALL = ("mix_norm_gain", "ssd_conv_w", "ssd_conv_b", "ssd_dt_bias", "ssd_a_log", "ssd_d", "ssd_out_gain", "gmlp_v_gain",
         "gmlp_w_s", "gmlp_b_s", "attn_q_gain", "attn_k_gain", "rel_bias_table", "ffn_norm_gain", "ffn_conv_w", "ffn_conv_b")
BIG = ("w_in", "w_out", "ffn_w_up", "ffn_w_down")
WEIGHTS = ("mix_norm_gain", "w_in", "ssd_conv_w", "ssd_conv_b", "ssd_dt_bias", "ssd_a_log", "ssd_d", "ssd_out_gain", "gmlp_v_gain",
           "gmlp_w_s", "gmlp_b_s", "attn_q_gain", "attn_k_gain", "rel_bias_table", "w_out", "ffn_norm_gain", "ffn_w_up",
           "ffn_conv_w", "ffn_conv_b", "ffn_w_down")
ADAM_ROWS = {"w_in": 512, "w_out": 128, "ffn_w_up": 256, "ffn_w_down": 352}


PACK_ROWS = 64


def _packed_rows(shape):
    return -(-int(np.prod(shape)) // 1024) * 8


def _pack(arrays):
    parts = []
    for a in arrays:
        rows = _packed_rows(a.shape)
        flat = a.reshape(-1).astype(F32)
        parts.append(jnp.pad(flat, (0, rows * 128 - flat.shape[0])).reshape(rows, 128))
    total = sum(p.shape[0] for p in parts)
    tail = -total % PACK_ROWS
    if tail:
        parts.append(jnp.zeros((tail, 128), F32))
    return jnp.concatenate(parts, axis=0)


def _unpack(buf, shapes):
    out, row = [], 0
    for s in shapes:
        rows, n = _packed_rows(s), int(np.prod(s))
        out.append(buf[row:row + rows].reshape(-1)[:n].reshape(s))
        row += rows
    return out


def _perm_cols(w):
    pad = jnp.zeros(w.shape[:-1] + (NP - IN_WIDTH,), w.dtype)
    return jnp.concatenate([w[..., :1536], w[..., 1552:], w[..., 1536:1552], pad], axis=-1)


def _unperm_cols(w):
    return jnp.concatenate([w[..., :1536], w[..., C_DT:C_DT + 16], w[..., 1536:C_DT]], axis=-1)


def _layer_params(l, p, conv5_w, conv3_w, bias):
    def make(mix_g, conv5, conv5_b, dt_bias, a_log, d_skip, out_gain, v_gain, w_s, b_s, q_gain, k_gain, ffn_g, conv3, conv3_b):
        lanes = lambda a: jnp.pad(a.reshape(1, 16), ((0, 0), (0, 112)))
        cw3 = jnp.pad(jnp.transpose(conv3.reshape(3, 2, FFN_DIM), (1, 0, 2)), ((0, 0), (0, 5), (0, 0)))
        return dict(mix_g=mix_g.reshape(1, D_MODEL), cw5=jnp.pad(conv5, ((0, 3), (0, 0))), cb5=conv5_b.reshape(1, SSD_XBC),
                    dtb=lanes(dt_bias), alog=lanes(a_log), d_exp=jnp.repeat(d_skip, HEAD).reshape(1, SSD_INNER),
                    out_gain=out_gain.reshape(1, SSD_INNER), v_gain=v_gain.reshape(1, GMLP_W), w_s=w_s,
                    b_exp=jnp.repeat(b_s.T, HEAD, axis=1), q_gain=jnp.tile(q_gain, 2).reshape(1, PAIR_W),
                    k_gain=jnp.tile(k_gain, 2).reshape(1, PAIR_W), ffn_g=ffn_g.reshape(1, D_MODEL), cw3=cw3,
                    cb3=conv3_b.reshape(2, 1, FFN_DIM))

    args = (p["mix_norm_gain"][l], conv5_w[l], p["ssd_conv_b"][l], p["ssd_dt_bias"][l], p["ssd_a_log"][l], p["ssd_d"][l],
            p["ssd_out_gain"][l], p["gmlp_v_gain"][l], p["gmlp_w_s"][l], p["gmlp_b_s"][l], p["attn_q_gain"][l], p["attn_k_gain"][l],
            p["ffn_norm_gain"][l], conv3_w[l], p["ffn_conv_b"][l])
    return jax.vjp(make, *args)


def _forward_layer(x, h, lp, w, bias, next_gain=None, hooks=None, resolve=None):
    hooks = hooks or {}
    proj = _mm_nn(h, w["w_in"], tm=1024, tn=1024, tk=1024, out_dtype=F32, name="mm_proj")
    xc = _ssd_pre_fwd(proj, lp["cw5"], lp["cb5"])
    (y_ssd, ssd_sum, ssd_states), got_self = _ssd_scan_fwd(xc, proj, lp["dtb"], lp["alog"], lp["d_exp"], lp["out_gain"],
                                                           hook=hooks.get("ssd"))
    if resolve is not None and got_self is not None:
        w = dict(w, **resolve(got_self))
    y_gmlp = _gmlp_fwd(proj, lp["v_gain"], lp["w_s"], lp["b_exp"])
    (y_attn, attn_o, attn_l), got_attn = _attn_fwd(proj, lp["q_gain"], lp["k_gain"], bias, hook=hooks.get("attn"))
    y = jnp.concatenate([y_ssd, y_gmlp, y_attn], axis=1)
    x2, hn = _mm_nn(y, w["w_out"], tm=1024, tn=1024, tk=1024, out_dtype=F32, res=x, norm_gain=lp["ffn_g"], name="mm_out")
    up3 = _mm_up(hn, w["ffn_w_up"])
    act, got_gate = _convgate_fwd(up3, lp["cw3"], lp["cb3"], hook=hooks.get("gate"))
    if next_gain is None:
        x3, h_next = _mm_nn(act, w["ffn_w_down"], tm=1024, tn=1024, tk=HALF_TILE, out_dtype=F32, res=x2, name="mm_down"), None
    else:
        x3, h_next = _mm_nn(act, w["ffn_w_down"], tm=1024, tn=1024, tk=HALF_TILE, out_dtype=F32, res=x2, norm_gain=next_gain,
                            name="mm_down_norm")
    saved = dict(x=x, h=h, proj=proj, xc=xc, y=y, x2=x2, hn=hn, up3=up3, act=act, attn_o=attn_o, attn_l=attn_l,
                 ssd_sum=ssd_sum, ssd_states=ssd_states)
    return x3, h_next, saved, w, dict(ssd=got_self, attn=got_attn, gate=got_gate)


def _backward_layer(dx3, sv, lp, w, bias, pending=None, reducer=None):
    d_act = _mm_nt(dx3, w["ffn_w_down"], tm=1024, tn=HALF_TILE, tk=1024, out_dtype=F32, name="mm_dact")
    dw_down = _mm_tn(sv["act"], dx3, tm=HALF_TILE, tn=1024, tk=1024, out_dtype=BF16, name="mm_dwdown")
    (dup3, dcw3), from_sibling = _convgate_bwd(sv["up3"], lp["cw3"], lp["cb3"], d_act, hook=pending.sibling_hook() if pending else None)
    if pending:
        pending.add_sibling(from_sibling)
    dx2, d_ffn_g = _mm_dhn(dup3, w["ffn_w_up"], sv["x2"], lp["ffn_g"], dx3)
    dw_up = _mm_dwup(sv["hn"], dup3)
    d_y = _mm_nt(dx2, w["w_out"], tm=1024, tn=1024, tk=1024, out_dtype=F32, name="mm_dy")
    dw_out = _mm_tn(sv["y"], dx2, tm=1024, tn=1024, tk=1024, out_dtype=BF16, name="mm_dwout")
    early = reducer(("w_out", "ffn_w_up", "ffn_w_down"), (dw_out, dw_up, dw_down)) if reducer else None
    proj, xc = sv["proj"], sv["xc"]
    (dxs, dbc, dcc, dz, ddt2, ddtb2, dal2, d_dexp, d_outg), from_chips = _ssd_scan_bwd(
        xc, proj, lp["dtb"], lp["alog"], lp["d_exp"], lp["out_gain"], d_y, sv["ssd_sum"], sv["ssd_states"],
        hook=pending.chips_hook() if pending else None)
    if pending:
        pending.add_chips(from_chips)
    (d_xbc, dcw5), from_sibling = _ssd_pre_bwd(proj, lp["cw5"], lp["cb5"], jnp.concatenate([dxs, dbc, dcc], axis=1),
                                               hook=early.sibling_hook() if early else None)
    if early:
        early.add_sibling(from_sibling)
    d_gu, d_gv, d_vg, d_ws, d_bexp = _gmlp_bwd(proj, lp["v_gain"], lp["w_s"], lp["b_exp"], d_y)
    (d_q, d_k, d_v, d_qg2, d_kg2, d_bias), from_chips = _attn_bwd(proj, lp["q_gain"], lp["k_gain"], bias, d_y, sv["attn_o"], sv["attn_l"],
                                                                  hook=early.chips_hook() if early else None)
    if early:
        early.add_chips(from_chips)
    d_dt = (ddt2[0] + ddt2[1]).astype(BF16)
    d_proj = jnp.concatenate([d_xbc, dz, d_gu, d_gv, d_q, d_k, d_v, d_dt, jnp.zeros((T, NP - C_DT - 128), BF16)], axis=1)
    dx, d_mix_g = _mm_nt(d_proj, w["w_in"], tm=1024, tn=1024, tk=1024, out_dtype=F32, res=dx2, norm_bwd=(sv["x"], lp["mix_g"]), name="mm_dh")
    dw_in = _mm_tn(sv["h"], d_proj, tm=1024, tn=1024, tk=1024, out_dtype=BF16, name="mm_dwin")
    late = None
    if reducer:
        late = reducer(("w_in",), (dw_in,))
        late.run_alone()
    d_lp = dict(mix_g=d_mix_g, cw5=dcw5[:8] * (jnp.arange(8) < 5)[:, None].astype(F32), cb5=dcw5[5:6],
                dtb=(ddtb2[0, :1] + ddtb2[1, :1]), alog=(dal2[0, :1] + dal2[1, :1]), d_exp=d_dexp, out_gain=d_outg,
                v_gain=d_vg, w_s=d_ws, b_exp=d_bexp, q_gain=d_qg2[0, :1] + d_qg2[1, :1], k_gain=d_kg2[0, :1] + d_kg2[1, :1],
                ffn_g=d_ffn_g, cw3=dcw3 * (jnp.arange(8) < 3)[None, :, None].astype(F32), cb3=dcw3[:, 3:4])
    return dx, dict(w_in=dw_in, w_out=dw_out, ffn_w_up=dw_up, ffn_w_down=dw_down), d_lp, d_bias, (early, late)


def _to_shard_major(name, dw):
    if name == "ffn_w_up":
        return dw
    if name == "w_in":
        r, c = dw.shape[0], IN_WIDTH
        return jnp.transpose(_unperm_cols(dw).reshape(r, N_CHIPS, c // N_CHIPS), (1, 0, 2))
    r, c = dw.shape
    return dw.reshape(N_CHIPS, r // N_CHIPS, c)


def _whole_weight(name, gathered, own, chip):
    if name == "w_in":
        return _perm_cols(jnp.concatenate([jnp.where(chip == k, own, gathered[k]) for k in range(N_CHIPS)], axis=1))
    w = lax.dynamic_update_index_in_dim(gathered, own, chip, axis=0)
    return w if name == "ffn_w_up" else w.reshape(N_CHIPS * own.shape[0], own.shape[1])


class _LayerReduce:
    def __init__(self, names, dws, chip, core):
        self.names, self.chip, self.core = names, chip, core
        self.parts = [_to_shard_major(n, dw) for n, dw in zip(names, dws)]

    def sibling_hook(self):
        return _to_sibling_hook(self.parts, half_rows=True)

    def add_sibling(self, got):
        self.sums = [_add_pair(a, b, self.core, "add_pair_" + n) for n, a, b in zip(self.names, self.parts, got)]

    def chips_hook(self):
        return _to_chips_hook(self.sums)

    def add_chips(self, got):
        self.half = {}
        for n, s2, g3 in zip(self.names, self.sums, got):
            own = lax.dynamic_index_in_dim(s2, self.chip, axis=0, keepdims=True)
            self.half[n] = _add_four(own, g3[:, None], own.shape[1], "add_four_" + n)[0]

    def run_alone(self):
        self.add_sibling(_run_hook(self.sibling_hook()))
        self.add_chips(_run_hook(self.chips_hook()))


LAYER_SMALL = ("mix_norm_gain", "ssd_conv_w", "ssd_conv_b", "ssd_dt_bias", "ssd_a_log", "ssd_d", "ssd_out_gain", "gmlp_v_gain",
               "gmlp_w_s", "gmlp_b_s", "attn_q_gain", "attn_k_gain", "ffn_norm_gain", "ffn_conv_w", "ffn_conv_b")


def _local_grads(x, loss_target, p, conv5_w, conv3_w, layer_w, exchange=None):
    bias, bias_vjp = jax.vjp(_attn_bias, p["rel_bias_table"])
    xt = x.reshape(T, D_MODEL)
    layer_w = list(layer_w)
    saved, lps, lp_vjps = [], [], []
    if exchange is not None:
        chip, core, own = exchange
        whole = lambda names, layer, gathered: {n: _whole_weight(n, g, own[layer][BIG.index(n)], chip) for n, g in zip(names, gathered)}
    for l in range(DEPTH):
        lp, lp_vjp = _layer_params(l, p, conv5_w, conv3_w, bias)
        lps.append(lp)
        lp_vjps.append(lp_vjp)
    h = _rmsnorm_fwd(xt, lps[0]["mix_g"], "rmsnorm_fwd")
    for l in range(DEPTH):
        lp = lps[l]
        plan = {}
        if exchange is not None and l == 0:
            plan = {"ssd": (BIG[1:], 0), "attn": (BIG[2:], 1), "gate": (BIG[:2], 1)}
        elif exchange is not None and l + 1 < DEPTH:
            plan = {"ssd": (BIG[:2], l + 1), "attn": (BIG[2:3], l + 1), "gate": (BIG[3:], l + 1)}
        hooks = {tag: _gather_hook([own[layer][BIG.index(n)] for n in names]) for tag, (names, layer) in plan.items()}
        xt, h, sv, layer_w[l], got = _forward_layer(xt, h, lp, layer_w[l], bias, lps[l + 1]["mix_g"] if l + 1 < DEPTH else None, hooks,
                                                    resolve=(lambda g: whole(BIG[1:], 0, g)) if l == 0 else None)
        ahead = {}
        for tag, (names, layer) in plan.items():
            if layer == l + 1:
                ahead.update(whole(names, layer, got[tag]))
        if ahead:
            layer_w.append(ahead)
        saved.append(sv)
    dxt, loss_parts = _loss_head(xt, loss_target.reshape(T, D_MODEL))
    loss_local = jnp.sum(loss_parts[::8, 0])

    big_grads = [None] * DEPTH
    small_layers = [None] * DEPTH
    d_bias_tot = jnp.zeros_like(bias)
    pending = None
    for l in reversed(range(DEPTH)):
        last = exchange is not None and l == 0
        dxt, big_grads[l], d_lp, d_bias, own_reduce = _backward_layer(
            dxt, saved[l], lps[l], layer_w[l], bias, pending=pending,
            reducer=(lambda names, dws: _LayerReduce(names, dws, chip, core)) if last else None)
        if pending is not None:
            big_grads[l + 1] = pending.half
        if last:
            big_grads[l] = dict(own_reduce[0].half, **own_reduce[1].half)
        elif exchange is not None:
            pending = _LayerReduce(BIG, [big_grads[l][n] for n in BIG], chip, core)
        small_layers[l] = lp_vjps[l](d_lp)
        d_bias_tot = d_bias_tot + d_bias
    (d_rel_table,) = bias_vjp(d_bias_tot)
    local_small = {n: jnp.stack([small_layers[l][i] for l in range(DEPTH)]) for i, n in enumerate(LAYER_SMALL)}
    local_small["rel_bias_table"] = d_rel_table
    return dxt, loss_local, big_grads, local_small


def kernel(x, mix_norm_gain, w_in, ssd_conv_w, ssd_conv_b, ssd_dt_bias, ssd_a_log, ssd_d, ssd_out_gain, gmlp_v_gain, gmlp_w_s, gmlp_b_s, attn_q_gain, attn_k_gain, rel_bias_table, w_out, ffn_norm_gain, ffn_w_up, ffn_conv_w, ffn_conv_b, ffn_w_down, loss_target, m_mix_norm_gain, m_w_in, m_ssd_conv_w, m_ssd_conv_b, m_ssd_dt_bias, m_ssd_a_log, m_ssd_d, m_ssd_out_gain, m_gmlp_v_gain, m_gmlp_w_s, m_gmlp_b_s, m_attn_q_gain, m_attn_k_gain, m_rel_bias_table, m_w_out, m_ffn_norm_gain, m_ffn_w_up, m_ffn_conv_w, m_ffn_conv_b, m_ffn_w_down, v_mix_norm_gain, v_w_in, v_ssd_conv_w, v_ssd_conv_b, v_ssd_dt_bias, v_ssd_a_log, v_ssd_d, v_ssd_out_gain, v_gmlp_v_gain, v_gmlp_w_s, v_gmlp_b_s, v_attn_q_gain, v_attn_k_gain, v_rel_bias_table, v_w_out, v_ffn_norm_gain, v_ffn_w_up, v_ffn_conv_w, v_ffn_conv_b, v_ffn_w_down):
    env = dict(locals())
    p = {n: env[n] for n in WEIGHTS}
    chip = 2 * lax.axis_index("x") + lax.axis_index("y")
    core = lax.axis_index("c")

    conv_slots = _allgather8(_pack([ssd_conv_w, ffn_conv_w]), "allgather_conv")
    conv_parts = [_unpack(conv_slots[2 * k], [ssd_conv_w.shape, ffn_conv_w.shape]) for k in range(N_CHIPS)]
    conv5_w = jnp.concatenate([cp[0] for cp in conv_parts], axis=-1)
    conv3_w = jnp.concatenate([cp[1] for cp in conv_parts], axis=-1)
    own = [[p[n][l].astype(BF16) for n in BIG] for l in range(DEPTH)]
    (first,) = _run_hook(_gather_hook(own[0][:1]))
    layer_w = [{"w_in": _whole_weight("w_in", first, own[0][0], chip)}]

    dxt, loss_local, reduced, local_small = _local_grads(x, loss_target, p, conv5_w, conv3_w, layer_w, exchange=(chip, core, own))

    small_shapes = [local_small[n].shape for n in SMALL] + [(1,)]
    summed = _unpack(_allreduce_small(_pack([local_small[n] for n in SMALL] + [loss_local.reshape(1)]), chip), small_shapes)
    grads = dict(zip(SMALL, summed[:-1]))
    loss = summed[-1][0]
    grads["ssd_conv_w"] = lax.dynamic_slice_in_dim(grads["ssd_conv_w"], chip * 256, 256, axis=2)
    grads["ffn_conv_w"] = lax.dynamic_slice_in_dim(grads["ffn_conv_w"], chip * (2 * FFN_DIM // N_CHIPS), 2 * FFN_DIM // N_CHIPS, axis=2)

    halves = [jnp.stack([reduced[l][n] for l in range(DEPTH)]) for n in BIG]
    others = _run_hook(dict(_to_sibling_hook(halves), name="swap_halves"))

    delta, new_m, new_v = {}, {}, {}
    for n, mine, other in zip(BIG, halves, others):
        grads[n], delta[n], new_m[n], new_v[n] = _adamw(p[n], mine, other, env["m_" + n], env["v_" + n], core, ADAM_ROWS[n], "adamw_" + n)
    d, nm, nv = _adamw_many([p[n] for n in SMALL], [grads[n] for n in SMALL], [env["m_" + n] for n in SMALL], [env["v_" + n] for n in SMALL])
    for n, a, b, c in zip(SMALL, d, nm, nv):
        delta[n], new_m[n], new_v[n] = a, b, c

    return (loss, dxt.reshape(B_LOC, SEQ, D_MODEL), *[grads[n] for n in WEIGHTS], *[delta[n] for n in WEIGHTS],
            *[new_m[n] for n in WEIGHTS], *[new_v[n] for n in WEIGHTS])
```

```python
import functools
import math

import jax
import jax.numpy as jnp
import numpy as np
from jax import lax
from jax.experimental import pallas as pl
from jax.experimental.pallas import tpu as pltpu

F32 = jnp.float32
BF16 = jnp.bfloat16
HI = lax.Precision.HIGHEST
SUM_PRECISION = lax.Precision.HIGH
MESH = pl.DeviceIdType.MESH
ANY = pl.BlockSpec(memory_space=pl.ANY)

D_MODEL = 1024
SEQ = 2048
B_LOC = 2
T = B_LOC * SEQ
DEPTH = 4
N_CHIPS = 4
N_DEV = 8
HEAD = 64
CHUNK = 128
N_CHUNK = SEQ // CHUNK
SSD_INNER = 512
SSD_XBC = 1024
FFN_DIM = 2816
IN_WIDTH = 2832
NP = 3072
C_XS, C_B, C_C, C_Z, C_GU, C_GV, C_Q, C_K, C_V, C_DT = 0, 512, 768, 1024, 1536, 1792, 2048, 2304, 2560, 2816
NORM_EPS = 1e-6
NEG_INF = -1e30
ATTN_DILS = (1, 4, 16)
ATTN_HALF = 64
ADAM_LR, ADAM_B1, ADAM_B2, ADAM_EPS, ADAM_WD, ADAM_STEP = 0.001, 0.9, 0.999, 1e-08, 0.01, 10
VMEM_LIMIT = 56 * 1024 * 1024

S_ = jax.ShapeDtypeStruct


def _cp():
    return pltpu.CompilerParams(vmem_limit_bytes=VMEM_LIMIT)


def _shift_rows(x, k):
    n = x.shape[0]
    if k == 0:
        return x
    r = pltpu.roll(x, (-k) % n, 0)
    t = lax.broadcasted_iota(jnp.int32, (n, 1), 0)
    return jnp.where((t + k >= 0) & (t + k < n), r, 0.0)


@functools.partial(jax.custom_vjp, nondiff_argnums=(1,))
def _shift(x, k):
    return _shift_rows(x, k)


def _shift_fwd(x, k):
    return _shift_rows(x, k), None


def _shift_bwd(k, _, g):
    return (_shift_rows(g, -k),)


_shift.defvjp(_shift_fwd, _shift_bwd)


def _dwconv(x, taps, bias):
    half = len(taps) // 2
    y = bias
    for k, w in enumerate(taps):
        y = y + w * _shift(x, k - half)
    return y


def _softplus(x):
    return jnp.maximum(x, 0.0) + jnp.log1p(jnp.exp(-jnp.abs(x)))


def _dot(a, b):
    return jnp.dot(a.astype(BF16), b.astype(BF16), preferred_element_type=F32)


def _dot_nt(a, b):
    return lax.dot_general(a.astype(BF16), b.astype(BF16), (((1,), (1,)), ((), ())), preferred_element_type=F32)


def _dot_tn(a, b):
    return lax.dot_general(a.astype(BF16), b.astype(BF16), (((0,), (0,)), ((), ())), preferred_element_type=F32)


def _head_sum_matrix(width):
    i = lax.broadcasted_iota(jnp.int32, (width, width), 0) // HEAD
    j = lax.broadcasted_iota(jnp.int32, (width, width), 1) // HEAD
    return (i == j).astype(F32)


def _matmul(a, b, *, dims, grid, a_spec, b_spec, o_spec, out_shape, acc_shape, res=None, res_spec=None, norm_gain=None,
            norm_bwd=None, name):
    nk = grid[2]
    n_in = 2 + (res is not None) + (norm_gain is not None) + 2 * (norm_bwd is not None)

    def body(*refs):
        a_ref, b_ref = refs[:2]
        r_ref = refs[2] if res is not None else None
        g_ref = refs[n_in - 1] if norm_gain is not None or norm_bwd is not None else None
        x_ref = refs[n_in - 2] if norm_bwd is not None else None
        o_ref = refs[n_in]
        n_ref = refs[n_in + 1] if norm_gain is not None or norm_bwd is not None else None
        row_tile = pl.program_id(0)

        def finish(tot):
            if x_ref is not None:
                xv = x_ref[...]
                scale = lax.rsqrt(jnp.mean(xv * xv, axis=-1, keepdims=True) + NORM_EPS)
                gd = tot * g_ref[...]
                dot = jnp.mean(gd * xv, axis=-1, keepdims=True)
                dx = r_ref[...] + scale * gd - xv * (scale * scale * scale * dot)
                o_ref[...] = dx
                refs[n_in + 2][...] = dx.astype(BF16)
                part = jnp.sum(tot * xv * scale, axis=0, keepdims=True)

                @pl.when(row_tile == 0)
                def _():
                    n_ref[...] = part

                @pl.when(row_tile > 0)
                def _():
                    n_ref[...] += part

                return
            if r_ref is not None:
                tot = tot + r_ref[...]
            o_ref[...] = tot.astype(o_ref.dtype)
            if n_ref is not None:
                scale = lax.rsqrt(jnp.mean(tot * tot, axis=-1, keepdims=True) + NORM_EPS)
                n_ref[...] = (tot * scale * g_ref[...]).astype(BF16)

        part = lax.dot_general(a_ref[...].astype(BF16), b_ref[...].astype(BF16), dims, preferred_element_type=F32)
        if nk == 1:
            finish(part)
            return
        acc_ref = refs[-1]
        k = pl.program_id(2)

        @pl.when(k == 0)
        def _():
            acc_ref[...] = part

        @pl.when(k > 0)
        def _():
            acc_ref[...] += part

        @pl.when(k == nk - 1)
        def _():
            finish(acc_ref[...])

    in_specs = [a_spec, b_spec] + ([res_spec] if res is not None else [])
    args = (a, b) + ((res,) if res is not None else ())
    out_specs, out_shapes = o_spec, out_shape
    row = pl.BlockSpec((1, acc_shape[1]), lambda i, j, q: (0, 0))
    if norm_gain is not None:
        in_specs.append(row)
        args = args + (norm_gain,)
        out_specs, out_shapes = [o_spec, o_spec], [out_shape, S_(out_shape.shape, BF16)]
    if norm_bwd is not None:
        in_specs += [res_spec, row]
        args = args + tuple(norm_bwd)
        out_specs, out_shapes = [o_spec, row, o_spec], [out_shape, S_((1, acc_shape[1]), F32), S_(out_shape.shape, BF16)]
    scratch = [] if nk == 1 else [pltpu.VMEM(acc_shape, F32)]
    return pl.pallas_call(body, grid=grid, in_specs=in_specs, out_specs=out_specs, out_shape=out_shapes,
                          scratch_shapes=scratch, compiler_params=_cp(), name=name)(*args)


NN = (((1,), (0,)), ((), ()))
NT = (((1,), (1,)), ((), ()))
TN = (((0,), (0,)), ((), ()))


def _mm_nn(a, b, *, tm, tn, tk, out_dtype, res=None, norm_gain=None, name):
    m, k = a.shape
    n = b.shape[1]
    assert norm_gain is None or tn == n
    return _matmul(a, b, dims=NN, grid=(m // tm, n // tn, k // tk),
                   a_spec=pl.BlockSpec((tm, tk), lambda i, j, q: (i, q)),
                   b_spec=pl.BlockSpec((tk, tn), lambda i, j, q: (q, j)),
                   o_spec=pl.BlockSpec((tm, tn), lambda i, j, q: (i, j)),
                   out_shape=S_((m, n), out_dtype), acc_shape=(tm, tn), res=res,
                   res_spec=pl.BlockSpec((tm, tn), lambda i, j, q: (i, j)), norm_gain=norm_gain, name=name)


def _mm_nt(a, b, *, tm, tn, tk, out_dtype, res=None, norm_bwd=None, name):
    m, k = a.shape
    n = b.shape[0]
    assert norm_bwd is None or tn == n
    return _matmul(a, b, dims=NT, grid=(m // tm, n // tn, k // tk),
                   a_spec=pl.BlockSpec((tm, tk), lambda i, j, q: (i, q)),
                   b_spec=pl.BlockSpec((tn, tk), lambda i, j, q: (j, q)),
                   o_spec=pl.BlockSpec((tm, tn), lambda i, j, q: (i, j)),
                   out_shape=S_((m, n), out_dtype), acc_shape=(tm, tn), res=res,
                   res_spec=pl.BlockSpec((tm, tn), lambda i, j, q: (i, j)), norm_bwd=norm_bwd, name=name)


def _mm_tn(a, b, *, tm, tn, tk, out_dtype, name):
    k, m = a.shape
    n = b.shape[1]
    return _matmul(a, b, dims=TN, grid=(m // tm, n // tn, k // tk),
                   a_spec=pl.BlockSpec((tk, tm), lambda i, j, q: (q, i)),
                   b_spec=pl.BlockSpec((tk, tn), lambda i, j, q: (q, j)),
                   o_spec=pl.BlockSpec((tm, tn), lambda i, j, q: (i, j)),
                   out_shape=S_((m, n), out_dtype), acc_shape=(tm, tn), name=name)


HALF_TILE = FFN_DIM // 2


def _mm_up(hn, w_up):
    return _matmul(hn, w_up, dims=NN, grid=(T // 1024, 4, 1),
                   a_spec=pl.BlockSpec((1024, D_MODEL), lambda i, j, q: (i, 0)),
                   b_spec=pl.BlockSpec((None, D_MODEL, HALF_TILE), lambda i, j, q: (j, 0, 0)),
                   o_spec=pl.BlockSpec((None, 1024, HALF_TILE), lambda i, j, q: (j // 2, i, j % 2)),
                   out_shape=S_((2, T, FFN_DIM), F32), acc_shape=(1024, HALF_TILE), name="mm_up")


def _mm_dhn(dup3, w_up, x2, gain, dres):
    row = pl.BlockSpec((1024, D_MODEL), lambda i, j, q: (i, 0))
    return _matmul(dup3, w_up, dims=NT, grid=(T // 1024, 1, 4),
                   a_spec=pl.BlockSpec((None, 1024, HALF_TILE), lambda i, j, q: (q // 2, i, q % 2)),
                   b_spec=pl.BlockSpec((None, D_MODEL, HALF_TILE), lambda i, j, q: (q, 0, 0)),
                   o_spec=row, out_shape=S_((T, D_MODEL), F32), acc_shape=(1024, D_MODEL), res=dres, res_spec=row,
                   norm_bwd=(x2, gain), name="mm_dhn")


def _mm_dwup(hn, dup3):
    return _matmul(hn, dup3, dims=TN, grid=(1, 4, T // 1024),
                   a_spec=pl.BlockSpec((1024, D_MODEL), lambda i, j, q: (q, 0)),
                   b_spec=pl.BlockSpec((None, 1024, HALF_TILE), lambda i, j, q: (j // 2, q, j % 2)),
                   o_spec=pl.BlockSpec((None, D_MODEL, HALF_TILE), lambda i, j, q: (j, 0, 0)),
                   out_shape=S_((N_CHIPS, D_MODEL, HALF_TILE), BF16), acc_shape=(D_MODEL, HALF_TILE), name="mm_dwup")


ROWS = 512


def _rmsnorm_fwd(x, gain, name):
    def body(x_ref, g_ref, o_ref):
        xv = x_ref[...]
        r = lax.rsqrt(jnp.mean(xv * xv, axis=-1, keepdims=True) + NORM_EPS)
        o_ref[...] = (xv * r * g_ref[...]).astype(BF16)

    return pl.pallas_call(body, grid=(T // ROWS,),
                          in_specs=[pl.BlockSpec((ROWS, D_MODEL), lambda i: (i, 0)), pl.BlockSpec((1, D_MODEL), lambda i: (0, 0))],
                          out_specs=pl.BlockSpec((ROWS, D_MODEL), lambda i: (i, 0)),
                          out_shape=S_((T, D_MODEL), BF16), name=name)(x, gain)


def _loss_head(y, target):
    def body(y_ref, t_ref, dy_ref, p_ref, lo_ref):
        e = y_ref[...] - t_ref[...]
        dy_ref[...] = e * (1.0 / D_MODEL)
        lo_ref[...] = (e * (1.0 / D_MODEL)).astype(BF16)
        p_ref[...] = jnp.full((8, 128), 0.5 / D_MODEL, F32) * jnp.sum(e * e)

    row = pl.BlockSpec((ROWS, D_MODEL), lambda i: (i, 0))
    return pl.pallas_call(body, grid=(T // ROWS,), in_specs=[row, row],
                          out_specs=[row, pl.BlockSpec((8, 128), lambda i: (i, 0)), row],
                          out_shape=[S_((T, D_MODEL), F32), S_((T // ROWS * 8, 128), F32), S_((T, D_MODEL), BF16)],
                          name="loss_head")(y, target)


def _adamw_update(w_ref, g_ref, m_ref, v_ref, d_ref, nm_ref, nv_ref):
    gv = g_ref[...]
    nm = ADAM_B1 * m_ref[...] + (1.0 - ADAM_B1) * gv
    nv = ADAM_B2 * v_ref[...] + (1.0 - ADAM_B2) * (gv * gv)
    m_hat = nm / (1.0 - ADAM_B1 ** ADAM_STEP)
    v_hat = nv / (1.0 - ADAM_B2 ** ADAM_STEP)
    d_ref[...] = -ADAM_LR * (m_hat / (jnp.sqrt(v_hat) + ADAM_EPS) + ADAM_WD * w_ref[...])
    nm_ref[...] = nm
    nv_ref[...] = nv


def _adamw(w, g_mine, g_other, m, v, core, rows, name):
    per_half = w.shape[1] // 2 // rows

    def body(core_ref, w_ref, gm_ref, go_ref, m_ref, v_ref, g_ref, d_ref, nm_ref, nv_ref):
        mine = (pl.program_id(1) // per_half) == core_ref[0]
        g_ref[...] = jnp.where(mine, gm_ref[...], go_ref[...])
        _adamw_update(w_ref, g_ref, m_ref, v_ref, d_ref, nm_ref, nv_ref)

    blk = pl.BlockSpec((None, rows, w.shape[2]), lambda l, i, core_ref: (l, i, 0))
    half = pl.BlockSpec((None, rows, w.shape[2]), lambda l, i, core_ref: (l, i % per_half, 0))
    out = S_(w.shape, F32)
    spec = pltpu.PrefetchScalarGridSpec(num_scalar_prefetch=1, grid=(w.shape[0], w.shape[1] // rows),
                                        in_specs=[blk, half, half, blk, blk], out_specs=[blk] * 4)
    return pl.pallas_call(body, grid_spec=spec, out_shape=[out] * 4, name=name)(core.reshape(1).astype(jnp.int32), w, g_mine, g_other, m, v)


def _adamw_many(ws, gs, ms, vs):
    n = len(ws)

    def body(*refs):
        for i in range(n):
            _adamw_update(*[refs[k * n + i] for k in range(7)])

    out = [S_(w.shape, F32) for w in ws]
    res = pl.pallas_call(body, out_shape=out * 3, name="adamw_small")(*ws, *gs, *ms, *vs)
    return res[:n], res[n:2 * n], res[2 * n:]


FFN_CT = 256


def _gate_fn(up_g, up_v, wg0, wg1, wg2, bg, wv0, wv1, wv2, bv):
    gate = _dwconv(up_g, [wg0, wg1, wg2], bg)
    val = _dwconv(up_v, [wv0, wv1, wv2], bv)
    return jax.nn.silu(gate) * val


def _taps(ref, part, n):
    return [ref[part, k:k + 1, :] for k in range(n)]


def _convgate_fwd(up3, cw, cb, hook=None):
    def body(up_ref, cw_ref, cb_ref, o_ref):
        o_ref[...] = _gate_fn(up_ref[0], up_ref[1], *_taps(cw_ref, 0, 3), cb_ref[0], *_taps(cw_ref, 1, 3), cb_ref[1]).astype(BF16)

    (act,), got = _hooked_call(
        body, grid=(FFN_DIM // FFN_CT, B_LOC),
        in_specs=[pl.BlockSpec((2, SEQ, FFN_CT), lambda j, b: (0, b, j)),
                  pl.BlockSpec((2, 8, FFN_CT), lambda j, b: (0, 0, j)),
                  pl.BlockSpec((2, 1, FFN_CT), lambda j, b: (0, 0, j))],
        out_specs=[pl.BlockSpec((SEQ, FFN_CT), lambda j, b: (b, j))],
        out_shape=[S_((T, FFN_DIM), BF16)], scratch_shapes=[], args=(up3, cw, cb), hook=hook, name="convgate_fwd")
    return act, got


def _convgate_bwd(up3, cw, cb, dact, hook=None):
    def body(up_ref, cw_ref, cb_ref, da_ref, dup_ref, dcw_ref):
        args = (up_ref[0], up_ref[1], *_taps(cw_ref, 0, 3), cb_ref[0], *_taps(cw_ref, 1, 3), cb_ref[1])
        _, vjp = jax.vjp(_gate_fn, *args)
        dg, dv, g0, g1, g2, gb, v0, v1, v2, vb = vjp(da_ref[...])
        dup_ref[0] = dg.astype(BF16)
        dup_ref[1] = dv.astype(BF16)
        zero = jnp.zeros((4, FFN_CT), F32)
        new = jnp.stack([jnp.concatenate([g0, g1, g2, gb, zero], axis=0), jnp.concatenate([v0, v1, v2, vb, zero], axis=0)])

        @pl.when(pl.program_id(1) == 0)
        def _():
            dcw_ref[...] = new

        @pl.when(pl.program_id(1) > 0)
        def _():
            dcw_ref[...] += new

    return _hooked_call(
        body, grid=(FFN_DIM // FFN_CT, B_LOC),
        in_specs=[pl.BlockSpec((2, SEQ, FFN_CT), lambda j, b: (0, b, j)),
                  pl.BlockSpec((2, 8, FFN_CT), lambda j, b: (0, 0, j)),
                  pl.BlockSpec((2, 1, FFN_CT), lambda j, b: (0, 0, j)),
                  pl.BlockSpec((SEQ, FFN_CT), lambda j, b: (b, j))],
        out_specs=[pl.BlockSpec((2, SEQ, FFN_CT), lambda j, b: (0, b, j)),
                   pl.BlockSpec((2, 8, FFN_CT), lambda j, b: (0, 0, j))],
        out_shape=[S_((2, T, FFN_DIM), BF16), S_((2, 8, FFN_DIM), F32)],
        scratch_shapes=[], args=(up3, cw, cb, dact), hook=hook, name="convgate_bwd")


SSD_CT = 256


def _conv5_fn(x, w0, w1, w2, w3, w4, b):
    return jax.nn.silu(_dwconv(x, [w0, w1, w2, w3, w4], b))


def _ssd_pre_fwd(proj, cw, cb):
    def body(x_ref, cw_ref, cb_ref, o_ref):
        o_ref[...] = _conv5_fn(x_ref[...], *[cw_ref[k:k + 1, :] for k in range(5)], cb_ref[...])

    return pl.pallas_call(
        body, grid=(SSD_XBC // SSD_CT, B_LOC),
        in_specs=[pl.BlockSpec((SEQ, SSD_CT), lambda j, b: (b, j)),
                  pl.BlockSpec((8, SSD_CT), lambda j, b: (0, j)),
                  pl.BlockSpec((1, SSD_CT), lambda j, b: (0, j))],
        out_specs=pl.BlockSpec((SEQ, SSD_CT), lambda j, b: (b, j)),
        out_shape=S_((T, SSD_XBC), F32), compiler_params=_cp(), name="ssd_pre_fwd")(proj, cw, cb)


def _ssd_pre_bwd(proj, cw, cb, dxc, hook=None):
    def body(x_ref, cw_ref, cb_ref, d_ref, dx_ref, dcw_ref):
        _, vjp = jax.vjp(_conv5_fn, x_ref[...], *[cw_ref[k:k + 1, :] for k in range(5)], cb_ref[...])
        dx, g0, g1, g2, g3, g4, gb = vjp(d_ref[...])
        dx_ref[...] = dx.astype(BF16)
        new = jnp.concatenate([g0, g1, g2, g3, g4, gb, jnp.zeros((2, SSD_CT), F32)], axis=0)

        @pl.when(pl.program_id(1) == 0)
        def _():
            dcw_ref[...] = new

        @pl.when(pl.program_id(1) > 0)
        def _():
            dcw_ref[...] += new

    return _hooked_call(
        body, grid=(SSD_XBC // SSD_CT, B_LOC),
        in_specs=[pl.BlockSpec((SEQ, SSD_CT), lambda j, b: (b, j)),
                  pl.BlockSpec((8, SSD_CT), lambda j, b: (0, j)),
                  pl.BlockSpec((1, SSD_CT), lambda j, b: (0, j)),
                  pl.BlockSpec((SEQ, SSD_CT), lambda j, b: (b, j))],
        out_specs=[pl.BlockSpec((SEQ, SSD_CT), lambda j, b: (b, j)),
                   pl.BlockSpec((8, SSD_CT), lambda j, b: (0, j))],
        out_shape=[S_((T, SSD_XBC), BF16), S_((8, SSD_XBC), F32)],
        scratch_shapes=[], args=(proj, cw, cb, dxc), hook=hook, name="ssd_pre_bwd")


GROUP_W = 256
ONE_BUFFER = dict(pipeline_mode=pl.Buffered(1))
HEADS_PER_GROUP = 4
FWD_SCAN_UNROLL = 4
BWD_SCAN_UNROLL = 2


def _ssd_dt_fn(dt_raw, bias, alog):
    dt = _softplus(dt_raw + bias)
    return dt, dt * (-jnp.exp(alog))


def _ssd_chunk_fn(direction, group, xc0, xc1, bc, cc, dt, da, prev0, prev1):
    q = CHUNK
    ti = lax.broadcasted_iota(jnp.int32, (q, q), 0)
    si = lax.broadcasted_iota(jnp.int32, (q, q), 1)
    keep = (ti >= si) if direction == 0 else (ti <= si)
    mat = keep.astype(F32)
    acs = jnp.dot(mat, da, precision=HI, preferred_element_type=F32)
    acs_t = lax.dot_general(da, mat, (((0,), (1,)), ((), ())), precision=HI, preferred_element_type=F32)
    tot = jnp.sum(da, axis=0, keepdims=True)
    lane = lax.broadcasted_iota(jnp.int32, (1, 128), 1)
    sub = lax.broadcasted_iota(jnp.int32, (128, 1), 0)
    first_head = lane < HEAD
    cb = _dot_nt(cc, bc)
    a_cols, tots, dt_cols, lows, douts = [], [], [], [], []
    for h in range(HEADS_PER_GROUP):
        ln = 8 * direction + 4 * group + h
        oh_l = (lane == ln).astype(F32)
        oh_s = (sub == ln).astype(F32)
        a_col = jnp.sum(acs * oh_l, axis=1, keepdims=True)
        a_row = jnp.sum(acs_t * oh_s, axis=0, keepdims=True)
        tot_h = jnp.sum(tot * oh_l, axis=1, keepdims=True)
        a_cols.append(a_col)
        tots.append(tot_h)
        dt_cols.append(jnp.sum(dt * oh_l, axis=1, keepdims=True))
        lows.append(cb * jnp.exp(jnp.where(keep, a_col - a_row, NEG_INF)))
        douts.append(bc * jnp.exp(tot_h - a_col))
    out = []
    for pair, (xc, prev) in enumerate(((xc0, prev0), (xc1, prev1))):
        h0, h1 = 2 * pair, 2 * pair + 1
        xdt = xc * jnp.where(first_head, dt_cols[h0], dt_cols[h1])
        y = jnp.where(first_head, jnp.exp(a_cols[h0]), jnp.exp(a_cols[h1])) * _dot(cc, prev)
        y = y + jnp.where(first_head, _dot(lows[h0], xdt), _dot(lows[h1], xdt))
        st = jnp.where(first_head, _dot_tn(douts[h0], xdt), _dot_tn(douts[h1], xdt))
        out.append((y, prev * jnp.where(first_head, jnp.exp(tots[h0]), jnp.exp(tots[h1])) + st))
    return out[0][0], out[1][0], out[0][1], out[1][1]


def _ssd_post_fn(y, xc, z, d_exp, gain):
    y = (y + d_exp * xc) * jax.nn.silu(z)
    return y * lax.rsqrt(jnp.mean(y * y, axis=-1, keepdims=True) + NORM_EPS) * gain


def _chunk_rows(c):
    return pl.ds(pl.multiple_of(c * CHUNK, CHUNK), CHUNK)


def _scan_loop(step, init, unroll):
    def body(i, carry):
        for k in range(unroll):
            carry = step(i * unroll + k, carry)
        return carry

    return lax.fori_loop(0, N_CHUNK // unroll, body, init)


def _ssd_scan_specs(**mode):
    return [pl.BlockSpec((SEQ, GROUP_W), lambda g, b: (b, g), **mode),
            pl.BlockSpec((SEQ, 128), lambda g, b: (b, C_B // 128 + g), **mode),
            pl.BlockSpec((SEQ, 128), lambda g, b: (b, C_C // 128 + g), **mode),
            pl.BlockSpec((SEQ, GROUP_W), lambda g, b: (b, C_Z // GROUP_W + g), **mode),
            pl.BlockSpec((SEQ, 128), lambda g, b: (b, C_DT // 128), **mode),
            pl.BlockSpec((1, 128), lambda g, b: (0, 0)),
            pl.BlockSpec((1, 128), lambda g, b: (0, 0)),
            pl.BlockSpec((1, GROUP_W), lambda g, b: (0, g)),
            pl.BlockSpec((1, GROUP_W), lambda g, b: (0, g))]


def _ssd_state_spec(**mode):
    return pl.BlockSpec((None, None, 2 * N_CHUNK, 128, GROUP_W), lambda g, b: (g, b, 0, 0, 0), **mode)


def _ssd_scan_fwd(xc, proj, dtb, alog, d_exp, gain, hook=None):
    def body(x_ref, b_ref, c_ref, z_ref, dt_ref, dtb_ref, al_ref, de_ref, g_ref, o_ref, y_s, st_ref, dt_s, da_s):
        group = pl.program_id(0)
        dt, da = _ssd_dt_fn(dt_ref[...], dtb_ref[...], al_ref[...])
        dt_s[...] = dt
        da_s[...] = da
        for direction in (0, 1):
            def step(i, prev, direction=direction):
                c = i if direction == 0 else N_CHUNK - 1 - i
                rows = _chunk_rows(c)
                st_ref[direction * N_CHUNK + c, :, 0:128] = prev[0]
                st_ref[direction * N_CHUNK + c, :, 128:256] = prev[1]
                y0, y1, nxt0, nxt1 = _ssd_chunk_fn(direction, group, x_ref[rows, 0:128], x_ref[rows, 128:256], b_ref[rows, :], c_ref[rows, :],
                                                   dt_s[rows, :], da_s[rows, :], prev[0], prev[1])
                if direction == 0:
                    y_s[rows, 0:128] = y0
                    y_s[rows, 128:256] = y1
                else:
                    y_s[rows, 0:128] += y0
                    y_s[rows, 128:256] += y1
                return nxt0, nxt1

            _scan_loop(step, (jnp.zeros((128, 128), F32), jnp.zeros((128, 128), F32)), FWD_SCAN_UNROLL)

        def post(c, carry):
            rows = _chunk_rows(c)
            o_ref[rows, :] = _ssd_post_fn(y_s[rows, :], x_ref[rows, :], z_ref[rows, :], de_ref[...], g_ref[...]).astype(BF16)
            return carry

        lax.fori_loop(0, N_CHUNK, post, 0)

    return _hooked_call(
        body, grid=(2, B_LOC), in_specs=_ssd_scan_specs(),
        out_specs=[pl.BlockSpec((SEQ, GROUP_W), lambda g, b: (b, g)), pl.BlockSpec((SEQ, GROUP_W), lambda g, b: (b, g)), _ssd_state_spec()],
        out_shape=[S_((T, SSD_INNER), BF16), S_((T, SSD_INNER), F32), S_((2, B_LOC, 2 * N_CHUNK, 128, GROUP_W), F32)],
        scratch_shapes=[pltpu.VMEM((SEQ, 128), F32), pltpu.VMEM((SEQ, 128), F32)],
        args=(xc, xc, xc, proj, proj, dtb, alog, d_exp, gain), hook=hook, name="ssd_scan_fwd")


def _ssd_scan_bwd(xc, proj, dtb, alog, d_exp, gain, dy, ysum, states, hook=None):
    def body(x_ref, b_ref, c_ref, z_ref, dt_ref, dtb_ref, al_ref, de_ref, g_ref, dy_ref, ys_ref, st_s,
             dx_ref, db_ref, dc_ref, dz_ref, ddt_ref, ddtb_ref, dal_ref, dde_ref, dg_ref,
             dt_s, da_s, y_s, ddt_s, dda_s):
        group = pl.program_id(0)
        first = pl.program_id(1) == 0
        (dt, da), dt_vjp = jax.vjp(_ssd_dt_fn, dt_ref[...], dtb_ref[...], al_ref[...])
        dt_s[...] = dt
        da_s[...] = da

        def post(c, carry):
            rows = _chunk_rows(c)
            _, post_vjp = jax.vjp(_ssd_post_fn, ys_ref[rows, :], x_ref[rows, :], z_ref[rows, :], de_ref[...], g_ref[...])
            d_y, d_x_skip, d_z, g_de, g_g = post_vjp(dy_ref[rows, :])
            dz_ref[rows, :] = d_z.astype(BF16)
            dx_ref[rows, :] = d_x_skip
            y_s[rows, :] = d_y
            return carry[0] + g_de, carry[1] + g_g

        d_de, d_g = lax.fori_loop(0, N_CHUNK, post, (jnp.zeros((1, GROUP_W), F32), jnp.zeros((1, GROUP_W), F32)))
        db_ref[...] = jnp.zeros((SEQ, 128), F32)
        dc_ref[...] = jnp.zeros((SEQ, 128), F32)
        ddt_s[...] = jnp.zeros((SEQ, 128), F32)
        dda_s[...] = jnp.zeros((SEQ, 128), F32)
        for direction in (0, 1):
            def bstep(i, dnxt, direction=direction):
                c = N_CHUNK - 1 - i if direction == 0 else i
                rows = _chunk_rows(c)
                fn = functools.partial(_ssd_chunk_fn, direction, group)
                _, vjp = jax.vjp(fn, x_ref[rows, 0:128], x_ref[rows, 128:256], b_ref[rows, :], c_ref[rows, :], dt_s[rows, :], da_s[rows, :],
                                 st_s[direction * N_CHUNK + c, :, 0:128], st_s[direction * N_CHUNK + c, :, 128:256])
                g_x0, g_x1, g_b, g_c, g_dt, g_da, g_prev0, g_prev1 = vjp((y_s[rows, 0:128], y_s[rows, 128:256], dnxt[0], dnxt[1]))
                dx_ref[rows, 0:128] += g_x0
                dx_ref[rows, 128:256] += g_x1
                db_ref[rows, :] += g_b
                dc_ref[rows, :] += g_c
                ddt_s[rows, :] += g_dt
                dda_s[rows, :] += g_da
                return g_prev0, g_prev1

            _scan_loop(bstep, (jnp.zeros((128, 128), F32), jnp.zeros((128, 128), F32)), BWD_SCAN_UNROLL)
        g_raw, g_bias, g_alog = dt_vjp((ddt_s[...], dda_s[...]))
        ddt_ref[...] = g_raw
        pad7 = jnp.zeros((7, 128), F32)
        new_b = jnp.concatenate([g_bias, pad7], axis=0)
        new_a = jnp.concatenate([g_alog, pad7], axis=0)

        @pl.when(first)
        def _():
            ddtb_ref[...] = new_b
            dal_ref[...] = new_a
            dde_ref[...] = d_de
            dg_ref[...] = d_g

        @pl.when(jnp.logical_not(first))
        def _():
            ddtb_ref[...] += new_b
            dal_ref[...] += new_a
            dde_ref[...] += d_de
            dg_ref[...] += d_g

    return _hooked_call(
        body, grid=(2, B_LOC),
        in_specs=_ssd_scan_specs(**ONE_BUFFER) + [pl.BlockSpec((SEQ, GROUP_W), lambda g, b: (b, g), **ONE_BUFFER),
                                                  pl.BlockSpec((SEQ, GROUP_W), lambda g, b: (b, g), **ONE_BUFFER),
                                                  _ssd_state_spec(**ONE_BUFFER)],
        out_specs=[pl.BlockSpec((SEQ, GROUP_W), lambda g, b: (b, g)),
                   pl.BlockSpec((SEQ, 128), lambda g, b: (b, g)),
                   pl.BlockSpec((SEQ, 128), lambda g, b: (b, g)),
                   pl.BlockSpec((SEQ, GROUP_W), lambda g, b: (b, g)),
                   pl.BlockSpec((None, SEQ, 128), lambda g, b: (g, b, 0)),
                   pl.BlockSpec((None, 8, 128), lambda g, b: (g, 0, 0)),
                   pl.BlockSpec((None, 8, 128), lambda g, b: (g, 0, 0)),
                   pl.BlockSpec((1, GROUP_W), lambda g, b: (0, g)),
                   pl.BlockSpec((1, GROUP_W), lambda g, b: (0, g))],
        out_shape=[S_((T, SSD_INNER), F32), S_((T, 256), F32), S_((T, 256), F32), S_((T, SSD_INNER), BF16),
                   S_((2, T, 128), F32), S_((2, 8, 128), F32), S_((2, 8, 128), F32),
                   S_((1, SSD_INNER), F32), S_((1, SSD_INNER), F32)],
        scratch_shapes=[pltpu.VMEM((SEQ, 128), F32), pltpu.VMEM((SEQ, 128), F32), pltpu.VMEM((SEQ, GROUP_W), F32),
                        pltpu.VMEM((SEQ, 128), F32), pltpu.VMEM((SEQ, 128), F32)],
        args=(xc, xc, xc, proj, proj, dtb, alog, d_exp, gain, dy, ysum, states), hook=hook, name="ssd_scan_bwd")


GMLP_W = 256


def _gmlp_chunk_fn(gu, gv, v_gain, w0, w1, w2, w3, b_exp):
    u = jax.nn.gelu(gu)
    v = jax.nn.gelu(gv)
    v = v * lax.rsqrt(jnp.mean(v * v, axis=-1, keepdims=True) + NORM_EPS) * v_gain
    col = lax.broadcasted_iota(jnp.int32, (1, GMLP_W), 1) // HEAD
    mixed = b_exp
    for g, w in enumerate((w0, w1, w2, w3)):
        mixed = mixed + (col == g).astype(F32) * _dot(w, v)
    return u * mixed


def _gmlp_specs():
    return [pl.BlockSpec((SEQ, GMLP_W), lambda b: (b, C_GU // GMLP_W)),
            pl.BlockSpec((SEQ, GMLP_W), lambda b: (b, C_GV // GMLP_W)),
            pl.BlockSpec((1, GMLP_W), lambda b: (0, 0)),
            pl.BlockSpec((4, CHUNK, CHUNK), lambda b: (0, 0, 0)),
            pl.BlockSpec((CHUNK, GMLP_W), lambda b: (0, 0))]


def _gmlp_fwd(proj, v_gain, w_s, b_exp):
    def body(u_ref, v_ref, g_ref, w_ref, b_ref, o_ref):
        def step(c, carry):
            rows = _chunk_rows(c)
            o_ref[rows, :] = _gmlp_chunk_fn(u_ref[rows, :], v_ref[rows, :], g_ref[...], w_ref[0], w_ref[1], w_ref[2], w_ref[3],
                                            b_ref[...]).astype(BF16)
            return carry

        lax.fori_loop(0, N_CHUNK, step, 0)

    return pl.pallas_call(body, grid=(B_LOC,), in_specs=_gmlp_specs(),
                          out_specs=pl.BlockSpec((SEQ, GMLP_W), lambda b: (b, 0)),
                          out_shape=S_((T, GMLP_W), BF16), name="gmlp_fwd")(proj, proj, v_gain, w_s, b_exp)


def _gmlp_bwd(proj, v_gain, w_s, b_exp, dy):
    def body(u_ref, v_ref, g_ref, w_ref, b_ref, dy_ref, du_ref, dv_ref, dg_ref, dw_ref, db_ref):
        @pl.when(pl.program_id(0) == 0)
        def _():
            dg_ref[...] = jnp.zeros_like(dg_ref)
            dw_ref[...] = jnp.zeros_like(dw_ref)
            db_ref[...] = jnp.zeros_like(db_ref)

        def step(c, carry):
            rows = _chunk_rows(c)
            _, vjp = jax.vjp(_gmlp_chunk_fn, u_ref[rows, :], v_ref[rows, :], g_ref[...], w_ref[0], w_ref[1], w_ref[2], w_ref[3], b_ref[...])
            g_u, g_v, g_g, g_w0, g_w1, g_w2, g_w3, g_b = vjp(dy_ref[rows, :])
            du_ref[rows, :] = g_u.astype(BF16)
            dv_ref[rows, :] = g_v.astype(BF16)
            dg_ref[...] += g_g
            db_ref[...] += g_b
            for g, gw in enumerate((g_w0, g_w1, g_w2, g_w3)):
                dw_ref[g] += gw
            return carry

        lax.fori_loop(0, N_CHUNK, step, 0)

    blk = pl.BlockSpec((SEQ, GMLP_W), lambda b: (b, 0))
    return pl.pallas_call(
        body, grid=(B_LOC,),
        in_specs=_gmlp_specs() + [pl.BlockSpec((SEQ, GMLP_W), lambda b: (b, SSD_INNER // GMLP_W))],
        out_specs=[blk, blk, pl.BlockSpec((1, GMLP_W), lambda b: (0, 0)),
                   pl.BlockSpec((4, CHUNK, CHUNK), lambda b: (0, 0, 0)), pl.BlockSpec((CHUNK, GMLP_W), lambda b: (0, 0))],
        out_shape=[S_((T, GMLP_W), BF16), S_((T, GMLP_W), BF16), S_((1, GMLP_W), F32),
                   S_((4, CHUNK, CHUNK), F32), S_((CHUNK, GMLP_W), F32)],
        name="gmlp_bwd")(proj, proj, v_gain, w_s, b_exp, dy)


PAIR_W = 128
QB = 128
KW = QB + 2 * ATTN_HALF
N_QB = SEQ // QB
FWD_BLOCK_UNROLL = 16
BWD_BLOCK_UNROLL = 16
PAD_ROWS = SEQ + 2 * ATTN_HALF


def _qk_norm_fn(x, gain):
    ms = jnp.dot(x * x, _head_sum_matrix(PAIR_W), precision=SUM_PRECISION, preferred_element_type=F32) * (1.0 / HEAD)
    return x * lax.rsqrt(ms + NORM_EPS) * gain


def _deinterleave(dst_ref, src_ref, dil, offset):
    length = SEQ // dil
    if dil == 1:
        dst_ref[pl.ds(offset, SEQ), :] = src_ref[...]
        return
    for r in range(dil):
        dst_ref[pl.ds(offset + r * length, length), :] = src_ref[pl.ds(r, length, stride=dil), :]


def _interleave(dst_ref, src_ref, dil, offset):
    length = SEQ // dil
    if dil == 1:
        dst_ref[...] = src_ref[pl.ds(offset, SEQ), :]
        return
    for r in range(dil):
        dst_ref[pl.ds(r, length, stride=dil), :] = src_ref[pl.ds(offset + r * length, length), :]


def _edge_mask(blk, dil):
    length = SEQ // dil
    qi = blk * QB + lax.broadcasted_iota(jnp.int32, (QB, KW), 0)
    kj = blk * QB - ATTN_HALF + lax.broadcasted_iota(jnp.int32, (QB, KW), 1)
    return (kj >= 0) & (kj < SEQ) & ((qi // length) == (kj // length))


def _lane_is_head(hh):
    return (lax.broadcasted_iota(jnp.int32, (1, PAIR_W), 1) // HEAD) == hh


def _dilate_qkv(dil, qn_s, kn_s, v_ref, qd_s, kd_s, vd_s):
    _deinterleave(qd_s, qn_s, dil, 0)
    _deinterleave(kd_s, kn_s, dil, ATTN_HALF)
    _deinterleave(vd_s, v_ref, dil, ATTN_HALF)


def _attn_branch_fwd(br, dil, qn_s, kn_s, v_ref, bias_ref, qd_s, kd_s, vd_s, od_s, ld_s):
    _dilate_qkv(dil, qn_s, kn_s, v_ref, qd_s, kd_s, vd_s)

    def step(blk, carry):
        rows = pl.ds(pl.multiple_of(blk * QB, QB), QB)
        win = pl.ds(pl.multiple_of(blk * QB, QB), KW)
        qb, kw, vw = qd_s[rows, :], kd_s[win, :], vd_s[win, :]
        edge = _edge_mask(blk, dil)
        out, lse = 0.0, 0.0
        for hh in range(2):
            is_h = _lane_is_head(hh)
            s = _dot_nt(jnp.where(is_h, qb, 0.0), kw) * (HEAD ** -0.5) + bias_ref[br, hh]
            s = jnp.where(edge, s, NEG_INF)
            m = jnp.max(s, axis=-1, keepdims=True)
            l_h = m + jnp.log(jnp.sum(jnp.exp(s - m), axis=-1, keepdims=True))
            out = out + jnp.where(is_h, _dot(jnp.exp(s - l_h), vw), 0.0)
            lse = lse + jnp.where(is_h, l_h, 0.0)
        od_s[rows, :] = out
        ld_s[rows, :] = lse
        return carry

    _block_loop(step, FWD_BLOCK_UNROLL)


def _block_loop(step, unroll):
    def body(i, carry):
        for k in range(unroll):
            carry = step(i * unroll + k, carry)
        return carry

    lax.fori_loop(0, N_QB // unroll, body, 0)


def _attn_specs():
    col = lambda c0: (lambda p, b: (b, c0 // PAIR_W + p))
    return [pl.BlockSpec((SEQ, PAIR_W), col(C_Q)), pl.BlockSpec((SEQ, PAIR_W), col(C_K)), pl.BlockSpec((SEQ, PAIR_W), col(C_V)),
            pl.BlockSpec((1, PAIR_W), lambda p, b: (0, 0)), pl.BlockSpec((1, PAIR_W), lambda p, b: (0, 0)),
            pl.BlockSpec((3, 2, QB, KW), lambda p, b: (0, p, 0, 0))]


def _attn_scratch():
    seq = pltpu.VMEM((SEQ, PAIR_W), F32)
    pad = pltpu.VMEM((PAD_ROWS, PAIR_W), F32)
    return [seq, seq, seq, pad, pad, seq, seq]


def _zero_pads(*refs):
    for ref in refs:
        ref[pl.ds(0, ATTN_HALF), :] = jnp.zeros((ATTN_HALF, PAIR_W), F32)
        ref[pl.ds(ATTN_HALF + SEQ, ATTN_HALF), :] = jnp.zeros((ATTN_HALF, PAIR_W), F32)


ROW_STEP = 256


def _row_steps(fn, init=0):
    return lax.fori_loop(0, SEQ // ROW_STEP, lambda i, c: fn(pl.ds(pl.multiple_of(i * ROW_STEP, ROW_STEP), ROW_STEP), c), init)


def _interleave_add(acc_ref, src_ref, dil, offset):
    length = SEQ // dil
    if dil == 1:
        acc_ref[...] += src_ref[pl.ds(offset, SEQ), :]
        return
    for r in range(dil):
        acc_ref[pl.ds(r, length, stride=dil), :] += src_ref[pl.ds(offset + r * length, length), :]


def _attn_norm_qk(q_ref, k_ref, qg_ref, kg_ref, qn_s, kn_s):
    def norm(rows, carry):
        qn_s[rows, :] = _qk_norm_fn(q_ref[rows, :], qg_ref[...])
        kn_s[rows, :] = _qk_norm_fn(k_ref[rows, :], kg_ref[...])
        return carry

    _row_steps(norm)


def _attn_forward_all(q_ref, k_ref, v_ref, qg_ref, kg_ref, bias_ref, qn_s, kn_s, qd_s, kd_s, vd_s, od_s, ld_s, on_s, ln_s):
    _attn_norm_qk(q_ref, k_ref, qg_ref, kg_ref, qn_s, kn_s)
    _zero_pads(kd_s, vd_s)
    for br, dil in enumerate(ATTN_DILS):
        _attn_branch_fwd(br, dil, qn_s, kn_s, v_ref, bias_ref, qd_s, kd_s, vd_s, od_s, ld_s)
        _interleave(on_s.at[br], od_s, dil, 0)
        _interleave(ln_s.at[br], ld_s, dil, 0)


def _merge_weights(ln_s, rows):
    l0, l1, l2 = ln_s[0, rows, :], ln_s[1, rows, :], ln_s[2, rows, :]
    m = jnp.maximum(jnp.maximum(l0, l1), l2)
    e = [jnp.exp(l0 - m), jnp.exp(l1 - m), jnp.exp(l2 - m)]
    den = e[0] + e[1] + e[2]
    return [e[0] / den, e[1] / den, e[2] / den]


def _attn_fwd(proj, q_gain, k_gain, bias, hook=None):
    def body(q_ref, k_ref, v_ref, qg_ref, kg_ref, bias_ref, o_ref, on_s, ln_s, qn_s, kn_s, qd_s, kd_s, vd_s, od_s, ld_s):
        _attn_forward_all(q_ref, k_ref, v_ref, qg_ref, kg_ref, bias_ref, qn_s, kn_s, qd_s, kd_s, vd_s, od_s, ld_s, on_s, ln_s)

        def merge(rows, carry):
            w = _merge_weights(ln_s, rows)
            o_ref[rows, :] = (w[0] * on_s[0, rows, :] + w[1] * on_s[1, rows, :] + w[2] * on_s[2, rows, :]).astype(BF16)
            return carry

        _row_steps(merge)

    kept = pl.BlockSpec((3, SEQ, PAIR_W), lambda p, b: (0, b, p))
    return _hooked_call(body, grid=(2, B_LOC), in_specs=_attn_specs(),
                        out_specs=[pl.BlockSpec((SEQ, PAIR_W), lambda p, b: (b, p)), kept, kept],
                        out_shape=[S_((T, 2 * PAIR_W), BF16), S_((3, T, 2 * PAIR_W), F32), S_((3, T, 2 * PAIR_W), F32)],
                        scratch_shapes=_attn_scratch(), args=(proj, proj, proj, q_gain, k_gain, bias), hook=hook, name="attn_fwd")


def _attn_bwd(proj, q_gain, k_gain, bias, dy, kept_o, kept_l, hook=None):
    def body(q_ref, k_ref, v_ref, qg_ref, kg_ref, bias_ref, dy_ref, on_ref, ln_ref,
             dq_ref, dk_ref, dv_ref, dqg_ref, dkg_ref, dbias_ref,
             qn_s, kn_s, qd_s, kd_s, vd_s, od_s, ld_s, don_s, dln_s, dod_s, dld_s, dqd_s, dkd_s, dvd_s, dqn_s, dkn_s, dvn_s):
        first = pl.program_id(1) == 0
        _attn_norm_qk(q_ref, k_ref, qg_ref, kg_ref, qn_s, kn_s)
        _zero_pads(kd_s, vd_s)

        def clear_acc(rows, carry):
            dqn_s[rows, :] = jnp.zeros((ROW_STEP, PAIR_W), F32)
            dkn_s[rows, :] = jnp.zeros((ROW_STEP, PAIR_W), F32)
            dvn_s[rows, :] = jnp.zeros((ROW_STEP, PAIR_W), F32)
            return carry

        _row_steps(clear_acc)

        @pl.when(first)
        def _():
            dbias_ref[...] = jnp.zeros_like(dbias_ref)

        def merge_bwd(rows, carry):
            w = _merge_weights(ln_ref, rows)
            dy = dy_ref[rows, :]
            same_head = _head_sum_matrix(PAIR_W)
            dws = [jnp.dot(dy * on_ref[j, rows, :], same_head, precision=SUM_PRECISION, preferred_element_type=F32) for j in range(3)]
            dbar = w[0] * dws[0] + w[1] * dws[1] + w[2] * dws[2]
            for j in range(3):
                don_s[j, rows, :] = w[j] * dy
                dln_s[j, rows, :] = w[j] * (dws[j] - dbar)
            return carry

        _row_steps(merge_bwd)
        for br, dil in enumerate(ATTN_DILS):
            _dilate_qkv(dil, qn_s, kn_s, v_ref, qd_s, kd_s, vd_s)
            _deinterleave(od_s, on_ref.at[br], dil, 0)
            _deinterleave(ld_s, ln_ref.at[br], dil, 0)
            _deinterleave(dod_s, don_s.at[br], dil, 0)
            _deinterleave(dld_s, dln_s.at[br], dil, 0)

            def clear(rows, carry):
                dkd_s[rows, :] = jnp.zeros((ROW_STEP, PAIR_W), F32)
                dvd_s[rows, :] = jnp.zeros((ROW_STEP, PAIR_W), F32)
                return carry

            _row_steps(clear)
            tail = pl.ds(SEQ, 2 * ATTN_HALF)
            dkd_s[tail, :] = jnp.zeros((2 * ATTN_HALF, PAIR_W), F32)
            dvd_s[tail, :] = jnp.zeros((2 * ATTN_HALF, PAIR_W), F32)

            def step(blk, carry, br=br, dil=dil):
                rows = pl.ds(pl.multiple_of(blk * QB, QB), QB)
                win = pl.ds(pl.multiple_of(blk * QB, QB), KW)
                qb, kw, vw = qd_s[rows, :], kd_s[win, :], vd_s[win, :]
                do_b, dl_b, o_b, l_b = dod_s[rows, :], dld_s[rows, :], od_s[rows, :], ld_s[rows, :]
                edge = _edge_mask(blk, dil)
                dq, dk, dv = 0.0, 0.0, 0.0
                for hh in range(2):
                    is_h = _lane_is_head(hh)
                    pick = (lax.broadcasted_iota(jnp.int32, (1, PAIR_W), 1) == hh * HEAD).astype(F32)
                    q_h = jnp.where(is_h, qb, 0.0)
                    do_h = jnp.where(is_h, do_b, 0.0)
                    s = _dot_nt(q_h, kw) * (HEAD ** -0.5) + bias_ref[br, hh]
                    s = jnp.where(edge, s, NEG_INF)
                    p = jnp.exp(s - jnp.sum(l_b * pick, axis=-1, keepdims=True))
                    dp = _dot_nt(do_h, vw)
                    delta = jnp.sum(do_h * o_b, axis=-1, keepdims=True)
                    ds = p * (dp - delta + jnp.sum(dl_b * pick, axis=-1, keepdims=True))
                    dbias_ref[br, hh] += ds
                    dq = dq + jnp.where(is_h, _dot(ds, kw), 0.0) * (HEAD ** -0.5)
                    dk = dk + _dot_tn(ds, q_h) * (HEAD ** -0.5)
                    dv = dv + _dot_tn(p, do_h)
                dqd_s[rows, :] = dq
                dkd_s[win, :] += dk
                dvd_s[win, :] += dv
                return carry

            _block_loop(step, BWD_BLOCK_UNROLL)
            _interleave_add(dqn_s, dqd_s, dil, 0)
            _interleave_add(dkn_s, dkd_s, dil, ATTN_HALF)
            _interleave_add(dvn_s, dvd_s, dil, ATTN_HALF)

        def norm_bwd(rows, carry):
            _, q_vjp = jax.vjp(_qk_norm_fn, q_ref[rows, :], qg_ref[...])
            _, k_vjp = jax.vjp(_qk_norm_fn, k_ref[rows, :], kg_ref[...])
            g_q, g_qg = q_vjp(dqn_s[rows, :])
            g_k, g_kg = k_vjp(dkn_s[rows, :])
            dq_ref[rows, :] = g_q.astype(BF16)
            dk_ref[rows, :] = g_k.astype(BF16)
            dv_ref[rows, :] = dvn_s[rows, :].astype(BF16)
            return carry[0] + g_qg, carry[1] + g_kg

        g_qg, g_kg = _row_steps(norm_bwd, (jnp.zeros((1, PAIR_W), F32), jnp.zeros((1, PAIR_W), F32)))
        pad7 = jnp.zeros((7, PAIR_W), F32)
        new_q = jnp.concatenate([g_qg, pad7], axis=0)
        new_k = jnp.concatenate([g_kg, pad7], axis=0)

        @pl.when(first)
        def _():
            dqg_ref[...] = new_q
            dkg_ref[...] = new_k

        @pl.when(jnp.logical_not(first))
        def _():
            dqg_ref[...] += new_q
            dkg_ref[...] += new_k

    seq = pltpu.VMEM((SEQ, PAIR_W), F32)
    seq3 = pltpu.VMEM((3, SEQ, PAIR_W), F32)
    pad = pltpu.VMEM((PAD_ROWS, PAIR_W), F32)
    kept = pl.BlockSpec((3, SEQ, PAIR_W), lambda p, b: (0, b, p))
    out_blk = pl.BlockSpec((SEQ, PAIR_W), lambda p, b: (b, p))
    gain_blk = pl.BlockSpec((None, 8, PAIR_W), lambda p, b: (p, 0, 0))
    return _hooked_call(
        body, grid=(2, B_LOC),
        in_specs=_attn_specs() + [pl.BlockSpec((SEQ, PAIR_W), lambda p, b: (b, (SSD_INNER + GMLP_W) // PAIR_W + p)), kept, kept],
        out_specs=[out_blk, out_blk, out_blk, gain_blk, gain_blk, pl.BlockSpec((3, 2, QB, KW), lambda p, b: (0, p, 0, 0))],
        out_shape=[S_((T, 2 * PAIR_W), BF16)] * 3 + [S_((2, 8, PAIR_W), F32)] * 2 + [S_((3, 4, QB, KW), F32)],
        scratch_shapes=_attn_scratch() + [seq3, seq3, seq, seq, seq, pad, pad, seq, seq, seq],
        args=(proj, proj, proj, q_gain, k_gain, bias, dy, kept_o, kept_l), hook=hook, name="attn_bwd")


def _rel_bucket(rel):
    nb = 16
    max_exact = nb // 2
    n = jnp.abs(rel)
    large = max_exact + (jnp.log(jnp.maximum(n, 1).astype(F32) / max_exact) / math.log(1024 / max_exact) * (nb - max_exact)).astype(jnp.int32)
    large = jnp.minimum(large, nb - 1)
    return jnp.where(rel > 0, nb, 0) + jnp.where(n < max_exact, n, large)


def _attn_bias(rel_table):
    rel = jnp.arange(KW)[None, :] - ATTN_HALF - jnp.arange(QB)[:, None]
    inside = (jnp.abs(rel) <= ATTN_HALF)
    out = []
    for dil in ATTN_DILS:
        one_hot = (_rel_bucket(rel * dil)[None] == jnp.arange(32)[:, None, None]).astype(F32)
        b = jnp.einsum("kh,kts->hts", rel_table, one_hot, precision=HI)
        out.append(jnp.where(inside[None], b, NEG_INF))
    return jnp.stack(out).astype(F32)


def _place():
    return lax.axis_index("x"), lax.axis_index("y"), lax.axis_index("c")


def _allgather8(buf, name):
    rows = buf.shape[0]
    flips = [(fx, fy, fc) for fx in (0, 1) for fy in (0, 1) for fc in (0, 1)][1:]

    def body(in_ref, out_ref, send_sems, recv_sems):
        x, y, c = _place()
        me = 4 * x + 2 * y + c
        peers = [(1 - x if fx else x, 1 - y if fy else y, 1 - c if fc else c) for fx, fy, fc in flips]

        def copy(k, slot, peer):
            return pltpu.make_async_remote_copy(src_ref=in_ref, dst_ref=out_ref.at[slot], send_sem=send_sems.at[k],
                                                recv_sem=recv_sems.at[k], device_id=peer, device_id_type=MESH)

        sends = [copy(k, me, peer) for k, peer in enumerate(peers)]
        for cp in sends:
            cp.start()
        for k, (px, py, pc) in enumerate(peers):
            copy(k, 4 * px + 2 * py + pc, (px, py, pc)).wait_recv()
        for cp in sends:
            cp.wait_send()

    slots = pl.pallas_call(body, in_specs=[ANY], out_specs=ANY, out_shape=S_((N_DEV, rows, 128), F32),
                           scratch_shapes=[pltpu.SemaphoreType.DMA((7,)), pltpu.SemaphoreType.DMA((7,))], name=name)(buf)
    x, y, c = _place()
    return lax.dynamic_update_index_in_dim(slots, buf, 4 * x + 2 * y + c, axis=0)


N_BIG = 4


def _other_chips(x, y):
    return [(1 - x, y), (x, 1 - y), (1 - x, 1 - y)]


def _hooked_call(body, *, grid, in_specs, out_specs, out_shape, scratch_shapes, args, hook, name):
    if hook is None:
        res = pl.pallas_call(body, grid=grid, in_specs=in_specs, out_specs=out_specs, out_shape=out_shape,
                             scratch_shapes=scratch_shapes, compiler_params=_cp(), name=name)(*args)
        return res, None
    counts = (len(in_specs), len(hook["arrays"]), len(out_specs), len(hook["out_shape"]), len(scratch_shapes), len(hook["sems"]))

    def wrapped(*refs):
        groups, pos = [], 0
        for n in counts:
            groups.append(refs[pos:pos + n])
            pos += n
        ins, h_ins, outs, h_outs, scr, sems = groups
        idx = [pl.program_id(a) for a in range(len(grid))]
        first = functools.reduce(jnp.logical_and, [i == 0 for i in idx])
        last = functools.reduce(jnp.logical_and, [i == g - 1 for i, g in zip(idx, grid)])

        @pl.when(first)
        def _():
            hook["start"](h_ins, h_outs, sems)

        body(*ins, *outs, *scr)

        @pl.when(last)
        def _():
            hook["finish"](h_ins, h_outs, sems)

    res = pl.pallas_call(wrapped, grid=grid, in_specs=list(in_specs) + [ANY] * counts[1], out_specs=list(out_specs) + [ANY] * counts[3],
                         out_shape=list(out_shape) + list(hook["out_shape"]), scratch_shapes=list(scratch_shapes) + list(hook["sems"]),
                         compiler_params=_cp(), name=name + "_" + hook["name"])(*args, *hook["arrays"])
    return res[:counts[2]], res[counts[2]:]


def _run_hook(hook):
    n_in, n_out = len(hook["arrays"]), len(hook["out_shape"])

    def body(*refs):
        h_ins, h_outs, sems = refs[:n_in], refs[n_in:n_in + n_out], refs[n_in + n_out:]
        hook["start"](h_ins, h_outs, sems)
        hook["finish"](h_ins, h_outs, sems)

    return pl.pallas_call(body, in_specs=[ANY] * n_in, out_specs=[ANY] * n_out, out_shape=list(hook["out_shape"]),
                          scratch_shapes=list(hook["sems"]), name=hook["name"])(*hook["arrays"])


def _remote(src, dst, send_sem, recv_sem, peer):
    return pltpu.make_async_remote_copy(src_ref=src, dst_ref=dst, send_sem=send_sem, recv_sem=recv_sem, device_id=peer, device_id_type=MESH)


def _gather_hook(shards):
    def copies(h_ins, h_outs, sems, kind):
        ici_send, ici_recv, d2d_send, d2d_recv = sems
        x, y, c = _place()
        chip = 2 * x + y
        out = []
        for t in range(len(shards)):
            half = shards[t].shape[0] // 2
            mine_r, other_r = pl.ds(c * half, half), pl.ds((1 - c) * half, half)
            for f, (px, py) in enumerate(_other_chips(x, y)):
                k, peer_chip = 3 * t + f, 2 * px + py
                if kind in ("send", "land"):
                    slot = chip if kind == "send" else peer_chip
                    out.append(_remote(h_ins[t].at[mine_r], h_outs[t].at[slot, mine_r], ici_send.at[k], ici_recv.at[k], (px, py, c)))
                else:
                    rows = mine_r if kind == "pass" else other_r
                    out.append(_remote(h_outs[t].at[peer_chip, rows], h_outs[t].at[peer_chip, rows], d2d_send.at[k], d2d_recv.at[k],
                                       (x, y, 1 - c)))
        return out

    def start(h_ins, h_outs, sems):
        for cp in copies(h_ins, h_outs, sems, "send"):
            cp.start()

    def finish(h_ins, h_outs, sems):
        passed = copies(h_ins, h_outs, sems, "pass")
        for landed, forward in zip(copies(h_ins, h_outs, sems, "land"), passed):
            landed.wait_recv()
            forward.start()
        for cp in copies(h_ins, h_outs, sems, "get"):
            cp.wait_recv()
        for cp in copies(h_ins, h_outs, sems, "send") + passed:
            cp.wait_send()

    return dict(name="gather", arrays=list(shards), out_shape=[S_((N_CHIPS,) + s.shape, s.dtype) for s in shards],
                sems=[pltpu.SemaphoreType.DMA((3 * len(shards),)) for _ in range(4)], start=start, finish=finish)


def _to_sibling_hook(parts, half_rows=False):
    def copies(h_ins, h_outs, sems):
        x, y, c = _place()
        out = []
        for t, p in enumerate(parts):
            src = h_ins[t].at[:, pl.ds((1 - c) * (p.shape[1] // 2), p.shape[1] // 2)] if half_rows else h_ins[t]
            out.append(_remote(src, h_outs[t], sems[0].at[t], sems[1].at[t], (x, y, 1 - c)))
        return out

    def start(h_ins, h_outs, sems):
        for cp in copies(h_ins, h_outs, sems):
            cp.start()

    def finish(h_ins, h_outs, sems):
        cps = copies(h_ins, h_outs, sems)
        for cp in cps:
            cp.wait_recv()
        for cp in cps:
            cp.wait_send()

    shapes = [(p.shape[0], p.shape[1] // 2, p.shape[2]) if half_rows else p.shape for p in parts]
    return dict(name="to_sibling", arrays=list(parts), out_shape=[S_(s, p.dtype) for s, p in zip(shapes, parts)],
                sems=[pltpu.SemaphoreType.DMA((len(parts),)), pltpu.SemaphoreType.DMA((len(parts),))], start=start, finish=finish)


def _to_chips_hook(parts):
    def copies(h_ins, h_outs, sems):
        x, y, c = _place()
        return [_remote(h_ins[t].at[2 * px + py], h_outs[t].at[f], sems[0].at[3 * t + f], sems[1].at[3 * t + f], (px, py, c))
                for t in range(len(parts)) for f, (px, py) in enumerate(_other_chips(x, y))]

    def start(h_ins, h_outs, sems):
        for cp in copies(h_ins, h_outs, sems):
            cp.start()

    def finish(h_ins, h_outs, sems):
        cps = copies(h_ins, h_outs, sems)
        for cp in cps:
            cp.wait_recv()
        for cp in cps:
            cp.wait_send()

    return dict(name="to_chips", arrays=list(parts), out_shape=[S_((3,) + p.shape[1:], p.dtype) for p in parts],
                sems=[pltpu.SemaphoreType.DMA((3 * len(parts),)), pltpu.SemaphoreType.DMA((3 * len(parts),))], start=start, finish=finish)


def _add_pair(a, b, core, name):
    n, half, c = b.shape

    def body(core_ref, a_ref, b_ref, o_ref):
        o_ref[...] = (a_ref[...].astype(F32) + b_ref[...].astype(F32)).astype(BF16)

    spec = pltpu.PrefetchScalarGridSpec(
        num_scalar_prefetch=1, grid=(n,),
        in_specs=[pl.BlockSpec((None, half, c), lambda i, core_ref: (i, core_ref[0], 0)),
                  pl.BlockSpec((None, half, c), lambda i, core_ref: (i, 0, 0))],
        out_specs=pl.BlockSpec((None, half, c), lambda i, core_ref: (i, 0, 0)))
    return pl.pallas_call(body, grid_spec=spec, out_shape=S_(b.shape, BF16), name=name)(core.reshape(1).astype(jnp.int32), a, b)


def _add_four(own, got, rows, name):
    n, r, c = own.shape

    def body(a_ref, g_ref, o_ref):
        o_ref[...] = ((a_ref[...].astype(F32) + g_ref[0].astype(F32)) + g_ref[1].astype(F32)) + g_ref[2].astype(F32)

    blk = pl.BlockSpec((None, rows, c), lambda i, j: (i, j, 0))
    return pl.pallas_call(body, grid=(n, r // rows), in_specs=[blk, pl.BlockSpec((3, None, rows, c), lambda i, j: (0, i, j, 0))],
                          out_specs=blk, out_shape=S_(own.shape, F32), name=name)(own, got)


def _sum_slots(slots):
    n, rows = slots.shape[:2]

    def body(s_ref, o_ref):
        tot = s_ref[0]
        for k in range(1, n):
            tot = tot + s_ref[k]
        o_ref[...] = tot

    return pl.pallas_call(body, out_shape=S_((rows, 128), F32), name="sum_slots")(slots)


def _add_two(a, b):
    def body(a_ref, b_ref, o_ref):
        o_ref[...] = a_ref[...] + b_ref[...]

    return pl.pallas_call(body, out_shape=S_(a.shape, a.dtype), name="add_two")(a, b)


def _allreduce_small(buf, chip):
    (theirs,) = _run_hook(dict(_to_sibling_hook([buf]), name="small_to_sibling"))
    pair = _add_two(buf, theirs)
    (slots,) = _run_hook(dict(_gather_hook([pair]), name="small_gather"))
    return _sum_slots(lax.dynamic_update_index_in_dim(slots, pair, chip, axis=0))


SMALL = ("mix_norm_gain", "ssd_conv_w", "ssd_conv_b", "ssd_dt_bias", "ssd_a_log", "ssd_d", "ssd_out_gain", "gmlp_v_gain",
         "gmlp_w_s", "gmlp_b_s", "attn_q_gain", "attn_k_gain", "rel_bias_table", "ffn_norm_gain", "ffn_conv_w", "ffn_conv_b")
BIG = ("w_in", "w_out", "ffn_w_up", "ffn_w_down")
WEIGHTS = ("mix_norm_gain", "w_in", "ssd_conv_w", "ssd_conv_b", "ssd_dt_bias", "ssd_a_log", "ssd_d", "ssd_out_gain", "gmlp_v_gain",
           "gmlp_w_s", "gmlp_b_s", "attn_q_gain", "attn_k_gain", "rel_bias_table", "w_out", "ffn_norm_gain", "ffn_w_up",
           "ffn_conv_w", "ffn_conv_b", "ffn_w_down")
ADAM_ROWS = {"w_in": 512, "w_out": 128, "ffn_w_up": 256, "ffn_w_down": 352}


PACK_ROWS = 64


def _packed_rows(shape):
    return -(-int(np.prod(shape)) // 1024) * 8


def _pack(arrays):
    parts = []
    for a in arrays:
        rows = _packed_rows(a.shape)
        flat = a.reshape(-1).astype(F32)
        parts.append(jnp.pad(flat, (0, rows * 128 - flat.shape[0])).reshape(rows, 128))
    total = sum(p.shape[0] for p in parts)
    tail = -total % PACK_ROWS
    if tail:
        parts.append(jnp.zeros((tail, 128), F32))
    return jnp.concatenate(parts, axis=0)


def _unpack(buf, shapes):
    out, row = [], 0
    for s in shapes:
        rows, n = _packed_rows(s), int(np.prod(s))
        out.append(buf[row:row + rows].reshape(-1)[:n].reshape(s))
        row += rows
    return out


def _perm_cols(w):
    pad = jnp.zeros(w.shape[:-1] + (NP - IN_WIDTH,), w.dtype)
    return jnp.concatenate([w[..., :1536], w[..., 1552:], w[..., 1536:1552], pad], axis=-1)


def _unperm_cols(w):
    return jnp.concatenate([w[..., :1536], w[..., C_DT:C_DT + 16], w[..., 1536:C_DT]], axis=-1)


def _layer_params(l, p, conv5_w, conv3_w, bias):
    def make(mix_g, conv5, conv5_b, dt_bias, a_log, d_skip, out_gain, v_gain, w_s, b_s, q_gain, k_gain, ffn_g, conv3, conv3_b):
        lanes = lambda a: jnp.pad(a.reshape(1, 16), ((0, 0), (0, 112)))
        cw3 = jnp.pad(jnp.transpose(conv3.reshape(3, 2, FFN_DIM), (1, 0, 2)), ((0, 0), (0, 5), (0, 0)))
        return dict(mix_g=mix_g.reshape(1, D_MODEL), cw5=jnp.pad(conv5, ((0, 3), (0, 0))), cb5=conv5_b.reshape(1, SSD_XBC),
                    dtb=lanes(dt_bias), alog=lanes(a_log), d_exp=jnp.repeat(d_skip, HEAD).reshape(1, SSD_INNER),
                    out_gain=out_gain.reshape(1, SSD_INNER), v_gain=v_gain.reshape(1, GMLP_W), w_s=w_s,
                    b_exp=jnp.repeat(b_s.T, HEAD, axis=1), q_gain=jnp.tile(q_gain, 2).reshape(1, PAIR_W),
                    k_gain=jnp.tile(k_gain, 2).reshape(1, PAIR_W), ffn_g=ffn_g.reshape(1, D_MODEL), cw3=cw3,
                    cb3=conv3_b.reshape(2, 1, FFN_DIM))

    args = (p["mix_norm_gain"][l], conv5_w[l], p["ssd_conv_b"][l], p["ssd_dt_bias"][l], p["ssd_a_log"][l], p["ssd_d"][l],
            p["ssd_out_gain"][l], p["gmlp_v_gain"][l], p["gmlp_w_s"][l], p["gmlp_b_s"][l], p["attn_q_gain"][l], p["attn_k_gain"][l],
            p["ffn_norm_gain"][l], conv3_w[l], p["ffn_conv_b"][l])
    return jax.vjp(make, *args)


def _forward_layer(x, h, lp, w, bias, next_gain=None, hooks=None, resolve=None):
    hooks = hooks or {}
    proj = _mm_nn(h, w["w_in"], tm=1024, tn=1024, tk=1024, out_dtype=F32, name="mm_proj")
    xc = _ssd_pre_fwd(proj, lp["cw5"], lp["cb5"])
    (y_ssd, ssd_sum, ssd_states), got_self = _ssd_scan_fwd(xc, proj, lp["dtb"], lp["alog"], lp["d_exp"], lp["out_gain"],
                                                           hook=hooks.get("ssd"))
    if resolve is not None and got_self is not None:
        w = dict(w, **resolve(got_self))
    y_gmlp = _gmlp_fwd(proj, lp["v_gain"], lp["w_s"], lp["b_exp"])
    (y_attn, attn_o, attn_l), got_attn = _attn_fwd(proj, lp["q_gain"], lp["k_gain"], bias, hook=hooks.get("attn"))
    y = jnp.concatenate([y_ssd, y_gmlp, y_attn], axis=1)
    x2, hn = _mm_nn(y, w["w_out"], tm=1024, tn=1024, tk=1024, out_dtype=F32, res=x, norm_gain=lp["ffn_g"], name="mm_out")
    up3 = _mm_up(hn, w["ffn_w_up"])
    act, got_gate = _convgate_fwd(up3, lp["cw3"], lp["cb3"], hook=hooks.get("gate"))
    if next_gain is None:
        x3, h_next = _mm_nn(act, w["ffn_w_down"], tm=1024, tn=1024, tk=HALF_TILE, out_dtype=F32, res=x2, name="mm_down"), None
    else:
        x3, h_next = _mm_nn(act, w["ffn_w_down"], tm=1024, tn=1024, tk=HALF_TILE, out_dtype=F32, res=x2, norm_gain=next_gain,
                            name="mm_down_norm")
    saved = dict(x=x, h=h, proj=proj, xc=xc, y=y, x2=x2, hn=hn, up3=up3, act=act, attn_o=attn_o, attn_l=attn_l,
                 ssd_sum=ssd_sum, ssd_states=ssd_states)
    return x3, h_next, saved, w, dict(ssd=got_self, attn=got_attn, gate=got_gate)


def _backward_layer(dx3, dx3_lo, sv, lp, w, bias, pending=None, reducer=None):
    d_act = _mm_nt(dx3_lo, w["ffn_w_down"], tm=1024, tn=HALF_TILE, tk=1024, out_dtype=F32, name="mm_dact")
    dw_down = _mm_tn(sv["act"], dx3_lo, tm=HALF_TILE, tn=1024, tk=1024, out_dtype=BF16, name="mm_dwdown")
    (dup3, dcw3), from_sibling = _convgate_bwd(sv["up3"], lp["cw3"], lp["cb3"], d_act, hook=pending.sibling_hook() if pending else None)
    if pending:
        pending.add_sibling(from_sibling)
    dx2, d_ffn_g, dx2_lo = _mm_dhn(dup3, w["ffn_w_up"], sv["x2"], lp["ffn_g"], dx3)
    dw_up = _mm_dwup(sv["hn"], dup3)
    d_y = _mm_nt(dx2_lo, w["w_out"], tm=1024, tn=1024, tk=1024, out_dtype=F32, name="mm_dy")
    dw_out = _mm_tn(sv["y"], dx2_lo, tm=1024, tn=1024, tk=1024, out_dtype=BF16, name="mm_dwout")
    early = reducer(("w_out", "ffn_w_up", "ffn_w_down"), (dw_out, dw_up, dw_down)) if reducer else None
    proj, xc = sv["proj"], sv["xc"]
    (dxs, dbc, dcc, dz, ddt2, ddtb2, dal2, d_dexp, d_outg), from_chips = _ssd_scan_bwd(
        xc, proj, lp["dtb"], lp["alog"], lp["d_exp"], lp["out_gain"], d_y, sv["ssd_sum"], sv["ssd_states"],
        hook=pending.chips_hook() if pending else None)
    if pending:
        pending.add_chips(from_chips)
    (d_xbc, dcw5), from_sibling = _ssd_pre_bwd(proj, lp["cw5"], lp["cb5"], jnp.concatenate([dxs, dbc, dcc], axis=1),
                                               hook=early.sibling_hook() if early else None)
    if early:
        early.add_sibling(from_sibling)
    d_gu, d_gv, d_vg, d_ws, d_bexp = _gmlp_bwd(proj, lp["v_gain"], lp["w_s"], lp["b_exp"], d_y)
    (d_q, d_k, d_v, d_qg2, d_kg2, d_bias), from_chips = _attn_bwd(proj, lp["q_gain"], lp["k_gain"], bias, d_y, sv["attn_o"], sv["attn_l"],
                                                                  hook=early.chips_hook() if early else None)
    if early:
        early.add_chips(from_chips)
    d_dt = (ddt2[0] + ddt2[1]).astype(BF16)
    d_proj = jnp.concatenate([d_xbc, dz, d_gu, d_gv, d_q, d_k, d_v, d_dt, jnp.zeros((T, NP - C_DT - 128), BF16)], axis=1)
    dx, d_mix_g, dx_lo = _mm_nt(d_proj, w["w_in"], tm=1024, tn=1024, tk=1024, out_dtype=F32, res=dx2, norm_bwd=(sv["x"], lp["mix_g"]), name="mm_dh")
    dw_in = _mm_tn(sv["h"], d_proj, tm=1024, tn=1024, tk=1024, out_dtype=BF16, name="mm_dwin")
    late = None
    if reducer:
        late = reducer(("w_in",), (dw_in,))
        late.run_alone()
    d_lp = dict(mix_g=d_mix_g, cw5=dcw5[:8] * (jnp.arange(8) < 5)[:, None].astype(F32), cb5=dcw5[5:6],
                dtb=(ddtb2[0, :1] + ddtb2[1, :1]), alog=(dal2[0, :1] + dal2[1, :1]), d_exp=d_dexp, out_gain=d_outg,
                v_gain=d_vg, w_s=d_ws, b_exp=d_bexp, q_gain=d_qg2[0, :1] + d_qg2[1, :1], k_gain=d_kg2[0, :1] + d_kg2[1, :1],
                ffn_g=d_ffn_g, cw3=dcw3 * (jnp.arange(8) < 3)[None, :, None].astype(F32), cb3=dcw3[:, 3:4])
    return (dx, dx_lo), dict(w_in=dw_in, w_out=dw_out, ffn_w_up=dw_up, ffn_w_down=dw_down), d_lp, d_bias, (early, late)


def _to_shard_major(name, dw):
    if name == "ffn_w_up":
        return dw
    if name == "w_in":
        r, c = dw.shape[0], IN_WIDTH
        return jnp.transpose(_unperm_cols(dw).reshape(r, N_CHIPS, c // N_CHIPS), (1, 0, 2))
    r, c = dw.shape
    return dw.reshape(N_CHIPS, r // N_CHIPS, c)


def _whole_weight(name, gathered, own, chip):
    if name == "w_in":
        return _perm_cols(jnp.concatenate([jnp.where(chip == k, own, gathered[k]) for k in range(N_CHIPS)], axis=1))
    w = lax.dynamic_update_index_in_dim(gathered, own, chip, axis=0)
    return w if name == "ffn_w_up" else w.reshape(N_CHIPS * own.shape[0], own.shape[1])


class _LayerReduce:
    def __init__(self, names, dws, chip, core):
        self.names, self.chip, self.core = names, chip, core
        self.parts = [_to_shard_major(n, dw) for n, dw in zip(names, dws)]

    def sibling_hook(self):
        return _to_sibling_hook(self.parts, half_rows=True)

    def add_sibling(self, got):
        self.sums = [_add_pair(a, b, self.core, "add_pair_" + n) for n, a, b in zip(self.names, self.parts, got)]

    def chips_hook(self):
        return _to_chips_hook(self.sums)

    def add_chips(self, got):
        self.half = {}
        for n, s2, g3 in zip(self.names, self.sums, got):
            own = lax.dynamic_index_in_dim(s2, self.chip, axis=0, keepdims=True)
            self.half[n] = _add_four(own, g3[:, None], own.shape[1], "add_four_" + n)[0]

    def run_alone(self):
        self.add_sibling(_run_hook(self.sibling_hook()))
        self.add_chips(_run_hook(self.chips_hook()))


LAYER_SMALL = ("mix_norm_gain", "ssd_conv_w", "ssd_conv_b", "ssd_dt_bias", "ssd_a_log", "ssd_d", "ssd_out_gain", "gmlp_v_gain",
               "gmlp_w_s", "gmlp_b_s", "attn_q_gain", "attn_k_gain", "ffn_norm_gain", "ffn_conv_w", "ffn_conv_b")


def _local_grads(x, loss_target, p, conv5_w, conv3_w, layer_w, exchange=None):
    bias, bias_vjp = jax.vjp(_attn_bias, p["rel_bias_table"])
    xt = x.reshape(T, D_MODEL)
    layer_w = list(layer_w)
    saved, lps, lp_vjps = [], [], []
    if exchange is not None:
        chip, core, own = exchange
        whole = lambda names, layer, gathered: {n: _whole_weight(n, g, own[layer][BIG.index(n)], chip) for n, g in zip(names, gathered)}
    for l in range(DEPTH):
        lp, lp_vjp = _layer_params(l, p, conv5_w, conv3_w, bias)
        lps.append(lp)
        lp_vjps.append(lp_vjp)
    h = _rmsnorm_fwd(xt, lps[0]["mix_g"], "rmsnorm_fwd")
    for l in range(DEPTH):
        lp = lps[l]
        plan = {}
        if exchange is not None and l == 0:
            plan = {"ssd": (BIG[1:], 0), "attn": (BIG[2:], 1), "gate": (BIG[:2], 1)}
        elif exchange is not None and l + 1 < DEPTH:
            plan = {"ssd": (BIG[:2], l + 1), "attn": (BIG[2:3], l + 1), "gate": (BIG[3:], l + 1)}
        hooks = {tag: _gather_hook([own[layer][BIG.index(n)] for n in names]) for tag, (names, layer) in plan.items()}
        xt, h, sv, layer_w[l], got = _forward_layer(xt, h, lp, layer_w[l], bias, lps[l + 1]["mix_g"] if l + 1 < DEPTH else None, hooks,
                                                    resolve=(lambda g: whole(BIG[1:], 0, g)) if l == 0 else None)
        ahead = {}
        for tag, (names, layer) in plan.items():
            if layer == l + 1:
                ahead.update(whole(names, layer, got[tag]))
        if ahead:
            layer_w.append(ahead)
        saved.append(sv)
    dxt, loss_parts, dxt_lo = _loss_head(xt, loss_target.reshape(T, D_MODEL))
    loss_local = jnp.sum(loss_parts[::8, 0])

    big_grads = [None] * DEPTH
    small_layers = [None] * DEPTH
    d_bias_tot = jnp.zeros_like(bias)
    pending = None
    for l in reversed(range(DEPTH)):
        last = exchange is not None and l == 0
        (dxt, dxt_lo), big_grads[l], d_lp, d_bias, own_reduce = _backward_layer(
            dxt, dxt_lo, saved[l], lps[l], layer_w[l], bias, pending=pending,
            reducer=(lambda names, dws: _LayerReduce(names, dws, chip, core)) if last else None)
        if pending is not None:
            big_grads[l + 1] = pending.half
        if last:
            big_grads[l] = dict(own_reduce[0].half, **own_reduce[1].half)
        elif exchange is not None:
            pending = _LayerReduce(BIG, [big_grads[l][n] for n in BIG], chip, core)
        small_layers[l] = lp_vjps[l](d_lp)
        d_bias_tot = d_bias_tot + d_bias
    (d_rel_table,) = bias_vjp(d_bias_tot)
    local_small = {n: jnp.stack([small_layers[l][i] for l in range(DEPTH)]) for i, n in enumerate(LAYER_SMALL)}
    local_small["rel_bias_table"] = d_rel_table
    return dxt, loss_local, big_grads, local_small


def kernel(x, mix_norm_gain, w_in, ssd_conv_w, ssd_conv_b, ssd_dt_bias, ssd_a_log, ssd_d, ssd_out_gain, gmlp_v_gain, gmlp_w_s, gmlp_b_s, attn_q_gain, attn_k_gain, rel_bias_table, w_out, ffn_norm_gain, ffn_w_up, ffn_conv_w, ffn_conv_b, ffn_w_down, loss_target, m_mix_norm_gain, m_w_in, m_ssd_conv_w, m_ssd_conv_b, m_ssd_dt_bias, m_ssd_a_log, m_ssd_d, m_ssd_out_gain, m_gmlp_v_gain, m_gmlp_w_s, m_gmlp_b_s, m_attn_q_gain, m_attn_k_gain, m_rel_bias_table, m_w_out, m_ffn_norm_gain, m_ffn_w_up, m_ffn_conv_w, m_ffn_conv_b, m_ffn_w_down, v_mix_norm_gain, v_w_in, v_ssd_conv_w, v_ssd_conv_b, v_ssd_dt_bias, v_ssd_a_log, v_ssd_d, v_ssd_out_gain, v_gmlp_v_gain, v_gmlp_w_s, v_gmlp_b_s, v_attn_q_gain, v_attn_k_gain, v_rel_bias_table, v_w_out, v_ffn_norm_gain, v_ffn_w_up, v_ffn_conv_w, v_ffn_conv_b, v_ffn_w_down):
    env = dict(locals())
    p = {n: env[n] for n in WEIGHTS}
    chip = 2 * lax.axis_index("x") + lax.axis_index("y")
    core = lax.axis_index("c")

    conv_slots = _allgather8(_pack([ssd_conv_w, ffn_conv_w]), "allgather_conv")
    conv_parts = [_unpack(conv_slots[2 * k], [ssd_conv_w.shape, ffn_conv_w.shape]) for k in range(N_CHIPS)]
    conv5_w = jnp.concatenate([cp[0] for cp in conv_parts], axis=-1)
    conv3_w = jnp.concatenate([cp[1] for cp in conv_parts], axis=-1)
    own = [[p[n][l].astype(BF16) for n in BIG] for l in range(DEPTH)]
    (first,) = _run_hook(_gather_hook(own[0][:1]))
    layer_w = [{"w_in": _whole_weight("w_in", first, own[0][0], chip)}]

    dxt, loss_local, reduced, local_small = _local_grads(x, loss_target, p, conv5_w, conv3_w, layer_w, exchange=(chip, core, own))

    small_shapes = [local_small[n].shape for n in SMALL] + [(1,)]
    summed = _unpack(_allreduce_small(_pack([local_small[n] for n in SMALL] + [loss_local.reshape(1)]), chip), small_shapes)
    grads = dict(zip(SMALL, summed[:-1]))
    loss = summed[-1][0]
    grads["ssd_conv_w"] = lax.dynamic_slice_in_dim(grads["ssd_conv_w"], chip * 256, 256, axis=2)
    grads["ffn_conv_w"] = lax.dynamic_slice_in_dim(grads["ffn_conv_w"], chip * (2 * FFN_DIM // N_CHIPS), 2 * FFN_DIM // N_CHIPS, axis=2)

    halves = [jnp.stack([reduced[l][n] for l in range(DEPTH)]) for n in BIG]
    others = _run_hook(dict(_to_sibling_hook(halves), name="swap_halves"))

    delta, new_m, new_v = {}, {}, {}
    for n, mine, other in zip(BIG, halves, others):
        grads[n], delta[n], new_m[n], new_v[n] = _adamw(p[n], mine, other, env["m_" + n], env["v_" + n], core, ADAM_ROWS[n], "adamw_" + n)
    d, nm, nv = _adamw_many([p[n] for n in SMALL], [grads[n] for n in SMALL], [env["m_" + n] for n in SMALL], [env["v_" + n] for n in SMALL])
    for n, a, b, c in zip(SMALL, d, nm, nv):
        delta[n], new_m[n], new_v[n] = a, b, c

    return (loss, dxt.reshape(B_LOC, SEQ, D_MODEL), *[grads[n] for n in WEIGHTS], *[delta[n] for n in WEIGHTS],
            *[new_m[n] for n in WEIGHTS], *[new_v[n] for n in WEIGHTS])
```
